```python
import jax, jax.numpy as jnp
from jax import lax
import numpy as np

D_MODEL = 1024
BATCH = 16
SEQ = 4096
DEPTH = 2

D_CONV = D_MODEL
CONV_WIDTH = 31
D_SGU = D_MODEL
SGU_GROUPS = 8
SGU_GROUP_DIM = D_SGU // SGU_GROUPS
CHUNK = 128
D_FF = ((8 * D_MODEL // 3 + 255) // 256) * 256
D_IN = 2 * D_CONV + 2 * D_SGU + 2 * D_MODEL
EPS = 1e-6

kernel_name = "hybrid_conformer_conv_gmlp_encoder"


def rmsnorm(x, g):
    xf = x.astype(jnp.float32)
    y = xf * lax.rsqrt(jnp.mean(xf * xf, axis=-1, keepdims=True) + EPS)
    return (y * g.astype(jnp.float32)).astype(x.dtype)


def layernorm(x, g, b):
    xf = x.astype(jnp.float32)
    mu = jnp.mean(xf, axis=-1, keepdims=True)
    var = jnp.mean(jnp.square(xf - mu), axis=-1, keepdims=True)
    y = (xf - mu) * lax.rsqrt(var + EPS)
    return (y * g.astype(jnp.float32) + b.astype(jnp.float32)).astype(x.dtype)


def depthwise_conv(x, w, b):
    pad = (CONV_WIDTH - 1) // 2
    y = lax.conv_general_dilated(
        x, w.astype(x.dtype), window_strides=(1,), padding=[(pad, pad)],
        dimension_numbers=("NWC", "WIO", "NWC"), feature_group_count=x.shape[-1])
    return y + b


def conformer_conv_branch(val, gate, conv_w, conv_b, ln_g, ln_b, w_out):
    c = val * jax.nn.sigmoid(gate)
    c = depthwise_conv(c, conv_w, conv_b)
    c = jax.nn.silu(layernorm(c, ln_g, ln_b))
    return c @ w_out


def spatial_gating_branch(u, v, ln_g, ln_b, w_s, b_s, w_out):
    bsz, seq, _ = v.shape
    n_chunks = seq // CHUNK
    vn = layernorm(v, ln_g, ln_b)
    vc = vn.reshape(bsz, n_chunks, CHUNK, SGU_GROUPS, SGU_GROUP_DIM)
    mixed = jnp.einsum("bcpgd,gqp->bcqgd", vc, w_s.astype(vc.dtype))
    mixed = mixed + jnp.transpose(b_s)[None, None, :, :, None]
    gated = u * mixed.reshape(bsz, seq, D_SGU)
    return gated @ w_out


def _fwd_setup_inputs(seed: int = 0) -> dict:
    key = jax.random.key(seed)
    ks = jax.random.split(key, 24)
    f32 = jnp.float32

    def nrm(k, shape, scale):
        return jax.random.normal(k, shape, f32) * scale

    L = DEPTH
    return {
        "x": jax.random.normal(ks[0], (BATCH, SEQ, D_MODEL), f32),
        "norm_mix": 1.0 + nrm(ks[1], (L, D_MODEL), 0.02),
        "w_in": nrm(ks[2], (L, D_MODEL, D_IN), D_MODEL ** -0.5),
        "gate_bias": nrm(ks[3], (L, 2 * D_MODEL), 0.02),
        "conv_w": nrm(ks[4], (L, CONV_WIDTH, 1, D_CONV), CONV_WIDTH ** -0.5),
        "conv_b": nrm(ks[5], (L, D_CONV), 0.02),
        "conv_ln_g": 1.0 + nrm(ks[6], (L, D_CONV), 0.02),
        "conv_ln_b": nrm(ks[7], (L, D_CONV), 0.02),
        "w_conv_out": nrm(ks[8], (L, D_CONV, D_MODEL), D_CONV ** -0.5),
        "sgu_ln_g": 1.0 + nrm(ks[9], (L, D_SGU), 0.02),
        "sgu_ln_b": nrm(ks[10], (L, D_SGU), 0.02),
        "w_spatial": nrm(ks[11], (L, SGU_GROUPS, CHUNK, CHUNK), CHUNK ** -0.5),
        "b_spatial": 1.0 + nrm(ks[12], (L, SGU_GROUPS, CHUNK), 0.02),
        "w_sgu_out": nrm(ks[13], (L, D_SGU, D_MODEL), D_SGU ** -0.5),
        "w_o": nrm(ks[14], (L, D_MODEL, D_MODEL), D_MODEL ** -0.5),
        "norm_ffn": 1.0 + nrm(ks[15], (L, D_MODEL), 0.02),
        "w_ffn_gate": nrm(ks[16], (L, D_MODEL, D_FF), D_MODEL ** -0.5),
        "w_ffn_up": nrm(ks[17], (L, D_MODEL, D_FF), D_MODEL ** -0.5),
        "w_ffn_down": nrm(ks[18], (L, D_FF, D_MODEL), D_FF ** -0.5),
        "norm_final": 1.0 + nrm(ks[19], (D_MODEL,), 0.02),
    }


def _fwd_reference(x, norm_mix, w_in, gate_bias, conv_w, conv_b, conv_ln_g, conv_ln_b,
              w_conv_out, sgu_ln_g, sgu_ln_b, w_spatial, b_spatial, w_sgu_out, w_o,
              norm_ffn, w_ffn_gate, w_ffn_up, w_ffn_down, norm_final):
    split_points = [D_CONV, 2 * D_CONV, 2 * D_CONV + D_SGU, 2 * D_CONV + 2 * D_SGU,
                    2 * D_CONV + 2 * D_SGU + D_MODEL]
    for l in range(DEPTH):
        h = rmsnorm(x, norm_mix[l])
        proj = h @ w_in[l]
        a_val, a_gate, u, v, g_a, g_b = jnp.split(proj, split_points, axis=-1)
        y_a = conformer_conv_branch(a_val, a_gate, conv_w[l], conv_b[l],
                                    conv_ln_g[l], conv_ln_b[l], w_conv_out[l])
        y_b = spatial_gating_branch(u, v, sgu_ln_g[l], sgu_ln_b[l],
                                    w_spatial[l], b_spatial[l], w_sgu_out[l])
        gb_a, gb_b = jnp.split(gate_bias[l], 2)
        merged = jax.nn.sigmoid(g_a + gb_a) * y_a + jax.nn.sigmoid(g_b + gb_b) * y_b
        x = x + merged @ w_o[l]
        h2 = rmsnorm(x, norm_ffn[l])
        x = x + (jax.nn.silu(h2 @ w_ffn_gate[l]) * (h2 @ w_ffn_up[l])) @ w_ffn_down[l]
    return rmsnorm(x, norm_final)


import jax as _jax
import jax.numpy as _jnp

TWIN_FORMAT = 'train_step'
FWD_PARAMS = ['x', 'norm_mix', 'w_in', 'gate_bias', 'conv_w', 'conv_b', 'conv_ln_g', 'conv_ln_b', 'w_conv_out', 'sgu_ln_g', 'sgu_ln_b', 'w_spatial', 'b_spatial', 'w_sgu_out', 'w_o', 'norm_ffn', 'w_ffn_gate', 'w_ffn_up', 'w_ffn_down', 'norm_final']
TWIN_WEIGHTS = ['norm_mix', 'w_in', 'gate_bias', 'conv_w', 'conv_b', 'conv_ln_g', 'conv_ln_b', 'w_conv_out', 'sgu_ln_g', 'sgu_ln_b', 'w_spatial', 'b_spatial', 'w_sgu_out', 'w_o', 'norm_ffn', 'w_ffn_gate', 'w_ffn_up', 'w_ffn_down', 'norm_final']
TWIN_DIFF_INPUT = 'x'
TWIN_INPUTS = ['x', 'norm_mix', 'w_in', 'gate_bias', 'conv_w', 'conv_b', 'conv_ln_g', 'conv_ln_b', 'w_conv_out', 'sgu_ln_g', 'sgu_ln_b', 'w_spatial', 'b_spatial', 'w_sgu_out', 'w_o', 'norm_ffn', 'w_ffn_gate', 'w_ffn_up', 'w_ffn_down', 'norm_final', 'loss_target', 'm_norm_mix', 'm_w_in', 'm_gate_bias', 'm_conv_w', 'm_conv_b', 'm_conv_ln_g', 'm_conv_ln_b', 'm_w_conv_out', 'm_sgu_ln_g', 'm_sgu_ln_b', 'm_w_spatial', 'm_b_spatial', 'm_w_sgu_out', 'm_w_o', 'm_norm_ffn', 'm_w_ffn_gate', 'm_w_ffn_up', 'm_w_ffn_down', 'm_norm_final', 'v_norm_mix', 'v_w_in', 'v_gate_bias', 'v_conv_w', 'v_conv_b', 'v_conv_ln_g', 'v_conv_ln_b', 'v_w_conv_out', 'v_sgu_ln_g', 'v_sgu_ln_b', 'v_w_spatial', 'v_b_spatial', 'v_w_sgu_out', 'v_w_o', 'v_norm_ffn', 'v_w_ffn_gate', 'v_w_ffn_up', 'v_w_ffn_down', 'v_norm_final']
TWIN_OUTPUTS = ['loss', 'grad_x', 'grad_norm_mix', 'grad_w_in', 'grad_gate_bias', 'grad_conv_w', 'grad_conv_b', 'grad_conv_ln_g', 'grad_conv_ln_b', 'grad_w_conv_out', 'grad_sgu_ln_g', 'grad_sgu_ln_b', 'grad_w_spatial', 'grad_b_spatial', 'grad_w_sgu_out', 'grad_w_o', 'grad_norm_ffn', 'grad_w_ffn_gate', 'grad_w_ffn_up', 'grad_w_ffn_down', 'grad_norm_final', 'delta_norm_mix', 'delta_w_in', 'delta_gate_bias', 'delta_conv_w', 'delta_conv_b', 'delta_conv_ln_g', 'delta_conv_ln_b', 'delta_w_conv_out', 'delta_sgu_ln_g', 'delta_sgu_ln_b', 'delta_w_spatial', 'delta_b_spatial', 'delta_w_sgu_out', 'delta_w_o', 'delta_norm_ffn', 'delta_w_ffn_gate', 'delta_w_ffn_up', 'delta_w_ffn_down', 'delta_norm_final', 'new_m_norm_mix', 'new_m_w_in', 'new_m_gate_bias', 'new_m_conv_w', 'new_m_conv_b', 'new_m_conv_ln_g', 'new_m_conv_ln_b', 'new_m_w_conv_out', 'new_m_sgu_ln_g', 'new_m_sgu_ln_b', 'new_m_w_spatial', 'new_m_b_spatial', 'new_m_w_sgu_out', 'new_m_w_o', 'new_m_norm_ffn', 'new_m_w_ffn_gate', 'new_m_w_ffn_up', 'new_m_w_ffn_down', 'new_m_norm_final', 'new_v_norm_mix', 'new_v_w_in', 'new_v_gate_bias', 'new_v_conv_w', 'new_v_conv_b', 'new_v_conv_ln_g', 'new_v_conv_ln_b', 'new_v_w_conv_out', 'new_v_sgu_ln_g', 'new_v_sgu_ln_b', 'new_v_w_spatial', 'new_v_b_spatial', 'new_v_w_sgu_out', 'new_v_w_o', 'new_v_norm_ffn', 'new_v_w_ffn_gate', 'new_v_w_ffn_up', 'new_v_w_ffn_down', 'new_v_norm_final']
TWIN_LEAF_KINDS = {'loss': 'loss', 'grad_x': 'grad_x', 'grad_norm_mix': 'grad_w', 'grad_w_in': 'grad_w', 'grad_gate_bias': 'grad_w', 'grad_conv_w': 'grad_w', 'grad_conv_b': 'grad_w', 'grad_conv_ln_g': 'grad_w', 'grad_conv_ln_b': 'grad_w', 'grad_w_conv_out': 'grad_w', 'grad_sgu_ln_g': 'grad_w', 'grad_sgu_ln_b': 'grad_w', 'grad_w_spatial': 'grad_w', 'grad_b_spatial': 'grad_w', 'grad_w_sgu_out': 'grad_w', 'grad_w_o': 'grad_w', 'grad_norm_ffn': 'grad_w', 'grad_w_ffn_gate': 'grad_w', 'grad_w_ffn_up': 'grad_w', 'grad_w_ffn_down': 'grad_w', 'grad_norm_final': 'grad_w', 'delta_norm_mix': 'delta_w', 'delta_w_in': 'delta_w', 'delta_gate_bias': 'delta_w', 'delta_conv_w': 'delta_w', 'delta_conv_b': 'delta_w', 'delta_conv_ln_g': 'delta_w', 'delta_conv_ln_b': 'delta_w', 'delta_w_conv_out': 'delta_w', 'delta_sgu_ln_g': 'delta_w', 'delta_sgu_ln_b': 'delta_w', 'delta_w_spatial': 'delta_w', 'delta_b_spatial': 'delta_w', 'delta_w_sgu_out': 'delta_w', 'delta_w_o': 'delta_w', 'delta_norm_ffn': 'delta_w', 'delta_w_ffn_gate': 'delta_w', 'delta_w_ffn_up': 'delta_w', 'delta_w_ffn_down': 'delta_w', 'delta_norm_final': 'delta_w', 'new_m_norm_mix': 'new_m', 'new_m_w_in': 'new_m', 'new_m_gate_bias': 'new_m', 'new_m_conv_w': 'new_m', 'new_m_conv_b': 'new_m', 'new_m_conv_ln_g': 'new_m', 'new_m_conv_ln_b': 'new_m', 'new_m_w_conv_out': 'new_m', 'new_m_sgu_ln_g': 'new_m', 'new_m_sgu_ln_b': 'new_m', 'new_m_w_spatial': 'new_m', 'new_m_b_spatial': 'new_m', 'new_m_w_sgu_out': 'new_m', 'new_m_w_o': 'new_m', 'new_m_norm_ffn': 'new_m', 'new_m_w_ffn_gate': 'new_m', 'new_m_w_ffn_up': 'new_m', 'new_m_w_ffn_down': 'new_m', 'new_m_norm_final': 'new_m', 'new_v_norm_mix': 'new_v', 'new_v_w_in': 'new_v', 'new_v_gate_bias': 'new_v', 'new_v_conv_w': 'new_v', 'new_v_conv_b': 'new_v', 'new_v_conv_ln_g': 'new_v', 'new_v_conv_ln_b': 'new_v', 'new_v_w_conv_out': 'new_v', 'new_v_sgu_ln_g': 'new_v', 'new_v_sgu_ln_b': 'new_v', 'new_v_w_spatial': 'new_v', 'new_v_b_spatial': 'new_v', 'new_v_w_sgu_out': 'new_v', 'new_v_w_o': 'new_v', 'new_v_norm_ffn': 'new_v', 'new_v_w_ffn_gate': 'new_v', 'new_v_w_ffn_up': 'new_v', 'new_v_w_ffn_down': 'new_v', 'new_v_norm_final': 'new_v'}


def _forward(args):
    return _fwd_reference(*[args[k] for k in FWD_PARAMS])


def _output_shape():
    out = _jax.eval_shape(lambda: _forward(_fwd_setup_inputs(0)))
    return out.shape, out.dtype

N_MICROBATCH = 1
ADAM_LR = 0.001
ADAM_B1 = 0.9
ADAM_B2 = 0.999
ADAM_EPS = 1e-08
ADAM_WD = 0.01
ADAM_STEP = 10
PER_EXAMPLE_BATCH_AXIS = {'x': 0, 'loss_target': 0}
SHARED_INPUTS = []
_WEIGHT_DTYPES = {'norm_mix': _jnp.float32, 'w_in': _jnp.float32, 'gate_bias': _jnp.float32, 'conv_w': _jnp.float32, 'conv_b': _jnp.float32, 'conv_ln_g': _jnp.float32, 'conv_ln_b': _jnp.float32, 'w_conv_out': _jnp.float32, 'sgu_ln_g': _jnp.float32, 'sgu_ln_b': _jnp.float32, 'w_spatial': _jnp.float32, 'b_spatial': _jnp.float32, 'w_sgu_out': _jnp.float32, 'w_o': _jnp.float32, 'norm_ffn': _jnp.float32, 'w_ffn_gate': _jnp.float32, 'w_ffn_up': _jnp.float32, 'w_ffn_down': _jnp.float32, 'norm_final': _jnp.float32}
MOMENT_SCALE = {'norm_mix': 2.135247e-01, 'w_in': 8.765460e-02, 'gate_bias': 5.034244e-02, 'conv_w': 6.826605e-02, 'conv_b': 1.241673e-01, 'conv_ln_g': 7.725395e-02, 'conv_ln_b': 6.595978e-02, 'w_conv_out': 6.606505e-02, 'sgu_ln_g': 1.069232e-01, 'sgu_ln_b': 1.176815e-01, 'w_spatial': 1.101370e-01, 'b_spatial': 1.128708e-01, 'w_sgu_out': 1.555602e-01, 'w_o': 1.686108e-01, 'norm_ffn': 1.498609e-01, 'w_ffn_gate': 6.433986e-02, 'w_ffn_up': 6.245124e-02, 'w_ffn_down': 1.033806e-01, 'norm_final': 6.383854e+01}


def _to_microbatches(a, axis):
    t = _jnp.moveaxis(a, axis, 0)
    t = t.reshape((N_MICROBATCH, t.shape[0] // N_MICROBATCH) + t.shape[1:])
    return _jnp.moveaxis(t, 1, axis + 1)


def setup_inputs(seed: int = 0) -> dict:
    inp = _fwd_setup_inputs(seed)
    key = _jax.random.fold_in(_jax.random.key(seed), 7919)
    shape, _ = _output_shape()
    out = dict(inp)
    out["loss_target"] = _jax.random.normal(_jax.random.fold_in(key, 0), shape, _jnp.float32)
    for i, name in enumerate(TWIN_WEIGHTS):
        w = inp[name].astype(_jnp.float32)
        if MOMENT_SCALE is None:
            s = _jnp.sqrt(_jnp.mean(_jnp.square(w)) + 1e-30)
        else:
            s = MOMENT_SCALE[name]
        km, kv = _jax.random.split(_jax.random.fold_in(key, i + 1))
        out[name] = w
        out["m_" + name] = s * _jax.random.normal(km, w.shape, _jnp.float32)
        out["v_" + name] = (s * s) * _jax.random.uniform(kv, w.shape, _jnp.float32, 0.5, 1.5)
    if N_MICROBATCH > 1:
        for name, axis in PER_EXAMPLE_BATCH_AXIS.items():
            out[name] = _to_microbatches(out[name], axis)
    return {'x': out['x'], 'norm_mix': out['norm_mix'], 'w_in': out['w_in'], 'gate_bias': out['gate_bias'], 'conv_w': out['conv_w'], 'conv_b': out['conv_b'], 'conv_ln_g': out['conv_ln_g'], 'conv_ln_b': out['conv_ln_b'], 'w_conv_out': out['w_conv_out'], 'sgu_ln_g': out['sgu_ln_g'], 'sgu_ln_b': out['sgu_ln_b'], 'w_spatial': out['w_spatial'], 'b_spatial': out['b_spatial'], 'w_sgu_out': out['w_sgu_out'], 'w_o': out['w_o'], 'norm_ffn': out['norm_ffn'], 'w_ffn_gate': out['w_ffn_gate'], 'w_ffn_up': out['w_ffn_up'], 'w_ffn_down': out['w_ffn_down'], 'norm_final': out['norm_final'], 'loss_target': out['loss_target'], 'm_norm_mix': out['m_norm_mix'], 'm_w_in': out['m_w_in'], 'm_gate_bias': out['m_gate_bias'], 'm_conv_w': out['m_conv_w'], 'm_conv_b': out['m_conv_b'], 'm_conv_ln_g': out['m_conv_ln_g'], 'm_conv_ln_b': out['m_conv_ln_b'], 'm_w_conv_out': out['m_w_conv_out'], 'm_sgu_ln_g': out['m_sgu_ln_g'], 'm_sgu_ln_b': out['m_sgu_ln_b'], 'm_w_spatial': out['m_w_spatial'], 'm_b_spatial': out['m_b_spatial'], 'm_w_sgu_out': out['m_w_sgu_out'], 'm_w_o': out['m_w_o'], 'm_norm_ffn': out['m_norm_ffn'], 'm_w_ffn_gate': out['m_w_ffn_gate'], 'm_w_ffn_up': out['m_w_ffn_up'], 'm_w_ffn_down': out['m_w_ffn_down'], 'm_norm_final': out['m_norm_final'], 'v_norm_mix': out['v_norm_mix'], 'v_w_in': out['v_w_in'], 'v_gate_bias': out['v_gate_bias'], 'v_conv_w': out['v_conv_w'], 'v_conv_b': out['v_conv_b'], 'v_conv_ln_g': out['v_conv_ln_g'], 'v_conv_ln_b': out['v_conv_ln_b'], 'v_w_conv_out': out['v_w_conv_out'], 'v_sgu_ln_g': out['v_sgu_ln_g'], 'v_sgu_ln_b': out['v_sgu_ln_b'], 'v_w_spatial': out['v_w_spatial'], 'v_b_spatial': out['v_b_spatial'], 'v_w_sgu_out': out['v_w_sgu_out'], 'v_w_o': out['v_w_o'], 'v_norm_ffn': out['v_norm_ffn'], 'v_w_ffn_gate': out['v_w_ffn_gate'], 'v_w_ffn_up': out['v_w_ffn_up'], 'v_w_ffn_down': out['v_w_ffn_down'], 'v_norm_final': out['v_norm_final']}


def _loss(weights, diff, rest, loss_target):
    with _jax.named_scope("forward"):
        args = {**rest, TWIN_DIFF_INPUT: diff, **{k: w.astype(_WEIGHT_DTYPES[k]) for k, w in weights.items()}}
        y = _forward(args)
    with _jax.named_scope("loss_head"):
        err = _jnp.square(y.astype(_jnp.float32) - loss_target)
        return 0.5 * _jnp.sum(_jnp.mean(err, axis=-1)) if err.ndim else 0.5 * err


def _adamw(w, g, m, v):
    m = ADAM_B1 * m + (1.0 - ADAM_B1) * g
    v = ADAM_B2 * v + (1.0 - ADAM_B2) * _jnp.square(g)
    m_hat = m / (1.0 - ADAM_B1 ** ADAM_STEP)
    v_hat = v / (1.0 - ADAM_B2 ** ADAM_STEP)
    delta = -ADAM_LR * (m_hat / (_jnp.sqrt(v_hat) + ADAM_EPS) + ADAM_WD * w)
    return delta, m, v


def reference(x, norm_mix, w_in, gate_bias, conv_w, conv_b, conv_ln_g, conv_ln_b, w_conv_out, sgu_ln_g, sgu_ln_b, w_spatial, b_spatial, w_sgu_out, w_o, norm_ffn, w_ffn_gate, w_ffn_up, w_ffn_down, norm_final, loss_target, m_norm_mix, m_w_in, m_gate_bias, m_conv_w, m_conv_b, m_conv_ln_g, m_conv_ln_b, m_w_conv_out, m_sgu_ln_g, m_sgu_ln_b, m_w_spatial, m_b_spatial, m_w_sgu_out, m_w_o, m_norm_ffn, m_w_ffn_gate, m_w_ffn_up, m_w_ffn_down, m_norm_final, v_norm_mix, v_w_in, v_gate_bias, v_conv_w, v_conv_b, v_conv_ln_g, v_conv_ln_b, v_w_conv_out, v_sgu_ln_g, v_sgu_ln_b, v_w_spatial, v_b_spatial, v_w_sgu_out, v_w_o, v_norm_ffn, v_w_ffn_gate, v_w_ffn_up, v_w_ffn_down, v_norm_final):
    given = dict(x=x, norm_mix=norm_mix, w_in=w_in, gate_bias=gate_bias, conv_w=conv_w, conv_b=conv_b, conv_ln_g=conv_ln_g, conv_ln_b=conv_ln_b, w_conv_out=w_conv_out, sgu_ln_g=sgu_ln_g, sgu_ln_b=sgu_ln_b, w_spatial=w_spatial, b_spatial=b_spatial, w_sgu_out=w_sgu_out, w_o=w_o, norm_ffn=norm_ffn, w_ffn_gate=w_ffn_gate, w_ffn_up=w_ffn_up, w_ffn_down=w_ffn_down, norm_final=norm_final, loss_target=loss_target, m_norm_mix=m_norm_mix, m_w_in=m_w_in, m_gate_bias=m_gate_bias, m_conv_w=m_conv_w, m_conv_b=m_conv_b, m_conv_ln_g=m_conv_ln_g, m_conv_ln_b=m_conv_ln_b, m_w_conv_out=m_w_conv_out, m_sgu_ln_g=m_sgu_ln_g, m_sgu_ln_b=m_sgu_ln_b, m_w_spatial=m_w_spatial, m_b_spatial=m_b_spatial, m_w_sgu_out=m_w_sgu_out, m_w_o=m_w_o, m_norm_ffn=m_norm_ffn, m_w_ffn_gate=m_w_ffn_gate, m_w_ffn_up=m_w_ffn_up, m_w_ffn_down=m_w_ffn_down, m_norm_final=m_norm_final, v_norm_mix=v_norm_mix, v_w_in=v_w_in, v_gate_bias=v_gate_bias, v_conv_w=v_conv_w, v_conv_b=v_conv_b, v_conv_ln_g=v_conv_ln_g, v_conv_ln_b=v_conv_ln_b, v_w_conv_out=v_w_conv_out, v_sgu_ln_g=v_sgu_ln_g, v_sgu_ln_b=v_sgu_ln_b, v_w_spatial=v_w_spatial, v_b_spatial=v_b_spatial, v_w_sgu_out=v_w_sgu_out, v_w_o=v_w_o, v_norm_ffn=v_norm_ffn, v_w_ffn_gate=v_w_ffn_gate, v_w_ffn_up=v_w_ffn_up, v_w_ffn_down=v_w_ffn_down, v_norm_final=v_norm_final)
    weights = {n: given[n] for n in TWIN_WEIGHTS}
    shared = {n: given[n] for n in SHARED_INPUTS}
    per_example = {n: given[n] for n in ['x']}
    grad_fn = _jax.value_and_grad(_loss, argnums=(0, 1))

    def one_microbatch(ex, loss_target):
        ex = dict(ex)
        diff = ex.pop(TWIN_DIFF_INPUT)
        return grad_fn(weights, diff, {**shared, **ex}, loss_target)

    if N_MICROBATCH == 1:
        loss, (grad_w, grad_x) = one_microbatch(per_example, given["loss_target"])
    else:
        def body(carry, xs):
            loss_sum, grad_sum = carry
            l_k, (gw_k, gx_k) = one_microbatch(xs[0], xs[1])
            with _jax.named_scope("update"):
                return (loss_sum + l_k, _jax.tree.map(_jnp.add, grad_sum, gw_k)), gx_k

        init = (_jnp.zeros((), _jnp.float32), _jax.tree.map(_jnp.zeros_like, weights))
        (loss, grad_w), grad_x = _jax.lax.scan(body, init, (per_example, given["loss_target"]))
    with _jax.named_scope("update"):
        delta_w, new_m, new_v = {}, {}, {}
        for n in TWIN_WEIGHTS:
            delta_w[n], new_m[n], new_v[n] = _adamw(weights[n], grad_w[n], given["m_" + n], given["v_" + n])
    return (loss, grad_x, *[grad_w[n] for n in TWIN_WEIGHTS], *[delta_w[n] for n in TWIN_WEIGHTS],
            *[new_m[n] for n in TWIN_WEIGHTS], *[new_v[n] for n in TWIN_WEIGHTS])
```

```python
import functools

import jax
import jax.numpy as jnp
from jax import lax
from jax.experimental import pallas as pl
from jax.experimental.pallas import tpu as pltpu

F32 = jnp.float32
BF16 = jnp.bfloat16
EPS = 1e-6
ADAM_LR = 0.001
ADAM_B1 = 0.9
ADAM_B2 = 0.999
ADAM_EPS = 1e-08
ADAM_WD = 0.01
ADAM_STEP = 10

NSHARD = 4
LANES = 128
HALO = 16
VMEM_LIMIT = 60 * 1024 * 1024
MESH_ID = pl.DeviceIdType.MESH


def _dot(a, b):
    return jnp.dot(a, b, preferred_element_type=F32)


def _dot_nt(a, b):
    return lax.dot_general(a, b, (((1,), (1,)), ((), ())), preferred_element_type=F32)


def _dot_tn(a, b):
    return lax.dot_general(a, b, (((0,), (0,)), ((), ())), preferred_element_type=F32)


def _sig(z):
    return 1.0 / (1.0 + jnp.exp(-z))


def _res(shape, imap=None):
    nd = len(shape)
    if imap is None:
        imap = lambda *_: (0,) * nd
    return pl.BlockSpec(shape, imap, pipeline_mode=pl.Buffered(1))


def _cparams(sem):
    return pltpu.CompilerParams(dimension_semantics=sem, vmem_limit_bytes=VMEM_LIMIT)


def _sds(shape, dtype):
    return jax.ShapeDtypeStruct(shape, dtype)


def _mix_in_fwd(x2d, g_mix, w_in_g, layer, tm):
    T, D = x2d.shape
    CS = w_in_g.shape[2]
    CN = CS // 3

    def body(x_ref, g_ref, w_ref, h_ref, p_ref):
        x = x_ref[...]
        rstd = lax.rsqrt(jnp.mean(x * x, axis=-1, keepdims=True) + EPS)
        h = (x * rstd * g_ref[...]).astype(BF16)
        h_ref[...] = h
        for s in range(NSHARD):
            for j in range(3):
                c0 = s * CS + j * CN
                p_ref[:, c0:c0 + CN] = _dot(h, w_ref[s, :, j * CN:(j + 1) * CN]).astype(BF16)

    return pl.pallas_call(
        body, name="mix_in_fwd", grid=(T // tm,),
        in_specs=[pl.BlockSpec((tm, D), lambda i: (i, 0)), _res((1, D)),
                  _res((NSHARD, D, CS), lambda i: (0, layer, 0))],
        out_specs=[pl.BlockSpec((tm, D), lambda i: (i, 0)), pl.BlockSpec((tm, NSHARD * CS), lambda i: (i, 0))],
        out_shape=[_sds((T, D), BF16), _sds((T, NSHARD * CS), BF16)],
        compiler_params=_cparams(("parallel",)),
    )(x2d, g_mix, w_in_g)


def _halo_maps(tm, n_rows):
    nb = tm // HALO
    last = n_rows // HALO - 1
    prev = lambda i: (jnp.maximum(i * nb - 1, 0), 0)
    nxt = lambda i: (jnp.minimum((i + 1) * nb, last), 0)
    return prev, nxt


def _dwconv(pad_ref, w_ref, out_ref, n_strips, tm, kw, rb):
    off = HALO - (kw - 1) // 2

    def strip(cs, carry):
        for r0 in range(0, tm, rb):
            acc = jnp.zeros((rb, LANES), F32)
            for k in range(kw):
                r = r0 + off + k
                acc = acc + w_ref[cs, k:k + 1, :] * pad_ref[cs, r:r + rb, :]
            out_ref[cs, r0:r0 + rb, :] = acc
        return carry

    lax.fori_loop(0, n_strips, strip, 0)


def _fill_c0_pad(pad_ref, pa_ref, pprev_ref, pnext_ref, D, tm, first, last):
    for cs in range(D // LANES):
        lo, hi = cs * LANES, (cs + 1) * LANES

        def c0_of(ref):
            return ref[:, lo:hi].astype(F32) * _sig(ref[:, D + lo:D + hi].astype(F32))

        pad_ref[cs, HALO:HALO + tm, :] = c0_of(pa_ref)
        pad_ref[cs, 0:HALO, :] = jnp.where(first, 0.0, c0_of(pprev_ref))
        pad_ref[cs, HALO + tm:HALO + tm + HALO, :] = jnp.where(last, 0.0, c0_of(pnext_ref))


def _conv_fwd(proj, conv_w_s, conv_b, ln_g, ln_b, w_sq_g, layer, seq, tm, rb):
    T = proj.shape[0]
    D = conv_b.shape[1]
    DQ = D // NSHARD
    NSTR = D // LANES
    KW = 31
    tps = seq // tm
    prev, nxt = _halo_maps(tm, T)

    def body(pa_ref, pprev_ref, pnext_ref, w_ref, b_ref, g_ref, be_ref, wco_ref,
             c1h_ref, rstd_ref, c3_ref, ya_ref, pad_ref, c1s_ref):
        i = pl.program_id(0)
        first = (i % tps) == 0
        last = (i % tps) == tps - 1
        _fill_c0_pad(pad_ref, pa_ref, pprev_ref, pnext_ref, D, tm, first, last)
        _dwconv(pad_ref, w_ref, c1s_ref, NSTR, tm, KW, rb)
        c1 = jnp.concatenate([c1s_ref[cs] for cs in range(NSTR)], axis=1) + b_ref[...]
        mu = jnp.mean(c1, axis=-1, keepdims=True)
        cc = c1 - mu
        rstd = lax.rsqrt(jnp.mean(cc * cc, axis=-1, keepdims=True) + EPS)
        c1h = cc * rstd
        c1h_ref[...] = c1h.astype(BF16)
        rstd_ref[...] = rstd
        c2 = c1h * g_ref[...] + be_ref[...]
        c3 = (c2 * _sig(c2)).astype(BF16)
        c3_ref[...] = c3
        ya_ref[...] = _dot(c3, wco_ref[...].reshape(D, D)).astype(BF16)

    row = lambda i: (i, 0)
    return pl.pallas_call(
        body, name="conv_fwd", grid=(T // tm,),
        in_specs=[pl.BlockSpec((tm, 2 * D), row), pl.BlockSpec((HALO, 2 * D), prev), pl.BlockSpec((HALO, 2 * D), nxt),
                  _res((NSTR, 32, LANES)), _res((1, D)), _res((1, D)), _res((1, D)),
                  _res((NSHARD, DQ, D), lambda i: (0, layer * 3 + 0, 0))],
        out_specs=[pl.BlockSpec((tm, D), row), pl.BlockSpec((tm, 1), row), pl.BlockSpec((tm, D), row),
                   pl.BlockSpec((tm, D), row)],
        out_shape=[_sds((T, D), BF16), _sds((T, 1), F32), _sds((T, D), BF16), _sds((T, D), BF16)],
        scratch_shapes=[pltpu.VMEM((NSTR, tm + 2 * HALO, LANES), F32), pltpu.VMEM((NSTR, tm, LANES), F32)],
        compiler_params=_cparams(("parallel",)),
    )(proj, proj, proj, conv_w_s, conv_b, ln_g, ln_b, w_sq_g)


def _sgu_merge_fwd(proj, ya, x2d, ln_g, ln_b, ws_b, bs_b, gate_bias, w_sq_g, layer, tm):
    T, D = x2d.shape
    DQ = D // NSHARD
    G, CH, _ = ws_b.shape
    GD = D // G

    def body(puv_ref, pg_ref, ya_ref, x_ref, g_ref, be_ref, ws_ref, bsb_ref, gb_ref, wso_ref, wo_ref,
             mixed_ref, gated_ref, yb_ref, merged_ref, x1_ref, mix_scr):
        u = puv_ref[:, :D].astype(F32)
        v = puv_ref[:, D:].astype(F32)
        mu = jnp.mean(v, axis=-1, keepdims=True)
        vc = v - mu
        rstd = lax.rsqrt(jnp.mean(vc * vc, axis=-1, keepdims=True) + EPS)
        vn = (vc * rstd * g_ref[...] + be_ref[...]).astype(BF16)
        for ch in range(tm // CH):
            for g in range(G):
                blk = vn[ch * CH:(ch + 1) * CH, g * GD:(g + 1) * GD]
                mix_scr[ch * CH:(ch + 1) * CH, g * GD:(g + 1) * GD] = (
                    _dot(ws_ref[g], blk) + bsb_ref[:, g * GD:(g + 1) * GD])
        mixed = mix_scr[...]
        mixed_ref[...] = mixed.astype(BF16)
        gated = (u * mixed).astype(BF16)
        gated_ref[...] = gated
        yb = _dot(gated, wso_ref[...].reshape(D, D))
        yb_ref[...] = yb.astype(BF16)
        sa = _sig(pg_ref[:, :D].astype(F32) + gb_ref[:, :D])
        sb = _sig(pg_ref[:, D:].astype(F32) + gb_ref[:, D:])
        merged = (sa * ya_ref[...].astype(F32) + sb * yb).astype(BF16)
        merged_ref[...] = merged
        x1_ref[...] = x_ref[...] + _dot(merged, wo_ref[...].reshape(D, D))

    row = lambda i: (i, 0)
    return pl.pallas_call(
        body, name="sgu_merge_fwd", grid=(T // tm,),
        in_specs=[pl.BlockSpec((tm, 2 * D), lambda i: (i, 1)), pl.BlockSpec((tm, 2 * D), lambda i: (i, 2)),
                  pl.BlockSpec((tm, D), row), pl.BlockSpec((tm, D), row),
                  _res((1, D)), _res((1, D)), _res((G, CH, CH)), _res((CH, D)), _res((1, 2 * D)),
                  _res((NSHARD, DQ, D), lambda i: (0, layer * 3 + 1, 0)),
                  _res((NSHARD, DQ, D), lambda i: (0, layer * 3 + 2, 0))],
        out_specs=[pl.BlockSpec((tm, D), row)] * 5,
        out_shape=[_sds((T, D), BF16)] * 4 + [_sds((T, D), F32)],
        scratch_shapes=[pltpu.VMEM((tm, D), F32)],
        compiler_params=_cparams(("parallel",)),
    )(proj, proj, ya, x2d, ln_g, ln_b, ws_b, bs_b, gate_bias, w_sq_g, w_sq_g)


def _ffn_fwd(x1, g_ffn, wg_g, wu_g, wd_g, layer, tm):
    T, D = x1.shape
    FP = wg_g.shape[2]
    F = NSHARD * FP

    def body(x_ref, g_ref, wg_ref, wu_ref, wd_ref, h2_ref, gt_ref, up_ref, act_ref, x2_ref):
        x = x_ref[...]
        rstd = lax.rsqrt(jnp.mean(x * x, axis=-1, keepdims=True) + EPS)
        h2 = (x * rstd * g_ref[...]).astype(BF16)
        h2_ref[...] = h2
        acc = x
        for s in range(NSHARD):
            gt = _dot(h2, wg_ref[s])
            up = _dot(h2, wu_ref[s])
            gt_ref[:, s * FP:(s + 1) * FP] = gt.astype(BF16)
            up_ref[:, s * FP:(s + 1) * FP] = up.astype(BF16)
            act = (gt * _sig(gt) * up).astype(BF16)
            act_ref[:, s * FP:(s + 1) * FP] = act
            acc = acc + _dot(act, wd_ref[s])
        x2_ref[...] = acc

    row = lambda i: (i, 0)
    return pl.pallas_call(
        body, name="ffn_fwd", grid=(T // tm,),
        in_specs=[pl.BlockSpec((tm, D), row), _res((1, D)),
                  _res((NSHARD, D, FP), lambda i: (0, layer, 0)), _res((NSHARD, D, FP), lambda i: (0, layer, 0)),
                  _res((NSHARD, FP, D), lambda i: (0, layer, 0))],
        out_specs=[pl.BlockSpec((tm, D), row), pl.BlockSpec((tm, F), row), pl.BlockSpec((tm, F), row),
                   pl.BlockSpec((tm, F), row), pl.BlockSpec((tm, D), row)],
        out_shape=[_sds((T, D), BF16), _sds((T, F), BF16), _sds((T, F), BF16), _sds((T, F), BF16), _sds((T, D), F32)],
        compiler_params=_cparams(("parallel",)),
    )(x1, g_ffn, wg_g, wu_g, wd_g)


def _loss_head(xf, g_fin, target, tm):
    T, D = xf.shape
    n = T // tm

    def body(x_ref, g_ref, t_ref, dx_ref, loss_ref, dg_ref, acc_ref):
        i = pl.program_id(0)

        @pl.when(i == 0)
        def _():
            acc_ref[...] = jnp.zeros_like(acc_ref)
            dg_ref[...] = jnp.zeros_like(dg_ref)

        x = x_ref[...]
        g = g_ref[...]
        rstd = lax.rsqrt(jnp.mean(x * x, axis=-1, keepdims=True) + EPS)
        xh = x * rstd
        diff = xh * g - t_ref[...]
        acc_ref[...] += jnp.sum(diff * diff, axis=0, keepdims=True)
        dy = diff * (1.0 / D)
        dg_ref[...] += jnp.sum(dy * xh, axis=0, keepdims=True)
        dxh = dy * g
        dx_ref[...] = rstd * (dxh - xh * jnp.mean(dxh * xh, axis=-1, keepdims=True))

        @pl.when(i == n - 1)
        def _():
            tot = jnp.sum(acc_ref[...], axis=-1, keepdims=True) * (0.5 / D)
            loss_ref[...] = jnp.broadcast_to(tot, loss_ref.shape)

    row = lambda i: (i, 0)
    return pl.pallas_call(
        body, name="loss_head", grid=(n,),
        in_specs=[pl.BlockSpec((tm, D), row), _res((1, D)), pl.BlockSpec((tm, D), row)],
        out_specs=[pl.BlockSpec((tm, D), row), pl.BlockSpec((1, LANES), lambda i: (0, 0)),
                   pl.BlockSpec((1, D), lambda i: (0, 0))],
        out_shape=[_sds((T, D), F32), _sds((1, LANES), F32), _sds((1, D), F32)],
        scratch_shapes=[pltpu.VMEM((1, D), F32)],
        compiler_params=_cparams(("arbitrary",)),
    )(xf, g_fin, target)


def _ffn_bwd(dx2, x1, gt, up, g_ffn, wg_g, wu_g, wd_g, layer, tm):
    T, D = x1.shape
    FP = wg_g.shape[2]
    F = NSHARD * FP

    def body(dx2_ref, x1_ref, gt_ref, up_ref, g_ref, wg_ref, wu_ref, wd_ref, dx1_ref, dgt_ref, dup_ref, dg_ref):
        i = pl.program_id(0)

        @pl.when(i == 0)
        def _():
            dg_ref[...] = jnp.zeros_like(dg_ref)

        dx2 = dx2_ref[...]
        dx2b = dx2.astype(BF16)
        dh2 = jnp.zeros((tm, D), F32)
        for s in range(NSHARD):
            dact = _dot_nt(dx2b, wd_ref[s])
            g = gt_ref[:, s * FP:(s + 1) * FP].astype(F32)
            u = up_ref[:, s * FP:(s + 1) * FP].astype(F32)
            sg = _sig(g)
            dup = (dact * (g * sg)).astype(BF16)
            dgt = (dact * u * (sg * (1.0 + g * (1.0 - sg)))).astype(BF16)
            dgt_ref[:, s * FP:(s + 1) * FP] = dgt
            dup_ref[:, s * FP:(s + 1) * FP] = dup
            dh2 = dh2 + _dot_nt(dgt, wg_ref[s]) + _dot_nt(dup, wu_ref[s])
        x = x1_ref[...]
        rstd = lax.rsqrt(jnp.mean(x * x, axis=-1, keepdims=True) + EPS)
        xh = x * rstd
        dg_ref[...] += jnp.sum(dh2 * xh, axis=0, keepdims=True)
        dxh = dh2 * g_ref[...]
        dx1_ref[...] = dx2 + rstd * (dxh - xh * jnp.mean(dxh * xh, axis=-1, keepdims=True))

    row = lambda i: (i, 0)
    return pl.pallas_call(
        body, name="ffn_bwd", grid=(T // tm,),
        in_specs=[pl.BlockSpec((tm, D), row), pl.BlockSpec((tm, D), row), pl.BlockSpec((tm, F), row),
                  pl.BlockSpec((tm, F), row), _res((1, D)),
                  _res((NSHARD, D, FP), lambda i: (0, layer, 0)), _res((NSHARD, D, FP), lambda i: (0, layer, 0)),
                  _res((NSHARD, FP, D), lambda i: (0, layer, 0))],
        out_specs=[pl.BlockSpec((tm, D), row), pl.BlockSpec((tm, F), row), pl.BlockSpec((tm, F), row),
                   pl.BlockSpec((1, D), lambda i: (0, 0))],
        out_shape=[_sds((T, D), F32), _sds((T, F), BF16), _sds((T, F), BF16), _sds((1, D), F32)],
        compiler_params=_cparams(("arbitrary",)),
    )(dx2, x1, gt, up, g_ffn, wg_g, wu_g, wd_g)


def _merge_sgu_bwd(dx1, proj, ya, yb, mixed, ln_g, ln_b, ws_b, wst_b, gate_bias, w_sq_g, layer, tm):
    T, D = dx1.shape
    DQ = D // NSHARD
    G, CH, _ = ws_b.shape
    GD = D // G

    def body(dx1_ref, puv_ref, pg_ref, ya_ref, yb_ref, mixed_ref, g_ref, be_ref, ws_ref, wst_ref, gb_ref,
             wso_ref, wo_ref, dya_ref, dyb_ref, dp_ref, dgb_ref, dlg_ref, dlb_ref, dbs_ref, dws_ref,
             dvn_scr, dbs_scr):
        i = pl.program_id(0)

        @pl.when(i == 0)
        def _():
            for r in (dgb_ref, dlg_ref, dlb_ref, dws_ref, dbs_scr):
                r[...] = jnp.zeros_like(r)

        dmerged = _dot_nt(dx1_ref[...].astype(BF16), wo_ref[...].reshape(D, D))
        sa = _sig(pg_ref[:, :D].astype(F32) + gb_ref[:, :D])
        sb = _sig(pg_ref[:, D:].astype(F32) + gb_ref[:, D:])
        dya = (dmerged * sa).astype(BF16)
        dyb = (dmerged * sb).astype(BF16)
        dya_ref[...] = dya
        dyb_ref[...] = dyb
        dga = dmerged * ya_ref[...].astype(F32) * (sa * (1.0 - sa))
        dgb = dmerged * yb_ref[...].astype(F32) * (sb * (1.0 - sb))
        dp_ref[1, :, :D] = dga.astype(BF16)
        dp_ref[1, :, D:] = dgb.astype(BF16)
        dgb_ref[:, :D] += jnp.sum(dga, axis=0, keepdims=True)
        dgb_ref[:, D:] += jnp.sum(dgb, axis=0, keepdims=True)

        dgated = _dot_nt(dyb, wso_ref[...].reshape(D, D))
        u = puv_ref[:, :D].astype(F32)
        v = puv_ref[:, D:].astype(F32)
        dp_ref[0, :, :D] = (dgated * mixed_ref[...].astype(F32)).astype(BF16)
        dmixed = dgated * u
        mu = jnp.mean(v, axis=-1, keepdims=True)
        vc = v - mu
        rstd = lax.rsqrt(jnp.mean(vc * vc, axis=-1, keepdims=True) + EPS)
        vh = vc * rstd
        vn = (vh * g_ref[...] + be_ref[...]).astype(BF16)
        dmb = dmixed.astype(BF16)
        bs_part = jnp.zeros((CH, D), F32)
        for ch in range(tm // CH):
            rows = slice(ch * CH, (ch + 1) * CH)
            bs_part = bs_part + dmixed[rows, :]
            for g in range(G):
                cols = slice(g * GD, (g + 1) * GD)
                dws_ref[g] += _dot_nt(dmb[rows, cols], vn[rows, cols])
                dvn_scr[rows, cols] = _dot(wst_ref[g], dmb[rows, cols])
        dbs_scr[...] += bs_part
        dvn = dvn_scr[...]
        dlg_ref[...] += jnp.sum(dvn * vh, axis=0, keepdims=True)
        dlb_ref[...] += jnp.sum(dvn, axis=0, keepdims=True)
        dxh = dvn * g_ref[...]
        dv = rstd * (dxh - jnp.mean(dxh, axis=-1, keepdims=True) - vh * jnp.mean(dxh * vh, axis=-1, keepdims=True))
        dp_ref[0, :, D:] = dv.astype(BF16)

        @pl.when(i == pl.num_programs(0) - 1)
        def _():
            for g in range(G):
                blk = dbs_scr[:, g * GD:(g + 1) * GD]
                if GD != CH:
                    blk = jnp.concatenate([blk, jnp.zeros((CH, CH - GD), F32)], axis=1)
                dbs_ref[:, g * CH:(g + 1) * CH] = jnp.sum(blk.T, axis=0, keepdims=True)

    row = lambda i: (i, 0)
    fixed2 = lambda i: (0, 0)
    return pl.pallas_call(
        body, name="merge_sgu_bwd", grid=(T // tm,),
        in_specs=[pl.BlockSpec((tm, D), row), pl.BlockSpec((tm, 2 * D), lambda i: (i, 1)),
                  pl.BlockSpec((tm, 2 * D), lambda i: (i, 2)), pl.BlockSpec((tm, D), row), pl.BlockSpec((tm, D), row),
                  pl.BlockSpec((tm, D), row), _res((1, D)), _res((1, D)), _res((G, CH, CH)), _res((G, CH, CH)),
                  _res((1, 2 * D)),
                  _res((NSHARD, DQ, D), lambda i: (0, layer * 3 + 1, 0)),
                  _res((NSHARD, DQ, D), lambda i: (0, layer * 3 + 2, 0))],
        out_specs=[pl.BlockSpec((tm, D), row), pl.BlockSpec((tm, D), row),
                   pl.BlockSpec((2, tm, 2 * D), lambda i: (0, i, 0)),
                   pl.BlockSpec((1, 2 * D), fixed2), pl.BlockSpec((1, D), fixed2), pl.BlockSpec((1, D), fixed2),
                   pl.BlockSpec((1, G * CH), fixed2), pl.BlockSpec((G, CH, CH), lambda i: (0, 0, 0))],
        out_shape=[_sds((T, D), BF16), _sds((T, D), BF16), _sds((3, T, 2 * D), BF16),
                   _sds((1, 2 * D), F32), _sds((1, D), F32), _sds((1, D), F32), _sds((1, G * CH), F32),
                   _sds((G, CH, CH), F32)],
        scratch_shapes=[pltpu.VMEM((tm, D), F32), pltpu.VMEM((CH, D), F32)],
        compiler_params=_cparams(("arbitrary",)),
    )(dx1, proj, proj, ya, yb, mixed, ln_g, ln_b, ws_b, wst_b, gate_bias, w_sq_g, w_sq_g)


def _conv_ln_bwd(dya, c1h, rstd_c, ln_g, ln_b, w_sq_g, layer, tm):
    T, D = dya.shape
    DQ = D // NSHARD

    def body(dya_ref, c1h_ref, rstd_ref, g_ref, be_ref, wco_ref, dc1_ref, dlg_ref, dlb_ref, dcb_ref):
        i = pl.program_id(0)

        @pl.when(i == 0)
        def _():
            for r in (dlg_ref, dlb_ref, dcb_ref):
                r[...] = jnp.zeros_like(r)

        dc3 = _dot_nt(dya_ref[...], wco_ref[...].reshape(D, D))
        c1h = c1h_ref[...].astype(F32)
        c2 = c1h * g_ref[...] + be_ref[...]
        sg = _sig(c2)
        dc2 = dc3 * (sg * (1.0 + c2 * (1.0 - sg)))
        dlg_ref[...] += jnp.sum(dc2 * c1h, axis=0, keepdims=True)
        dlb_ref[...] += jnp.sum(dc2, axis=0, keepdims=True)
        dxh = dc2 * g_ref[...]
        dc1 = rstd_ref[...] * (dxh - jnp.mean(dxh, axis=-1, keepdims=True)
                               - c1h * jnp.mean(dxh * c1h, axis=-1, keepdims=True))
        dc1_ref[...] = dc1
        dcb_ref[...] += jnp.sum(dc1, axis=0, keepdims=True)

    row = lambda i: (i, 0)
    fixed2 = lambda i: (0, 0)
    return pl.pallas_call(
        body, name="conv_ln_bwd", grid=(T // tm,),
        in_specs=[pl.BlockSpec((tm, D), row), pl.BlockSpec((tm, D), row), pl.BlockSpec((tm, 1), row),
                  _res((1, D)), _res((1, D)), _res((NSHARD, DQ, D), lambda i: (0, layer * 3 + 0, 0))],
        out_specs=[pl.BlockSpec((tm, D), row), pl.BlockSpec((1, D), fixed2), pl.BlockSpec((1, D), fixed2),
                   pl.BlockSpec((1, D), fixed2)],
        out_shape=[_sds((T, D), F32), _sds((1, D), F32), _sds((1, D), F32), _sds((1, D), F32)],
        compiler_params=_cparams(("arbitrary",)),
    )(dya, c1h, rstd_c, ln_g, ln_b, w_sq_g)


def _conv_bwd(dc1, proj, dp3, conv_wf_s, seq, tm, rb):
    T, D = dc1.shape
    NSTR = D // LANES
    KW = 31
    PADK = (KW - 1) // 2
    tps = seq // tm
    prev, nxt = _halo_maps(tm, T)
    n = T // tm

    def body(dc_ref, dcprev_ref, dcnext_ref, pa_ref, pprev_ref, pnext_ref, wf_ref, dp_in_ref,
             dp_ref, dw_ref, pad_ref, dpad_ref, dc0_ref, dwacc_ref):
        del dp_in_ref
        i = pl.program_id(0)
        first = (i % tps) == 0
        last = (i % tps) == tps - 1

        @pl.when(i == 0)
        def _():
            dwacc_ref[...] = jnp.zeros_like(dwacc_ref)

        _fill_c0_pad(pad_ref, pa_ref, pprev_ref, pnext_ref, D, tm, first, last)
        for cs in range(NSTR):
            lo, hi = cs * LANES, (cs + 1) * LANES
            dpad_ref[cs, HALO:HALO + tm, :] = dc_ref[:, lo:hi]
            dpad_ref[cs, 0:HALO, :] = jnp.where(first, 0.0, dcprev_ref[:, lo:hi])
            dpad_ref[cs, HALO + tm:HALO + tm + HALO, :] = jnp.where(last, 0.0, dcnext_ref[:, lo:hi])
        _dwconv(dpad_ref, wf_ref, dc0_ref, NSTR, tm, KW, rb)

        def strip(cs, carry):
            for r0 in range(0, tm, rb):
                d = dpad_ref[cs, HALO + r0:HALO + r0 + rb, :]
                for k in range(KW):
                    r = r0 + HALO - PADK + k
                    prod = d * pad_ref[cs, r:r + rb, :]
                    dwacc_ref[cs, k * 8:(k + 1) * 8, :] += jnp.sum(prod.reshape(rb // 8, 8, LANES), axis=0)
            return carry

        lax.fori_loop(0, NSTR, strip, 0)

        for cs in range(NSTR):
            lo, hi = cs * LANES, (cs + 1) * LANES
            av = pa_ref[:, lo:hi].astype(F32)
            sg = _sig(pa_ref[:, D + lo:D + hi].astype(F32))
            dc0 = dc0_ref[cs]
            dp_ref[:, lo:hi] = (dc0 * sg).astype(BF16)
            dp_ref[:, D + lo:D + hi] = (dc0 * av * (sg * (1.0 - sg))).astype(BF16)

        @pl.when(i == n - 1)
        def _():
            for cs in range(NSTR):
                dw_ref[cs] = jnp.sum(dwacc_ref[cs].reshape(32, 8, LANES), axis=1)

    row = lambda i: (i, 0)
    return pl.pallas_call(
        body, name="conv_bwd", grid=(n,),
        in_specs=[pl.BlockSpec((tm, D), row), pl.BlockSpec((HALO, D), prev), pl.BlockSpec((HALO, D), nxt),
                  pl.BlockSpec((tm, 2 * D), row), pl.BlockSpec((HALO, 2 * D), prev), pl.BlockSpec((HALO, 2 * D), nxt),
                  _res((NSTR, 32, LANES)), pl.BlockSpec(memory_space=pl.ANY)],
        out_specs=[pl.BlockSpec((None, tm, 2 * D), lambda i: (2, i, 0)),
                   pl.BlockSpec((NSTR, 32, LANES), lambda i: (0, 0, 0))],
        out_shape=[_sds(dp3.shape, BF16), _sds((NSTR, 32, LANES), F32)],
        scratch_shapes=[pltpu.VMEM((NSTR, tm + 2 * HALO, LANES), F32), pltpu.VMEM((NSTR, tm + 2 * HALO, LANES), F32),
                        pltpu.VMEM((NSTR, tm, LANES), F32), pltpu.VMEM((NSTR, 32 * 8, LANES), F32)],
        input_output_aliases={7: 0},
        compiler_params=_cparams(("arbitrary",)),
    )(dc1, dc1, dc1, proj, proj, proj, conv_wf_s, dp3)


def _dp3_chunk(j):
    return (j // 4 + 2) % 3, j % 4


def _mix_in_bwd(dx1, dp3, x2d, g_mix, w_in_g, layer, tm):
    T, D = x2d.shape
    CS = w_in_g.shape[2]
    CN = CS // 3

    def body(dx1_ref, dp_ref, x_ref, g_ref, w_ref, dx_ref, dg_ref):
        i = pl.program_id(0)

        @pl.when(i == 0)
        def _():
            dg_ref[...] = jnp.zeros_like(dg_ref)

        dh = jnp.zeros((tm, D), F32)
        for j in range(12):
            slab, cb = _dp3_chunk(j)
            dh = dh + _dot_nt(dp_ref[slab, :, cb * CN:(cb + 1) * CN], w_ref[j // 3, :, (j % 3) * CN:(j % 3 + 1) * CN])
        x = x_ref[...]
        rstd = lax.rsqrt(jnp.mean(x * x, axis=-1, keepdims=True) + EPS)
        xh = x * rstd
        dg_ref[...] += jnp.sum(dh * xh, axis=0, keepdims=True)
        dxh = dh * g_ref[...]
        dx_ref[...] = dx1_ref[...] + rstd * (dxh - xh * jnp.mean(dxh * xh, axis=-1, keepdims=True))

    row = lambda i: (i, 0)
    return pl.pallas_call(
        body, name="mix_in_bwd", grid=(T // tm,),
        in_specs=[pl.BlockSpec((tm, D), row), pl.BlockSpec((3, tm, 2 * D), lambda i: (0, i, 0)),
                  pl.BlockSpec((tm, D), row), _res((1, D)), _res((NSHARD, D, CS), lambda i: (0, layer, 0))],
        out_specs=[pl.BlockSpec((tm, D), row), pl.BlockSpec((1, D), lambda i: (0, 0))],
        out_shape=[_sds((T, D), F32), _sds((1, D), F32)],
        compiler_params=_cparams(("arbitrary",)),
    )(dx1, dp3, x2d, g_mix, w_in_g)


def _tn_matmul(name, a, a_block, a_map, b, b_block, b_map, out_shape, out_block, out_map, nj, nt, prev=None):
    kk = [d for d in a_block if d is not None][-1]
    nn = [d for d in b_block if d is not None][-1]

    def body(*refs):
        a_ref, b_ref = refs[0], refs[1]
        o_ref, acc_ref = refs[-2], refs[-1]
        t = pl.program_id(1)

        @pl.when(t == 0)
        def _():
            acc_ref[...] = jnp.zeros_like(acc_ref)

        acc_ref[...] += _dot_tn(a_ref[...].astype(BF16), b_ref[...].astype(BF16))

        @pl.when(t == nt - 1)
        def _():
            o_ref[...] = acc_ref[...].astype(o_ref.dtype)

    in_specs = [pl.BlockSpec(a_block, a_map), pl.BlockSpec(b_block, b_map)]
    args = [a, b]
    aliases = {}
    if prev is not None:
        in_specs.append(pl.BlockSpec(memory_space=pl.ANY))
        args.append(prev)
        aliases = {2: 0}
    return pl.pallas_call(
        body, name=name, grid=(nj, nt), in_specs=in_specs,
        out_specs=pl.BlockSpec(out_block, out_map), out_shape=_sds(out_shape, BF16),
        scratch_shapes=[pltpu.VMEM((kk, nn), F32)], input_output_aliases=aliases,
        compiler_params=_cparams(("parallel", "arbitrary")),
    )(*args)


def _add_halves(name, g, rbuf, c_idx, tr):
    NB, R, C = g.shape
    H = R // 2
    nr = H // tr

    def body(c_ref, g_ref, r_ref, o_ref):
        del c_ref
        o_ref[...] = (g_ref[...].astype(F32) + r_ref[...].astype(F32)).astype(BF16)

    grid_spec = pltpu.PrefetchScalarGridSpec(
        num_scalar_prefetch=1, grid=(NB, nr),
        in_specs=[pl.BlockSpec((None, tr, C), lambda b, r, c: (b, c[0] * nr + r, 0)),
                  pl.BlockSpec((None, tr, C), lambda b, r, c: (b, r, 0))],
        out_specs=pl.BlockSpec((None, tr, C), lambda b, r, c: (b, r, 0)))
    return pl.pallas_call(body, name=name, grid_spec=grid_spec, out_shape=_sds((NB, H, C), BF16),
                          compiler_params=_cparams(("parallel", "parallel")))(c_idx, g, rbuf)


def _add_shards(name, p, rbuf, s_idx, tr):
    L, _, H, C = p.shape
    nr = H // tr

    def body(s_ref, p_ref, r_ref, o_ref):
        del s_ref
        acc = p_ref[...].astype(F32)
        for j in range(3):
            acc = acc + r_ref[j].astype(F32)
        o_ref[...] = acc

    grid_spec = pltpu.PrefetchScalarGridSpec(
        num_scalar_prefetch=1, grid=(L, nr),
        in_specs=[pl.BlockSpec((None, None, tr, C), lambda l, r, s: (l, s[0], r, 0)),
                  pl.BlockSpec((3, None, tr, C), lambda l, r, s: (0, l, r, 0))],
        out_specs=pl.BlockSpec((None, tr, C), lambda l, r, s: (l, r, 0)))
    return pl.pallas_call(body, name=name, grid_spec=grid_spec, out_shape=_sds((L, H, C), F32),
                          compiler_params=_cparams(("parallel", "parallel")))(s_idx, p, rbuf)


def _sum_slots(buf, tr):
    NS8, R, C = buf.shape

    def body(b_ref, o_ref):
        acc = b_ref[0]
        for j in range(1, NS8):
            acc = acc + b_ref[j]
        o_ref[...] = acc

    return pl.pallas_call(
        body, name="sum_slots", grid=(R // tr,),
        in_specs=[pl.BlockSpec((NS8, tr, C), lambda i: (0, i, 0))],
        out_specs=pl.BlockSpec((tr, C), lambda i: (i, 0)), out_shape=_sds((R, C), F32),
        compiler_params=_cparams(("parallel",)))(buf)


def _adamw(name, w, g, m, v, tr):
    R, C = w.shape
    c1 = 1.0 - ADAM_B1 ** ADAM_STEP
    c2 = 1.0 - ADAM_B2 ** ADAM_STEP

    def body(w_ref, g_ref, m_ref, v_ref, d_ref, mo_ref, vo_ref):
        g_ = g_ref[...]
        m_ = ADAM_B1 * m_ref[...] + (1.0 - ADAM_B1) * g_
        v_ = ADAM_B2 * v_ref[...] + (1.0 - ADAM_B2) * (g_ * g_)
        mo_ref[...] = m_
        vo_ref[...] = v_
        d_ref[...] = -ADAM_LR * ((m_ / c1) / (jnp.sqrt(v_ / c2) + ADAM_EPS) + ADAM_WD * w_ref[...])

    spec = pl.BlockSpec((tr, C), lambda i: (i, 0))
    return pl.pallas_call(
        body, name=name, grid=(R // tr,), in_specs=[spec] * 4, out_specs=[spec] * 3,
        out_shape=[_sds((R, C), F32)] * 3, compiler_params=_cparams(("parallel",)))(w, g, m, v)


def _row_tile(rows, cap):
    best = rows
    for t in range(8, min(rows, cap) + 1, 8):
        if rows % t == 0:
            best = t
    return best


HBM_SPEC = pl.BlockSpec(memory_space=pltpu.HBM)


def _mesh_pos():
    return lax.axis_index("x"), lax.axis_index("y"), lax.axis_index("c")


def _other_chips(x, y):
    return [(1 - x, y), (x, 1 - y), (1 - x, 1 - y)]


def _gather_shards(arrs):
    n = len(arrs)

    def body(*refs):
        ins, outs = refs[:n], refs[n:2 * n]
        send_ici, recv_ici, send_d2d, recv_d2d, loc = refs[2 * n:]
        x, y, c = _mesh_pos()
        s = 2 * x + y
        chips = _other_chips(x, y)
        local = []
        for i in range(n):
            cp = pltpu.make_async_copy(ins[i], outs[i].at[s], loc.at[i])
            cp.start()
            local.append(cp)

        def half(i, shard, core):
            h = arrs[i].shape[0] // 2
            return outs[i].at[shard, pl.ds(core * h, h), :]

        sends = []
        for i in range(n):
            h = arrs[i].shape[0] // 2
            for j, (px, py) in enumerate(chips):
                cp = pltpu.make_async_remote_copy(
                    src_ref=ins[i].at[pl.ds(c * h, h), :], dst_ref=half(i, s, c),
                    send_sem=send_ici.at[i, j], recv_sem=recv_ici.at[i, j],
                    device_id=(px, py, c), device_id_type=MESH_ID)
                cp.start()
                sends.append(cp)
        for i in range(n):
            for j, (px, py) in enumerate(chips):
                ps = 2 * px + py
                landed = half(i, ps, c)
                pltpu.make_async_remote_copy(
                    src_ref=landed, dst_ref=landed, send_sem=send_ici.at[i, j], recv_sem=recv_ici.at[i, j],
                    device_id=(px, py, c), device_id_type=MESH_ID).wait_recv()
                cp = pltpu.make_async_remote_copy(
                    src_ref=landed, dst_ref=landed, send_sem=send_d2d.at[i, j], recv_sem=recv_d2d.at[i, j],
                    device_id=(x, y, 1 - c), device_id_type=MESH_ID)
                cp.start()
                sends.append(cp)
        for i in range(n):
            for j, (px, py) in enumerate(chips):
                ps = 2 * px + py
                theirs = half(i, ps, 1 - c)
                pltpu.make_async_remote_copy(
                    src_ref=theirs, dst_ref=theirs, send_sem=send_d2d.at[i, j], recv_sem=recv_d2d.at[i, j],
                    device_id=(x, y, 1 - c), device_id_type=MESH_ID).wait_recv()
        for cp in sends:
            cp.wait_send()
        for cp in local:
            cp.wait()

    sem = pltpu.SemaphoreType.DMA
    return pl.pallas_call(
        body, name="gather_shards",
        in_specs=[HBM_SPEC] * n, out_specs=[HBM_SPEC] * n,
        out_shape=[_sds((NSHARD,) + a.shape, a.dtype) for a in arrs],
        scratch_shapes=[sem((n, 3)), sem((n, 3)), sem((n, 3)), sem((n, 3)), sem((n,))],
    )(*arrs)


def _send_sibling_halves(arrs):
    n = len(arrs)

    def body(*refs):
        ins, outs = refs[:n], refs[n:2 * n]
        send, recv = refs[2 * n:]
        x, y, c = _mesh_pos()
        cps = []
        for i in range(n):
            h = arrs[i].shape[1] // 2
            cp = pltpu.make_async_remote_copy(
                src_ref=ins[i].at[:, pl.ds((1 - c) * h, h), :], dst_ref=outs[i],
                send_sem=send.at[i], recv_sem=recv.at[i], device_id=(x, y, 1 - c), device_id_type=MESH_ID)
            cp.start()
            cps.append(cp)
        for cp in cps:
            cp.wait()

    sem = pltpu.SemaphoreType.DMA
    return pl.pallas_call(
        body, name="send_sibling_halves", in_specs=[HBM_SPEC] * n, out_specs=[HBM_SPEC] * n,
        out_shape=[_sds((a.shape[0], a.shape[1] // 2, a.shape[2]), a.dtype) for a in arrs],
        scratch_shapes=[sem((n,)), sem((n,))],
    )(*arrs)


def _send_chip_shards(arrs):
    n = len(arrs)

    def body(*refs):
        ins, outs = refs[:n], refs[n:2 * n]
        send, recv = refs[2 * n:]
        x, y, c = _mesh_pos()
        cps = []
        for i in range(n):
            for j, (px, py) in enumerate(_other_chips(x, y)):
                cp = pltpu.make_async_remote_copy(
                    src_ref=ins[i].at[:, 2 * px + py], dst_ref=outs[i].at[j],
                    send_sem=send.at[i, j], recv_sem=recv.at[i, j], device_id=(px, py, c), device_id_type=MESH_ID)
                cp.start()
                cps.append(cp)
        for cp in cps:
            cp.wait()

    sem = pltpu.SemaphoreType.DMA
    return pl.pallas_call(
        body, name="send_chip_shards", in_specs=[HBM_SPEC] * n, out_specs=[HBM_SPEC] * n,
        out_shape=[_sds((3, a.shape[0], a.shape[2], a.shape[3]), a.dtype) for a in arrs],
        scratch_shapes=[sem((n, 3)), sem((n, 3))],
    )(*arrs)


def _join_halves(arrs):
    n = len(arrs)

    def body(*refs):
        ins, outs = refs[:n], refs[n:2 * n]
        send, recv, loc = refs[2 * n:]
        x, y, c = _mesh_pos()
        cps = []
        for i in range(n):
            h = arrs[i].shape[1]
            mine = outs[i].at[:, pl.ds(c * h, h), :]
            lc = pltpu.make_async_copy(ins[i], mine, loc.at[i])
            lc.start()
            cp = pltpu.make_async_remote_copy(
                src_ref=ins[i], dst_ref=mine, send_sem=send.at[i], recv_sem=recv.at[i],
                device_id=(x, y, 1 - c), device_id_type=MESH_ID)
            cp.start()
            cps.append((lc, cp))
        for i, (lc, cp) in enumerate(cps):
            h = arrs[i].shape[1]
            theirs = outs[i].at[:, pl.ds((1 - c) * h, h), :]
            cp.wait_send()
            pltpu.make_async_remote_copy(
                src_ref=ins[i], dst_ref=theirs, send_sem=send.at[i], recv_sem=recv.at[i],
                device_id=(x, y, 1 - c), device_id_type=MESH_ID).wait_recv()
            lc.wait()

    sem = pltpu.SemaphoreType.DMA
    return pl.pallas_call(
        body, name="join_halves", in_specs=[HBM_SPEC] * n, out_specs=[HBM_SPEC] * n,
        out_shape=[_sds((a.shape[0], 2 * a.shape[1], a.shape[2]), a.dtype) for a in arrs],
        scratch_shapes=[sem((n,)), sem((n,)), sem((n,))],
    )(*arrs)


def _exchange_all(buf):
    def body(in_ref, out_ref, send, recv, loc):
        x, y, c = _mesh_pos()
        me = 4 * x + 2 * y + c
        lc = pltpu.make_async_copy(in_ref, out_ref.at[me], loc)
        lc.start()
        cps = []
        for k in range(1, 8):
            fx, fy, fc = (k >> 2) & 1, (k >> 1) & 1, k & 1
            peer = (x ^ fx, y ^ fy, c ^ fc)
            cp = pltpu.make_async_remote_copy(
                src_ref=in_ref, dst_ref=out_ref.at[me], send_sem=send.at[k - 1], recv_sem=recv.at[k - 1],
                device_id=peer, device_id_type=MESH_ID)
            cp.start()
            cps.append(cp)
        for k in range(1, 8):
            fx, fy, fc = (k >> 2) & 1, (k >> 1) & 1, k & 1
            peer = (x ^ fx, y ^ fy, c ^ fc)
            src_slot = 4 * peer[0] + 2 * peer[1] + peer[2]
            cps[k - 1].wait_send()
            pltpu.make_async_remote_copy(
                src_ref=in_ref, dst_ref=out_ref.at[src_slot], send_sem=send.at[k - 1], recv_sem=recv.at[k - 1],
                device_id=peer, device_id_type=MESH_ID).wait_recv()
        lc.wait()

    sem = pltpu.SemaphoreType.DMA
    return pl.pallas_call(
        body, name="exchange_all", in_specs=[HBM_SPEC], out_specs=HBM_SPEC,
        out_shape=_sds((8,) + buf.shape, buf.dtype),
        scratch_shapes=[sem((7,)), sem((7,)), sem(())],
    )(buf)


def _pad_to(a, axis, size):
    pad = [(0, 0)] * a.ndim
    pad[axis] = (0, size - a.shape[axis])
    return jnp.pad(a, pad)


def _strips(w):
    k, d = w.shape
    return _pad_to(w, 0, 32).reshape(32, d // LANES, LANES).transpose(1, 0, 2)


def kernel(x, norm_mix, w_in, gate_bias, conv_w, conv_b, conv_ln_g, conv_ln_b, w_conv_out, sgu_ln_g, sgu_ln_b, w_spatial, b_spatial, w_sgu_out, w_o, norm_ffn, w_ffn_gate, w_ffn_up, w_ffn_down, norm_final, loss_target, m_norm_mix, m_w_in, m_gate_bias, m_conv_w, m_conv_b, m_conv_ln_g, m_conv_ln_b, m_w_conv_out, m_sgu_ln_g, m_sgu_ln_b, m_w_spatial, m_b_spatial, m_w_sgu_out, m_w_o, m_norm_ffn, m_w_ffn_gate, m_w_ffn_up, m_w_ffn_down, m_norm_final, v_norm_mix, v_w_in, v_gate_bias, v_conv_w, v_conv_b, v_conv_ln_g, v_conv_ln_b, v_w_conv_out, v_sgu_ln_g, v_sgu_ln_b, v_w_spatial, v_b_spatial, v_w_sgu_out, v_w_o, v_norm_ffn, v_w_ffn_gate, v_w_ffn_up, v_w_ffn_down, v_norm_final):
    BL, S, D = x.shape
    T = BL * S
    L = w_in.shape[0]
    CS = w_in.shape[2]
    CN = CS // 3
    DQ = D // NSHARD
    FS = w_ffn_gate.shape[2]
    FP = -(-FS // 256) * 256
    G, CH = w_spatial.shape[1], w_spatial.shape[2]
    KW = conv_w.shape[1]
    CQ = conv_w.shape[3]
    NSTR = D // LANES
    tm = min(512, S // 2)
    tm2 = max(tm // 2, CH)
    rb = min(64, tm)
    mx, my, mc = _mesh_pos()
    s_idx = (2 * mx + my).astype(jnp.int32).reshape(1)
    c_idx = mc.astype(jnp.int32).reshape(1)

    w_in_b = w_in.astype(BF16).reshape(L * D, CS)
    w_sq_b = jnp.stack([w_conv_out, w_sgu_out, w_o], axis=1).astype(BF16).reshape(L * 3 * DQ, D)
    wg_b = _pad_to(w_ffn_gate.astype(BF16), 2, FP).reshape(L * D, FP)
    wu_b = _pad_to(w_ffn_up.astype(BF16), 2, FP).reshape(L * D, FP)
    wd_b = _pad_to(w_ffn_down.astype(BF16), 1, FP).reshape(L * FP, D)
    cw_l = _pad_to(conv_w.reshape(L, KW, CQ), 1, 32).reshape(L * 32, CQ)
    w_in_g, w_sq_g, wg_g, wu_g, wd_g, cw_g = _gather_shards([w_in_b, w_sq_b, wg_b, wu_b, wd_b, cw_l])
    conv_w_full = cw_g.reshape(NSHARD, L, 32, CQ).transpose(1, 2, 0, 3).reshape(L, 32, D)[:, :KW]

    x2d = x.reshape(T, D)
    tgt = loss_target.reshape(T, D)
    row = lambda a, l: a[l].reshape(1, -1)

    saved = []
    xc = x2d
    for l in range(L):
        ws_b = w_spatial[l].astype(BF16)
        bs_b = jnp.repeat(b_spatial[l].T, D // G, axis=1)
        cw_s = _strips(conv_w_full[l])
        h, proj = _mix_in_fwd(xc, row(norm_mix, l), w_in_g, l, tm)
        c1h, rstd_c, c3, ya = _conv_fwd(proj, cw_s, row(conv_b, l), row(conv_ln_g, l), row(conv_ln_b, l),
                                        w_sq_g, l, S, tm, rb)
        mixed, gated, yb, merged, x1 = _sgu_merge_fwd(proj, ya, xc, row(sgu_ln_g, l), row(sgu_ln_b, l), ws_b, bs_b,
                                                      row(gate_bias, l), w_sq_g, l, tm2)
        h2, gt, up, act, x2 = _ffn_fwd(x1, row(norm_ffn, l), wg_g, wu_g, wd_g, l, tm2)
        saved.append(dict(x=xc, h=h, proj=proj, c1h=c1h, rstd_c=rstd_c, c3=c3, ya=ya, mixed=mixed, gated=gated,
                          yb=yb, merged=merged, x1=x1, h2=h2, gt=gt, up=up, act=act, ws_b=ws_b, cw=conv_w_full[l]))
        xc = x2

    dx, loss_part, d_norm_final = _loss_head(xc, norm_final.reshape(1, D), tgt, tm)
    loss = lax.psum(loss_part[0, 0], ("x", "y", "c"))

    g_in = g_co = g_so = g_o = g_g = g_u = g_d = None
    small = [None] * L
    tt = tm
    nt = T // tt
    for l in reversed(range(L)):
        sv = saved[l]
        dx1, dgt, dup, d_norm_ffn = _ffn_bwd(dx, sv["x1"], sv["gt"], sv["up"], row(norm_ffn, l), wg_g, wu_g, wd_g, l, tm2)
        g_g =_tn_matmul("grad_w_gate", sv["h2"], (tt, D), lambda j, t: (t, 0), dgt, (tt, FP), lambda j, t: (t, j),
                         (L, NSHARD, D, FP), (None, None, D, FP), lambda j, t: (l, j, 0, 0), NSHARD, nt, g_g)
        g_u = _tn_matmul("grad_w_up", sv["h2"], (tt, D), lambda j, t: (t, 0), dup, (tt, FP), lambda j, t: (t, j),
                         (L, NSHARD, D, FP), (None, None, D, FP), lambda j, t: (l, j, 0, 0), NSHARD, nt, g_u)
        g_d = _tn_matmul("grad_w_down", sv["act"], (tt, FP), lambda j, t: (t, j), dx, (tt, D), lambda j, t: (t, 0),
                         (L, NSHARD, FP, D), (None, None, FP, D), lambda j, t: (l, j, 0, 0), NSHARD, nt, g_d)
        wst_b = jnp.swapaxes(sv["ws_b"], 1, 2)
        dya, dyb, dp3, d_gate_bias, d_sgu_g, d_sgu_b, d_bs, d_ws = _merge_sgu_bwd(
            dx1, sv["proj"], sv["ya"], sv["yb"], sv["mixed"], row(sgu_ln_g, l), row(sgu_ln_b, l), sv["ws_b"], wst_b,
            row(gate_bias, l), w_sq_g, l, tm2)
        sq_args = ((tt, D), lambda j, t: (t, 0))
        sq_out = ((L, D, D), (None, D, D), lambda j, t: (l, 0, 0), 1, nt)
        g_o = _tn_matmul("grad_w_o", sv["merged"], *sq_args, dx1, *sq_args, *sq_out, g_o)
        g_so = _tn_matmul("grad_w_sgu_out", sv["gated"], *sq_args, dyb, *sq_args, *sq_out, g_so)
        g_co = _tn_matmul("grad_w_conv_out", sv["c3"], *sq_args, dya, *sq_args, *sq_out, g_co)
        dc1, d_cln_g, d_cln_b, d_conv_b = _conv_ln_bwd(dya, sv["c1h"], sv["rstd_c"], row(conv_ln_g, l),
                                                       row(conv_ln_b, l), w_sq_g, l, tm)
        dp3, d_cw_s = _conv_bwd(dc1, sv["proj"], dp3, _strips(sv["cw"][::-1]), S, tm, rb)
        g_in = _tn_matmul("grad_w_in", sv["h"], (tt, D), lambda j, t: (t, 0), dp3, (None, tt, CN),
                          lambda j, t: ((j // 4 + 2) % 3, t, j % 4),
                          (L, NSHARD, D, CS), (None, None, D, CN), lambda j, t: (l, j // 3, 0, j % 3), 12, nt, g_in)
        dx, d_norm_mix = _mix_in_bwd(dx1, dp3, sv["x"], row(norm_mix, l), w_in_g, l, tm)
        d_cw = d_cw_s.transpose(1, 0, 2).reshape(32, D)
        small[l] = [d_norm_mix, d_gate_bias.reshape(2, D), d_cw, d_conv_b, d_cln_g, d_cln_b, d_sgu_g, d_sgu_b,
                    d_ws.reshape(G * CH * CH // D, D), d_bs.reshape(G * CH // D, D), d_norm_ffn]
    grad_x = dx.reshape(BL, S, D)

    big = [g_in.reshape(L * NSHARD, D, CS), g_co.reshape(L * NSHARD, DQ, D), g_so.reshape(L * NSHARD, DQ, D),
           g_o.reshape(L * NSHARD, DQ, D), g_g.reshape(L * NSHARD, D, FP), g_u.reshape(L * NSHARD, D, FP),
           g_d.reshape(L * NSHARD, FP, D)]
    names = ["w_in", "w_conv_out", "w_sgu_out", "w_o", "w_ffn_gate", "w_ffn_up", "w_ffn_down"]
    from_sib = _send_sibling_halves(big)
    chip_sum = []
    for nm, g, r in zip(names, big, from_sib):
        h = g.shape[1] // 2
        p = _add_halves("presum_" + nm, g, r, c_idx, _row_tile(h, 256))
        chip_sum.append(p.reshape(L, NSHARD, h, g.shape[2]))
    from_chips = _send_chip_shards(chip_sum)
    halves = []
    for nm, p, r in zip(names, chip_sum, from_chips):
        halves.append(_add_shards("shardsum_" + nm, p, r, s_idx, _row_tile(p.shape[2], 256)))
    g_full = _join_halves(halves)
    g_w_in, g_w_co, g_w_so, g_w_o, g_w_g, g_w_u, g_w_d = g_full
    g_w_g = g_w_g[:, :, :FS]
    g_w_u = g_w_u[:, :, :FS]
    g_w_d = g_w_d[:, :FS, :]

    pieces = [p for l in range(L) for p in small[l]] + [d_norm_final]
    packed = jnp.concatenate(pieces, axis=0)
    n_rows = packed.shape[0]
    n_pad = -(-n_rows // 8) * 8
    packed = _pad_to(packed, 0, n_pad)
    summed = _sum_slots(_exchange_all(packed), 8)
    off = 0
    sg = []
    for l in range(L):
        cur = []
        for p in small[l]:
            cur.append(summed[off:off + p.shape[0]])
            off += p.shape[0]
        sg.append(cur)
    g_norm_final = summed[off]

    def per_layer(k, shape):
        return jnp.stack([sg[l][k] for l in range(L)]).reshape(shape)

    g_norm_mix = per_layer(0, (L, D))
    g_gate_bias = per_layer(1, (L, 2 * D))
    g_conv_w_full = jnp.stack([sg[l][2][:KW] for l in range(L)])
    g_conv_w = lax.dynamic_slice_in_dim(g_conv_w_full, (2 * mx + my) * CQ, CQ, axis=2).reshape(L, KW, 1, CQ)
    g_conv_b = per_layer(3, (L, D))
    g_conv_ln_g = per_layer(4, (L, D))
    g_conv_ln_b = per_layer(5, (L, D))
    g_sgu_ln_g = per_layer(6, (L, D))
    g_sgu_ln_b = per_layer(7, (L, D))
    g_w_spatial = per_layer(8, (L, G, CH, CH))
    g_b_spatial = per_layer(9, (L, G, CH))
    g_norm_ffn = per_layer(10, (L, D))

    grads = [g_norm_mix, g_w_in, g_gate_bias, g_conv_w, g_conv_b, g_conv_ln_g, g_conv_ln_b, g_w_co, g_sgu_ln_g,
             g_sgu_ln_b, g_w_spatial, g_b_spatial, g_w_so, g_w_o, g_norm_ffn, g_w_g, g_w_u, g_w_d, g_norm_final]
    weights = [norm_mix, w_in, gate_bias, conv_w, conv_b, conv_ln_g, conv_ln_b, w_conv_out, sgu_ln_g, sgu_ln_b,
               w_spatial, b_spatial, w_sgu_out, w_o, norm_ffn, w_ffn_gate, w_ffn_up, w_ffn_down, norm_final]
    ms = [m_norm_mix, m_w_in, m_gate_bias, m_conv_w, m_conv_b, m_conv_ln_g, m_conv_ln_b, m_w_conv_out, m_sgu_ln_g,
          m_sgu_ln_b, m_w_spatial, m_b_spatial, m_w_sgu_out, m_w_o, m_norm_ffn, m_w_ffn_gate, m_w_ffn_up,
          m_w_ffn_down, m_norm_final]
    vs = [v_norm_mix, v_w_in, v_gate_bias, v_conv_w, v_conv_b, v_conv_ln_g, v_conv_ln_b, v_w_conv_out, v_sgu_ln_g,
          v_sgu_ln_b, v_w_spatial, v_b_spatial, v_w_sgu_out, v_w_o, v_norm_ffn, v_w_ffn_gate, v_w_ffn_up,
          v_w_ffn_down, v_norm_final]

    big_idx = [1, 7, 12, 13, 15, 16, 17]
    deltas, new_m, new_v = [None] * 19, [None] * 19, [None] * 19
    for k in big_idx:
        shp = weights[k].shape
        r2 = (shp[0] * shp[1], shp[2])
        d_, m_, v_ = _adamw("adamw_" + str(k), weights[k].reshape(r2), grads[k].reshape(r2), ms[k].reshape(r2),
                            vs[k].reshape(r2), _row_tile(r2[0], 256))
        deltas[k], new_m[k], new_v[k] = d_.reshape(shp), m_.reshape(shp), v_.reshape(shp)
    small_idx = [k for k in range(19) if k not in big_idx]

    def pack(arrs):
        flat = jnp.concatenate([arrs[k].reshape(-1) for k in small_idx])
        rows = -(-flat.shape[0] // (8 * LANES)) * 8
        return _pad_to(flat, 0, rows * LANES).reshape(rows, LANES)

    pw, pg, pm, pv = pack(weights), pack(grads), pack(ms), pack(vs)
    d_, m_, v_ = _adamw("adamw_small", pw, pg, pm, pv, _row_tile(pw.shape[0], 512))
    off = 0
    for k in small_idx:
        n_el = weights[k].size
        shp = weights[k].shape
        deltas[k] = d_.reshape(-1)[off:off + n_el].reshape(shp)
        new_m[k] = m_.reshape(-1)[off:off + n_el].reshape(shp)
        new_v[k] = v_.reshape(-1)[off:off + n_el].reshape(shp)
        off += n_el

    return (loss, grad_x, *grads, *deltas, *new_m, *new_v)
```

```python
import functools

import jax
import jax.numpy as jnp
from jax import lax
from jax.experimental import pallas as pl
from jax.experimental.pallas import tpu as pltpu

F32 = jnp.float32
BF16 = jnp.bfloat16
EPS = 1e-6
ADAM_LR = 0.001
ADAM_B1 = 0.9
ADAM_B2 = 0.999
ADAM_EPS = 1e-08
ADAM_WD = 0.01
ADAM_STEP = 10

NSHARD = 4
LANES = 128
HALO = 16
VMEM_LIMIT = 60 * 1024 * 1024
MESH_ID = pl.DeviceIdType.MESH


def _dot(a, b):
    return jnp.dot(a, b, preferred_element_type=F32)


def _dot_nt(a, b):
    return lax.dot_general(a, b, (((1,), (1,)), ((), ())), preferred_element_type=F32)


def _dot_tn(a, b):
    return lax.dot_general(a, b, (((0,), (0,)), ((), ())), preferred_element_type=F32)


def _sig(z):
    return 1.0 / (1.0 + jnp.exp(-z))


def _res(shape, imap=None):
    nd = len(shape)
    if imap is None:
        imap = lambda *_: (0,) * nd
    return pl.BlockSpec(shape, imap, pipeline_mode=pl.Buffered(1))


def _cparams(sem):
    return pltpu.CompilerParams(dimension_semantics=sem, vmem_limit_bytes=VMEM_LIMIT)


def _sds(shape, dtype):
    return jax.ShapeDtypeStruct(shape, dtype)


def _mix_in_fwd(x2d, g_mix, w_in_g, layer, tm):
    T, D = x2d.shape
    CS = w_in_g.shape[2]
    CN = CS // 3

    def body(x_ref, g_ref, w_ref, h_ref, p_ref):
        x = x_ref[...]
        rstd = lax.rsqrt(jnp.mean(x * x, axis=-1, keepdims=True) + EPS)
        h = (x * rstd * g_ref[...]).astype(BF16)
        h_ref[...] = h
        for s in range(NSHARD):
            for j in range(3):
                c0 = s * CS + j * CN
                p_ref[:, c0:c0 + CN] = _dot(h, w_ref[s, :, j * CN:(j + 1) * CN]).astype(BF16)

    return pl.pallas_call(
        body, name="mix_in_fwd", grid=(T // tm,),
        in_specs=[pl.BlockSpec((tm, D), lambda i: (i, 0)), _res((1, D)),
                  _res((NSHARD, D, CS), lambda i: (0, layer, 0))],
        out_specs=[pl.BlockSpec((tm, D), lambda i: (i, 0)), pl.BlockSpec((tm, NSHARD * CS), lambda i: (i, 0))],
        out_shape=[_sds((T, D), BF16), _sds((T, NSHARD * CS), BF16)],
        compiler_params=_cparams(("parallel",)),
    )(x2d, g_mix, w_in_g)


def _halo_maps(tm, n_rows):
    nb = tm // HALO
    last = n_rows // HALO - 1
    prev = lambda i: (jnp.maximum(i * nb - 1, 0), 0)
    nxt = lambda i: (jnp.minimum((i + 1) * nb, last), 0)
    return prev, nxt


def _dwconv(pad_ref, w_ref, out_ref, n_strips, tm, kw, rb):
    off = HALO - (kw - 1) // 2

    def strip(cs, carry):
        for r0 in range(0, tm, rb):
            acc = jnp.zeros((rb, LANES), F32)
            for k in range(kw):
                r = r0 + off + k
                acc = acc + w_ref[cs, k:k + 1, :] * pad_ref[cs, r:r + rb, :]
            out_ref[cs, r0:r0 + rb, :] = acc
        return carry

    lax.fori_loop(0, n_strips, strip, 0)


def _fill_c0_pad(pad_ref, pa_ref, pprev_ref, pnext_ref, D, tm, first, last):
    for cs in range(D // LANES):
        lo, hi = cs * LANES, (cs + 1) * LANES

        def c0_of(ref):
            return ref[:, lo:hi].astype(F32) * _sig(ref[:, D + lo:D + hi].astype(F32))

        pad_ref[cs, HALO:HALO + tm, :] = c0_of(pa_ref)
        pad_ref[cs, 0:HALO, :] = jnp.where(first, 0.0, c0_of(pprev_ref))
        pad_ref[cs, HALO + tm:HALO + tm + HALO, :] = jnp.where(last, 0.0, c0_of(pnext_ref))


def _conv_fwd(proj, conv_w_s, conv_b, ln_g, ln_b, w_sq_g, layer, seq, tm, rb):
    T = proj.shape[0]
    D = conv_b.shape[1]
    DQ = D // NSHARD
    NSTR = D // LANES
    KW = 31
    tps = seq // tm
    prev, nxt = _halo_maps(tm, T)

    def body(pa_ref, pprev_ref, pnext_ref, w_ref, b_ref, g_ref, be_ref, wco_ref,
             c1h_ref, rstd_ref, c3_ref, ya_ref, pad_ref, c1s_ref):
        i = pl.program_id(0)
        first = (i % tps) == 0
        last = (i % tps) == tps - 1
        _fill_c0_pad(pad_ref, pa_ref, pprev_ref, pnext_ref, D, tm, first, last)
        _dwconv(pad_ref, w_ref, c1s_ref, NSTR, tm, KW, rb)
        c1 = jnp.concatenate([c1s_ref[cs] for cs in range(NSTR)], axis=1) + b_ref[...]
        mu = jnp.mean(c1, axis=-1, keepdims=True)
        cc = c1 - mu
        rstd = lax.rsqrt(jnp.mean(cc * cc, axis=-1, keepdims=True) + EPS)
        c1h = cc * rstd
        c1h_ref[...] = c1h.astype(BF16)
        rstd_ref[...] = rstd
        c2 = c1h * g_ref[...] + be_ref[...]
        c3 = (c2 * _sig(c2)).astype(BF16)
        c3_ref[...] = c3
        ya_ref[...] = _dot(c3, wco_ref[...].reshape(D, D)).astype(BF16)

    row = lambda i: (i, 0)
    return pl.pallas_call(
        body, name="conv_fwd", grid=(T // tm,),
        in_specs=[pl.BlockSpec((tm, 2 * D), row), pl.BlockSpec((HALO, 2 * D), prev), pl.BlockSpec((HALO, 2 * D), nxt),
                  _res((NSTR, 32, LANES)), _res((1, D)), _res((1, D)), _res((1, D)),
                  _res((NSHARD, DQ, D), lambda i: (0, layer * 3 + 0, 0))],
        out_specs=[pl.BlockSpec((tm, D), row), pl.BlockSpec((tm, 1), row), pl.BlockSpec((tm, D), row),
                   pl.BlockSpec((tm, D), row)],
        out_shape=[_sds((T, D), BF16), _sds((T, 1), F32), _sds((T, D), BF16), _sds((T, D), BF16)],
        scratch_shapes=[pltpu.VMEM((NSTR, tm + 2 * HALO, LANES), F32), pltpu.VMEM((NSTR, tm, LANES), F32)],
        compiler_params=_cparams(("parallel",)),
    )(proj, proj, proj, conv_w_s, conv_b, ln_g, ln_b, w_sq_g)


def _sgu_merge_fwd(proj, ya, x2d, ln_g, ln_b, ws_b, bs_b, gate_bias, w_sq_g, layer, tm):
    T, D = x2d.shape
    DQ = D // NSHARD
    G, CH, _ = ws_b.shape
    GD = D // G

    def body(puv_ref, pg_ref, ya_ref, x_ref, g_ref, be_ref, ws_ref, bsb_ref, gb_ref, wso_ref, wo_ref,
             mixed_ref, gated_ref, yb_ref, merged_ref, x1_ref, mix_scr):
        u = puv_ref[:, :D].astype(F32)
        v = puv_ref[:, D:].astype(F32)
        mu = jnp.mean(v, axis=-1, keepdims=True)
        vc = v - mu
        rstd = lax.rsqrt(jnp.mean(vc * vc, axis=-1, keepdims=True) + EPS)
        vn = (vc * rstd * g_ref[...] + be_ref[...]).astype(BF16)
        for ch in range(tm // CH):
            for g in range(G):
                blk = vn[ch * CH:(ch + 1) * CH, g * GD:(g + 1) * GD]
                mix_scr[ch * CH:(ch + 1) * CH, g * GD:(g + 1) * GD] = (
                    _dot(ws_ref[g], blk) + bsb_ref[:, g * GD:(g + 1) * GD])
        mixed = mix_scr[...]
        mixed_ref[...] = mixed.astype(BF16)
        gated = (u * mixed).astype(BF16)
        gated_ref[...] = gated
        yb = _dot(gated, wso_ref[...].reshape(D, D))
        yb_ref[...] = yb.astype(BF16)
        sa = _sig(pg_ref[:, :D].astype(F32) + gb_ref[:, :D])
        sb = _sig(pg_ref[:, D:].astype(F32) + gb_ref[:, D:])
        merged = (sa * ya_ref[...].astype(F32) + sb * yb).astype(BF16)
        merged_ref[...] = merged
        x1_ref[...] = x_ref[...] + _dot(merged, wo_ref[...].reshape(D, D))

    row = lambda i: (i, 0)
    return pl.pallas_call(
        body, name="sgu_merge_fwd", grid=(T // tm,),
        in_specs=[pl.BlockSpec((tm, 2 * D), lambda i: (i, 1)), pl.BlockSpec((tm, 2 * D), lambda i: (i, 2)),
                  pl.BlockSpec((tm, D), row), pl.BlockSpec((tm, D), row),
                  _res((1, D)), _res((1, D)), _res((G, CH, CH)), _res((CH, D)), _res((1, 2 * D)),
                  _res((NSHARD, DQ, D), lambda i: (0, layer * 3 + 1, 0)),
                  _res((NSHARD, DQ, D), lambda i: (0, layer * 3 + 2, 0))],
        out_specs=[pl.BlockSpec((tm, D), row)] * 5,
        out_shape=[_sds((T, D), BF16)] * 4 + [_sds((T, D), F32)],
        scratch_shapes=[pltpu.VMEM((tm, D), F32)],
        compiler_params=_cparams(("parallel",)),
    )(proj, proj, ya, x2d, ln_g, ln_b, ws_b, bs_b, gate_bias, w_sq_g, w_sq_g)


def _ffn_fwd(x1, g_ffn, wg_g, wu_g, wd_g, layer, tm):
    T, D = x1.shape
    FP = wg_g.shape[2]
    F = NSHARD * FP

    def body(x_ref, g_ref, wg_ref, wu_ref, wd_ref, h2_ref, gt_ref, up_ref, act_ref, x2_ref):
        x = x_ref[...]
        rstd = lax.rsqrt(jnp.mean(x * x, axis=-1, keepdims=True) + EPS)
        h2 = (x * rstd * g_ref[...]).astype(BF16)
        h2_ref[...] = h2
        acc = x
        for s in range(NSHARD):
            gt = _dot(h2, wg_ref[s])
            up = _dot(h2, wu_ref[s])
            gt_ref[:, s * FP:(s + 1) * FP] = gt.astype(BF16)
            up_ref[:, s * FP:(s + 1) * FP] = up.astype(BF16)
            act = (gt * _sig(gt) * up).astype(BF16)
            act_ref[:, s * FP:(s + 1) * FP] = act
            acc = acc + _dot(act, wd_ref[s])
        x2_ref[...] = acc

    row = lambda i: (i, 0)
    return pl.pallas_call(
        body, name="ffn_fwd", grid=(T // tm,),
        in_specs=[pl.BlockSpec((tm, D), row), _res((1, D)),
                  _res((NSHARD, D, FP), lambda i: (0, layer, 0)), _res((NSHARD, D, FP), lambda i: (0, layer, 0)),
                  _res((NSHARD, FP, D), lambda i: (0, layer, 0))],
        out_specs=[pl.BlockSpec((tm, D), row), pl.BlockSpec((tm, F), row), pl.BlockSpec((tm, F), row),
                   pl.BlockSpec((tm, F), row), pl.BlockSpec((tm, D), row)],
        out_shape=[_sds((T, D), BF16), _sds((T, F), BF16), _sds((T, F), BF16), _sds((T, F), BF16), _sds((T, D), F32)],
        compiler_params=_cparams(("parallel",)),
    )(x1, g_ffn, wg_g, wu_g, wd_g)


def _loss_head(xf, g_fin, target, tm):
    T, D = xf.shape
    n = T // tm

    def body(x_ref, g_ref, t_ref, dx_ref, loss_ref, dg_ref, acc_ref):
        i = pl.program_id(0)

        @pl.when(i == 0)
        def _():
            acc_ref[...] = jnp.zeros_like(acc_ref)
            dg_ref[...] = jnp.zeros_like(dg_ref)

        x = x_ref[...]
        g = g_ref[...]
        rstd = lax.rsqrt(jnp.mean(x * x, axis=-1, keepdims=True) + EPS)
        xh = x * rstd
        diff = xh * g - t_ref[...]
        acc_ref[...] += jnp.sum(diff * diff, axis=0, keepdims=True)
        dy = diff * (1.0 / D)
        dg_ref[...] += jnp.sum(dy * xh, axis=0, keepdims=True)
        dxh = dy * g
        dx_ref[...] = rstd * (dxh - xh * jnp.mean(dxh * xh, axis=-1, keepdims=True))

        @pl.when(i == n - 1)
        def _():
            tot = jnp.sum(acc_ref[...], axis=-1, keepdims=True) * (0.5 / D)
            loss_ref[...] = jnp.broadcast_to(tot, loss_ref.shape)

    row = lambda i: (i, 0)
    return pl.pallas_call(
        body, name="loss_head", grid=(n,),
        in_specs=[pl.BlockSpec((tm, D), row), _res((1, D)), pl.BlockSpec((tm, D), row)],
        out_specs=[pl.BlockSpec((tm, D), row), pl.BlockSpec((1, LANES), lambda i: (0, 0)),
                   pl.BlockSpec((1, D), lambda i: (0, 0))],
        out_shape=[_sds((T, D), F32), _sds((1, LANES), F32), _sds((1, D), F32)],
        scratch_shapes=[pltpu.VMEM((1, D), F32)],
        compiler_params=_cparams(("arbitrary",)),
    )(xf, g_fin, target)


def _ffn_bwd(dx2, x1, gt, up, g_ffn, wg_g, wu_g, wd_g, layer, tm):
    T, D = x1.shape
    FP = wg_g.shape[2]
    F = NSHARD * FP

    def body(dx2_ref, x1_ref, gt_ref, up_ref, g_ref, wg_ref, wu_ref, wd_ref, dx1_ref, dgt_ref, dup_ref, dg_ref):
        i = pl.program_id(0)

        @pl.when(i == 0)
        def _():
            dg_ref[...] = jnp.zeros_like(dg_ref)

        dx2 = dx2_ref[...]
        dx2b = dx2.astype(BF16)
        dh2 = jnp.zeros((tm, D), F32)
        for s in range(NSHARD):
            dact = _dot_nt(dx2b, wd_ref[s])
            g = gt_ref[:, s * FP:(s + 1) * FP].astype(F32)
            u = up_ref[:, s * FP:(s + 1) * FP].astype(F32)
            sg = _sig(g)
            dup = (dact * (g * sg)).astype(BF16)
            dgt = (dact * u * (sg * (1.0 + g * (1.0 - sg)))).astype(BF16)
            dgt_ref[:, s * FP:(s + 1) * FP] = dgt
            dup_ref[:, s * FP:(s + 1) * FP] = dup
            dh2 = dh2 + _dot_nt(dgt, wg_ref[s]) + _dot_nt(dup, wu_ref[s])
        x = x1_ref[...]
        rstd = lax.rsqrt(jnp.mean(x * x, axis=-1, keepdims=True) + EPS)
        xh = x * rstd
        dg_ref[...] += jnp.sum(dh2 * xh, axis=0, keepdims=True)
        dxh = dh2 * g_ref[...]
        dx1_ref[...] = dx2 + rstd * (dxh - xh * jnp.mean(dxh * xh, axis=-1, keepdims=True))

    row = lambda i: (i, 0)
    return pl.pallas_call(
        body, name="ffn_bwd", grid=(T // tm,),
        in_specs=[pl.BlockSpec((tm, D), row), pl.BlockSpec((tm, D), row), pl.BlockSpec((tm, F), row),
                  pl.BlockSpec((tm, F), row), _res((1, D)),
                  _res((NSHARD, D, FP), lambda i: (0, layer, 0)), _res((NSHARD, D, FP), lambda i: (0, layer, 0)),
                  _res((NSHARD, FP, D), lambda i: (0, layer, 0))],
        out_specs=[pl.BlockSpec((tm, D), row), pl.BlockSpec((tm, F), row), pl.BlockSpec((tm, F), row),
                   pl.BlockSpec((1, D), lambda i: (0, 0))],
        out_shape=[_sds((T, D), F32), _sds((T, F), BF16), _sds((T, F), BF16), _sds((1, D), F32)],
        compiler_params=_cparams(("arbitrary",)),
    )(dx2, x1, gt, up, g_ffn, wg_g, wu_g, wd_g)


def _merge_sgu_bwd(dx1, proj, ya, yb, mixed, ln_g, ln_b, ws_b, wst_b, gate_bias, w_sq_g, layer, tm):
    T, D = dx1.shape
    DQ = D // NSHARD
    G, CH, _ = ws_b.shape
    GD = D // G

    def body(dx1_ref, puv_ref, pg_ref, ya_ref, yb_ref, mixed_ref, g_ref, be_ref, ws_ref, wst_ref, gb_ref,
             wso_ref, wo_ref, dya_ref, dyb_ref, dp_ref, dgb_ref, dlg_ref, dlb_ref, dbs_ref, dws_ref,
             dvn_scr, dbs_scr):
        i = pl.program_id(0)

        @pl.when(i == 0)
        def _():
            for r in (dgb_ref, dlg_ref, dlb_ref, dws_ref, dbs_scr):
                r[...] = jnp.zeros_like(r)

        dmerged = _dot_nt(dx1_ref[...].astype(BF16), wo_ref[...].reshape(D, D))
        sa = _sig(pg_ref[:, :D].astype(F32) + gb_ref[:, :D])
        sb = _sig(pg_ref[:, D:].astype(F32) + gb_ref[:, D:])
        dya = (dmerged * sa).astype(BF16)
        dyb = (dmerged * sb).astype(BF16)
        dya_ref[...] = dya
        dyb_ref[...] = dyb
        dga = dmerged * ya_ref[...].astype(F32) * (sa * (1.0 - sa))
        dgb = dmerged * yb_ref[...].astype(F32) * (sb * (1.0 - sb))
        dp_ref[:, 4 * D:5 * D] = dga.astype(BF16)
        dp_ref[:, 5 * D:6 * D] = dgb.astype(BF16)
        dgb_ref[:, :D] += jnp.sum(dga, axis=0, keepdims=True)
        dgb_ref[:, D:] += jnp.sum(dgb, axis=0, keepdims=True)

        dgated = _dot_nt(dyb, wso_ref[...].reshape(D, D))
        u = puv_ref[:, :D].astype(F32)
        v = puv_ref[:, D:].astype(F32)
        dp_ref[:, 2 * D:3 * D] = (dgated * mixed_ref[...].astype(F32)).astype(BF16)
        dmixed = dgated * u
        mu = jnp.mean(v, axis=-1, keepdims=True)
        vc = v - mu
        rstd = lax.rsqrt(jnp.mean(vc * vc, axis=-1, keepdims=True) + EPS)
        vh = vc * rstd
        vn = (vh * g_ref[...] + be_ref[...]).astype(BF16)
        dmb = dmixed.astype(BF16)
        bs_part = jnp.zeros((CH, D), F32)
        for ch in range(tm // CH):
            rows = slice(ch * CH, (ch + 1) * CH)
            bs_part = bs_part + dmixed[rows, :]
            for g in range(G):
                cols = slice(g * GD, (g + 1) * GD)
                dws_ref[g] += _dot_nt(dmb[rows, cols], vn[rows, cols])
                dvn_scr[rows, cols] = _dot(wst_ref[g], dmb[rows, cols])
        dbs_scr[...] += bs_part
        dvn = dvn_scr[...]
        dlg_ref[...] += jnp.sum(dvn * vh, axis=0, keepdims=True)
        dlb_ref[...] += jnp.sum(dvn, axis=0, keepdims=True)
        dxh = dvn * g_ref[...]
        dv = rstd * (dxh - jnp.mean(dxh, axis=-1, keepdims=True) - vh * jnp.mean(dxh * vh, axis=-1, keepdims=True))
        dp_ref[:, 3 * D:4 * D] = dv.astype(BF16)

        @pl.when(i == pl.num_programs(0) - 1)
        def _():
            for g in range(G):
                blk = dbs_scr[:, g * GD:(g + 1) * GD]
                if GD != CH:
                    blk = jnp.concatenate([blk, jnp.zeros((CH, CH - GD), F32)], axis=1)
                dbs_ref[:, g * CH:(g + 1) * CH] = jnp.sum(blk.T, axis=0, keepdims=True)

    row = lambda i: (i, 0)
    fixed2 = lambda i: (0, 0)
    return pl.pallas_call(
        body, name="merge_sgu_bwd", grid=(T // tm,),
        in_specs=[pl.BlockSpec((tm, D), row), pl.BlockSpec((tm, 2 * D), lambda i: (i, 1)),
                  pl.BlockSpec((tm, 2 * D), lambda i: (i, 2)), pl.BlockSpec((tm, D), row), pl.BlockSpec((tm, D), row),
                  pl.BlockSpec((tm, D), row), _res((1, D)), _res((1, D)), _res((G, CH, CH)), _res((G, CH, CH)),
                  _res((1, 2 * D)),
                  _res((NSHARD, DQ, D), lambda i: (0, layer * 3 + 1, 0)),
                  _res((NSHARD, DQ, D), lambda i: (0, layer * 3 + 2, 0))],
        out_specs=[pl.BlockSpec((tm, D), row), pl.BlockSpec((tm, D), row),
                   pl.BlockSpec((tm, 6 * D), row),
                   pl.BlockSpec((1, 2 * D), fixed2), pl.BlockSpec((1, D), fixed2), pl.BlockSpec((1, D), fixed2),
                   pl.BlockSpec((1, G * CH), fixed2), pl.BlockSpec((G, CH, CH), lambda i: (0, 0, 0))],
        out_shape=[_sds((T, D), BF16), _sds((T, D), BF16), _sds((T, 6 * D), BF16),
                   _sds((1, 2 * D), F32), _sds((1, D), F32), _sds((1, D), F32), _sds((1, G * CH), F32),
                   _sds((G, CH, CH), F32)],
        scratch_shapes=[pltpu.VMEM((tm, D), F32), pltpu.VMEM((CH, D), F32)],
        compiler_params=_cparams(("arbitrary",)),
    )(dx1, proj, proj, ya, yb, mixed, ln_g, ln_b, ws_b, wst_b, gate_bias, w_sq_g, w_sq_g)


def _conv_ln_bwd(dya, c1h, rstd_c, ln_g, ln_b, w_sq_g, layer, tm):
    T, D = dya.shape
    DQ = D // NSHARD

    def body(dya_ref, c1h_ref, rstd_ref, g_ref, be_ref, wco_ref, dc1_ref, dlg_ref, dlb_ref, dcb_ref):
        i = pl.program_id(0)

        @pl.when(i == 0)
        def _():
            for r in (dlg_ref, dlb_ref, dcb_ref):
                r[...] = jnp.zeros_like(r)

        dc3 = _dot_nt(dya_ref[...], wco_ref[...].reshape(D, D))
        c1h = c1h_ref[...].astype(F32)
        c2 = c1h * g_ref[...] + be_ref[...]
        sg = _sig(c2)
        dc2 = dc3 * (sg * (1.0 + c2 * (1.0 - sg)))
        dlg_ref[...] += jnp.sum(dc2 * c1h, axis=0, keepdims=True)
        dlb_ref[...] += jnp.sum(dc2, axis=0, keepdims=True)
        dxh = dc2 * g_ref[...]
        dc1 = rstd_ref[...] * (dxh - jnp.mean(dxh, axis=-1, keepdims=True)
                               - c1h * jnp.mean(dxh * c1h, axis=-1, keepdims=True))
        dc1_ref[...] = dc1
        dcb_ref[...] += jnp.sum(dc1, axis=0, keepdims=True)

    row = lambda i: (i, 0)
    fixed2 = lambda i: (0, 0)
    return pl.pallas_call(
        body, name="conv_ln_bwd", grid=(T // tm,),
        in_specs=[pl.BlockSpec((tm, D), row), pl.BlockSpec((tm, D), row), pl.BlockSpec((tm, 1), row),
                  _res((1, D)), _res((1, D)), _res((NSHARD, DQ, D), lambda i: (0, layer * 3 + 0, 0))],
        out_specs=[pl.BlockSpec((tm, D), row), pl.BlockSpec((1, D), fixed2), pl.BlockSpec((1, D), fixed2),
                   pl.BlockSpec((1, D), fixed2)],
        out_shape=[_sds((T, D), F32), _sds((1, D), F32), _sds((1, D), F32), _sds((1, D), F32)],
        compiler_params=_cparams(("arbitrary",)),
    )(dya, c1h, rstd_c, ln_g, ln_b, w_sq_g)


def _conv_bwd(dc1, proj, dp3, conv_wf_s, seq, tm, rb):
    T, D = dc1.shape
    NSTR = D // LANES
    KW = 31
    PADK = (KW - 1) // 2
    tps = seq // tm
    prev, nxt = _halo_maps(tm, T)
    n = T // tm

    def body(dc_ref, dcprev_ref, dcnext_ref, pa_ref, pprev_ref, pnext_ref, wf_ref, dp_in_ref,
             dp_ref, dw_ref, pad_ref, dpad_ref, dc0_ref, dwacc_ref):
        del dp_in_ref
        i = pl.program_id(0)
        first = (i % tps) == 0
        last = (i % tps) == tps - 1

        @pl.when(i == 0)
        def _():
            dwacc_ref[...] = jnp.zeros_like(dwacc_ref)

        _fill_c0_pad(pad_ref, pa_ref, pprev_ref, pnext_ref, D, tm, first, last)
        for cs in range(NSTR):
            lo, hi = cs * LANES, (cs + 1) * LANES
            dpad_ref[cs, HALO:HALO + tm, :] = dc_ref[:, lo:hi]
            dpad_ref[cs, 0:HALO, :] = jnp.where(first, 0.0, dcprev_ref[:, lo:hi])
            dpad_ref[cs, HALO + tm:HALO + tm + HALO, :] = jnp.where(last, 0.0, dcnext_ref[:, lo:hi])
        _dwconv(dpad_ref, wf_ref, dc0_ref, NSTR, tm, KW, rb)

        def strip(cs, carry):
            for r0 in range(0, tm, rb):
                d = dpad_ref[cs, HALO + r0:HALO + r0 + rb, :]
                for k in range(KW):
                    r = r0 + HALO - PADK + k
                    prod = d * pad_ref[cs, r:r + rb, :]
                    dwacc_ref[cs, k * 8:(k + 1) * 8, :] += jnp.sum(prod.reshape(rb // 8, 8, LANES), axis=0)
            return carry

        lax.fori_loop(0, NSTR, strip, 0)

        for cs in range(NSTR):
            lo, hi = cs * LANES, (cs + 1) * LANES
            av = pa_ref[:, lo:hi].astype(F32)
            sg = _sig(pa_ref[:, D + lo:D + hi].astype(F32))
            dc0 = dc0_ref[cs]
            dp_ref[:, lo:hi] = (dc0 * sg).astype(BF16)
            dp_ref[:, D + lo:D + hi] = (dc0 * av * (sg * (1.0 - sg))).astype(BF16)

        @pl.when(i == n - 1)
        def _():
            for cs in range(NSTR):
                dw_ref[cs] = jnp.sum(dwacc_ref[cs].reshape(32, 8, LANES), axis=1)

    row = lambda i: (i, 0)
    return pl.pallas_call(
        body, name="conv_bwd", grid=(n,),
        in_specs=[pl.BlockSpec((tm, D), row), pl.BlockSpec((HALO, D), prev), pl.BlockSpec((HALO, D), nxt),
                  pl.BlockSpec((tm, 2 * D), row), pl.BlockSpec((HALO, 2 * D), prev), pl.BlockSpec((HALO, 2 * D), nxt),
                  _res((NSTR, 32, LANES)), pl.BlockSpec(memory_space=pl.ANY)],
        out_specs=[pl.BlockSpec((tm, 2 * D), row),
                   pl.BlockSpec((NSTR, 32, LANES), lambda i: (0, 0, 0))],
        out_shape=[_sds(dp3.shape, BF16), _sds((NSTR, 32, LANES), F32)],
        scratch_shapes=[pltpu.VMEM((NSTR, tm + 2 * HALO, LANES), F32), pltpu.VMEM((NSTR, tm + 2 * HALO, LANES), F32),
                        pltpu.VMEM((NSTR, tm, LANES), F32), pltpu.VMEM((NSTR, 32 * 8, LANES), F32)],
        input_output_aliases={7: 0},
        compiler_params=_cparams(("arbitrary",)),
    )(dc1, dc1, dc1, proj, proj, proj, conv_wf_s, dp3)


def _mix_in_bwd(dx1, dp3, x2d, g_mix, w_in_g, layer, tm):
    T, D = x2d.shape
    CS = w_in_g.shape[2]
    CN = CS // 3

    def body(dx1_ref, dp_ref, x_ref, g_ref, w_ref, dx_ref, dg_ref):
        i = pl.program_id(0)

        @pl.when(i == 0)
        def _():
            dg_ref[...] = jnp.zeros_like(dg_ref)

        dh = jnp.zeros((tm, D), F32)
        for j in range(12):
            dh = dh + _dot_nt(dp_ref[:, j * CN:(j + 1) * CN], w_ref[j // 3, :, (j % 3) * CN:(j % 3 + 1) * CN])
        x = x_ref[...]
        rstd = lax.rsqrt(jnp.mean(x * x, axis=-1, keepdims=True) + EPS)
        xh = x * rstd
        dg_ref[...] += jnp.sum(dh * xh, axis=0, keepdims=True)
        dxh = dh * g_ref[...]
        dx_ref[...] = dx1_ref[...] + rstd * (dxh - xh * jnp.mean(dxh * xh, axis=-1, keepdims=True))

    row = lambda i: (i, 0)
    return pl.pallas_call(
        body, name="mix_in_bwd", grid=(T // tm,),
        in_specs=[pl.BlockSpec((tm, D), row), pl.BlockSpec((tm, 6 * D), row),
                  pl.BlockSpec((tm, D), row), _res((1, D)), _res((NSHARD, D, CS), lambda i: (0, layer, 0))],
        out_specs=[pl.BlockSpec((tm, D), row), pl.BlockSpec((1, D), lambda i: (0, 0))],
        out_shape=[_sds((T, D), F32), _sds((1, D), F32)],
        compiler_params=_cparams(("arbitrary",)),
    )(dx1, dp3, x2d, g_mix, w_in_g)


def _tn_matmul(name, a, a_block, a_map, bs, b_block, b_map, out_shape, out_block, out_map, nj, nt, prevs=None):
    kk = [d for d in a_block if d is not None][-1]
    nn = [d for d in b_block if d is not None][-1]
    nb = len(bs)

    def body(*refs):
        a_ref, b_refs = refs[0], refs[1:1 + nb]
        o_refs, acc_refs = refs[-2 * nb:-nb], refs[-nb:]
        t = pl.program_id(1)

        @pl.when(t == 0)
        def _():
            for acc_ref in acc_refs:
                acc_ref[...] = jnp.zeros_like(acc_ref)

        a_t = a_ref[...].astype(BF16)
        for b_ref, acc_ref in zip(b_refs, acc_refs):
            acc_ref[...] += _dot_tn(a_t, b_ref[...].astype(BF16))

        @pl.when(t == nt - 1)
        def _():
            for o_ref, acc_ref in zip(o_refs, acc_refs):
                o_ref[...] = acc_ref[...].astype(o_ref.dtype)

    in_specs = [pl.BlockSpec(a_block, a_map)] + [pl.BlockSpec(b_block, b_map)] * nb
    args = [a] + list(bs)
    aliases = {}
    if prevs is not None:
        in_specs += [pl.BlockSpec(memory_space=pl.ANY)] * nb
        args += list(prevs)
        aliases = {1 + nb + i: i for i in range(nb)}
    return pl.pallas_call(
        body, name=name, grid=(nj, nt), in_specs=in_specs,
        out_specs=[pl.BlockSpec(out_block, out_map)] * nb, out_shape=[_sds(out_shape, BF16)] * nb,
        scratch_shapes=[pltpu.VMEM((kk, nn), F32)] * nb, input_output_aliases=aliases,
        compiler_params=_cparams(("parallel", "arbitrary")),
    )(*args)


def _add_halves(name, g, rbuf, pos, tr):
    L, NS, R, C = g.shape
    LH = L // 2

    def body(pos_ref, g_ref, r_ref, o_ref):
        del pos_ref
        o_ref[...] = (g_ref[...].astype(F32) + r_ref[...].astype(F32)).astype(BF16)

    blk = (None, None, tr, C)
    grid_spec = pltpu.PrefetchScalarGridSpec(
        num_scalar_prefetch=1, grid=(LH, NS, R // tr),
        in_specs=[pl.BlockSpec(blk, lambda l, s, r, pos: (pos[0] * LH + l, s, r, 0)),
                  pl.BlockSpec(blk, lambda l, s, r, pos: (l, s, r, 0))],
        out_specs=pl.BlockSpec(blk, lambda l, s, r, pos: (l, s, r, 0)))
    return pl.pallas_call(body, name=name, grid_spec=grid_spec, out_shape=_sds((LH, NS, R, C), BF16),
                          compiler_params=_cparams(("parallel",) * 3))(pos, g, rbuf)


def _add_shards(name, p, rbuf, pos, tr):
    LH, _, R, C = p.shape

    def body(pos_ref, p_ref, r_ref, o_ref):
        del pos_ref
        acc = p_ref[...].astype(F32)
        for j in range(3):
            acc = acc + r_ref[j].astype(F32)
        o_ref[...] = acc

    grid_spec = pltpu.PrefetchScalarGridSpec(
        num_scalar_prefetch=1, grid=(LH, R // tr),
        in_specs=[pl.BlockSpec((None, None, tr, C), lambda l, r, pos: (l, pos[1], r, 0)),
                  pl.BlockSpec((3, None, tr, C), lambda l, r, pos: (0, l, r, 0))],
        out_specs=pl.BlockSpec((None, tr, C), lambda l, r, pos: (pos[0] * LH + l, r, 0)))
    return pl.pallas_call(body, name=name, grid_spec=grid_spec, out_shape=_sds((2 * LH, R, C), F32),
                          compiler_params=_cparams(("parallel", "parallel")))(pos, p, rbuf)


def _sum_slots(buf, tr):
    NS8, R, C = buf.shape

    def body(b_ref, o_ref):
        acc = b_ref[0]
        for j in range(1, NS8):
            acc = acc + b_ref[j]
        o_ref[...] = acc

    return pl.pallas_call(
        body, name="sum_slots", grid=(R // tr,),
        in_specs=[pl.BlockSpec((NS8, tr, C), lambda i: (0, i, 0))],
        out_specs=pl.BlockSpec((tr, C), lambda i: (i, 0)), out_shape=_sds((R, C), F32),
        compiler_params=_cparams(("parallel",)))(buf)


def _adamw(name, w, g, m, v, tr):
    R, C = w.shape
    c1 = 1.0 - ADAM_B1 ** ADAM_STEP
    c2 = 1.0 - ADAM_B2 ** ADAM_STEP

    def body(w_ref, g_ref, m_ref, v_ref, d_ref, mo_ref, vo_ref):
        g_ = g_ref[...]
        m_ = ADAM_B1 * m_ref[...] + (1.0 - ADAM_B1) * g_
        v_ = ADAM_B2 * v_ref[...] + (1.0 - ADAM_B2) * (g_ * g_)
        mo_ref[...] = m_
        vo_ref[...] = v_
        d_ref[...] = -ADAM_LR * ((m_ / c1) / (jnp.sqrt(v_ / c2) + ADAM_EPS) + ADAM_WD * w_ref[...])

    spec = pl.BlockSpec((tr, C), lambda i: (i, 0))
    return pl.pallas_call(
        body, name=name, grid=(R // tr,), in_specs=[spec] * 4, out_specs=[spec] * 3,
        out_shape=[_sds((R, C), F32)] * 3, compiler_params=_cparams(("parallel",)))(w, g, m, v)


def _row_tile(rows, cap):
    best = rows
    for t in range(8, min(rows, cap) + 1, 8):
        if rows % t == 0:
            best = t
    return best


HBM_SPEC = pl.BlockSpec(memory_space=pltpu.HBM)


def _mesh_pos():
    return lax.axis_index("x"), lax.axis_index("y"), lax.axis_index("c")


def _other_chips(x, y):
    return [(1 - x, y), (x, 1 - y), (1 - x, 1 - y)]


def _gather_shards(arrs):
    n = len(arrs)

    def body(*refs):
        ins, outs = refs[:n], refs[n:2 * n]
        send_ici, recv_ici, send_d2d, recv_d2d, loc = refs[2 * n:]
        x, y, c = _mesh_pos()
        s = 2 * x + y
        chips = _other_chips(x, y)
        local = []
        for i in range(n):
            cp = pltpu.make_async_copy(ins[i], outs[i].at[s], loc.at[i])
            cp.start()
            local.append(cp)

        def half(i, shard, core):
            h = arrs[i].shape[0] // 2
            return outs[i].at[shard, pl.ds(core * h, h), :]

        sends = []
        for i in range(n):
            h = arrs[i].shape[0] // 2
            for j, (px, py) in enumerate(chips):
                cp = pltpu.make_async_remote_copy(
                    src_ref=ins[i].at[pl.ds(c * h, h), :], dst_ref=half(i, s, c),
                    send_sem=send_ici.at[i, j], recv_sem=recv_ici.at[i, j],
                    device_id=(px, py, c), device_id_type=MESH_ID)
                cp.start()
                sends.append(cp)
        for i in range(n):
            for j, (px, py) in enumerate(chips):
                ps = 2 * px + py
                landed = half(i, ps, c)
                pltpu.make_async_remote_copy(
                    src_ref=landed, dst_ref=landed, send_sem=send_ici.at[i, j], recv_sem=recv_ici.at[i, j],
                    device_id=(px, py, c), device_id_type=MESH_ID).wait_recv()
                cp = pltpu.make_async_remote_copy(
                    src_ref=landed, dst_ref=landed, send_sem=send_d2d.at[i, j], recv_sem=recv_d2d.at[i, j],
                    device_id=(x, y, 1 - c), device_id_type=MESH_ID)
                cp.start()
                sends.append(cp)
        for i in range(n):
            for j, (px, py) in enumerate(chips):
                ps = 2 * px + py
                theirs = half(i, ps, 1 - c)
                pltpu.make_async_remote_copy(
                    src_ref=theirs, dst_ref=theirs, send_sem=send_d2d.at[i, j], recv_sem=recv_d2d.at[i, j],
                    device_id=(x, y, 1 - c), device_id_type=MESH_ID).wait_recv()
        for cp in sends:
            cp.wait_send()
        for cp in local:
            cp.wait()

    sem = pltpu.SemaphoreType.DMA
    return pl.pallas_call(
        body, name="gather_shards",
        in_specs=[HBM_SPEC] * n, out_specs=[HBM_SPEC] * n,
        out_shape=[_sds((NSHARD,) + a.shape, a.dtype) for a in arrs],
        scratch_shapes=[sem((n, 3)), sem((n, 3)), sem((n, 3)), sem((n, 3)), sem((n,))],
    )(*arrs)


def _send_sibling_halves(arrs):
    n = len(arrs)

    def body(*refs):
        ins, outs = refs[:n], refs[n:2 * n]
        send, recv = refs[2 * n:]
        x, y, c = _mesh_pos()
        cps = []
        for i in range(n):
            lh = arrs[i].shape[0] // 2
            cp = pltpu.make_async_remote_copy(
                src_ref=ins[i].at[pl.ds((1 - c) * lh, lh)], dst_ref=outs[i],
                send_sem=send.at[i], recv_sem=recv.at[i], device_id=(x, y, 1 - c), device_id_type=MESH_ID)
            cp.start()
            cps.append(cp)
        for cp in cps:
            cp.wait()

    sem = pltpu.SemaphoreType.DMA
    return pl.pallas_call(
        body, name="send_sibling_halves", in_specs=[HBM_SPEC] * n, out_specs=[HBM_SPEC] * n,
        out_shape=[_sds((a.shape[0] // 2,) + a.shape[1:], a.dtype) for a in arrs],
        scratch_shapes=[sem((n,)), sem((n,))],
    )(*arrs)


def _send_chip_shards(arrs):
    n = len(arrs)

    def body(*refs):
        ins, outs = refs[:n], refs[n:2 * n]
        send, recv = refs[2 * n:]
        x, y, c = _mesh_pos()
        cps = []
        for i in range(n):
            for j, (px, py) in enumerate(_other_chips(x, y)):
                cp = pltpu.make_async_remote_copy(
                    src_ref=ins[i].at[:, 2 * px + py], dst_ref=outs[i].at[j],
                    send_sem=send.at[i, j], recv_sem=recv.at[i, j], device_id=(px, py, c), device_id_type=MESH_ID)
                cp.start()
                cps.append(cp)
        for cp in cps:
            cp.wait()

    sem = pltpu.SemaphoreType.DMA
    return pl.pallas_call(
        body, name="send_chip_shards", in_specs=[HBM_SPEC] * n, out_specs=[HBM_SPEC] * n,
        out_shape=[_sds((3, a.shape[0], a.shape[2], a.shape[3]), a.dtype) for a in arrs],
        scratch_shapes=[sem((n, 3)), sem((n, 3))],
    )(*arrs)


def _join_halves(arrs):
    n = len(arrs)

    def body(*refs):
        bufs = refs[n:2 * n]
        send, recv = refs[2 * n:]
        x, y, c = _mesh_pos()
        cps = []
        for i in range(n):
            lh = arrs[i].shape[0] // 2
            mine = bufs[i].at[pl.ds(c * lh, lh)]
            cp = pltpu.make_async_remote_copy(
                src_ref=mine, dst_ref=mine, send_sem=send.at[i], recv_sem=recv.at[i],
                device_id=(x, y, 1 - c), device_id_type=MESH_ID)
            cp.start()
            cps.append(cp)
        for i, cp in enumerate(cps):
            lh = arrs[i].shape[0] // 2
            theirs = bufs[i].at[pl.ds((1 - c) * lh, lh)]
            cp.wait_send()
            pltpu.make_async_remote_copy(
                src_ref=theirs, dst_ref=theirs, send_sem=send.at[i], recv_sem=recv.at[i],
                device_id=(x, y, 1 - c), device_id_type=MESH_ID).wait_recv()

    sem = pltpu.SemaphoreType.DMA
    return pl.pallas_call(
        body, name="join_halves", in_specs=[HBM_SPEC] * n, out_specs=[HBM_SPEC] * n,
        out_shape=[_sds(a.shape, a.dtype) for a in arrs],
        scratch_shapes=[sem((n,)), sem((n,))], input_output_aliases={i: i for i in range(n)},
    )(*arrs)


def _exchange_all(buf):
    def body(in_ref, out_ref, send, recv, loc):
        x, y, c = _mesh_pos()
        me = 4 * x + 2 * y + c
        lc = pltpu.make_async_copy(in_ref, out_ref.at[me], loc)
        lc.start()
        cps = []
        for k in range(1, 8):
            fx, fy, fc = (k >> 2) & 1, (k >> 1) & 1, k & 1
            peer = (x ^ fx, y ^ fy, c ^ fc)
            cp = pltpu.make_async_remote_copy(
                src_ref=in_ref, dst_ref=out_ref.at[me], send_sem=send.at[k - 1], recv_sem=recv.at[k - 1],
                device_id=peer, device_id_type=MESH_ID)
            cp.start()
            cps.append(cp)
        for k in range(1, 8):
            fx, fy, fc = (k >> 2) & 1, (k >> 1) & 1, k & 1
            peer = (x ^ fx, y ^ fy, c ^ fc)
            src_slot = 4 * peer[0] + 2 * peer[1] + peer[2]
            cps[k - 1].wait_send()
            pltpu.make_async_remote_copy(
                src_ref=in_ref, dst_ref=out_ref.at[src_slot], send_sem=send.at[k - 1], recv_sem=recv.at[k - 1],
                device_id=peer, device_id_type=MESH_ID).wait_recv()
        lc.wait()

    sem = pltpu.SemaphoreType.DMA
    return pl.pallas_call(
        body, name="exchange_all", in_specs=[HBM_SPEC], out_specs=HBM_SPEC,
        out_shape=_sds((8,) + buf.shape, buf.dtype),
        scratch_shapes=[sem((7,)), sem((7,)), sem(())],
    )(buf)


def _pad_to(a, axis, size):
    pad = [(0, 0)] * a.ndim
    pad[axis] = (0, size - a.shape[axis])
    return jnp.pad(a, pad)


def _strips(w):
    k, d = w.shape
    return _pad_to(w, 0, 32).reshape(32, d // LANES, LANES).transpose(1, 0, 2)


def kernel(x, norm_mix, w_in, gate_bias, conv_w, conv_b, conv_ln_g, conv_ln_b, w_conv_out, sgu_ln_g, sgu_ln_b, w_spatial, b_spatial, w_sgu_out, w_o, norm_ffn, w_ffn_gate, w_ffn_up, w_ffn_down, norm_final, loss_target, m_norm_mix, m_w_in, m_gate_bias, m_conv_w, m_conv_b, m_conv_ln_g, m_conv_ln_b, m_w_conv_out, m_sgu_ln_g, m_sgu_ln_b, m_w_spatial, m_b_spatial, m_w_sgu_out, m_w_o, m_norm_ffn, m_w_ffn_gate, m_w_ffn_up, m_w_ffn_down, m_norm_final, v_norm_mix, v_w_in, v_gate_bias, v_conv_w, v_conv_b, v_conv_ln_g, v_conv_ln_b, v_w_conv_out, v_sgu_ln_g, v_sgu_ln_b, v_w_spatial, v_b_spatial, v_w_sgu_out, v_w_o, v_norm_ffn, v_w_ffn_gate, v_w_ffn_up, v_w_ffn_down, v_norm_final):
    BL, S, D = x.shape
    T = BL * S
    L = w_in.shape[0]
    CS = w_in.shape[2]
    CN = CS // 3
    DQ = D // NSHARD
    FS = w_ffn_gate.shape[2]
    FP = -(-FS // 256) * 256
    G, CH = w_spatial.shape[1], w_spatial.shape[2]
    KW = conv_w.shape[1]
    CQ = conv_w.shape[3]
    NSTR = D // LANES
    tm = min(512, S // 2)
    tm2 = max(tm // 2, CH)
    rb = min(64, tm)
    mx, my, mc = _mesh_pos()
    pos = jnp.stack([mc, 2 * mx + my]).astype(jnp.int32)

    w_in_b = w_in.astype(BF16).reshape(L * D, CS)
    w_sq_b = jnp.stack([w_conv_out, w_sgu_out, w_o], axis=1).astype(BF16).reshape(L * 3 * DQ, D)
    wg_b = _pad_to(w_ffn_gate.astype(BF16), 2, FP).reshape(L * D, FP)
    wu_b = _pad_to(w_ffn_up.astype(BF16), 2, FP).reshape(L * D, FP)
    wd_b = _pad_to(w_ffn_down.astype(BF16), 1, FP).reshape(L * FP, D)
    cw_l = _pad_to(conv_w.reshape(L, KW, CQ), 1, 32).reshape(L * 32, CQ)
    w_in_g, w_sq_g, wg_g, wu_g, wd_g, cw_g = _gather_shards([w_in_b, w_sq_b, wg_b, wu_b, wd_b, cw_l])
    conv_w_full = cw_g.reshape(NSHARD, L, 32, CQ).transpose(1, 2, 0, 3).reshape(L, 32, D)[:, :KW]

    x2d = x.reshape(T, D)
    tgt = loss_target.reshape(T, D)
    row = lambda a, l: a[l].reshape(1, -1)

    saved = []
    xc = x2d
    for l in range(L):
        ws_b = w_spatial[l].astype(BF16)
        bs_b = jnp.repeat(b_spatial[l].T, D // G, axis=1)
        cw_s = _strips(conv_w_full[l])
        h, proj = _mix_in_fwd(xc, row(norm_mix, l), w_in_g, l, tm)
        c1h, rstd_c, c3, ya = _conv_fwd(proj, cw_s, row(conv_b, l), row(conv_ln_g, l), row(conv_ln_b, l),
                                        w_sq_g, l, S, tm, rb)
        mixed, gated, yb, merged, x1 = _sgu_merge_fwd(proj, ya, xc, row(sgu_ln_g, l), row(sgu_ln_b, l), ws_b, bs_b,
                                                      row(gate_bias, l), w_sq_g, l, tm2)
        h2, gt, up, act, x2 = _ffn_fwd(x1, row(norm_ffn, l), wg_g, wu_g, wd_g, l, tm2)
        saved.append(dict(x=xc, h=h, proj=proj, c1h=c1h, rstd_c=rstd_c, c3=c3, ya=ya, mixed=mixed, gated=gated,
                          yb=yb, merged=merged, x1=x1, h2=h2, gt=gt, up=up, act=act, ws_b=ws_b, cw=conv_w_full[l]))
        xc = x2

    dx, loss_part, d_norm_final = _loss_head(xc, norm_final.reshape(1, D), tgt, tm)
    loss = lax.psum(loss_part[0, 0], ("x", "y", "c"))

    g_in = g_co = g_so = g_o = g_gu = g_d = None
    small = [None] * L
    tt = min(1024, T // 2)
    nt = T // tt
    for l in reversed(range(L)):
        sv = saved[l]
        dx1, dgt, dup, d_norm_ffn = _ffn_bwd(dx, sv["x1"], sv["gt"], sv["up"], row(norm_ffn, l), wg_g, wu_g, wd_g, l, tm2)
        g_gu = _tn_matmul("grad_w_gate_up", sv["h2"], (tt, D), lambda j, t: (t, 0), [dgt, dup], (tt, FP),
                          lambda j, t: (t, j), (L, NSHARD, D, FP), (None, None, D, FP), lambda j, t: (l, j, 0, 0),
                          NSHARD, nt, g_gu)
        g_d = _tn_matmul("grad_w_down", sv["act"], (tt, FP), lambda j, t: (t, j), [dx], (tt, D), lambda j, t: (t, 0),
                         (L, NSHARD, FP, D), (None, None, FP, D), lambda j, t: (l, j, 0, 0), NSHARD, nt, g_d)
        wst_b = jnp.swapaxes(sv["ws_b"], 1, 2)
        dya, dyb, dp3, d_gate_bias, d_sgu_g, d_sgu_b, d_bs, d_ws = _merge_sgu_bwd(
            dx1, sv["proj"], sv["ya"], sv["yb"], sv["mixed"], row(sgu_ln_g, l), row(sgu_ln_b, l), sv["ws_b"], wst_b,
            row(gate_bias, l), w_sq_g, l, tm2)
        sq_args = ((tt, D), lambda j, t: (t, 0))
        sq_out = ((L, D, D), (None, D, D), lambda j, t: (l, 0, 0), 1, nt)
        g_o = _tn_matmul("grad_w_o", sv["merged"], *sq_args, [dx1], *sq_args, *sq_out, g_o)
        g_so = _tn_matmul("grad_w_sgu_out", sv["gated"], *sq_args, [dyb], *sq_args, *sq_out, g_so)
        g_co = _tn_matmul("grad_w_conv_out", sv["c3"], *sq_args, [dya], *sq_args, *sq_out, g_co)
        dc1, d_cln_g, d_cln_b, d_conv_b = _conv_ln_bwd(dya, sv["c1h"], sv["rstd_c"], row(conv_ln_g, l),
                                                       row(conv_ln_b, l), w_sq_g, l, tm)
        dp3, d_cw_s = _conv_bwd(dc1, sv["proj"], dp3, _strips(sv["cw"][::-1]), S, tm, rb)
        g_in = _tn_matmul("grad_w_in", sv["h"], (tt, D), lambda j, t: (t, 0), [dp3], (tt, CS), lambda j, t: (t, j),
                          (L, NSHARD, D, CS), (None, None, D, CS), lambda j, t: (l, j, 0, 0), NSHARD, nt, g_in)
        dx, d_norm_mix = _mix_in_bwd(dx1, dp3, sv["x"], row(norm_mix, l), w_in_g, l, tm)
        d_cw = d_cw_s.transpose(1, 0, 2).reshape(32, D)
        small[l] = [d_norm_mix, d_gate_bias.reshape(2, D), d_cw, d_conv_b, d_cln_g, d_cln_b, d_sgu_g, d_sgu_b,
                    d_ws.reshape(G * CH * CH // D, D), d_bs.reshape(G * CH // D, D), d_norm_ffn]
    grad_x = dx.reshape(BL, S, D)

    big = [g_in[0], g_co[0].reshape(L, NSHARD, DQ, D), g_so[0].reshape(L, NSHARD, DQ, D),
           g_o[0].reshape(L, NSHARD, DQ, D), g_gu[0], g_gu[1], g_d[0]]
    names = ["w_in", "w_conv_out", "w_sgu_out", "w_o", "w_ffn_gate", "w_ffn_up", "w_ffn_down"]
    from_sib = _send_sibling_halves(big)
    chip_sum = [_add_halves("presum_" + nm, g, r, pos, _row_tile(g.shape[2], 256))
                for nm, g, r in zip(names, big, from_sib)]
    from_chips = _send_chip_shards(chip_sum)
    halves = [_add_shards("shardsum_" + nm, p, r, pos, _row_tile(p.shape[2], 256))
              for nm, p, r in zip(names, chip_sum, from_chips)]
    g_full = _join_halves(halves)
    g_w_in, g_w_co, g_w_so, g_w_o, g_w_g, g_w_u, g_w_d = g_full
    g_w_g = g_w_g[:, :, :FS]
    g_w_u = g_w_u[:, :, :FS]
    g_w_d = g_w_d[:, :FS, :]

    pieces = [p for l in range(L) for p in small[l]] + [d_norm_final]
    packed = jnp.concatenate(pieces, axis=0)
    n_rows = packed.shape[0]
    n_pad = -(-n_rows // 8) * 8
    packed = _pad_to(packed, 0, n_pad)
    summed = _sum_slots(_exchange_all(packed), 8)
    off = 0
    sg = []
    for l in range(L):
        cur = []
        for p in small[l]:
            cur.append(summed[off:off + p.shape[0]])
            off += p.shape[0]
        sg.append(cur)
    g_norm_final = summed[off]

    def per_layer(k, shape):
        return jnp.stack([sg[l][k] for l in range(L)]).reshape(shape)

    g_norm_mix = per_layer(0, (L, D))
    g_gate_bias = per_layer(1, (L, 2 * D))
    g_conv_w_full = jnp.stack([sg[l][2][:KW] for l in range(L)])
    g_conv_w = lax.dynamic_slice_in_dim(g_conv_w_full, (2 * mx + my) * CQ, CQ, axis=2).reshape(L, KW, 1, CQ)
    g_conv_b = per_layer(3, (L, D))
    g_conv_ln_g = per_layer(4, (L, D))
    g_conv_ln_b = per_layer(5, (L, D))
    g_sgu_ln_g = per_layer(6, (L, D))
    g_sgu_ln_b = per_layer(7, (L, D))
    g_w_spatial = per_layer(8, (L, G, CH, CH))
    g_b_spatial = per_layer(9, (L, G, CH))
    g_norm_ffn = per_layer(10, (L, D))

    grads = [g_norm_mix, g_w_in, g_gate_bias, g_conv_w, g_conv_b, g_conv_ln_g, g_conv_ln_b, g_w_co, g_sgu_ln_g,
             g_sgu_ln_b, g_w_spatial, g_b_spatial, g_w_so, g_w_o, g_norm_ffn, g_w_g, g_w_u, g_w_d, g_norm_final]
    weights = [norm_mix, w_in, gate_bias, conv_w, conv_b, conv_ln_g, conv_ln_b, w_conv_out, sgu_ln_g, sgu_ln_b,
               w_spatial, b_spatial, w_sgu_out, w_o, norm_ffn, w_ffn_gate, w_ffn_up, w_ffn_down, norm_final]
    ms = [m_norm_mix, m_w_in, m_gate_bias, m_conv_w, m_conv_b, m_conv_ln_g, m_conv_ln_b, m_w_conv_out, m_sgu_ln_g,
          m_sgu_ln_b, m_w_spatial, m_b_spatial, m_w_sgu_out, m_w_o, m_norm_ffn, m_w_ffn_gate, m_w_ffn_up,
          m_w_ffn_down, m_norm_final]
    vs = [v_norm_mix, v_w_in, v_gate_bias, v_conv_w, v_conv_b, v_conv_ln_g, v_conv_ln_b, v_w_conv_out, v_sgu_ln_g,
          v_sgu_ln_b, v_w_spatial, v_b_spatial, v_w_sgu_out, v_w_o, v_norm_ffn, v_w_ffn_gate, v_w_ffn_up,
          v_w_ffn_down, v_norm_final]

    big_idx = [1, 7, 12, 13, 15, 16, 17]
    deltas, new_m, new_v = [None] * 19, [None] * 19, [None] * 19
    for k in big_idx:
        shp = weights[k].shape
        r2 = (shp[0] * shp[1], shp[2])
        d_, m_, v_ = _adamw("adamw_" + str(k), weights[k].reshape(r2), grads[k].reshape(r2), ms[k].reshape(r2),
                            vs[k].reshape(r2), _row_tile(r2[0], 256))
        deltas[k], new_m[k], new_v[k] = d_.reshape(shp), m_.reshape(shp), v_.reshape(shp)
    small_idx = [k for k in range(19) if k not in big_idx]

    def pack(arrs):
        flat = jnp.concatenate([arrs[k].reshape(-1) for k in small_idx])
        rows = -(-flat.shape[0] // (256 * LANES)) * 256
        return _pad_to(flat, 0, rows * LANES).reshape(rows, LANES)

    pw, pg, pm, pv = pack(weights), pack(grads), pack(ms), pack(vs)
    d_, m_, v_ = _adamw("adamw_small", pw, pg, pm, pv, _row_tile(pw.shape[0], 512))
    off = 0
    for k in small_idx:
        n_el = weights[k].size
        shp = weights[k].shape
        deltas[k] = d_.reshape(-1)[off:off + n_el].reshape(shp)
        new_m[k] = m_.reshape(-1)[off:off + n_el].reshape(shp)
        new_v[k] = v_.reshape(-1)[off:off + n_el].reshape(shp)
        off += n_el

    return (loss, grad_x, *grads, *deltas, *new_m, *new_v)
```

```python
import functools

import jax
import jax.numpy as jnp
from jax import lax
from jax.experimental import pallas as pl
from jax.experimental.pallas import tpu as pltpu

F32 = jnp.float32
BF16 = jnp.bfloat16
EPS = 1e-6
ADAM_LR = 0.001
ADAM_B1 = 0.9
ADAM_B2 = 0.999
ADAM_EPS = 1e-08
ADAM_WD = 0.01
ADAM_STEP = 10

NSHARD = 4
LANES = 128
HALO = 16
VMEM_LIMIT = 60 * 1024 * 1024
MESH_ID = pl.DeviceIdType.MESH


def _dot(a, b):
    return jnp.dot(a, b, preferred_element_type=F32)


def _dot_nt(a, b):
    return lax.dot_general(a, b, (((1,), (1,)), ((), ())), preferred_element_type=F32)


def _dot_tn(a, b):
    return lax.dot_general(a, b, (((0,), (0,)), ((), ())), preferred_element_type=F32)


def _sig(z):
    return 1.0 / (1.0 + jnp.exp(-z))


def _res(shape, imap=None):
    nd = len(shape)
    if imap is None:
        imap = lambda *_: (0,) * nd
    return pl.BlockSpec(shape, imap, pipeline_mode=pl.Buffered(1))


def _cparams(sem):
    return pltpu.CompilerParams(dimension_semantics=sem, vmem_limit_bytes=VMEM_LIMIT)


def _sds(shape, dtype):
    return jax.ShapeDtypeStruct(shape, dtype)


def _after(body, n_in, dep):
    if dep is None:
        return body, [], []

    def wrapped(*refs):
        return body(*refs[:n_in], *refs[n_in + 1:])

    return wrapped, [pl.BlockSpec(memory_space=pl.ANY)], [dep]


def _mix_in_fwd(x2d, g_mix, w_in_g, layer, tm, dep=None):
    T, D = x2d.shape
    CS = w_in_g.shape[2]
    CN = CS // 3

    def body(x_ref, g_ref, w_ref, h_ref, p_ref):
        x = x_ref[...]
        rstd = lax.rsqrt(jnp.mean(x * x, axis=-1, keepdims=True) + EPS)
        h = (x * rstd * g_ref[...]).astype(BF16)
        h_ref[...] = h
        for s in range(NSHARD):
            for j in range(3):
                c0 = s * CS + j * CN
                p_ref[:, c0:c0 + CN] = _dot(h, w_ref[s, :, j * CN:(j + 1) * CN]).astype(BF16)

    body, dep_spec, dep_arg = _after(body, 3, dep)
    return pl.pallas_call(
        body, name="mix_in_fwd", grid=(T // tm,),
        in_specs=[pl.BlockSpec((tm, D), lambda i: (i, 0)), _res((1, D)),
                  _res((NSHARD, D, CS), lambda i: (0, layer, 0))] + dep_spec,
        out_specs=[pl.BlockSpec((tm, D), lambda i: (i, 0)), pl.BlockSpec((tm, NSHARD * CS), lambda i: (i, 0))],
        out_shape=[_sds((T, D), BF16), _sds((T, NSHARD * CS), BF16)],
        compiler_params=_cparams(("parallel",)),
    )(x2d, g_mix, w_in_g, *dep_arg)


def _halo_maps(tm, n_rows):
    nb = tm // HALO
    last = n_rows // HALO - 1
    prev = lambda i: (jnp.maximum(i * nb - 1, 0), 0)
    nxt = lambda i: (jnp.minimum((i + 1) * nb, last), 0)
    return prev, nxt


def _dwconv(pad_ref, w_ref, out_ref, n_strips, tm, kw, rb):
    off = HALO - (kw - 1) // 2

    def strip(cs, carry):
        for r0 in range(0, tm, rb):
            acc = jnp.zeros((rb, LANES), F32)
            for k in range(kw):
                r = r0 + off + k
                acc = acc + w_ref[cs, k:k + 1, :] * pad_ref[cs, r:r + rb, :]
            out_ref[cs, r0:r0 + rb, :] = acc
        return carry

    lax.fori_loop(0, n_strips, strip, 0)


def _fill_c0_pad(pad_ref, pa_ref, pprev_ref, pnext_ref, D, tm, first, last):
    for cs in range(D // LANES):
        lo, hi = cs * LANES, (cs + 1) * LANES

        def c0_of(ref):
            return ref[:, lo:hi].astype(F32) * _sig(ref[:, D + lo:D + hi].astype(F32))

        pad_ref[cs, HALO:HALO + tm, :] = c0_of(pa_ref)
        pad_ref[cs, 0:HALO, :] = jnp.where(first, 0.0, c0_of(pprev_ref))
        pad_ref[cs, HALO + tm:HALO + tm + HALO, :] = jnp.where(last, 0.0, c0_of(pnext_ref))


def _conv_fwd(proj, conv_w_s, conv_b, ln_g, ln_b, w_sq_g, layer, seq, tm, rb):
    T = proj.shape[0]
    D = conv_b.shape[1]
    DQ = D // NSHARD
    NSTR = D // LANES
    KW = 31
    tps = seq // tm
    prev, nxt = _halo_maps(tm, T)

    def body(pa_ref, pprev_ref, pnext_ref, w_ref, b_ref, g_ref, be_ref, wco_ref,
             c1h_ref, rstd_ref, c3_ref, ya_ref, pad_ref, c1s_ref):
        i = pl.program_id(0)
        first = (i % tps) == 0
        last = (i % tps) == tps - 1
        _fill_c0_pad(pad_ref, pa_ref, pprev_ref, pnext_ref, D, tm, first, last)
        _dwconv(pad_ref, w_ref, c1s_ref, NSTR, tm, KW, rb)
        c1 = jnp.concatenate([c1s_ref[cs] for cs in range(NSTR)], axis=1) + b_ref[...]
        mu = jnp.mean(c1, axis=-1, keepdims=True)
        cc = c1 - mu
        rstd = lax.rsqrt(jnp.mean(cc * cc, axis=-1, keepdims=True) + EPS)
        c1h = cc * rstd
        c1h_ref[...] = c1h.astype(BF16)
        rstd_ref[...] = rstd
        c2 = c1h * g_ref[...] + be_ref[...]
        c3 = (c2 * _sig(c2)).astype(BF16)
        c3_ref[...] = c3
        ya_ref[...] = _dot(c3, wco_ref[...].reshape(D, D)).astype(BF16)

    row = lambda i: (i, 0)
    return pl.pallas_call(
        body, name="conv_fwd", grid=(T // tm,),
        in_specs=[pl.BlockSpec((tm, 2 * D), row), pl.BlockSpec((HALO, 2 * D), prev), pl.BlockSpec((HALO, 2 * D), nxt),
                  _res((NSTR, 32, LANES)), _res((1, D)), _res((1, D)), _res((1, D)),
                  _res((NSHARD, DQ, D), lambda i: (0, layer * 3 + 0, 0))],
        out_specs=[pl.BlockSpec((tm, D), row), pl.BlockSpec((tm, 1), row), pl.BlockSpec((tm, D), row),
                   pl.BlockSpec((tm, D), row)],
        out_shape=[_sds((T, D), BF16), _sds((T, 1), F32), _sds((T, D), BF16), _sds((T, D), BF16)],
        scratch_shapes=[pltpu.VMEM((NSTR, tm + 2 * HALO, LANES), F32), pltpu.VMEM((NSTR, tm, LANES), F32)],
        compiler_params=_cparams(("parallel",)),
    )(proj, proj, proj, conv_w_s, conv_b, ln_g, ln_b, w_sq_g)


def _sgu_merge_fwd(proj, ya, x2d, ln_g, ln_b, ws_b, bs_b, gate_bias, w_sq_g, layer, tm):
    T, D = x2d.shape
    DQ = D // NSHARD
    G, CH, _ = ws_b.shape
    GD = D // G

    def body(puv_ref, pg_ref, ya_ref, x_ref, g_ref, be_ref, ws_ref, bsb_ref, gb_ref, wso_ref, wo_ref,
             mixed_ref, gated_ref, yb_ref, merged_ref, x1_ref, mix_scr):
        u = puv_ref[:, :D].astype(F32)
        v = puv_ref[:, D:].astype(F32)
        mu = jnp.mean(v, axis=-1, keepdims=True)
        vc = v - mu
        rstd = lax.rsqrt(jnp.mean(vc * vc, axis=-1, keepdims=True) + EPS)
        vn = (vc * rstd * g_ref[...] + be_ref[...]).astype(BF16)
        for ch in range(tm // CH):
            for g in range(G):
                blk = vn[ch * CH:(ch + 1) * CH, g * GD:(g + 1) * GD]
                mix_scr[ch * CH:(ch + 1) * CH, g * GD:(g + 1) * GD] = (
                    _dot(ws_ref[g], blk) + bsb_ref[:, g * GD:(g + 1) * GD])
        mixed = mix_scr[...]
        mixed_ref[...] = mixed.astype(BF16)
        gated = (u * mixed).astype(BF16)
        gated_ref[...] = gated
        yb = _dot(gated, wso_ref[...].reshape(D, D))
        yb_ref[...] = yb.astype(BF16)
        sa = _sig(pg_ref[:, :D].astype(F32) + gb_ref[:, :D])
        sb = _sig(pg_ref[:, D:].astype(F32) + gb_ref[:, D:])
        merged = (sa * ya_ref[...].astype(F32) + sb * yb).astype(BF16)
        merged_ref[...] = merged
        x1_ref[...] = x_ref[...] + _dot(merged, wo_ref[...].reshape(D, D))

    row = lambda i: (i, 0)
    return pl.pallas_call(
        body, name="sgu_merge_fwd", grid=(T // tm,),
        in_specs=[pl.BlockSpec((tm, 2 * D), lambda i: (i, 1)), pl.BlockSpec((tm, 2 * D), lambda i: (i, 2)),
                  pl.BlockSpec((tm, D), row), pl.BlockSpec((tm, D), row),
                  _res((1, D)), _res((1, D)), _res((G, CH, CH)), _res((CH, D)), _res((1, 2 * D)),
                  _res((NSHARD, DQ, D), lambda i: (0, layer * 3 + 1, 0)),
                  _res((NSHARD, DQ, D), lambda i: (0, layer * 3 + 2, 0))],
        out_specs=[pl.BlockSpec((tm, D), row)] * 5,
        out_shape=[_sds((T, D), BF16)] * 4 + [_sds((T, D), F32)],
        scratch_shapes=[pltpu.VMEM((tm, D), F32)],
        compiler_params=_cparams(("parallel",)),
    )(proj, proj, ya, x2d, ln_g, ln_b, ws_b, bs_b, gate_bias, w_sq_g, w_sq_g)


def _ffn_fwd(x1, g_ffn, wg_g, wu_g, wd_g, layer, tm):
    T, D = x1.shape
    FP = wg_g.shape[2]
    F = NSHARD * FP

    def body(x_ref, g_ref, wg_ref, wu_ref, wd_ref, h2_ref, gt_ref, up_ref, act_ref, x2_ref):
        x = x_ref[...]
        rstd = lax.rsqrt(jnp.mean(x * x, axis=-1, keepdims=True) + EPS)
        h2 = (x * rstd * g_ref[...]).astype(BF16)
        h2_ref[...] = h2
        acc = x
        for s in range(NSHARD):
            gt = _dot(h2, wg_ref[s])
            up = _dot(h2, wu_ref[s])
            gt_ref[:, s * FP:(s + 1) * FP] = gt.astype(BF16)
            up_ref[:, s * FP:(s + 1) * FP] = up.astype(BF16)
            act = (gt * _sig(gt) * up).astype(BF16)
            act_ref[:, s * FP:(s + 1) * FP] = act
            acc = acc + _dot(act, wd_ref[s])
        x2_ref[...] = acc

    row = lambda i: (i, 0)
    return pl.pallas_call(
        body, name="ffn_fwd", grid=(T // tm,),
        in_specs=[pl.BlockSpec((tm, D), row), _res((1, D)),
                  _res((NSHARD, D, FP), lambda i: (0, layer, 0)), _res((NSHARD, D, FP), lambda i: (0, layer, 0)),
                  _res((NSHARD, FP, D), lambda i: (0, layer, 0))],
        out_specs=[pl.BlockSpec((tm, D), row), pl.BlockSpec((tm, F), row), pl.BlockSpec((tm, F), row),
                   pl.BlockSpec((tm, F), row), pl.BlockSpec((tm, D), row)],
        out_shape=[_sds((T, D), BF16), _sds((T, F), BF16), _sds((T, F), BF16), _sds((T, F), BF16), _sds((T, D), F32)],
        compiler_params=_cparams(("parallel",)),
    )(x1, g_ffn, wg_g, wu_g, wd_g)


def _loss_head(xf, g_fin, target, tm):
    T, D = xf.shape
    n = T // tm

    def body(x_ref, g_ref, t_ref, dx_ref, loss_ref, dg_ref, acc_ref):
        i = pl.program_id(0)

        @pl.when(i == 0)
        def _():
            acc_ref[...] = jnp.zeros_like(acc_ref)
            dg_ref[...] = jnp.zeros_like(dg_ref)

        x = x_ref[...]
        g = g_ref[...]
        rstd = lax.rsqrt(jnp.mean(x * x, axis=-1, keepdims=True) + EPS)
        xh = x * rstd
        diff = xh * g - t_ref[...]
        acc_ref[...] += jnp.sum(diff * diff, axis=0, keepdims=True)
        dy = diff * (1.0 / D)
        dg_ref[...] += jnp.sum(dy * xh, axis=0, keepdims=True)
        dxh = dy * g
        dx_ref[...] = rstd * (dxh - xh * jnp.mean(dxh * xh, axis=-1, keepdims=True))

        @pl.when(i == n - 1)
        def _():
            tot = jnp.sum(acc_ref[...], axis=-1, keepdims=True) * (0.5 / D)
            loss_ref[...] = jnp.broadcast_to(tot, loss_ref.shape)

    row = lambda i: (i, 0)
    return pl.pallas_call(
        body, name="loss_head", grid=(n,),
        in_specs=[pl.BlockSpec((tm, D), row), _res((1, D)), pl.BlockSpec((tm, D), row)],
        out_specs=[pl.BlockSpec((tm, D), row), pl.BlockSpec((1, LANES), lambda i: (0, 0)),
                   pl.BlockSpec((1, D), lambda i: (0, 0))],
        out_shape=[_sds((T, D), F32), _sds((1, LANES), F32), _sds((1, D), F32)],
        scratch_shapes=[pltpu.VMEM((1, D), F32)],
        compiler_params=_cparams(("arbitrary",)),
    )(xf, g_fin, target)


def _ffn_bwd(dx2, x1, gt, up, g_ffn, wg_g, wu_g, wd_g, layer, tm, dep=None):
    T, D = x1.shape
    FP = wg_g.shape[2]
    F = NSHARD * FP

    def body(dx2_ref, x1_ref, gt_ref, up_ref, g_ref, wg_ref, wu_ref, wd_ref, dx1_ref, dgt_ref, dup_ref, dg_ref):
        i = pl.program_id(0)

        @pl.when(i == 0)
        def _():
            dg_ref[...] = jnp.zeros_like(dg_ref)

        dx2 = dx2_ref[...]
        dx2b = dx2.astype(BF16)
        dh2 = jnp.zeros((tm, D), F32)
        for s in range(NSHARD):
            dact = _dot_nt(dx2b, wd_ref[s])
            g = gt_ref[:, s * FP:(s + 1) * FP].astype(F32)
            u = up_ref[:, s * FP:(s + 1) * FP].astype(F32)
            sg = _sig(g)
            dup = (dact * (g * sg)).astype(BF16)
            dgt = (dact * u * (sg * (1.0 + g * (1.0 - sg)))).astype(BF16)
            dgt_ref[:, s * FP:(s + 1) * FP] = dgt
            dup_ref[:, s * FP:(s + 1) * FP] = dup
            dh2 = dh2 + _dot_nt(dgt, wg_ref[s]) + _dot_nt(dup, wu_ref[s])
        x = x1_ref[...]
        rstd = lax.rsqrt(jnp.mean(x * x, axis=-1, keepdims=True) + EPS)
        xh = x * rstd
        dg_ref[...] += jnp.sum(dh2 * xh, axis=0, keepdims=True)
        dxh = dh2 * g_ref[...]
        dx1_ref[...] = dx2 + rstd * (dxh - xh * jnp.mean(dxh * xh, axis=-1, keepdims=True))

    row = lambda i: (i, 0)
    body, dep_spec, dep_arg = _after(body, 8, dep)
    return pl.pallas_call(
        body, name="ffn_bwd", grid=(T // tm,),
        in_specs=[pl.BlockSpec((tm, D), row), pl.BlockSpec((tm, D), row), pl.BlockSpec((tm, F), row),
                  pl.BlockSpec((tm, F), row), _res((1, D)),
                  _res((NSHARD, D, FP), lambda i: (0, layer, 0)), _res((NSHARD, D, FP), lambda i: (0, layer, 0)),
                  _res((NSHARD, FP, D), lambda i: (0, layer, 0))] + dep_spec,
        out_specs=[pl.BlockSpec((tm, D), row), pl.BlockSpec((tm, F), row), pl.BlockSpec((tm, F), row),
                   pl.BlockSpec((1, D), lambda i: (0, 0))],
        out_shape=[_sds((T, D), F32), _sds((T, F), BF16), _sds((T, F), BF16), _sds((1, D), F32)],
        compiler_params=_cparams(("arbitrary",)),
    )(dx2, x1, gt, up, g_ffn, wg_g, wu_g, wd_g, *dep_arg)


def _merge_sgu_bwd(dx1, proj, ya, yb, mixed, ln_g, ln_b, ws_b, wst_b, gate_bias, w_sq_g, layer, tm, dep=None):
    T, D = dx1.shape
    DQ = D // NSHARD
    G, CH, _ = ws_b.shape
    GD = D // G

    def body(dx1_ref, puv_ref, pg_ref, ya_ref, yb_ref, mixed_ref, g_ref, be_ref, ws_ref, wst_ref, gb_ref,
             wso_ref, wo_ref, dya_ref, dyb_ref, dp_ref, dgb_ref, dlg_ref, dlb_ref, dbs_ref, dws_ref,
             dvn_scr, dbs_scr):
        i = pl.program_id(0)

        @pl.when(i == 0)
        def _():
            for r in (dgb_ref, dlg_ref, dlb_ref, dws_ref, dbs_scr):
                r[...] = jnp.zeros_like(r)

        dmerged = _dot_nt(dx1_ref[...].astype(BF16), wo_ref[...].reshape(D, D))
        sa = _sig(pg_ref[:, :D].astype(F32) + gb_ref[:, :D])
        sb = _sig(pg_ref[:, D:].astype(F32) + gb_ref[:, D:])
        dya = (dmerged * sa).astype(BF16)
        dyb = (dmerged * sb).astype(BF16)
        dya_ref[...] = dya
        dyb_ref[...] = dyb
        dga = dmerged * ya_ref[...].astype(F32) * (sa * (1.0 - sa))
        dgb = dmerged * yb_ref[...].astype(F32) * (sb * (1.0 - sb))
        dp_ref[:, 4 * D:5 * D] = dga.astype(BF16)
        dp_ref[:, 5 * D:6 * D] = dgb.astype(BF16)
        dgb_ref[:, :D] += jnp.sum(dga, axis=0, keepdims=True)
        dgb_ref[:, D:] += jnp.sum(dgb, axis=0, keepdims=True)

        dgated = _dot_nt(dyb, wso_ref[...].reshape(D, D))
        u = puv_ref[:, :D].astype(F32)
        v = puv_ref[:, D:].astype(F32)
        dp_ref[:, 2 * D:3 * D] = (dgated * mixed_ref[...].astype(F32)).astype(BF16)
        dmixed = dgated * u
        mu = jnp.mean(v, axis=-1, keepdims=True)
        vc = v - mu
        rstd = lax.rsqrt(jnp.mean(vc * vc, axis=-1, keepdims=True) + EPS)
        vh = vc * rstd
        vn = (vh * g_ref[...] + be_ref[...]).astype(BF16)
        dmb = dmixed.astype(BF16)
        bs_part = jnp.zeros((CH, D), F32)
        for ch in range(tm // CH):
            rows = slice(ch * CH, (ch + 1) * CH)
            bs_part = bs_part + dmixed[rows, :]
            for g in range(G):
                cols = slice(g * GD, (g + 1) * GD)
                dws_ref[g] += _dot_nt(dmb[rows, cols], vn[rows, cols])
                dvn_scr[rows, cols] = _dot(wst_ref[g], dmb[rows, cols])
        dbs_scr[...] += bs_part
        dvn = dvn_scr[...]
        dlg_ref[...] += jnp.sum(dvn * vh, axis=0, keepdims=True)
        dlb_ref[...] += jnp.sum(dvn, axis=0, keepdims=True)
        dxh = dvn * g_ref[...]
        dv = rstd * (dxh - jnp.mean(dxh, axis=-1, keepdims=True) - vh * jnp.mean(dxh * vh, axis=-1, keepdims=True))
        dp_ref[:, 3 * D:4 * D] = dv.astype(BF16)

        @pl.when(i == pl.num_programs(0) - 1)
        def _():
            for g in range(G):
                blk = dbs_scr[:, g * GD:(g + 1) * GD]
                if GD != CH:
                    blk = jnp.concatenate([blk, jnp.zeros((CH, CH - GD), F32)], axis=1)
                dbs_ref[:, g * CH:(g + 1) * CH] = jnp.sum(blk.T, axis=0, keepdims=True)

    row = lambda i: (i, 0)
    fixed2 = lambda i: (0, 0)
    body, dep_spec, dep_arg = _after(body, 13, dep)
    return pl.pallas_call(
        body, name="merge_sgu_bwd", grid=(T // tm,),
        in_specs=[pl.BlockSpec((tm, D), row), pl.BlockSpec((tm, 2 * D), lambda i: (i, 1)),
                  pl.BlockSpec((tm, 2 * D), lambda i: (i, 2)), pl.BlockSpec((tm, D), row), pl.BlockSpec((tm, D), row),
                  pl.BlockSpec((tm, D), row), _res((1, D)), _res((1, D)), _res((G, CH, CH)), _res((G, CH, CH)),
                  _res((1, 2 * D)),
                  _res((NSHARD, DQ, D), lambda i: (0, layer * 3 + 1, 0)),
                  _res((NSHARD, DQ, D), lambda i: (0, layer * 3 + 2, 0))] + dep_spec,
        out_specs=[pl.BlockSpec((tm, D), row), pl.BlockSpec((tm, D), row),
                   pl.BlockSpec((tm, 6 * D), row),
                   pl.BlockSpec((1, 2 * D), fixed2), pl.BlockSpec((1, D), fixed2), pl.BlockSpec((1, D), fixed2),
                   pl.BlockSpec((1, G * CH), fixed2), pl.BlockSpec((G, CH, CH), lambda i: (0, 0, 0))],
        out_shape=[_sds((T, D), BF16), _sds((T, D), BF16), _sds((T, 6 * D), BF16),
                   _sds((1, 2 * D), F32), _sds((1, D), F32), _sds((1, D), F32), _sds((1, G * CH), F32),
                   _sds((G, CH, CH), F32)],
        scratch_shapes=[pltpu.VMEM((tm, D), F32), pltpu.VMEM((CH, D), F32)],
        compiler_params=_cparams(("arbitrary",)),
    )(dx1, proj, proj, ya, yb, mixed, ln_g, ln_b, ws_b, wst_b, gate_bias, w_sq_g, w_sq_g, *dep_arg)


def _conv_ln_bwd(dya, c1h, rstd_c, ln_g, ln_b, w_sq_g, layer, tm):
    T, D = dya.shape
    DQ = D // NSHARD

    def body(dya_ref, c1h_ref, rstd_ref, g_ref, be_ref, wco_ref, dc1_ref, dlg_ref, dlb_ref, dcb_ref):
        i = pl.program_id(0)

        @pl.when(i == 0)
        def _():
            for r in (dlg_ref, dlb_ref, dcb_ref):
                r[...] = jnp.zeros_like(r)

        dc3 = _dot_nt(dya_ref[...], wco_ref[...].reshape(D, D))
        c1h = c1h_ref[...].astype(F32)
        c2 = c1h * g_ref[...] + be_ref[...]
        sg = _sig(c2)
        dc2 = dc3 * (sg * (1.0 + c2 * (1.0 - sg)))
        dlg_ref[...] += jnp.sum(dc2 * c1h, axis=0, keepdims=True)
        dlb_ref[...] += jnp.sum(dc2, axis=0, keepdims=True)
        dxh = dc2 * g_ref[...]
        dc1 = rstd_ref[...] * (dxh - jnp.mean(dxh, axis=-1, keepdims=True)
                               - c1h * jnp.mean(dxh * c1h, axis=-1, keepdims=True))
        dc1_ref[...] = dc1
        dcb_ref[...] += jnp.sum(dc1, axis=0, keepdims=True)

    row = lambda i: (i, 0)
    fixed2 = lambda i: (0, 0)
    return pl.pallas_call(
        body, name="conv_ln_bwd", grid=(T // tm,),
        in_specs=[pl.BlockSpec((tm, D), row), pl.BlockSpec((tm, D), row), pl.BlockSpec((tm, 1), row),
                  _res((1, D)), _res((1, D)), _res((NSHARD, DQ, D), lambda i: (0, layer * 3 + 0, 0))],
        out_specs=[pl.BlockSpec((tm, D), row), pl.BlockSpec((1, D), fixed2), pl.BlockSpec((1, D), fixed2),
                   pl.BlockSpec((1, D), fixed2)],
        out_shape=[_sds((T, D), F32), _sds((1, D), F32), _sds((1, D), F32), _sds((1, D), F32)],
        compiler_params=_cparams(("arbitrary",)),
    )(dya, c1h, rstd_c, ln_g, ln_b, w_sq_g)


def _conv_bwd(dc1, proj, dp3, conv_wf_s, seq, tm, rb):
    T, D = dc1.shape
    NSTR = D // LANES
    KW = 31
    PADK = (KW - 1) // 2
    tps = seq // tm
    prev, nxt = _halo_maps(tm, T)
    n = T // tm

    def body(dc_ref, dcprev_ref, dcnext_ref, pa_ref, pprev_ref, pnext_ref, wf_ref, dp_in_ref,
             dp_ref, dw_ref, pad_ref, dpad_ref, dc0_ref, dwacc_ref):
        del dp_in_ref
        i = pl.program_id(0)
        first = (i % tps) == 0
        last = (i % tps) == tps - 1

        @pl.when(i == 0)
        def _():
            dwacc_ref[...] = jnp.zeros_like(dwacc_ref)

        _fill_c0_pad(pad_ref, pa_ref, pprev_ref, pnext_ref, D, tm, first, last)
        for cs in range(NSTR):
            lo, hi = cs * LANES, (cs + 1) * LANES
            dpad_ref[cs, HALO:HALO + tm, :] = dc_ref[:, lo:hi]
            dpad_ref[cs, 0:HALO, :] = jnp.where(first, 0.0, dcprev_ref[:, lo:hi])
            dpad_ref[cs, HALO + tm:HALO + tm + HALO, :] = jnp.where(last, 0.0, dcnext_ref[:, lo:hi])
        _dwconv(dpad_ref, wf_ref, dc0_ref, NSTR, tm, KW, rb)

        def strip(cs, carry):
            for r0 in range(0, tm, rb):
                d = dpad_ref[cs, HALO + r0:HALO + r0 + rb, :]
                for k in range(KW):
                    r = r0 + HALO - PADK + k
                    prod = d * pad_ref[cs, r:r + rb, :]
                    dwacc_ref[cs, k * 8:(k + 1) * 8, :] += jnp.sum(prod.reshape(rb // 8, 8, LANES), axis=0)
            return carry

        lax.fori_loop(0, NSTR, strip, 0)

        for cs in range(NSTR):
            lo, hi = cs * LANES, (cs + 1) * LANES
            av = pa_ref[:, lo:hi].astype(F32)
            sg = _sig(pa_ref[:, D + lo:D + hi].astype(F32))
            dc0 = dc0_ref[cs]
            dp_ref[:, lo:hi] = (dc0 * sg).astype(BF16)
            dp_ref[:, D + lo:D + hi] = (dc0 * av * (sg * (1.0 - sg))).astype(BF16)

        @pl.when(i == n - 1)
        def _():
            for cs in range(NSTR):
                dw_ref[cs] = jnp.sum(dwacc_ref[cs].reshape(32, 8, LANES), axis=1)

    row = lambda i: (i, 0)
    return pl.pallas_call(
        body, name="conv_bwd", grid=(n,),
        in_specs=[pl.BlockSpec((tm, D), row), pl.BlockSpec((HALO, D), prev), pl.BlockSpec((HALO, D), nxt),
                  pl.BlockSpec((tm, 2 * D), row), pl.BlockSpec((HALO, 2 * D), prev), pl.BlockSpec((HALO, 2 * D), nxt),
                  _res((NSTR, 32, LANES)), pl.BlockSpec(memory_space=pl.ANY)],
        out_specs=[pl.BlockSpec((tm, 2 * D), row),
                   pl.BlockSpec((NSTR, 32, LANES), lambda i: (0, 0, 0))],
        out_shape=[_sds(dp3.shape, BF16), _sds((NSTR, 32, LANES), F32)],
        scratch_shapes=[pltpu.VMEM((NSTR, tm + 2 * HALO, LANES), F32), pltpu.VMEM((NSTR, tm + 2 * HALO, LANES), F32),
                        pltpu.VMEM((NSTR, tm, LANES), F32), pltpu.VMEM((NSTR, 32 * 8, LANES), F32)],
        input_output_aliases={7: 0},
        compiler_params=_cparams(("arbitrary",)),
    )(dc1, dc1, dc1, proj, proj, proj, conv_wf_s, dp3)


def _mix_in_bwd(dx1, dp3, x2d, g_mix, w_in_g, layer, tm):
    T, D = x2d.shape
    CS = w_in_g.shape[2]
    CN = CS // 3

    def body(dx1_ref, dp_ref, x_ref, g_ref, w_ref, dx_ref, dg_ref):
        i = pl.program_id(0)

        @pl.when(i == 0)
        def _():
            dg_ref[...] = jnp.zeros_like(dg_ref)

        dh = jnp.zeros((tm, D), F32)
        for j in range(12):
            dh = dh + _dot_nt(dp_ref[:, j * CN:(j + 1) * CN], w_ref[j // 3, :, (j % 3) * CN:(j % 3 + 1) * CN])
        x = x_ref[...]
        rstd = lax.rsqrt(jnp.mean(x * x, axis=-1, keepdims=True) + EPS)
        xh = x * rstd
        dg_ref[...] += jnp.sum(dh * xh, axis=0, keepdims=True)
        dxh = dh * g_ref[...]
        dx_ref[...] = dx1_ref[...] + rstd * (dxh - xh * jnp.mean(dxh * xh, axis=-1, keepdims=True))

    row = lambda i: (i, 0)
    return pl.pallas_call(
        body, name="mix_in_bwd", grid=(T // tm,),
        in_specs=[pl.BlockSpec((tm, D), row), pl.BlockSpec((tm, 6 * D), row),
                  pl.BlockSpec((tm, D), row), _res((1, D)), _res((NSHARD, D, CS), lambda i: (0, layer, 0))],
        out_specs=[pl.BlockSpec((tm, D), row), pl.BlockSpec((1, D), lambda i: (0, 0))],
        out_shape=[_sds((T, D), F32), _sds((1, D), F32)],
        compiler_params=_cparams(("arbitrary",)),
    )(dx1, dp3, x2d, g_mix, w_in_g)


def _tn_matmul(name, a, a_block, a_map, bs, b_block, b_map, out_shape, out_block, out_map, nj, nt):
    kk = [d for d in a_block if d is not None][-1]
    nn = [d for d in b_block if d is not None][-1]
    nb = len(bs)

    def body(*refs):
        a_ref, b_refs = refs[0], refs[1:1 + nb]
        o_refs, acc_refs = refs[-2 * nb:-nb], refs[-nb:]
        t = pl.program_id(1)

        @pl.when(t == 0)
        def _():
            for acc_ref in acc_refs:
                acc_ref[...] = jnp.zeros_like(acc_ref)

        a_t = a_ref[...].astype(BF16)
        for b_ref, acc_ref in zip(b_refs, acc_refs):
            acc_ref[...] += _dot_tn(a_t, b_ref[...].astype(BF16))

        @pl.when(t == nt - 1)
        def _():
            for o_ref, acc_ref in zip(o_refs, acc_refs):
                o_ref[...] = acc_ref[...].astype(o_ref.dtype)

    return pl.pallas_call(
        body, name=name, grid=(nj, nt),
        in_specs=[pl.BlockSpec(a_block, a_map)] + [pl.BlockSpec(b_block, b_map)] * nb,
        out_specs=[pl.BlockSpec(out_block, out_map)] * nb, out_shape=[_sds(out_shape, BF16)] * nb,
        scratch_shapes=[pltpu.VMEM((kk, nn), F32)] * nb,
        compiler_params=_cparams(("parallel", "arbitrary")),
    )(a, *bs)


def _place_shard(name, w, pos, dtype, tr):
    R, C = w.shape

    def body(pos_ref, w_ref, o_ref):
        del pos_ref
        o_ref[...] = w_ref[...].astype(dtype)

    grid_spec = pltpu.PrefetchScalarGridSpec(
        num_scalar_prefetch=1, grid=(R // tr,),
        in_specs=[pl.BlockSpec((tr, C), lambda r, pos: (r, 0))],
        out_specs=pl.BlockSpec((None, tr, C), lambda r, pos: (pos[1], r, 0)))
    return pl.pallas_call(body, name=name, grid_spec=grid_spec, out_shape=_sds((NSHARD, R, C), dtype),
                          compiler_params=_cparams(("parallel",)))(pos, w)


def _add_halves(name, g, rbuf, pos, tr):
    NS, _, H, C = g.shape

    def body(pos_ref, g_ref, r_ref, o_ref):
        del pos_ref
        o_ref[...] = (g_ref[...].astype(F32) + r_ref[...].astype(F32)).astype(BF16)

    grid_spec = pltpu.PrefetchScalarGridSpec(
        num_scalar_prefetch=1, grid=(NS, H // tr),
        in_specs=[pl.BlockSpec((None, None, tr, C), lambda s, r, pos: (s, pos[0], r, 0)),
                  pl.BlockSpec((None, tr, C), lambda s, r, pos: (s, r, 0))],
        out_specs=pl.BlockSpec((None, tr, C), lambda s, r, pos: (s, r, 0)))
    return pl.pallas_call(body, name=name, grid_spec=grid_spec, out_shape=_sds((NS, H, C), BF16),
                          compiler_params=_cparams(("parallel", "parallel")))(pos, g, rbuf)


def _add_shards(name, p, rbuf, pos, tr, layer, n_layers, prev):
    _, H, C = p.shape

    def body(pos_ref, p_ref, r_ref, *rest):
        del pos_ref
        o_ref = rest[-1]
        acc = p_ref[...].astype(F32)
        for j in range(3):
            acc = acc + r_ref[j].astype(F32)
        o_ref[...] = acc

    in_specs = [pl.BlockSpec((None, tr, C), lambda r, pos: (pos[1], r, 0)),
                pl.BlockSpec((3, tr, C), lambda r, pos: (0, r, 0))]
    args = [pos, p, rbuf]
    aliases = {}
    if prev is not None:
        in_specs.append(pl.BlockSpec(memory_space=pl.ANY))
        args.append(prev)
        aliases = {3: 0}
    grid_spec = pltpu.PrefetchScalarGridSpec(
        num_scalar_prefetch=1, grid=(H // tr,), in_specs=in_specs,
        out_specs=pl.BlockSpec((None, None, tr, C), lambda r, pos: (layer, pos[0], r, 0)))
    return pl.pallas_call(body, name=name, grid_spec=grid_spec, out_shape=_sds((n_layers, 2, H, C), F32),
                          input_output_aliases=aliases, compiler_params=_cparams(("parallel",)))(*args)


def _sum_slots(buf, tr):
    NS8, R, C = buf.shape

    def body(b_ref, o_ref):
        acc = b_ref[0]
        for j in range(1, NS8):
            acc = acc + b_ref[j]
        o_ref[...] = acc

    return pl.pallas_call(
        body, name="sum_slots", grid=(R // tr,),
        in_specs=[pl.BlockSpec((NS8, tr, C), lambda i: (0, i, 0))],
        out_specs=pl.BlockSpec((tr, C), lambda i: (i, 0)), out_shape=_sds((R, C), F32),
        compiler_params=_cparams(("parallel",)))(buf)


def _adamw(name, w, g, m, v, tr):
    R, C = w.shape
    c1 = 1.0 - ADAM_B1 ** ADAM_STEP
    c2 = 1.0 - ADAM_B2 ** ADAM_STEP

    def body(w_ref, g_ref, m_ref, v_ref, d_ref, mo_ref, vo_ref):
        g_ = g_ref[...]
        m_ = ADAM_B1 * m_ref[...] + (1.0 - ADAM_B1) * g_
        v_ = ADAM_B2 * v_ref[...] + (1.0 - ADAM_B2) * (g_ * g_)
        mo_ref[...] = m_
        vo_ref[...] = v_
        d_ref[...] = -ADAM_LR * ((m_ / c1) / (jnp.sqrt(v_ / c2) + ADAM_EPS) + ADAM_WD * w_ref[...])

    spec = pl.BlockSpec((tr, C), lambda i: (i, 0))
    return pl.pallas_call(
        body, name=name, grid=(R // tr,), in_specs=[spec] * 4, out_specs=[spec] * 3,
        out_shape=[_sds((R, C), F32)] * 3, compiler_params=_cparams(("parallel",)))(w, g, m, v)


def _row_tile(rows, cap):
    best = rows
    for t in range(8, min(rows, cap) + 1, 8):
        if rows % t == 0:
            best = t
    return best


HBM_SPEC = pl.BlockSpec(memory_space=pltpu.HBM)
SEM_SPEC = pl.BlockSpec(memory_space=pltpu.SEMAPHORE)
DATAFLOW = pltpu.SideEffectType.DATAFLOW_SIDE_EFFECTING
DMA_SEM = pltpu.SemaphoreType.DMA


def _hbm(a):
    return pltpu.with_memory_space_constraint(a, pltpu.HBM)


def _mesh_pos():
    return lax.axis_index("x"), lax.axis_index("y"), lax.axis_index("c")


def _other_chips(x, y):
    return [(1 - x, y), (x, 1 - y), (1 - x, 1 - y)]


def _half_rows(buf, shard, core):
    h = buf.shape[1] // 2
    return buf.at[shard, pl.ds(core * h, h), :]


def _ici_copy(buf, j, send, recv, landing):
    x, y, c = _mesh_pos()
    px, py = _other_chips(x, y)[j]
    part = _half_rows(buf, 2 * px + py if landing else 2 * x + y, c)
    return pltpu.make_async_remote_copy(src_ref=part, dst_ref=part, send_sem=send, recv_sem=recv,
                                        device_id=(px, py, c), device_id_type=MESH_ID)


def _sibling_copy(buf, j, send, recv, landing):
    x, y, c = _mesh_pos()
    px, py = _other_chips(x, y)[j]
    part = _half_rows(buf, 2 * px + py, 1 - c if landing else c)
    return pltpu.make_async_remote_copy(src_ref=part, dst_ref=part, send_sem=send, recv_sem=recv,
                                        device_id=(x, y, 1 - c), device_id_type=MESH_ID)


def _forward_sibling(name, bufs, with_ici):
    n = len(bufs)

    def body(*refs):
        ins = refs[:n]
        send_ici, recv_ici, send_d2d, recv_d2d = refs[2 * n:]
        sends = []
        if with_ici:
            for i in range(n):
                for j in range(3):
                    cp = _ici_copy(ins[i], j, send_ici.at[i, j], recv_ici.at[i, j], False)
                    cp.start()
                    sends.append(cp)
        for i in range(n):
            for j in range(3):
                if with_ici:
                    _ici_copy(ins[i], j, send_ici.at[i, j], recv_ici.at[i, j], True).wait_recv()
                cp = _sibling_copy(ins[i], j, send_d2d.at[i, j], recv_d2d.at[i, j], False)
                cp.start()
                sends.append(cp)
        for i in range(n):
            for j in range(3):
                _sibling_copy(ins[i], j, send_d2d.at[i, j], recv_d2d.at[i, j], True).wait_recv()
        for cp in sends:
            cp.wait_send()

    return pl.pallas_call(
        body, name=name, in_specs=[HBM_SPEC] * n, out_specs=[HBM_SPEC] * n,
        out_shape=[_sds(b.shape, b.dtype) for b in bufs],
        scratch_shapes=[DMA_SEM((n, 3))] * 4, input_output_aliases={i: i for i in range(n)},
    )(*bufs)


def _gather_start(groups):
    flat = [b for g in groups for b in g]
    n, ng = len(flat), len(groups)

    def body(*refs):
        ins, sems, token = refs[:n], refs[n:n + 2 * ng], refs[-1]
        k = 0
        for gi, g in enumerate(groups):
            for a in range(len(g)):
                for j in range(3):
                    _ici_copy(ins[k], j, sems[2 * gi], sems[2 * gi + 1], False).start()
                k += 1
        token[...] = jnp.zeros_like(token)

    res = pl.pallas_call(
        body, name="gather_start", in_specs=[HBM_SPEC] * n,
        out_specs=[SEM_SPEC] * (2 * ng) + [HBM_SPEC] * n + [pl.BlockSpec(memory_space=pltpu.VMEM)],
        out_shape=[DMA_SEM(()) for g in groups for _ in range(2)]
        + [pltpu.HBM(b.shape, b.dtype) for b in flat] + [_sds((8, LANES), F32)],
        input_output_aliases={i: 2 * ng + i for i in range(n)},
        compiler_params=pltpu.CompilerParams(has_side_effects=DATAFLOW),
    )(*[_hbm(b) for b in flat])
    sems = [(res[2 * gi], res[2 * gi + 1]) for gi in range(ng)]
    thru, k = [], 2 * ng
    for g in groups:
        thru.append(list(res[k:k + len(g)]))
        k += len(g)
    return sems, thru, res[-1]


def _gather_wait(name, bufs, sems, after):
    n = len(bufs)

    def body(*refs):
        ins, send, recv = refs[:n], refs[n], refs[n + 1]
        for a in range(n):
            for j in range(3):
                _ici_copy(ins[a], j, send, recv, False).wait_send()
                _ici_copy(ins[a], j, send, recv, True).wait_recv()

    return pl.pallas_call(
        body, name=name, in_specs=[HBM_SPEC] * n + [SEM_SPEC, SEM_SPEC, pl.BlockSpec(memory_space=pl.ANY)],
        out_specs=[HBM_SPEC] * n, out_shape=[pltpu.HBM(b.shape, b.dtype) for b in bufs],
        input_output_aliases={i: i for i in range(n)},
        compiler_params=pltpu.CompilerParams(has_side_effects=DATAFLOW),
    )(*bufs, sems[0], sems[1], after)


def _send_sibling_halves(name, arrs):
    n = len(arrs)

    def body(*refs):
        ins, outs = refs[:n], refs[n:2 * n]
        send, recv = refs[2 * n:]
        x, y, c = _mesh_pos()
        cps = []
        for i in range(n):
            cp = pltpu.make_async_remote_copy(
                src_ref=ins[i].at[:, 1 - c], dst_ref=outs[i],
                send_sem=send.at[i], recv_sem=recv.at[i], device_id=(x, y, 1 - c), device_id_type=MESH_ID)
            cp.start()
            cps.append(cp)
        for cp in cps:
            cp.wait()

    return pl.pallas_call(
        body, name=name, in_specs=[HBM_SPEC] * n, out_specs=[HBM_SPEC] * n,
        out_shape=[_sds((a.shape[0],) + a.shape[2:], a.dtype) for a in arrs],
        scratch_shapes=[DMA_SEM((n,)), DMA_SEM((n,))],
    )(*arrs)


def _chip_copy(p, land, j, send, recv):
    x, y, c = _mesh_pos()
    px, py = _other_chips(x, y)[j]
    return pltpu.make_async_remote_copy(src_ref=p.at[2 * px + py], dst_ref=land.at[j], send_sem=send, recv_sem=recv,
                                        device_id=(px, py, c), device_id_type=MESH_ID)


def _chip_send_start(name, ps):
    n = len(ps)
    lands = [lax.empty((3,) + p.shape[1:], p.dtype) for p in ps]

    def body(*refs):
        ins, lnd, send, recv, token = refs[:n], refs[n:2 * n], refs[2 * n], refs[2 * n + 1], refs[-1]
        for i in range(n):
            for j in range(3):
                _chip_copy(ins[i], lnd[i], j, send, recv).start()
        token[...] = jnp.zeros_like(token)

    res = pl.pallas_call(
        body, name=name, in_specs=[HBM_SPEC] * (2 * n),
        out_specs=[SEM_SPEC, SEM_SPEC] + [HBM_SPEC] * (2 * n) + [pl.BlockSpec(memory_space=pltpu.VMEM)],
        out_shape=[DMA_SEM(()), DMA_SEM(())] + [pltpu.HBM(a.shape, a.dtype) for a in ps + lands]
        + [_sds((8, LANES), F32)],
        input_output_aliases={i: 2 + i for i in range(2 * n)},
        compiler_params=pltpu.CompilerParams(has_side_effects=DATAFLOW),
    )(*[_hbm(a) for a in ps + lands])
    return (res[0], res[1]), list(res[2:2 + n]), list(res[2 + n:2 + 2 * n]), res[-1]


def _chip_send_wait(name, ps, lands, sems, after):
    n = len(ps)

    def body(*refs):
        ins, lnd, send, recv = refs[:n], refs[n:2 * n], refs[2 * n], refs[2 * n + 1]
        for i in range(n):
            for j in range(3):
                cp = _chip_copy(ins[i], lnd[i], j, send, recv)
                cp.wait_send()
                cp.wait_recv()

    res = pl.pallas_call(
        body, name=name, in_specs=[HBM_SPEC] * (2 * n) + [SEM_SPEC, SEM_SPEC, pl.BlockSpec(memory_space=pl.ANY)],
        out_specs=[HBM_SPEC] * (2 * n), out_shape=[pltpu.HBM(a.shape, a.dtype) for a in ps + lands],
        input_output_aliases={i: i for i in range(2 * n)},
        compiler_params=pltpu.CompilerParams(has_side_effects=DATAFLOW),
    )(*ps, *lands, sems[0], sems[1], after)
    return list(res[:n]), list(res[n:])


def _join_halves(arrs):
    n = len(arrs)

    def body(*refs):
        bufs = refs[n:2 * n]
        send, recv = refs[2 * n:]
        x, y, c = _mesh_pos()
        cps = []
        for i in range(n):
            mine = bufs[i].at[:, c]
            cp = pltpu.make_async_remote_copy(
                src_ref=mine, dst_ref=mine, send_sem=send.at[i], recv_sem=recv.at[i],
                device_id=(x, y, 1 - c), device_id_type=MESH_ID)
            cp.start()
            cps.append(cp)
        for i, cp in enumerate(cps):
            theirs = bufs[i].at[:, 1 - c]
            cp.wait_send()
            pltpu.make_async_remote_copy(
                src_ref=theirs, dst_ref=theirs, send_sem=send.at[i], recv_sem=recv.at[i],
                device_id=(x, y, 1 - c), device_id_type=MESH_ID).wait_recv()

    return pl.pallas_call(
        body, name="join_halves", in_specs=[HBM_SPEC] * n, out_specs=[HBM_SPEC] * n,
        out_shape=[_sds(a.shape, a.dtype) for a in arrs],
        scratch_shapes=[DMA_SEM((n,)), DMA_SEM((n,))], input_output_aliases={i: i for i in range(n)},
    )(*arrs)


def _exchange_all(buf):
    def body(in_ref, out_ref, send, recv, loc):
        x, y, c = _mesh_pos()
        me = 4 * x + 2 * y + c
        lc = pltpu.make_async_copy(in_ref, out_ref.at[me], loc)
        lc.start()
        cps = []
        for k in range(1, 8):
            fx, fy, fc = (k >> 2) & 1, (k >> 1) & 1, k & 1
            peer = (x ^ fx, y ^ fy, c ^ fc)
            cp = pltpu.make_async_remote_copy(
                src_ref=in_ref, dst_ref=out_ref.at[me], send_sem=send.at[k - 1], recv_sem=recv.at[k - 1],
                device_id=peer, device_id_type=MESH_ID)
            cp.start()
            cps.append(cp)
        for k in range(1, 8):
            fx, fy, fc = (k >> 2) & 1, (k >> 1) & 1, k & 1
            peer = (x ^ fx, y ^ fy, c ^ fc)
            src_slot = 4 * peer[0] + 2 * peer[1] + peer[2]
            cps[k - 1].wait_send()
            pltpu.make_async_remote_copy(
                src_ref=in_ref, dst_ref=out_ref.at[src_slot], send_sem=send.at[k - 1], recv_sem=recv.at[k - 1],
                device_id=peer, device_id_type=MESH_ID).wait_recv()
        lc.wait()

    sem = pltpu.SemaphoreType.DMA
    return pl.pallas_call(
        body, name="exchange_all", in_specs=[HBM_SPEC], out_specs=HBM_SPEC,
        out_shape=_sds((8,) + buf.shape, buf.dtype),
        scratch_shapes=[sem((7,)), sem((7,)), sem(())],
    )(buf)


def _pad_to(a, axis, size):
    pad = [(0, 0)] * a.ndim
    pad[axis] = (0, size - a.shape[axis])
    return jnp.pad(a, pad)


def _strips(w):
    k, d = w.shape
    return _pad_to(w, 0, 32).reshape(32, d // LANES, LANES).transpose(1, 0, 2)


def kernel(x, norm_mix, w_in, gate_bias, conv_w, conv_b, conv_ln_g, conv_ln_b, w_conv_out, sgu_ln_g, sgu_ln_b, w_spatial, b_spatial, w_sgu_out, w_o, norm_ffn, w_ffn_gate, w_ffn_up, w_ffn_down, norm_final, loss_target, m_norm_mix, m_w_in, m_gate_bias, m_conv_w, m_conv_b, m_conv_ln_g, m_conv_ln_b, m_w_conv_out, m_sgu_ln_g, m_sgu_ln_b, m_w_spatial, m_b_spatial, m_w_sgu_out, m_w_o, m_norm_ffn, m_w_ffn_gate, m_w_ffn_up, m_w_ffn_down, m_norm_final, v_norm_mix, v_w_in, v_gate_bias, v_conv_w, v_conv_b, v_conv_ln_g, v_conv_ln_b, v_w_conv_out, v_sgu_ln_g, v_sgu_ln_b, v_w_spatial, v_b_spatial, v_w_sgu_out, v_w_o, v_norm_ffn, v_w_ffn_gate, v_w_ffn_up, v_w_ffn_down, v_norm_final):
    BL, S, D = x.shape
    T = BL * S
    L = w_in.shape[0]
    CS = w_in.shape[2]
    CN = CS // 3
    DQ = D // NSHARD
    FS = w_ffn_gate.shape[2]
    FP = -(-FS // 256) * 256
    G, CH = w_spatial.shape[1], w_spatial.shape[2]
    KW = conv_w.shape[1]
    CQ = conv_w.shape[3]
    NSTR = D // LANES
    tm = min(512, S // 2)
    tm2 = max(tm // 2, CH)
    rb = min(64, tm)
    mx, my, mc = _mesh_pos()
    pos = jnp.stack([mc, 2 * mx + my]).astype(jnp.int32)

    def placed(name, w, dtype=BF16):
        return _place_shard("place_" + name, w, pos, dtype, _row_tile(w.shape[0], 256))

    wts = []
    for l in range(L):
        w_sq = jnp.concatenate([w_conv_out[l], w_sgu_out[l], w_o[l]], axis=0)
        wts.append(dict(w_in=placed("w_in", w_in[l]), w_sq=placed("w_sq", w_sq),
                        wg=placed("w_gate", _pad_to(w_ffn_gate[l], 1, FP)),
                        wu=placed("w_up", _pad_to(w_ffn_up[l], 1, FP)),
                        wd=placed("w_down", _pad_to(w_ffn_down[l], 0, FP))))
    cw_p = placed("conv_w", _pad_to(conv_w.reshape(L, KW, CQ), 1, 32).reshape(L * 32, CQ), F32)

    wts[0]["w_in"], cw_g = _forward_sibling("gather_first", [wts[0]["w_in"], cw_p], True)
    conv_w_full = cw_g.reshape(NSHARD, L, 32, CQ).transpose(1, 2, 0, 3).reshape(L, 32, D)[:, :KW]
    ffn_keys = ["wg", "wu", "wd"]
    order = [[(0, "w_sq")], [(0, k) for k in ffn_keys]]
    order += [[(l, k) for k in ["w_in", "w_sq"] + ffn_keys] for l in range(1, L)]
    gsems, flying, token = _gather_start([[wts[l][k] for l, k in grp] for grp in order])

    def land(gi, after):
        bufs = _gather_wait("gather_wait_%d" % gi, flying[gi], gsems[gi], after)
        bufs = _forward_sibling("gather_forward_%d" % gi, bufs, False)
        for (l, k), b in zip(order[gi], bufs):
            wts[l][k] = b

    x2d = x.reshape(T, D)
    tgt = loss_target.reshape(T, D)
    row = lambda a, l: a[l].reshape(1, -1)

    saved = []
    xc = x2d
    for l in range(L):
        ws_b = w_spatial[l].astype(BF16)
        bs_b = jnp.repeat(b_spatial[l].T, D // G, axis=1)
        cw_s = _strips(conv_w_full[l])
        h, proj = _mix_in_fwd(xc, row(norm_mix, l), wts[l]["w_in"], 0, tm, token if l == 0 else None)
        if l == 0:
            land(0, h)
        c1h, rstd_c, c3, ya = _conv_fwd(proj, cw_s, row(conv_b, l), row(conv_ln_g, l), row(conv_ln_b, l),
                                        wts[l]["w_sq"], 0, S, tm, rb)
        if l == 0:
            land(1, ya)
        mixed, gated, yb, merged, x1 = _sgu_merge_fwd(proj, ya, xc, row(sgu_ln_g, l), row(sgu_ln_b, l), ws_b, bs_b,
                                                      row(gate_bias, l), wts[l]["w_sq"], 0, tm2)
        h2, gt, up, act, x2 = _ffn_fwd(x1, row(norm_ffn, l), wts[l]["wg"], wts[l]["wu"], wts[l]["wd"], 0, tm2)
        if l + 1 < L:
            land(l + 2, x2)
        saved.append(dict(x=xc, h=h, proj=proj, c1h=c1h, rstd_c=rstd_c, c3=c3, ya=ya, mixed=mixed, gated=gated,
                          yb=yb, merged=merged, x1=x1, h2=h2, gt=gt, up=up, act=act, ws_b=ws_b, cw=conv_w_full[l]))
        xc = x2

    dx, loss_part, d_norm_final = _loss_head(xc, norm_final.reshape(1, D), tgt, tm)
    loss = lax.psum(loss_part[0, 0], ("x", "y", "c"))

    g_acc = {}

    def reduce_start(tag, layer, named):
        arrs = [g.reshape(NSHARD, 2, g.shape[1] // 2, g.shape[2]) for _, g in named]
        from_sib = _send_sibling_halves("sibling_" + tag, arrs)
        ps = [_add_halves("presum_" + nm, a, r, pos, _row_tile(a.shape[2], 256))
              for (nm, _), a, r in zip(named, arrs, from_sib)]
        sems, ps, lands, tok = _chip_send_start("chip_send_start_" + tag, ps)
        return dict(tag=tag, layer=layer, names=[nm for nm, _ in named], ps=ps, lands=lands, sems=sems), tok

    def reduce_finish(pend, after):
        ps, lands = _chip_send_wait("chip_send_wait_" + pend["tag"], pend["ps"], pend["lands"], pend["sems"], after)
        for nm, p, r in zip(pend["names"], ps, lands):
            g_acc[nm] = _add_shards("shardsum_" + nm, p, r, pos, _row_tile(p.shape[1], 256), pend["layer"], L,
                                    g_acc.get(nm))

    small = [None] * L
    tt = min(1024, T // 2)
    nt = T // tt
    pending, tok = None, None
    for l in reversed(range(L)):
        sv, wt = saved[l], wts[l]
        dx1, dgt, dup, d_norm_ffn = _ffn_bwd(dx, sv["x1"], sv["gt"], sv["up"], row(norm_ffn, l), wt["wg"], wt["wu"],
                                             wt["wd"], 0, tm2, tok)
        g_g, g_u = _tn_matmul("grad_w_gate_up", sv["h2"], (tt, D), lambda j, t: (t, 0), [dgt, dup], (tt, FP),
                              lambda j, t: (t, j), (NSHARD, D, FP), (None, D, FP), lambda j, t: (j, 0, 0), NSHARD, nt)
        g_d, = _tn_matmul("grad_w_down", sv["act"], (tt, FP), lambda j, t: (t, j), [dx], (tt, D), lambda j, t: (t, 0),
                          (NSHARD, FP, D), (None, FP, D), lambda j, t: (j, 0, 0), NSHARD, nt)
        if pending is not None:
            reduce_finish(pending, g_d)
        ffn_pend, tok = reduce_start("ffn%d" % l, l, [("w_ffn_gate", g_g), ("w_ffn_up", g_u), ("w_ffn_down", g_d)])
        wst_b = jnp.swapaxes(sv["ws_b"], 1, 2)
        dya, dyb, dp3, d_gate_bias, d_sgu_g, d_sgu_b, d_bs, d_ws = _merge_sgu_bwd(
            dx1, sv["proj"], sv["ya"], sv["yb"], sv["mixed"], row(sgu_ln_g, l), row(sgu_ln_b, l), sv["ws_b"], wst_b,
            row(gate_bias, l), wt["w_sq"], 0, tm2, tok)
        sq_args = ((tt, D), lambda j, t: (t, 0))
        sq_out = ((D, D), (D, D), lambda j, t: (0, 0), 1, nt)
        g_o, = _tn_matmul("grad_w_o", sv["merged"], *sq_args, [dx1], *sq_args, *sq_out)
        g_so, = _tn_matmul("grad_w_sgu_out", sv["gated"], *sq_args, [dyb], *sq_args, *sq_out)
        g_co, = _tn_matmul("grad_w_conv_out", sv["c3"], *sq_args, [dya], *sq_args, *sq_out)
        dc1, d_cln_g, d_cln_b, d_conv_b = _conv_ln_bwd(dya, sv["c1h"], sv["rstd_c"], row(conv_ln_g, l),
                                                       row(conv_ln_b, l), wt["w_sq"], 0, tm)
        dp3, d_cw_s = _conv_bwd(dc1, sv["proj"], dp3, _strips(sv["cw"][::-1]), S, tm, rb)
        g_in, = _tn_matmul("grad_w_in", sv["h"], (tt, D), lambda j, t: (t, 0), [dp3], (tt, CS), lambda j, t: (t, j),
                           (NSHARD, D, CS), (None, D, CS), lambda j, t: (j, 0, 0), NSHARD, nt)
        dx, d_norm_mix = _mix_in_bwd(dx1, dp3, sv["x"], row(norm_mix, l), wt["w_in"], 0, tm)
        reduce_finish(ffn_pend, dx)
        pending, tok = reduce_start("mix%d" % l, l, [
            ("w_in", g_in), ("w_conv_out", g_co.reshape(NSHARD, DQ, D)), ("w_sgu_out", g_so.reshape(NSHARD, DQ, D)),
            ("w_o", g_o.reshape(NSHARD, DQ, D))])
        d_cw = d_cw_s.transpose(1, 0, 2).reshape(32, D)
        small[l] = [d_norm_mix, d_gate_bias.reshape(2, D), d_cw, d_conv_b, d_cln_g, d_cln_b, d_sgu_g, d_sgu_b,
                    d_ws.reshape(G * CH * CH // D, D), d_bs.reshape(G * CH // D, D), d_norm_ffn]
    reduce_finish(pending, tok)
    grad_x = dx.reshape(BL, S, D)

    names = ["w_in", "w_conv_out", "w_sgu_out", "w_o", "w_ffn_gate", "w_ffn_up", "w_ffn_down"]
    g_full = _join_halves([g_acc[nm] for nm in names])
    g_w_in, g_w_co, g_w_so, g_w_o, g_w_g, g_w_u, g_w_d = [g.reshape(L, 2 * g.shape[2], g.shape[3]) for g in g_full]
    g_w_g = g_w_g[:, :, :FS]
    g_w_u = g_w_u[:, :, :FS]
    g_w_d = g_w_d[:, :FS, :]

    pieces = [p for l in range(L) for p in small[l]] + [d_norm_final]
    packed = jnp.concatenate(pieces, axis=0)
    n_rows = packed.shape[0]
    n_pad = -(-n_rows // 8) * 8
    packed = _pad_to(packed, 0, n_pad)
    summed = _sum_slots(_exchange_all(packed), 8)
    off = 0
    sg = []
    for l in range(L):
        cur = []
        for p in small[l]:
            cur.append(summed[off:off + p.shape[0]])
            off += p.shape[0]
        sg.append(cur)
    g_norm_final = summed[off]

    def per_layer(k, shape):
        return jnp.stack([sg[l][k] for l in range(L)]).reshape(shape)

    g_norm_mix = per_layer(0, (L, D))
    g_gate_bias = per_layer(1, (L, 2 * D))
    g_conv_w_full = jnp.stack([sg[l][2][:KW] for l in range(L)])
    g_conv_w = lax.dynamic_slice_in_dim(g_conv_w_full, (2 * mx + my) * CQ, CQ, axis=2).reshape(L, KW, 1, CQ)
    g_conv_b = per_layer(3, (L, D))
    g_conv_ln_g = per_layer(4, (L, D))
    g_conv_ln_b = per_layer(5, (L, D))
    g_sgu_ln_g = per_layer(6, (L, D))
    g_sgu_ln_b = per_layer(7, (L, D))
    g_w_spatial = per_layer(8, (L, G, CH, CH))
    g_b_spatial = per_layer(9, (L, G, CH))
    g_norm_ffn = per_layer(10, (L, D))

    grads = [g_norm_mix, g_w_in, g_gate_bias, g_conv_w, g_conv_b, g_conv_ln_g, g_conv_ln_b, g_w_co, g_sgu_ln_g,
             g_sgu_ln_b, g_w_spatial, g_b_spatial, g_w_so, g_w_o, g_norm_ffn, g_w_g, g_w_u, g_w_d, g_norm_final]
    weights = [norm_mix, w_in, gate_bias, conv_w, conv_b, conv_ln_g, conv_ln_b, w_conv_out, sgu_ln_g, sgu_ln_b,
               w_spatial, b_spatial, w_sgu_out, w_o, norm_ffn, w_ffn_gate, w_ffn_up, w_ffn_down, norm_final]
    ms = [m_norm_mix, m_w_in, m_gate_bias, m_conv_w, m_conv_b, m_conv_ln_g, m_conv_ln_b, m_w_conv_out, m_sgu_ln_g,
          m_sgu_ln_b, m_w_spatial, m_b_spatial, m_w_sgu_out, m_w_o, m_norm_ffn, m_w_ffn_gate, m_w_ffn_up,
          m_w_ffn_down, m_norm_final]
    vs = [v_norm_mix, v_w_in, v_gate_bias, v_conv_w, v_conv_b, v_conv_ln_g, v_conv_ln_b, v_w_conv_out, v_sgu_ln_g,
          v_sgu_ln_b, v_w_spatial, v_b_spatial, v_w_sgu_out, v_w_o, v_norm_ffn, v_w_ffn_gate, v_w_ffn_up,
          v_w_ffn_down, v_norm_final]

    big_idx = [1, 7, 12, 13, 15, 16, 17]
    deltas, new_m, new_v = [None] * 19, [None] * 19, [None] * 19
    for k in big_idx:
        shp = weights[k].shape
        r2 = (shp[0] * shp[1], shp[2])
        d_, m_, v_ = _adamw("adamw_" + str(k), weights[k].reshape(r2), grads[k].reshape(r2), ms[k].reshape(r2),
                            vs[k].reshape(r2), _row_tile(r2[0], 256))
        deltas[k], new_m[k], new_v[k] = d_.reshape(shp), m_.reshape(shp), v_.reshape(shp)
    small_idx = [k for k in range(19) if k not in big_idx]

    def pack(arrs):
        flat = jnp.concatenate([arrs[k].reshape(-1) for k in small_idx])
        rows = -(-flat.shape[0] // (256 * LANES)) * 256
        return _pad_to(flat, 0, rows * LANES).reshape(rows, LANES)

    pw, pg, pm, pv = pack(weights), pack(grads), pack(ms), pack(vs)
    d_, m_, v_ = _adamw("adamw_small", pw, pg, pm, pv, _row_tile(pw.shape[0], 512))
    off = 0
    for k in small_idx:
        n_el = weights[k].size
        shp = weights[k].shape
        deltas[k] = d_.reshape(-1)[off:off + n_el].reshape(shp)
        new_m[k] = m_.reshape(-1)[off:off + n_el].reshape(shp)
        new_v[k] = v_.reshape(-1)[off:off + n_el].reshape(shp)
        off += n_el

    return (loss, grad_x, *grads, *deltas, *new_m, *new_v)
```

```python
import functools

import jax
import jax.numpy as jnp
from jax import lax
from jax.experimental import pallas as pl
from jax.experimental.pallas import tpu as pltpu

F32 = jnp.float32
BF16 = jnp.bfloat16
EPS = 1e-6
ADAM_LR = 0.001
ADAM_B1 = 0.9
ADAM_B2 = 0.999
ADAM_EPS = 1e-08
ADAM_WD = 0.01
ADAM_STEP = 10

NSHARD = 4
LANES = 128
HALO = 16
VMEM_LIMIT = 60 * 1024 * 1024
MESH_ID = pl.DeviceIdType.MESH


def _dot(a, b):
    return jnp.dot(a, b, preferred_element_type=F32)


def _dot_nt(a, b):
    return lax.dot_general(a, b, (((1,), (1,)), ((), ())), preferred_element_type=F32)


def _dot_tn(a, b):
    return lax.dot_general(a, b, (((0,), (0,)), ((), ())), preferred_element_type=F32)


def _sig(z):
    return 1.0 / (1.0 + jnp.exp(-z))


def _res(shape, imap=None):
    nd = len(shape)
    if imap is None:
        imap = lambda *_: (0,) * nd
    return pl.BlockSpec(shape, imap, pipeline_mode=pl.Buffered(1))


def _cparams(sem):
    return pltpu.CompilerParams(dimension_semantics=sem, vmem_limit_bytes=VMEM_LIMIT)


def _sds(shape, dtype):
    return jax.ShapeDtypeStruct(shape, dtype)


def _after(body, n_in, dep):
    deps = [] if dep is None else [d for d in (dep if isinstance(dep, (list, tuple)) else [dep]) if d is not None]
    if not deps:
        return body, [], []

    def wrapped(*refs):
        return body(*refs[:n_in], *refs[n_in + len(deps):])

    return wrapped, [pl.BlockSpec(memory_space=pl.ANY)] * len(deps), deps


def _mix_in_fwd(x2d, g_mix, w_in_g, layer, tm, dep=None):
    T, D = x2d.shape
    CS = w_in_g.shape[2]
    CN = CS // 3

    def body(x_ref, g_ref, w_ref, h_ref, p_ref):
        x = x_ref[...]
        rstd = lax.rsqrt(jnp.mean(x * x, axis=-1, keepdims=True) + EPS)
        h = (x * rstd * g_ref[...]).astype(BF16)
        h_ref[...] = h
        for s in range(NSHARD):
            for j in range(3):
                c0 = s * CS + j * CN
                p_ref[:, c0:c0 + CN] = _dot(h, w_ref[s, :, j * CN:(j + 1) * CN]).astype(BF16)

    body, dep_spec, dep_arg = _after(body, 3, dep)
    return pl.pallas_call(
        body, name="mix_in_fwd", grid=(T // tm,),
        in_specs=[pl.BlockSpec((tm, D), lambda i: (i, 0)), _res((1, D)),
                  _res((NSHARD, D, CS), lambda i: (0, layer, 0))] + dep_spec,
        out_specs=[pl.BlockSpec((tm, D), lambda i: (i, 0)), pl.BlockSpec((tm, NSHARD * CS), lambda i: (i, 0))],
        out_shape=[_sds((T, D), BF16), _sds((T, NSHARD * CS), BF16)],
        compiler_params=_cparams(("parallel",)),
    )(x2d, g_mix, w_in_g, *dep_arg)


def _halo_maps(tm, n_rows):
    nb = tm // HALO
    last = n_rows // HALO - 1
    prev = lambda i: (jnp.maximum(i * nb - 1, 0), 0)
    nxt = lambda i: (jnp.minimum((i + 1) * nb, last), 0)
    return prev, nxt


def _dwconv(pad_ref, w_ref, out_ref, n_strips, tm, kw, rb):
    off = HALO - (kw - 1) // 2

    def strip(cs, carry):
        for r0 in range(0, tm, rb):
            acc = jnp.zeros((rb, LANES), F32)
            for k in range(kw):
                r = r0 + off + k
                acc = acc + w_ref[cs, k:k + 1, :] * pad_ref[cs, r:r + rb, :]
            out_ref[cs, r0:r0 + rb, :] = acc
        return carry

    lax.fori_loop(0, n_strips, strip, 0)


def _fill_c0_pad(pad_ref, pa_ref, pprev_ref, pnext_ref, D, tm, first, last):
    for cs in range(D // LANES):
        lo, hi = cs * LANES, (cs + 1) * LANES

        def c0_of(ref):
            return ref[:, lo:hi].astype(F32) * _sig(ref[:, D + lo:D + hi].astype(F32))

        pad_ref[cs, HALO:HALO + tm, :] = c0_of(pa_ref)
        pad_ref[cs, 0:HALO, :] = jnp.where(first, 0.0, c0_of(pprev_ref))
        pad_ref[cs, HALO + tm:HALO + tm + HALO, :] = jnp.where(last, 0.0, c0_of(pnext_ref))


def _conv_fwd(proj, conv_w_s, conv_b, ln_g, ln_b, w_sq_g, layer, seq, tm, rb):
    T = proj.shape[0]
    D = conv_b.shape[1]
    DQ = D // NSHARD
    NSTR = D // LANES
    KW = 31
    tps = seq // tm
    prev, nxt = _halo_maps(tm, T)

    def body(pa_ref, pprev_ref, pnext_ref, w_ref, b_ref, g_ref, be_ref, wco_ref,
             c1h_ref, rstd_ref, c3_ref, ya_ref, pad_ref, c1s_ref):
        i = pl.program_id(0)
        first = (i % tps) == 0
        last = (i % tps) == tps - 1
        _fill_c0_pad(pad_ref, pa_ref, pprev_ref, pnext_ref, D, tm, first, last)
        _dwconv(pad_ref, w_ref, c1s_ref, NSTR, tm, KW, rb)
        c1 = jnp.concatenate([c1s_ref[cs] for cs in range(NSTR)], axis=1) + b_ref[...]
        mu = jnp.mean(c1, axis=-1, keepdims=True)
        cc = c1 - mu
        rstd = lax.rsqrt(jnp.mean(cc * cc, axis=-1, keepdims=True) + EPS)
        c1h = cc * rstd
        c1h_ref[...] = c1h.astype(BF16)
        rstd_ref[...] = rstd
        c2 = c1h * g_ref[...] + be_ref[...]
        c3 = (c2 * _sig(c2)).astype(BF16)
        c3_ref[...] = c3
        ya_ref[...] = _dot(c3, wco_ref[...].reshape(D, D)).astype(BF16)

    row = lambda i: (i, 0)
    return pl.pallas_call(
        body, name="conv_fwd", grid=(T // tm,),
        in_specs=[pl.BlockSpec((tm, 2 * D), row), pl.BlockSpec((HALO, 2 * D), prev), pl.BlockSpec((HALO, 2 * D), nxt),
                  _res((NSTR, 32, LANES)), _res((1, D)), _res((1, D)), _res((1, D)),
                  _res((NSHARD, DQ, D), lambda i: (0, layer * 3 + 0, 0))],
        out_specs=[pl.BlockSpec((tm, D), row), pl.BlockSpec((tm, 1), row), pl.BlockSpec((tm, D), row),
                   pl.BlockSpec((tm, D), row)],
        out_shape=[_sds((T, D), BF16), _sds((T, 1), F32), _sds((T, D), BF16), _sds((T, D), BF16)],
        scratch_shapes=[pltpu.VMEM((NSTR, tm + 2 * HALO, LANES), F32), pltpu.VMEM((NSTR, tm, LANES), F32)],
        compiler_params=_cparams(("parallel",)),
    )(proj, proj, proj, conv_w_s, conv_b, ln_g, ln_b, w_sq_g)


def _sgu_merge_fwd(proj, ya, x2d, ln_g, ln_b, ws_b, bs_b, gate_bias, w_sq_g, layer, tm):
    T, D = x2d.shape
    DQ = D // NSHARD
    G, CH, _ = ws_b.shape
    GD = D // G

    def body(puv_ref, pg_ref, ya_ref, x_ref, g_ref, be_ref, ws_ref, bsb_ref, gb_ref, wso_ref, wo_ref,
             mixed_ref, gated_ref, yb_ref, merged_ref, x1_ref, mix_scr):
        u = puv_ref[:, :D].astype(F32)
        v = puv_ref[:, D:].astype(F32)
        mu = jnp.mean(v, axis=-1, keepdims=True)
        vc = v - mu
        rstd = lax.rsqrt(jnp.mean(vc * vc, axis=-1, keepdims=True) + EPS)
        vn = (vc * rstd * g_ref[...] + be_ref[...]).astype(BF16)
        for ch in range(tm // CH):
            for g in range(G):
                blk = vn[ch * CH:(ch + 1) * CH, g * GD:(g + 1) * GD]
                mix_scr[ch * CH:(ch + 1) * CH, g * GD:(g + 1) * GD] = (
                    _dot(ws_ref[g], blk) + bsb_ref[:, g * GD:(g + 1) * GD])
        mixed = mix_scr[...]
        mixed_ref[...] = mixed.astype(BF16)
        gated = (u * mixed).astype(BF16)
        gated_ref[...] = gated
        yb = _dot(gated, wso_ref[...].reshape(D, D))
        yb_ref[...] = yb.astype(BF16)
        sa = _sig(pg_ref[:, :D].astype(F32) + gb_ref[:, :D])
        sb = _sig(pg_ref[:, D:].astype(F32) + gb_ref[:, D:])
        merged = (sa * ya_ref[...].astype(F32) + sb * yb).astype(BF16)
        merged_ref[...] = merged
        x1_ref[...] = x_ref[...] + _dot(merged, wo_ref[...].reshape(D, D))

    row = lambda i: (i, 0)
    return pl.pallas_call(
        body, name="sgu_merge_fwd", grid=(T // tm,),
        in_specs=[pl.BlockSpec((tm, 2 * D), lambda i: (i, 1)), pl.BlockSpec((tm, 2 * D), lambda i: (i, 2)),
                  pl.BlockSpec((tm, D), row), pl.BlockSpec((tm, D), row),
                  _res((1, D)), _res((1, D)), _res((G, CH, CH)), _res((CH, D)), _res((1, 2 * D)),
                  _res((NSHARD, DQ, D), lambda i: (0, layer * 3 + 1, 0)),
                  _res((NSHARD, DQ, D), lambda i: (0, layer * 3 + 2, 0))],
        out_specs=[pl.BlockSpec((tm, D), row)] * 5,
        out_shape=[_sds((T, D), BF16)] * 4 + [_sds((T, D), F32)],
        scratch_shapes=[pltpu.VMEM((tm, D), F32)],
        compiler_params=_cparams(("parallel",)),
    )(proj, proj, ya, x2d, ln_g, ln_b, ws_b, bs_b, gate_bias, w_sq_g, w_sq_g)


def _ffn_fwd(x1, g_ffn, wg_g, wu_g, wd_g, layer, tm):
    T, D = x1.shape
    FP = wg_g.shape[2]
    F = NSHARD * FP

    def body(x_ref, g_ref, wg_ref, wu_ref, wd_ref, h2_ref, gt_ref, up_ref, act_ref, x2_ref):
        x = x_ref[...]
        rstd = lax.rsqrt(jnp.mean(x * x, axis=-1, keepdims=True) + EPS)
        h2 = (x * rstd * g_ref[...]).astype(BF16)
        h2_ref[...] = h2
        acc = x
        for s in range(NSHARD):
            gt = _dot(h2, wg_ref[s])
            up = _dot(h2, wu_ref[s])
            gt_ref[:, s * FP:(s + 1) * FP] = gt.astype(BF16)
            up_ref[:, s * FP:(s + 1) * FP] = up.astype(BF16)
            act = (gt * _sig(gt) * up).astype(BF16)
            act_ref[:, s * FP:(s + 1) * FP] = act
            acc = acc + _dot(act, wd_ref[s])
        x2_ref[...] = acc

    row = lambda i: (i, 0)
    return pl.pallas_call(
        body, name="ffn_fwd", grid=(T // tm,),
        in_specs=[pl.BlockSpec((tm, D), row), _res((1, D)),
                  _res((NSHARD, D, FP), lambda i: (0, layer, 0)), _res((NSHARD, D, FP), lambda i: (0, layer, 0)),
                  _res((NSHARD, FP, D), lambda i: (0, layer, 0))],
        out_specs=[pl.BlockSpec((tm, D), row), pl.BlockSpec((tm, F), row), pl.BlockSpec((tm, F), row),
                   pl.BlockSpec((tm, F), row), pl.BlockSpec((tm, D), row)],
        out_shape=[_sds((T, D), BF16), _sds((T, F), BF16), _sds((T, F), BF16), _sds((T, F), BF16), _sds((T, D), F32)],
        compiler_params=_cparams(("parallel",)),
    )(x1, g_ffn, wg_g, wu_g, wd_g)


def _loss_head(xf, g_fin, target, tm):
    T, D = xf.shape
    n = T // tm

    def body(x_ref, g_ref, t_ref, dx_ref, loss_ref, dg_ref, acc_ref):
        i = pl.program_id(0)

        @pl.when(i == 0)
        def _():
            acc_ref[...] = jnp.zeros_like(acc_ref)
            dg_ref[...] = jnp.zeros_like(dg_ref)

        x = x_ref[...]
        g = g_ref[...]
        rstd = lax.rsqrt(jnp.mean(x * x, axis=-1, keepdims=True) + EPS)
        xh = x * rstd
        diff = xh * g - t_ref[...]
        acc_ref[...] += jnp.sum(diff * diff, axis=0, keepdims=True)
        dy = diff * (1.0 / D)
        dg_ref[...] += jnp.sum(dy * xh, axis=0, keepdims=True)
        dxh = dy * g
        dx_ref[...] = rstd * (dxh - xh * jnp.mean(dxh * xh, axis=-1, keepdims=True))

        @pl.when(i == n - 1)
        def _():
            tot = jnp.sum(acc_ref[...], axis=-1, keepdims=True) * (0.5 / D)
            loss_ref[...] = jnp.broadcast_to(tot, loss_ref.shape)

    row = lambda i: (i, 0)
    return pl.pallas_call(
        body, name="loss_head", grid=(n,),
        in_specs=[pl.BlockSpec((tm, D), row), _res((1, D)), pl.BlockSpec((tm, D), row)],
        out_specs=[pl.BlockSpec((tm, D), row), pl.BlockSpec((1, LANES), lambda i: (0, 0)),
                   pl.BlockSpec((1, D), lambda i: (0, 0))],
        out_shape=[_sds((T, D), F32), _sds((1, LANES), F32), _sds((1, D), F32)],
        scratch_shapes=[pltpu.VMEM((1, D), F32)],
        compiler_params=_cparams(("arbitrary",)),
    )(xf, g_fin, target)


def _ffn_bwd(dx2, x1, gt, up, g_ffn, wg_g, wu_g, wd_g, layer, tm, dep=None):
    T, D = x1.shape
    FP = wg_g.shape[2]
    F = NSHARD * FP

    def body(dx2_ref, x1_ref, gt_ref, up_ref, g_ref, wg_ref, wu_ref, wd_ref, dx1_ref, dgt_ref, dup_ref, dg_ref):
        i = pl.program_id(0)

        @pl.when(i == 0)
        def _():
            dg_ref[...] = jnp.zeros_like(dg_ref)

        dx2 = dx2_ref[...]
        dx2b = dx2.astype(BF16)
        dh2 = jnp.zeros((tm, D), F32)
        for s in range(NSHARD):
            dact = _dot_nt(dx2b, wd_ref[s])
            g = gt_ref[:, s * FP:(s + 1) * FP].astype(F32)
            u = up_ref[:, s * FP:(s + 1) * FP].astype(F32)
            sg = _sig(g)
            dup = (dact * (g * sg)).astype(BF16)
            dgt = (dact * u * (sg * (1.0 + g * (1.0 - sg)))).astype(BF16)
            dgt_ref[:, s * FP:(s + 1) * FP] = dgt
            dup_ref[:, s * FP:(s + 1) * FP] = dup
            dh2 = dh2 + _dot_nt(dgt, wg_ref[s]) + _dot_nt(dup, wu_ref[s])
        x = x1_ref[...]
        rstd = lax.rsqrt(jnp.mean(x * x, axis=-1, keepdims=True) + EPS)
        xh = x * rstd
        dg_ref[...] += jnp.sum(dh2 * xh, axis=0, keepdims=True)
        dxh = dh2 * g_ref[...]
        dx1_ref[...] = dx2 + rstd * (dxh - xh * jnp.mean(dxh * xh, axis=-1, keepdims=True))

    row = lambda i: (i, 0)
    body, dep_spec, dep_arg = _after(body, 8, dep)
    return pl.pallas_call(
        body, name="ffn_bwd", grid=(T // tm,),
        in_specs=[pl.BlockSpec((tm, D), row), pl.BlockSpec((tm, D), row), pl.BlockSpec((tm, F), row),
                  pl.BlockSpec((tm, F), row), _res((1, D)),
                  _res((NSHARD, D, FP), lambda i: (0, layer, 0)), _res((NSHARD, D, FP), lambda i: (0, layer, 0)),
                  _res((NSHARD, FP, D), lambda i: (0, layer, 0))] + dep_spec,
        out_specs=[pl.BlockSpec((tm, D), row), pl.BlockSpec((tm, F), row), pl.BlockSpec((tm, F), row),
                   pl.BlockSpec((1, D), lambda i: (0, 0))],
        out_shape=[_sds((T, D), F32), _sds((T, F), BF16), _sds((T, F), BF16), _sds((1, D), F32)],
        compiler_params=_cparams(("arbitrary",)),
    )(dx2, x1, gt, up, g_ffn, wg_g, wu_g, wd_g, *dep_arg)


def _merge_sgu_bwd(dx1, proj, ya, yb, mixed, ln_g, ln_b, ws_b, wst_b, gate_bias, w_sq_g, layer, tm, dep=None):
    T, D = dx1.shape
    DQ = D // NSHARD
    G, CH, _ = ws_b.shape
    GD = D // G

    def body(dx1_ref, puv_ref, pg_ref, ya_ref, yb_ref, mixed_ref, g_ref, be_ref, ws_ref, wst_ref, gb_ref,
             wso_ref, wo_ref, dya_ref, dyb_ref, dp_ref, dgb_ref, dlg_ref, dlb_ref, dbs_ref, dws_ref,
             dvn_scr, dbs_scr):
        i = pl.program_id(0)

        @pl.when(i == 0)
        def _():
            for r in (dgb_ref, dlg_ref, dlb_ref, dws_ref, dbs_scr):
                r[...] = jnp.zeros_like(r)

        dmerged = _dot_nt(dx1_ref[...].astype(BF16), wo_ref[...].reshape(D, D))
        sa = _sig(pg_ref[:, :D].astype(F32) + gb_ref[:, :D])
        sb = _sig(pg_ref[:, D:].astype(F32) + gb_ref[:, D:])
        dya = (dmerged * sa).astype(BF16)
        dyb = (dmerged * sb).astype(BF16)
        dya_ref[...] = dya
        dyb_ref[...] = dyb
        dga = dmerged * ya_ref[...].astype(F32) * (sa * (1.0 - sa))
        dgb = dmerged * yb_ref[...].astype(F32) * (sb * (1.0 - sb))
        dp_ref[:, 4 * D:5 * D] = dga.astype(BF16)
        dp_ref[:, 5 * D:6 * D] = dgb.astype(BF16)
        dgb_ref[:, :D] += jnp.sum(dga, axis=0, keepdims=True)
        dgb_ref[:, D:] += jnp.sum(dgb, axis=0, keepdims=True)

        dgated = _dot_nt(dyb, wso_ref[...].reshape(D, D))
        u = puv_ref[:, :D].astype(F32)
        v = puv_ref[:, D:].astype(F32)
        dp_ref[:, 2 * D:3 * D] = (dgated * mixed_ref[...].astype(F32)).astype(BF16)
        dmixed = dgated * u
        mu = jnp.mean(v, axis=-1, keepdims=True)
        vc = v - mu
        rstd = lax.rsqrt(jnp.mean(vc * vc, axis=-1, keepdims=True) + EPS)
        vh = vc * rstd
        vn = (vh * g_ref[...] + be_ref[...]).astype(BF16)
        dmb = dmixed.astype(BF16)
        bs_part = jnp.zeros((CH, D), F32)
        for ch in range(tm // CH):
            rows = slice(ch * CH, (ch + 1) * CH)
            bs_part = bs_part + dmixed[rows, :]
            for g in range(G):
                cols = slice(g * GD, (g + 1) * GD)
                dws_ref[g] += _dot_nt(dmb[rows, cols], vn[rows, cols])
                dvn_scr[rows, cols] = _dot(wst_ref[g], dmb[rows, cols])
        dbs_scr[...] += bs_part
        dvn = dvn_scr[...]
        dlg_ref[...] += jnp.sum(dvn * vh, axis=0, keepdims=True)
        dlb_ref[...] += jnp.sum(dvn, axis=0, keepdims=True)
        dxh = dvn * g_ref[...]
        dv = rstd * (dxh - jnp.mean(dxh, axis=-1, keepdims=True) - vh * jnp.mean(dxh * vh, axis=-1, keepdims=True))
        dp_ref[:, 3 * D:4 * D] = dv.astype(BF16)

        @pl.when(i == pl.num_programs(0) - 1)
        def _():
            for g in range(G):
                blk = dbs_scr[:, g * GD:(g + 1) * GD]
                if GD != CH:
                    blk = jnp.concatenate([blk, jnp.zeros((CH, CH - GD), F32)], axis=1)
                dbs_ref[:, g * CH:(g + 1) * CH] = jnp.sum(blk.T, axis=0, keepdims=True)

    row = lambda i: (i, 0)
    fixed2 = lambda i: (0, 0)
    body, dep_spec, dep_arg = _after(body, 13, dep)
    return pl.pallas_call(
        body, name="merge_sgu_bwd", grid=(T // tm,),
        in_specs=[pl.BlockSpec((tm, D), row), pl.BlockSpec((tm, 2 * D), lambda i: (i, 1)),
                  pl.BlockSpec((tm, 2 * D), lambda i: (i, 2)), pl.BlockSpec((tm, D), row), pl.BlockSpec((tm, D), row),
                  pl.BlockSpec((tm, D), row), _res((1, D)), _res((1, D)), _res((G, CH, CH)), _res((G, CH, CH)),
                  _res((1, 2 * D)),
                  _res((NSHARD, DQ, D), lambda i: (0, layer * 3 + 1, 0)),
                  _res((NSHARD, DQ, D), lambda i: (0, layer * 3 + 2, 0))] + dep_spec,
        out_specs=[pl.BlockSpec((tm, D), row), pl.BlockSpec((tm, D), row),
                   pl.BlockSpec((tm, 6 * D), row),
                   pl.BlockSpec((1, 2 * D), fixed2), pl.BlockSpec((1, D), fixed2), pl.BlockSpec((1, D), fixed2),
                   pl.BlockSpec((1, G * CH), fixed2), pl.BlockSpec((G, CH, CH), lambda i: (0, 0, 0))],
        out_shape=[_sds((T, D), BF16), _sds((T, D), BF16), _sds((T, 6 * D), BF16),
                   _sds((1, 2 * D), F32), _sds((1, D), F32), _sds((1, D), F32), _sds((1, G * CH), F32),
                   _sds((G, CH, CH), F32)],
        scratch_shapes=[pltpu.VMEM((tm, D), F32), pltpu.VMEM((CH, D), F32)],
        compiler_params=_cparams(("arbitrary",)),
    )(dx1, proj, proj, ya, yb, mixed, ln_g, ln_b, ws_b, wst_b, gate_bias, w_sq_g, w_sq_g, *dep_arg)


def _conv_ln_bwd(dya, c1h, rstd_c, ln_g, ln_b, w_sq_g, layer, tm):
    T, D = dya.shape
    DQ = D // NSHARD

    def body(dya_ref, c1h_ref, rstd_ref, g_ref, be_ref, wco_ref, dc1_ref, dlg_ref, dlb_ref, dcb_ref):
        i = pl.program_id(0)

        @pl.when(i == 0)
        def _():
            for r in (dlg_ref, dlb_ref, dcb_ref):
                r[...] = jnp.zeros_like(r)

        dc3 = _dot_nt(dya_ref[...], wco_ref[...].reshape(D, D))
        c1h = c1h_ref[...].astype(F32)
        c2 = c1h * g_ref[...] + be_ref[...]
        sg = _sig(c2)
        dc2 = dc3 * (sg * (1.0 + c2 * (1.0 - sg)))
        dlg_ref[...] += jnp.sum(dc2 * c1h, axis=0, keepdims=True)
        dlb_ref[...] += jnp.sum(dc2, axis=0, keepdims=True)
        dxh = dc2 * g_ref[...]
        dc1 = rstd_ref[...] * (dxh - jnp.mean(dxh, axis=-1, keepdims=True)
                               - c1h * jnp.mean(dxh * c1h, axis=-1, keepdims=True))
        dc1_ref[...] = dc1
        dcb_ref[...] += jnp.sum(dc1, axis=0, keepdims=True)

    row = lambda i: (i, 0)
    fixed2 = lambda i: (0, 0)
    return pl.pallas_call(
        body, name="conv_ln_bwd", grid=(T // tm,),
        in_specs=[pl.BlockSpec((tm, D), row), pl.BlockSpec((tm, D), row), pl.BlockSpec((tm, 1), row),
                  _res((1, D)), _res((1, D)), _res((NSHARD, DQ, D), lambda i: (0, layer * 3 + 0, 0))],
        out_specs=[pl.BlockSpec((tm, D), row), pl.BlockSpec((1, D), fixed2), pl.BlockSpec((1, D), fixed2),
                   pl.BlockSpec((1, D), fixed2)],
        out_shape=[_sds((T, D), F32), _sds((1, D), F32), _sds((1, D), F32), _sds((1, D), F32)],
        compiler_params=_cparams(("arbitrary",)),
    )(dya, c1h, rstd_c, ln_g, ln_b, w_sq_g)


def _conv_bwd(dc1, proj, dp3, conv_wf_s, seq, tm, rb, dep=None):
    T, D = dc1.shape
    NSTR = D // LANES
    KW = 31
    PADK = (KW - 1) // 2
    tps = seq // tm
    prev, nxt = _halo_maps(tm, T)
    n = T // tm

    def body(dc_ref, dcprev_ref, dcnext_ref, pa_ref, pprev_ref, pnext_ref, wf_ref, dp_in_ref,
             dp_ref, dw_ref, pad_ref, dpad_ref, dc0_ref, dwacc_ref):
        del dp_in_ref
        i = pl.program_id(0)
        first = (i % tps) == 0
        last = (i % tps) == tps - 1

        @pl.when(i == 0)
        def _():
            dwacc_ref[...] = jnp.zeros_like(dwacc_ref)

        _fill_c0_pad(pad_ref, pa_ref, pprev_ref, pnext_ref, D, tm, first, last)
        for cs in range(NSTR):
            lo, hi = cs * LANES, (cs + 1) * LANES
            dpad_ref[cs, HALO:HALO + tm, :] = dc_ref[:, lo:hi]
            dpad_ref[cs, 0:HALO, :] = jnp.where(first, 0.0, dcprev_ref[:, lo:hi])
            dpad_ref[cs, HALO + tm:HALO + tm + HALO, :] = jnp.where(last, 0.0, dcnext_ref[:, lo:hi])
        _dwconv(dpad_ref, wf_ref, dc0_ref, NSTR, tm, KW, rb)

        def strip(cs, carry):
            for r0 in range(0, tm, rb):
                d = dpad_ref[cs, HALO + r0:HALO + r0 + rb, :]
                for k in range(KW):
                    r = r0 + HALO - PADK + k
                    prod = d * pad_ref[cs, r:r + rb, :]
                    dwacc_ref[cs, k * 8:(k + 1) * 8, :] += jnp.sum(prod.reshape(rb // 8, 8, LANES), axis=0)
            return carry

        lax.fori_loop(0, NSTR, strip, 0)

        for cs in range(NSTR):
            lo, hi = cs * LANES, (cs + 1) * LANES
            av = pa_ref[:, lo:hi].astype(F32)
            sg = _sig(pa_ref[:, D + lo:D + hi].astype(F32))
            dc0 = dc0_ref[cs]
            dp_ref[:, lo:hi] = (dc0 * sg).astype(BF16)
            dp_ref[:, D + lo:D + hi] = (dc0 * av * (sg * (1.0 - sg))).astype(BF16)

        @pl.when(i == n - 1)
        def _():
            for cs in range(NSTR):
                dw_ref[cs] = jnp.sum(dwacc_ref[cs].reshape(32, 8, LANES), axis=1)

    row = lambda i: (i, 0)
    body, dep_spec, dep_arg = _after(body, 8, dep)
    return pl.pallas_call(
        body, name="conv_bwd", grid=(n,),
        in_specs=[pl.BlockSpec((tm, D), row), pl.BlockSpec((HALO, D), prev), pl.BlockSpec((HALO, D), nxt),
                  pl.BlockSpec((tm, 2 * D), row), pl.BlockSpec((HALO, 2 * D), prev), pl.BlockSpec((HALO, 2 * D), nxt),
                  _res((NSTR, 32, LANES)), pl.BlockSpec(memory_space=pl.ANY)] + dep_spec,
        out_specs=[pl.BlockSpec((tm, 2 * D), row),
                   pl.BlockSpec((NSTR, 32, LANES), lambda i: (0, 0, 0))],
        out_shape=[_sds(dp3.shape, BF16), _sds((NSTR, 32, LANES), F32)],
        scratch_shapes=[pltpu.VMEM((NSTR, tm + 2 * HALO, LANES), F32), pltpu.VMEM((NSTR, tm + 2 * HALO, LANES), F32),
                        pltpu.VMEM((NSTR, tm, LANES), F32), pltpu.VMEM((NSTR, 32 * 8, LANES), F32)],
        input_output_aliases={7: 0},
        compiler_params=_cparams(("arbitrary",)),
    )(dc1, dc1, dc1, proj, proj, proj, conv_wf_s, dp3, *dep_arg)


def _mix_in_bwd(dx1, dp3, x2d, g_mix, w_in_g, layer, tm, dep=None):
    T, D = x2d.shape
    CS = w_in_g.shape[2]
    CN = CS // 3

    def body(dx1_ref, dp_ref, x_ref, g_ref, w_ref, dx_ref, dg_ref):
        i = pl.program_id(0)

        @pl.when(i == 0)
        def _():
            dg_ref[...] = jnp.zeros_like(dg_ref)

        dh = jnp.zeros((tm, D), F32)
        for j in range(12):
            dh = dh + _dot_nt(dp_ref[:, j * CN:(j + 1) * CN], w_ref[j // 3, :, (j % 3) * CN:(j % 3 + 1) * CN])
        x = x_ref[...]
        rstd = lax.rsqrt(jnp.mean(x * x, axis=-1, keepdims=True) + EPS)
        xh = x * rstd
        dg_ref[...] += jnp.sum(dh * xh, axis=0, keepdims=True)
        dxh = dh * g_ref[...]
        dx_ref[...] = dx1_ref[...] + rstd * (dxh - xh * jnp.mean(dxh * xh, axis=-1, keepdims=True))

    row = lambda i: (i, 0)
    body, dep_spec, dep_arg = _after(body, 5, dep)
    return pl.pallas_call(
        body, name="mix_in_bwd", grid=(T // tm,),
        in_specs=[pl.BlockSpec((tm, D), row), pl.BlockSpec((tm, 6 * D), row),
                  pl.BlockSpec((tm, D), row), _res((1, D)), _res((NSHARD, D, CS), lambda i: (0, layer, 0))] + dep_spec,
        out_specs=[pl.BlockSpec((tm, D), row), pl.BlockSpec((1, D), lambda i: (0, 0))],
        out_shape=[_sds((T, D), F32), _sds((1, D), F32)],
        compiler_params=_cparams(("arbitrary",)),
    )(dx1, dp3, x2d, g_mix, w_in_g, *dep_arg)


def _tn_matmul(name, a, a_block, a_map, bs, b_block, b_map, out_shape, out_block, out_map, nj, nt):
    kk = [d for d in a_block if d is not None][-1]
    nn = [d for d in b_block if d is not None][-1]
    nb = len(bs)

    def body(*refs):
        a_ref, b_refs = refs[0], refs[1:1 + nb]
        o_refs, acc_refs = refs[-2 * nb:-nb], refs[-nb:]
        t = pl.program_id(1)

        @pl.when(t == 0)
        def _():
            for acc_ref in acc_refs:
                acc_ref[...] = jnp.zeros_like(acc_ref)

        a_t = a_ref[...].astype(BF16)
        for b_ref, acc_ref in zip(b_refs, acc_refs):
            acc_ref[...] += _dot_tn(a_t, b_ref[...].astype(BF16))

        @pl.when(t == nt - 1)
        def _():
            for o_ref, acc_ref in zip(o_refs, acc_refs):
                o_ref[...] = acc_ref[...].astype(o_ref.dtype)

    return pl.pallas_call(
        body, name=name, grid=(nj, nt),
        in_specs=[pl.BlockSpec(a_block, a_map)] + [pl.BlockSpec(b_block, b_map)] * nb,
        out_specs=[pl.BlockSpec(out_block, out_map)] * nb, out_shape=[_sds(out_shape, BF16)] * nb,
        scratch_shapes=[pltpu.VMEM((kk, nn), F32)] * nb,
        compiler_params=_cparams(("parallel", "arbitrary")),
    )(a, *bs)


def _place_shard(name, w, pos, dtype, tr):
    R, C = w.shape

    def body(pos_ref, w_ref, o_ref):
        del pos_ref
        o_ref[...] = w_ref[...].astype(dtype)

    grid_spec = pltpu.PrefetchScalarGridSpec(
        num_scalar_prefetch=1, grid=(R // tr,),
        in_specs=[pl.BlockSpec((tr, C), lambda r, pos: (r, 0))],
        out_specs=pl.BlockSpec((None, tr, C), lambda r, pos: (pos[1], r, 0)))
    return pl.pallas_call(body, name=name, grid_spec=grid_spec, out_shape=_sds((NSHARD, R, C), dtype),
                          compiler_params=_cparams(("parallel",)))(pos, w)


def _add_halves(name, g, rbuf, pos, tr):
    NS, _, H, C = g.shape

    def body(pos_ref, g_ref, r_ref, o_ref):
        del pos_ref
        o_ref[...] = (g_ref[...].astype(F32) + r_ref[...].astype(F32)).astype(BF16)

    grid_spec = pltpu.PrefetchScalarGridSpec(
        num_scalar_prefetch=1, grid=(NS, H // tr),
        in_specs=[pl.BlockSpec((None, None, tr, C), lambda s, r, pos: (s, pos[0], r, 0)),
                  pl.BlockSpec((None, tr, C), lambda s, r, pos: (s, r, 0))],
        out_specs=pl.BlockSpec((None, tr, C), lambda s, r, pos: (s, r, 0)))
    return pl.pallas_call(body, name=name, grid_spec=grid_spec, out_shape=_sds((NS, H, C), BF16),
                          compiler_params=_cparams(("parallel", "parallel")))(pos, g, rbuf)


def _add_shards(name, p, rbuf, pos, tr, layer, n_layers, prev):
    _, H, C = p.shape

    def body(pos_ref, p_ref, r_ref, *rest):
        del pos_ref
        o_ref = rest[-1]
        acc = p_ref[...].astype(F32)
        for j in range(3):
            acc = acc + r_ref[j].astype(F32)
        o_ref[...] = acc

    in_specs = [pl.BlockSpec((None, tr, C), lambda r, pos: (pos[1], r, 0)),
                pl.BlockSpec((3, tr, C), lambda r, pos: (0, r, 0))]
    args = [pos, p, rbuf]
    aliases = {}
    if prev is not None:
        in_specs.append(pl.BlockSpec(memory_space=pl.ANY))
        args.append(prev)
        aliases = {3: 0}
    grid_spec = pltpu.PrefetchScalarGridSpec(
        num_scalar_prefetch=1, grid=(H // tr,), in_specs=in_specs,
        out_specs=pl.BlockSpec((None, None, tr, C), lambda r, pos: (layer, pos[0], r, 0)))
    return pl.pallas_call(body, name=name, grid_spec=grid_spec, out_shape=_sds((n_layers, 2, H, C), F32),
                          input_output_aliases=aliases, compiler_params=_cparams(("parallel",)))(*args)


def _sum_slots(own, land, me, tr):
    NS8, R, C = land.shape

    def body(me_ref, own_ref, l_ref, o_ref):
        acc = None
        for j in range(NS8):
            term = jnp.where(me_ref[0] == j, own_ref[...], l_ref[j])
            acc = term if acc is None else acc + term
        o_ref[...] = acc

    grid_spec = pltpu.PrefetchScalarGridSpec(
        num_scalar_prefetch=1, grid=(R // tr,),
        in_specs=[pl.BlockSpec((tr, C), lambda i, me: (i, 0)), pl.BlockSpec((NS8, tr, C), lambda i, me: (0, i, 0))],
        out_specs=pl.BlockSpec((tr, C), lambda i, me: (i, 0)))
    return pl.pallas_call(body, name="sum_slots", grid_spec=grid_spec, out_shape=_sds((R, C), F32),
                          compiler_params=_cparams(("parallel",)))(me, own, land)


def _adamw(name, w, g, m, v, tr):
    R, C = w.shape
    c1 = 1.0 - ADAM_B1 ** ADAM_STEP
    c2 = 1.0 - ADAM_B2 ** ADAM_STEP

    def body(w_ref, g_ref, m_ref, v_ref, d_ref, mo_ref, vo_ref):
        g_ = g_ref[...]
        m_ = ADAM_B1 * m_ref[...] + (1.0 - ADAM_B1) * g_
        v_ = ADAM_B2 * v_ref[...] + (1.0 - ADAM_B2) * (g_ * g_)
        mo_ref[...] = m_
        vo_ref[...] = v_
        d_ref[...] = -ADAM_LR * ((m_ / c1) / (jnp.sqrt(v_ / c2) + ADAM_EPS) + ADAM_WD * w_ref[...])

    spec = pl.BlockSpec((tr, C), lambda i: (i, 0))
    return pl.pallas_call(
        body, name=name, grid=(R // tr,), in_specs=[spec] * 4, out_specs=[spec] * 3,
        out_shape=[_sds((R, C), F32)] * 3, compiler_params=_cparams(("parallel",)))(w, g, m, v)


def _row_tile(rows, cap):
    best = rows
    for t in range(8, min(rows, cap) + 1, 8):
        if rows % t == 0:
            best = t
    return best


HBM_SPEC = pl.BlockSpec(memory_space=pltpu.HBM)
SEM_SPEC = pl.BlockSpec(memory_space=pltpu.SEMAPHORE)
DATAFLOW = pltpu.SideEffectType.DATAFLOW_SIDE_EFFECTING
DMA_SEM = pltpu.SemaphoreType.DMA


def _hbm(a):
    return pltpu.with_memory_space_constraint(a, pltpu.HBM)


def _mesh_pos():
    return lax.axis_index("x"), lax.axis_index("y"), lax.axis_index("c")


def _other_chips(x, y):
    return [(1 - x, y), (x, 1 - y), (1 - x, 1 - y)]


def _half_rows(buf, shard, core):
    h = buf.shape[1] // 2
    return buf.at[shard, pl.ds(core * h, h), :]


def _ici_copy(buf, j, send, recv, landing):
    x, y, c = _mesh_pos()
    px, py = _other_chips(x, y)[j]
    part = _half_rows(buf, 2 * px + py if landing else 2 * x + y, c)
    return pltpu.make_async_remote_copy(src_ref=part, dst_ref=part, send_sem=send, recv_sem=recv,
                                        device_id=(px, py, c), device_id_type=MESH_ID)


def _sibling_copy(buf, j, send, recv, landing):
    x, y, c = _mesh_pos()
    px, py = _other_chips(x, y)[j]
    part = _half_rows(buf, 2 * px + py, 1 - c if landing else c)
    return pltpu.make_async_remote_copy(src_ref=part, dst_ref=part, send_sem=send, recv_sem=recv,
                                        device_id=(x, y, 1 - c), device_id_type=MESH_ID)


def _forward_sibling(name, bufs, with_ici):
    n = len(bufs)

    def body(*refs):
        ins = refs[:n]
        send_ici, recv_ici, send_d2d, recv_d2d = refs[2 * n:]
        sends = []
        if with_ici:
            for i in range(n):
                for j in range(3):
                    cp = _ici_copy(ins[i], j, send_ici.at[i, j], recv_ici.at[i, j], False)
                    cp.start()
                    sends.append(cp)
        for i in range(n):
            for j in range(3):
                if with_ici:
                    _ici_copy(ins[i], j, send_ici.at[i, j], recv_ici.at[i, j], True).wait_recv()
                cp = _sibling_copy(ins[i], j, send_d2d.at[i, j], recv_d2d.at[i, j], False)
                cp.start()
                sends.append(cp)
        for i in range(n):
            for j in range(3):
                _sibling_copy(ins[i], j, send_d2d.at[i, j], recv_d2d.at[i, j], True).wait_recv()
        for cp in sends:
            cp.wait_send()

    return pl.pallas_call(
        body, name=name, in_specs=[HBM_SPEC] * n, out_specs=[HBM_SPEC] * n,
        out_shape=[_sds(b.shape, b.dtype) for b in bufs],
        scratch_shapes=[DMA_SEM((n, 3))] * 4, input_output_aliases={i: i for i in range(n)},
    )(*bufs)


def _gather_start(groups):
    flat = [b for g in groups for b in g]
    n, ng = len(flat), len(groups)

    def body(*refs):
        ins, sems, token = refs[:n], refs[n:n + 2 * ng], refs[-1]
        k = 0
        for gi, g in enumerate(groups):
            for a in range(len(g)):
                for j in range(3):
                    _ici_copy(ins[k], j, sems[2 * gi], sems[2 * gi + 1], False).start()
                k += 1
        token[...] = jnp.zeros_like(token)

    res = pl.pallas_call(
        body, name="gather_start", in_specs=[HBM_SPEC] * n,
        out_specs=[SEM_SPEC] * (2 * ng) + [HBM_SPEC] * n + [pl.BlockSpec(memory_space=pltpu.VMEM)],
        out_shape=[DMA_SEM(()) for g in groups for _ in range(2)]
        + [pltpu.HBM(b.shape, b.dtype) for b in flat] + [_sds((8, LANES), F32)],
        input_output_aliases={i: 2 * ng + i for i in range(n)},
        compiler_params=pltpu.CompilerParams(has_side_effects=DATAFLOW),
    )(*[_hbm(b) for b in flat])
    sems = [(res[2 * gi], res[2 * gi + 1]) for gi in range(ng)]
    thru, k = [], 2 * ng
    for g in groups:
        thru.append(list(res[k:k + len(g)]))
        k += len(g)
    return sems, thru, res[-1]


def _gather_wait(name, bufs, sems, after):
    n = len(bufs)

    def body(*refs):
        ins, send, recv = refs[:n], refs[n], refs[n + 1]
        for a in range(n):
            for j in range(3):
                _ici_copy(ins[a], j, send, recv, False).wait_send()
                _ici_copy(ins[a], j, send, recv, True).wait_recv()

    return pl.pallas_call(
        body, name=name, in_specs=[HBM_SPEC] * n + [SEM_SPEC, SEM_SPEC, pl.BlockSpec(memory_space=pl.ANY)],
        out_specs=[HBM_SPEC] * n, out_shape=[pltpu.HBM(b.shape, b.dtype) for b in bufs],
        input_output_aliases={i: i for i in range(n)},
        compiler_params=pltpu.CompilerParams(has_side_effects=DATAFLOW),
    )(*bufs, sems[0], sems[1], after)


def _send_sibling_halves(name, arrs):
    n = len(arrs)

    def body(*refs):
        ins, outs = refs[:n], refs[n:2 * n]
        send, recv = refs[2 * n:]
        x, y, c = _mesh_pos()
        cps = []
        for i in range(n):
            cp = pltpu.make_async_remote_copy(
                src_ref=ins[i].at[:, 1 - c], dst_ref=outs[i],
                send_sem=send.at[i], recv_sem=recv.at[i], device_id=(x, y, 1 - c), device_id_type=MESH_ID)
            cp.start()
            cps.append(cp)
        for cp in cps:
            cp.wait()

    return pl.pallas_call(
        body, name=name, in_specs=[HBM_SPEC] * n, out_specs=[HBM_SPEC] * n,
        out_shape=[_sds((a.shape[0],) + a.shape[2:], a.dtype) for a in arrs],
        scratch_shapes=[DMA_SEM((n,)), DMA_SEM((n,))],
    )(*arrs)


def _chip_copy(p, land, j, send, recv):
    x, y, c = _mesh_pos()
    px, py = _other_chips(x, y)[j]
    return pltpu.make_async_remote_copy(src_ref=p.at[2 * px + py], dst_ref=land.at[j], send_sem=send, recv_sem=recv,
                                        device_id=(px, py, c), device_id_type=MESH_ID)


def _chip_send_start(name, ps):
    n = len(ps)
    lands = [lax.empty((3,) + p.shape[1:], p.dtype) for p in ps]

    def body(*refs):
        ins, lnd, send, recv, token = refs[:n], refs[n:2 * n], refs[2 * n], refs[2 * n + 1], refs[-1]
        for i in range(n):
            for j in range(3):
                _chip_copy(ins[i], lnd[i], j, send, recv).start()
        token[...] = jnp.zeros_like(token)

    res = pl.pallas_call(
        body, name=name, in_specs=[HBM_SPEC] * (2 * n),
        out_specs=[SEM_SPEC, SEM_SPEC] + [HBM_SPEC] * (2 * n) + [pl.BlockSpec(memory_space=pltpu.VMEM)],
        out_shape=[DMA_SEM(()), DMA_SEM(())] + [pltpu.HBM(a.shape, a.dtype) for a in ps + lands]
        + [_sds((8, LANES), F32)],
        input_output_aliases={i: 2 + i for i in range(2 * n)},
        compiler_params=pltpu.CompilerParams(has_side_effects=DATAFLOW),
    )(*[_hbm(a) for a in ps + lands])
    return (res[0], res[1]), list(res[2:2 + n]), list(res[2 + n:2 + 2 * n]), res[-1]


def _chip_send_wait(name, ps, lands, sems, after):
    n = len(ps)

    def body(*refs):
        ins, lnd, send, recv = refs[:n], refs[n:2 * n], refs[2 * n], refs[2 * n + 1]
        for i in range(n):
            for j in range(3):
                cp = _chip_copy(ins[i], lnd[i], j, send, recv)
                cp.wait_send()
                cp.wait_recv()

    res = pl.pallas_call(
        body, name=name, in_specs=[HBM_SPEC] * (2 * n) + [SEM_SPEC, SEM_SPEC, pl.BlockSpec(memory_space=pl.ANY)],
        out_specs=[HBM_SPEC] * (2 * n), out_shape=[pltpu.HBM(a.shape, a.dtype) for a in ps + lands],
        input_output_aliases={i: i for i in range(2 * n)},
        compiler_params=pltpu.CompilerParams(has_side_effects=DATAFLOW),
    )(*ps, *lands, sems[0], sems[1], after)
    return list(res[:n]), list(res[n:])


def _join_halves(arrs):
    n = len(arrs)

    def body(*refs):
        bufs = refs[n:2 * n]
        send, recv = refs[2 * n:]
        x, y, c = _mesh_pos()
        cps = []
        for i in range(n):
            mine = bufs[i].at[:, c]
            cp = pltpu.make_async_remote_copy(
                src_ref=mine, dst_ref=mine, send_sem=send.at[i], recv_sem=recv.at[i],
                device_id=(x, y, 1 - c), device_id_type=MESH_ID)
            cp.start()
            cps.append(cp)
        for i, cp in enumerate(cps):
            theirs = bufs[i].at[:, 1 - c]
            cp.wait_send()
            pltpu.make_async_remote_copy(
                src_ref=theirs, dst_ref=theirs, send_sem=send.at[i], recv_sem=recv.at[i],
                device_id=(x, y, 1 - c), device_id_type=MESH_ID).wait_recv()

    return pl.pallas_call(
        body, name="join_halves", in_specs=[HBM_SPEC] * n, out_specs=[HBM_SPEC] * n,
        out_shape=[_sds(a.shape, a.dtype) for a in arrs],
        scratch_shapes=[DMA_SEM((n,)), DMA_SEM((n,))], input_output_aliases={i: i for i in range(n)},
    )(*arrs)


def _peer_copy(buf, land, k, send, recv, landing):
    x, y, c = _mesh_pos()
    px, py, pc = x ^ ((k >> 2) & 1), y ^ ((k >> 1) & 1), c ^ (k & 1)
    slot = 4 * px + 2 * py + pc if landing else 4 * x + 2 * y + c
    return pltpu.make_async_remote_copy(src_ref=buf, dst_ref=land.at[slot], send_sem=send, recv_sem=recv,
                                        device_id=(px, py, pc), device_id_type=MESH_ID)


def _exchange_all(buf):
    def body(in_ref, out_ref, send, recv):
        cps = [_peer_copy(in_ref, out_ref, k, send.at[k - 1], recv.at[k - 1], False) for k in range(1, 8)]
        for cp in cps:
            cp.start()
        for k in range(1, 8):
            cps[k - 1].wait_send()
            _peer_copy(in_ref, out_ref, k, send.at[k - 1], recv.at[k - 1], True).wait_recv()

    return pl.pallas_call(
        body, name="exchange_all", in_specs=[HBM_SPEC], out_specs=HBM_SPEC,
        out_shape=_sds((8,) + buf.shape, buf.dtype), scratch_shapes=[DMA_SEM((7,)), DMA_SEM((7,))],
    )(buf)


def _exchange_start(name, buf):
    land = lax.empty((8,) + buf.shape, buf.dtype)

    def body(in_ref, land_ref, send, recv, in_thru, land_thru, token):
        for k in range(1, 8):
            _peer_copy(in_ref, land_ref, k, send, recv, False).start()
        token[...] = jnp.zeros_like(token)

    res = pl.pallas_call(
        body, name=name, in_specs=[HBM_SPEC] * 2,
        out_specs=[SEM_SPEC, SEM_SPEC, HBM_SPEC, HBM_SPEC, pl.BlockSpec(memory_space=pltpu.VMEM)],
        out_shape=[DMA_SEM(()), DMA_SEM(()), pltpu.HBM(buf.shape, buf.dtype), pltpu.HBM(land.shape, land.dtype),
                   _sds((8, LANES), F32)],
        input_output_aliases={0: 2, 1: 3}, compiler_params=pltpu.CompilerParams(has_side_effects=DATAFLOW),
    )(_hbm(buf), _hbm(land))
    return (res[0], res[1]), res[2], res[3], res[4]


def _exchange_wait(name, buf, land, sems, after):
    def body(in_ref, land_ref, send, recv, after_ref, in_thru, land_thru):
        for k in range(1, 8):
            _peer_copy(in_ref, land_ref, k, send, recv, False).wait_send()
            _peer_copy(in_ref, land_ref, k, send, recv, True).wait_recv()

    res = pl.pallas_call(
        body, name=name, in_specs=[HBM_SPEC, HBM_SPEC, SEM_SPEC, SEM_SPEC, pl.BlockSpec(memory_space=pl.ANY)],
        out_specs=[HBM_SPEC, HBM_SPEC], out_shape=[pltpu.HBM(buf.shape, buf.dtype), pltpu.HBM(land.shape, land.dtype)],
        input_output_aliases={0: 0, 1: 1}, compiler_params=pltpu.CompilerParams(has_side_effects=DATAFLOW),
    )(buf, land, sems[0], sems[1], after)
    return res[0], res[1]


def _pad_to(a, axis, size):
    pad = [(0, 0)] * a.ndim
    pad[axis] = (0, size - a.shape[axis])
    return jnp.pad(a, pad)


def _strips(w):
    k, d = w.shape
    return _pad_to(w, 0, 32).reshape(32, d // LANES, LANES).transpose(1, 0, 2)


def kernel(x, norm_mix, w_in, gate_bias, conv_w, conv_b, conv_ln_g, conv_ln_b, w_conv_out, sgu_ln_g, sgu_ln_b, w_spatial, b_spatial, w_sgu_out, w_o, norm_ffn, w_ffn_gate, w_ffn_up, w_ffn_down, norm_final, loss_target, m_norm_mix, m_w_in, m_gate_bias, m_conv_w, m_conv_b, m_conv_ln_g, m_conv_ln_b, m_w_conv_out, m_sgu_ln_g, m_sgu_ln_b, m_w_spatial, m_b_spatial, m_w_sgu_out, m_w_o, m_norm_ffn, m_w_ffn_gate, m_w_ffn_up, m_w_ffn_down, m_norm_final, v_norm_mix, v_w_in, v_gate_bias, v_conv_w, v_conv_b, v_conv_ln_g, v_conv_ln_b, v_w_conv_out, v_sgu_ln_g, v_sgu_ln_b, v_w_spatial, v_b_spatial, v_w_sgu_out, v_w_o, v_norm_ffn, v_w_ffn_gate, v_w_ffn_up, v_w_ffn_down, v_norm_final):
    BL, S, D = x.shape
    T = BL * S
    L = w_in.shape[0]
    CS = w_in.shape[2]
    CN = CS // 3
    DQ = D // NSHARD
    FS = w_ffn_gate.shape[2]
    FP = -(-FS // 256) * 256
    G, CH = w_spatial.shape[1], w_spatial.shape[2]
    KW = conv_w.shape[1]
    CQ = conv_w.shape[3]
    NSTR = D // LANES
    tm = min(512, S // 2)
    tm2 = max(tm // 2, CH)
    rb = min(64, tm)
    mx, my, mc = _mesh_pos()
    pos = jnp.stack([mc, 2 * mx + my]).astype(jnp.int32)

    def placed(name, w, dtype=BF16):
        return _place_shard("place_" + name, w, pos, dtype, _row_tile(w.shape[0], 256))

    wts = []
    for l in range(L):
        w_sq = jnp.concatenate([w_conv_out[l], w_sgu_out[l], w_o[l]], axis=0)
        wts.append(dict(w_in=placed("w_in", w_in[l]), w_sq=placed("w_sq", w_sq),
                        wg=placed("w_gate", _pad_to(w_ffn_gate[l], 1, FP)),
                        wu=placed("w_up", _pad_to(w_ffn_up[l], 1, FP)),
                        wd=placed("w_down", _pad_to(w_ffn_down[l], 0, FP))))
    cw_p = placed("conv_w", _pad_to(conv_w.reshape(L, KW, CQ), 1, 32).reshape(L * 32, CQ), F32)

    wts[0]["w_in"], cw_g = _forward_sibling("gather_first", [wts[0]["w_in"], cw_p], True)
    conv_w_full = cw_g.reshape(NSHARD, L, 32, CQ).transpose(1, 2, 0, 3).reshape(L, 32, D)[:, :KW]
    ffn_keys = ["wg", "wu", "wd"]
    order = [[(0, "w_sq")], [(0, k) for k in ffn_keys]]
    order += [[(l, k) for k in ["w_in", "w_sq"] + ffn_keys] for l in range(1, L)]
    gsems, flying, token = _gather_start([[wts[l][k] for l, k in grp] for grp in order])

    def land(gi, after):
        bufs = _gather_wait("gather_wait_%d" % gi, flying[gi], gsems[gi], after)
        bufs = _forward_sibling("gather_forward_%d" % gi, bufs, False)
        for (l, k), b in zip(order[gi], bufs):
            wts[l][k] = b

    x2d = x.reshape(T, D)
    tgt = loss_target.reshape(T, D)
    row = lambda a, l: a[l].reshape(1, -1)

    saved = []
    xc = x2d
    for l in range(L):
        ws_b = w_spatial[l].astype(BF16)
        bs_b = jnp.repeat(b_spatial[l].T, D // G, axis=1)
        cw_s = _strips(conv_w_full[l])
        h, proj = _mix_in_fwd(xc, row(norm_mix, l), wts[l]["w_in"], 0, tm, token if l == 0 else None)
        if l == 0:
            land(0, h)
        c1h, rstd_c, c3, ya = _conv_fwd(proj, cw_s, row(conv_b, l), row(conv_ln_g, l), row(conv_ln_b, l),
                                        wts[l]["w_sq"], 0, S, tm, rb)
        if l == 0:
            land(1, ya)
        mixed, gated, yb, merged, x1 = _sgu_merge_fwd(proj, ya, xc, row(sgu_ln_g, l), row(sgu_ln_b, l), ws_b, bs_b,
                                                      row(gate_bias, l), wts[l]["w_sq"], 0, tm2)
        h2, gt, up, act, x2 = _ffn_fwd(x1, row(norm_ffn, l), wts[l]["wg"], wts[l]["wu"], wts[l]["wd"], 0, tm2)
        if l + 1 < L:
            land(l + 2, x2)
        saved.append(dict(x=xc, h=h, proj=proj, c1h=c1h, rstd_c=rstd_c, c3=c3, ya=ya, mixed=mixed, gated=gated,
                          yb=yb, merged=merged, x1=x1, h2=h2, gt=gt, up=up, act=act, ws_b=ws_b, cw=conv_w_full[l]))
        xc = x2

    dx, loss_part, d_norm_final = _loss_head(xc, norm_final.reshape(1, D), tgt, tm)
    loss = lax.psum(loss_part[0, 0], ("x", "y", "c"))

    g_acc = {}

    def reduce_start(tag, layer, named):
        arrs = [g.reshape(NSHARD, 2, g.shape[1] // 2, g.shape[2]) for _, g in named]
        from_sib = _send_sibling_halves("sibling_" + tag, arrs)
        ps = [_add_halves("presum_" + nm, a, r, pos, _row_tile(a.shape[2], 256))
              for (nm, _), a, r in zip(named, arrs, from_sib)]
        sems, ps, lands, tok = _chip_send_start("chip_send_start_" + tag, ps)
        return dict(tag=tag, layer=layer, names=[nm for nm, _ in named], ps=ps, lands=lands, sems=sems), tok

    def reduce_finish(pend, after):
        ps, lands = _chip_send_wait("chip_send_wait_" + pend["tag"], pend["ps"], pend["lands"], pend["sems"], after)
        for nm, p, r in zip(pend["names"], ps, lands):
            g_acc[nm] = _add_shards("shardsum_" + nm, p, r, pos, _row_tile(p.shape[1], 256), pend["layer"], L,
                                    g_acc.get(nm))

    me_idx = (4 * mx + 2 * my + mc).astype(jnp.int32).reshape(1)
    exchanges = []

    def pack_rows(pieces):
        packed = jnp.concatenate(pieces, axis=0)
        return _pad_to(packed, 0, -(-packed.shape[0] // 8) * 8)

    def unpack_rows(summed, pieces):
        out, off = [], 0
        for p in pieces:
            out.append(summed[off:off + p.shape[0]])
            off += p.shape[0]
        return out

    def small_start(tag, pieces):
        sems, buf, land, token = _exchange_start("exchange_start_" + tag, pack_rows(pieces))
        return dict(tag=tag, pieces=pieces, buf=buf, land=land, sems=sems, token=token)

    def small_finish(st, after):
        buf, land = _exchange_wait("exchange_wait_" + st["tag"], st["buf"], st["land"], st["sems"], after)
        return unpack_rows(_sum_slots(buf, land, me_idx, 8), st["pieces"])

    small = [None] * L
    tt = min(1024, T // 2)
    nt = T // tt
    pending, tok = None, None
    for l in reversed(range(L)):
        sv, wt = saved[l], wts[l]
        dx1, dgt, dup, d_norm_ffn = _ffn_bwd(dx, sv["x1"], sv["gt"], sv["up"], row(norm_ffn, l), wt["wg"], wt["wu"],
                                             wt["wd"], 0, tm2, tok)
        g_g, g_u = _tn_matmul("grad_w_gate_up", sv["h2"], (tt, D), lambda j, t: (t, 0), [dgt, dup], (tt, FP),
                              lambda j, t: (t, j), (NSHARD, D, FP), (None, D, FP), lambda j, t: (j, 0, 0), NSHARD, nt)
        g_d, = _tn_matmul("grad_w_down", sv["act"], (tt, FP), lambda j, t: (t, j), [dx], (tt, D), lambda j, t: (t, 0),
                          (NSHARD, FP, D), (None, FP, D), lambda j, t: (j, 0, 0), NSHARD, nt)
        if pending is not None:
            reduce_finish(pending, g_d)
        ffn_pend, tok = reduce_start("ffn%d" % l, l, [("w_ffn_gate", g_g), ("w_ffn_up", g_u), ("w_ffn_down", g_d)])
        wst_b = jnp.swapaxes(sv["ws_b"], 1, 2)
        dya, dyb, dp3, d_gate_bias, d_sgu_g, d_sgu_b, d_bs, d_ws = _merge_sgu_bwd(
            dx1, sv["proj"], sv["ya"], sv["yb"], sv["mixed"], row(sgu_ln_g, l), row(sgu_ln_b, l), sv["ws_b"], wst_b,
            row(gate_bias, l), wt["w_sq"], 0, tm2, tok)
        sq_args = ((tt, D), lambda j, t: (t, 0))
        sq_out = ((D, D), (D, D), lambda j, t: (0, 0), 1, nt)
        g_o, = _tn_matmul("grad_w_o", sv["merged"], *sq_args, [dx1], *sq_args, *sq_out)
        g_so, = _tn_matmul("grad_w_sgu_out", sv["gated"], *sq_args, [dyb], *sq_args, *sq_out)
        g_co, = _tn_matmul("grad_w_conv_out", sv["c3"], *sq_args, [dya], *sq_args, *sq_out)
        dc1, d_cln_g, d_cln_b, d_conv_b = _conv_ln_bwd(dya, sv["c1h"], sv["rstd_c"], row(conv_ln_g, l),
                                                       row(conv_ln_b, l), wt["w_sq"], 0, tm)
        small[l] = [None, d_gate_bias.reshape(2, D), None, d_conv_b, d_cln_g, d_cln_b, d_sgu_g, d_sgu_b,
                    d_ws.reshape(G * CH * CH // D, D), d_bs.reshape(G * CH // D, D), d_norm_ffn]
        tok_x = None
        if l == 0:
            early = [k for k in range(len(small[0])) if small[0][k] is not None]
            exchanges.append((small_start("early0", [small[0][k] for k in early]), [(0, k) for k in early]))
            tok_x = exchanges[-1][0]["token"]
        dp3, d_cw_s = _conv_bwd(dc1, sv["proj"], dp3, _strips(sv["cw"][::-1]), S, tm, rb, tok_x)
        g_in, = _tn_matmul("grad_w_in", sv["h"], (tt, D), lambda j, t: (t, 0), [dp3], (tt, CS), lambda j, t: (t, j),
                           (NSHARD, D, CS), (None, D, CS), lambda j, t: (j, 0, 0), NSHARD, nt)
        reduce_finish(ffn_pend, g_in)
        pending, tok = reduce_start("mix%d" % l, l, [
            ("w_in", g_in), ("w_conv_out", g_co.reshape(NSHARD, DQ, D)), ("w_sgu_out", g_so.reshape(NSHARD, DQ, D)),
            ("w_o", g_o.reshape(NSHARD, DQ, D))])
        dx, d_norm_mix = _mix_in_bwd(dx1, dp3, sv["x"], row(norm_mix, l), wt["w_in"], 0, tm, tok)
        small[l][0] = d_norm_mix
        small[l][2] = d_cw_s.transpose(1, 0, 2).reshape(32, D)
        if l > 0:
            exchanges.append((small_start("layer%d" % l, small[l]), [(l, k) for k in range(len(small[l]))]))
            tok = [tok, exchanges[-1][0]["token"]]
    reduce_finish(pending, dx)
    grad_x = dx.reshape(BL, S, D)

    names = ["w_in", "w_conv_out", "w_sgu_out", "w_o", "w_ffn_gate", "w_ffn_up", "w_ffn_down"]
    g_full = _join_halves([g_acc[nm] for nm in names])
    g_w_in, g_w_co, g_w_so, g_w_o, g_w_g, g_w_u, g_w_d = [g.reshape(L, 2 * g.shape[2], g.shape[3]) for g in g_full]
    g_w_g = g_w_g[:, :, :FS]
    g_w_u = g_w_u[:, :, :FS]
    g_w_d = g_w_d[:, :FS, :]

    late = [small[0][0], small[0][2], d_norm_final]
    packed = pack_rows(late)
    summed = _sum_slots(packed, _exchange_all(packed), me_idx, 8)
    sg = [[None] * len(small[l]) for l in range(L)]
    sg[0][0], sg[0][2], g_norm_final = unpack_rows(summed, late)
    g_norm_final = g_norm_final[0]
    for st, where in exchanges:
        for (l, k), piece in zip(where, small_finish(st, summed)):
            sg[l][k] = piece

    def per_layer(k, shape):
        return jnp.stack([sg[l][k] for l in range(L)]).reshape(shape)

    g_norm_mix = per_layer(0, (L, D))
    g_gate_bias = per_layer(1, (L, 2 * D))
    g_conv_w_full = jnp.stack([sg[l][2][:KW] for l in range(L)])
    g_conv_w = lax.dynamic_slice_in_dim(g_conv_w_full, (2 * mx + my) * CQ, CQ, axis=2).reshape(L, KW, 1, CQ)
    g_conv_b = per_layer(3, (L, D))
    g_conv_ln_g = per_layer(4, (L, D))
    g_conv_ln_b = per_layer(5, (L, D))
    g_sgu_ln_g = per_layer(6, (L, D))
    g_sgu_ln_b = per_layer(7, (L, D))
    g_w_spatial = per_layer(8, (L, G, CH, CH))
    g_b_spatial = per_layer(9, (L, G, CH))
    g_norm_ffn = per_layer(10, (L, D))

    grads = [g_norm_mix, g_w_in, g_gate_bias, g_conv_w, g_conv_b, g_conv_ln_g, g_conv_ln_b, g_w_co, g_sgu_ln_g,
             g_sgu_ln_b, g_w_spatial, g_b_spatial, g_w_so, g_w_o, g_norm_ffn, g_w_g, g_w_u, g_w_d, g_norm_final]
    weights = [norm_mix, w_in, gate_bias, conv_w, conv_b, conv_ln_g, conv_ln_b, w_conv_out, sgu_ln_g, sgu_ln_b,
               w_spatial, b_spatial, w_sgu_out, w_o, norm_ffn, w_ffn_gate, w_ffn_up, w_ffn_down, norm_final]
    ms = [m_norm_mix, m_w_in, m_gate_bias, m_conv_w, m_conv_b, m_conv_ln_g, m_conv_ln_b, m_w_conv_out, m_sgu_ln_g,
          m_sgu_ln_b, m_w_spatial, m_b_spatial, m_w_sgu_out, m_w_o, m_norm_ffn, m_w_ffn_gate, m_w_ffn_up,
          m_w_ffn_down, m_norm_final]
    vs = [v_norm_mix, v_w_in, v_gate_bias, v_conv_w, v_conv_b, v_conv_ln_g, v_conv_ln_b, v_w_conv_out, v_sgu_ln_g,
          v_sgu_ln_b, v_w_spatial, v_b_spatial, v_w_sgu_out, v_w_o, v_norm_ffn, v_w_ffn_gate, v_w_ffn_up,
          v_w_ffn_down, v_norm_final]

    big_idx = [1, 7, 12, 13, 15, 16, 17]
    deltas, new_m, new_v = [None] * 19, [None] * 19, [None] * 19
    for k in big_idx:
        shp = weights[k].shape
        r2 = (shp[0] * shp[1], shp[2])
        d_, m_, v_ = _adamw("adamw_" + str(k), weights[k].reshape(r2), grads[k].reshape(r2), ms[k].reshape(r2),
                            vs[k].reshape(r2), _row_tile(r2[0], 256))
        deltas[k], new_m[k], new_v[k] = d_.reshape(shp), m_.reshape(shp), v_.reshape(shp)
    small_idx = [k for k in range(19) if k not in big_idx]

    def pack(arrs):
        flat = jnp.concatenate([arrs[k].reshape(-1) for k in small_idx])
        rows = -(-flat.shape[0] // (256 * LANES)) * 256
        return _pad_to(flat, 0, rows * LANES).reshape(rows, LANES)

    pw, pg, pm, pv = pack(weights), pack(grads), pack(ms), pack(vs)
    d_, m_, v_ = _adamw("adamw_small", pw, pg, pm, pv, _row_tile(pw.shape[0], 512))
    off = 0
    for k in small_idx:
        n_el = weights[k].size
        shp = weights[k].shape
        deltas[k] = d_.reshape(-1)[off:off + n_el].reshape(shp)
        new_m[k] = m_.reshape(-1)[off:off + n_el].reshape(shp)
        new_v[k] = v_.reshape(-1)[off:off + n_el].reshape(shp)
        off += n_el

    return (loss, grad_x, *grads, *deltas, *new_m, *new_v)
```

```python
import functools

import jax
import jax.numpy as jnp
from jax import lax
from jax.experimental import pallas as pl
from jax.experimental.pallas import tpu as pltpu

F32 = jnp.float32
BF16 = jnp.bfloat16
EPS = 1e-6
ADAM_LR = 0.001
ADAM_B1 = 0.9
ADAM_B2 = 0.999
ADAM_EPS = 1e-08
ADAM_WD = 0.01
ADAM_STEP = 10

NSHARD = 4
LANES = 128
HALO = 16
VMEM_LIMIT = 60 * 1024 * 1024
MESH_ID = pl.DeviceIdType.MESH


def _dot(a, b):
    return jnp.dot(a, b, preferred_element_type=F32)


def _dot_nt(a, b):
    return lax.dot_general(a, b, (((1,), (1,)), ((), ())), preferred_element_type=F32)


def _dot_tn(a, b):
    return lax.dot_general(a, b, (((0,), (0,)), ((), ())), preferred_element_type=F32)


def _sig(z):
    return 1.0 / (1.0 + jnp.exp(-z))


def _res(shape, imap=None):
    nd = len(shape)
    if imap is None:
        imap = lambda *_: (0,) * nd
    return pl.BlockSpec(shape, imap, pipeline_mode=pl.Buffered(1))


def _cparams(sem):
    return pltpu.CompilerParams(dimension_semantics=sem, vmem_limit_bytes=VMEM_LIMIT)


def _sds(shape, dtype):
    return jax.ShapeDtypeStruct(shape, dtype)


def _after(body, n_in, dep):
    deps = [] if dep is None else [d for d in (dep if isinstance(dep, (list, tuple)) else [dep]) if d is not None]
    if not deps:
        return body, [], []

    def wrapped(*refs):
        return body(*refs[:n_in], *refs[n_in + len(deps):])

    return wrapped, [pl.BlockSpec(memory_space=pl.ANY)] * len(deps), deps


def _mix_in_fwd(x2d, g_mix, w_in_g, layer, tm, dep=None):
    T, D = x2d.shape
    CS = w_in_g.shape[2]
    CN = CS // 3

    def body(x_ref, g_ref, w_ref, h_ref, p_ref):
        x = x_ref[...]
        rstd = lax.rsqrt(jnp.mean(x * x, axis=-1, keepdims=True) + EPS)
        h = (x * rstd * g_ref[...]).astype(BF16)
        h_ref[...] = h
        for s in range(NSHARD):
            for j in range(3):
                c0 = s * CS + j * CN
                p_ref[:, c0:c0 + CN] = _dot(h, w_ref[s, :, j * CN:(j + 1) * CN]).astype(BF16)

    body, dep_spec, dep_arg = _after(body, 3, dep)
    return pl.pallas_call(
        body, name="mix_in_fwd", grid=(T // tm,),
        in_specs=[pl.BlockSpec((tm, D), lambda i: (i, 0)), _res((1, D)),
                  _res((NSHARD, D, CS), lambda i: (0, layer, 0))] + dep_spec,
        out_specs=[pl.BlockSpec((tm, D), lambda i: (i, 0)), pl.BlockSpec((tm, NSHARD * CS), lambda i: (i, 0))],
        out_shape=[_sds((T, D), BF16), _sds((T, NSHARD * CS), BF16)],
        compiler_params=_cparams(("parallel",)),
    )(x2d, g_mix, w_in_g, *dep_arg)


def _halo_maps(tm, n_rows):
    nb = tm // HALO
    last = n_rows // HALO - 1
    prev = lambda i: (jnp.maximum(i * nb - 1, 0), 0)
    nxt = lambda i: (jnp.minimum((i + 1) * nb, last), 0)
    return prev, nxt


def _dwconv(pad_ref, w_ref, out_ref, n_strips, tm, kw, rb):
    off = HALO - (kw - 1) // 2

    def strip(cs, carry):
        for r0 in range(0, tm, rb):
            acc = jnp.zeros((rb, LANES), F32)
            for k in range(kw):
                r = r0 + off + k
                acc = acc + w_ref[cs, k:k + 1, :] * pad_ref[cs, r:r + rb, :]
            out_ref[cs, r0:r0 + rb, :] = acc
        return carry

    lax.fori_loop(0, n_strips, strip, 0)


def _fill_c0_pad(pad_ref, pa_ref, pprev_ref, pnext_ref, D, tm, first, last):
    for cs in range(D // LANES):
        lo, hi = cs * LANES, (cs + 1) * LANES

        def c0_of(ref):
            return ref[:, lo:hi].astype(F32) * _sig(ref[:, D + lo:D + hi].astype(F32))

        pad_ref[cs, HALO:HALO + tm, :] = c0_of(pa_ref)
        pad_ref[cs, 0:HALO, :] = jnp.where(first, 0.0, c0_of(pprev_ref))
        pad_ref[cs, HALO + tm:HALO + tm + HALO, :] = jnp.where(last, 0.0, c0_of(pnext_ref))


def _conv_fwd(proj, conv_w_s, conv_b, ln_g, ln_b, w_sq_g, layer, seq, tm, rb):
    T = proj.shape[0]
    D = conv_b.shape[1]
    DQ = D // NSHARD
    NSTR = D // LANES
    KW = 31
    tps = seq // tm
    prev, nxt = _halo_maps(tm, T)

    def body(pa_ref, pprev_ref, pnext_ref, w_ref, b_ref, g_ref, be_ref, wco_ref,
             c1h_ref, rstd_ref, c3_ref, ya_ref, pad_ref, c1s_ref):
        i = pl.program_id(0)
        first = (i % tps) == 0
        last = (i % tps) == tps - 1
        _fill_c0_pad(pad_ref, pa_ref, pprev_ref, pnext_ref, D, tm, first, last)
        _dwconv(pad_ref, w_ref, c1s_ref, NSTR, tm, KW, rb)
        c1 = jnp.concatenate([c1s_ref[cs] for cs in range(NSTR)], axis=1) + b_ref[...]
        mu = jnp.mean(c1, axis=-1, keepdims=True)
        cc = c1 - mu
        rstd = lax.rsqrt(jnp.mean(cc * cc, axis=-1, keepdims=True) + EPS)
        c1h = cc * rstd
        c1h_ref[...] = c1h.astype(BF16)
        rstd_ref[...] = rstd
        c2 = c1h * g_ref[...] + be_ref[...]
        c3 = (c2 * _sig(c2)).astype(BF16)
        c3_ref[...] = c3
        ya_ref[...] = _dot(c3, wco_ref[...].reshape(D, D)).astype(BF16)

    row = lambda i: (i, 0)
    return pl.pallas_call(
        body, name="conv_fwd", grid=(T // tm,),
        in_specs=[pl.BlockSpec((tm, 2 * D), row), pl.BlockSpec((HALO, 2 * D), prev), pl.BlockSpec((HALO, 2 * D), nxt),
                  _res((NSTR, 32, LANES)), _res((1, D)), _res((1, D)), _res((1, D)),
                  _res((NSHARD, DQ, D), lambda i: (0, layer * 3 + 0, 0))],
        out_specs=[pl.BlockSpec((tm, D), row), pl.BlockSpec((tm, 1), row), pl.BlockSpec((tm, D), row),
                   pl.BlockSpec((tm, D), row)],
        out_shape=[_sds((T, D), BF16), _sds((T, 1), F32), _sds((T, D), BF16), _sds((T, D), BF16)],
        scratch_shapes=[pltpu.VMEM((NSTR, tm + 2 * HALO, LANES), F32), pltpu.VMEM((NSTR, tm, LANES), F32)],
        compiler_params=_cparams(("parallel",)),
    )(proj, proj, proj, conv_w_s, conv_b, ln_g, ln_b, w_sq_g)


def _sgu_merge_fwd(proj, ya, x2d, ln_g, ln_b, ws_b, bs_b, gate_bias, w_sq_g, layer, tm):
    T, D = x2d.shape
    DQ = D // NSHARD
    G, CH, _ = ws_b.shape
    GD = D // G

    def body(puv_ref, pg_ref, ya_ref, x_ref, g_ref, be_ref, ws_ref, bsb_ref, gb_ref, wso_ref, wo_ref,
             mixed_ref, gated_ref, yb_ref, merged_ref, x1_ref, mix_scr):
        u = puv_ref[:, :D].astype(F32)
        v = puv_ref[:, D:].astype(F32)
        mu = jnp.mean(v, axis=-1, keepdims=True)
        vc = v - mu
        rstd = lax.rsqrt(jnp.mean(vc * vc, axis=-1, keepdims=True) + EPS)
        vn = (vc * rstd * g_ref[...] + be_ref[...]).astype(BF16)
        nch = tm // CH
        for g in range(G):
            cols = slice(g * GD, (g + 1) * GD)
            rhs = jnp.concatenate([vn[ch * CH:(ch + 1) * CH, cols] for ch in range(nch)], axis=1)
            res = _dot(ws_ref[g], rhs)
            for ch in range(nch):
                mix_scr[ch * CH:(ch + 1) * CH, cols] = res[:, ch * GD:(ch + 1) * GD] + bsb_ref[:, cols]
        mixed = mix_scr[...]
        mixed_ref[...] = mixed.astype(BF16)
        gated = (u * mixed).astype(BF16)
        gated_ref[...] = gated
        yb = _dot(gated, wso_ref[...].reshape(D, D))
        yb_ref[...] = yb.astype(BF16)
        sa = _sig(pg_ref[:, :D].astype(F32) + gb_ref[:, :D])
        sb = _sig(pg_ref[:, D:].astype(F32) + gb_ref[:, D:])
        merged = (sa * ya_ref[...].astype(F32) + sb * yb).astype(BF16)
        merged_ref[...] = merged
        x1_ref[...] = x_ref[...] + _dot(merged, wo_ref[...].reshape(D, D))

    row = lambda i: (i, 0)
    return pl.pallas_call(
        body, name="sgu_merge_fwd", grid=(T // tm,),
        in_specs=[pl.BlockSpec((tm, 2 * D), lambda i: (i, 1)), pl.BlockSpec((tm, 2 * D), lambda i: (i, 2)),
                  pl.BlockSpec((tm, D), row), pl.BlockSpec((tm, D), row),
                  _res((1, D)), _res((1, D)), _res((G, CH, CH)), _res((CH, D)), _res((1, 2 * D)),
                  _res((NSHARD, DQ, D), lambda i: (0, layer * 3 + 1, 0)),
                  _res((NSHARD, DQ, D), lambda i: (0, layer * 3 + 2, 0))],
        out_specs=[pl.BlockSpec((tm, D), row)] * 5,
        out_shape=[_sds((T, D), BF16)] * 4 + [_sds((T, D), F32)],
        scratch_shapes=[pltpu.VMEM((tm, D), F32)],
        compiler_params=_cparams(("parallel",)),
    )(proj, proj, ya, x2d, ln_g, ln_b, ws_b, bs_b, gate_bias, w_sq_g, w_sq_g)


def _ffn_fwd(x1, g_ffn, wg_g, wu_g, wd_g, layer, tm):
    T, D = x1.shape
    FP = wg_g.shape[2]
    F = NSHARD * FP

    def body(x_ref, g_ref, wg_ref, wu_ref, wd_ref, h2_ref, gt_ref, up_ref, act_ref, x2_ref):
        x = x_ref[...]
        rstd = lax.rsqrt(jnp.mean(x * x, axis=-1, keepdims=True) + EPS)
        h2 = (x * rstd * g_ref[...]).astype(BF16)
        h2_ref[...] = h2
        acc = x
        for s in range(NSHARD):
            gt = _dot(h2, wg_ref[s])
            up = _dot(h2, wu_ref[s])
            gt_ref[:, s * FP:(s + 1) * FP] = gt.astype(BF16)
            up_ref[:, s * FP:(s + 1) * FP] = up.astype(BF16)
            act = (gt * _sig(gt) * up).astype(BF16)
            act_ref[:, s * FP:(s + 1) * FP] = act
            acc = acc + _dot(act, wd_ref[s])
        x2_ref[...] = acc

    row = lambda i: (i, 0)
    return pl.pallas_call(
        body, name="ffn_fwd", grid=(T // tm,),
        in_specs=[pl.BlockSpec((tm, D), row), _res((1, D)),
                  _res((NSHARD, D, FP), lambda i: (0, layer, 0)), _res((NSHARD, D, FP), lambda i: (0, layer, 0)),
                  _res((NSHARD, FP, D), lambda i: (0, layer, 0))],
        out_specs=[pl.BlockSpec((tm, D), row), pl.BlockSpec((tm, F), row), pl.BlockSpec((tm, F), row),
                   pl.BlockSpec((tm, F), row), pl.BlockSpec((tm, D), row)],
        out_shape=[_sds((T, D), BF16), _sds((T, F), BF16), _sds((T, F), BF16), _sds((T, F), BF16), _sds((T, D), F32)],
        compiler_params=_cparams(("parallel",)),
    )(x1, g_ffn, wg_g, wu_g, wd_g)


def _loss_head(xf, g_fin, target, tm):
    T, D = xf.shape
    n = T // tm

    def body(x_ref, g_ref, t_ref, dx_ref, loss_ref, dg_ref, acc_ref):
        i = pl.program_id(0)

        @pl.when(i == 0)
        def _():
            acc_ref[...] = jnp.zeros_like(acc_ref)
            dg_ref[...] = jnp.zeros_like(dg_ref)

        x = x_ref[...]
        g = g_ref[...]
        rstd = lax.rsqrt(jnp.mean(x * x, axis=-1, keepdims=True) + EPS)
        xh = x * rstd
        diff = xh * g - t_ref[...]
        acc_ref[...] += jnp.sum(diff * diff, axis=0, keepdims=True)
        dy = diff * (1.0 / D)
        dg_ref[...] += jnp.sum(dy * xh, axis=0, keepdims=True)
        dxh = dy * g
        dx_ref[...] = rstd * (dxh - xh * jnp.mean(dxh * xh, axis=-1, keepdims=True))

        @pl.when(i == n - 1)
        def _():
            tot = jnp.sum(acc_ref[...], axis=-1, keepdims=True) * (0.5 / D)
            loss_ref[...] = jnp.broadcast_to(tot, loss_ref.shape)

    row = lambda i: (i, 0)
    return pl.pallas_call(
        body, name="loss_head", grid=(n,),
        in_specs=[pl.BlockSpec((tm, D), row), _res((1, D)), pl.BlockSpec((tm, D), row)],
        out_specs=[pl.BlockSpec((tm, D), row), pl.BlockSpec((1, LANES), lambda i: (0, 0)),
                   pl.BlockSpec((1, D), lambda i: (0, 0))],
        out_shape=[_sds((T, D), F32), _sds((1, LANES), F32), _sds((1, D), F32)],
        scratch_shapes=[pltpu.VMEM((1, D), F32)],
        compiler_params=_cparams(("arbitrary",)),
    )(xf, g_fin, target)


def _ffn_bwd(dx2, x1, gt, up, g_ffn, wg_g, wu_g, wd_g, layer, tm, dep=None):
    T, D = x1.shape
    FP = wg_g.shape[2]
    F = NSHARD * FP

    def body(dx2_ref, x1_ref, gt_ref, up_ref, g_ref, wg_ref, wu_ref, wd_ref, dx1_ref, dgt_ref, dup_ref, dg_ref):
        i = pl.program_id(0)

        @pl.when(i == 0)
        def _():
            dg_ref[...] = jnp.zeros_like(dg_ref)

        dx2 = dx2_ref[...]
        dx2b = dx2.astype(BF16)
        dh2 = jnp.zeros((tm, D), F32)
        for s in range(NSHARD):
            dact = _dot_nt(dx2b, wd_ref[s])
            g = gt_ref[:, s * FP:(s + 1) * FP].astype(F32)
            u = up_ref[:, s * FP:(s + 1) * FP].astype(F32)
            sg = _sig(g)
            dup = (dact * (g * sg)).astype(BF16)
            dgt = (dact * u * (sg * (1.0 + g * (1.0 - sg)))).astype(BF16)
            dgt_ref[:, s * FP:(s + 1) * FP] = dgt
            dup_ref[:, s * FP:(s + 1) * FP] = dup
            dh2 = dh2 + _dot_nt(dgt, wg_ref[s]) + _dot_nt(dup, wu_ref[s])
        x = x1_ref[...]
        rstd = lax.rsqrt(jnp.mean(x * x, axis=-1, keepdims=True) + EPS)
        xh = x * rstd
        dg_ref[...] += jnp.sum(dh2 * xh, axis=0, keepdims=True)
        dxh = dh2 * g_ref[...]
        dx1_ref[...] = dx2 + rstd * (dxh - xh * jnp.mean(dxh * xh, axis=-1, keepdims=True))

    row = lambda i: (i, 0)
    body, dep_spec, dep_arg = _after(body, 8, dep)
    return pl.pallas_call(
        body, name="ffn_bwd", grid=(T // tm,),
        in_specs=[pl.BlockSpec((tm, D), row), pl.BlockSpec((tm, D), row), pl.BlockSpec((tm, F), row),
                  pl.BlockSpec((tm, F), row), _res((1, D)),
                  _res((NSHARD, D, FP), lambda i: (0, layer, 0)), _res((NSHARD, D, FP), lambda i: (0, layer, 0)),
                  _res((NSHARD, FP, D), lambda i: (0, layer, 0))] + dep_spec,
        out_specs=[pl.BlockSpec((tm, D), row), pl.BlockSpec((tm, F), row), pl.BlockSpec((tm, F), row),
                   pl.BlockSpec((1, D), lambda i: (0, 0))],
        out_shape=[_sds((T, D), F32), _sds((T, F), BF16), _sds((T, F), BF16), _sds((1, D), F32)],
        compiler_params=_cparams(("arbitrary",)),
    )(dx2, x1, gt, up, g_ffn, wg_g, wu_g, wd_g, *dep_arg)


def _merge_sgu_bwd(dx1, proj, ya, yb, mixed, ln_g, ln_b, ws_b, wst_b, gate_bias, w_sq_g, layer, tm, dep=None):
    T, D = dx1.shape
    DQ = D // NSHARD
    G, CH, _ = ws_b.shape
    GD = D // G

    def body(dx1_ref, puv_ref, pg_ref, ya_ref, yb_ref, mixed_ref, g_ref, be_ref, ws_ref, wst_ref, gb_ref,
             wso_ref, wo_ref, dya_ref, dyb_ref, dp_ref, dgb_ref, dlg_ref, dlb_ref, dbs_ref, dws_ref,
             dvn_scr, dbs_scr):
        i = pl.program_id(0)

        @pl.when(i == 0)
        def _():
            for r in (dgb_ref, dlg_ref, dlb_ref, dws_ref, dbs_scr):
                r[...] = jnp.zeros_like(r)

        dmerged = _dot_nt(dx1_ref[...].astype(BF16), wo_ref[...].reshape(D, D))
        sa = _sig(pg_ref[:, :D].astype(F32) + gb_ref[:, :D])
        sb = _sig(pg_ref[:, D:].astype(F32) + gb_ref[:, D:])
        dya = (dmerged * sa).astype(BF16)
        dyb = (dmerged * sb).astype(BF16)
        dya_ref[...] = dya
        dyb_ref[...] = dyb
        dga = dmerged * ya_ref[...].astype(F32) * (sa * (1.0 - sa))
        dgb = dmerged * yb_ref[...].astype(F32) * (sb * (1.0 - sb))
        dp_ref[:, 4 * D:5 * D] = dga.astype(BF16)
        dp_ref[:, 5 * D:6 * D] = dgb.astype(BF16)
        dgb_ref[:, :D] += jnp.sum(dga, axis=0, keepdims=True)
        dgb_ref[:, D:] += jnp.sum(dgb, axis=0, keepdims=True)

        dgated = _dot_nt(dyb, wso_ref[...].reshape(D, D))
        u = puv_ref[:, :D].astype(F32)
        v = puv_ref[:, D:].astype(F32)
        dp_ref[:, 2 * D:3 * D] = (dgated * mixed_ref[...].astype(F32)).astype(BF16)
        dmixed = dgated * u
        mu = jnp.mean(v, axis=-1, keepdims=True)
        vc = v - mu
        rstd = lax.rsqrt(jnp.mean(vc * vc, axis=-1, keepdims=True) + EPS)
        vh = vc * rstd
        vn = (vh * g_ref[...] + be_ref[...]).astype(BF16)
        dmb = dmixed.astype(BF16)
        nch = tm // CH
        bs_part = dmixed[0:CH, :]
        for ch in range(1, nch):
            bs_part = bs_part + dmixed[ch * CH:(ch + 1) * CH, :]
        dbs_scr[...] += bs_part
        for g in range(G):
            cols = slice(g * GD, (g + 1) * GD)
            dm_g = jnp.concatenate([dmb[ch * CH:(ch + 1) * CH, cols] for ch in range(nch)], axis=1)
            vn_g = jnp.concatenate([vn[ch * CH:(ch + 1) * CH, cols] for ch in range(nch)], axis=1)
            dws_ref[g] += _dot_nt(dm_g, vn_g)
            dvn_g = _dot(wst_ref[g], dm_g)
            for ch in range(nch):
                dvn_scr[ch * CH:(ch + 1) * CH, cols] = dvn_g[:, ch * GD:(ch + 1) * GD]
        dvn = dvn_scr[...]
        dlg_ref[...] += jnp.sum(dvn * vh, axis=0, keepdims=True)
        dlb_ref[...] += jnp.sum(dvn, axis=0, keepdims=True)
        dxh = dvn * g_ref[...]
        dv = rstd * (dxh - jnp.mean(dxh, axis=-1, keepdims=True) - vh * jnp.mean(dxh * vh, axis=-1, keepdims=True))
        dp_ref[:, 3 * D:4 * D] = dv.astype(BF16)

        @pl.when(i == pl.num_programs(0) - 1)
        def _():
            for g in range(G):
                blk = dbs_scr[:, g * GD:(g + 1) * GD]
                if GD != CH:
                    blk = jnp.concatenate([blk, jnp.zeros((CH, CH - GD), F32)], axis=1)
                dbs_ref[:, g * CH:(g + 1) * CH] = jnp.sum(blk.T, axis=0, keepdims=True)

    row = lambda i: (i, 0)
    fixed2 = lambda i: (0, 0)
    body, dep_spec, dep_arg = _after(body, 13, dep)
    return pl.pallas_call(
        body, name="merge_sgu_bwd", grid=(T // tm,),
        in_specs=[pl.BlockSpec((tm, D), row), pl.BlockSpec((tm, 2 * D), lambda i: (i, 1)),
                  pl.BlockSpec((tm, 2 * D), lambda i: (i, 2)), pl.BlockSpec((tm, D), row), pl.BlockSpec((tm, D), row),
                  pl.BlockSpec((tm, D), row), _res((1, D)), _res((1, D)), _res((G, CH, CH)), _res((G, CH, CH)),
                  _res((1, 2 * D)),
                  _res((NSHARD, DQ, D), lambda i: (0, layer * 3 + 1, 0)),
                  _res((NSHARD, DQ, D), lambda i: (0, layer * 3 + 2, 0))] + dep_spec,
        out_specs=[pl.BlockSpec((tm, D), row), pl.BlockSpec((tm, D), row),
                   pl.BlockSpec((tm, 6 * D), row),
                   pl.BlockSpec((1, 2 * D), fixed2), pl.BlockSpec((1, D), fixed2), pl.BlockSpec((1, D), fixed2),
                   pl.BlockSpec((1, G * CH), fixed2), pl.BlockSpec((G, CH, CH), lambda i: (0, 0, 0))],
        out_shape=[_sds((T, D), BF16), _sds((T, D), BF16), _sds((T, 6 * D), BF16),
                   _sds((1, 2 * D), F32), _sds((1, D), F32), _sds((1, D), F32), _sds((1, G * CH), F32),
                   _sds((G, CH, CH), F32)],
        scratch_shapes=[pltpu.VMEM((tm, D), F32), pltpu.VMEM((CH, D), F32)],
        compiler_params=_cparams(("arbitrary",)),
    )(dx1, proj, proj, ya, yb, mixed, ln_g, ln_b, ws_b, wst_b, gate_bias, w_sq_g, w_sq_g, *dep_arg)


def _conv_ln_bwd(dya, c1h, rstd_c, ln_g, ln_b, w_sq_g, layer, tm):
    T, D = dya.shape
    DQ = D // NSHARD

    def body(dya_ref, c1h_ref, rstd_ref, g_ref, be_ref, wco_ref, dc1_ref, dlg_ref, dlb_ref, dcb_ref):
        i = pl.program_id(0)

        @pl.when(i == 0)
        def _():
            for r in (dlg_ref, dlb_ref, dcb_ref):
                r[...] = jnp.zeros_like(r)

        dc3 = _dot_nt(dya_ref[...], wco_ref[...].reshape(D, D))
        c1h = c1h_ref[...].astype(F32)
        c2 = c1h * g_ref[...] + be_ref[...]
        sg = _sig(c2)
        dc2 = dc3 * (sg * (1.0 + c2 * (1.0 - sg)))
        dlg_ref[...] += jnp.sum(dc2 * c1h, axis=0, keepdims=True)
        dlb_ref[...] += jnp.sum(dc2, axis=0, keepdims=True)
        dxh = dc2 * g_ref[...]
        dc1 = rstd_ref[...] * (dxh - jnp.mean(dxh, axis=-1, keepdims=True)
                               - c1h * jnp.mean(dxh * c1h, axis=-1, keepdims=True))
        dc1_ref[...] = dc1
        dcb_ref[...] += jnp.sum(dc1, axis=0, keepdims=True)

    row = lambda i: (i, 0)
    fixed2 = lambda i: (0, 0)
    return pl.pallas_call(
        body, name="conv_ln_bwd", grid=(T // tm,),
        in_specs=[pl.BlockSpec((tm, D), row), pl.BlockSpec((tm, D), row), pl.BlockSpec((tm, 1), row),
                  _res((1, D)), _res((1, D)), _res((NSHARD, DQ, D), lambda i: (0, layer * 3 + 0, 0))],
        out_specs=[pl.BlockSpec((tm, D), row), pl.BlockSpec((1, D), fixed2), pl.BlockSpec((1, D), fixed2),
                   pl.BlockSpec((1, D), fixed2)],
        out_shape=[_sds((T, D), F32), _sds((1, D), F32), _sds((1, D), F32), _sds((1, D), F32)],
        compiler_params=_cparams(("arbitrary",)),
    )(dya, c1h, rstd_c, ln_g, ln_b, w_sq_g)


def _conv_bwd(dc1, proj, dp3, conv_wf_s, seq, tm, rb, dep=None):
    T, D = dc1.shape
    NSTR = D // LANES
    KW = 31
    PADK = (KW - 1) // 2
    tps = seq // tm
    prev, nxt = _halo_maps(tm, T)
    n = T // tm

    def body(dc_ref, dcprev_ref, dcnext_ref, pa_ref, pprev_ref, pnext_ref, wf_ref, dp_in_ref,
             dp_ref, dw_ref, pad_ref, dpad_ref, dc0_ref, dwacc_ref):
        del dp_in_ref
        i = pl.program_id(0)
        first = (i % tps) == 0
        last = (i % tps) == tps - 1

        @pl.when(i == 0)
        def _():
            dwacc_ref[...] = jnp.zeros_like(dwacc_ref)

        _fill_c0_pad(pad_ref, pa_ref, pprev_ref, pnext_ref, D, tm, first, last)
        for cs in range(NSTR):
            lo, hi = cs * LANES, (cs + 1) * LANES
            dpad_ref[cs, HALO:HALO + tm, :] = dc_ref[:, lo:hi]
            dpad_ref[cs, 0:HALO, :] = jnp.where(first, 0.0, dcprev_ref[:, lo:hi])
            dpad_ref[cs, HALO + tm:HALO + tm + HALO, :] = jnp.where(last, 0.0, dcnext_ref[:, lo:hi])
        _dwconv(dpad_ref, wf_ref, dc0_ref, NSTR, tm, KW, rb)

        def strip(cs, carry):
            for r0 in range(0, tm, rb):
                d = dpad_ref[cs, HALO + r0:HALO + r0 + rb, :]
                for k in range(KW):
                    r = r0 + HALO - PADK + k
                    prod = d * pad_ref[cs, r:r + rb, :]
                    dwacc_ref[cs, k * 8:(k + 1) * 8, :] += jnp.sum(prod.reshape(rb // 8, 8, LANES), axis=0)
            return carry

        lax.fori_loop(0, NSTR, strip, 0)

        for cs in range(NSTR):
            lo, hi = cs * LANES, (cs + 1) * LANES
            av = pa_ref[:, lo:hi].astype(F32)
            sg = _sig(pa_ref[:, D + lo:D + hi].astype(F32))
            dc0 = dc0_ref[cs]
            dp_ref[:, lo:hi] = (dc0 * sg).astype(BF16)
            dp_ref[:, D + lo:D + hi] = (dc0 * av * (sg * (1.0 - sg))).astype(BF16)

        @pl.when(i == n - 1)
        def _():
            for cs in range(NSTR):
                dw_ref[cs] = jnp.sum(dwacc_ref[cs].reshape(32, 8, LANES), axis=1)

    row = lambda i: (i, 0)
    body, dep_spec, dep_arg = _after(body, 8, dep)
    return pl.pallas_call(
        body, name="conv_bwd", grid=(n,),
        in_specs=[pl.BlockSpec((tm, D), row), pl.BlockSpec((HALO, D), prev), pl.BlockSpec((HALO, D), nxt),
                  pl.BlockSpec((tm, 2 * D), row), pl.BlockSpec((HALO, 2 * D), prev), pl.BlockSpec((HALO, 2 * D), nxt),
                  _res((NSTR, 32, LANES)), pl.BlockSpec(memory_space=pl.ANY)] + dep_spec,
        out_specs=[pl.BlockSpec((tm, 2 * D), row),
                   pl.BlockSpec((NSTR, 32, LANES), lambda i: (0, 0, 0))],
        out_shape=[_sds(dp3.shape, BF16), _sds((NSTR, 32, LANES), F32)],
        scratch_shapes=[pltpu.VMEM((NSTR, tm + 2 * HALO, LANES), F32), pltpu.VMEM((NSTR, tm + 2 * HALO, LANES), F32),
                        pltpu.VMEM((NSTR, tm, LANES), F32), pltpu.VMEM((NSTR, 32 * 8, LANES), F32)],
        input_output_aliases={7: 0},
        compiler_params=_cparams(("arbitrary",)),
    )(dc1, dc1, dc1, proj, proj, proj, conv_wf_s, dp3, *dep_arg)


def _mix_in_bwd(dx1, dp3, x2d, g_mix, w_in_g, layer, tm, dep=None):
    T, D = x2d.shape
    CS = w_in_g.shape[2]
    CN = CS // 3

    def body(dx1_ref, dp_ref, x_ref, g_ref, w_ref, dx_ref, dg_ref):
        i = pl.program_id(0)

        @pl.when(i == 0)
        def _():
            dg_ref[...] = jnp.zeros_like(dg_ref)

        dh = jnp.zeros((tm, D), F32)
        for j in range(12):
            dh = dh + _dot_nt(dp_ref[:, j * CN:(j + 1) * CN], w_ref[j // 3, :, (j % 3) * CN:(j % 3 + 1) * CN])
        x = x_ref[...]
        rstd = lax.rsqrt(jnp.mean(x * x, axis=-1, keepdims=True) + EPS)
        xh = x * rstd
        dg_ref[...] += jnp.sum(dh * xh, axis=0, keepdims=True)
        dxh = dh * g_ref[...]
        dx_ref[...] = dx1_ref[...] + rstd * (dxh - xh * jnp.mean(dxh * xh, axis=-1, keepdims=True))

    row = lambda i: (i, 0)
    body, dep_spec, dep_arg = _after(body, 5, dep)
    return pl.pallas_call(
        body, name="mix_in_bwd", grid=(T // tm,),
        in_specs=[pl.BlockSpec((tm, D), row), pl.BlockSpec((tm, 6 * D), row),
                  pl.BlockSpec((tm, D), row), _res((1, D)), _res((NSHARD, D, CS), lambda i: (0, layer, 0))] + dep_spec,
        out_specs=[pl.BlockSpec((tm, D), row), pl.BlockSpec((1, D), lambda i: (0, 0))],
        out_shape=[_sds((T, D), F32), _sds((1, D), F32)],
        compiler_params=_cparams(("arbitrary",)),
    )(dx1, dp3, x2d, g_mix, w_in_g, *dep_arg)


def _tn_matmul(name, a, a_block, a_map, bs, b_block, b_map, out_shape, out_block, out_map, nj, nt):
    kk = [d for d in a_block if d is not None][-1]
    nn = [d for d in b_block if d is not None][-1]
    nb = len(bs)

    def body(*refs):
        a_ref, b_refs = refs[0], refs[1:1 + nb]
        o_refs, acc_refs = refs[-2 * nb:-nb], refs[-nb:]
        t = pl.program_id(1)

        @pl.when(t == 0)
        def _():
            for acc_ref in acc_refs:
                acc_ref[...] = jnp.zeros_like(acc_ref)

        a_t = a_ref[...].astype(BF16)
        for b_ref, acc_ref in zip(b_refs, acc_refs):
            acc_ref[...] += _dot_tn(a_t, b_ref[...].astype(BF16))

        @pl.when(t == nt - 1)
        def _():
            for o_ref, acc_ref in zip(o_refs, acc_refs):
                o_ref[...] = acc_ref[...].astype(o_ref.dtype)

    return pl.pallas_call(
        body, name=name, grid=(nj, nt),
        in_specs=[pl.BlockSpec(a_block, a_map)] + [pl.BlockSpec(b_block, b_map)] * nb,
        out_specs=[pl.BlockSpec(out_block, out_map)] * nb, out_shape=[_sds(out_shape, BF16)] * nb,
        scratch_shapes=[pltpu.VMEM((kk, nn), F32)] * nb,
        compiler_params=_cparams(("parallel", "arbitrary")),
    )(a, *bs)


def _place_shard(name, w, pos, dtype, tr, dep=None):
    R, C = w.shape

    def body(pos_ref, w_ref, o_ref):
        del pos_ref
        o_ref[...] = w_ref[...].astype(dtype)

    body, dep_spec, dep_arg = _after(body, 2, dep)
    grid_spec = pltpu.PrefetchScalarGridSpec(
        num_scalar_prefetch=1, grid=(R // tr,),
        in_specs=[pl.BlockSpec((tr, C), lambda r, pos: (r, 0))] + dep_spec,
        out_specs=pl.BlockSpec((None, tr, C), lambda r, pos: (pos[1], r, 0)))
    return pl.pallas_call(body, name=name, grid_spec=grid_spec, out_shape=_sds((NSHARD, R, C), dtype),
                          compiler_params=_cparams(("parallel",)))(pos, w, *dep_arg)


def _add_halves(name, g, rbuf, pos, tr):
    NS, _, H, C = g.shape

    def body(pos_ref, g_ref, r_ref, o_ref):
        del pos_ref
        o_ref[...] = (g_ref[...].astype(F32) + r_ref[...].astype(F32)).astype(BF16)

    grid_spec = pltpu.PrefetchScalarGridSpec(
        num_scalar_prefetch=1, grid=(NS, H // tr),
        in_specs=[pl.BlockSpec((None, None, tr, C), lambda s, r, pos: (s, pos[0], r, 0)),
                  pl.BlockSpec((None, tr, C), lambda s, r, pos: (s, r, 0))],
        out_specs=pl.BlockSpec((None, tr, C), lambda s, r, pos: (s, r, 0)))
    return pl.pallas_call(body, name=name, grid_spec=grid_spec, out_shape=_sds((NS, H, C), BF16),
                          compiler_params=_cparams(("parallel", "parallel")))(pos, g, rbuf)


def _add_shards(name, p, rbuf, pos, tr, layer, n_layers, prev):
    _, H, C = p.shape

    def body(pos_ref, p_ref, r_ref, *rest):
        del pos_ref
        o_ref = rest[-1]
        acc = p_ref[...].astype(F32)
        for j in range(3):
            acc = acc + r_ref[j].astype(F32)
        o_ref[...] = acc

    in_specs = [pl.BlockSpec((None, tr, C), lambda r, pos: (pos[1], r, 0)),
                pl.BlockSpec((3, tr, C), lambda r, pos: (0, r, 0))]
    args = [pos, p, rbuf]
    aliases = {}
    if prev is not None:
        in_specs.append(pl.BlockSpec(memory_space=pl.ANY))
        args.append(prev)
        aliases = {3: 0}
    grid_spec = pltpu.PrefetchScalarGridSpec(
        num_scalar_prefetch=1, grid=(H // tr,), in_specs=in_specs,
        out_specs=pl.BlockSpec((None, None, tr, C), lambda r, pos: (layer, pos[0], r, 0)))
    return pl.pallas_call(body, name=name, grid_spec=grid_spec, out_shape=_sds((n_layers, 2, H, C), F32),
                          input_output_aliases=aliases, compiler_params=_cparams(("parallel",)))(*args)


def _sum_slots(own, land, me, tr):
    NS8, R, C = land.shape

    def body(me_ref, own_ref, l_ref, o_ref):
        acc = None
        for j in range(NS8):
            term = jnp.where(me_ref[0] == j, own_ref[...], l_ref[j])
            acc = term if acc is None else acc + term
        o_ref[...] = acc

    grid_spec = pltpu.PrefetchScalarGridSpec(
        num_scalar_prefetch=1, grid=(R // tr,),
        in_specs=[pl.BlockSpec((tr, C), lambda i, me: (i, 0)), pl.BlockSpec((NS8, tr, C), lambda i, me: (0, i, 0))],
        out_specs=pl.BlockSpec((tr, C), lambda i, me: (i, 0)))
    return pl.pallas_call(body, name="sum_slots", grid_spec=grid_spec, out_shape=_sds((R, C), F32),
                          compiler_params=_cparams(("parallel",)))(me, own, land)


def _adamw(name, w, g, m, v, tr):
    R, C = w.shape
    c1 = 1.0 - ADAM_B1 ** ADAM_STEP
    c2 = 1.0 - ADAM_B2 ** ADAM_STEP

    def body(w_ref, g_ref, m_ref, v_ref, d_ref, mo_ref, vo_ref):
        g_ = g_ref[...]
        m_ = ADAM_B1 * m_ref[...] + (1.0 - ADAM_B1) * g_
        v_ = ADAM_B2 * v_ref[...] + (1.0 - ADAM_B2) * (g_ * g_)
        mo_ref[...] = m_
        vo_ref[...] = v_
        d_ref[...] = -ADAM_LR * ((m_ / c1) / (jnp.sqrt(v_ / c2) + ADAM_EPS) + ADAM_WD * w_ref[...])

    spec = pl.BlockSpec((tr, C), lambda i: (i, 0))
    return pl.pallas_call(
        body, name=name, grid=(R // tr,), in_specs=[spec] * 4, out_specs=[spec] * 3,
        out_shape=[_sds((R, C), F32)] * 3, compiler_params=_cparams(("parallel",)))(w, g, m, v)


def _row_tile(rows, cap):
    best = rows
    for t in range(8, min(rows, cap) + 1, 8):
        if rows % t == 0:
            best = t
    return best


HBM_SPEC = pl.BlockSpec(memory_space=pltpu.HBM)
SEM_SPEC = pl.BlockSpec(memory_space=pltpu.SEMAPHORE)
DATAFLOW = pltpu.SideEffectType.DATAFLOW_SIDE_EFFECTING
DMA_SEM = pltpu.SemaphoreType.DMA


def _hbm(a):
    return pltpu.with_memory_space_constraint(a, pltpu.HBM)


def _mesh_pos():
    return lax.axis_index("x"), lax.axis_index("y"), lax.axis_index("c")


def _other_chips(x, y):
    return [(1 - x, y), (x, 1 - y), (1 - x, 1 - y)]


def _half_rows(buf, shard, core):
    h = buf.shape[1] // 2
    return buf.at[shard, pl.ds(core * h, h), :]


def _ici_copy(buf, j, send, recv, landing):
    x, y, c = _mesh_pos()
    px, py = _other_chips(x, y)[j]
    part = _half_rows(buf, 2 * px + py if landing else 2 * x + y, c)
    return pltpu.make_async_remote_copy(src_ref=part, dst_ref=part, send_sem=send, recv_sem=recv,
                                        device_id=(px, py, c), device_id_type=MESH_ID)


def _sibling_copy(buf, j, send, recv, landing):
    x, y, c = _mesh_pos()
    px, py = _other_chips(x, y)[j]
    part = _half_rows(buf, 2 * px + py, 1 - c if landing else c)
    return pltpu.make_async_remote_copy(src_ref=part, dst_ref=part, send_sem=send, recv_sem=recv,
                                        device_id=(x, y, 1 - c), device_id_type=MESH_ID)


def _forward_sibling(name, bufs, with_ici):
    n = len(bufs)

    def body(*refs):
        ins = refs[:n]
        send_ici, recv_ici, send_d2d, recv_d2d = refs[2 * n:]
        sends = []
        if with_ici:
            for i in range(n):
                for j in range(3):
                    cp = _ici_copy(ins[i], j, send_ici.at[i, j], recv_ici.at[i, j], False)
                    cp.start()
                    sends.append(cp)
        for i in range(n):
            for j in range(3):
                if with_ici:
                    _ici_copy(ins[i], j, send_ici.at[i, j], recv_ici.at[i, j], True).wait_recv()
                cp = _sibling_copy(ins[i], j, send_d2d.at[i, j], recv_d2d.at[i, j], False)
                cp.start()
                sends.append(cp)
        for i in range(n):
            for j in range(3):
                _sibling_copy(ins[i], j, send_d2d.at[i, j], recv_d2d.at[i, j], True).wait_recv()
        for cp in sends:
            cp.wait_send()

    return pl.pallas_call(
        body, name=name, in_specs=[HBM_SPEC] * n, out_specs=[HBM_SPEC] * n,
        out_shape=[_sds(b.shape, b.dtype) for b in bufs],
        scratch_shapes=[DMA_SEM((n, 3))] * 4, input_output_aliases={i: i for i in range(n)},
    )(*bufs)


def _gather_start(name, groups):
    flat = [b for g in groups for b in g]
    n, ng = len(flat), len(groups)

    def body(*refs):
        ins, sems, token = refs[:n], refs[n:n + 2 * ng], refs[-1]
        k = 0
        for gi, g in enumerate(groups):
            for a in range(len(g)):
                for j in range(3):
                    _ici_copy(ins[k], j, sems[2 * gi], sems[2 * gi + 1], False).start()
                k += 1
        token[...] = jnp.zeros_like(token)

    res = pl.pallas_call(
        body, name=name, in_specs=[HBM_SPEC] * n,
        out_specs=[SEM_SPEC] * (2 * ng) + [HBM_SPEC] * n + [pl.BlockSpec(memory_space=pltpu.VMEM)],
        out_shape=[DMA_SEM(()) for g in groups for _ in range(2)]
        + [pltpu.HBM(b.shape, b.dtype) for b in flat] + [_sds((8, LANES), F32)],
        input_output_aliases={i: 2 * ng + i for i in range(n)},
        compiler_params=pltpu.CompilerParams(has_side_effects=DATAFLOW),
    )(*[_hbm(b) for b in flat])
    sems = [(res[2 * gi], res[2 * gi + 1]) for gi in range(ng)]
    thru, k = [], 2 * ng
    for g in groups:
        thru.append(list(res[k:k + len(g)]))
        k += len(g)
    return sems, thru, res[-1]


def _gather_wait(name, bufs, sems, after):
    n = len(bufs)

    def body(*refs):
        ins, send, recv = refs[:n], refs[n], refs[n + 1]
        for a in range(n):
            for j in range(3):
                _ici_copy(ins[a], j, send, recv, False).wait_send()
                _ici_copy(ins[a], j, send, recv, True).wait_recv()

    return pl.pallas_call(
        body, name=name, in_specs=[HBM_SPEC] * n + [SEM_SPEC, SEM_SPEC, pl.BlockSpec(memory_space=pl.ANY)],
        out_specs=[HBM_SPEC] * n, out_shape=[pltpu.HBM(b.shape, b.dtype) for b in bufs],
        input_output_aliases={i: i for i in range(n)},
        compiler_params=pltpu.CompilerParams(has_side_effects=DATAFLOW),
    )(*bufs, sems[0], sems[1], after)


def _send_sibling_halves(name, arrs):
    n = len(arrs)

    def body(*refs):
        ins, outs = refs[:n], refs[n:2 * n]
        send, recv = refs[2 * n:]
        x, y, c = _mesh_pos()
        cps = []
        for i in range(n):
            cp = pltpu.make_async_remote_copy(
                src_ref=ins[i].at[:, 1 - c], dst_ref=outs[i],
                send_sem=send.at[i], recv_sem=recv.at[i], device_id=(x, y, 1 - c), device_id_type=MESH_ID)
            cp.start()
            cps.append(cp)
        for cp in cps:
            cp.wait()

    return pl.pallas_call(
        body, name=name, in_specs=[HBM_SPEC] * n, out_specs=[HBM_SPEC] * n,
        out_shape=[_sds((a.shape[0],) + a.shape[2:], a.dtype) for a in arrs],
        scratch_shapes=[DMA_SEM((n,)), DMA_SEM((n,))],
    )(*arrs)


def _chip_copy(p, land, j, send, recv):
    x, y, c = _mesh_pos()
    px, py = _other_chips(x, y)[j]
    return pltpu.make_async_remote_copy(src_ref=p.at[2 * px + py], dst_ref=land.at[j], send_sem=send, recv_sem=recv,
                                        device_id=(px, py, c), device_id_type=MESH_ID)


def _chip_send_start(name, ps):
    n = len(ps)
    lands = [lax.empty((3,) + p.shape[1:], p.dtype) for p in ps]

    def body(*refs):
        ins, lnd, send, recv, token = refs[:n], refs[n:2 * n], refs[2 * n], refs[2 * n + 1], refs[-1]
        for i in range(n):
            for j in range(3):
                _chip_copy(ins[i], lnd[i], j, send, recv).start()
        token[...] = jnp.zeros_like(token)

    res = pl.pallas_call(
        body, name=name, in_specs=[HBM_SPEC] * (2 * n),
        out_specs=[SEM_SPEC, SEM_SPEC] + [HBM_SPEC] * (2 * n) + [pl.BlockSpec(memory_space=pltpu.VMEM)],
        out_shape=[DMA_SEM(()), DMA_SEM(())] + [pltpu.HBM(a.shape, a.dtype) for a in ps + lands]
        + [_sds((8, LANES), F32)],
        input_output_aliases={i: 2 + i for i in range(2 * n)},
        compiler_params=pltpu.CompilerParams(has_side_effects=DATAFLOW),
    )(*[_hbm(a) for a in ps + lands])
    return (res[0], res[1]), list(res[2:2 + n]), list(res[2 + n:2 + 2 * n]), res[-1]


def _chip_send_wait(name, ps, lands, sems, after):
    n = len(ps)

    def body(*refs):
        ins, lnd, send, recv = refs[:n], refs[n:2 * n], refs[2 * n], refs[2 * n + 1]
        for i in range(n):
            for j in range(3):
                cp = _chip_copy(ins[i], lnd[i], j, send, recv)
                cp.wait_send()
                cp.wait_recv()

    res = pl.pallas_call(
        body, name=name, in_specs=[HBM_SPEC] * (2 * n) + [SEM_SPEC, SEM_SPEC, pl.BlockSpec(memory_space=pl.ANY)],
        out_specs=[HBM_SPEC] * (2 * n), out_shape=[pltpu.HBM(a.shape, a.dtype) for a in ps + lands],
        input_output_aliases={i: i for i in range(2 * n)},
        compiler_params=pltpu.CompilerParams(has_side_effects=DATAFLOW),
    )(*ps, *lands, sems[0], sems[1], after)
    return list(res[:n]), list(res[n:])


def _join_halves(arrs):
    n = len(arrs)

    def body(*refs):
        bufs = refs[n:2 * n]
        send, recv = refs[2 * n:]
        x, y, c = _mesh_pos()
        cps = []
        for i in range(n):
            mine = bufs[i].at[:, c]
            cp = pltpu.make_async_remote_copy(
                src_ref=mine, dst_ref=mine, send_sem=send.at[i], recv_sem=recv.at[i],
                device_id=(x, y, 1 - c), device_id_type=MESH_ID)
            cp.start()
            cps.append(cp)
        for i, cp in enumerate(cps):
            theirs = bufs[i].at[:, 1 - c]
            cp.wait_send()
            pltpu.make_async_remote_copy(
                src_ref=theirs, dst_ref=theirs, send_sem=send.at[i], recv_sem=recv.at[i],
                device_id=(x, y, 1 - c), device_id_type=MESH_ID).wait_recv()

    return pl.pallas_call(
        body, name="join_halves", in_specs=[HBM_SPEC] * n, out_specs=[HBM_SPEC] * n,
        out_shape=[_sds(a.shape, a.dtype) for a in arrs],
        scratch_shapes=[DMA_SEM((n,)), DMA_SEM((n,))], input_output_aliases={i: i for i in range(n)},
    )(*arrs)


def _peer_copy(buf, land, k, send, recv, landing):
    x, y, c = _mesh_pos()
    px, py, pc = x ^ ((k >> 2) & 1), y ^ ((k >> 1) & 1), c ^ (k & 1)
    slot = 4 * px + 2 * py + pc if landing else 4 * x + 2 * y + c
    return pltpu.make_async_remote_copy(src_ref=buf, dst_ref=land.at[slot], send_sem=send, recv_sem=recv,
                                        device_id=(px, py, pc), device_id_type=MESH_ID)


def _exchange_all(buf):
    def body(in_ref, out_ref, send, recv):
        cps = [_peer_copy(in_ref, out_ref, k, send.at[k - 1], recv.at[k - 1], False) for k in range(1, 8)]
        for cp in cps:
            cp.start()
        for k in range(1, 8):
            cps[k - 1].wait_send()
            _peer_copy(in_ref, out_ref, k, send.at[k - 1], recv.at[k - 1], True).wait_recv()

    return pl.pallas_call(
        body, name="exchange_all", in_specs=[HBM_SPEC], out_specs=HBM_SPEC,
        out_shape=_sds((8,) + buf.shape, buf.dtype), scratch_shapes=[DMA_SEM((7,)), DMA_SEM((7,))],
    )(buf)


def _exchange_start(name, buf):
    land = lax.empty((8,) + buf.shape, buf.dtype)

    def body(in_ref, land_ref, send, recv, in_thru, land_thru, token):
        for k in range(1, 8):
            _peer_copy(in_ref, land_ref, k, send, recv, False).start()
        token[...] = jnp.zeros_like(token)

    res = pl.pallas_call(
        body, name=name, in_specs=[HBM_SPEC] * 2,
        out_specs=[SEM_SPEC, SEM_SPEC, HBM_SPEC, HBM_SPEC, pl.BlockSpec(memory_space=pltpu.VMEM)],
        out_shape=[DMA_SEM(()), DMA_SEM(()), pltpu.HBM(buf.shape, buf.dtype), pltpu.HBM(land.shape, land.dtype),
                   _sds((8, LANES), F32)],
        input_output_aliases={0: 2, 1: 3}, compiler_params=pltpu.CompilerParams(has_side_effects=DATAFLOW),
    )(_hbm(buf), _hbm(land))
    return (res[0], res[1]), res[2], res[3], res[4]


def _exchange_wait(name, buf, land, sems, after):
    def body(in_ref, land_ref, send, recv, after_ref, in_thru, land_thru):
        for k in range(1, 8):
            _peer_copy(in_ref, land_ref, k, send, recv, False).wait_send()
            _peer_copy(in_ref, land_ref, k, send, recv, True).wait_recv()

    res = pl.pallas_call(
        body, name=name, in_specs=[HBM_SPEC, HBM_SPEC, SEM_SPEC, SEM_SPEC, pl.BlockSpec(memory_space=pl.ANY)],
        out_specs=[HBM_SPEC, HBM_SPEC], out_shape=[pltpu.HBM(buf.shape, buf.dtype), pltpu.HBM(land.shape, land.dtype)],
        input_output_aliases={0: 0, 1: 1}, compiler_params=pltpu.CompilerParams(has_side_effects=DATAFLOW),
    )(buf, land, sems[0], sems[1], after)
    return res[0], res[1]


def _pad_to(a, axis, size):
    pad = [(0, 0)] * a.ndim
    pad[axis] = (0, size - a.shape[axis])
    return jnp.pad(a, pad)


def _strips(w):
    k, d = w.shape
    return _pad_to(w, 0, 32).reshape(32, d // LANES, LANES).transpose(1, 0, 2)


def kernel(x, norm_mix, w_in, gate_bias, conv_w, conv_b, conv_ln_g, conv_ln_b, w_conv_out, sgu_ln_g, sgu_ln_b, w_spatial, b_spatial, w_sgu_out, w_o, norm_ffn, w_ffn_gate, w_ffn_up, w_ffn_down, norm_final, loss_target, m_norm_mix, m_w_in, m_gate_bias, m_conv_w, m_conv_b, m_conv_ln_g, m_conv_ln_b, m_w_conv_out, m_sgu_ln_g, m_sgu_ln_b, m_w_spatial, m_b_spatial, m_w_sgu_out, m_w_o, m_norm_ffn, m_w_ffn_gate, m_w_ffn_up, m_w_ffn_down, m_norm_final, v_norm_mix, v_w_in, v_gate_bias, v_conv_w, v_conv_b, v_conv_ln_g, v_conv_ln_b, v_w_conv_out, v_sgu_ln_g, v_sgu_ln_b, v_w_spatial, v_b_spatial, v_w_sgu_out, v_w_o, v_norm_ffn, v_w_ffn_gate, v_w_ffn_up, v_w_ffn_down, v_norm_final):
    BL, S, D = x.shape
    T = BL * S
    L = w_in.shape[0]
    CS = w_in.shape[2]
    CN = CS // 3
    DQ = D // NSHARD
    FS = w_ffn_gate.shape[2]
    FP = -(-FS // 256) * 256
    G, CH = w_spatial.shape[1], w_spatial.shape[2]
    KW = conv_w.shape[1]
    CQ = conv_w.shape[3]
    NSTR = D // LANES
    tm = min(512, S // 2)
    tm2 = max(tm // 2, CH)
    rb = min(64, tm)
    mx, my, mc = _mesh_pos()
    pos = jnp.stack([mc, 2 * mx + my]).astype(jnp.int32)

    def placed(name, w, dtype=BF16, dep=None):
        return _place_shard("place_" + name, w, pos, dtype, _row_tile(w.shape[0], 256), dep)

    w_in0 = placed("w_in", w_in[0])
    cw_p = placed("conv_w", _pad_to(conv_w.reshape(L, KW, CQ), 1, 32).reshape(L * 32, CQ), F32)
    fsems, fflying, ftoken = _gather_start("gather_start_first", [[w_in0, cw_p]])
    wts = []
    for l in range(L):
        w_sq = jnp.concatenate([w_conv_out[l], w_sgu_out[l], w_o[l]], axis=0)
        wts.append(dict(w_in=placed("w_in", w_in[l], dep=ftoken) if l else None, w_sq=placed("w_sq", w_sq, dep=ftoken),
                        wg=placed("w_gate", _pad_to(w_ffn_gate[l], 1, FP), dep=ftoken),
                        wu=placed("w_up", _pad_to(w_ffn_up[l], 1, FP), dep=ftoken),
                        wd=placed("w_down", _pad_to(w_ffn_down[l], 0, FP), dep=ftoken)))
    ffn_keys = ["wg", "wu", "wd"]
    order = [[(0, "w_sq")], [(0, k) for k in ffn_keys]]
    order += [[(l, k) for k in ["w_in", "w_sq"] + ffn_keys] for l in range(1, L)]
    gsems, flying, token = _gather_start("gather_start", [[wts[l][k] for l, k in grp] for grp in order])
    first = _gather_wait("gather_wait_first", fflying[0], fsems[0], token)
    wts[0]["w_in"], cw_g = _forward_sibling("gather_first", first, False)
    conv_w_full = cw_g.reshape(NSHARD, L, 32, CQ).transpose(1, 2, 0, 3).reshape(L, 32, D)[:, :KW]

    def land(gi, after):
        bufs = _gather_wait("gather_wait_%d" % gi, flying[gi], gsems[gi], after)
        bufs = _forward_sibling("gather_forward_%d" % gi, bufs, False)
        for (l, k), b in zip(order[gi], bufs):
            wts[l][k] = b

    x2d = x.reshape(T, D)
    tgt = loss_target.reshape(T, D)
    row = lambda a, l: a[l].reshape(1, -1)

    saved = []
    xc = x2d
    for l in range(L):
        ws_b = w_spatial[l].astype(BF16)
        bs_b = jnp.repeat(b_spatial[l].T, D // G, axis=1)
        cw_s = _strips(conv_w_full[l])
        h, proj = _mix_in_fwd(xc, row(norm_mix, l), wts[l]["w_in"], 0, tm, token if l == 0 else None)
        if l == 0:
            land(0, h)
        c1h, rstd_c, c3, ya = _conv_fwd(proj, cw_s, row(conv_b, l), row(conv_ln_g, l), row(conv_ln_b, l),
                                        wts[l]["w_sq"], 0, S, tm, rb)
        if l == 0:
            land(1, ya)
        mixed, gated, yb, merged, x1 = _sgu_merge_fwd(proj, ya, xc, row(sgu_ln_g, l), row(sgu_ln_b, l), ws_b, bs_b,
                                                      row(gate_bias, l), wts[l]["w_sq"], 0, tm2)
        h2, gt, up, act, x2 = _ffn_fwd(x1, row(norm_ffn, l), wts[l]["wg"], wts[l]["wu"], wts[l]["wd"], 0, tm2)
        if l + 1 < L:
            land(l + 2, x2)
        saved.append(dict(x=xc, h=h, proj=proj, c1h=c1h, rstd_c=rstd_c, c3=c3, ya=ya, mixed=mixed, gated=gated,
                          yb=yb, merged=merged, x1=x1, h2=h2, gt=gt, up=up, act=act, ws_b=ws_b, cw=conv_w_full[l]))
        xc = x2

    dx, loss_part, d_norm_final = _loss_head(xc, norm_final.reshape(1, D), tgt, tm)
    loss = lax.psum(loss_part[0, 0], ("x", "y", "c"))

    g_acc = {}

    def reduce_start(tag, layer, named):
        arrs = [g.reshape(NSHARD, 2, g.shape[1] // 2, g.shape[2]) for _, g in named]
        from_sib = _send_sibling_halves("sibling_" + tag, arrs)
        ps = [_add_halves("presum_" + nm, a, r, pos, _row_tile(a.shape[2], 256))
              for (nm, _), a, r in zip(named, arrs, from_sib)]
        sems, ps, lands, tok = _chip_send_start("chip_send_start_" + tag, ps)
        return dict(tag=tag, layer=layer, names=[nm for nm, _ in named], ps=ps, lands=lands, sems=sems), tok

    def reduce_finish(pend, after):
        ps, lands = _chip_send_wait("chip_send_wait_" + pend["tag"], pend["ps"], pend["lands"], pend["sems"], after)
        for nm, p, r in zip(pend["names"], ps, lands):
            g_acc[nm] = _add_shards("shardsum_" + nm, p, r, pos, _row_tile(p.shape[1], 256), pend["layer"], L,
                                    g_acc.get(nm))

    me_idx = (4 * mx + 2 * my + mc).astype(jnp.int32).reshape(1)
    exchanges = []

    def pack_rows(pieces):
        packed = jnp.concatenate(pieces, axis=0)
        return _pad_to(packed, 0, -(-packed.shape[0] // 8) * 8)

    def unpack_rows(summed, pieces):
        out, off = [], 0
        for p in pieces:
            out.append(summed[off:off + p.shape[0]])
            off += p.shape[0]
        return out

    def small_start(tag, pieces):
        sems, buf, land, token = _exchange_start("exchange_start_" + tag, pack_rows(pieces))
        return dict(tag=tag, pieces=pieces, buf=buf, land=land, sems=sems, token=token)

    def small_finish(st, after):
        buf, land = _exchange_wait("exchange_wait_" + st["tag"], st["buf"], st["land"], st["sems"], after)
        return unpack_rows(_sum_slots(buf, land, me_idx, _row_tile(buf.shape[0], 256)), st["pieces"])

    small = [None] * L
    tt = min(1024, T // 2)
    nt = T // tt
    pending, tok = None, None
    for l in reversed(range(L)):
        sv, wt = saved[l], wts[l]
        dx1, dgt, dup, d_norm_ffn = _ffn_bwd(dx, sv["x1"], sv["gt"], sv["up"], row(norm_ffn, l), wt["wg"], wt["wu"],
                                             wt["wd"], 0, tm2, tok)
        g_g, g_u = _tn_matmul("grad_w_gate_up", sv["h2"], (tt, D), lambda j, t: (t, 0), [dgt, dup], (tt, FP),
                              lambda j, t: (t, j), (NSHARD, D, FP), (None, D, FP), lambda j, t: (j, 0, 0), NSHARD, nt)
        g_d, = _tn_matmul("grad_w_down", sv["act"], (tt, FP), lambda j, t: (t, j), [dx], (tt, D), lambda j, t: (t, 0),
                          (NSHARD, FP, D), (None, FP, D), lambda j, t: (j, 0, 0), NSHARD, nt)
        if pending is not None:
            reduce_finish(pending, g_d)
        ffn_pend, tok = reduce_start("ffn%d" % l, l, [("w_ffn_gate", g_g), ("w_ffn_up", g_u), ("w_ffn_down", g_d)])
        wst_b = jnp.swapaxes(sv["ws_b"], 1, 2)
        dya, dyb, dp3, d_gate_bias, d_sgu_g, d_sgu_b, d_bs, d_ws = _merge_sgu_bwd(
            dx1, sv["proj"], sv["ya"], sv["yb"], sv["mixed"], row(sgu_ln_g, l), row(sgu_ln_b, l), sv["ws_b"], wst_b,
            row(gate_bias, l), wt["w_sq"], 0, tm2, tok)
        sq_args = ((tt, D), lambda j, t: (t, 0))
        sq_out = ((D, D), (D, D), lambda j, t: (0, 0), 1, nt)
        g_o, = _tn_matmul("grad_w_o", sv["merged"], *sq_args, [dx1], *sq_args, *sq_out)
        g_so, = _tn_matmul("grad_w_sgu_out", sv["gated"], *sq_args, [dyb], *sq_args, *sq_out)
        g_co, = _tn_matmul("grad_w_conv_out", sv["c3"], *sq_args, [dya], *sq_args, *sq_out)
        dc1, d_cln_g, d_cln_b, d_conv_b = _conv_ln_bwd(dya, sv["c1h"], sv["rstd_c"], row(conv_ln_g, l),
                                                       row(conv_ln_b, l), wt["w_sq"], 0, tm)
        small[l] = [None, d_gate_bias.reshape(2, D), None, d_conv_b, d_cln_g, d_cln_b, d_sgu_g, d_sgu_b,
                    d_ws.reshape(G * CH * CH // D, D), d_bs.reshape(G * CH // D, D), d_norm_ffn]
        tok_x = None
        if l == 0:
            early = [k for k in range(len(small[0])) if small[0][k] is not None]
            exchanges.append((small_start("early0", [small[0][k] for k in early]), [(0, k) for k in early]))
            tok_x = exchanges[-1][0]["token"]
        dp3, d_cw_s = _conv_bwd(dc1, sv["proj"], dp3, _strips(sv["cw"][::-1]), S, tm, rb, tok_x)
        g_in, = _tn_matmul("grad_w_in", sv["h"], (tt, D), lambda j, t: (t, 0), [dp3], (tt, CS), lambda j, t: (t, j),
                           (NSHARD, D, CS), (None, D, CS), lambda j, t: (j, 0, 0), NSHARD, nt)
        reduce_finish(ffn_pend, g_in)
        pending, tok = reduce_start("mix%d" % l, l, [
            ("w_in", g_in), ("w_conv_out", g_co.reshape(NSHARD, DQ, D)), ("w_sgu_out", g_so.reshape(NSHARD, DQ, D)),
            ("w_o", g_o.reshape(NSHARD, DQ, D))])
        dx, d_norm_mix = _mix_in_bwd(dx1, dp3, sv["x"], row(norm_mix, l), wt["w_in"], 0, tm, tok)
        small[l][0] = d_norm_mix
        small[l][2] = d_cw_s.transpose(1, 0, 2).reshape(32, D)
        if l > 0:
            exchanges.append((small_start("layer%d" % l, small[l]), [(l, k) for k in range(len(small[l]))]))
            tok = [tok, exchanges[-1][0]["token"]]
    reduce_finish(pending, dx)
    grad_x = dx.reshape(BL, S, D)

    names = ["w_in", "w_conv_out", "w_sgu_out", "w_o", "w_ffn_gate", "w_ffn_up", "w_ffn_down"]
    g_full = _join_halves([g_acc[nm] for nm in names])
    g_w_in, g_w_co, g_w_so, g_w_o, g_w_g, g_w_u, g_w_d = [g.reshape(L, 2 * g.shape[2], g.shape[3]) for g in g_full]
    g_w_g = g_w_g[:, :, :FS]
    g_w_u = g_w_u[:, :, :FS]
    g_w_d = g_w_d[:, :FS, :]

    late = [small[0][0], small[0][2], d_norm_final]
    packed = pack_rows(late)
    summed = _sum_slots(packed, _exchange_all(packed), me_idx, _row_tile(packed.shape[0], 256))
    sg = [[None] * len(small[l]) for l in range(L)]
    sg[0][0], sg[0][2], g_norm_final = unpack_rows(summed, late)
    g_norm_final = g_norm_final[0]
    for st, where in exchanges:
        for (l, k), piece in zip(where, small_finish(st, summed)):
            sg[l][k] = piece

    def per_layer(k, shape):
        return jnp.stack([sg[l][k] for l in range(L)]).reshape(shape)

    g_norm_mix = per_layer(0, (L, D))
    g_gate_bias = per_layer(1, (L, 2 * D))
    g_conv_w_full = jnp.stack([sg[l][2][:KW] for l in range(L)])
    g_conv_w = lax.dynamic_slice_in_dim(g_conv_w_full, (2 * mx + my) * CQ, CQ, axis=2).reshape(L, KW, 1, CQ)
    g_conv_b = per_layer(3, (L, D))
    g_conv_ln_g = per_layer(4, (L, D))
    g_conv_ln_b = per_layer(5, (L, D))
    g_sgu_ln_g = per_layer(6, (L, D))
    g_sgu_ln_b = per_layer(7, (L, D))
    g_w_spatial = per_layer(8, (L, G, CH, CH))
    g_b_spatial = per_layer(9, (L, G, CH))
    g_norm_ffn = per_layer(10, (L, D))

    grads = [g_norm_mix, g_w_in, g_gate_bias, g_conv_w, g_conv_b, g_conv_ln_g, g_conv_ln_b, g_w_co, g_sgu_ln_g,
             g_sgu_ln_b, g_w_spatial, g_b_spatial, g_w_so, g_w_o, g_norm_ffn, g_w_g, g_w_u, g_w_d, g_norm_final]
    weights = [norm_mix, w_in, gate_bias, conv_w, conv_b, conv_ln_g, conv_ln_b, w_conv_out, sgu_ln_g, sgu_ln_b,
               w_spatial, b_spatial, w_sgu_out, w_o, norm_ffn, w_ffn_gate, w_ffn_up, w_ffn_down, norm_final]
    ms = [m_norm_mix, m_w_in, m_gate_bias, m_conv_w, m_conv_b, m_conv_ln_g, m_conv_ln_b, m_w_conv_out, m_sgu_ln_g,
          m_sgu_ln_b, m_w_spatial, m_b_spatial, m_w_sgu_out, m_w_o, m_norm_ffn, m_w_ffn_gate, m_w_ffn_up,
          m_w_ffn_down, m_norm_final]
    vs = [v_norm_mix, v_w_in, v_gate_bias, v_conv_w, v_conv_b, v_conv_ln_g, v_conv_ln_b, v_w_conv_out, v_sgu_ln_g,
          v_sgu_ln_b, v_w_spatial, v_b_spatial, v_w_sgu_out, v_w_o, v_norm_ffn, v_w_ffn_gate, v_w_ffn_up,
          v_w_ffn_down, v_norm_final]

    big_idx = [1, 7, 12, 13, 15, 16, 17]
    deltas, new_m, new_v = [None] * 19, [None] * 19, [None] * 19
    for k in big_idx:
        shp = weights[k].shape
        r2 = (shp[0] * shp[1], shp[2])
        d_, m_, v_ = _adamw("adamw_" + str(k), weights[k].reshape(r2), grads[k].reshape(r2), ms[k].reshape(r2),
                            vs[k].reshape(r2), _row_tile(r2[0], 256))
        deltas[k], new_m[k], new_v[k] = d_.reshape(shp), m_.reshape(shp), v_.reshape(shp)
    small_idx = [k for k in range(19) if k not in big_idx]

    def pack(arrs):
        flat = jnp.concatenate([arrs[k].reshape(-1) for k in small_idx])
        rows = -(-flat.shape[0] // (256 * LANES)) * 256
        return _pad_to(flat, 0, rows * LANES).reshape(rows, LANES)

    pw, pg, pm, pv = pack(weights), pack(grads), pack(ms), pack(vs)
    d_, m_, v_ = _adamw("adamw_small", pw, pg, pm, pv, _row_tile(pw.shape[0], 512))
    off = 0
    for k in small_idx:
        n_el = weights[k].size
        shp = weights[k].shape
        deltas[k] = d_.reshape(-1)[off:off + n_el].reshape(shp)
        new_m[k] = m_.reshape(-1)[off:off + n_el].reshape(shp)
        new_v[k] = v_.reshape(-1)[off:off + n_el].reshape(shp)
        off += n_el

    return (loss, grad_x, *grads, *deltas, *new_m, *new_v)
```

```python
import functools

import jax
import jax.numpy as jnp
from jax import lax
from jax.experimental import pallas as pl
from jax.experimental.pallas import tpu as pltpu

F32 = jnp.float32
BF16 = jnp.bfloat16
EPS = 1e-6
ADAM_LR = 0.001
ADAM_B1 = 0.9
ADAM_B2 = 0.999
ADAM_EPS = 1e-08
ADAM_WD = 0.01
ADAM_STEP = 10

NSHARD = 4
LANES = 128
HALO = 16
VMEM_LIMIT = 60 * 1024 * 1024
MESH_ID = pl.DeviceIdType.MESH


def _dot(a, b):
    return jnp.dot(a, b, preferred_element_type=F32)


def _dot_nt(a, b):
    return lax.dot_general(a, b, (((1,), (1,)), ((), ())), preferred_element_type=F32)


def _dot_tn(a, b):
    return lax.dot_general(a, b, (((0,), (0,)), ((), ())), preferred_element_type=F32)


def _sig(z):
    return 1.0 / (1.0 + jnp.exp(-z))


def _res(shape, imap=None):
    nd = len(shape)
    if imap is None:
        imap = lambda *_: (0,) * nd
    return pl.BlockSpec(shape, imap, pipeline_mode=pl.Buffered(1))


def _cparams(sem):
    return pltpu.CompilerParams(dimension_semantics=sem, vmem_limit_bytes=VMEM_LIMIT)


def _sds(shape, dtype):
    return jax.ShapeDtypeStruct(shape, dtype)


def _after(body, n_in, dep):
    deps = [] if dep is None else [d for d in (dep if isinstance(dep, (list, tuple)) else [dep]) if d is not None]
    if not deps:
        return body, [], []

    def wrapped(*refs):
        return body(*refs[:n_in], *refs[n_in + len(deps):])

    return wrapped, [pl.BlockSpec(memory_space=pl.ANY)] * len(deps), deps


def _mix_in_fwd(x2d, g_mix, w_in_g, layer, tm, dep=None):
    T, D = x2d.shape
    CS = w_in_g.shape[2]
    CN = CS // 3

    def body(x_ref, g_ref, w_ref, h_ref, p_ref):
        x = x_ref[...]
        rstd = lax.rsqrt(jnp.mean(x * x, axis=-1, keepdims=True) + EPS)
        h = (x * rstd * g_ref[...]).astype(BF16)
        h_ref[...] = h
        for s in range(NSHARD):
            for j in range(3):
                c0 = s * CS + j * CN
                p_ref[:, c0:c0 + CN] = _dot(h, w_ref[s, :, j * CN:(j + 1) * CN]).astype(BF16)

    body, dep_spec, dep_arg = _after(body, 3, dep)
    return pl.pallas_call(
        body, name="mix_in_fwd", grid=(T // tm,),
        in_specs=[pl.BlockSpec((tm, D), lambda i: (i, 0)), _res((1, D)),
                  _res((NSHARD, D, CS), lambda i: (0, layer, 0))] + dep_spec,
        out_specs=[pl.BlockSpec((tm, D), lambda i: (i, 0)), pl.BlockSpec((tm, NSHARD * CS), lambda i: (i, 0))],
        out_shape=[_sds((T, D), BF16), _sds((T, NSHARD * CS), BF16)],
        compiler_params=_cparams(("parallel",)),
    )(x2d, g_mix, w_in_g, *dep_arg)


def _halo_maps(tm, n_rows):
    nb = tm // HALO
    last = n_rows // HALO - 1
    prev = lambda i: (jnp.maximum(i * nb - 1, 0), 0)
    nxt = lambda i: (jnp.minimum((i + 1) * nb, last), 0)
    return prev, nxt


def _dwconv(pad_ref, w_ref, out_ref, n_strips, tm, kw, rb):
    off = HALO - (kw - 1) // 2

    def strip(cs, carry):
        for r0 in range(0, tm, rb):
            acc = jnp.zeros((rb, LANES), F32)
            for k in range(kw):
                r = r0 + off + k
                acc = acc + w_ref[cs, k:k + 1, :] * pad_ref[cs, r:r + rb, :]
            out_ref[cs, r0:r0 + rb, :] = acc
        return carry

    lax.fori_loop(0, n_strips, strip, 0)


def _fill_c0_pad(pad_ref, pa_ref, pprev_ref, pnext_ref, D, tm, first, last):
    for cs in range(D // LANES):
        lo, hi = cs * LANES, (cs + 1) * LANES

        def c0_of(ref):
            return ref[:, lo:hi].astype(F32) * _sig(ref[:, D + lo:D + hi].astype(F32))

        pad_ref[cs, HALO:HALO + tm, :] = c0_of(pa_ref)
        pad_ref[cs, 0:HALO, :] = jnp.where(first, 0.0, c0_of(pprev_ref))
        pad_ref[cs, HALO + tm:HALO + tm + HALO, :] = jnp.where(last, 0.0, c0_of(pnext_ref))


def _conv_fwd(proj, conv_w_s, conv_b, ln_g, ln_b, w_sq_g, layer, seq, tm, rb):
    T = proj.shape[0]
    D = conv_b.shape[1]
    DQ = D // NSHARD
    NSTR = D // LANES
    KW = 31
    tps = seq // tm
    prev, nxt = _halo_maps(tm, T)

    def body(pa_ref, pprev_ref, pnext_ref, w_ref, b_ref, g_ref, be_ref, wco_ref,
             c1h_ref, rstd_ref, c3_ref, ya_ref, pad_ref, c1s_ref):
        i = pl.program_id(0)
        first = (i % tps) == 0
        last = (i % tps) == tps - 1
        _fill_c0_pad(pad_ref, pa_ref, pprev_ref, pnext_ref, D, tm, first, last)
        _dwconv(pad_ref, w_ref, c1s_ref, NSTR, tm, KW, rb)
        c1 = jnp.concatenate([c1s_ref[cs] for cs in range(NSTR)], axis=1) + b_ref[...]
        mu = jnp.mean(c1, axis=-1, keepdims=True)
        cc = c1 - mu
        rstd = lax.rsqrt(jnp.mean(cc * cc, axis=-1, keepdims=True) + EPS)
        c1h = cc * rstd
        c1h_ref[...] = c1h.astype(BF16)
        rstd_ref[...] = rstd
        c2 = c1h * g_ref[...] + be_ref[...]
        c3 = (c2 * _sig(c2)).astype(BF16)
        c3_ref[...] = c3
        ya_ref[...] = _dot(c3, wco_ref[...].reshape(D, D)).astype(BF16)

    row = lambda i: (i, 0)
    return pl.pallas_call(
        body, name="conv_fwd", grid=(T // tm,),
        in_specs=[pl.BlockSpec((tm, 2 * D), row), pl.BlockSpec((HALO, 2 * D), prev), pl.BlockSpec((HALO, 2 * D), nxt),
                  _res((NSTR, 32, LANES)), _res((1, D)), _res((1, D)), _res((1, D)),
                  _res((NSHARD, DQ, D), lambda i: (0, layer * 3 + 0, 0))],
        out_specs=[pl.BlockSpec((tm, D), row), pl.BlockSpec((tm, 1), row), pl.BlockSpec((tm, D), row),
                   pl.BlockSpec((tm, D), row)],
        out_shape=[_sds((T, D), BF16), _sds((T, 1), F32), _sds((T, D), BF16), _sds((T, D), BF16)],
        scratch_shapes=[pltpu.VMEM((NSTR, tm + 2 * HALO, LANES), F32), pltpu.VMEM((NSTR, tm, LANES), F32)],
        compiler_params=_cparams(("parallel",)),
    )(proj, proj, proj, conv_w_s, conv_b, ln_g, ln_b, w_sq_g)


def _sgu_merge_fwd(proj, ya, x2d, ln_g, ln_b, ws_b, bs_b, gate_bias, w_sq_g, layer, tm):
    T, D = x2d.shape
    DQ = D // NSHARD
    G, CH, _ = ws_b.shape
    GD = D // G

    def body(puv_ref, pg_ref, ya_ref, x_ref, g_ref, be_ref, ws_ref, bsb_ref, gb_ref, wso_ref, wo_ref,
             mixed_ref, gated_ref, yb_ref, merged_ref, x1_ref, mix_scr):
        u = puv_ref[:, :D].astype(F32)
        v = puv_ref[:, D:].astype(F32)
        mu = jnp.mean(v, axis=-1, keepdims=True)
        vc = v - mu
        rstd = lax.rsqrt(jnp.mean(vc * vc, axis=-1, keepdims=True) + EPS)
        vn = (vc * rstd * g_ref[...] + be_ref[...]).astype(BF16)
        nch = tm // CH
        for g in range(G):
            cols = slice(g * GD, (g + 1) * GD)
            rhs = jnp.concatenate([vn[ch * CH:(ch + 1) * CH, cols] for ch in range(nch)], axis=1)
            res = _dot(ws_ref[g], rhs)
            for ch in range(nch):
                mix_scr[ch * CH:(ch + 1) * CH, cols] = res[:, ch * GD:(ch + 1) * GD] + bsb_ref[:, cols]
        mixed = mix_scr[...]
        mixed_ref[...] = mixed.astype(BF16)
        gated = (u * mixed).astype(BF16)
        gated_ref[...] = gated
        yb = _dot(gated, wso_ref[...].reshape(D, D))
        yb_ref[...] = yb.astype(BF16)
        sa = _sig(pg_ref[:, :D].astype(F32) + gb_ref[:, :D])
        sb = _sig(pg_ref[:, D:].astype(F32) + gb_ref[:, D:])
        merged = (sa * ya_ref[...].astype(F32) + sb * yb).astype(BF16)
        merged_ref[...] = merged
        x1_ref[...] = x_ref[...] + _dot(merged, wo_ref[...].reshape(D, D))

    row = lambda i: (i, 0)
    return pl.pallas_call(
        body, name="sgu_merge_fwd", grid=(T // tm,),
        in_specs=[pl.BlockSpec((tm, 2 * D), lambda i: (i, 1)), pl.BlockSpec((tm, 2 * D), lambda i: (i, 2)),
                  pl.BlockSpec((tm, D), row), pl.BlockSpec((tm, D), row),
                  _res((1, D)), _res((1, D)), _res((G, CH, CH)), _res((CH, D)), _res((1, 2 * D)),
                  _res((NSHARD, DQ, D), lambda i: (0, layer * 3 + 1, 0)),
                  _res((NSHARD, DQ, D), lambda i: (0, layer * 3 + 2, 0))],
        out_specs=[pl.BlockSpec((tm, D), row)] * 5,
        out_shape=[_sds((T, D), BF16)] * 4 + [_sds((T, D), F32)],
        scratch_shapes=[pltpu.VMEM((tm, D), F32)],
        compiler_params=_cparams(("parallel",)),
    )(proj, proj, ya, x2d, ln_g, ln_b, ws_b, bs_b, gate_bias, w_sq_g, w_sq_g)


def _ffn_chunks(F):
    assert F % 256 == 0, F
    return [(c0, min(512, F - c0)) for c0 in range(0, F, 512)]


def _ffn_fwd(x1, g_ffn, wgt, wut, wd, tm):
    T, D = x1.shape
    F = wd.shape[0]

    def body(x_ref, g_ref, wg_ref, wu_ref, wd_ref, h2_ref, gt_ref, up_ref, act_ref, x2_ref):
        x = x_ref[...]
        rstd = lax.rsqrt(jnp.mean(x * x, axis=-1, keepdims=True) + EPS)
        h2 = (x * rstd * g_ref[...]).astype(BF16)
        h2_ref[...] = h2
        acc = x
        for c0, cw in _ffn_chunks(F):
            gt = _dot_nt(h2, wg_ref[c0:c0 + cw, :])
            up = _dot_nt(h2, wu_ref[c0:c0 + cw, :])
            gt_ref[:, c0:c0 + cw] = gt.astype(BF16)
            up_ref[:, c0:c0 + cw] = up.astype(BF16)
            act = (gt * _sig(gt) * up).astype(BF16)
            act_ref[:, c0:c0 + cw] = act
            acc = acc + _dot(act, wd_ref[c0:c0 + cw, :])
        x2_ref[...] = acc

    row = lambda i: (i, 0)
    return pl.pallas_call(
        body, name="ffn_fwd", grid=(T // tm,),
        in_specs=[pl.BlockSpec((tm, D), row), _res((1, D)), _res((F, D)), _res((F, D)), _res((F, D))],
        out_specs=[pl.BlockSpec((tm, D), row), pl.BlockSpec((tm, F), row), pl.BlockSpec((tm, F), row),
                   pl.BlockSpec((tm, F), row), pl.BlockSpec((tm, D), row)],
        out_shape=[_sds((T, D), BF16), _sds((T, F), BF16), _sds((T, F), BF16), _sds((T, F), BF16), _sds((T, D), F32)],
        compiler_params=_cparams(("parallel",)),
    )(x1, g_ffn, wgt, wut, wd)


def _loss_head(xf, g_fin, target, tm):
    T, D = xf.shape
    n = T // tm

    def body(x_ref, g_ref, t_ref, dx_ref, loss_ref, dg_ref, acc_ref):
        i = pl.program_id(0)

        @pl.when(i == 0)
        def _():
            acc_ref[...] = jnp.zeros_like(acc_ref)
            dg_ref[...] = jnp.zeros_like(dg_ref)

        x = x_ref[...]
        g = g_ref[...]
        rstd = lax.rsqrt(jnp.mean(x * x, axis=-1, keepdims=True) + EPS)
        xh = x * rstd
        diff = xh * g - t_ref[...]
        acc_ref[...] += jnp.sum(diff * diff, axis=0, keepdims=True)
        dy = diff * (1.0 / D)
        dg_ref[...] += jnp.sum(dy * xh, axis=0, keepdims=True)
        dxh = dy * g
        dx_ref[...] = rstd * (dxh - xh * jnp.mean(dxh * xh, axis=-1, keepdims=True))

        @pl.when(i == n - 1)
        def _():
            tot = jnp.sum(acc_ref[...], axis=-1, keepdims=True) * (0.5 / D)
            loss_ref[...] = jnp.broadcast_to(tot, loss_ref.shape)

    row = lambda i: (i, 0)
    return pl.pallas_call(
        body, name="loss_head", grid=(n,),
        in_specs=[pl.BlockSpec((tm, D), row), _res((1, D)), pl.BlockSpec((tm, D), row)],
        out_specs=[pl.BlockSpec((tm, D), row), pl.BlockSpec((1, LANES), lambda i: (0, 0)),
                   pl.BlockSpec((1, D), lambda i: (0, 0))],
        out_shape=[_sds((T, D), F32), _sds((1, LANES), F32), _sds((1, D), F32)],
        scratch_shapes=[pltpu.VMEM((1, D), F32)],
        compiler_params=_cparams(("arbitrary",)),
    )(xf, g_fin, target)


def _ffn_bwd(dx2, x1, gt, up, g_ffn, wgt, wut, wd, tm, dep=None):
    T, D = x1.shape
    F = wd.shape[0]

    def body(dx2_ref, x1_ref, gt_ref, up_ref, g_ref, wg_ref, wu_ref, wd_ref, dx1_ref, dgt_ref, dup_ref, dg_ref):
        i = pl.program_id(0)

        @pl.when(i == 0)
        def _():
            dg_ref[...] = jnp.zeros_like(dg_ref)

        dx2 = dx2_ref[...]
        dx2b = dx2.astype(BF16)
        dh2 = jnp.zeros((tm, D), F32)
        for c0, cw in _ffn_chunks(F):
            dact = _dot_nt(dx2b, wd_ref[c0:c0 + cw, :])
            g = gt_ref[:, c0:c0 + cw].astype(F32)
            u = up_ref[:, c0:c0 + cw].astype(F32)
            sg = _sig(g)
            dup = (dact * (g * sg)).astype(BF16)
            dgt = (dact * u * (sg * (1.0 + g * (1.0 - sg)))).astype(BF16)
            dgt_ref[:, c0:c0 + cw] = dgt
            dup_ref[:, c0:c0 + cw] = dup
            dh2 = dh2 + _dot(dgt, wg_ref[c0:c0 + cw, :]) + _dot(dup, wu_ref[c0:c0 + cw, :])
        x = x1_ref[...]
        rstd = lax.rsqrt(jnp.mean(x * x, axis=-1, keepdims=True) + EPS)
        xh = x * rstd
        dg_ref[...] += jnp.sum(dh2 * xh, axis=0, keepdims=True)
        dxh = dh2 * g_ref[...]
        dx1_ref[...] = dx2 + rstd * (dxh - xh * jnp.mean(dxh * xh, axis=-1, keepdims=True))

    row = lambda i: (i, 0)
    body, dep_spec, dep_arg = _after(body, 8, dep)
    return pl.pallas_call(
        body, name="ffn_bwd", grid=(T // tm,),
        in_specs=[pl.BlockSpec((tm, D), row), pl.BlockSpec((tm, D), row), pl.BlockSpec((tm, F), row),
                  pl.BlockSpec((tm, F), row), _res((1, D)), _res((F, D)), _res((F, D)), _res((F, D))] + dep_spec,
        out_specs=[pl.BlockSpec((tm, D), row), pl.BlockSpec((tm, F), row), pl.BlockSpec((tm, F), row),
                   pl.BlockSpec((1, D), lambda i: (0, 0))],
        out_shape=[_sds((T, D), F32), _sds((T, F), BF16), _sds((T, F), BF16), _sds((1, D), F32)],
        compiler_params=_cparams(("arbitrary",)),
    )(dx2, x1, gt, up, g_ffn, wgt, wut, wd, *dep_arg)


def _merge_sgu_bwd(dx1, proj, ya, yb, mixed, ln_g, ln_b, ws_b, wst_b, gate_bias, w_sq_g, layer, tm, dep=None):
    T, D = dx1.shape
    DQ = D // NSHARD
    G, CH, _ = ws_b.shape
    GD = D // G

    def body(dx1_ref, puv_ref, pg_ref, ya_ref, yb_ref, mixed_ref, g_ref, be_ref, ws_ref, wst_ref, gb_ref,
             wso_ref, wo_ref, dya_ref, dyb_ref, dp_ref, dgb_ref, dlg_ref, dlb_ref, dbs_ref, dws_ref,
             dvn_scr, dbs_scr):
        i = pl.program_id(0)

        @pl.when(i == 0)
        def _():
            for r in (dgb_ref, dlg_ref, dlb_ref, dws_ref, dbs_scr):
                r[...] = jnp.zeros_like(r)

        dmerged = _dot_nt(dx1_ref[...].astype(BF16), wo_ref[...].reshape(D, D))
        sa = _sig(pg_ref[:, :D].astype(F32) + gb_ref[:, :D])
        sb = _sig(pg_ref[:, D:].astype(F32) + gb_ref[:, D:])
        dya = (dmerged * sa).astype(BF16)
        dyb = (dmerged * sb).astype(BF16)
        dya_ref[...] = dya
        dyb_ref[...] = dyb
        dga = dmerged * ya_ref[...].astype(F32) * (sa * (1.0 - sa))
        dgb = dmerged * yb_ref[...].astype(F32) * (sb * (1.0 - sb))
        dp_ref[:, 4 * D:5 * D] = dga.astype(BF16)
        dp_ref[:, 5 * D:6 * D] = dgb.astype(BF16)
        dgb_ref[:, :D] += jnp.sum(dga, axis=0, keepdims=True)
        dgb_ref[:, D:] += jnp.sum(dgb, axis=0, keepdims=True)

        dgated = _dot_nt(dyb, wso_ref[...].reshape(D, D))
        u = puv_ref[:, :D].astype(F32)
        v = puv_ref[:, D:].astype(F32)
        dp_ref[:, 2 * D:3 * D] = (dgated * mixed_ref[...].astype(F32)).astype(BF16)
        dmixed = dgated * u
        mu = jnp.mean(v, axis=-1, keepdims=True)
        vc = v - mu
        rstd = lax.rsqrt(jnp.mean(vc * vc, axis=-1, keepdims=True) + EPS)
        vh = vc * rstd
        vn = (vh * g_ref[...] + be_ref[...]).astype(BF16)
        dmb = dmixed.astype(BF16)
        nch = tm // CH
        bs_part = dmixed[0:CH, :]
        for ch in range(1, nch):
            bs_part = bs_part + dmixed[ch * CH:(ch + 1) * CH, :]
        dbs_scr[...] += bs_part
        for g in range(G):
            cols = slice(g * GD, (g + 1) * GD)
            dm_g = jnp.concatenate([dmb[ch * CH:(ch + 1) * CH, cols] for ch in range(nch)], axis=1)
            vn_g = jnp.concatenate([vn[ch * CH:(ch + 1) * CH, cols] for ch in range(nch)], axis=1)
            dws_ref[g] += _dot_nt(dm_g, vn_g)
            dvn_g = _dot(wst_ref[g], dm_g)
            for ch in range(nch):
                dvn_scr[ch * CH:(ch + 1) * CH, cols] = dvn_g[:, ch * GD:(ch + 1) * GD]
        dvn = dvn_scr[...]
        dlg_ref[...] += jnp.sum(dvn * vh, axis=0, keepdims=True)
        dlb_ref[...] += jnp.sum(dvn, axis=0, keepdims=True)
        dxh = dvn * g_ref[...]
        dv = rstd * (dxh - jnp.mean(dxh, axis=-1, keepdims=True) - vh * jnp.mean(dxh * vh, axis=-1, keepdims=True))
        dp_ref[:, 3 * D:4 * D] = dv.astype(BF16)

        @pl.when(i == pl.num_programs(0) - 1)
        def _():
            for g in range(G):
                blk = dbs_scr[:, g * GD:(g + 1) * GD]
                if GD != CH:
                    blk = jnp.concatenate([blk, jnp.zeros((CH, CH - GD), F32)], axis=1)
                dbs_ref[:, g * CH:(g + 1) * CH] = jnp.sum(blk.T, axis=0, keepdims=True)

    row = lambda i: (i, 0)
    fixed2 = lambda i: (0, 0)
    body, dep_spec, dep_arg = _after(body, 13, dep)
    return pl.pallas_call(
        body, name="merge_sgu_bwd", grid=(T // tm,),
        in_specs=[pl.BlockSpec((tm, D), row), pl.BlockSpec((tm, 2 * D), lambda i: (i, 1)),
                  pl.BlockSpec((tm, 2 * D), lambda i: (i, 2)), pl.BlockSpec((tm, D), row), pl.BlockSpec((tm, D), row),
                  pl.BlockSpec((tm, D), row), _res((1, D)), _res((1, D)), _res((G, CH, CH)), _res((G, CH, CH)),
                  _res((1, 2 * D)),
                  _res((NSHARD, DQ, D), lambda i: (0, layer * 3 + 1, 0)),
                  _res((NSHARD, DQ, D), lambda i: (0, layer * 3 + 2, 0))] + dep_spec,
        out_specs=[pl.BlockSpec((tm, D), row), pl.BlockSpec((tm, D), row),
                   pl.BlockSpec((tm, 6 * D), row),
                   pl.BlockSpec((1, 2 * D), fixed2), pl.BlockSpec((1, D), fixed2), pl.BlockSpec((1, D), fixed2),
                   pl.BlockSpec((1, G * CH), fixed2), pl.BlockSpec((G, CH, CH), lambda i: (0, 0, 0))],
        out_shape=[_sds((T, D), BF16), _sds((T, D), BF16), _sds((T, 6 * D), BF16),
                   _sds((1, 2 * D), F32), _sds((1, D), F32), _sds((1, D), F32), _sds((1, G * CH), F32),
                   _sds((G, CH, CH), F32)],
        scratch_shapes=[pltpu.VMEM((tm, D), F32), pltpu.VMEM((CH, D), F32)],
        compiler_params=_cparams(("arbitrary",)),
    )(dx1, proj, proj, ya, yb, mixed, ln_g, ln_b, ws_b, wst_b, gate_bias, w_sq_g, w_sq_g, *dep_arg)


def _conv_ln_bwd(dya, c1h, rstd_c, ln_g, ln_b, w_sq_g, layer, tm):
    T, D = dya.shape
    DQ = D // NSHARD

    def body(dya_ref, c1h_ref, rstd_ref, g_ref, be_ref, wco_ref, dc1_ref, dlg_ref, dlb_ref, dcb_ref):
        i = pl.program_id(0)

        @pl.when(i == 0)
        def _():
            for r in (dlg_ref, dlb_ref, dcb_ref):
                r[...] = jnp.zeros_like(r)

        dc3 = _dot_nt(dya_ref[...], wco_ref[...].reshape(D, D))
        c1h = c1h_ref[...].astype(F32)
        c2 = c1h * g_ref[...] + be_ref[...]
        sg = _sig(c2)
        dc2 = dc3 * (sg * (1.0 + c2 * (1.0 - sg)))
        dlg_ref[...] += jnp.sum(dc2 * c1h, axis=0, keepdims=True)
        dlb_ref[...] += jnp.sum(dc2, axis=0, keepdims=True)
        dxh = dc2 * g_ref[...]
        dc1 = rstd_ref[...] * (dxh - jnp.mean(dxh, axis=-1, keepdims=True)
                               - c1h * jnp.mean(dxh * c1h, axis=-1, keepdims=True))
        dc1_ref[...] = dc1
        dcb_ref[...] += jnp.sum(dc1, axis=0, keepdims=True)

    row = lambda i: (i, 0)
    fixed2 = lambda i: (0, 0)
    return pl.pallas_call(
        body, name="conv_ln_bwd", grid=(T // tm,),
        in_specs=[pl.BlockSpec((tm, D), row), pl.BlockSpec((tm, D), row), pl.BlockSpec((tm, 1), row),
                  _res((1, D)), _res((1, D)), _res((NSHARD, DQ, D), lambda i: (0, layer * 3 + 0, 0))],
        out_specs=[pl.BlockSpec((tm, D), row), pl.BlockSpec((1, D), fixed2), pl.BlockSpec((1, D), fixed2),
                   pl.BlockSpec((1, D), fixed2)],
        out_shape=[_sds((T, D), F32), _sds((1, D), F32), _sds((1, D), F32), _sds((1, D), F32)],
        compiler_params=_cparams(("arbitrary",)),
    )(dya, c1h, rstd_c, ln_g, ln_b, w_sq_g)


def _conv_bwd(dc1, proj, dp3, conv_wf_s, seq, tm, rb, dep=None):
    T, D = dc1.shape
    NSTR = D // LANES
    KW = 31
    PADK = (KW - 1) // 2
    tps = seq // tm
    prev, nxt = _halo_maps(tm, T)
    n = T // tm

    def body(dc_ref, dcprev_ref, dcnext_ref, pa_ref, pprev_ref, pnext_ref, wf_ref, dp_in_ref,
             dp_ref, dw_ref, pad_ref, dpad_ref, dc0_ref, dwacc_ref):
        del dp_in_ref
        i = pl.program_id(0)
        first = (i % tps) == 0
        last = (i % tps) == tps - 1

        @pl.when(i == 0)
        def _():
            dwacc_ref[...] = jnp.zeros_like(dwacc_ref)

        _fill_c0_pad(pad_ref, pa_ref, pprev_ref, pnext_ref, D, tm, first, last)
        for cs in range(NSTR):
            lo, hi = cs * LANES, (cs + 1) * LANES
            dpad_ref[cs, HALO:HALO + tm, :] = dc_ref[:, lo:hi]
            dpad_ref[cs, 0:HALO, :] = jnp.where(first, 0.0, dcprev_ref[:, lo:hi])
            dpad_ref[cs, HALO + tm:HALO + tm + HALO, :] = jnp.where(last, 0.0, dcnext_ref[:, lo:hi])
        _dwconv(dpad_ref, wf_ref, dc0_ref, NSTR, tm, KW, rb)

        def strip(cs, carry):
            for r0 in range(0, tm, rb):
                d = dpad_ref[cs, HALO + r0:HALO + r0 + rb, :]
                for k in range(KW):
                    r = r0 + HALO - PADK + k
                    prod = d * pad_ref[cs, r:r + rb, :]
                    dwacc_ref[cs, k * 8:(k + 1) * 8, :] += jnp.sum(prod.reshape(rb // 8, 8, LANES), axis=0)
            return carry

        lax.fori_loop(0, NSTR, strip, 0)

        for cs in range(NSTR):
            lo, hi = cs * LANES, (cs + 1) * LANES
            av = pa_ref[:, lo:hi].astype(F32)
            sg = _sig(pa_ref[:, D + lo:D + hi].astype(F32))
            dc0 = dc0_ref[cs]
            dp_ref[:, lo:hi] = (dc0 * sg).astype(BF16)
            dp_ref[:, D + lo:D + hi] = (dc0 * av * (sg * (1.0 - sg))).astype(BF16)

        @pl.when(i == n - 1)
        def _():
            for cs in range(NSTR):
                dw_ref[cs] = jnp.sum(dwacc_ref[cs].reshape(32, 8, LANES), axis=1)

    row = lambda i: (i, 0)
    body, dep_spec, dep_arg = _after(body, 8, dep)
    return pl.pallas_call(
        body, name="conv_bwd", grid=(n,),
        in_specs=[pl.BlockSpec((tm, D), row), pl.BlockSpec((HALO, D), prev), pl.BlockSpec((HALO, D), nxt),
                  pl.BlockSpec((tm, 2 * D), row), pl.BlockSpec((HALO, 2 * D), prev), pl.BlockSpec((HALO, 2 * D), nxt),
                  _res((NSTR, 32, LANES)), pl.BlockSpec(memory_space=pl.ANY)] + dep_spec,
        out_specs=[pl.BlockSpec((tm, 2 * D), row),
                   pl.BlockSpec((NSTR, 32, LANES), lambda i: (0, 0, 0))],
        out_shape=[_sds(dp3.shape, BF16), _sds((NSTR, 32, LANES), F32)],
        scratch_shapes=[pltpu.VMEM((NSTR, tm + 2 * HALO, LANES), F32), pltpu.VMEM((NSTR, tm + 2 * HALO, LANES), F32),
                        pltpu.VMEM((NSTR, tm, LANES), F32), pltpu.VMEM((NSTR, 32 * 8, LANES), F32)],
        input_output_aliases={7: 0},
        compiler_params=_cparams(("arbitrary",)),
    )(dc1, dc1, dc1, proj, proj, proj, conv_wf_s, dp3, *dep_arg)


def _mix_in_bwd(dx1, dp3, x2d, g_mix, w_in_g, layer, tm, dep=None):
    T, D = x2d.shape
    CS = w_in_g.shape[2]
    CN = CS // 3

    def body(dx1_ref, dp_ref, x_ref, g_ref, w_ref, dx_ref, dg_ref):
        i = pl.program_id(0)

        @pl.when(i == 0)
        def _():
            dg_ref[...] = jnp.zeros_like(dg_ref)

        dh = jnp.zeros((tm, D), F32)
        for j in range(12):
            dh = dh + _dot_nt(dp_ref[:, j * CN:(j + 1) * CN], w_ref[j // 3, :, (j % 3) * CN:(j % 3 + 1) * CN])
        x = x_ref[...]
        rstd = lax.rsqrt(jnp.mean(x * x, axis=-1, keepdims=True) + EPS)
        xh = x * rstd
        dg_ref[...] += jnp.sum(dh * xh, axis=0, keepdims=True)
        dxh = dh * g_ref[...]
        dx_ref[...] = dx1_ref[...] + rstd * (dxh - xh * jnp.mean(dxh * xh, axis=-1, keepdims=True))

    row = lambda i: (i, 0)
    body, dep_spec, dep_arg = _after(body, 5, dep)
    return pl.pallas_call(
        body, name="mix_in_bwd", grid=(T // tm,),
        in_specs=[pl.BlockSpec((tm, D), row), pl.BlockSpec((tm, 6 * D), row),
                  pl.BlockSpec((tm, D), row), _res((1, D)), _res((NSHARD, D, CS), lambda i: (0, layer, 0))] + dep_spec,
        out_specs=[pl.BlockSpec((tm, D), row), pl.BlockSpec((1, D), lambda i: (0, 0))],
        out_shape=[_sds((T, D), F32), _sds((1, D), F32)],
        compiler_params=_cparams(("arbitrary",)),
    )(dx1, dp3, x2d, g_mix, w_in_g, *dep_arg)


def _tn_matmul(name, a, a_block, a_map, bs, b_block, b_map, out_shape, out_block, out_map, nj, nt):
    kk = [d for d in a_block if d is not None][-1]
    nn = [d for d in b_block if d is not None][-1]
    nb = len(bs)

    def body(*refs):
        a_ref, b_refs = refs[0], refs[1:1 + nb]
        o_refs, acc_refs = refs[-2 * nb:-nb], refs[-nb:]
        t = pl.program_id(1)

        @pl.when(t == 0)
        def _():
            for acc_ref in acc_refs:
                acc_ref[...] = jnp.zeros_like(acc_ref)

        a_t = a_ref[...].astype(BF16)
        for b_ref, acc_ref in zip(b_refs, acc_refs):
            acc_ref[...] += _dot_tn(a_t, b_ref[...].astype(BF16))

        @pl.when(t == nt - 1)
        def _():
            for o_ref, acc_ref in zip(o_refs, acc_refs):
                o_ref[...] = acc_ref[...].astype(o_ref.dtype)

    return pl.pallas_call(
        body, name=name, grid=(nj, nt),
        in_specs=[pl.BlockSpec(a_block, a_map)] + [pl.BlockSpec(b_block, b_map)] * nb,
        out_specs=[pl.BlockSpec(out_block, out_map)] * nb, out_shape=[_sds(out_shape, BF16)] * nb,
        scratch_shapes=[pltpu.VMEM((kk, nn), F32)] * nb,
        compiler_params=_cparams(("parallel", "arbitrary")),
    )(a, *bs)


def _place_shard(name, w, pos, dtype, tr, dep=None):
    R, C = w.shape

    def body(pos_ref, w_ref, o_ref):
        del pos_ref
        o_ref[...] = w_ref[...].astype(dtype)

    body, dep_spec, dep_arg = _after(body, 2, dep)
    grid_spec = pltpu.PrefetchScalarGridSpec(
        num_scalar_prefetch=1, grid=(R // tr,),
        in_specs=[pl.BlockSpec((tr, C), lambda r, pos: (r, 0))] + dep_spec,
        out_specs=pl.BlockSpec((None, tr, C), lambda r, pos: (pos[1], r, 0)))
    return pl.pallas_call(body, name=name, grid_spec=grid_spec, out_shape=_sds((NSHARD, R, C), dtype),
                          compiler_params=_cparams(("parallel",)))(pos, w, *dep_arg)


def _add_halves(name, g, rbuf, pos, tr):
    NS, _, H, C = g.shape

    def body(pos_ref, g_ref, r_ref, o_ref):
        del pos_ref
        o_ref[...] = (g_ref[...].astype(F32) + r_ref[...].astype(F32)).astype(BF16)

    grid_spec = pltpu.PrefetchScalarGridSpec(
        num_scalar_prefetch=1, grid=(NS, H // tr),
        in_specs=[pl.BlockSpec((None, None, tr, C), lambda s, r, pos: (s, pos[0], r, 0)),
                  pl.BlockSpec((None, tr, C), lambda s, r, pos: (s, r, 0))],
        out_specs=pl.BlockSpec((None, tr, C), lambda s, r, pos: (s, r, 0)))
    return pl.pallas_call(body, name=name, grid_spec=grid_spec, out_shape=_sds((NS, H, C), BF16),
                          compiler_params=_cparams(("parallel", "parallel")))(pos, g, rbuf)


def _add_shards(name, p, rbuf, pos, tr, layer, n_layers, prev):
    _, H, C = p.shape

    def body(pos_ref, p_ref, r_ref, *rest):
        del pos_ref
        o_ref = rest[-1]
        acc = p_ref[...].astype(F32)
        for j in range(3):
            acc = acc + r_ref[j].astype(F32)
        o_ref[...] = acc

    in_specs = [pl.BlockSpec((None, tr, C), lambda r, pos: (pos[1], r, 0)),
                pl.BlockSpec((3, tr, C), lambda r, pos: (0, r, 0))]
    args = [pos, p, rbuf]
    aliases = {}
    if prev is not None:
        in_specs.append(pl.BlockSpec(memory_space=pl.ANY))
        args.append(prev)
        aliases = {3: 0}
    grid_spec = pltpu.PrefetchScalarGridSpec(
        num_scalar_prefetch=1, grid=(H // tr,), in_specs=in_specs,
        out_specs=pl.BlockSpec((None, None, tr, C), lambda r, pos: (layer, pos[0], r, 0)))
    return pl.pallas_call(body, name=name, grid_spec=grid_spec, out_shape=_sds((n_layers, 2, H, C), F32),
                          input_output_aliases=aliases, compiler_params=_cparams(("parallel",)))(*args)


def _sum_slots(own, land, me, tr):
    NS8, R, C = land.shape

    def body(me_ref, own_ref, l_ref, o_ref):
        acc = None
        for j in range(NS8):
            term = jnp.where(me_ref[0] == j, own_ref[...], l_ref[j])
            acc = term if acc is None else acc + term
        o_ref[...] = acc

    grid_spec = pltpu.PrefetchScalarGridSpec(
        num_scalar_prefetch=1, grid=(R // tr,),
        in_specs=[pl.BlockSpec((tr, C), lambda i, me: (i, 0)), pl.BlockSpec((NS8, tr, C), lambda i, me: (0, i, 0))],
        out_specs=pl.BlockSpec((tr, C), lambda i, me: (i, 0)))
    return pl.pallas_call(body, name="sum_slots", grid_spec=grid_spec, out_shape=_sds((R, C), F32),
                          compiler_params=_cparams(("parallel",)))(me, own, land)


def _adamw(name, w, g, m, v, tr):
    R, C = w.shape
    c1 = 1.0 - ADAM_B1 ** ADAM_STEP
    c2 = 1.0 - ADAM_B2 ** ADAM_STEP

    def body(w_ref, g_ref, m_ref, v_ref, d_ref, mo_ref, vo_ref):
        g_ = g_ref[...]
        m_ = ADAM_B1 * m_ref[...] + (1.0 - ADAM_B1) * g_
        v_ = ADAM_B2 * v_ref[...] + (1.0 - ADAM_B2) * (g_ * g_)
        mo_ref[...] = m_
        vo_ref[...] = v_
        d_ref[...] = -ADAM_LR * ((m_ / c1) / (jnp.sqrt(v_ / c2) + ADAM_EPS) + ADAM_WD * w_ref[...])

    spec = pl.BlockSpec((tr, C), lambda i: (i, 0))
    return pl.pallas_call(
        body, name=name, grid=(R // tr,), in_specs=[spec] * 4, out_specs=[spec] * 3,
        out_shape=[_sds((R, C), F32)] * 3, compiler_params=_cparams(("parallel",)))(w, g, m, v)


def _row_tile(rows, cap):
    best = rows
    for t in range(8, min(rows, cap) + 1, 8):
        if rows % t == 0:
            best = t
    return best


HBM_SPEC = pl.BlockSpec(memory_space=pltpu.HBM)
SEM_SPEC = pl.BlockSpec(memory_space=pltpu.SEMAPHORE)
DATAFLOW = pltpu.SideEffectType.DATAFLOW_SIDE_EFFECTING
DMA_SEM = pltpu.SemaphoreType.DMA


def _hbm(a):
    return pltpu.with_memory_space_constraint(a, pltpu.HBM)


def _mesh_pos():
    return lax.axis_index("x"), lax.axis_index("y"), lax.axis_index("c")


def _other_chips(x, y):
    return [(1 - x, y), (x, 1 - y), (1 - x, 1 - y)]


def _half_rows(buf, shard, core):
    h = buf.shape[1] // 2
    return buf.at[shard, pl.ds(core * h, h), :]


def _ici_copy(buf, j, send, recv, landing):
    x, y, c = _mesh_pos()
    px, py = _other_chips(x, y)[j]
    part = _half_rows(buf, 2 * px + py if landing else 2 * x + y, c)
    return pltpu.make_async_remote_copy(src_ref=part, dst_ref=part, send_sem=send, recv_sem=recv,
                                        device_id=(px, py, c), device_id_type=MESH_ID)


def _sibling_copy(buf, j, send, recv, landing):
    x, y, c = _mesh_pos()
    px, py = _other_chips(x, y)[j]
    part = _half_rows(buf, 2 * px + py, 1 - c if landing else c)
    return pltpu.make_async_remote_copy(src_ref=part, dst_ref=part, send_sem=send, recv_sem=recv,
                                        device_id=(x, y, 1 - c), device_id_type=MESH_ID)


def _forward_sibling(name, bufs, with_ici):
    n = len(bufs)

    def body(*refs):
        ins = refs[:n]
        send_ici, recv_ici, send_d2d, recv_d2d = refs[2 * n:]
        sends = []
        if with_ici:
            for i in range(n):
                for j in range(3):
                    cp = _ici_copy(ins[i], j, send_ici.at[i, j], recv_ici.at[i, j], False)
                    cp.start()
                    sends.append(cp)
        for i in range(n):
            for j in range(3):
                if with_ici:
                    _ici_copy(ins[i], j, send_ici.at[i, j], recv_ici.at[i, j], True).wait_recv()
                cp = _sibling_copy(ins[i], j, send_d2d.at[i, j], recv_d2d.at[i, j], False)
                cp.start()
                sends.append(cp)
        for i in range(n):
            for j in range(3):
                _sibling_copy(ins[i], j, send_d2d.at[i, j], recv_d2d.at[i, j], True).wait_recv()
        for cp in sends:
            cp.wait_send()

    return pl.pallas_call(
        body, name=name, in_specs=[HBM_SPEC] * n, out_specs=[HBM_SPEC] * n,
        out_shape=[_sds(b.shape, b.dtype) for b in bufs],
        scratch_shapes=[DMA_SEM((n, 3))] * 4, input_output_aliases={i: i for i in range(n)},
    )(*bufs)


def _gather_start(name, groups):
    flat = [b for g in groups for b in g]
    n, ng = len(flat), len(groups)

    def body(*refs):
        ins, sems, token = refs[:n], refs[n:n + 2 * ng], refs[-1]
        k = 0
        for gi, g in enumerate(groups):
            for a in range(len(g)):
                for j in range(3):
                    _ici_copy(ins[k], j, sems[2 * gi], sems[2 * gi + 1], False).start()
                k += 1
        token[...] = jnp.zeros_like(token)

    res = pl.pallas_call(
        body, name=name, in_specs=[HBM_SPEC] * n,
        out_specs=[SEM_SPEC] * (2 * ng) + [HBM_SPEC] * n + [pl.BlockSpec(memory_space=pltpu.VMEM)],
        out_shape=[DMA_SEM(()) for g in groups for _ in range(2)]
        + [pltpu.HBM(b.shape, b.dtype) for b in flat] + [_sds((8, LANES), F32)],
        input_output_aliases={i: 2 * ng + i for i in range(n)},
        compiler_params=pltpu.CompilerParams(has_side_effects=DATAFLOW),
    )(*[_hbm(b) for b in flat])
    sems = [(res[2 * gi], res[2 * gi + 1]) for gi in range(ng)]
    thru, k = [], 2 * ng
    for g in groups:
        thru.append(list(res[k:k + len(g)]))
        k += len(g)
    return sems, thru, res[-1]


def _gather_wait(name, bufs, sems, after):
    n = len(bufs)

    def body(*refs):
        ins, send, recv = refs[:n], refs[n], refs[n + 1]
        for a in range(n):
            for j in range(3):
                _ici_copy(ins[a], j, send, recv, False).wait_send()
                _ici_copy(ins[a], j, send, recv, True).wait_recv()

    return pl.pallas_call(
        body, name=name, in_specs=[HBM_SPEC] * n + [SEM_SPEC, SEM_SPEC, pl.BlockSpec(memory_space=pl.ANY)],
        out_specs=[HBM_SPEC] * n, out_shape=[pltpu.HBM(b.shape, b.dtype) for b in bufs],
        input_output_aliases={i: i for i in range(n)},
        compiler_params=pltpu.CompilerParams(has_side_effects=DATAFLOW),
    )(*bufs, sems[0], sems[1], after)


def _send_sibling_halves(name, arrs):
    n = len(arrs)

    def body(*refs):
        ins, outs = refs[:n], refs[n:2 * n]
        send, recv = refs[2 * n:]
        x, y, c = _mesh_pos()
        cps = []
        for i in range(n):
            cp = pltpu.make_async_remote_copy(
                src_ref=ins[i].at[:, 1 - c], dst_ref=outs[i],
                send_sem=send.at[i], recv_sem=recv.at[i], device_id=(x, y, 1 - c), device_id_type=MESH_ID)
            cp.start()
            cps.append(cp)
        for cp in cps:
            cp.wait()

    return pl.pallas_call(
        body, name=name, in_specs=[HBM_SPEC] * n, out_specs=[HBM_SPEC] * n,
        out_shape=[_sds((a.shape[0],) + a.shape[2:], a.dtype) for a in arrs],
        scratch_shapes=[DMA_SEM((n,)), DMA_SEM((n,))],
    )(*arrs)


def _chip_copy(p, land, j, send, recv):
    x, y, c = _mesh_pos()
    px, py = _other_chips(x, y)[j]
    return pltpu.make_async_remote_copy(src_ref=p.at[2 * px + py], dst_ref=land.at[j], send_sem=send, recv_sem=recv,
                                        device_id=(px, py, c), device_id_type=MESH_ID)


def _chip_send_start(name, ps):
    n = len(ps)
    lands = [lax.empty((3,) + p.shape[1:], p.dtype) for p in ps]

    def body(*refs):
        ins, lnd, send, recv, token = refs[:n], refs[n:2 * n], refs[2 * n], refs[2 * n + 1], refs[-1]
        for i in range(n):
            for j in range(3):
                _chip_copy(ins[i], lnd[i], j, send, recv).start()
        token[...] = jnp.zeros_like(token)

    res = pl.pallas_call(
        body, name=name, in_specs=[HBM_SPEC] * (2 * n),
        out_specs=[SEM_SPEC, SEM_SPEC] + [HBM_SPEC] * (2 * n) + [pl.BlockSpec(memory_space=pltpu.VMEM)],
        out_shape=[DMA_SEM(()), DMA_SEM(())] + [pltpu.HBM(a.shape, a.dtype) for a in ps + lands]
        + [_sds((8, LANES), F32)],
        input_output_aliases={i: 2 + i for i in range(2 * n)},
        compiler_params=pltpu.CompilerParams(has_side_effects=DATAFLOW),
    )(*[_hbm(a) for a in ps + lands])
    return (res[0], res[1]), list(res[2:2 + n]), list(res[2 + n:2 + 2 * n]), res[-1]


def _chip_send_wait(name, ps, lands, sems, after):
    n = len(ps)

    def body(*refs):
        ins, lnd, send, recv = refs[:n], refs[n:2 * n], refs[2 * n], refs[2 * n + 1]
        for i in range(n):
            for j in range(3):
                cp = _chip_copy(ins[i], lnd[i], j, send, recv)
                cp.wait_send()
                cp.wait_recv()

    res = pl.pallas_call(
        body, name=name, in_specs=[HBM_SPEC] * (2 * n) + [SEM_SPEC, SEM_SPEC, pl.BlockSpec(memory_space=pl.ANY)],
        out_specs=[HBM_SPEC] * (2 * n), out_shape=[pltpu.HBM(a.shape, a.dtype) for a in ps + lands],
        input_output_aliases={i: i for i in range(2 * n)},
        compiler_params=pltpu.CompilerParams(has_side_effects=DATAFLOW),
    )(*ps, *lands, sems[0], sems[1], after)
    return list(res[:n]), list(res[n:])


def _join_halves(arrs):
    n = len(arrs)

    def body(*refs):
        bufs = refs[n:2 * n]
        send, recv = refs[2 * n:]
        x, y, c = _mesh_pos()
        cps = []
        for i in range(n):
            mine = bufs[i].at[:, c]
            cp = pltpu.make_async_remote_copy(
                src_ref=mine, dst_ref=mine, send_sem=send.at[i], recv_sem=recv.at[i],
                device_id=(x, y, 1 - c), device_id_type=MESH_ID)
            cp.start()
            cps.append(cp)
        for i, cp in enumerate(cps):
            theirs = bufs[i].at[:, 1 - c]
            cp.wait_send()
            pltpu.make_async_remote_copy(
                src_ref=theirs, dst_ref=theirs, send_sem=send.at[i], recv_sem=recv.at[i],
                device_id=(x, y, 1 - c), device_id_type=MESH_ID).wait_recv()

    return pl.pallas_call(
        body, name="join_halves", in_specs=[HBM_SPEC] * n, out_specs=[HBM_SPEC] * n,
        out_shape=[_sds(a.shape, a.dtype) for a in arrs],
        scratch_shapes=[DMA_SEM((n,)), DMA_SEM((n,))], input_output_aliases={i: i for i in range(n)},
    )(*arrs)


def _peer_copy(buf, land, k, send, recv, landing):
    x, y, c = _mesh_pos()
    px, py, pc = x ^ ((k >> 2) & 1), y ^ ((k >> 1) & 1), c ^ (k & 1)
    slot = 4 * px + 2 * py + pc if landing else 4 * x + 2 * y + c
    return pltpu.make_async_remote_copy(src_ref=buf, dst_ref=land.at[slot], send_sem=send, recv_sem=recv,
                                        device_id=(px, py, pc), device_id_type=MESH_ID)


def _exchange_all(buf):
    def body(in_ref, out_ref, send, recv):
        cps = [_peer_copy(in_ref, out_ref, k, send.at[k - 1], recv.at[k - 1], False) for k in range(1, 8)]
        for cp in cps:
            cp.start()
        for k in range(1, 8):
            cps[k - 1].wait_send()
            _peer_copy(in_ref, out_ref, k, send.at[k - 1], recv.at[k - 1], True).wait_recv()

    return pl.pallas_call(
        body, name="exchange_all", in_specs=[HBM_SPEC], out_specs=HBM_SPEC,
        out_shape=_sds((8,) + buf.shape, buf.dtype), scratch_shapes=[DMA_SEM((7,)), DMA_SEM((7,))],
    )(buf)


def _exchange_start(name, buf):
    land = lax.empty((8,) + buf.shape, buf.dtype)

    def body(in_ref, land_ref, send, recv, in_thru, land_thru, token):
        for k in range(1, 8):
            _peer_copy(in_ref, land_ref, k, send, recv, False).start()
        token[...] = jnp.zeros_like(token)

    res = pl.pallas_call(
        body, name=name, in_specs=[HBM_SPEC] * 2,
        out_specs=[SEM_SPEC, SEM_SPEC, HBM_SPEC, HBM_SPEC, pl.BlockSpec(memory_space=pltpu.VMEM)],
        out_shape=[DMA_SEM(()), DMA_SEM(()), pltpu.HBM(buf.shape, buf.dtype), pltpu.HBM(land.shape, land.dtype),
                   _sds((8, LANES), F32)],
        input_output_aliases={0: 2, 1: 3}, compiler_params=pltpu.CompilerParams(has_side_effects=DATAFLOW),
    )(_hbm(buf), _hbm(land))
    return (res[0], res[1]), res[2], res[3], res[4]


def _exchange_wait(name, buf, land, sems, after):
    def body(in_ref, land_ref, send, recv, after_ref, in_thru, land_thru):
        for k in range(1, 8):
            _peer_copy(in_ref, land_ref, k, send, recv, False).wait_send()
            _peer_copy(in_ref, land_ref, k, send, recv, True).wait_recv()

    res = pl.pallas_call(
        body, name=name, in_specs=[HBM_SPEC, HBM_SPEC, SEM_SPEC, SEM_SPEC, pl.BlockSpec(memory_space=pl.ANY)],
        out_specs=[HBM_SPEC, HBM_SPEC], out_shape=[pltpu.HBM(buf.shape, buf.dtype), pltpu.HBM(land.shape, land.dtype)],
        input_output_aliases={0: 0, 1: 1}, compiler_params=pltpu.CompilerParams(has_side_effects=DATAFLOW),
    )(buf, land, sems[0], sems[1], after)
    return res[0], res[1]


def _pad_to(a, axis, size):
    pad = [(0, 0)] * a.ndim
    pad[axis] = (0, size - a.shape[axis])
    return jnp.pad(a, pad)


def _strips(w):
    k, d = w.shape
    return _pad_to(w, 0, 32).reshape(32, d // LANES, LANES).transpose(1, 0, 2)


def kernel(x, norm_mix, w_in, gate_bias, conv_w, conv_b, conv_ln_g, conv_ln_b, w_conv_out, sgu_ln_g, sgu_ln_b, w_spatial, b_spatial, w_sgu_out, w_o, norm_ffn, w_ffn_gate, w_ffn_up, w_ffn_down, norm_final, loss_target, m_norm_mix, m_w_in, m_gate_bias, m_conv_w, m_conv_b, m_conv_ln_g, m_conv_ln_b, m_w_conv_out, m_sgu_ln_g, m_sgu_ln_b, m_w_spatial, m_b_spatial, m_w_sgu_out, m_w_o, m_norm_ffn, m_w_ffn_gate, m_w_ffn_up, m_w_ffn_down, m_norm_final, v_norm_mix, v_w_in, v_gate_bias, v_conv_w, v_conv_b, v_conv_ln_g, v_conv_ln_b, v_w_conv_out, v_sgu_ln_g, v_sgu_ln_b, v_w_spatial, v_b_spatial, v_w_sgu_out, v_w_o, v_norm_ffn, v_w_ffn_gate, v_w_ffn_up, v_w_ffn_down, v_norm_final):
    BL, S, D = x.shape
    T = BL * S
    L = w_in.shape[0]
    CS = w_in.shape[2]
    CN = CS // 3
    DQ = D // NSHARD
    FS = w_ffn_gate.shape[2]
    F = NSHARD * FS
    G, CH = w_spatial.shape[1], w_spatial.shape[2]
    KW = conv_w.shape[1]
    CQ = conv_w.shape[3]
    NSTR = D // LANES
    tm = min(512, S // 2)
    tm2 = max(tm // 2, CH)
    rb = min(64, tm)
    mx, my, mc = _mesh_pos()
    pos = jnp.stack([mc, 2 * mx + my]).astype(jnp.int32)

    def placed(name, w, dtype=BF16, dep=None):
        return _place_shard("place_" + name, w, pos, dtype, _row_tile(w.shape[0], 256), dep)

    w_in0 = placed("w_in", w_in[0])
    cw_p = placed("conv_w", _pad_to(conv_w.reshape(L, KW, CQ), 1, 32).reshape(L * 32, CQ), F32)
    fsems, fflying, ftoken = _gather_start("gather_start_first", [[w_in0, cw_p]])
    wts = []
    for l in range(L):
        w_sq = jnp.concatenate([w_conv_out[l], w_sgu_out[l], w_o[l]], axis=0)
        wts.append(dict(w_in=placed("w_in", w_in[l], dep=ftoken) if l else None, w_sq=placed("w_sq", w_sq, dep=ftoken),
                        wg=placed("w_gate", w_ffn_gate[l].T, dep=ftoken), wu=placed("w_up", w_ffn_up[l].T, dep=ftoken),
                        wd=placed("w_down", w_ffn_down[l], dep=ftoken)))
    ffn_keys = ["wg", "wu", "wd"]
    order = [[(0, "w_sq")], [(0, k) for k in ffn_keys]]
    order += [[(l, k) for k in ["w_in", "w_sq"] + ffn_keys] for l in range(1, L)]
    gsems, flying, token = _gather_start("gather_start", [[wts[l][k] for l, k in grp] for grp in order])
    first = _gather_wait("gather_wait_first", fflying[0], fsems[0], token)
    wts[0]["w_in"], cw_g = _forward_sibling("gather_first", first, False)
    conv_w_full = cw_g.reshape(NSHARD, L, 32, CQ).transpose(1, 2, 0, 3).reshape(L, 32, D)[:, :KW]

    def land(gi, after):
        bufs = _gather_wait("gather_wait_%d" % gi, flying[gi], gsems[gi], after)
        bufs = _forward_sibling("gather_forward_%d" % gi, bufs, False)
        for (l, k), b in zip(order[gi], bufs):
            wts[l][k] = b

    x2d = x.reshape(T, D)
    tgt = loss_target.reshape(T, D)
    row = lambda a, l: a[l].reshape(1, -1)

    saved = []
    xc = x2d
    for l in range(L):
        ws_b = w_spatial[l].astype(BF16)
        bs_b = jnp.repeat(b_spatial[l].T, D // G, axis=1)
        cw_s = _strips(conv_w_full[l])
        h, proj = _mix_in_fwd(xc, row(norm_mix, l), wts[l]["w_in"], 0, tm, token if l == 0 else None)
        if l == 0:
            land(0, h)
        c1h, rstd_c, c3, ya = _conv_fwd(proj, cw_s, row(conv_b, l), row(conv_ln_g, l), row(conv_ln_b, l),
                                        wts[l]["w_sq"], 0, S, tm, rb)
        if l == 0:
            land(1, ya)
        mixed, gated, yb, merged, x1 = _sgu_merge_fwd(proj, ya, xc, row(sgu_ln_g, l), row(sgu_ln_b, l), ws_b, bs_b,
                                                      row(gate_bias, l), wts[l]["w_sq"], 0, tm2)
        ffn_w = [wts[l][k].reshape(F, D) for k in ffn_keys]
        h2, gt, up, act, x2 = _ffn_fwd(x1, row(norm_ffn, l), *ffn_w, tm2)
        if l + 1 < L:
            land(l + 2, x2)
        saved.append(dict(x=xc, h=h, proj=proj, c1h=c1h, rstd_c=rstd_c, c3=c3, ya=ya, mixed=mixed, gated=gated,
                          yb=yb, merged=merged, x1=x1, h2=h2, gt=gt, up=up, act=act, ws_b=ws_b, cw=conv_w_full[l]))
        xc = x2

    dx, loss_part, d_norm_final = _loss_head(xc, norm_final.reshape(1, D), tgt, tm)
    loss = lax.psum(loss_part[0, 0], ("x", "y", "c"))

    g_acc = {}

    def reduce_start(tag, layer, named):
        arrs = [g.reshape(NSHARD, 2, g.shape[1] // 2, g.shape[2]) for _, g in named]
        from_sib = _send_sibling_halves("sibling_" + tag, arrs)
        ps = [_add_halves("presum_" + nm, a, r, pos, _row_tile(a.shape[2], 256))
              for (nm, _), a, r in zip(named, arrs, from_sib)]
        sems, ps, lands, tok = _chip_send_start("chip_send_start_" + tag, ps)
        return dict(tag=tag, layer=layer, names=[nm for nm, _ in named], ps=ps, lands=lands, sems=sems), tok

    def reduce_finish(pend, after):
        ps, lands = _chip_send_wait("chip_send_wait_" + pend["tag"], pend["ps"], pend["lands"], pend["sems"], after)
        for nm, p, r in zip(pend["names"], ps, lands):
            g_acc[nm] = _add_shards("shardsum_" + nm, p, r, pos, _row_tile(p.shape[1], 256), pend["layer"], L,
                                    g_acc.get(nm))

    me_idx = (4 * mx + 2 * my + mc).astype(jnp.int32).reshape(1)
    exchanges = []

    def pack_rows(pieces):
        packed = jnp.concatenate(pieces, axis=0)
        return _pad_to(packed, 0, -(-packed.shape[0] // 8) * 8)

    def unpack_rows(summed, pieces):
        out, off = [], 0
        for p in pieces:
            out.append(summed[off:off + p.shape[0]])
            off += p.shape[0]
        return out

    def small_start(tag, pieces):
        sems, buf, land, token = _exchange_start("exchange_start_" + tag, pack_rows(pieces))
        return dict(tag=tag, pieces=pieces, buf=buf, land=land, sems=sems, token=token)

    def small_finish(st, after):
        buf, land = _exchange_wait("exchange_wait_" + st["tag"], st["buf"], st["land"], st["sems"], after)
        return unpack_rows(_sum_slots(buf, land, me_idx, _row_tile(buf.shape[0], 256)), st["pieces"])

    small = [None] * L
    tt = min(1024, T // 2)
    nt = T // tt
    pending, tok = None, None
    for l in reversed(range(L)):
        sv, wt = saved[l], wts[l]
        ffn_w = [wt[k].reshape(F, D) for k in ffn_keys]
        dx1, dgt, dup, d_norm_ffn = _ffn_bwd(dx, sv["x1"], sv["gt"], sv["up"], row(norm_ffn, l), *ffn_w, tm2, tok)
        tn_a = ((tt, F // 2), lambda j, t: (t, j))
        tn_b = ((tt, D), lambda j, t: (t, 0))
        tn_o = ((F, D), (F // 2, D), lambda j, t: (j, 0), 2, nt)
        g_g, = _tn_matmul("grad_w_gate", dgt, *tn_a, [sv["h2"]], *tn_b, *tn_o)
        g_u, = _tn_matmul("grad_w_up", dup, *tn_a, [sv["h2"]], *tn_b, *tn_o)
        g_d, = _tn_matmul("grad_w_down", sv["act"], *tn_a, [dx], *tn_b, *tn_o)
        if pending is not None:
            reduce_finish(pending, g_d)
        ffn_pend, tok = reduce_start("ffn%d" % l, l, [
            ("w_ffn_gate", g_g.reshape(NSHARD, FS, D)), ("w_ffn_up", g_u.reshape(NSHARD, FS, D)),
            ("w_ffn_down", g_d.reshape(NSHARD, FS, D))])
        wst_b = jnp.swapaxes(sv["ws_b"], 1, 2)
        dya, dyb, dp3, d_gate_bias, d_sgu_g, d_sgu_b, d_bs, d_ws = _merge_sgu_bwd(
            dx1, sv["proj"], sv["ya"], sv["yb"], sv["mixed"], row(sgu_ln_g, l), row(sgu_ln_b, l), sv["ws_b"], wst_b,
            row(gate_bias, l), wt["w_sq"], 0, tm2, tok)
        sq_args = ((tt, D), lambda j, t: (t, 0))
        sq_out = ((D, D), (D, D), lambda j, t: (0, 0), 1, nt)
        g_o, = _tn_matmul("grad_w_o", sv["merged"], *sq_args, [dx1], *sq_args, *sq_out)
        g_so, = _tn_matmul("grad_w_sgu_out", sv["gated"], *sq_args, [dyb], *sq_args, *sq_out)
        g_co, = _tn_matmul("grad_w_conv_out", sv["c3"], *sq_args, [dya], *sq_args, *sq_out)
        dc1, d_cln_g, d_cln_b, d_conv_b = _conv_ln_bwd(dya, sv["c1h"], sv["rstd_c"], row(conv_ln_g, l),
                                                       row(conv_ln_b, l), wt["w_sq"], 0, tm)
        small[l] = [None, d_gate_bias.reshape(2, D), None, d_conv_b, d_cln_g, d_cln_b, d_sgu_g, d_sgu_b,
                    d_ws.reshape(G * CH * CH // D, D), d_bs.reshape(G * CH // D, D), d_norm_ffn]
        tok_x = None
        if l == 0:
            early = [k for k in range(len(small[0])) if small[0][k] is not None]
            exchanges.append((small_start("early0", [small[0][k] for k in early]), [(0, k) for k in early]))
            tok_x = exchanges[-1][0]["token"]
        dp3, d_cw_s = _conv_bwd(dc1, sv["proj"], dp3, _strips(sv["cw"][::-1]), S, tm, rb, tok_x)
        g_in, = _tn_matmul("grad_w_in", sv["h"], (tt, D), lambda j, t: (t, 0), [dp3], (tt, CS), lambda j, t: (t, j),
                           (NSHARD, D, CS), (None, D, CS), lambda j, t: (j, 0, 0), NSHARD, nt)
        reduce_finish(ffn_pend, g_in)
        pending, tok = reduce_start("mix%d" % l, l, [
            ("w_in", g_in), ("w_conv_out", g_co.reshape(NSHARD, DQ, D)), ("w_sgu_out", g_so.reshape(NSHARD, DQ, D)),
            ("w_o", g_o.reshape(NSHARD, DQ, D))])
        dx, d_norm_mix = _mix_in_bwd(dx1, dp3, sv["x"], row(norm_mix, l), wt["w_in"], 0, tm, tok)
        small[l][0] = d_norm_mix
        small[l][2] = d_cw_s.transpose(1, 0, 2).reshape(32, D)
        if l > 0:
            exchanges.append((small_start("layer%d" % l, small[l]), [(l, k) for k in range(len(small[l]))]))
            tok = [tok, exchanges[-1][0]["token"]]
    reduce_finish(pending, dx)
    grad_x = dx.reshape(BL, S, D)

    names = ["w_in", "w_conv_out", "w_sgu_out", "w_o", "w_ffn_gate", "w_ffn_up", "w_ffn_down"]
    g_full = _join_halves([g_acc[nm] for nm in names])
    g_w_in, g_w_co, g_w_so, g_w_o, g_w_g, g_w_u, g_w_d = [g.reshape(L, 2 * g.shape[2], g.shape[3]) for g in g_full]
    g_w_g = jnp.swapaxes(g_w_g, 1, 2)
    g_w_u = jnp.swapaxes(g_w_u, 1, 2)

    late = [small[0][0], small[0][2], d_norm_final]
    packed = pack_rows(late)
    summed = _sum_slots(packed, _exchange_all(packed), me_idx, _row_tile(packed.shape[0], 256))
    sg = [[None] * len(small[l]) for l in range(L)]
    sg[0][0], sg[0][2], g_norm_final = unpack_rows(summed, late)
    g_norm_final = g_norm_final[0]
    for st, where in exchanges:
        for (l, k), piece in zip(where, small_finish(st, summed)):
            sg[l][k] = piece

    def per_layer(k, shape):
        return jnp.stack([sg[l][k] for l in range(L)]).reshape(shape)

    g_norm_mix = per_layer(0, (L, D))
    g_gate_bias = per_layer(1, (L, 2 * D))
    g_conv_w_full = jnp.stack([sg[l][2][:KW] for l in range(L)])
    g_conv_w = lax.dynamic_slice_in_dim(g_conv_w_full, (2 * mx + my) * CQ, CQ, axis=2).reshape(L, KW, 1, CQ)
    g_conv_b = per_layer(3, (L, D))
    g_conv_ln_g = per_layer(4, (L, D))
    g_conv_ln_b = per_layer(5, (L, D))
    g_sgu_ln_g = per_layer(6, (L, D))
    g_sgu_ln_b = per_layer(7, (L, D))
    g_w_spatial = per_layer(8, (L, G, CH, CH))
    g_b_spatial = per_layer(9, (L, G, CH))
    g_norm_ffn = per_layer(10, (L, D))

    grads = [g_norm_mix, g_w_in, g_gate_bias, g_conv_w, g_conv_b, g_conv_ln_g, g_conv_ln_b, g_w_co, g_sgu_ln_g,
             g_sgu_ln_b, g_w_spatial, g_b_spatial, g_w_so, g_w_o, g_norm_ffn, g_w_g, g_w_u, g_w_d, g_norm_final]
    weights = [norm_mix, w_in, gate_bias, conv_w, conv_b, conv_ln_g, conv_ln_b, w_conv_out, sgu_ln_g, sgu_ln_b,
               w_spatial, b_spatial, w_sgu_out, w_o, norm_ffn, w_ffn_gate, w_ffn_up, w_ffn_down, norm_final]
    ms = [m_norm_mix, m_w_in, m_gate_bias, m_conv_w, m_conv_b, m_conv_ln_g, m_conv_ln_b, m_w_conv_out, m_sgu_ln_g,
          m_sgu_ln_b, m_w_spatial, m_b_spatial, m_w_sgu_out, m_w_o, m_norm_ffn, m_w_ffn_gate, m_w_ffn_up,
          m_w_ffn_down, m_norm_final]
    vs = [v_norm_mix, v_w_in, v_gate_bias, v_conv_w, v_conv_b, v_conv_ln_g, v_conv_ln_b, v_w_conv_out, v_sgu_ln_g,
          v_sgu_ln_b, v_w_spatial, v_b_spatial, v_w_sgu_out, v_w_o, v_norm_ffn, v_w_ffn_gate, v_w_ffn_up,
          v_w_ffn_down, v_norm_final]

    big_idx = [1, 7, 12, 13, 15, 16, 17]
    deltas, new_m, new_v = [None] * 19, [None] * 19, [None] * 19
    for k in big_idx:
        shp = weights[k].shape
        r2 = (shp[0] * shp[1], shp[2])
        d_, m_, v_ = _adamw("adamw_" + str(k), weights[k].reshape(r2), grads[k].reshape(r2), ms[k].reshape(r2),
                            vs[k].reshape(r2), _row_tile(r2[0], 256))
        deltas[k], new_m[k], new_v[k] = d_.reshape(shp), m_.reshape(shp), v_.reshape(shp)
    small_idx = [k for k in range(19) if k not in big_idx]

    def pack(arrs):
        flat = jnp.concatenate([arrs[k].reshape(-1) for k in small_idx])
        rows = -(-flat.shape[0] // (256 * LANES)) * 256
        return _pad_to(flat, 0, rows * LANES).reshape(rows, LANES)

    pw, pg, pm, pv = pack(weights), pack(grads), pack(ms), pack(vs)
    d_, m_, v_ = _adamw("adamw_small", pw, pg, pm, pv, _row_tile(pw.shape[0], 512))
    off = 0
    for k in small_idx:
        n_el = weights[k].size
        shp = weights[k].shape
        deltas[k] = d_.reshape(-1)[off:off + n_el].reshape(shp)
        new_m[k] = m_.reshape(-1)[off:off + n_el].reshape(shp)
        new_v[k] = v_.reshape(-1)[off:off + n_el].reshape(shp)
        off += n_el

    return (loss, grad_x, *grads, *deltas, *new_m, *new_v)
```

```python
import functools

import jax
import jax.numpy as jnp
from jax import lax
from jax.experimental import pallas as pl
from jax.experimental.pallas import tpu as pltpu

F32 = jnp.float32
BF16 = jnp.bfloat16
EPS = 1e-6
ADAM_LR = 0.001
ADAM_B1 = 0.9
ADAM_B2 = 0.999
ADAM_EPS = 1e-08
ADAM_WD = 0.01
ADAM_STEP = 10

NSHARD = 4
LANES = 128
HALO = 16
VMEM_LIMIT = 60 * 1024 * 1024
MESH_ID = pl.DeviceIdType.MESH


def _dot(a, b):
    return jnp.dot(a, b, preferred_element_type=F32)


def _dot_nt(a, b):
    return lax.dot_general(a, b, (((1,), (1,)), ((), ())), preferred_element_type=F32)


def _dot_tn(a, b):
    return lax.dot_general(a, b, (((0,), (0,)), ((), ())), preferred_element_type=F32)


def _sig(z):
    return 1.0 / (1.0 + jnp.exp(-z))


def _res(shape, imap=None):
    nd = len(shape)
    if imap is None:
        imap = lambda *_: (0,) * nd
    return pl.BlockSpec(shape, imap, pipeline_mode=pl.Buffered(1))


def _cparams(sem):
    return pltpu.CompilerParams(dimension_semantics=sem, vmem_limit_bytes=VMEM_LIMIT)


def _sds(shape, dtype):
    return jax.ShapeDtypeStruct(shape, dtype)


def _after(body, n_in, dep):
    deps = [] if dep is None else [d for d in (dep if isinstance(dep, (list, tuple)) else [dep]) if d is not None]
    if not deps:
        return body, [], []

    def wrapped(*refs):
        return body(*refs[:n_in], *refs[n_in + len(deps):])

    return wrapped, [pl.BlockSpec(memory_space=pl.ANY)] * len(deps), deps


def _mix_in_fwd(x2d, g_mix, w_in_g, layer, tm, dep=None):
    T, D = x2d.shape
    CS = w_in_g.shape[2]
    CN = CS // 3

    def body(x_ref, g_ref, w_ref, h_ref, p_ref):
        x = x_ref[...]
        rstd = lax.rsqrt(jnp.mean(x * x, axis=-1, keepdims=True) + EPS)
        h = (x * rstd * g_ref[...]).astype(BF16)
        h_ref[...] = h
        for s in range(NSHARD):
            for j in range(3):
                c0 = s * CS + j * CN
                p_ref[:, c0:c0 + CN] = _dot(h, w_ref[s, :, j * CN:(j + 1) * CN]).astype(BF16)

    body, dep_spec, dep_arg = _after(body, 3, dep)
    return pl.pallas_call(
        body, name="mix_in_fwd", grid=(T // tm,),
        in_specs=[pl.BlockSpec((tm, D), lambda i: (i, 0)), _res((1, D)),
                  _res((NSHARD, D, CS), lambda i: (0, layer, 0))] + dep_spec,
        out_specs=[pl.BlockSpec((tm, D), lambda i: (i, 0)), pl.BlockSpec((tm, NSHARD * CS), lambda i: (i, 0))],
        out_shape=[_sds((T, D), BF16), _sds((T, NSHARD * CS), BF16)],
        compiler_params=_cparams(("parallel",)),
    )(x2d, g_mix, w_in_g, *dep_arg)


def _halo_maps(tm, n_rows):
    nb = tm // HALO
    last = n_rows // HALO - 1
    prev = lambda i: (jnp.maximum(i * nb - 1, 0), 0)
    nxt = lambda i: (jnp.minimum((i + 1) * nb, last), 0)
    return prev, nxt


def _dwconv(pad_ref, w_ref, out_ref, n_strips, tm, kw, rb):
    off = HALO - (kw - 1) // 2

    def strip(cs, carry):
        for r0 in range(0, tm, rb):
            acc = jnp.zeros((rb, LANES), F32)
            for k in range(kw):
                r = r0 + off + k
                acc = acc + w_ref[cs, k:k + 1, :] * pad_ref[cs, r:r + rb, :]
            out_ref[cs, r0:r0 + rb, :] = acc
        return carry

    lax.fori_loop(0, n_strips, strip, 0)


def _fill_c0_pad(pad_ref, pa_ref, pprev_ref, pnext_ref, D, tm, first, last):
    for cs in range(D // LANES):
        lo, hi = cs * LANES, (cs + 1) * LANES

        def c0_of(ref):
            return ref[:, lo:hi].astype(F32) * _sig(ref[:, D + lo:D + hi].astype(F32))

        pad_ref[cs, HALO:HALO + tm, :] = c0_of(pa_ref)
        pad_ref[cs, 0:HALO, :] = jnp.where(first, 0.0, c0_of(pprev_ref))
        pad_ref[cs, HALO + tm:HALO + tm + HALO, :] = jnp.where(last, 0.0, c0_of(pnext_ref))


def _conv_fwd(proj, conv_w_s, conv_b, ln_g, ln_b, w_sq_g, layer, seq, tm, rb):
    T = proj.shape[0]
    D = conv_b.shape[1]
    DQ = D // NSHARD
    NSTR = D // LANES
    KW = 31
    tps = seq // tm
    prev, nxt = _halo_maps(tm, T)

    def body(pa_ref, pprev_ref, pnext_ref, w_ref, b_ref, g_ref, be_ref, wco_ref,
             c1h_ref, rstd_ref, c3_ref, ya_ref, pad_ref, c1s_ref):
        i = pl.program_id(0)
        first = (i % tps) == 0
        last = (i % tps) == tps - 1
        _fill_c0_pad(pad_ref, pa_ref, pprev_ref, pnext_ref, D, tm, first, last)
        _dwconv(pad_ref, w_ref, c1s_ref, NSTR, tm, KW, rb)
        c1 = jnp.concatenate([c1s_ref[cs] for cs in range(NSTR)], axis=1) + b_ref[...]
        mu = jnp.mean(c1, axis=-1, keepdims=True)
        cc = c1 - mu
        rstd = lax.rsqrt(jnp.mean(cc * cc, axis=-1, keepdims=True) + EPS)
        c1h = cc * rstd
        c1h_ref[...] = c1h.astype(BF16)
        rstd_ref[...] = rstd
        c2 = c1h * g_ref[...] + be_ref[...]
        c3 = (c2 * _sig(c2)).astype(BF16)
        c3_ref[...] = c3
        ya_ref[...] = _dot(c3, wco_ref[...].reshape(D, D)).astype(BF16)

    row = lambda i: (i, 0)
    return pl.pallas_call(
        body, name="conv_fwd", grid=(T // tm,),
        in_specs=[pl.BlockSpec((tm, 2 * D), row), pl.BlockSpec((HALO, 2 * D), prev), pl.BlockSpec((HALO, 2 * D), nxt),
                  _res((NSTR, 32, LANES)), _res((1, D)), _res((1, D)), _res((1, D)),
                  _res((NSHARD, DQ, D), lambda i: (0, layer * 3 + 0, 0))],
        out_specs=[pl.BlockSpec((tm, D), row), pl.BlockSpec((tm, 1), row), pl.BlockSpec((tm, D), row),
                   pl.BlockSpec((tm, D), row)],
        out_shape=[_sds((T, D), BF16), _sds((T, 1), F32), _sds((T, D), BF16), _sds((T, D), BF16)],
        scratch_shapes=[pltpu.VMEM((NSTR, tm + 2 * HALO, LANES), F32), pltpu.VMEM((NSTR, tm, LANES), F32)],
        compiler_params=_cparams(("parallel",)),
    )(proj, proj, proj, conv_w_s, conv_b, ln_g, ln_b, w_sq_g)


def _sgu_merge_fwd(proj, ya, x2d, ln_g, ln_b, ws_b, bs_b, gate_bias, w_sq_g, layer, tm):
    T, D = x2d.shape
    DQ = D // NSHARD
    G, CH, _ = ws_b.shape
    GD = D // G

    def body(puv_ref, pg_ref, ya_ref, x_ref, g_ref, be_ref, ws_ref, bsb_ref, gb_ref, wso_ref, wo_ref,
             mixed_ref, gated_ref, yb_ref, merged_ref, x1_ref, mix_scr):
        u = puv_ref[:, :D].astype(F32)
        v = puv_ref[:, D:].astype(F32)
        mu = jnp.mean(v, axis=-1, keepdims=True)
        vc = v - mu
        rstd = lax.rsqrt(jnp.mean(vc * vc, axis=-1, keepdims=True) + EPS)
        vn = (vc * rstd * g_ref[...] + be_ref[...]).astype(BF16)
        nch = tm // CH
        for g in range(G):
            cols = slice(g * GD, (g + 1) * GD)
            rhs = jnp.concatenate([vn[ch * CH:(ch + 1) * CH, cols] for ch in range(nch)], axis=1)
            res = _dot(ws_ref[g], rhs)
            for ch in range(nch):
                mix_scr[ch * CH:(ch + 1) * CH, cols] = res[:, ch * GD:(ch + 1) * GD] + bsb_ref[:, cols]
        mixed = mix_scr[...]
        mixed_ref[...] = mixed.astype(BF16)
        gated = (u * mixed).astype(BF16)
        gated_ref[...] = gated
        yb = _dot(gated, wso_ref[...].reshape(D, D))
        yb_ref[...] = yb.astype(BF16)
        sa = _sig(pg_ref[:, :D].astype(F32) + gb_ref[:, :D])
        sb = _sig(pg_ref[:, D:].astype(F32) + gb_ref[:, D:])
        merged = (sa * ya_ref[...].astype(F32) + sb * yb).astype(BF16)
        merged_ref[...] = merged
        x1_ref[...] = x_ref[...] + _dot(merged, wo_ref[...].reshape(D, D))

    row = lambda i: (i, 0)
    return pl.pallas_call(
        body, name="sgu_merge_fwd", grid=(T // tm,),
        in_specs=[pl.BlockSpec((tm, 2 * D), lambda i: (i, 1)), pl.BlockSpec((tm, 2 * D), lambda i: (i, 2)),
                  pl.BlockSpec((tm, D), row), pl.BlockSpec((tm, D), row),
                  _res((1, D)), _res((1, D)), _res((G, CH, CH)), _res((CH, D)), _res((1, 2 * D)),
                  _res((NSHARD, DQ, D), lambda i: (0, layer * 3 + 1, 0)),
                  _res((NSHARD, DQ, D), lambda i: (0, layer * 3 + 2, 0))],
        out_specs=[pl.BlockSpec((tm, D), row)] * 5,
        out_shape=[_sds((T, D), BF16)] * 4 + [_sds((T, D), F32)],
        scratch_shapes=[pltpu.VMEM((tm, D), F32)],
        compiler_params=_cparams(("parallel",)),
    )(proj, proj, ya, x2d, ln_g, ln_b, ws_b, bs_b, gate_bias, w_sq_g, w_sq_g)


def _ffn_chunks(F):
    assert F % 256 == 0, F
    return [(c0, min(512, F - c0)) for c0 in range(0, F, 512)]


def _ffn_fwd(x1, g_ffn, wgt, wut, wd, tm):
    T, D = x1.shape
    F = wd.shape[0]

    def body(x_ref, g_ref, wg_ref, wu_ref, wd_ref, h2_ref, gt_ref, up_ref, act_ref, x2_ref):
        x = x_ref[...]
        rstd = lax.rsqrt(jnp.mean(x * x, axis=-1, keepdims=True) + EPS)
        h2 = (x * rstd * g_ref[...]).astype(BF16)
        h2_ref[...] = h2
        acc = x
        for c0, cw in _ffn_chunks(F):
            gt = _dot_nt(h2, wg_ref[c0:c0 + cw, :])
            up = _dot_nt(h2, wu_ref[c0:c0 + cw, :])
            gt_ref[:, c0:c0 + cw] = gt.astype(BF16)
            up_ref[:, c0:c0 + cw] = up.astype(BF16)
            act = (gt * _sig(gt) * up).astype(BF16)
            act_ref[:, c0:c0 + cw] = act
            acc = acc + _dot(act, wd_ref[c0:c0 + cw, :])
        x2_ref[...] = acc

    row = lambda i: (i, 0)
    return pl.pallas_call(
        body, name="ffn_fwd", grid=(T // tm,),
        in_specs=[pl.BlockSpec((tm, D), row), _res((1, D)), _res((F, D)), _res((F, D)), _res((F, D))],
        out_specs=[pl.BlockSpec((tm, D), row), pl.BlockSpec((tm, F), row), pl.BlockSpec((tm, F), row),
                   pl.BlockSpec((tm, F), row), pl.BlockSpec((tm, D), row)],
        out_shape=[_sds((T, D), BF16), _sds((T, F), BF16), _sds((T, F), BF16), _sds((T, F), BF16), _sds((T, D), F32)],
        compiler_params=_cparams(("parallel",)),
    )(x1, g_ffn, wgt, wut, wd)


def _loss_head(xf, g_fin, target, tm):
    T, D = xf.shape
    n = T // tm

    def body(x_ref, g_ref, t_ref, dx_ref, loss_ref, dg_ref, acc_ref):
        i = pl.program_id(0)

        @pl.when(i == 0)
        def _():
            acc_ref[...] = jnp.zeros_like(acc_ref)
            dg_ref[...] = jnp.zeros_like(dg_ref)

        x = x_ref[...]
        g = g_ref[...]
        rstd = lax.rsqrt(jnp.mean(x * x, axis=-1, keepdims=True) + EPS)
        xh = x * rstd
        diff = xh * g - t_ref[...]
        acc_ref[...] += jnp.sum(diff * diff, axis=0, keepdims=True)
        dy = diff * (1.0 / D)
        dg_ref[...] += jnp.sum(dy * xh, axis=0, keepdims=True)
        dxh = dy * g
        dx_ref[...] = rstd * (dxh - xh * jnp.mean(dxh * xh, axis=-1, keepdims=True))

        @pl.when(i == n - 1)
        def _():
            tot = jnp.sum(acc_ref[...], axis=-1, keepdims=True) * (0.5 / D)
            loss_ref[...] = jnp.broadcast_to(tot, loss_ref.shape)

    row = lambda i: (i, 0)
    return pl.pallas_call(
        body, name="loss_head", grid=(n,),
        in_specs=[pl.BlockSpec((tm, D), row), _res((1, D)), pl.BlockSpec((tm, D), row)],
        out_specs=[pl.BlockSpec((tm, D), row), pl.BlockSpec((1, LANES), lambda i: (0, 0)),
                   pl.BlockSpec((1, D), lambda i: (0, 0))],
        out_shape=[_sds((T, D), F32), _sds((1, LANES), F32), _sds((1, D), F32)],
        scratch_shapes=[pltpu.VMEM((1, D), F32)],
        compiler_params=_cparams(("arbitrary",)),
    )(xf, g_fin, target)


def _ffn_bwd(dx2, x1, gt, up, g_ffn, wgt, wut, wd, tm, dep=None):
    T, D = x1.shape
    F = wd.shape[0]

    def body(dx2_ref, x1_ref, gt_ref, up_ref, g_ref, wg_ref, wu_ref, wd_ref, dx1_ref, dgt_ref, dup_ref, dg_ref):
        i = pl.program_id(0)

        @pl.when(i == 0)
        def _():
            dg_ref[...] = jnp.zeros_like(dg_ref)

        dx2 = dx2_ref[...]
        dx2b = dx2.astype(BF16)
        dh2 = jnp.zeros((tm, D), F32)
        for c0, cw in _ffn_chunks(F):
            dact = _dot_nt(dx2b, wd_ref[c0:c0 + cw, :])
            g = gt_ref[:, c0:c0 + cw].astype(F32)
            u = up_ref[:, c0:c0 + cw].astype(F32)
            sg = _sig(g)
            dup = (dact * (g * sg)).astype(BF16)
            dgt = (dact * u * (sg * (1.0 + g * (1.0 - sg)))).astype(BF16)
            dgt_ref[:, c0:c0 + cw] = dgt
            dup_ref[:, c0:c0 + cw] = dup
            dh2 = dh2 + _dot(dgt, wg_ref[c0:c0 + cw, :]) + _dot(dup, wu_ref[c0:c0 + cw, :])
        x = x1_ref[...]
        rstd = lax.rsqrt(jnp.mean(x * x, axis=-1, keepdims=True) + EPS)
        xh = x * rstd
        dg_ref[...] += jnp.sum(dh2 * xh, axis=0, keepdims=True)
        dxh = dh2 * g_ref[...]
        dx1_ref[...] = dx2 + rstd * (dxh - xh * jnp.mean(dxh * xh, axis=-1, keepdims=True))

    row = lambda i: (i, 0)
    body, dep_spec, dep_arg = _after(body, 8, dep)
    return pl.pallas_call(
        body, name="ffn_bwd", grid=(T // tm,),
        in_specs=[pl.BlockSpec((tm, D), row), pl.BlockSpec((tm, D), row), pl.BlockSpec((tm, F), row),
                  pl.BlockSpec((tm, F), row), _res((1, D)), _res((F, D)), _res((F, D)), _res((F, D))] + dep_spec,
        out_specs=[pl.BlockSpec((tm, D), row), pl.BlockSpec((tm, F), row), pl.BlockSpec((tm, F), row),
                   pl.BlockSpec((1, D), lambda i: (0, 0))],
        out_shape=[_sds((T, D), F32), _sds((T, F), BF16), _sds((T, F), BF16), _sds((1, D), F32)],
        compiler_params=_cparams(("arbitrary",)),
    )(dx2, x1, gt, up, g_ffn, wgt, wut, wd, *dep_arg)


def _merge_sgu_bwd(dx1, proj, ya, yb, mixed, ln_g, ln_b, ws_b, wst_b, gate_bias, w_sq_g, layer, tm, dep=None):
    T, D = dx1.shape
    DQ = D // NSHARD
    G, CH, _ = ws_b.shape
    GD = D // G

    def body(dx1_ref, puv_ref, pg_ref, ya_ref, yb_ref, mixed_ref, g_ref, be_ref, ws_ref, wst_ref, gb_ref,
             wso_ref, wo_ref, dya_ref, dyb_ref, dp_ref, dgb_ref, dlg_ref, dlb_ref, dbs_ref, dws_ref,
             dvn_scr, dbs_scr):
        i = pl.program_id(0)

        @pl.when(i == 0)
        def _():
            for r in (dgb_ref, dlg_ref, dlb_ref, dws_ref, dbs_scr):
                r[...] = jnp.zeros_like(r)

        dmerged = _dot_nt(dx1_ref[...].astype(BF16), wo_ref[...].reshape(D, D))
        sa = _sig(pg_ref[:, :D].astype(F32) + gb_ref[:, :D])
        sb = _sig(pg_ref[:, D:].astype(F32) + gb_ref[:, D:])
        dya = (dmerged * sa).astype(BF16)
        dyb = (dmerged * sb).astype(BF16)
        dya_ref[...] = dya
        dyb_ref[...] = dyb
        dga = dmerged * ya_ref[...].astype(F32) * (sa * (1.0 - sa))
        dgb = dmerged * yb_ref[...].astype(F32) * (sb * (1.0 - sb))
        dp_ref[:, 4 * D:5 * D] = dga.astype(BF16)
        dp_ref[:, 5 * D:6 * D] = dgb.astype(BF16)
        dgb_ref[:, :D] += jnp.sum(dga, axis=0, keepdims=True)
        dgb_ref[:, D:] += jnp.sum(dgb, axis=0, keepdims=True)

        dgated = _dot_nt(dyb, wso_ref[...].reshape(D, D))
        u = puv_ref[:, :D].astype(F32)
        v = puv_ref[:, D:].astype(F32)
        dp_ref[:, 2 * D:3 * D] = (dgated * mixed_ref[...].astype(F32)).astype(BF16)
        dmixed = dgated * u
        mu = jnp.mean(v, axis=-1, keepdims=True)
        vc = v - mu
        rstd = lax.rsqrt(jnp.mean(vc * vc, axis=-1, keepdims=True) + EPS)
        vh = vc * rstd
        vn = (vh * g_ref[...] + be_ref[...]).astype(BF16)
        dmb = dmixed.astype(BF16)
        nch = tm // CH
        bs_part = dmixed[0:CH, :]
        for ch in range(1, nch):
            bs_part = bs_part + dmixed[ch * CH:(ch + 1) * CH, :]
        dbs_scr[...] += bs_part
        for g in range(G):
            cols = slice(g * GD, (g + 1) * GD)
            dm_g = jnp.concatenate([dmb[ch * CH:(ch + 1) * CH, cols] for ch in range(nch)], axis=1)
            vn_g = jnp.concatenate([vn[ch * CH:(ch + 1) * CH, cols] for ch in range(nch)], axis=1)
            dws_ref[g] += _dot_nt(dm_g, vn_g)
            dvn_g = _dot(wst_ref[g], dm_g)
            for ch in range(nch):
                dvn_scr[ch * CH:(ch + 1) * CH, cols] = dvn_g[:, ch * GD:(ch + 1) * GD]
        dvn = dvn_scr[...]
        dlg_ref[...] += jnp.sum(dvn * vh, axis=0, keepdims=True)
        dlb_ref[...] += jnp.sum(dvn, axis=0, keepdims=True)
        dxh = dvn * g_ref[...]
        dv = rstd * (dxh - jnp.mean(dxh, axis=-1, keepdims=True) - vh * jnp.mean(dxh * vh, axis=-1, keepdims=True))
        dp_ref[:, 3 * D:4 * D] = dv.astype(BF16)

        @pl.when(i == pl.num_programs(0) - 1)
        def _():
            for g in range(G):
                blk = dbs_scr[:, g * GD:(g + 1) * GD]
                if GD != CH:
                    blk = jnp.concatenate([blk, jnp.zeros((CH, CH - GD), F32)], axis=1)
                dbs_ref[:, g * CH:(g + 1) * CH] = jnp.sum(blk.T, axis=0, keepdims=True)

    row = lambda i: (i, 0)
    fixed2 = lambda i: (0, 0)
    body, dep_spec, dep_arg = _after(body, 13, dep)
    return pl.pallas_call(
        body, name="merge_sgu_bwd", grid=(T // tm,),
        in_specs=[pl.BlockSpec((tm, D), row), pl.BlockSpec((tm, 2 * D), lambda i: (i, 1)),
                  pl.BlockSpec((tm, 2 * D), lambda i: (i, 2)), pl.BlockSpec((tm, D), row), pl.BlockSpec((tm, D), row),
                  pl.BlockSpec((tm, D), row), _res((1, D)), _res((1, D)), _res((G, CH, CH)), _res((G, CH, CH)),
                  _res((1, 2 * D)),
                  _res((NSHARD, DQ, D), lambda i: (0, layer * 3 + 1, 0)),
                  _res((NSHARD, DQ, D), lambda i: (0, layer * 3 + 2, 0))] + dep_spec,
        out_specs=[pl.BlockSpec((tm, D), row), pl.BlockSpec((tm, D), row),
                   pl.BlockSpec((tm, 6 * D), row),
                   pl.BlockSpec((1, 2 * D), fixed2), pl.BlockSpec((1, D), fixed2), pl.BlockSpec((1, D), fixed2),
                   pl.BlockSpec((1, G * CH), fixed2), pl.BlockSpec((G, CH, CH), lambda i: (0, 0, 0))],
        out_shape=[_sds((T, D), BF16), _sds((T, D), BF16), _sds((T, 6 * D), BF16),
                   _sds((1, 2 * D), F32), _sds((1, D), F32), _sds((1, D), F32), _sds((1, G * CH), F32),
                   _sds((G, CH, CH), F32)],
        scratch_shapes=[pltpu.VMEM((tm, D), F32), pltpu.VMEM((CH, D), F32)],
        compiler_params=_cparams(("arbitrary",)),
    )(dx1, proj, proj, ya, yb, mixed, ln_g, ln_b, ws_b, wst_b, gate_bias, w_sq_g, w_sq_g, *dep_arg)


def _conv_ln_bwd(dya, c1h, rstd_c, ln_g, ln_b, w_sq_g, layer, tm):
    T, D = dya.shape
    DQ = D // NSHARD

    def body(dya_ref, c1h_ref, rstd_ref, g_ref, be_ref, wco_ref, dc1_ref, dlg_ref, dlb_ref, dcb_ref):
        i = pl.program_id(0)

        @pl.when(i == 0)
        def _():
            for r in (dlg_ref, dlb_ref, dcb_ref):
                r[...] = jnp.zeros_like(r)

        dc3 = _dot_nt(dya_ref[...], wco_ref[...].reshape(D, D))
        c1h = c1h_ref[...].astype(F32)
        c2 = c1h * g_ref[...] + be_ref[...]
        sg = _sig(c2)
        dc2 = dc3 * (sg * (1.0 + c2 * (1.0 - sg)))
        dlg_ref[...] += jnp.sum(dc2 * c1h, axis=0, keepdims=True)
        dlb_ref[...] += jnp.sum(dc2, axis=0, keepdims=True)
        dxh = dc2 * g_ref[...]
        dc1 = rstd_ref[...] * (dxh - jnp.mean(dxh, axis=-1, keepdims=True)
                               - c1h * jnp.mean(dxh * c1h, axis=-1, keepdims=True))
        dc1_ref[...] = dc1
        dcb_ref[...] += jnp.sum(dc1, axis=0, keepdims=True)

    row = lambda i: (i, 0)
    fixed2 = lambda i: (0, 0)
    return pl.pallas_call(
        body, name="conv_ln_bwd", grid=(T // tm,),
        in_specs=[pl.BlockSpec((tm, D), row), pl.BlockSpec((tm, D), row), pl.BlockSpec((tm, 1), row),
                  _res((1, D)), _res((1, D)), _res((NSHARD, DQ, D), lambda i: (0, layer * 3 + 0, 0))],
        out_specs=[pl.BlockSpec((tm, D), row), pl.BlockSpec((1, D), fixed2), pl.BlockSpec((1, D), fixed2),
                   pl.BlockSpec((1, D), fixed2)],
        out_shape=[_sds((T, D), F32), _sds((1, D), F32), _sds((1, D), F32), _sds((1, D), F32)],
        compiler_params=_cparams(("arbitrary",)),
    )(dya, c1h, rstd_c, ln_g, ln_b, w_sq_g)


def _conv_bwd(dc1, proj, dp3, conv_wf_s, seq, tm, rb, dep=None):
    T, D = dc1.shape
    NSTR = D // LANES
    KW = 31
    PADK = (KW - 1) // 2
    tps = seq // tm
    prev, nxt = _halo_maps(tm, T)
    n = T // tm

    def body(dc_ref, dcprev_ref, dcnext_ref, pa_ref, pprev_ref, pnext_ref, wf_ref, dp_in_ref,
             dp_ref, dw_ref, pad_ref, dpad_ref, dc0_ref, dwacc_ref):
        del dp_in_ref
        i = pl.program_id(0)
        first = (i % tps) == 0
        last = (i % tps) == tps - 1

        @pl.when(i == 0)
        def _():
            dwacc_ref[...] = jnp.zeros_like(dwacc_ref)

        _fill_c0_pad(pad_ref, pa_ref, pprev_ref, pnext_ref, D, tm, first, last)
        for cs in range(NSTR):
            lo, hi = cs * LANES, (cs + 1) * LANES
            dpad_ref[cs, HALO:HALO + tm, :] = dc_ref[:, lo:hi]
            dpad_ref[cs, 0:HALO, :] = jnp.where(first, 0.0, dcprev_ref[:, lo:hi])
            dpad_ref[cs, HALO + tm:HALO + tm + HALO, :] = jnp.where(last, 0.0, dcnext_ref[:, lo:hi])
        _dwconv(dpad_ref, wf_ref, dc0_ref, NSTR, tm, KW, rb)

        def strip(cs, carry):
            for r0 in range(0, tm, rb):
                d = dpad_ref[cs, HALO + r0:HALO + r0 + rb, :]
                for k in range(KW):
                    r = r0 + HALO - PADK + k
                    prod = d * pad_ref[cs, r:r + rb, :]
                    dwacc_ref[cs, k * 8:(k + 1) * 8, :] += jnp.sum(prod.reshape(rb // 8, 8, LANES), axis=0)
            return carry

        lax.fori_loop(0, NSTR, strip, 0)

        for cs in range(NSTR):
            lo, hi = cs * LANES, (cs + 1) * LANES
            av = pa_ref[:, lo:hi].astype(F32)
            sg = _sig(pa_ref[:, D + lo:D + hi].astype(F32))
            dc0 = dc0_ref[cs]
            dp_ref[:, lo:hi] = (dc0 * sg).astype(BF16)
            dp_ref[:, D + lo:D + hi] = (dc0 * av * (sg * (1.0 - sg))).astype(BF16)

        @pl.when(i == n - 1)
        def _():
            for cs in range(NSTR):
                dw_ref[cs] = jnp.sum(dwacc_ref[cs].reshape(32, 8, LANES), axis=1)

    row = lambda i: (i, 0)
    body, dep_spec, dep_arg = _after(body, 8, dep)
    return pl.pallas_call(
        body, name="conv_bwd", grid=(n,),
        in_specs=[pl.BlockSpec((tm, D), row), pl.BlockSpec((HALO, D), prev), pl.BlockSpec((HALO, D), nxt),
                  pl.BlockSpec((tm, 2 * D), row), pl.BlockSpec((HALO, 2 * D), prev), pl.BlockSpec((HALO, 2 * D), nxt),
                  _res((NSTR, 32, LANES)), pl.BlockSpec(memory_space=pl.ANY)] + dep_spec,
        out_specs=[pl.BlockSpec((tm, 2 * D), row),
                   pl.BlockSpec((NSTR, 32, LANES), lambda i: (0, 0, 0))],
        out_shape=[_sds(dp3.shape, BF16), _sds((NSTR, 32, LANES), F32)],
        scratch_shapes=[pltpu.VMEM((NSTR, tm + 2 * HALO, LANES), F32), pltpu.VMEM((NSTR, tm + 2 * HALO, LANES), F32),
                        pltpu.VMEM((NSTR, tm, LANES), F32), pltpu.VMEM((NSTR, 32 * 8, LANES), F32)],
        input_output_aliases={7: 0},
        compiler_params=_cparams(("arbitrary",)),
    )(dc1, dc1, dc1, proj, proj, proj, conv_wf_s, dp3, *dep_arg)


def _mix_in_bwd(dx1, dp3, x2d, g_mix, w_in_g, layer, tm, dep=None):
    T, D = x2d.shape
    CS = w_in_g.shape[2]
    CN = CS // 3

    def body(dx1_ref, dp_ref, x_ref, g_ref, w_ref, dx_ref, dg_ref):
        i = pl.program_id(0)

        @pl.when(i == 0)
        def _():
            dg_ref[...] = jnp.zeros_like(dg_ref)

        dh = jnp.zeros((tm, D), F32)
        for j in range(12):
            dh = dh + _dot_nt(dp_ref[:, j * CN:(j + 1) * CN], w_ref[j // 3, :, (j % 3) * CN:(j % 3 + 1) * CN])
        x = x_ref[...]
        rstd = lax.rsqrt(jnp.mean(x * x, axis=-1, keepdims=True) + EPS)
        xh = x * rstd
        dg_ref[...] += jnp.sum(dh * xh, axis=0, keepdims=True)
        dxh = dh * g_ref[...]
        dx_ref[...] = dx1_ref[...] + rstd * (dxh - xh * jnp.mean(dxh * xh, axis=-1, keepdims=True))

    row = lambda i: (i, 0)
    body, dep_spec, dep_arg = _after(body, 5, dep)
    return pl.pallas_call(
        body, name="mix_in_bwd", grid=(T // tm,),
        in_specs=[pl.BlockSpec((tm, D), row), pl.BlockSpec((tm, 6 * D), row),
                  pl.BlockSpec((tm, D), row), _res((1, D)), _res((NSHARD, D, CS), lambda i: (0, layer, 0))] + dep_spec,
        out_specs=[pl.BlockSpec((tm, D), row), pl.BlockSpec((1, D), lambda i: (0, 0))],
        out_shape=[_sds((T, D), F32), _sds((1, D), F32)],
        compiler_params=_cparams(("arbitrary",)),
    )(dx1, dp3, x2d, g_mix, w_in_g, *dep_arg)


def _tn_matmul(name, a, a_block, a_map, bs, b_block, b_map, out_shape, out_block, out_map, nj, nt):
    kk = [d for d in a_block if d is not None][-1]
    nn = [d for d in b_block if d is not None][-1]
    nb = len(bs)
    a_list = a if isinstance(a, (list, tuple)) else [a]
    na = len(a_list)

    def body(*refs):
        a_refs, b_refs = refs[:na], refs[na:na + nb]
        o_refs, acc_refs = refs[-2 * nb:-nb], refs[-nb:]
        t = pl.program_id(1)

        @pl.when(t == 0)
        def _():
            for acc_ref in acc_refs:
                acc_ref[...] = jnp.zeros_like(acc_ref)

        a_ts = [a_ref[...].astype(BF16) for a_ref in a_refs]
        for i, (b_ref, acc_ref) in enumerate(zip(b_refs, acc_refs)):
            acc_ref[...] += _dot_tn(a_ts[i % na], b_ref[...].astype(BF16))

        @pl.when(t == nt - 1)
        def _():
            for o_ref, acc_ref in zip(o_refs, acc_refs):
                o_ref[...] = acc_ref[...].astype(o_ref.dtype)

    return pl.pallas_call(
        body, name=name, grid=(nj, nt),
        in_specs=[pl.BlockSpec(a_block, a_map)] * na + [pl.BlockSpec(b_block, b_map)] * nb,
        out_specs=[pl.BlockSpec(out_block, out_map)] * nb, out_shape=[_sds(out_shape, BF16)] * nb,
        scratch_shapes=[pltpu.VMEM((kk, nn), F32)] * nb,
        compiler_params=_cparams(("parallel", "arbitrary")),
    )(*a_list, *bs)


def _place_shard(name, w, pos, dtype, tr, dep=None):
    R, C = w.shape

    def body(pos_ref, w_ref, o_ref):
        del pos_ref
        o_ref[...] = w_ref[...].astype(dtype)

    body, dep_spec, dep_arg = _after(body, 2, dep)
    grid_spec = pltpu.PrefetchScalarGridSpec(
        num_scalar_prefetch=1, grid=(R // tr,),
        in_specs=[pl.BlockSpec((tr, C), lambda r, pos: (r, 0))] + dep_spec,
        out_specs=pl.BlockSpec((None, tr, C), lambda r, pos: (pos[1], r, 0)))
    return pl.pallas_call(body, name=name, grid_spec=grid_spec, out_shape=_sds((NSHARD, R, C), dtype),
                          compiler_params=_cparams(("parallel",)))(pos, w, *dep_arg)


def _add_halves(name, g, rbuf, pos, tr):
    NS, _, H, C = g.shape

    def body(pos_ref, g_ref, r_ref, o_ref):
        del pos_ref
        o_ref[...] = (g_ref[...].astype(F32) + r_ref[...].astype(F32)).astype(BF16)

    grid_spec = pltpu.PrefetchScalarGridSpec(
        num_scalar_prefetch=1, grid=(NS, H // tr),
        in_specs=[pl.BlockSpec((None, None, tr, C), lambda s, r, pos: (s, pos[0], r, 0)),
                  pl.BlockSpec((None, tr, C), lambda s, r, pos: (s, r, 0))],
        out_specs=pl.BlockSpec((None, tr, C), lambda s, r, pos: (s, r, 0)))
    return pl.pallas_call(body, name=name, grid_spec=grid_spec, out_shape=_sds((NS, H, C), BF16),
                          compiler_params=_cparams(("parallel", "parallel")))(pos, g, rbuf)


def _add_shards(name, p, rbuf, pos, tr, layer, n_layers, prev):
    _, H, C = p.shape

    def body(pos_ref, p_ref, r_ref, *rest):
        del pos_ref
        o_ref = rest[-1]
        acc = p_ref[...].astype(F32)
        for j in range(3):
            acc = acc + r_ref[j].astype(F32)
        o_ref[...] = acc

    in_specs = [pl.BlockSpec((None, tr, C), lambda r, pos: (pos[1], r, 0)),
                pl.BlockSpec((3, tr, C), lambda r, pos: (0, r, 0))]
    args = [pos, p, rbuf]
    aliases = {}
    if prev is not None:
        in_specs.append(pl.BlockSpec(memory_space=pl.ANY))
        args.append(prev)
        aliases = {3: 0}
    grid_spec = pltpu.PrefetchScalarGridSpec(
        num_scalar_prefetch=1, grid=(H // tr,), in_specs=in_specs,
        out_specs=pl.BlockSpec((None, None, tr, C), lambda r, pos: (layer, pos[0], r, 0)))
    return pl.pallas_call(body, name=name, grid_spec=grid_spec, out_shape=_sds((n_layers, 2, H, C), F32),
                          input_output_aliases=aliases, compiler_params=_cparams(("parallel",)))(*args)


def _sum_slots(own, land, me, tr):
    NS8, R, C = land.shape

    def body(me_ref, own_ref, l_ref, o_ref):
        acc = None
        for j in range(NS8):
            term = jnp.where(me_ref[0] == j, own_ref[...], l_ref[j])
            acc = term if acc is None else acc + term
        o_ref[...] = acc

    grid_spec = pltpu.PrefetchScalarGridSpec(
        num_scalar_prefetch=1, grid=(R // tr,),
        in_specs=[pl.BlockSpec((tr, C), lambda i, me: (i, 0)), pl.BlockSpec((NS8, tr, C), lambda i, me: (0, i, 0))],
        out_specs=pl.BlockSpec((tr, C), lambda i, me: (i, 0)))
    return pl.pallas_call(body, name="sum_slots", grid_spec=grid_spec, out_shape=_sds((R, C), F32),
                          compiler_params=_cparams(("parallel",)))(me, own, land)


def _adamw_update(w_ref, g_ref, m_ref, v_ref, d_ref, mo_ref, vo_ref):
    g_ = g_ref[...]
    m_ = ADAM_B1 * m_ref[...] + (1.0 - ADAM_B1) * g_
    v_ = ADAM_B2 * v_ref[...] + (1.0 - ADAM_B2) * (g_ * g_)
    mo_ref[...] = m_
    vo_ref[...] = v_
    m_hat = m_ / (1.0 - ADAM_B1 ** ADAM_STEP)
    v_hat = v_ / (1.0 - ADAM_B2 ** ADAM_STEP)
    d_ref[...] = -ADAM_LR * (m_hat / (jnp.sqrt(v_hat) + ADAM_EPS) + ADAM_WD * w_ref[...])


def _adamw(name, w, g, m, v, tr, emit_g):
    R, C = w.shape
    n_out = 4 if emit_g else 3

    def body(w_ref, g_ref, m_ref, v_ref, d_ref, mo_ref, vo_ref, *go_ref):
        _adamw_update(w_ref, g_ref, m_ref, v_ref, d_ref, mo_ref, vo_ref)
        if emit_g:
            go_ref[0][...] = g_ref[...]

    spec = pl.BlockSpec((tr, C), lambda i: (i, 0))
    return pl.pallas_call(
        body, name=name, grid=(R // tr,), in_specs=[spec] * 4, out_specs=[spec] * n_out,
        out_shape=[_sds((R, C), F32)] * n_out, compiler_params=_cparams(("parallel",)))(w, g, m, v)


def _adamw_many(ws, gs, ms, vs):
    n = len(ws)

    def body(*refs):
        ins, outs = refs[:4 * n], refs[4 * n:]
        for k in range(n):
            _adamw_update(ins[k], ins[n + k], ins[2 * n + k], ins[3 * n + k], outs[k], outs[n + k], outs[2 * n + k])

    vmem = pl.BlockSpec(memory_space=pltpu.VMEM)
    res = pl.pallas_call(
        body, name="adamw_small", in_specs=[vmem] * (4 * n), out_specs=[vmem] * (3 * n),
        out_shape=[_sds(w.shape, F32) for w in ws] * 3,
        compiler_params=pltpu.CompilerParams(vmem_limit_bytes=VMEM_LIMIT))(*ws, *gs, *ms, *vs)
    return list(res[:n]), list(res[n:2 * n]), list(res[2 * n:])


def _row_tile(rows, cap):
    best = rows
    for t in range(8, min(rows, cap) + 1, 8):
        if rows % t == 0:
            best = t
    return best


HBM_SPEC = pl.BlockSpec(memory_space=pltpu.HBM)
SEM_SPEC = pl.BlockSpec(memory_space=pltpu.SEMAPHORE)
DATAFLOW = pltpu.SideEffectType.DATAFLOW_SIDE_EFFECTING
DMA_SEM = pltpu.SemaphoreType.DMA


def _hbm(a):
    return pltpu.with_memory_space_constraint(a, pltpu.HBM)


def _mesh_pos():
    return lax.axis_index("x"), lax.axis_index("y"), lax.axis_index("c")


def _other_chips(x, y):
    return [(1 - x, y), (x, 1 - y), (1 - x, 1 - y)]


def _half_rows(buf, shard, core):
    h = buf.shape[1] // 2
    return buf.at[shard, pl.ds(core * h, h), :]


def _ici_copy(buf, j, send, recv, landing):
    x, y, c = _mesh_pos()
    px, py = _other_chips(x, y)[j]
    part = _half_rows(buf, 2 * px + py if landing else 2 * x + y, c)
    return pltpu.make_async_remote_copy(src_ref=part, dst_ref=part, send_sem=send, recv_sem=recv,
                                        device_id=(px, py, c), device_id_type=MESH_ID)


def _sibling_copy(buf, j, send, recv, landing):
    x, y, c = _mesh_pos()
    px, py = _other_chips(x, y)[j]
    part = _half_rows(buf, 2 * px + py, 1 - c if landing else c)
    return pltpu.make_async_remote_copy(src_ref=part, dst_ref=part, send_sem=send, recv_sem=recv,
                                        device_id=(x, y, 1 - c), device_id_type=MESH_ID)


def _forward_sibling(name, bufs, with_ici):
    n = len(bufs)

    def body(*refs):
        ins = refs[:n]
        send_ici, recv_ici, send_d2d, recv_d2d = refs[2 * n:]
        sends = []
        if with_ici:
            for i in range(n):
                for j in range(3):
                    cp = _ici_copy(ins[i], j, send_ici.at[i, j], recv_ici.at[i, j], False)
                    cp.start()
                    sends.append(cp)
        for i in range(n):
            for j in range(3):
                if with_ici:
                    _ici_copy(ins[i], j, send_ici.at[i, j], recv_ici.at[i, j], True).wait_recv()
                cp = _sibling_copy(ins[i], j, send_d2d.at[i, j], recv_d2d.at[i, j], False)
                cp.start()
                sends.append(cp)
        for i in range(n):
            for j in range(3):
                _sibling_copy(ins[i], j, send_d2d.at[i, j], recv_d2d.at[i, j], True).wait_recv()
        for cp in sends:
            cp.wait_send()

    return pl.pallas_call(
        body, name=name, in_specs=[HBM_SPEC] * n, out_specs=[HBM_SPEC] * n,
        out_shape=[_sds(b.shape, b.dtype) for b in bufs],
        scratch_shapes=[DMA_SEM((n, 3))] * 4, input_output_aliases={i: i for i in range(n)},
    )(*bufs)


def _gather_start(name, groups):
    flat = [b for g in groups for b in g]
    n, ng = len(flat), len(groups)

    def body(*refs):
        ins, sems, token = refs[:n], refs[n:n + 2 * ng], refs[-1]
        k = 0
        for gi, g in enumerate(groups):
            for a in range(len(g)):
                for j in range(3):
                    _ici_copy(ins[k], j, sems[2 * gi], sems[2 * gi + 1], False).start()
                k += 1
        token[...] = jnp.zeros_like(token)

    res = pl.pallas_call(
        body, name=name, in_specs=[HBM_SPEC] * n,
        out_specs=[SEM_SPEC] * (2 * ng) + [HBM_SPEC] * n + [pl.BlockSpec(memory_space=pltpu.VMEM)],
        out_shape=[DMA_SEM(()) for g in groups for _ in range(2)]
        + [pltpu.HBM(b.shape, b.dtype) for b in flat] + [_sds((8, LANES), F32)],
        input_output_aliases={i: 2 * ng + i for i in range(n)},
        compiler_params=pltpu.CompilerParams(has_side_effects=DATAFLOW),
    )(*[_hbm(b) for b in flat])
    sems = [(res[2 * gi], res[2 * gi + 1]) for gi in range(ng)]
    thru, k = [], 2 * ng
    for g in groups:
        thru.append(list(res[k:k + len(g)]))
        k += len(g)
    return sems, thru, res[-1]


def _gather_wait(name, bufs, sems, after):
    n = len(bufs)

    def body(*refs):
        ins, send, recv = refs[:n], refs[n], refs[n + 1]
        for a in range(n):
            for j in range(3):
                _ici_copy(ins[a], j, send, recv, False).wait_send()
                _ici_copy(ins[a], j, send, recv, True).wait_recv()

    return pl.pallas_call(
        body, name=name, in_specs=[HBM_SPEC] * n + [SEM_SPEC, SEM_SPEC, pl.BlockSpec(memory_space=pl.ANY)],
        out_specs=[HBM_SPEC] * n, out_shape=[pltpu.HBM(b.shape, b.dtype) for b in bufs],
        input_output_aliases={i: i for i in range(n)},
        compiler_params=pltpu.CompilerParams(has_side_effects=DATAFLOW),
    )(*bufs, sems[0], sems[1], after)


def _send_sibling_halves(name, arrs):
    n = len(arrs)

    def body(*refs):
        ins, outs = refs[:n], refs[n:2 * n]
        send, recv = refs[2 * n:]
        x, y, c = _mesh_pos()
        cps = []
        for i in range(n):
            cp = pltpu.make_async_remote_copy(
                src_ref=ins[i].at[:, 1 - c], dst_ref=outs[i],
                send_sem=send.at[i], recv_sem=recv.at[i], device_id=(x, y, 1 - c), device_id_type=MESH_ID)
            cp.start()
            cps.append(cp)
        for cp in cps:
            cp.wait()

    return pl.pallas_call(
        body, name=name, in_specs=[HBM_SPEC] * n, out_specs=[HBM_SPEC] * n,
        out_shape=[_sds((a.shape[0],) + a.shape[2:], a.dtype) for a in arrs],
        scratch_shapes=[DMA_SEM((n,)), DMA_SEM((n,))],
    )(*arrs)


def _chip_copy(p, land, j, send, recv):
    x, y, c = _mesh_pos()
    px, py = _other_chips(x, y)[j]
    return pltpu.make_async_remote_copy(src_ref=p.at[2 * px + py], dst_ref=land.at[j], send_sem=send, recv_sem=recv,
                                        device_id=(px, py, c), device_id_type=MESH_ID)


def _chip_send_start(name, ps):
    n = len(ps)
    lands = [lax.empty((3,) + p.shape[1:], p.dtype) for p in ps]

    def body(*refs):
        ins, lnd, send, recv, token = refs[:n], refs[n:2 * n], refs[2 * n], refs[2 * n + 1], refs[-1]
        for i in range(n):
            for j in range(3):
                _chip_copy(ins[i], lnd[i], j, send, recv).start()
        token[...] = jnp.zeros_like(token)

    res = pl.pallas_call(
        body, name=name, in_specs=[HBM_SPEC] * (2 * n),
        out_specs=[SEM_SPEC, SEM_SPEC] + [HBM_SPEC] * (2 * n) + [pl.BlockSpec(memory_space=pltpu.VMEM)],
        out_shape=[DMA_SEM(()), DMA_SEM(())] + [pltpu.HBM(a.shape, a.dtype) for a in ps + lands]
        + [_sds((8, LANES), F32)],
        input_output_aliases={i: 2 + i for i in range(2 * n)},
        compiler_params=pltpu.CompilerParams(has_side_effects=DATAFLOW),
    )(*[_hbm(a) for a in ps + lands])
    return (res[0], res[1]), list(res[2:2 + n]), list(res[2 + n:2 + 2 * n]), res[-1]


def _chip_send_wait(name, ps, lands, sems, after):
    n = len(ps)

    def body(*refs):
        ins, lnd, send, recv = refs[:n], refs[n:2 * n], refs[2 * n], refs[2 * n + 1]
        for i in range(n):
            for j in range(3):
                cp = _chip_copy(ins[i], lnd[i], j, send, recv)
                cp.wait_send()
                cp.wait_recv()

    res = pl.pallas_call(
        body, name=name, in_specs=[HBM_SPEC] * (2 * n) + [SEM_SPEC, SEM_SPEC, pl.BlockSpec(memory_space=pl.ANY)],
        out_specs=[HBM_SPEC] * (2 * n), out_shape=[pltpu.HBM(a.shape, a.dtype) for a in ps + lands],
        input_output_aliases={i: i for i in range(2 * n)},
        compiler_params=pltpu.CompilerParams(has_side_effects=DATAFLOW),
    )(*ps, *lands, sems[0], sems[1], after)
    return list(res[:n]), list(res[n:])


def _join_halves(arrs):
    n = len(arrs)

    def body(*refs):
        bufs = refs[n:2 * n]
        send, recv = refs[2 * n:]
        x, y, c = _mesh_pos()
        cps = []
        for i in range(n):
            mine = bufs[i].at[:, c]
            cp = pltpu.make_async_remote_copy(
                src_ref=mine, dst_ref=mine, send_sem=send.at[i], recv_sem=recv.at[i],
                device_id=(x, y, 1 - c), device_id_type=MESH_ID)
            cp.start()
            cps.append(cp)
        for i, cp in enumerate(cps):
            theirs = bufs[i].at[:, 1 - c]
            cp.wait_send()
            pltpu.make_async_remote_copy(
                src_ref=theirs, dst_ref=theirs, send_sem=send.at[i], recv_sem=recv.at[i],
                device_id=(x, y, 1 - c), device_id_type=MESH_ID).wait_recv()

    return pl.pallas_call(
        body, name="join_halves", in_specs=[HBM_SPEC] * n, out_specs=[HBM_SPEC] * n,
        out_shape=[_sds(a.shape, a.dtype) for a in arrs],
        scratch_shapes=[DMA_SEM((n,)), DMA_SEM((n,))], input_output_aliases={i: i for i in range(n)},
    )(*arrs)


def _peer_copy(buf, land, k, send, recv, landing):
    x, y, c = _mesh_pos()
    px, py, pc = x ^ ((k >> 2) & 1), y ^ ((k >> 1) & 1), c ^ (k & 1)
    slot = 4 * px + 2 * py + pc if landing else 4 * x + 2 * y + c
    return pltpu.make_async_remote_copy(src_ref=buf, dst_ref=land.at[slot], send_sem=send, recv_sem=recv,
                                        device_id=(px, py, pc), device_id_type=MESH_ID)


def _exchange_all(buf):
    def body(in_ref, out_ref, send, recv):
        cps = [_peer_copy(in_ref, out_ref, k, send.at[k - 1], recv.at[k - 1], False) for k in range(1, 8)]
        for cp in cps:
            cp.start()
        for k in range(1, 8):
            cps[k - 1].wait_send()
            _peer_copy(in_ref, out_ref, k, send.at[k - 1], recv.at[k - 1], True).wait_recv()

    return pl.pallas_call(
        body, name="exchange_all", in_specs=[HBM_SPEC], out_specs=HBM_SPEC,
        out_shape=_sds((8,) + buf.shape, buf.dtype), scratch_shapes=[DMA_SEM((7,)), DMA_SEM((7,))],
    )(buf)


def _exchange_start(name, buf):
    land = lax.empty((8,) + buf.shape, buf.dtype)

    def body(in_ref, land_ref, send, recv, in_thru, land_thru, token):
        for k in range(1, 8):
            _peer_copy(in_ref, land_ref, k, send, recv, False).start()
        token[...] = jnp.zeros_like(token)

    res = pl.pallas_call(
        body, name=name, in_specs=[HBM_SPEC] * 2,
        out_specs=[SEM_SPEC, SEM_SPEC, HBM_SPEC, HBM_SPEC, pl.BlockSpec(memory_space=pltpu.VMEM)],
        out_shape=[DMA_SEM(()), DMA_SEM(()), pltpu.HBM(buf.shape, buf.dtype), pltpu.HBM(land.shape, land.dtype),
                   _sds((8, LANES), F32)],
        input_output_aliases={0: 2, 1: 3}, compiler_params=pltpu.CompilerParams(has_side_effects=DATAFLOW),
    )(_hbm(buf), _hbm(land))
    return (res[0], res[1]), res[2], res[3], res[4]


def _exchange_wait(name, buf, land, sems, after):
    def body(in_ref, land_ref, send, recv, after_ref, in_thru, land_thru):
        for k in range(1, 8):
            _peer_copy(in_ref, land_ref, k, send, recv, False).wait_send()
            _peer_copy(in_ref, land_ref, k, send, recv, True).wait_recv()

    res = pl.pallas_call(
        body, name=name, in_specs=[HBM_SPEC, HBM_SPEC, SEM_SPEC, SEM_SPEC, pl.BlockSpec(memory_space=pl.ANY)],
        out_specs=[HBM_SPEC, HBM_SPEC], out_shape=[pltpu.HBM(buf.shape, buf.dtype), pltpu.HBM(land.shape, land.dtype)],
        input_output_aliases={0: 0, 1: 1}, compiler_params=pltpu.CompilerParams(has_side_effects=DATAFLOW),
    )(buf, land, sems[0], sems[1], after)
    return res[0], res[1]


def _pad_to(a, axis, size):
    pad = [(0, 0)] * a.ndim
    pad[axis] = (0, size - a.shape[axis])
    return jnp.pad(a, pad)


def _strips(w):
    k, d = w.shape
    return _pad_to(w, 0, 32).reshape(32, d // LANES, LANES).transpose(1, 0, 2)


def kernel(x, norm_mix, w_in, gate_bias, conv_w, conv_b, conv_ln_g, conv_ln_b, w_conv_out, sgu_ln_g, sgu_ln_b, w_spatial, b_spatial, w_sgu_out, w_o, norm_ffn, w_ffn_gate, w_ffn_up, w_ffn_down, norm_final, loss_target, m_norm_mix, m_w_in, m_gate_bias, m_conv_w, m_conv_b, m_conv_ln_g, m_conv_ln_b, m_w_conv_out, m_sgu_ln_g, m_sgu_ln_b, m_w_spatial, m_b_spatial, m_w_sgu_out, m_w_o, m_norm_ffn, m_w_ffn_gate, m_w_ffn_up, m_w_ffn_down, m_norm_final, v_norm_mix, v_w_in, v_gate_bias, v_conv_w, v_conv_b, v_conv_ln_g, v_conv_ln_b, v_w_conv_out, v_sgu_ln_g, v_sgu_ln_b, v_w_spatial, v_b_spatial, v_w_sgu_out, v_w_o, v_norm_ffn, v_w_ffn_gate, v_w_ffn_up, v_w_ffn_down, v_norm_final):
    BL, S, D = x.shape
    T = BL * S
    L = w_in.shape[0]
    CS = w_in.shape[2]
    CN = CS // 3
    DQ = D // NSHARD
    FS = w_ffn_gate.shape[2]
    F = NSHARD * FS
    G, CH = w_spatial.shape[1], w_spatial.shape[2]
    KW = conv_w.shape[1]
    CQ = conv_w.shape[3]
    NSTR = D // LANES
    tm = min(512, S // 2)
    tm2 = max(tm // 2, CH)
    rb = min(64, tm)
    mx, my, mc = _mesh_pos()
    pos = jnp.stack([mc, 2 * mx + my]).astype(jnp.int32)

    def placed(name, w, dtype=BF16, dep=None):
        return _place_shard("place_" + name, w, pos, dtype, _row_tile(w.shape[0], 256), dep)

    w_in0 = placed("w_in", w_in[0])
    cw_p = placed("conv_w", _pad_to(conv_w.reshape(L, KW, CQ), 1, 32).reshape(L * 32, CQ), F32)
    fsems, fflying, ftoken = _gather_start("gather_start_first", [[w_in0, cw_p]])
    wts = []
    for l in range(L):
        w_sq = jnp.concatenate([w_conv_out[l], w_sgu_out[l], w_o[l]], axis=0)
        wts.append(dict(w_in=placed("w_in", w_in[l], dep=ftoken) if l else None, w_sq=placed("w_sq", w_sq, dep=ftoken),
                        wg=placed("w_gate", w_ffn_gate[l].T, dep=ftoken), wu=placed("w_up", w_ffn_up[l].T, dep=ftoken),
                        wd=placed("w_down", w_ffn_down[l], dep=ftoken)))
    ffn_keys = ["wg", "wu", "wd"]
    order = [[(0, "w_sq")], [(0, k) for k in ffn_keys]]
    order += [[(l, k) for k in ["w_in", "w_sq"] + ffn_keys] for l in range(1, L)]
    gsems, flying, token = _gather_start("gather_start", [[wts[l][k] for l, k in grp] for grp in order])
    first = _gather_wait("gather_wait_first", fflying[0], fsems[0], token)
    wts[0]["w_in"], cw_g = _forward_sibling("gather_first", first, False)
    conv_w_full = cw_g.reshape(NSHARD, L, 32, CQ).transpose(1, 2, 0, 3).reshape(L, 32, D)[:, :KW]

    def land(gi, after):
        bufs = _gather_wait("gather_wait_%d" % gi, flying[gi], gsems[gi], after)
        bufs = _forward_sibling("gather_forward_%d" % gi, bufs, False)
        for (l, k), b in zip(order[gi], bufs):
            wts[l][k] = b

    x2d = x.reshape(T, D)
    tgt = loss_target.reshape(T, D)
    row = lambda a, l: a[l].reshape(1, -1)

    saved = []
    xc = x2d
    for l in range(L):
        ws_b = w_spatial[l].astype(BF16)
        bs_b = jnp.repeat(b_spatial[l].T, D // G, axis=1)
        cw_s = _strips(conv_w_full[l])
        h, proj = _mix_in_fwd(xc, row(norm_mix, l), wts[l]["w_in"], 0, tm, token if l == 0 else None)
        if l == 0:
            land(0, h)
        c1h, rstd_c, c3, ya = _conv_fwd(proj, cw_s, row(conv_b, l), row(conv_ln_g, l), row(conv_ln_b, l),
                                        wts[l]["w_sq"], 0, S, tm, rb)
        if l == 0:
            land(1, ya)
        mixed, gated, yb, merged, x1 = _sgu_merge_fwd(proj, ya, xc, row(sgu_ln_g, l), row(sgu_ln_b, l), ws_b, bs_b,
                                                      row(gate_bias, l), wts[l]["w_sq"], 0, tm2)
        ffn_w = [wts[l][k].reshape(F, D) for k in ffn_keys]
        h2, gt, up, act, x2 = _ffn_fwd(x1, row(norm_ffn, l), *ffn_w, tm2)
        if l + 1 < L:
            land(l + 2, x2)
        saved.append(dict(x=xc, h=h, proj=proj, c1h=c1h, rstd_c=rstd_c, c3=c3, ya=ya, mixed=mixed, gated=gated,
                          yb=yb, merged=merged, x1=x1, h2=h2, gt=gt, up=up, act=act, ws_b=ws_b, cw=conv_w_full[l]))
        xc = x2

    dx, loss_part, d_norm_final = _loss_head(xc, norm_final.reshape(1, D), tgt, tm)
    loss = lax.psum(loss_part[0, 0], ("x", "y", "c"))

    g_acc = {}

    def reduce_start(tag, layer, named):
        arrs = [g.reshape(NSHARD, 2, g.shape[1] // 2, g.shape[2]) for _, g in named]
        from_sib = _send_sibling_halves("sibling_" + tag, arrs)
        ps = [_add_halves("presum_" + nm, a, r, pos, _row_tile(a.shape[2], 256))
              for (nm, _), a, r in zip(named, arrs, from_sib)]
        sems, ps, lands, tok = _chip_send_start("chip_send_start_" + tag, ps)
        return dict(tag=tag, layer=layer, names=[nm for nm, _ in named], ps=ps, lands=lands, sems=sems), tok

    def reduce_finish(pend, after):
        ps, lands = _chip_send_wait("chip_send_wait_" + pend["tag"], pend["ps"], pend["lands"], pend["sems"], after)
        for nm, p, r in zip(pend["names"], ps, lands):
            g_acc[nm] = _add_shards("shardsum_" + nm, p, r, pos, _row_tile(p.shape[1], 256), pend["layer"], L,
                                    g_acc.get(nm))

    me_idx = (4 * mx + 2 * my + mc).astype(jnp.int32).reshape(1)
    exchanges = []

    def pack_rows(pieces):
        packed = jnp.concatenate(pieces, axis=0)
        return _pad_to(packed, 0, -(-packed.shape[0] // 8) * 8)

    def unpack_rows(summed, pieces):
        out, off = [], 0
        for p in pieces:
            out.append(summed[off:off + p.shape[0]])
            off += p.shape[0]
        return out

    def small_start(tag, pieces):
        sems, buf, land, token = _exchange_start("exchange_start_" + tag, pack_rows(pieces))
        return dict(tag=tag, pieces=pieces, buf=buf, land=land, sems=sems, token=token)

    def small_finish(st, after):
        buf, land = _exchange_wait("exchange_wait_" + st["tag"], st["buf"], st["land"], st["sems"], after)
        return unpack_rows(_sum_slots(buf, land, me_idx, _row_tile(buf.shape[0], 256)), st["pieces"])

    small = [None] * L
    tt = min(1024, T // 2)
    nt = T // tt
    pending, tok = None, None
    for l in reversed(range(L)):
        sv, wt = saved[l], wts[l]
        ffn_w = [wt[k].reshape(F, D) for k in ffn_keys]
        dx1, dgt, dup, d_norm_ffn = _ffn_bwd(dx, sv["x1"], sv["gt"], sv["up"], row(norm_ffn, l), *ffn_w, tm2, tok)
        tn_a = ((tt, F // 2), lambda j, t: (t, j))
        tn_b = ((tt, D), lambda j, t: (t, 0))
        tn_o = ((F, D), (F // 2, D), lambda j, t: (j, 0), 2, nt)
        g_g, = _tn_matmul("grad_w_gate", dgt, *tn_a, [sv["h2"]], *tn_b, *tn_o)
        g_u, = _tn_matmul("grad_w_up", dup, *tn_a, [sv["h2"]], *tn_b, *tn_o)
        g_d, = _tn_matmul("grad_w_down", sv["act"], *tn_a, [dx], *tn_b, *tn_o)
        if pending is not None:
            reduce_finish(pending, g_d)
        ffn_pend, tok = reduce_start("ffn%d" % l, l, [
            ("w_ffn_gate", g_g.reshape(NSHARD, FS, D)), ("w_ffn_up", g_u.reshape(NSHARD, FS, D)),
            ("w_ffn_down", g_d.reshape(NSHARD, FS, D))])
        wst_b = jnp.swapaxes(sv["ws_b"], 1, 2)
        dya, dyb, dp3, d_gate_bias, d_sgu_g, d_sgu_b, d_bs, d_ws = _merge_sgu_bwd(
            dx1, sv["proj"], sv["ya"], sv["yb"], sv["mixed"], row(sgu_ln_g, l), row(sgu_ln_b, l), sv["ws_b"], wst_b,
            row(gate_bias, l), wt["w_sq"], 0, tm2, tok)
        sq_args = ((tt, D), lambda j, t: (t, 0))
        sq_out = ((D, D), (D, D), lambda j, t: (0, 0), 1, nt)
        g_o, = _tn_matmul("grad_w_o", sv["merged"], *sq_args, [dx1], *sq_args, *sq_out)
        g_so, = _tn_matmul("grad_w_sgu_out", sv["gated"], *sq_args, [dyb], *sq_args, *sq_out)
        g_co, = _tn_matmul("grad_w_conv_out", sv["c3"], *sq_args, [dya], *sq_args, *sq_out)
        dc1, d_cln_g, d_cln_b, d_conv_b = _conv_ln_bwd(dya, sv["c1h"], sv["rstd_c"], row(conv_ln_g, l),
                                                       row(conv_ln_b, l), wt["w_sq"], 0, tm)
        small[l] = [None, d_gate_bias.reshape(2, D), None, d_conv_b, d_cln_g, d_cln_b, d_sgu_g, d_sgu_b,
                    d_ws.reshape(G * CH * CH // D, D), d_bs.reshape(G * CH // D, D), d_norm_ffn]
        tok_x = None
        if l == 0:
            early = [k for k in range(len(small[0])) if small[0][k] is not None]
            exchanges.append((small_start("early0", [small[0][k] for k in early]), [(0, k) for k in early]))
            tok_x = exchanges[-1][0]["token"]
        dp3, d_cw_s = _conv_bwd(dc1, sv["proj"], dp3, _strips(sv["cw"][::-1]), S, tm, rb, tok_x)
        g_in, = _tn_matmul("grad_w_in", sv["h"], (tt, D), lambda j, t: (t, 0), [dp3], (tt, CS), lambda j, t: (t, j),
                           (NSHARD, D, CS), (None, D, CS), lambda j, t: (j, 0, 0), NSHARD, nt)
        reduce_finish(ffn_pend, g_in)
        pending, tok = reduce_start("mix%d" % l, l, [
            ("w_in", g_in), ("w_conv_out", g_co.reshape(NSHARD, DQ, D)), ("w_sgu_out", g_so.reshape(NSHARD, DQ, D)),
            ("w_o", g_o.reshape(NSHARD, DQ, D))])
        dx, d_norm_mix = _mix_in_bwd(dx1, dp3, sv["x"], row(norm_mix, l), wt["w_in"], 0, tm, tok)
        small[l][0] = d_norm_mix
        small[l][2] = d_cw_s.transpose(1, 0, 2).reshape(32, D)
        if l > 0:
            exchanges.append((small_start("layer%d" % l, small[l]), [(l, k) for k in range(len(small[l]))]))
            tok = [tok, exchanges[-1][0]["token"]]
    reduce_finish(pending, dx)
    grad_x = dx.reshape(BL, S, D)

    names = ["w_in", "w_conv_out", "w_sgu_out", "w_o", "w_ffn_gate", "w_ffn_up", "w_ffn_down"]
    g_full = _join_halves([g_acc[nm] for nm in names])
    g_w_in, g_w_co, g_w_so, g_w_o, g_w_g, g_w_u, g_w_d = [g.reshape(L, 2 * g.shape[2], g.shape[3]) for g in g_full]
    g_w_g = jnp.swapaxes(g_w_g, 1, 2)
    g_w_u = jnp.swapaxes(g_w_u, 1, 2)

    late = [small[0][0], small[0][2], d_norm_final]
    packed = pack_rows(late)
    summed = _sum_slots(packed, _exchange_all(packed), me_idx, _row_tile(packed.shape[0], 256))
    sg = [[None] * len(small[l]) for l in range(L)]
    sg[0][0], sg[0][2], g_norm_final = unpack_rows(summed, late)
    g_norm_final = g_norm_final[0]
    for st, where in exchanges:
        for (l, k), piece in zip(where, small_finish(st, summed)):
            sg[l][k] = piece

    def per_layer(k, shape):
        return jnp.stack([sg[l][k] for l in range(L)]).reshape(shape)

    g_norm_mix = per_layer(0, (L, D))
    g_gate_bias = per_layer(1, (L, 2 * D))
    g_conv_w_full = jnp.stack([sg[l][2][:KW] for l in range(L)])
    g_conv_w = lax.dynamic_slice_in_dim(g_conv_w_full, (2 * mx + my) * CQ, CQ, axis=2).reshape(L, KW, 1, CQ)
    g_conv_b = per_layer(3, (L, D))
    g_conv_ln_g = per_layer(4, (L, D))
    g_conv_ln_b = per_layer(5, (L, D))
    g_sgu_ln_g = per_layer(6, (L, D))
    g_sgu_ln_b = per_layer(7, (L, D))
    g_w_spatial = per_layer(8, (L, G, CH, CH))
    g_b_spatial = per_layer(9, (L, G, CH))
    g_norm_ffn = per_layer(10, (L, D))

    grads = [g_norm_mix, g_w_in, g_gate_bias, g_conv_w, g_conv_b, g_conv_ln_g, g_conv_ln_b, g_w_co, g_sgu_ln_g,
             g_sgu_ln_b, g_w_spatial, g_b_spatial, g_w_so, g_w_o, g_norm_ffn, g_w_g, g_w_u, g_w_d, g_norm_final]
    weights = [norm_mix, w_in, gate_bias, conv_w, conv_b, conv_ln_g, conv_ln_b, w_conv_out, sgu_ln_g, sgu_ln_b,
               w_spatial, b_spatial, w_sgu_out, w_o, norm_ffn, w_ffn_gate, w_ffn_up, w_ffn_down, norm_final]
    ms = [m_norm_mix, m_w_in, m_gate_bias, m_conv_w, m_conv_b, m_conv_ln_g, m_conv_ln_b, m_w_conv_out, m_sgu_ln_g,
          m_sgu_ln_b, m_w_spatial, m_b_spatial, m_w_sgu_out, m_w_o, m_norm_ffn, m_w_ffn_gate, m_w_ffn_up,
          m_w_ffn_down, m_norm_final]
    vs = [v_norm_mix, v_w_in, v_gate_bias, v_conv_w, v_conv_b, v_conv_ln_g, v_conv_ln_b, v_w_conv_out, v_sgu_ln_g,
          v_sgu_ln_b, v_w_spatial, v_b_spatial, v_w_sgu_out, v_w_o, v_norm_ffn, v_w_ffn_gate, v_w_ffn_up,
          v_w_ffn_down, v_norm_final]

    big_idx = [1, 7, 12, 13, 15, 16, 17]
    transposed = [15, 16]
    deltas, new_m, new_v = [None] * 19, [None] * 19, [None] * 19
    for k in big_idx:
        shp = weights[k].shape
        r2 = (shp[0] * shp[1], shp[2])
        res = _adamw("adamw_" + str(k), weights[k].reshape(r2), grads[k].reshape(r2), ms[k].reshape(r2),
                     vs[k].reshape(r2), _row_tile(r2[0], 256), k not in transposed)
        deltas[k], new_m[k], new_v[k] = [a.reshape(shp) for a in res[:3]]
        if k not in transposed:
            grads[k] = res[3].reshape(shp)
    small_idx = [k for k in range(19) if k not in big_idx]
    pick = lambda arrs: [arrs[k].reshape(1, -1) if arrs[k].ndim == 1 else arrs[k] for k in small_idx]
    d_, m_, v_ = _adamw_many(pick(weights), pick(grads), pick(ms), pick(vs))
    for i, k in enumerate(small_idx):
        shp = weights[k].shape
        deltas[k], new_m[k], new_v[k] = d_[i].reshape(shp), m_[i].reshape(shp), v_[i].reshape(shp)

    return (loss, grad_x, *grads, *deltas, *new_m, *new_v)
```

```python
import functools

import jax
import jax.numpy as jnp
from jax import lax
from jax.experimental import pallas as pl
from jax.experimental.pallas import tpu as pltpu

F32 = jnp.float32
BF16 = jnp.bfloat16
EPS = 1e-6
ADAM_LR = 0.001
ADAM_B1 = 0.9
ADAM_B2 = 0.999
ADAM_EPS = 1e-08
ADAM_WD = 0.01
ADAM_STEP = 10

NSHARD = 4
LANES = 128
HALO = 16
VMEM_LIMIT = 60 * 1024 * 1024
MESH_ID = pl.DeviceIdType.MESH


def _dot(a, b):
    return jnp.dot(a, b, preferred_element_type=F32)


def _dot_nt(a, b):
    return lax.dot_general(a, b, (((1,), (1,)), ((), ())), preferred_element_type=F32)


def _dot_tn(a, b):
    return lax.dot_general(a, b, (((0,), (0,)), ((), ())), preferred_element_type=F32)


def _sig(z):
    return 1.0 / (1.0 + jnp.exp(-z))


def _res(shape, imap=None):
    nd = len(shape)
    if imap is None:
        imap = lambda *_: (0,) * nd
    return pl.BlockSpec(shape, imap, pipeline_mode=pl.Buffered(1))


def _cparams(sem):
    return pltpu.CompilerParams(dimension_semantics=sem, vmem_limit_bytes=VMEM_LIMIT)


def _sds(shape, dtype):
    return jax.ShapeDtypeStruct(shape, dtype)


def _after(body, n_in, dep):
    deps = [] if dep is None else [d for d in (dep if isinstance(dep, (list, tuple)) else [dep]) if d is not None]
    if not deps:
        return body, [], []

    def wrapped(*refs):
        return body(*refs[:n_in], *refs[n_in + len(deps):])

    return wrapped, [pl.BlockSpec(memory_space=pl.ANY)] * len(deps), deps


def _mix_in_fwd(x2d, g_mix, w_in_g, layer, tm, dep=None):
    T, D = x2d.shape
    CS = w_in_g.shape[2]
    CN = CS // 3

    def body(x_ref, g_ref, w_ref, h_ref, p_ref):
        x = x_ref[...]
        rstd = lax.rsqrt(jnp.mean(x * x, axis=-1, keepdims=True) + EPS)
        h = (x * rstd * g_ref[...]).astype(BF16)
        h_ref[...] = h
        for s in range(NSHARD):
            for j in range(3):
                c0 = s * CS + j * CN
                p_ref[:, c0:c0 + CN] = _dot(h, w_ref[s, :, j * CN:(j + 1) * CN]).astype(BF16)

    body, dep_spec, dep_arg = _after(body, 3, dep)
    return pl.pallas_call(
        body, name="mix_in_fwd", grid=(T // tm,),
        in_specs=[pl.BlockSpec((tm, D), lambda i: (i, 0)), _res((1, D)),
                  _res((NSHARD, D, CS), lambda i: (0, layer, 0))] + dep_spec,
        out_specs=[pl.BlockSpec((tm, D), lambda i: (i, 0)), pl.BlockSpec((tm, NSHARD * CS), lambda i: (i, 0))],
        out_shape=[_sds((T, D), BF16), _sds((T, NSHARD * CS), BF16)],
        compiler_params=_cparams(("parallel",)),
    )(x2d, g_mix, w_in_g, *dep_arg)


def _halo_maps(tm, n_rows):
    nb = tm // HALO
    last = n_rows // HALO - 1
    prev = lambda i: (jnp.maximum(i * nb - 1, 0), 0)
    nxt = lambda i: (jnp.minimum((i + 1) * nb, last), 0)
    return prev, nxt


def _dwconv(pad_ref, w_ref, out_ref, n_strips, tm, kw, rb):
    off = HALO - (kw - 1) // 2

    def strip(cs, carry):
        for r0 in range(0, tm, rb):
            acc = jnp.zeros((rb, LANES), F32)
            for k in range(kw):
                r = r0 + off + k
                acc = acc + w_ref[cs, k:k + 1, :] * pad_ref[cs, r:r + rb, :]
            out_ref[cs, r0:r0 + rb, :] = acc
        return carry

    lax.fori_loop(0, n_strips, strip, 0)


def _fill_c0_pad(pad_ref, pa_ref, pprev_ref, pnext_ref, D, tm, first, last):
    for cs in range(D // LANES):
        lo, hi = cs * LANES, (cs + 1) * LANES

        def c0_of(ref):
            return ref[:, lo:hi].astype(F32) * _sig(ref[:, D + lo:D + hi].astype(F32))

        pad_ref[cs, HALO:HALO + tm, :] = c0_of(pa_ref)
        pad_ref[cs, 0:HALO, :] = jnp.where(first, 0.0, c0_of(pprev_ref))
        pad_ref[cs, HALO + tm:HALO + tm + HALO, :] = jnp.where(last, 0.0, c0_of(pnext_ref))


def _conv_fwd(proj, conv_w_s, conv_b, ln_g, ln_b, w_sq_g, layer, seq, tm, rb):
    T = proj.shape[0]
    D = conv_b.shape[1]
    DQ = D // NSHARD
    NSTR = D // LANES
    KW = 31
    tps = seq // tm
    prev, nxt = _halo_maps(tm, T)

    def body(pa_ref, pprev_ref, pnext_ref, w_ref, b_ref, g_ref, be_ref, wco_ref,
             c1h_ref, rstd_ref, c3_ref, ya_ref, pad_ref, c1s_ref):
        i = pl.program_id(0)
        first = (i % tps) == 0
        last = (i % tps) == tps - 1
        _fill_c0_pad(pad_ref, pa_ref, pprev_ref, pnext_ref, D, tm, first, last)
        _dwconv(pad_ref, w_ref, c1s_ref, NSTR, tm, KW, rb)
        c1 = jnp.concatenate([c1s_ref[cs] for cs in range(NSTR)], axis=1) + b_ref[...]
        mu = jnp.mean(c1, axis=-1, keepdims=True)
        cc = c1 - mu
        rstd = lax.rsqrt(jnp.mean(cc * cc, axis=-1, keepdims=True) + EPS)
        c1h = cc * rstd
        c1h_ref[...] = c1h.astype(BF16)
        rstd_ref[...] = rstd
        c2 = c1h * g_ref[...] + be_ref[...]
        c3 = (c2 * _sig(c2)).astype(BF16)
        c3_ref[...] = c3
        ya_ref[...] = _dot(c3, wco_ref[...].reshape(D, D)).astype(BF16)

    row = lambda i: (i, 0)
    return pl.pallas_call(
        body, name="conv_fwd", grid=(T // tm,),
        in_specs=[pl.BlockSpec((tm, 2 * D), row), pl.BlockSpec((HALO, 2 * D), prev), pl.BlockSpec((HALO, 2 * D), nxt),
                  _res((NSTR, 32, LANES)), _res((1, D)), _res((1, D)), _res((1, D)),
                  _res((NSHARD, DQ, D), lambda i: (0, layer * 3 + 0, 0))],
        out_specs=[pl.BlockSpec((tm, D), row), pl.BlockSpec((tm, 1), row), pl.BlockSpec((tm, D), row),
                   pl.BlockSpec((tm, D), row)],
        out_shape=[_sds((T, D), BF16), _sds((T, 1), F32), _sds((T, D), BF16), _sds((T, D), BF16)],
        scratch_shapes=[pltpu.VMEM((NSTR, tm + 2 * HALO, LANES), F32), pltpu.VMEM((NSTR, tm, LANES), F32)],
        compiler_params=_cparams(("parallel",)),
    )(proj, proj, proj, conv_w_s, conv_b, ln_g, ln_b, w_sq_g)


def _sgu_merge_fwd(proj, ya, x2d, ln_g, ln_b, ws_b, bs_b, gate_bias, w_sq_g, layer, tm):
    T, D = x2d.shape
    DQ = D // NSHARD
    G, CH, _ = ws_b.shape
    GD = D // G

    def body(puv_ref, pg_ref, ya_ref, x_ref, g_ref, be_ref, ws_ref, bsb_ref, gb_ref, wso_ref, wo_ref,
             mixed_ref, gated_ref, yb_ref, merged_ref, x1_ref, mix_scr):
        u = puv_ref[:, :D].astype(F32)
        v = puv_ref[:, D:].astype(F32)
        mu = jnp.mean(v, axis=-1, keepdims=True)
        vc = v - mu
        rstd = lax.rsqrt(jnp.mean(vc * vc, axis=-1, keepdims=True) + EPS)
        vn = (vc * rstd * g_ref[...] + be_ref[...]).astype(BF16)
        nch = tm // CH
        for g in range(G):
            cols = slice(g * GD, (g + 1) * GD)
            rhs = jnp.concatenate([vn[ch * CH:(ch + 1) * CH, cols] for ch in range(nch)], axis=1)
            res = _dot(ws_ref[g], rhs)
            for ch in range(nch):
                mix_scr[ch * CH:(ch + 1) * CH, cols] = res[:, ch * GD:(ch + 1) * GD] + bsb_ref[:, cols]
        mixed = mix_scr[...]
        mixed_ref[...] = mixed.astype(BF16)
        gated = (u * mixed).astype(BF16)
        gated_ref[...] = gated
        yb = _dot(gated, wso_ref[...].reshape(D, D))
        yb_ref[...] = yb.astype(BF16)
        sa = _sig(pg_ref[:, :D].astype(F32) + gb_ref[:, :D])
        sb = _sig(pg_ref[:, D:].astype(F32) + gb_ref[:, D:])
        merged = (sa * ya_ref[...].astype(F32) + sb * yb).astype(BF16)
        merged_ref[...] = merged
        x1_ref[...] = x_ref[...] + _dot(merged, wo_ref[...].reshape(D, D))

    row = lambda i: (i, 0)
    return pl.pallas_call(
        body, name="sgu_merge_fwd", grid=(T // tm,),
        in_specs=[pl.BlockSpec((tm, 2 * D), lambda i: (i, 1)), pl.BlockSpec((tm, 2 * D), lambda i: (i, 2)),
                  pl.BlockSpec((tm, D), row), pl.BlockSpec((tm, D), row),
                  _res((1, D)), _res((1, D)), _res((G, CH, CH)), _res((CH, D)), _res((1, 2 * D)),
                  _res((NSHARD, DQ, D), lambda i: (0, layer * 3 + 1, 0)),
                  _res((NSHARD, DQ, D), lambda i: (0, layer * 3 + 2, 0))],
        out_specs=[pl.BlockSpec((tm, D), row)] * 5,
        out_shape=[_sds((T, D), BF16)] * 4 + [_sds((T, D), F32)],
        scratch_shapes=[pltpu.VMEM((tm, D), F32)],
        compiler_params=_cparams(("parallel",)),
    )(proj, proj, ya, x2d, ln_g, ln_b, ws_b, bs_b, gate_bias, w_sq_g, w_sq_g)


def _ffn_chunks(F):
    assert F % 256 == 0, F
    return [(c0, min(512, F - c0)) for c0 in range(0, F, 512)]


def _ffn_fwd(x1, g_ffn, wgt, wut, wd, tm):
    T, D = x1.shape
    F = wd.shape[0]

    def body(x_ref, g_ref, wg_ref, wu_ref, wd_ref, h2_ref, gt_ref, up_ref, act_ref, x2_ref):
        x = x_ref[...]
        rstd = lax.rsqrt(jnp.mean(x * x, axis=-1, keepdims=True) + EPS)
        h2 = (x * rstd * g_ref[...]).astype(BF16)
        h2_ref[...] = h2
        acc = x
        for c0, cw in _ffn_chunks(F):
            gt = _dot_nt(h2, wg_ref[c0:c0 + cw, :])
            up = _dot_nt(h2, wu_ref[c0:c0 + cw, :])
            gt_ref[:, c0:c0 + cw] = gt.astype(BF16)
            up_ref[:, c0:c0 + cw] = up.astype(BF16)
            act = (gt * _sig(gt) * up).astype(BF16)
            act_ref[:, c0:c0 + cw] = act
            acc = acc + _dot(act, wd_ref[c0:c0 + cw, :])
        x2_ref[...] = acc

    row = lambda i: (i, 0)
    return pl.pallas_call(
        body, name="ffn_fwd", grid=(T // tm,),
        in_specs=[pl.BlockSpec((tm, D), row), _res((1, D)), _res((F, D)), _res((F, D)), _res((F, D))],
        out_specs=[pl.BlockSpec((tm, D), row), pl.BlockSpec((tm, F), row), pl.BlockSpec((tm, F), row),
                   pl.BlockSpec((tm, F), row), pl.BlockSpec((tm, D), row)],
        out_shape=[_sds((T, D), BF16), _sds((T, F), BF16), _sds((T, F), BF16), _sds((T, F), BF16), _sds((T, D), F32)],
        compiler_params=_cparams(("parallel",)),
    )(x1, g_ffn, wgt, wut, wd)


def _loss_head(xf, g_fin, target, tm):
    T, D = xf.shape
    n = T // tm

    def body(x_ref, g_ref, t_ref, dx_ref, loss_ref, dg_ref, acc_ref):
        i = pl.program_id(0)

        @pl.when(i == 0)
        def _():
            acc_ref[...] = jnp.zeros_like(acc_ref)
            dg_ref[...] = jnp.zeros_like(dg_ref)

        x = x_ref[...]
        g = g_ref[...]
        rstd = lax.rsqrt(jnp.mean(x * x, axis=-1, keepdims=True) + EPS)
        xh = x * rstd
        diff = xh * g - t_ref[...]
        acc_ref[...] += jnp.sum(diff * diff, axis=0, keepdims=True)
        dy = diff * (1.0 / D)
        dg_ref[...] += jnp.sum(dy * xh, axis=0, keepdims=True)
        dxh = dy * g
        dx_ref[...] = rstd * (dxh - xh * jnp.mean(dxh * xh, axis=-1, keepdims=True))

        @pl.when(i == n - 1)
        def _():
            tot = jnp.sum(acc_ref[...], axis=-1, keepdims=True) * (0.5 / D)
            loss_ref[...] = jnp.broadcast_to(tot, loss_ref.shape)

    row = lambda i: (i, 0)
    return pl.pallas_call(
        body, name="loss_head", grid=(n,),
        in_specs=[pl.BlockSpec((tm, D), row), _res((1, D)), pl.BlockSpec((tm, D), row)],
        out_specs=[pl.BlockSpec((tm, D), row), pl.BlockSpec((1, LANES), lambda i: (0, 0)),
                   pl.BlockSpec((1, D), lambda i: (0, 0))],
        out_shape=[_sds((T, D), F32), _sds((1, LANES), F32), _sds((1, D), F32)],
        scratch_shapes=[pltpu.VMEM((1, D), F32)],
        compiler_params=_cparams(("arbitrary",)),
    )(xf, g_fin, target)


def _ffn_bwd(dx2, x1, gt, up, g_ffn, wgt, wut, wd, tm, dep=None):
    T, D = x1.shape
    F = wd.shape[0]

    def body(dx2_ref, x1_ref, gt_ref, up_ref, g_ref, wg_ref, wu_ref, wd_ref, dx1_ref, dgt_ref, dup_ref, dg_ref):
        i = pl.program_id(0)

        @pl.when(i == 0)
        def _():
            dg_ref[...] = jnp.zeros_like(dg_ref)

        dx2 = dx2_ref[...]
        dx2b = dx2.astype(BF16)
        dh2 = jnp.zeros((tm, D), F32)
        for c0, cw in _ffn_chunks(F):
            dact = _dot_nt(dx2b, wd_ref[c0:c0 + cw, :])
            g = gt_ref[:, c0:c0 + cw].astype(F32)
            u = up_ref[:, c0:c0 + cw].astype(F32)
            sg = _sig(g)
            dup = (dact * (g * sg)).astype(BF16)
            dgt = (dact * u * (sg * (1.0 + g * (1.0 - sg)))).astype(BF16)
            dgt_ref[:, c0:c0 + cw] = dgt
            dup_ref[:, c0:c0 + cw] = dup
            dh2 = dh2 + _dot(dgt, wg_ref[c0:c0 + cw, :]) + _dot(dup, wu_ref[c0:c0 + cw, :])
        x = x1_ref[...]
        rstd = lax.rsqrt(jnp.mean(x * x, axis=-1, keepdims=True) + EPS)
        xh = x * rstd
        dg_ref[...] += jnp.sum(dh2 * xh, axis=0, keepdims=True)
        dxh = dh2 * g_ref[...]
        dx1_ref[...] = dx2 + rstd * (dxh - xh * jnp.mean(dxh * xh, axis=-1, keepdims=True))

    row = lambda i: (i, 0)
    body, dep_spec, dep_arg = _after(body, 8, dep)
    return pl.pallas_call(
        body, name="ffn_bwd", grid=(T // tm,),
        in_specs=[pl.BlockSpec((tm, D), row), pl.BlockSpec((tm, D), row), pl.BlockSpec((tm, F), row),
                  pl.BlockSpec((tm, F), row), _res((1, D)), _res((F, D)), _res((F, D)), _res((F, D))] + dep_spec,
        out_specs=[pl.BlockSpec((tm, D), row), pl.BlockSpec((tm, F), row), pl.BlockSpec((tm, F), row),
                   pl.BlockSpec((1, D), lambda i: (0, 0))],
        out_shape=[_sds((T, D), F32), _sds((T, F), BF16), _sds((T, F), BF16), _sds((1, D), F32)],
        compiler_params=_cparams(("arbitrary",)),
    )(dx2, x1, gt, up, g_ffn, wgt, wut, wd, *dep_arg)


def _merge_sgu_bwd(dx1, proj, ya, yb, mixed, ln_g, ln_b, ws_b, wst_b, gate_bias, w_sq_g, layer, tm, dep=None):
    T, D = dx1.shape
    DQ = D // NSHARD
    G, CH, _ = ws_b.shape
    GD = D // G

    def body(dx1_ref, puv_ref, pg_ref, ya_ref, yb_ref, mixed_ref, g_ref, be_ref, ws_ref, wst_ref, gb_ref,
             wso_ref, wo_ref, dya_ref, dyb_ref, dp_ref, dgb_ref, dlg_ref, dlb_ref, dbs_ref, dws_ref,
             dvn_scr, dbs_scr):
        i = pl.program_id(0)

        @pl.when(i == 0)
        def _():
            for r in (dgb_ref, dlg_ref, dlb_ref, dws_ref, dbs_scr):
                r[...] = jnp.zeros_like(r)

        dmerged = _dot_nt(dx1_ref[...].astype(BF16), wo_ref[...].reshape(D, D))
        sa = _sig(pg_ref[:, :D].astype(F32) + gb_ref[:, :D])
        sb = _sig(pg_ref[:, D:].astype(F32) + gb_ref[:, D:])
        dya = (dmerged * sa).astype(BF16)
        dyb = (dmerged * sb).astype(BF16)
        dya_ref[...] = dya
        dyb_ref[...] = dyb
        dga = dmerged * ya_ref[...].astype(F32) * (sa * (1.0 - sa))
        dgb = dmerged * yb_ref[...].astype(F32) * (sb * (1.0 - sb))
        dp_ref[:, 4 * D:5 * D] = dga.astype(BF16)
        dp_ref[:, 5 * D:6 * D] = dgb.astype(BF16)
        dgb_ref[:, :D] += jnp.sum(dga, axis=0, keepdims=True)
        dgb_ref[:, D:] += jnp.sum(dgb, axis=0, keepdims=True)

        dgated = _dot_nt(dyb, wso_ref[...].reshape(D, D))
        u = puv_ref[:, :D].astype(F32)
        v = puv_ref[:, D:].astype(F32)
        dp_ref[:, 2 * D:3 * D] = (dgated * mixed_ref[...].astype(F32)).astype(BF16)
        dmixed = dgated * u
        mu = jnp.mean(v, axis=-1, keepdims=True)
        vc = v - mu
        rstd = lax.rsqrt(jnp.mean(vc * vc, axis=-1, keepdims=True) + EPS)
        vh = vc * rstd
        vn = (vh * g_ref[...] + be_ref[...]).astype(BF16)
        dmb = dmixed.astype(BF16)
        nch = tm // CH
        bs_part = dmixed[0:CH, :]
        for ch in range(1, nch):
            bs_part = bs_part + dmixed[ch * CH:(ch + 1) * CH, :]
        dbs_scr[...] += bs_part
        for g in range(G):
            cols = slice(g * GD, (g + 1) * GD)
            dm_g = jnp.concatenate([dmb[ch * CH:(ch + 1) * CH, cols] for ch in range(nch)], axis=1)
            vn_g = jnp.concatenate([vn[ch * CH:(ch + 1) * CH, cols] for ch in range(nch)], axis=1)
            dws_ref[g] += _dot_nt(dm_g, vn_g)
            dvn_g = _dot(wst_ref[g], dm_g)
            for ch in range(nch):
                dvn_scr[ch * CH:(ch + 1) * CH, cols] = dvn_g[:, ch * GD:(ch + 1) * GD]
        dvn = dvn_scr[...]
        dlg_ref[...] += jnp.sum(dvn * vh, axis=0, keepdims=True)
        dlb_ref[...] += jnp.sum(dvn, axis=0, keepdims=True)
        dxh = dvn * g_ref[...]
        dv = rstd * (dxh - jnp.mean(dxh, axis=-1, keepdims=True) - vh * jnp.mean(dxh * vh, axis=-1, keepdims=True))
        dp_ref[:, 3 * D:4 * D] = dv.astype(BF16)

        @pl.when(i == pl.num_programs(0) - 1)
        def _():
            for g in range(G):
                blk = dbs_scr[:, g * GD:(g + 1) * GD]
                if GD != CH:
                    blk = jnp.concatenate([blk, jnp.zeros((CH, CH - GD), F32)], axis=1)
                dbs_ref[:, g * CH:(g + 1) * CH] = jnp.sum(blk.T, axis=0, keepdims=True)

    row = lambda i: (i, 0)
    fixed2 = lambda i: (0, 0)
    body, dep_spec, dep_arg = _after(body, 13, dep)
    return pl.pallas_call(
        body, name="merge_sgu_bwd", grid=(T // tm,),
        in_specs=[pl.BlockSpec((tm, D), row), pl.BlockSpec((tm, 2 * D), lambda i: (i, 1)),
                  pl.BlockSpec((tm, 2 * D), lambda i: (i, 2)), pl.BlockSpec((tm, D), row), pl.BlockSpec((tm, D), row),
                  pl.BlockSpec((tm, D), row), _res((1, D)), _res((1, D)), _res((G, CH, CH)), _res((G, CH, CH)),
                  _res((1, 2 * D)),
                  _res((NSHARD, DQ, D), lambda i: (0, layer * 3 + 1, 0)),
                  _res((NSHARD, DQ, D), lambda i: (0, layer * 3 + 2, 0))] + dep_spec,
        out_specs=[pl.BlockSpec((tm, D), row), pl.BlockSpec((tm, D), row),
                   pl.BlockSpec((tm, 6 * D), row),
                   pl.BlockSpec((1, 2 * D), fixed2), pl.BlockSpec((1, D), fixed2), pl.BlockSpec((1, D), fixed2),
                   pl.BlockSpec((1, G * CH), fixed2), pl.BlockSpec((G, CH, CH), lambda i: (0, 0, 0))],
        out_shape=[_sds((T, D), BF16), _sds((T, D), BF16), _sds((T, 6 * D), BF16),
                   _sds((1, 2 * D), F32), _sds((1, D), F32), _sds((1, D), F32), _sds((1, G * CH), F32),
                   _sds((G, CH, CH), F32)],
        scratch_shapes=[pltpu.VMEM((tm, D), F32), pltpu.VMEM((CH, D), F32)],
        compiler_params=_cparams(("arbitrary",)),
    )(dx1, proj, proj, ya, yb, mixed, ln_g, ln_b, ws_b, wst_b, gate_bias, w_sq_g, w_sq_g, *dep_arg)


def _conv_ln_bwd(dya, c1h, rstd_c, ln_g, ln_b, w_sq_g, layer, tm):
    T, D = dya.shape
    DQ = D // NSHARD

    def body(dya_ref, c1h_ref, rstd_ref, g_ref, be_ref, wco_ref, dc1_ref, dlg_ref, dlb_ref, dcb_ref):
        i = pl.program_id(0)

        @pl.when(i == 0)
        def _():
            for r in (dlg_ref, dlb_ref, dcb_ref):
                r[...] = jnp.zeros_like(r)

        dc3 = _dot_nt(dya_ref[...], wco_ref[...].reshape(D, D))
        c1h = c1h_ref[...].astype(F32)
        c2 = c1h * g_ref[...] + be_ref[...]
        sg = _sig(c2)
        dc2 = dc3 * (sg * (1.0 + c2 * (1.0 - sg)))
        dlg_ref[...] += jnp.sum(dc2 * c1h, axis=0, keepdims=True)
        dlb_ref[...] += jnp.sum(dc2, axis=0, keepdims=True)
        dxh = dc2 * g_ref[...]
        dc1 = rstd_ref[...] * (dxh - jnp.mean(dxh, axis=-1, keepdims=True)
                               - c1h * jnp.mean(dxh * c1h, axis=-1, keepdims=True))
        dc1_ref[...] = dc1
        dcb_ref[...] += jnp.sum(dc1, axis=0, keepdims=True)

    row = lambda i: (i, 0)
    fixed2 = lambda i: (0, 0)
    return pl.pallas_call(
        body, name="conv_ln_bwd", grid=(T // tm,),
        in_specs=[pl.BlockSpec((tm, D), row), pl.BlockSpec((tm, D), row), pl.BlockSpec((tm, 1), row),
                  _res((1, D)), _res((1, D)), _res((NSHARD, DQ, D), lambda i: (0, layer * 3 + 0, 0))],
        out_specs=[pl.BlockSpec((tm, D), row), pl.BlockSpec((1, D), fixed2), pl.BlockSpec((1, D), fixed2),
                   pl.BlockSpec((1, D), fixed2)],
        out_shape=[_sds((T, D), F32), _sds((1, D), F32), _sds((1, D), F32), _sds((1, D), F32)],
        compiler_params=_cparams(("arbitrary",)),
    )(dya, c1h, rstd_c, ln_g, ln_b, w_sq_g)


def _conv_bwd(dc1, proj, dp3, conv_wf_s, seq, tm, rb, dep=None):
    T, D = dc1.shape
    NSTR = D // LANES
    KW = 31
    PADK = (KW - 1) // 2
    tps = seq // tm
    prev, nxt = _halo_maps(tm, T)
    n = T // tm

    def body(dc_ref, dcprev_ref, dcnext_ref, pa_ref, pprev_ref, pnext_ref, wf_ref, dp_in_ref,
             dp_ref, dw_ref, pad_ref, dpad_ref, dc0_ref, dwacc_ref):
        del dp_in_ref
        i = pl.program_id(0)
        first = (i % tps) == 0
        last = (i % tps) == tps - 1

        @pl.when(i == 0)
        def _():
            dwacc_ref[...] = jnp.zeros_like(dwacc_ref)

        _fill_c0_pad(pad_ref, pa_ref, pprev_ref, pnext_ref, D, tm, first, last)
        for cs in range(NSTR):
            lo, hi = cs * LANES, (cs + 1) * LANES
            dpad_ref[cs, HALO:HALO + tm, :] = dc_ref[:, lo:hi]
            dpad_ref[cs, 0:HALO, :] = jnp.where(first, 0.0, dcprev_ref[:, lo:hi])
            dpad_ref[cs, HALO + tm:HALO + tm + HALO, :] = jnp.where(last, 0.0, dcnext_ref[:, lo:hi])
        _dwconv(dpad_ref, wf_ref, dc0_ref, NSTR, tm, KW, rb)

        def strip(cs, carry):
            for r0 in range(0, tm, rb):
                d = dpad_ref[cs, HALO + r0:HALO + r0 + rb, :]
                for k in range(KW):
                    r = r0 + HALO - PADK + k
                    prod = d * pad_ref[cs, r:r + rb, :]
                    dwacc_ref[cs, k * 8:(k + 1) * 8, :] += jnp.sum(prod.reshape(rb // 8, 8, LANES), axis=0)
            return carry

        lax.fori_loop(0, NSTR, strip, 0)

        for cs in range(NSTR):
            lo, hi = cs * LANES, (cs + 1) * LANES
            av = pa_ref[:, lo:hi].astype(F32)
            sg = _sig(pa_ref[:, D + lo:D + hi].astype(F32))
            dc0 = dc0_ref[cs]
            dp_ref[:, lo:hi] = (dc0 * sg).astype(BF16)
            dp_ref[:, D + lo:D + hi] = (dc0 * av * (sg * (1.0 - sg))).astype(BF16)

        @pl.when(i == n - 1)
        def _():
            for cs in range(NSTR):
                dw_ref[cs] = jnp.sum(dwacc_ref[cs].reshape(32, 8, LANES), axis=1)

    row = lambda i: (i, 0)
    body, dep_spec, dep_arg = _after(body, 8, dep)
    return pl.pallas_call(
        body, name="conv_bwd", grid=(n,),
        in_specs=[pl.BlockSpec((tm, D), row), pl.BlockSpec((HALO, D), prev), pl.BlockSpec((HALO, D), nxt),
                  pl.BlockSpec((tm, 2 * D), row), pl.BlockSpec((HALO, 2 * D), prev), pl.BlockSpec((HALO, 2 * D), nxt),
                  _res((NSTR, 32, LANES)), pl.BlockSpec(memory_space=pl.ANY)] + dep_spec,
        out_specs=[pl.BlockSpec((tm, 2 * D), row),
                   pl.BlockSpec((NSTR, 32, LANES), lambda i: (0, 0, 0))],
        out_shape=[_sds(dp3.shape, BF16), _sds((NSTR, 32, LANES), F32)],
        scratch_shapes=[pltpu.VMEM((NSTR, tm + 2 * HALO, LANES), F32), pltpu.VMEM((NSTR, tm + 2 * HALO, LANES), F32),
                        pltpu.VMEM((NSTR, tm, LANES), F32), pltpu.VMEM((NSTR, 32 * 8, LANES), F32)],
        input_output_aliases={7: 0},
        compiler_params=_cparams(("arbitrary",)),
    )(dc1, dc1, dc1, proj, proj, proj, conv_wf_s, dp3, *dep_arg)


def _mix_in_bwd(dx1, dp3, x2d, g_mix, w_in_g, layer, tm, dep=None):
    T, D = x2d.shape
    CS = w_in_g.shape[2]
    CN = CS // 3

    def body(dx1_ref, dp_ref, x_ref, g_ref, w_ref, dx_ref, dg_ref):
        i = pl.program_id(0)

        @pl.when(i == 0)
        def _():
            dg_ref[...] = jnp.zeros_like(dg_ref)

        dh = jnp.zeros((tm, D), F32)
        for j in range(12):
            dh = dh + _dot_nt(dp_ref[:, j * CN:(j + 1) * CN], w_ref[j // 3, :, (j % 3) * CN:(j % 3 + 1) * CN])
        x = x_ref[...]
        rstd = lax.rsqrt(jnp.mean(x * x, axis=-1, keepdims=True) + EPS)
        xh = x * rstd
        dg_ref[...] += jnp.sum(dh * xh, axis=0, keepdims=True)
        dxh = dh * g_ref[...]
        dx_ref[...] = dx1_ref[...] + rstd * (dxh - xh * jnp.mean(dxh * xh, axis=-1, keepdims=True))

    row = lambda i: (i, 0)
    body, dep_spec, dep_arg = _after(body, 5, dep)
    return pl.pallas_call(
        body, name="mix_in_bwd", grid=(T // tm,),
        in_specs=[pl.BlockSpec((tm, D), row), pl.BlockSpec((tm, 6 * D), row),
                  pl.BlockSpec((tm, D), row), _res((1, D)), _res((NSHARD, D, CS), lambda i: (0, layer, 0))] + dep_spec,
        out_specs=[pl.BlockSpec((tm, D), row), pl.BlockSpec((1, D), lambda i: (0, 0))],
        out_shape=[_sds((T, D), F32), _sds((1, D), F32)],
        compiler_params=_cparams(("arbitrary",)),
    )(dx1, dp3, x2d, g_mix, w_in_g, *dep_arg)


def _tn_matmul(name, a, a_block, a_map, bs, b_block, b_map, out_shape, out_block, out_map, nj, nt):
    kk = [d for d in a_block if d is not None][-1]
    nn = [d for d in b_block if d is not None][-1]
    nb = len(bs)
    a_list = a if isinstance(a, (list, tuple)) else [a]
    na = len(a_list)

    def body(*refs):
        a_refs, b_refs = refs[:na], refs[na:na + nb]
        o_refs, acc_refs = refs[-2 * nb:-nb], refs[-nb:]
        t = pl.program_id(1)

        @pl.when(t == 0)
        def _():
            for acc_ref in acc_refs:
                acc_ref[...] = jnp.zeros_like(acc_ref)

        a_ts = [a_ref[...].astype(BF16) for a_ref in a_refs]
        for i, (b_ref, acc_ref) in enumerate(zip(b_refs, acc_refs)):
            acc_ref[...] += _dot_tn(a_ts[i % na], b_ref[...].astype(BF16))

        @pl.when(t == nt - 1)
        def _():
            for o_ref, acc_ref in zip(o_refs, acc_refs):
                o_ref[...] = acc_ref[...].astype(o_ref.dtype)

    return pl.pallas_call(
        body, name=name, grid=(nj, nt),
        in_specs=[pl.BlockSpec(a_block, a_map)] * na + [pl.BlockSpec(b_block, b_map)] * nb,
        out_specs=[pl.BlockSpec(out_block, out_map)] * nb, out_shape=[_sds(out_shape, BF16)] * nb,
        scratch_shapes=[pltpu.VMEM((kk, nn), F32)] * nb,
        compiler_params=_cparams(("parallel", "arbitrary")),
    )(*a_list, *bs)


def _place_shard(name, w, pos, dtype, tr, dep=None, layer=None, into=None, row_off=0, out_rows=None):
    R, C = w.shape[-2:]
    out_rows = out_rows or R

    def body(pos_ref, w_ref, *rest):
        del pos_ref
        rest[-1][...] = w_ref[...].astype(dtype)

    in_spec = (pl.BlockSpec((tr, C), lambda r, pos: (r, 0)) if layer is None else
               pl.BlockSpec((None, tr, C), lambda r, pos: (layer, r, 0)))
    extra, extra_args = ([], []) if into is None else ([pl.BlockSpec(memory_space=pl.ANY)], [into])
    body, dep_spec, dep_arg = _after(body, 2 + len(extra), dep)
    grid_spec = pltpu.PrefetchScalarGridSpec(
        num_scalar_prefetch=1, grid=(R // tr,), in_specs=[in_spec] + extra + dep_spec,
        out_specs=pl.BlockSpec((None, tr, C), lambda r, pos: (pos[1], row_off // tr + r, 0)))
    return pl.pallas_call(body, name=name, grid_spec=grid_spec, out_shape=_sds((NSHARD, out_rows, C), dtype),
                          input_output_aliases={} if into is None else {2: 0},
                          compiler_params=_cparams(("parallel",)))(pos, w, *extra_args, *dep_arg)


def _add_halves(name, g, rbuf, pos, tr):
    NS, _, H, C = g.shape

    def body(pos_ref, g_ref, r_ref, o_ref):
        del pos_ref
        o_ref[...] = (g_ref[...].astype(F32) + r_ref[...].astype(F32)).astype(BF16)

    grid_spec = pltpu.PrefetchScalarGridSpec(
        num_scalar_prefetch=1, grid=(NS, H // tr),
        in_specs=[pl.BlockSpec((None, None, tr, C), lambda s, r, pos: (s, pos[0], r, 0)),
                  pl.BlockSpec((None, tr, C), lambda s, r, pos: (s, r, 0))],
        out_specs=pl.BlockSpec((None, tr, C), lambda s, r, pos: (s, r, 0)))
    return pl.pallas_call(body, name=name, grid_spec=grid_spec, out_shape=_sds((NS, H, C), BF16),
                          compiler_params=_cparams(("parallel", "parallel")))(pos, g, rbuf)


def _add_shards(name, p, rbuf, pos, tr, layer, n_layers, prev):
    _, H, C = p.shape

    def body(pos_ref, p_ref, r_ref, *rest):
        del pos_ref
        o_ref = rest[-1]
        acc = p_ref[...].astype(F32)
        for j in range(3):
            acc = acc + r_ref[j].astype(F32)
        o_ref[...] = acc

    in_specs = [pl.BlockSpec((None, tr, C), lambda r, pos: (pos[1], r, 0)),
                pl.BlockSpec((3, tr, C), lambda r, pos: (0, r, 0))]
    args = [pos, p, rbuf]
    aliases = {}
    if prev is not None:
        in_specs.append(pl.BlockSpec(memory_space=pl.ANY))
        args.append(prev)
        aliases = {3: 0}
    grid_spec = pltpu.PrefetchScalarGridSpec(
        num_scalar_prefetch=1, grid=(H // tr,), in_specs=in_specs,
        out_specs=pl.BlockSpec((None, None, tr, C), lambda r, pos: (layer, pos[0], r, 0)))
    return pl.pallas_call(body, name=name, grid_spec=grid_spec, out_shape=_sds((n_layers, 2, H, C), F32),
                          input_output_aliases=aliases, compiler_params=_cparams(("parallel",)))(*args)


def _sum_slots(own, land, me, tr):
    NS8, R, C = land.shape

    def body(me_ref, own_ref, l_ref, o_ref):
        acc = None
        for j in range(NS8):
            term = jnp.where(me_ref[0] == j, own_ref[...], l_ref[j])
            acc = term if acc is None else acc + term
        o_ref[...] = acc

    grid_spec = pltpu.PrefetchScalarGridSpec(
        num_scalar_prefetch=1, grid=(R // tr,),
        in_specs=[pl.BlockSpec((tr, C), lambda i, me: (i, 0)), pl.BlockSpec((NS8, tr, C), lambda i, me: (0, i, 0))],
        out_specs=pl.BlockSpec((tr, C), lambda i, me: (i, 0)))
    return pl.pallas_call(body, name="sum_slots", grid_spec=grid_spec, out_shape=_sds((R, C), F32),
                          compiler_params=_cparams(("parallel",)))(me, own, land)


def _adamw_update(w_ref, g_ref, m_ref, v_ref, d_ref, mo_ref, vo_ref):
    g_ = g_ref[...]
    m_ = ADAM_B1 * m_ref[...] + (1.0 - ADAM_B1) * g_
    v_ = ADAM_B2 * v_ref[...] + (1.0 - ADAM_B2) * (g_ * g_)
    mo_ref[...] = m_
    vo_ref[...] = v_
    m_hat = m_ / (1.0 - ADAM_B1 ** ADAM_STEP)
    v_hat = v_ / (1.0 - ADAM_B2 ** ADAM_STEP)
    d_ref[...] = -ADAM_LR * (m_hat / (jnp.sqrt(v_hat) + ADAM_EPS) + ADAM_WD * w_ref[...])


def _adamw(name, w, g, m, v, tr, emit_g):
    R, C = w.shape
    n_out = 4 if emit_g else 3

    def body(w_ref, g_ref, m_ref, v_ref, d_ref, mo_ref, vo_ref, *go_ref):
        _adamw_update(w_ref, g_ref, m_ref, v_ref, d_ref, mo_ref, vo_ref)
        if emit_g:
            go_ref[0][...] = g_ref[...]

    spec = pl.BlockSpec((tr, C), lambda i: (i, 0))
    return pl.pallas_call(
        body, name=name, grid=(R // tr,), in_specs=[spec] * 4, out_specs=[spec] * n_out,
        out_shape=[_sds((R, C), F32)] * n_out, compiler_params=_cparams(("parallel",)))(w, g, m, v)


def _adamw_many(ws, gs, ms, vs):
    n = len(ws)

    def body(*refs):
        ins, outs = refs[:4 * n], refs[4 * n:]
        for k in range(n):
            _adamw_update(ins[k], ins[n + k], ins[2 * n + k], ins[3 * n + k], outs[k], outs[n + k], outs[2 * n + k])

    vmem = pl.BlockSpec(memory_space=pltpu.VMEM)
    res = pl.pallas_call(
        body, name="adamw_small", in_specs=[vmem] * (4 * n), out_specs=[vmem] * (3 * n),
        out_shape=[_sds(w.shape, F32) for w in ws] * 3,
        compiler_params=pltpu.CompilerParams(vmem_limit_bytes=VMEM_LIMIT))(*ws, *gs, *ms, *vs)
    return list(res[:n]), list(res[n:2 * n]), list(res[2 * n:])


def _row_tile(rows, cap):
    best = rows
    for t in range(8, min(rows, cap) + 1, 8):
        if rows % t == 0:
            best = t
    return best


HBM_SPEC = pl.BlockSpec(memory_space=pltpu.HBM)
SEM_SPEC = pl.BlockSpec(memory_space=pltpu.SEMAPHORE)
DATAFLOW = pltpu.SideEffectType.DATAFLOW_SIDE_EFFECTING
DMA_SEM = pltpu.SemaphoreType.DMA


def _hbm(a):
    return pltpu.with_memory_space_constraint(a, pltpu.HBM)


def _mesh_pos():
    return lax.axis_index("x"), lax.axis_index("y"), lax.axis_index("c")


def _other_chips(x, y):
    return [(1 - x, y), (x, 1 - y), (1 - x, 1 - y)]


def _half_rows(buf, shard, core):
    h = buf.shape[1] // 2
    return buf.at[shard, pl.ds(core * h, h), :]


def _ici_copy(buf, j, send, recv, landing):
    x, y, c = _mesh_pos()
    px, py = _other_chips(x, y)[j]
    part = _half_rows(buf, 2 * px + py if landing else 2 * x + y, c)
    return pltpu.make_async_remote_copy(src_ref=part, dst_ref=part, send_sem=send, recv_sem=recv,
                                        device_id=(px, py, c), device_id_type=MESH_ID)


def _sibling_copy(buf, j, send, recv, landing):
    x, y, c = _mesh_pos()
    px, py = _other_chips(x, y)[j]
    part = _half_rows(buf, 2 * px + py, 1 - c if landing else c)
    return pltpu.make_async_remote_copy(src_ref=part, dst_ref=part, send_sem=send, recv_sem=recv,
                                        device_id=(x, y, 1 - c), device_id_type=MESH_ID)


def _forward_sibling(name, bufs, with_ici):
    n = len(bufs)

    def body(*refs):
        ins = refs[:n]
        send_ici, recv_ici, send_d2d, recv_d2d = refs[2 * n:]
        sends = []
        if with_ici:
            for i in range(n):
                for j in range(3):
                    cp = _ici_copy(ins[i], j, send_ici.at[i, j], recv_ici.at[i, j], False)
                    cp.start()
                    sends.append(cp)
        for i in range(n):
            for j in range(3):
                if with_ici:
                    _ici_copy(ins[i], j, send_ici.at[i, j], recv_ici.at[i, j], True).wait_recv()
                cp = _sibling_copy(ins[i], j, send_d2d.at[i, j], recv_d2d.at[i, j], False)
                cp.start()
                sends.append(cp)
        for i in range(n):
            for j in range(3):
                _sibling_copy(ins[i], j, send_d2d.at[i, j], recv_d2d.at[i, j], True).wait_recv()
        for cp in sends:
            cp.wait_send()

    return pl.pallas_call(
        body, name=name, in_specs=[HBM_SPEC] * n, out_specs=[HBM_SPEC] * n,
        out_shape=[_sds(b.shape, b.dtype) for b in bufs],
        scratch_shapes=[DMA_SEM((n, 3))] * 4, input_output_aliases={i: i for i in range(n)},
    )(*bufs)


def _gather_start(name, groups):
    flat = [b for g in groups for b in g]
    n, ng = len(flat), len(groups)

    def body(*refs):
        ins, sems, token = refs[:n], refs[n:n + 2 * ng], refs[-1]
        k = 0
        for gi, g in enumerate(groups):
            for a in range(len(g)):
                for j in range(3):
                    _ici_copy(ins[k], j, sems[2 * gi], sems[2 * gi + 1], False).start()
                k += 1
        token[...] = jnp.zeros_like(token)

    res = pl.pallas_call(
        body, name=name, in_specs=[HBM_SPEC] * n,
        out_specs=[SEM_SPEC] * (2 * ng) + [HBM_SPEC] * n + [pl.BlockSpec(memory_space=pltpu.VMEM)],
        out_shape=[DMA_SEM(()) for g in groups for _ in range(2)]
        + [pltpu.HBM(b.shape, b.dtype) for b in flat] + [_sds((8, LANES), F32)],
        input_output_aliases={i: 2 * ng + i for i in range(n)},
        compiler_params=pltpu.CompilerParams(has_side_effects=DATAFLOW),
    )(*[_hbm(b) for b in flat])
    sems = [(res[2 * gi], res[2 * gi + 1]) for gi in range(ng)]
    thru, k = [], 2 * ng
    for g in groups:
        thru.append(list(res[k:k + len(g)]))
        k += len(g)
    return sems, thru, res[-1]


def _gather_wait(name, bufs, sems, after):
    n = len(bufs)

    def body(*refs):
        ins, send, recv = refs[:n], refs[n], refs[n + 1]
        for a in range(n):
            for j in range(3):
                _ici_copy(ins[a], j, send, recv, False).wait_send()
                _ici_copy(ins[a], j, send, recv, True).wait_recv()

    return pl.pallas_call(
        body, name=name, in_specs=[HBM_SPEC] * n + [SEM_SPEC, SEM_SPEC, pl.BlockSpec(memory_space=pl.ANY)],
        out_specs=[HBM_SPEC] * n, out_shape=[pltpu.HBM(b.shape, b.dtype) for b in bufs],
        input_output_aliases={i: i for i in range(n)},
        compiler_params=pltpu.CompilerParams(has_side_effects=DATAFLOW),
    )(*bufs, sems[0], sems[1], after)


def _send_sibling_halves(name, arrs):
    n = len(arrs)

    def body(*refs):
        ins, outs = refs[:n], refs[n:2 * n]
        send, recv = refs[2 * n:]
        x, y, c = _mesh_pos()
        cps = []
        for i in range(n):
            cp = pltpu.make_async_remote_copy(
                src_ref=ins[i].at[:, 1 - c], dst_ref=outs[i],
                send_sem=send.at[i], recv_sem=recv.at[i], device_id=(x, y, 1 - c), device_id_type=MESH_ID)
            cp.start()
            cps.append(cp)
        for cp in cps:
            cp.wait()

    return pl.pallas_call(
        body, name=name, in_specs=[HBM_SPEC] * n, out_specs=[HBM_SPEC] * n,
        out_shape=[_sds((a.shape[0],) + a.shape[2:], a.dtype) for a in arrs],
        scratch_shapes=[DMA_SEM((n,)), DMA_SEM((n,))],
    )(*arrs)


def _chip_copy(p, land, j, send, recv):
    x, y, c = _mesh_pos()
    px, py = _other_chips(x, y)[j]
    return pltpu.make_async_remote_copy(src_ref=p.at[2 * px + py], dst_ref=land.at[j], send_sem=send, recv_sem=recv,
                                        device_id=(px, py, c), device_id_type=MESH_ID)


def _chip_send_start(name, ps):
    n = len(ps)
    lands = [lax.empty((3,) + p.shape[1:], p.dtype) for p in ps]

    def body(*refs):
        ins, lnd, send, recv, token = refs[:n], refs[n:2 * n], refs[2 * n], refs[2 * n + 1], refs[-1]
        for i in range(n):
            for j in range(3):
                _chip_copy(ins[i], lnd[i], j, send, recv).start()
        token[...] = jnp.zeros_like(token)

    res = pl.pallas_call(
        body, name=name, in_specs=[HBM_SPEC] * (2 * n),
        out_specs=[SEM_SPEC, SEM_SPEC] + [HBM_SPEC] * (2 * n) + [pl.BlockSpec(memory_space=pltpu.VMEM)],
        out_shape=[DMA_SEM(()), DMA_SEM(())] + [pltpu.HBM(a.shape, a.dtype) for a in ps + lands]
        + [_sds((8, LANES), F32)],
        input_output_aliases={i: 2 + i for i in range(2 * n)},
        compiler_params=pltpu.CompilerParams(has_side_effects=DATAFLOW),
    )(*[_hbm(a) for a in ps + lands])
    return (res[0], res[1]), list(res[2:2 + n]), list(res[2 + n:2 + 2 * n]), res[-1]


def _chip_send_wait(name, ps, lands, sems, after):
    n = len(ps)

    def body(*refs):
        ins, lnd, send, recv = refs[:n], refs[n:2 * n], refs[2 * n], refs[2 * n + 1]
        for i in range(n):
            for j in range(3):
                cp = _chip_copy(ins[i], lnd[i], j, send, recv)
                cp.wait_send()
                cp.wait_recv()

    res = pl.pallas_call(
        body, name=name, in_specs=[HBM_SPEC] * (2 * n) + [SEM_SPEC, SEM_SPEC, pl.BlockSpec(memory_space=pl.ANY)],
        out_specs=[HBM_SPEC] * (2 * n), out_shape=[pltpu.HBM(a.shape, a.dtype) for a in ps + lands],
        input_output_aliases={i: i for i in range(2 * n)},
        compiler_params=pltpu.CompilerParams(has_side_effects=DATAFLOW),
    )(*ps, *lands, sems[0], sems[1], after)
    return list(res[:n]), list(res[n:])


def _join_halves(arrs):
    n = len(arrs)

    def body(*refs):
        bufs = refs[n:2 * n]
        send, recv = refs[2 * n:]
        x, y, c = _mesh_pos()
        cps = []
        for i in range(n):
            mine = bufs[i].at[:, c]
            cp = pltpu.make_async_remote_copy(
                src_ref=mine, dst_ref=mine, send_sem=send.at[i], recv_sem=recv.at[i],
                device_id=(x, y, 1 - c), device_id_type=MESH_ID)
            cp.start()
            cps.append(cp)
        for i, cp in enumerate(cps):
            theirs = bufs[i].at[:, 1 - c]
            cp.wait_send()
            pltpu.make_async_remote_copy(
                src_ref=theirs, dst_ref=theirs, send_sem=send.at[i], recv_sem=recv.at[i],
                device_id=(x, y, 1 - c), device_id_type=MESH_ID).wait_recv()

    return pl.pallas_call(
        body, name="join_halves", in_specs=[HBM_SPEC] * n, out_specs=[HBM_SPEC] * n,
        out_shape=[_sds(a.shape, a.dtype) for a in arrs],
        scratch_shapes=[DMA_SEM((n,)), DMA_SEM((n,))], input_output_aliases={i: i for i in range(n)},
    )(*arrs)


def _peer_copy(buf, land, k, send, recv, landing):
    x, y, c = _mesh_pos()
    px, py, pc = x ^ ((k >> 2) & 1), y ^ ((k >> 1) & 1), c ^ (k & 1)
    slot = 4 * px + 2 * py + pc if landing else 4 * x + 2 * y + c
    return pltpu.make_async_remote_copy(src_ref=buf, dst_ref=land.at[slot], send_sem=send, recv_sem=recv,
                                        device_id=(px, py, pc), device_id_type=MESH_ID)


def _exchange_all(buf):
    def body(in_ref, out_ref, send, recv):
        cps = [_peer_copy(in_ref, out_ref, k, send.at[k - 1], recv.at[k - 1], False) for k in range(1, 8)]
        for cp in cps:
            cp.start()
        for k in range(1, 8):
            cps[k - 1].wait_send()
            _peer_copy(in_ref, out_ref, k, send.at[k - 1], recv.at[k - 1], True).wait_recv()

    return pl.pallas_call(
        body, name="exchange_all", in_specs=[HBM_SPEC], out_specs=HBM_SPEC,
        out_shape=_sds((8,) + buf.shape, buf.dtype), scratch_shapes=[DMA_SEM((7,)), DMA_SEM((7,))],
    )(buf)


def _exchange_start(name, buf):
    land = lax.empty((8,) + buf.shape, buf.dtype)

    def body(in_ref, land_ref, send, recv, in_thru, land_thru, token):
        for k in range(1, 8):
            _peer_copy(in_ref, land_ref, k, send, recv, False).start()
        token[...] = jnp.zeros_like(token)

    res = pl.pallas_call(
        body, name=name, in_specs=[HBM_SPEC] * 2,
        out_specs=[SEM_SPEC, SEM_SPEC, HBM_SPEC, HBM_SPEC, pl.BlockSpec(memory_space=pltpu.VMEM)],
        out_shape=[DMA_SEM(()), DMA_SEM(()), pltpu.HBM(buf.shape, buf.dtype), pltpu.HBM(land.shape, land.dtype),
                   _sds((8, LANES), F32)],
        input_output_aliases={0: 2, 1: 3}, compiler_params=pltpu.CompilerParams(has_side_effects=DATAFLOW),
    )(_hbm(buf), _hbm(land))
    return (res[0], res[1]), res[2], res[3], res[4]


def _exchange_wait(name, buf, land, sems, after):
    def body(in_ref, land_ref, send, recv, after_ref, in_thru, land_thru):
        for k in range(1, 8):
            _peer_copy(in_ref, land_ref, k, send, recv, False).wait_send()
            _peer_copy(in_ref, land_ref, k, send, recv, True).wait_recv()

    res = pl.pallas_call(
        body, name=name, in_specs=[HBM_SPEC, HBM_SPEC, SEM_SPEC, SEM_SPEC, pl.BlockSpec(memory_space=pl.ANY)],
        out_specs=[HBM_SPEC, HBM_SPEC], out_shape=[pltpu.HBM(buf.shape, buf.dtype), pltpu.HBM(land.shape, land.dtype)],
        input_output_aliases={0: 0, 1: 1}, compiler_params=pltpu.CompilerParams(has_side_effects=DATAFLOW),
    )(buf, land, sems[0], sems[1], after)
    return res[0], res[1]


def _pad_to(a, axis, size):
    pad = [(0, 0)] * a.ndim
    pad[axis] = (0, size - a.shape[axis])
    return jnp.pad(a, pad)


def _strips(w):
    k, d = w.shape
    return _pad_to(w, 0, 32).reshape(32, d // LANES, LANES).transpose(1, 0, 2)


def kernel(x, norm_mix, w_in, gate_bias, conv_w, conv_b, conv_ln_g, conv_ln_b, w_conv_out, sgu_ln_g, sgu_ln_b, w_spatial, b_spatial, w_sgu_out, w_o, norm_ffn, w_ffn_gate, w_ffn_up, w_ffn_down, norm_final, loss_target, m_norm_mix, m_w_in, m_gate_bias, m_conv_w, m_conv_b, m_conv_ln_g, m_conv_ln_b, m_w_conv_out, m_sgu_ln_g, m_sgu_ln_b, m_w_spatial, m_b_spatial, m_w_sgu_out, m_w_o, m_norm_ffn, m_w_ffn_gate, m_w_ffn_up, m_w_ffn_down, m_norm_final, v_norm_mix, v_w_in, v_gate_bias, v_conv_w, v_conv_b, v_conv_ln_g, v_conv_ln_b, v_w_conv_out, v_sgu_ln_g, v_sgu_ln_b, v_w_spatial, v_b_spatial, v_w_sgu_out, v_w_o, v_norm_ffn, v_w_ffn_gate, v_w_ffn_up, v_w_ffn_down, v_norm_final):
    BL, S, D = x.shape
    T = BL * S
    L = w_in.shape[0]
    CS = w_in.shape[2]
    CN = CS // 3
    DQ = D // NSHARD
    FS = w_ffn_gate.shape[2]
    F = NSHARD * FS
    G, CH = w_spatial.shape[1], w_spatial.shape[2]
    KW = conv_w.shape[1]
    CQ = conv_w.shape[3]
    NSTR = D // LANES
    tm = min(512, S // 2)
    tm2 = max(tm // 2, CH)
    rb = min(64, tm)
    mx, my, mc = _mesh_pos()
    pos = jnp.stack([mc, 2 * mx + my]).astype(jnp.int32)

    def placed(name, w, dtype=BF16, dep=None, layer=None, **kw):
        return _place_shard("place_" + name, w, pos, dtype, _row_tile(w.shape[-2], 256), dep, layer, **kw)

    w_in0 = placed("w_in", w_in, layer=0)
    cw_p = placed("conv_w", _pad_to(conv_w.reshape(L, KW, CQ), 1, 32).reshape(L * 32, CQ), F32)
    fsems, fflying, ftoken = _gather_start("gather_start_first", [[w_in0, cw_p]])
    wts = []
    for l in range(L):
        w_sq = None
        for i, w in enumerate([w_conv_out, w_sgu_out, w_o]):
            w_sq = placed("w_sq", w, dep=ftoken, layer=l, into=w_sq, row_off=i * DQ, out_rows=3 * DQ)
        wts.append(dict(w_in=placed("w_in", w_in, dep=ftoken, layer=l) if l else None, w_sq=w_sq,
                        wg=placed("w_gate", w_ffn_gate[l].T, dep=ftoken), wu=placed("w_up", w_ffn_up[l].T, dep=ftoken),
                        wd=placed("w_down", w_ffn_down, dep=ftoken, layer=l)))
    ffn_keys = ["wg", "wu", "wd"]
    order = [[(0, "w_sq")], [(0, k) for k in ffn_keys]]
    order += [[(l, k) for k in ["w_in", "w_sq"] + ffn_keys] for l in range(1, L)]
    gsems, flying, token = _gather_start("gather_start", [[wts[l][k] for l, k in grp] for grp in order])
    first = _gather_wait("gather_wait_first", fflying[0], fsems[0], token)
    wts[0]["w_in"], cw_g = _forward_sibling("gather_first", first, False)
    conv_w_full = cw_g.reshape(NSHARD, L, 32, CQ).transpose(1, 2, 0, 3).reshape(L, 32, D)[:, :KW]

    def land(gi, after):
        bufs = _gather_wait("gather_wait_%d" % gi, flying[gi], gsems[gi], after)
        bufs = _forward_sibling("gather_forward_%d" % gi, bufs, False)
        for (l, k), b in zip(order[gi], bufs):
            wts[l][k] = b

    x2d = x.reshape(T, D)
    tgt = loss_target.reshape(T, D)
    row = lambda a, l: a[l].reshape(1, -1)

    saved = []
    xc = x2d
    for l in range(L):
        ws_b = w_spatial[l].astype(BF16)
        bs_b = jnp.repeat(b_spatial[l].T, D // G, axis=1)
        cw_s = _strips(conv_w_full[l])
        h, proj = _mix_in_fwd(xc, row(norm_mix, l), wts[l]["w_in"], 0, tm, token if l == 0 else None)
        if l == 0:
            land(0, h)
        c1h, rstd_c, c3, ya = _conv_fwd(proj, cw_s, row(conv_b, l), row(conv_ln_g, l), row(conv_ln_b, l),
                                        wts[l]["w_sq"], 0, S, tm, rb)
        if l == 0:
            land(1, ya)
        mixed, gated, yb, merged, x1 = _sgu_merge_fwd(proj, ya, xc, row(sgu_ln_g, l), row(sgu_ln_b, l), ws_b, bs_b,
                                                      row(gate_bias, l), wts[l]["w_sq"], 0, tm2)
        ffn_w = [wts[l][k].reshape(F, D) for k in ffn_keys]
        h2, gt, up, act, x2 = _ffn_fwd(x1, row(norm_ffn, l), *ffn_w, tm2)
        if l + 1 < L:
            land(l + 2, x2)
        saved.append(dict(x=xc, h=h, proj=proj, c1h=c1h, rstd_c=rstd_c, c3=c3, ya=ya, mixed=mixed, gated=gated,
                          yb=yb, merged=merged, x1=x1, h2=h2, gt=gt, up=up, act=act, ws_b=ws_b, cw=conv_w_full[l]))
        xc = x2

    dx, loss_part, d_norm_final = _loss_head(xc, norm_final.reshape(1, D), tgt, tm)
    loss = lax.psum(loss_part[0, 0], ("x", "y", "c"))

    g_acc = {}

    def reduce_start(tag, layer, named):
        arrs = [g.reshape(NSHARD, 2, g.shape[1] // 2, g.shape[2]) for _, g in named]
        from_sib = _send_sibling_halves("sibling_" + tag, arrs)
        ps = [_add_halves("presum_" + nm, a, r, pos, _row_tile(a.shape[2], 256))
              for (nm, _), a, r in zip(named, arrs, from_sib)]
        sems, ps, lands, tok = _chip_send_start("chip_send_start_" + tag, ps)
        return dict(tag=tag, layer=layer, names=[nm for nm, _ in named], ps=ps, lands=lands, sems=sems), tok

    def reduce_finish(pend, after):
        ps, lands = _chip_send_wait("chip_send_wait_" + pend["tag"], pend["ps"], pend["lands"], pend["sems"], after)
        for nm, p, r in zip(pend["names"], ps, lands):
            g_acc[nm] = _add_shards("shardsum_" + nm, p, r, pos, _row_tile(p.shape[1], 256), pend["layer"], L,
                                    g_acc.get(nm))

    me_idx = (4 * mx + 2 * my + mc).astype(jnp.int32).reshape(1)
    exchanges = []

    def pack_rows(pieces):
        packed = jnp.concatenate(pieces, axis=0)
        return _pad_to(packed, 0, -(-packed.shape[0] // 8) * 8)

    def unpack_rows(summed, pieces):
        out, off = [], 0
        for p in pieces:
            out.append(summed[off:off + p.shape[0]])
            off += p.shape[0]
        return out

    def small_start(tag, pieces):
        sems, buf, land, token = _exchange_start("exchange_start_" + tag, pack_rows(pieces))
        return dict(tag=tag, pieces=pieces, buf=buf, land=land, sems=sems, token=token)

    def small_finish(st, after):
        buf, land = _exchange_wait("exchange_wait_" + st["tag"], st["buf"], st["land"], st["sems"], after)
        return unpack_rows(_sum_slots(buf, land, me_idx, _row_tile(buf.shape[0], 256)), st["pieces"])

    small = [None] * L
    tt = min(1024, T // 2)
    nt = T // tt
    pending, tok = None, None
    for l in reversed(range(L)):
        sv, wt = saved[l], wts[l]
        ffn_w = [wt[k].reshape(F, D) for k in ffn_keys]
        dx1, dgt, dup, d_norm_ffn = _ffn_bwd(dx, sv["x1"], sv["gt"], sv["up"], row(norm_ffn, l), *ffn_w, tm2, tok)
        tn_a = ((tt, F // 2), lambda j, t: (t, j))
        tn_b = ((tt, D), lambda j, t: (t, 0))
        tn_o = ((F, D), (F // 2, D), lambda j, t: (j, 0), 2, nt)
        g_g, = _tn_matmul("grad_w_gate", dgt, *tn_a, [sv["h2"]], *tn_b, *tn_o)
        g_u, = _tn_matmul("grad_w_up", dup, *tn_a, [sv["h2"]], *tn_b, *tn_o)
        g_d, = _tn_matmul("grad_w_down", sv["act"], *tn_a, [dx], *tn_b, *tn_o)
        if pending is not None:
            reduce_finish(pending, g_d)
        ffn_pend, tok = reduce_start("ffn%d" % l, l, [
            ("w_ffn_gate", g_g.reshape(NSHARD, FS, D)), ("w_ffn_up", g_u.reshape(NSHARD, FS, D)),
            ("w_ffn_down", g_d.reshape(NSHARD, FS, D))])
        wst_b = jnp.swapaxes(sv["ws_b"], 1, 2)
        dya, dyb, dp3, d_gate_bias, d_sgu_g, d_sgu_b, d_bs, d_ws = _merge_sgu_bwd(
            dx1, sv["proj"], sv["ya"], sv["yb"], sv["mixed"], row(sgu_ln_g, l), row(sgu_ln_b, l), sv["ws_b"], wst_b,
            row(gate_bias, l), wt["w_sq"], 0, tm2, tok)
        sq_args = ((tt, D), lambda j, t: (t, 0))
        sq_out = ((D, D), (D, D), lambda j, t: (0, 0), 1, nt)
        g_o, = _tn_matmul("grad_w_o", sv["merged"], *sq_args, [dx1], *sq_args, *sq_out)
        g_so, = _tn_matmul("grad_w_sgu_out", sv["gated"], *sq_args, [dyb], *sq_args, *sq_out)
        g_co, = _tn_matmul("grad_w_conv_out", sv["c3"], *sq_args, [dya], *sq_args, *sq_out)
        dc1, d_cln_g, d_cln_b, d_conv_b = _conv_ln_bwd(dya, sv["c1h"], sv["rstd_c"], row(conv_ln_g, l),
                                                       row(conv_ln_b, l), wt["w_sq"], 0, tm)
        small[l] = [None, d_gate_bias.reshape(2, D), None, d_conv_b, d_cln_g, d_cln_b, d_sgu_g, d_sgu_b,
                    d_ws.reshape(G * CH * CH // D, D), d_bs.reshape(G * CH // D, D), d_norm_ffn]
        tok_x = None
        if l == 0:
            early = [k for k in range(len(small[0])) if small[0][k] is not None]
            exchanges.append((small_start("early0", [small[0][k] for k in early]), [(0, k) for k in early]))
            tok_x = exchanges[-1][0]["token"]
        dp3, d_cw_s = _conv_bwd(dc1, sv["proj"], dp3, _strips(sv["cw"][::-1]), S, tm, rb, tok_x)
        g_in, = _tn_matmul("grad_w_in", sv["h"], (tt, D), lambda j, t: (t, 0), [dp3], (tt, CS), lambda j, t: (t, j),
                           (NSHARD, D, CS), (None, D, CS), lambda j, t: (j, 0, 0), NSHARD, nt)
        reduce_finish(ffn_pend, g_in)
        pending, tok = reduce_start("mix%d" % l, l, [
            ("w_in", g_in), ("w_conv_out", g_co.reshape(NSHARD, DQ, D)), ("w_sgu_out", g_so.reshape(NSHARD, DQ, D)),
            ("w_o", g_o.reshape(NSHARD, DQ, D))])
        dx, d_norm_mix = _mix_in_bwd(dx1, dp3, sv["x"], row(norm_mix, l), wt["w_in"], 0, tm, tok)
        small[l][0] = d_norm_mix
        small[l][2] = d_cw_s.transpose(1, 0, 2).reshape(32, D)
        if l > 0:
            exchanges.append((small_start("layer%d" % l, small[l]), [(l, k) for k in range(len(small[l]))]))
            tok = [tok, exchanges[-1][0]["token"]]
    reduce_finish(pending, dx)
    grad_x = dx.reshape(BL, S, D)

    names = ["w_in", "w_conv_out", "w_sgu_out", "w_o", "w_ffn_gate", "w_ffn_up", "w_ffn_down"]
    g_full = _join_halves([g_acc[nm] for nm in names])
    g_w_in, g_w_co, g_w_so, g_w_o, g_w_g, g_w_u, g_w_d = [g.reshape(L, 2 * g.shape[2], g.shape[3]) for g in g_full]
    g_w_g = jnp.swapaxes(g_w_g, 1, 2)
    g_w_u = jnp.swapaxes(g_w_u, 1, 2)

    late = [small[0][0], small[0][2], d_norm_final]
    packed = pack_rows(late)
    summed = _sum_slots(packed, _exchange_all(packed), me_idx, _row_tile(packed.shape[0], 256))
    sg = [[None] * len(small[l]) for l in range(L)]
    sg[0][0], sg[0][2], g_norm_final = unpack_rows(summed, late)
    g_norm_final = g_norm_final[0]
    for st, where in exchanges:
        for (l, k), piece in zip(where, small_finish(st, summed)):
            sg[l][k] = piece

    def per_layer(k, shape):
        return jnp.stack([sg[l][k] for l in range(L)]).reshape(shape)

    g_norm_mix = per_layer(0, (L, D))
    g_gate_bias = per_layer(1, (L, 2 * D))
    g_conv_w_full = jnp.stack([sg[l][2][:KW] for l in range(L)])
    g_conv_w = lax.dynamic_slice_in_dim(g_conv_w_full, (2 * mx + my) * CQ, CQ, axis=2).reshape(L, KW, 1, CQ)
    g_conv_b = per_layer(3, (L, D))
    g_conv_ln_g = per_layer(4, (L, D))
    g_conv_ln_b = per_layer(5, (L, D))
    g_sgu_ln_g = per_layer(6, (L, D))
    g_sgu_ln_b = per_layer(7, (L, D))
    g_w_spatial = per_layer(8, (L, G, CH, CH))
    g_b_spatial = per_layer(9, (L, G, CH))
    g_norm_ffn = per_layer(10, (L, D))

    grads = [g_norm_mix, g_w_in, g_gate_bias, g_conv_w, g_conv_b, g_conv_ln_g, g_conv_ln_b, g_w_co, g_sgu_ln_g,
             g_sgu_ln_b, g_w_spatial, g_b_spatial, g_w_so, g_w_o, g_norm_ffn, g_w_g, g_w_u, g_w_d, g_norm_final]
    weights = [norm_mix, w_in, gate_bias, conv_w, conv_b, conv_ln_g, conv_ln_b, w_conv_out, sgu_ln_g, sgu_ln_b,
               w_spatial, b_spatial, w_sgu_out, w_o, norm_ffn, w_ffn_gate, w_ffn_up, w_ffn_down, norm_final]
    ms = [m_norm_mix, m_w_in, m_gate_bias, m_conv_w, m_conv_b, m_conv_ln_g, m_conv_ln_b, m_w_conv_out, m_sgu_ln_g,
          m_sgu_ln_b, m_w_spatial, m_b_spatial, m_w_sgu_out, m_w_o, m_norm_ffn, m_w_ffn_gate, m_w_ffn_up,
          m_w_ffn_down, m_norm_final]
    vs = [v_norm_mix, v_w_in, v_gate_bias, v_conv_w, v_conv_b, v_conv_ln_g, v_conv_ln_b, v_w_conv_out, v_sgu_ln_g,
          v_sgu_ln_b, v_w_spatial, v_b_spatial, v_w_sgu_out, v_w_o, v_norm_ffn, v_w_ffn_gate, v_w_ffn_up,
          v_w_ffn_down, v_norm_final]

    big_idx = [1, 7, 12, 13, 15, 16, 17]
    transposed = [15, 16]
    deltas, new_m, new_v = [None] * 19, [None] * 19, [None] * 19
    for k in big_idx:
        shp = weights[k].shape
        r2 = (shp[0] * shp[1], shp[2])
        res = _adamw("adamw_" + str(k), weights[k].reshape(r2), grads[k].reshape(r2), ms[k].reshape(r2),
                     vs[k].reshape(r2), _row_tile(r2[0], 256), k not in transposed)
        deltas[k], new_m[k], new_v[k] = [a.reshape(shp) for a in res[:3]]
        if k not in transposed:
            grads[k] = res[3].reshape(shp)
    small_idx = [k for k in range(19) if k not in big_idx]
    pick = lambda arrs: [arrs[k].reshape(1, -1) if arrs[k].ndim == 1 else arrs[k] for k in small_idx]
    d_, m_, v_ = _adamw_many(pick(weights), pick(grads), pick(ms), pick(vs))
    for i, k in enumerate(small_idx):
        shp = weights[k].shape
        deltas[k], new_m[k], new_v[k] = d_[i].reshape(shp), m_[i].reshape(shp), v_[i].reshape(shp)

    return (loss, grad_x, *grads, *deltas, *new_m, *new_v)
```

```python
import functools

import jax
import jax.numpy as jnp
from jax import lax
from jax.experimental import pallas as pl
from jax.experimental.pallas import tpu as pltpu

F32 = jnp.float32
BF16 = jnp.bfloat16
EPS = 1e-6
ADAM_LR = 0.001
ADAM_B1 = 0.9
ADAM_B2 = 0.999
ADAM_EPS = 1e-08
ADAM_WD = 0.01
ADAM_STEP = 10

NSHARD = 4
LANES = 128
HALO = 16
VMEM_LIMIT = 60 * 1024 * 1024
MESH_ID = pl.DeviceIdType.MESH


def _dot(a, b):
    return jnp.dot(a, b, preferred_element_type=F32)


def _dot_nt(a, b):
    return lax.dot_general(a, b, (((1,), (1,)), ((), ())), preferred_element_type=F32)


def _dot_tn(a, b):
    return lax.dot_general(a, b, (((0,), (0,)), ((), ())), preferred_element_type=F32)


def _sig(z):
    return 1.0 / (1.0 + jnp.exp(-z))


def _res(shape, imap=None):
    nd = len(shape)
    if imap is None:
        imap = lambda *_: (0,) * nd
    return pl.BlockSpec(shape, imap, pipeline_mode=pl.Buffered(1))


def _cparams(sem):
    return pltpu.CompilerParams(dimension_semantics=sem, vmem_limit_bytes=VMEM_LIMIT)


def _sds(shape, dtype):
    return jax.ShapeDtypeStruct(shape, dtype)


def _after(body, n_in, dep):
    deps = [] if dep is None else [d for d in (dep if isinstance(dep, (list, tuple)) else [dep]) if d is not None]
    if not deps:
        return body, [], []

    def wrapped(*refs):
        return body(*refs[:n_in], *refs[n_in + len(deps):])

    return wrapped, [pl.BlockSpec(memory_space=pl.ANY)] * len(deps), deps


def _mix_in_fwd(x2d, g_mix, w_in_g, layer, tm, dep=None):
    T, D = x2d.shape
    CS = w_in_g.shape[2]
    CN = CS // 3

    def body(x_ref, g_ref, w_ref, h_ref, p_ref):
        x = x_ref[...]
        rstd = lax.rsqrt(jnp.mean(x * x, axis=-1, keepdims=True) + EPS)
        h = (x * rstd * g_ref[...]).astype(BF16)
        h_ref[...] = h
        for s in range(NSHARD):
            for j in range(3):
                c0 = s * CS + j * CN
                p_ref[:, c0:c0 + CN] = _dot(h, w_ref[s, :, j * CN:(j + 1) * CN]).astype(BF16)

    body, dep_spec, dep_arg = _after(body, 3, dep)
    return pl.pallas_call(
        body, name="mix_in_fwd", grid=(T // tm,),
        in_specs=[pl.BlockSpec((tm, D), lambda i: (i, 0)), _res((1, D)),
                  _res((NSHARD, D, CS), lambda i: (0, layer, 0))] + dep_spec,
        out_specs=[pl.BlockSpec((tm, D), lambda i: (i, 0)), pl.BlockSpec((tm, NSHARD * CS), lambda i: (i, 0))],
        out_shape=[_sds((T, D), BF16), _sds((T, NSHARD * CS), BF16)],
        compiler_params=_cparams(("parallel",)),
    )(x2d, g_mix, w_in_g, *dep_arg)


def _halo_maps(tm, n_rows):
    nb = tm // HALO
    last = n_rows // HALO - 1
    prev = lambda i: (jnp.maximum(i * nb - 1, 0), 0)
    nxt = lambda i: (jnp.minimum((i + 1) * nb, last), 0)
    return prev, nxt


def _dwconv(pad_ref, w_ref, out_ref, n_strips, tm, kw, rb):
    off = HALO - (kw - 1) // 2

    def strip(cs, carry):
        for r0 in range(0, tm, rb):
            acc = jnp.zeros((rb, LANES), F32)
            for k in range(kw):
                r = r0 + off + k
                acc = acc + w_ref[cs, k:k + 1, :] * pad_ref[cs, r:r + rb, :]
            out_ref[cs, r0:r0 + rb, :] = acc
        return carry

    lax.fori_loop(0, n_strips, strip, 0)


def _fill_c0_pad(pad_ref, pa_ref, pprev_ref, pnext_ref, D, tm, first, last):
    for cs in range(D // LANES):
        lo, hi = cs * LANES, (cs + 1) * LANES

        def c0_of(ref):
            return ref[:, lo:hi].astype(F32) * _sig(ref[:, D + lo:D + hi].astype(F32))

        pad_ref[cs, HALO:HALO + tm, :] = c0_of(pa_ref)
        pad_ref[cs, 0:HALO, :] = jnp.where(first, 0.0, c0_of(pprev_ref))
        pad_ref[cs, HALO + tm:HALO + tm + HALO, :] = jnp.where(last, 0.0, c0_of(pnext_ref))


def _conv_fwd(proj, conv_w_s, conv_b, ln_g, ln_b, w_sq_g, layer, seq, tm, rb):
    T = proj.shape[0]
    D = conv_b.shape[1]
    DQ = D // NSHARD
    NSTR = D // LANES
    KW = 31
    tps = seq // tm
    prev, nxt = _halo_maps(tm, T)

    def body(pa_ref, pprev_ref, pnext_ref, w_ref, b_ref, g_ref, be_ref, wco_ref,
             c1h_ref, rstd_ref, c3_ref, ya_ref, pad_ref, c1s_ref):
        i = pl.program_id(0)
        first = (i % tps) == 0
        last = (i % tps) == tps - 1
        _fill_c0_pad(pad_ref, pa_ref, pprev_ref, pnext_ref, D, tm, first, last)
        _dwconv(pad_ref, w_ref, c1s_ref, NSTR, tm, KW, rb)
        c1 = jnp.concatenate([c1s_ref[cs] for cs in range(NSTR)], axis=1) + b_ref[...]
        mu = jnp.mean(c1, axis=-1, keepdims=True)
        cc = c1 - mu
        rstd = lax.rsqrt(jnp.mean(cc * cc, axis=-1, keepdims=True) + EPS)
        c1h = cc * rstd
        c1h_ref[...] = c1h.astype(BF16)
        rstd_ref[...] = rstd
        c2 = c1h * g_ref[...] + be_ref[...]
        c3 = (c2 * _sig(c2)).astype(BF16)
        c3_ref[...] = c3
        ya_ref[...] = _dot(c3, wco_ref[...].reshape(D, D)).astype(BF16)

    row = lambda i: (i, 0)
    return pl.pallas_call(
        body, name="conv_fwd", grid=(T // tm,),
        in_specs=[pl.BlockSpec((tm, 2 * D), row), pl.BlockSpec((HALO, 2 * D), prev), pl.BlockSpec((HALO, 2 * D), nxt),
                  _res((NSTR, 32, LANES)), _res((1, D)), _res((1, D)), _res((1, D)),
                  _res((NSHARD, DQ, D), lambda i: (0, layer * 3 + 0, 0))],
        out_specs=[pl.BlockSpec((tm, D), row), pl.BlockSpec((tm, 1), row), pl.BlockSpec((tm, D), row),
                   pl.BlockSpec((tm, D), row)],
        out_shape=[_sds((T, D), BF16), _sds((T, 1), F32), _sds((T, D), BF16), _sds((T, D), BF16)],
        scratch_shapes=[pltpu.VMEM((NSTR, tm + 2 * HALO, LANES), F32), pltpu.VMEM((NSTR, tm, LANES), F32)],
        compiler_params=_cparams(("parallel",)),
    )(proj, proj, proj, conv_w_s, conv_b, ln_g, ln_b, w_sq_g)


def _sgu_merge_fwd(proj, ya, x2d, ln_g, ln_b, ws_b, bs_b, gate_bias, w_sq_g, layer, tm):
    T, D = x2d.shape
    DQ = D // NSHARD
    G, CH, _ = ws_b.shape
    GD = D // G

    def body(puv_ref, pg_ref, ya_ref, x_ref, g_ref, be_ref, ws_ref, bsb_ref, gb_ref, wso_ref, wo_ref,
             mixed_ref, gated_ref, yb_ref, merged_ref, x1_ref, mix_scr):
        u = puv_ref[:, :D].astype(F32)
        v = puv_ref[:, D:].astype(F32)
        mu = jnp.mean(v, axis=-1, keepdims=True)
        vc = v - mu
        rstd = lax.rsqrt(jnp.mean(vc * vc, axis=-1, keepdims=True) + EPS)
        vn = (vc * rstd * g_ref[...] + be_ref[...]).astype(BF16)
        nch = tm // CH
        for g in range(G):
            cols = slice(g * GD, (g + 1) * GD)
            rhs = jnp.concatenate([vn[ch * CH:(ch + 1) * CH, cols] for ch in range(nch)], axis=1)
            res = _dot(ws_ref[g], rhs)
            for ch in range(nch):
                mix_scr[ch * CH:(ch + 1) * CH, cols] = res[:, ch * GD:(ch + 1) * GD] + bsb_ref[:, cols]
        mixed = mix_scr[...]
        mixed_ref[...] = mixed.astype(BF16)
        gated = (u * mixed).astype(BF16)
        gated_ref[...] = gated
        yb = _dot(gated, wso_ref[...].reshape(D, D))
        yb_ref[...] = yb.astype(BF16)
        sa = _sig(pg_ref[:, :D].astype(F32) + gb_ref[:, :D])
        sb = _sig(pg_ref[:, D:].astype(F32) + gb_ref[:, D:])
        merged = (sa * ya_ref[...].astype(F32) + sb * yb).astype(BF16)
        merged_ref[...] = merged
        x1_ref[...] = x_ref[...] + _dot(merged, wo_ref[...].reshape(D, D))

    row = lambda i: (i, 0)
    return pl.pallas_call(
        body, name="sgu_merge_fwd", grid=(T // tm,),
        in_specs=[pl.BlockSpec((tm, 2 * D), lambda i: (i, 1)), pl.BlockSpec((tm, 2 * D), lambda i: (i, 2)),
                  pl.BlockSpec((tm, D), row), pl.BlockSpec((tm, D), row),
                  _res((1, D)), _res((1, D)), _res((G, CH, CH)), _res((CH, D)), _res((1, 2 * D)),
                  _res((NSHARD, DQ, D), lambda i: (0, layer * 3 + 1, 0)),
                  _res((NSHARD, DQ, D), lambda i: (0, layer * 3 + 2, 0))],
        out_specs=[pl.BlockSpec((tm, D), row)] * 5,
        out_shape=[_sds((T, D), BF16)] * 4 + [_sds((T, D), F32)],
        scratch_shapes=[pltpu.VMEM((tm, D), F32)],
        compiler_params=_cparams(("parallel",)),
    )(proj, proj, ya, x2d, ln_g, ln_b, ws_b, bs_b, gate_bias, w_sq_g, w_sq_g)


def _ffn_chunks(F):
    assert F % 256 == 0, F
    return [(c0, min(512, F - c0)) for c0 in range(0, F, 512)]


def _ffn_fwd(x1, g_ffn, wgt, wut, wd, tm):
    T, D = x1.shape
    F = wd.shape[0]

    def body(x_ref, g_ref, wg_ref, wu_ref, wd_ref, h2_ref, gt_ref, up_ref, act_ref, x2_ref):
        x = x_ref[...]
        rstd = lax.rsqrt(jnp.mean(x * x, axis=-1, keepdims=True) + EPS)
        h2 = (x * rstd * g_ref[...]).astype(BF16)
        h2_ref[...] = h2
        acc = x
        chunks = _ffn_chunks(F)

        def gate_up(c0, cw):
            return _dot_nt(h2, wg_ref[c0:c0 + cw, :]), _dot_nt(h2, wu_ref[c0:c0 + cw, :])

        nxt = gate_up(*chunks[0])
        for ci, (c0, cw) in enumerate(chunks):
            gt, up = nxt
            if ci + 1 < len(chunks):
                nxt = gate_up(*chunks[ci + 1])
            gt_ref[:, c0:c0 + cw] = gt.astype(BF16)
            up_ref[:, c0:c0 + cw] = up.astype(BF16)
            act = (gt * _sig(gt) * up).astype(BF16)
            act_ref[:, c0:c0 + cw] = act
            acc = acc + _dot(act, wd_ref[c0:c0 + cw, :])
        x2_ref[...] = acc

    row = lambda i: (i, 0)
    return pl.pallas_call(
        body, name="ffn_fwd", grid=(T // tm,),
        in_specs=[pl.BlockSpec((tm, D), row), _res((1, D)), _res((F, D)), _res((F, D)), _res((F, D))],
        out_specs=[pl.BlockSpec((tm, D), row), pl.BlockSpec((tm, F), row), pl.BlockSpec((tm, F), row),
                   pl.BlockSpec((tm, F), row), pl.BlockSpec((tm, D), row)],
        out_shape=[_sds((T, D), BF16), _sds((T, F), BF16), _sds((T, F), BF16), _sds((T, F), BF16), _sds((T, D), F32)],
        compiler_params=_cparams(("parallel",)),
    )(x1, g_ffn, wgt, wut, wd)


def _loss_head(xf, g_fin, target, tm):
    T, D = xf.shape
    n = T // tm

    def body(x_ref, g_ref, t_ref, dx_ref, loss_ref, dg_ref, acc_ref):
        i = pl.program_id(0)

        @pl.when(i == 0)
        def _():
            acc_ref[...] = jnp.zeros_like(acc_ref)
            dg_ref[...] = jnp.zeros_like(dg_ref)

        x = x_ref[...]
        g = g_ref[...]
        rstd = lax.rsqrt(jnp.mean(x * x, axis=-1, keepdims=True) + EPS)
        xh = x * rstd
        diff = xh * g - t_ref[...]
        acc_ref[...] += jnp.sum(diff * diff, axis=0, keepdims=True)
        dy = diff * (1.0 / D)
        dg_ref[...] += jnp.sum(dy * xh, axis=0, keepdims=True)
        dxh = dy * g
        dx_ref[...] = rstd * (dxh - xh * jnp.mean(dxh * xh, axis=-1, keepdims=True))

        @pl.when(i == n - 1)
        def _():
            tot = jnp.sum(acc_ref[...], axis=-1, keepdims=True) * (0.5 / D)
            loss_ref[...] = jnp.broadcast_to(tot, loss_ref.shape)

    row = lambda i: (i, 0)
    return pl.pallas_call(
        body, name="loss_head", grid=(n,),
        in_specs=[pl.BlockSpec((tm, D), row), _res((1, D)), pl.BlockSpec((tm, D), row)],
        out_specs=[pl.BlockSpec((tm, D), row), pl.BlockSpec((1, LANES), lambda i: (0, 0)),
                   pl.BlockSpec((1, D), lambda i: (0, 0))],
        out_shape=[_sds((T, D), F32), _sds((1, LANES), F32), _sds((1, D), F32)],
        scratch_shapes=[pltpu.VMEM((1, D), F32)],
        compiler_params=_cparams(("arbitrary",)),
    )(xf, g_fin, target)


def _ffn_bwd(dx2, x1, gt, up, g_ffn, wgt, wut, wd, tm, dep=None):
    T, D = x1.shape
    F = wd.shape[0]

    def body(dx2_ref, x1_ref, gt_ref, up_ref, g_ref, wg_ref, wu_ref, wd_ref, dx1_ref, dgt_ref, dup_ref, dg_ref):
        i = pl.program_id(0)

        @pl.when(i == 0)
        def _():
            dg_ref[...] = jnp.zeros_like(dg_ref)

        dx2 = dx2_ref[...]
        dx2b = dx2.astype(BF16)
        dh2 = jnp.zeros((tm, D), F32)
        chunks = _ffn_chunks(F)
        dact_next = _dot_nt(dx2b, wd_ref[0:chunks[0][1], :])
        for ci, (c0, cw) in enumerate(chunks):
            dact = dact_next
            if ci + 1 < len(chunks):
                n0, nw = chunks[ci + 1]
                dact_next = _dot_nt(dx2b, wd_ref[n0:n0 + nw, :])
            g = gt_ref[:, c0:c0 + cw].astype(F32)
            u = up_ref[:, c0:c0 + cw].astype(F32)
            sg = _sig(g)
            dup = (dact * (g * sg)).astype(BF16)
            dgt = (dact * u * (sg * (1.0 + g * (1.0 - sg)))).astype(BF16)
            dgt_ref[:, c0:c0 + cw] = dgt
            dup_ref[:, c0:c0 + cw] = dup
            dh2 = dh2 + _dot(dgt, wg_ref[c0:c0 + cw, :]) + _dot(dup, wu_ref[c0:c0 + cw, :])
        x = x1_ref[...]
        rstd = lax.rsqrt(jnp.mean(x * x, axis=-1, keepdims=True) + EPS)
        xh = x * rstd
        dg_ref[...] += jnp.sum(dh2 * xh, axis=0, keepdims=True)
        dxh = dh2 * g_ref[...]
        dx1_ref[...] = dx2 + rstd * (dxh - xh * jnp.mean(dxh * xh, axis=-1, keepdims=True))

    row = lambda i: (i, 0)
    body, dep_spec, dep_arg = _after(body, 8, dep)
    return pl.pallas_call(
        body, name="ffn_bwd", grid=(T // tm,),
        in_specs=[pl.BlockSpec((tm, D), row), pl.BlockSpec((tm, D), row), pl.BlockSpec((tm, F), row),
                  pl.BlockSpec((tm, F), row), _res((1, D)), _res((F, D)), _res((F, D)), _res((F, D))] + dep_spec,
        out_specs=[pl.BlockSpec((tm, D), row), pl.BlockSpec((tm, F), row), pl.BlockSpec((tm, F), row),
                   pl.BlockSpec((1, D), lambda i: (0, 0))],
        out_shape=[_sds((T, D), F32), _sds((T, F), BF16), _sds((T, F), BF16), _sds((1, D), F32)],
        compiler_params=_cparams(("arbitrary",)),
    )(dx2, x1, gt, up, g_ffn, wgt, wut, wd, *dep_arg)


def _merge_sgu_bwd(dx1, proj, ya, yb, mixed, ln_g, ln_b, ws_b, wst_b, gate_bias, w_sq_g, layer, tm, dep=None):
    T, D = dx1.shape
    DQ = D // NSHARD
    G, CH, _ = ws_b.shape
    GD = D // G

    def body(dx1_ref, puv_ref, pg_ref, ya_ref, yb_ref, mixed_ref, g_ref, be_ref, ws_ref, wst_ref, gb_ref,
             wso_ref, wo_ref, dya_ref, dyb_ref, dp_ref, dgb_ref, dlg_ref, dlb_ref, dbs_ref, dws_ref,
             dvn_scr, dbs_scr):
        i = pl.program_id(0)

        @pl.when(i == 0)
        def _():
            for r in (dgb_ref, dlg_ref, dlb_ref, dws_ref, dbs_scr):
                r[...] = jnp.zeros_like(r)

        dmerged = _dot_nt(dx1_ref[...].astype(BF16), wo_ref[...].reshape(D, D))
        sa = _sig(pg_ref[:, :D].astype(F32) + gb_ref[:, :D])
        sb = _sig(pg_ref[:, D:].astype(F32) + gb_ref[:, D:])
        dya = (dmerged * sa).astype(BF16)
        dyb = (dmerged * sb).astype(BF16)
        dya_ref[...] = dya
        dyb_ref[...] = dyb
        dga = dmerged * ya_ref[...].astype(F32) * (sa * (1.0 - sa))
        dgb = dmerged * yb_ref[...].astype(F32) * (sb * (1.0 - sb))
        dp_ref[:, 4 * D:5 * D] = dga.astype(BF16)
        dp_ref[:, 5 * D:6 * D] = dgb.astype(BF16)
        dgb_ref[:, :D] += jnp.sum(dga, axis=0, keepdims=True)
        dgb_ref[:, D:] += jnp.sum(dgb, axis=0, keepdims=True)

        dgated = _dot_nt(dyb, wso_ref[...].reshape(D, D))
        u = puv_ref[:, :D].astype(F32)
        v = puv_ref[:, D:].astype(F32)
        dp_ref[:, 2 * D:3 * D] = (dgated * mixed_ref[...].astype(F32)).astype(BF16)
        dmixed = dgated * u
        mu = jnp.mean(v, axis=-1, keepdims=True)
        vc = v - mu
        rstd = lax.rsqrt(jnp.mean(vc * vc, axis=-1, keepdims=True) + EPS)
        vh = vc * rstd
        vn = (vh * g_ref[...] + be_ref[...]).astype(BF16)
        dmb = dmixed.astype(BF16)
        nch = tm // CH
        bs_part = dmixed[0:CH, :]
        for ch in range(1, nch):
            bs_part = bs_part + dmixed[ch * CH:(ch + 1) * CH, :]
        dbs_scr[...] += bs_part
        for g in range(G):
            cols = slice(g * GD, (g + 1) * GD)
            dm_g = jnp.concatenate([dmb[ch * CH:(ch + 1) * CH, cols] for ch in range(nch)], axis=1)
            vn_g = jnp.concatenate([vn[ch * CH:(ch + 1) * CH, cols] for ch in range(nch)], axis=1)
            dws_ref[g] += _dot_nt(dm_g, vn_g)
            dvn_g = _dot(wst_ref[g], dm_g)
            for ch in range(nch):
                dvn_scr[ch * CH:(ch + 1) * CH, cols] = dvn_g[:, ch * GD:(ch + 1) * GD]
        dvn = dvn_scr[...]
        dlg_ref[...] += jnp.sum(dvn * vh, axis=0, keepdims=True)
        dlb_ref[...] += jnp.sum(dvn, axis=0, keepdims=True)
        dxh = dvn * g_ref[...]
        dv = rstd * (dxh - jnp.mean(dxh, axis=-1, keepdims=True) - vh * jnp.mean(dxh * vh, axis=-1, keepdims=True))
        dp_ref[:, 3 * D:4 * D] = dv.astype(BF16)

        @pl.when(i == pl.num_programs(0) - 1)
        def _():
            for g in range(G):
                blk = dbs_scr[:, g * GD:(g + 1) * GD]
                if GD != CH:
                    blk = jnp.concatenate([blk, jnp.zeros((CH, CH - GD), F32)], axis=1)
                dbs_ref[:, g * CH:(g + 1) * CH] = jnp.sum(blk.T, axis=0, keepdims=True)

    row = lambda i: (i, 0)
    fixed2 = lambda i: (0, 0)
    body, dep_spec, dep_arg = _after(body, 13, dep)
    return pl.pallas_call(
        body, name="merge_sgu_bwd", grid=(T // tm,),
        in_specs=[pl.BlockSpec((tm, D), row), pl.BlockSpec((tm, 2 * D), lambda i: (i, 1)),
                  pl.BlockSpec((tm, 2 * D), lambda i: (i, 2)), pl.BlockSpec((tm, D), row), pl.BlockSpec((tm, D), row),
                  pl.BlockSpec((tm, D), row), _res((1, D)), _res((1, D)), _res((G, CH, CH)), _res((G, CH, CH)),
                  _res((1, 2 * D)),
                  _res((NSHARD, DQ, D), lambda i: (0, layer * 3 + 1, 0)),
                  _res((NSHARD, DQ, D), lambda i: (0, layer * 3 + 2, 0))] + dep_spec,
        out_specs=[pl.BlockSpec((tm, D), row), pl.BlockSpec((tm, D), row),
                   pl.BlockSpec((tm, 6 * D), row),
                   pl.BlockSpec((1, 2 * D), fixed2), pl.BlockSpec((1, D), fixed2), pl.BlockSpec((1, D), fixed2),
                   pl.BlockSpec((1, G * CH), fixed2), pl.BlockSpec((G, CH, CH), lambda i: (0, 0, 0))],
        out_shape=[_sds((T, D), BF16), _sds((T, D), BF16), _sds((T, 6 * D), BF16),
                   _sds((1, 2 * D), F32), _sds((1, D), F32), _sds((1, D), F32), _sds((1, G * CH), F32),
                   _sds((G, CH, CH), F32)],
        scratch_shapes=[pltpu.VMEM((tm, D), F32), pltpu.VMEM((CH, D), F32)],
        compiler_params=_cparams(("arbitrary",)),
    )(dx1, proj, proj, ya, yb, mixed, ln_g, ln_b, ws_b, wst_b, gate_bias, w_sq_g, w_sq_g, *dep_arg)


def _conv_ln_bwd(dya, c1h, rstd_c, ln_g, ln_b, w_sq_g, layer, tm):
    T, D = dya.shape
    DQ = D // NSHARD

    def body(dya_ref, c1h_ref, rstd_ref, g_ref, be_ref, wco_ref, dc1_ref, dlg_ref, dlb_ref, dcb_ref):
        i = pl.program_id(0)

        @pl.when(i == 0)
        def _():
            for r in (dlg_ref, dlb_ref, dcb_ref):
                r[...] = jnp.zeros_like(r)

        dc3 = _dot_nt(dya_ref[...], wco_ref[...].reshape(D, D))
        c1h = c1h_ref[...].astype(F32)
        c2 = c1h * g_ref[...] + be_ref[...]
        sg = _sig(c2)
        dc2 = dc3 * (sg * (1.0 + c2 * (1.0 - sg)))
        dlg_ref[...] += jnp.sum(dc2 * c1h, axis=0, keepdims=True)
        dlb_ref[...] += jnp.sum(dc2, axis=0, keepdims=True)
        dxh = dc2 * g_ref[...]
        dc1 = rstd_ref[...] * (dxh - jnp.mean(dxh, axis=-1, keepdims=True)
                               - c1h * jnp.mean(dxh * c1h, axis=-1, keepdims=True))
        dc1_ref[...] = dc1
        dcb_ref[...] += jnp.sum(dc1, axis=0, keepdims=True)

    row = lambda i: (i, 0)
    fixed2 = lambda i: (0, 0)
    return pl.pallas_call(
        body, name="conv_ln_bwd", grid=(T // tm,),
        in_specs=[pl.BlockSpec((tm, D), row), pl.BlockSpec((tm, D), row), pl.BlockSpec((tm, 1), row),
                  _res((1, D)), _res((1, D)), _res((NSHARD, DQ, D), lambda i: (0, layer * 3 + 0, 0))],
        out_specs=[pl.BlockSpec((tm, D), row), pl.BlockSpec((1, D), fixed2), pl.BlockSpec((1, D), fixed2),
                   pl.BlockSpec((1, D), fixed2)],
        out_shape=[_sds((T, D), F32), _sds((1, D), F32), _sds((1, D), F32), _sds((1, D), F32)],
        compiler_params=_cparams(("arbitrary",)),
    )(dya, c1h, rstd_c, ln_g, ln_b, w_sq_g)


def _conv_bwd(dc1, proj, dp3, conv_wf_s, seq, tm, rb, dep=None):
    T, D = dc1.shape
    NSTR = D // LANES
    KW = 31
    PADK = (KW - 1) // 2
    tps = seq // tm
    prev, nxt = _halo_maps(tm, T)
    n = T // tm

    def body(dc_ref, dcprev_ref, dcnext_ref, pa_ref, pprev_ref, pnext_ref, wf_ref, dp_in_ref,
             dp_ref, dw_ref, pad_ref, dpad_ref, dc0_ref, dwacc_ref):
        del dp_in_ref
        i = pl.program_id(0)
        first = (i % tps) == 0
        last = (i % tps) == tps - 1

        @pl.when(i == 0)
        def _():
            dwacc_ref[...] = jnp.zeros_like(dwacc_ref)

        _fill_c0_pad(pad_ref, pa_ref, pprev_ref, pnext_ref, D, tm, first, last)
        for cs in range(NSTR):
            lo, hi = cs * LANES, (cs + 1) * LANES
            dpad_ref[cs, HALO:HALO + tm, :] = dc_ref[:, lo:hi]
            dpad_ref[cs, 0:HALO, :] = jnp.where(first, 0.0, dcprev_ref[:, lo:hi])
            dpad_ref[cs, HALO + tm:HALO + tm + HALO, :] = jnp.where(last, 0.0, dcnext_ref[:, lo:hi])
        _dwconv(dpad_ref, wf_ref, dc0_ref, NSTR, tm, KW, rb)

        def strip(cs, carry):
            for r0 in range(0, tm, rb):
                d = dpad_ref[cs, HALO + r0:HALO + r0 + rb, :]
                for k in range(KW):
                    r = r0 + HALO - PADK + k
                    prod = d * pad_ref[cs, r:r + rb, :]
                    dwacc_ref[cs, k * 8:(k + 1) * 8, :] += jnp.sum(prod.reshape(rb // 8, 8, LANES), axis=0)
            return carry

        lax.fori_loop(0, NSTR, strip, 0)

        for cs in range(NSTR):
            lo, hi = cs * LANES, (cs + 1) * LANES
            av = pa_ref[:, lo:hi].astype(F32)
            sg = _sig(pa_ref[:, D + lo:D + hi].astype(F32))
            dc0 = dc0_ref[cs]
            dp_ref[:, lo:hi] = (dc0 * sg).astype(BF16)
            dp_ref[:, D + lo:D + hi] = (dc0 * av * (sg * (1.0 - sg))).astype(BF16)

        @pl.when(i == n - 1)
        def _():
            for cs in range(NSTR):
                dw_ref[cs] = jnp.sum(dwacc_ref[cs].reshape(32, 8, LANES), axis=1)

    row = lambda i: (i, 0)
    body, dep_spec, dep_arg = _after(body, 8, dep)
    return pl.pallas_call(
        body, name="conv_bwd", grid=(n,),
        in_specs=[pl.BlockSpec((tm, D), row), pl.BlockSpec((HALO, D), prev), pl.BlockSpec((HALO, D), nxt),
                  pl.BlockSpec((tm, 2 * D), row), pl.BlockSpec((HALO, 2 * D), prev), pl.BlockSpec((HALO, 2 * D), nxt),
                  _res((NSTR, 32, LANES)), pl.BlockSpec(memory_space=pl.ANY)] + dep_spec,
        out_specs=[pl.BlockSpec((tm, 2 * D), row),
                   pl.BlockSpec((NSTR, 32, LANES), lambda i: (0, 0, 0))],
        out_shape=[_sds(dp3.shape, BF16), _sds((NSTR, 32, LANES), F32)],
        scratch_shapes=[pltpu.VMEM((NSTR, tm + 2 * HALO, LANES), F32), pltpu.VMEM((NSTR, tm + 2 * HALO, LANES), F32),
                        pltpu.VMEM((NSTR, tm, LANES), F32), pltpu.VMEM((NSTR, 32 * 8, LANES), F32)],
        input_output_aliases={7: 0},
        compiler_params=_cparams(("arbitrary",)),
    )(dc1, dc1, dc1, proj, proj, proj, conv_wf_s, dp3, *dep_arg)


def _mix_in_bwd(dx1, dp3, x2d, g_mix, w_in_g, layer, tm, dep=None):
    T, D = x2d.shape
    CS = w_in_g.shape[2]
    CN = CS // 3

    def body(dx1_ref, dp_ref, x_ref, g_ref, w_ref, dx_ref, dg_ref):
        i = pl.program_id(0)

        @pl.when(i == 0)
        def _():
            dg_ref[...] = jnp.zeros_like(dg_ref)

        dh = jnp.zeros((tm, D), F32)
        for j in range(12):
            dh = dh + _dot_nt(dp_ref[:, j * CN:(j + 1) * CN], w_ref[j // 3, :, (j % 3) * CN:(j % 3 + 1) * CN])
        x = x_ref[...]
        rstd = lax.rsqrt(jnp.mean(x * x, axis=-1, keepdims=True) + EPS)
        xh = x * rstd
        dg_ref[...] += jnp.sum(dh * xh, axis=0, keepdims=True)
        dxh = dh * g_ref[...]
        dx_ref[...] = dx1_ref[...] + rstd * (dxh - xh * jnp.mean(dxh * xh, axis=-1, keepdims=True))

    row = lambda i: (i, 0)
    body, dep_spec, dep_arg = _after(body, 5, dep)
    return pl.pallas_call(
        body, name="mix_in_bwd", grid=(T // tm,),
        in_specs=[pl.BlockSpec((tm, D), row), pl.BlockSpec((tm, 6 * D), row),
                  pl.BlockSpec((tm, D), row), _res((1, D)), _res((NSHARD, D, CS), lambda i: (0, layer, 0))] + dep_spec,
        out_specs=[pl.BlockSpec((tm, D), row), pl.BlockSpec((1, D), lambda i: (0, 0))],
        out_shape=[_sds((T, D), F32), _sds((1, D), F32)],
        compiler_params=_cparams(("arbitrary",)),
    )(dx1, dp3, x2d, g_mix, w_in_g, *dep_arg)


def _tn_matmul(name, a, a_block, a_map, bs, b_block, b_map, out_shape, out_block, out_map, nj, nt):
    kk = [d for d in a_block if d is not None][-1]
    nn = [d for d in b_block if d is not None][-1]
    nb = len(bs)
    a_list = a if isinstance(a, (list, tuple)) else [a]
    na = len(a_list)

    def body(*refs):
        a_refs, b_refs = refs[:na], refs[na:na + nb]
        o_refs, acc_refs = refs[-2 * nb:-nb], refs[-nb:]
        t = pl.program_id(1)

        @pl.when(t == 0)
        def _():
            for acc_ref in acc_refs:
                acc_ref[...] = jnp.zeros_like(acc_ref)

        a_ts = [a_ref[...].astype(BF16) for a_ref in a_refs]
        for i, (b_ref, acc_ref) in enumerate(zip(b_refs, acc_refs)):
            acc_ref[...] += _dot_tn(a_ts[i % na], b_ref[...].astype(BF16))

        @pl.when(t == nt - 1)
        def _():
            for o_ref, acc_ref in zip(o_refs, acc_refs):
                o_ref[...] = acc_ref[...].astype(o_ref.dtype)

    return pl.pallas_call(
        body, name=name, grid=(nj, nt),
        in_specs=[pl.BlockSpec(a_block, a_map)] * na + [pl.BlockSpec(b_block, b_map)] * nb,
        out_specs=[pl.BlockSpec(out_block, out_map)] * nb, out_shape=[_sds(out_shape, BF16)] * nb,
        scratch_shapes=[pltpu.VMEM((kk, nn), F32)] * nb,
        compiler_params=_cparams(("parallel", "arbitrary")),
    )(*a_list, *bs)


def _place_shard(name, w, pos, dtype, tr, dep=None, layer=None, into=None, row_off=0, out_rows=None):
    R, C = w.shape[-2:]
    out_rows = out_rows or R

    def body(pos_ref, w_ref, *rest):
        del pos_ref
        rest[-1][...] = w_ref[...].astype(dtype)

    in_spec = (pl.BlockSpec((tr, C), lambda r, pos: (r, 0)) if layer is None else
               pl.BlockSpec((None, tr, C), lambda r, pos: (layer, r, 0)))
    extra, extra_args = ([], []) if into is None else ([pl.BlockSpec(memory_space=pl.ANY)], [into])
    body, dep_spec, dep_arg = _after(body, 2 + len(extra), dep)
    grid_spec = pltpu.PrefetchScalarGridSpec(
        num_scalar_prefetch=1, grid=(R // tr,), in_specs=[in_spec] + extra + dep_spec,
        out_specs=pl.BlockSpec((None, tr, C), lambda r, pos: (pos[1], row_off // tr + r, 0)))
    return pl.pallas_call(body, name=name, grid_spec=grid_spec, out_shape=_sds((NSHARD, out_rows, C), dtype),
                          input_output_aliases={} if into is None else {2: 0},
                          compiler_params=_cparams(("parallel",)))(pos, w, *extra_args, *dep_arg)


def _add_halves(name, g, rbuf, pos, tr):
    NS, _, H, C = g.shape

    def body(pos_ref, g_ref, r_ref, o_ref):
        del pos_ref
        o_ref[...] = (g_ref[...].astype(F32) + r_ref[...].astype(F32)).astype(BF16)

    grid_spec = pltpu.PrefetchScalarGridSpec(
        num_scalar_prefetch=1, grid=(NS, H // tr),
        in_specs=[pl.BlockSpec((None, None, tr, C), lambda s, r, pos: (s, pos[0], r, 0)),
                  pl.BlockSpec((None, tr, C), lambda s, r, pos: (s, r, 0))],
        out_specs=pl.BlockSpec((None, tr, C), lambda s, r, pos: (s, r, 0)))
    return pl.pallas_call(body, name=name, grid_spec=grid_spec, out_shape=_sds((NS, H, C), BF16),
                          compiler_params=_cparams(("parallel", "parallel")))(pos, g, rbuf)


def _add_shards(name, p, rbuf, pos, tr, layer, n_layers, prev):
    _, H, C = p.shape

    def body(pos_ref, p_ref, r_ref, *rest):
        del pos_ref
        o_ref = rest[-1]
        acc = p_ref[...].astype(F32)
        for j in range(3):
            acc = acc + r_ref[j].astype(F32)
        o_ref[...] = acc

    in_specs = [pl.BlockSpec((None, tr, C), lambda r, pos: (pos[1], r, 0)),
                pl.BlockSpec((3, tr, C), lambda r, pos: (0, r, 0))]
    args = [pos, p, rbuf]
    aliases = {}
    if prev is not None:
        in_specs.append(pl.BlockSpec(memory_space=pl.ANY))
        args.append(prev)
        aliases = {3: 0}
    grid_spec = pltpu.PrefetchScalarGridSpec(
        num_scalar_prefetch=1, grid=(H // tr,), in_specs=in_specs,
        out_specs=pl.BlockSpec((None, None, tr, C), lambda r, pos: (layer, pos[0], r, 0)))
    return pl.pallas_call(body, name=name, grid_spec=grid_spec, out_shape=_sds((n_layers, 2, H, C), F32),
                          input_output_aliases=aliases, compiler_params=_cparams(("parallel",)))(*args)


def _sum_slots(own, land, me, tr):
    NS8, R, C = land.shape

    def body(me_ref, own_ref, l_ref, o_ref):
        acc = None
        for j in range(NS8):
            term = jnp.where(me_ref[0] == j, own_ref[...], l_ref[j])
            acc = term if acc is None else acc + term
        o_ref[...] = acc

    grid_spec = pltpu.PrefetchScalarGridSpec(
        num_scalar_prefetch=1, grid=(R // tr,),
        in_specs=[pl.BlockSpec((tr, C), lambda i, me: (i, 0)), pl.BlockSpec((NS8, tr, C), lambda i, me: (0, i, 0))],
        out_specs=pl.BlockSpec((tr, C), lambda i, me: (i, 0)))
    return pl.pallas_call(body, name="sum_slots", grid_spec=grid_spec, out_shape=_sds((R, C), F32),
                          compiler_params=_cparams(("parallel",)))(me, own, land)


def _adamw_update(w_ref, g_ref, m_ref, v_ref, d_ref, mo_ref, vo_ref):
    g_ = g_ref[...]
    m_ = ADAM_B1 * m_ref[...] + (1.0 - ADAM_B1) * g_
    v_ = ADAM_B2 * v_ref[...] + (1.0 - ADAM_B2) * (g_ * g_)
    mo_ref[...] = m_
    vo_ref[...] = v_
    m_hat = m_ / (1.0 - ADAM_B1 ** ADAM_STEP)
    v_hat = v_ / (1.0 - ADAM_B2 ** ADAM_STEP)
    d_ref[...] = -ADAM_LR * (m_hat / (jnp.sqrt(v_hat) + ADAM_EPS) + ADAM_WD * w_ref[...])


def _adamw(name, w, g, m, v, tr, emit_g):
    R, C = w.shape
    n_out = 4 if emit_g else 3

    def body(w_ref, g_ref, m_ref, v_ref, d_ref, mo_ref, vo_ref, *go_ref):
        _adamw_update(w_ref, g_ref, m_ref, v_ref, d_ref, mo_ref, vo_ref)
        if emit_g:
            go_ref[0][...] = g_ref[...]

    spec = pl.BlockSpec((tr, C), lambda i: (i, 0))
    return pl.pallas_call(
        body, name=name, grid=(R // tr,), in_specs=[spec] * 4, out_specs=[spec] * n_out,
        out_shape=[_sds((R, C), F32)] * n_out, compiler_params=_cparams(("parallel",)))(w, g, m, v)


def _adamw_many(ws, gs, ms, vs):
    n = len(ws)

    def body(*refs):
        ins, outs = refs[:4 * n], refs[4 * n:]
        for k in range(n):
            _adamw_update(ins[k], ins[n + k], ins[2 * n + k], ins[3 * n + k], outs[k], outs[n + k], outs[2 * n + k])

    vmem = pl.BlockSpec(memory_space=pltpu.VMEM)
    res = pl.pallas_call(
        body, name="adamw_small", in_specs=[vmem] * (4 * n), out_specs=[vmem] * (3 * n),
        out_shape=[_sds(w.shape, F32) for w in ws] * 3,
        compiler_params=pltpu.CompilerParams(vmem_limit_bytes=VMEM_LIMIT))(*ws, *gs, *ms, *vs)
    return list(res[:n]), list(res[n:2 * n]), list(res[2 * n:])


def _row_tile(rows, cap):
    best = rows
    for t in range(8, min(rows, cap) + 1, 8):
        if rows % t == 0:
            best = t
    return best


HBM_SPEC = pl.BlockSpec(memory_space=pltpu.HBM)
SEM_SPEC = pl.BlockSpec(memory_space=pltpu.SEMAPHORE)
DATAFLOW = pltpu.SideEffectType.DATAFLOW_SIDE_EFFECTING
DMA_SEM = pltpu.SemaphoreType.DMA


def _hbm(a):
    return pltpu.with_memory_space_constraint(a, pltpu.HBM)


def _mesh_pos():
    return lax.axis_index("x"), lax.axis_index("y"), lax.axis_index("c")


def _other_chips(x, y):
    return [(1 - x, y), (x, 1 - y), (1 - x, 1 - y)]


def _half_rows(buf, shard, core):
    h = buf.shape[1] // 2
    return buf.at[shard, pl.ds(core * h, h), :]


def _ici_copy(buf, j, send, recv, landing):
    x, y, c = _mesh_pos()
    px, py = _other_chips(x, y)[j]
    part = _half_rows(buf, 2 * px + py if landing else 2 * x + y, c)
    return pltpu.make_async_remote_copy(src_ref=part, dst_ref=part, send_sem=send, recv_sem=recv,
                                        device_id=(px, py, c), device_id_type=MESH_ID)


def _sibling_copy(buf, j, send, recv, landing):
    x, y, c = _mesh_pos()
    px, py = _other_chips(x, y)[j]
    part = _half_rows(buf, 2 * px + py, 1 - c if landing else c)
    return pltpu.make_async_remote_copy(src_ref=part, dst_ref=part, send_sem=send, recv_sem=recv,
                                        device_id=(x, y, 1 - c), device_id_type=MESH_ID)


def _forward_sibling(name, bufs, with_ici):
    n = len(bufs)

    def body(*refs):
        ins = refs[:n]
        send_ici, recv_ici, send_d2d, recv_d2d = refs[2 * n:]
        sends = []
        if with_ici:
            for i in range(n):
                for j in range(3):
                    cp = _ici_copy(ins[i], j, send_ici.at[i, j], recv_ici.at[i, j], False)
                    cp.start()
                    sends.append(cp)
        for i in range(n):
            for j in range(3):
                if with_ici:
                    _ici_copy(ins[i], j, send_ici.at[i, j], recv_ici.at[i, j], True).wait_recv()
                cp = _sibling_copy(ins[i], j, send_d2d.at[i, j], recv_d2d.at[i, j], False)
                cp.start()
                sends.append(cp)
        for i in range(n):
            for j in range(3):
                _sibling_copy(ins[i], j, send_d2d.at[i, j], recv_d2d.at[i, j], True).wait_recv()
        for cp in sends:
            cp.wait_send()

    return pl.pallas_call(
        body, name=name, in_specs=[HBM_SPEC] * n, out_specs=[HBM_SPEC] * n,
        out_shape=[_sds(b.shape, b.dtype) for b in bufs],
        scratch_shapes=[DMA_SEM((n, 3))] * 4, input_output_aliases={i: i for i in range(n)},
    )(*bufs)


def _gather_start(name, groups):
    flat = [b for g in groups for b in g]
    n, ng = len(flat), len(groups)

    def body(*refs):
        ins, sems, token = refs[:n], refs[n:n + 2 * ng], refs[-1]
        k = 0
        for gi, g in enumerate(groups):
            for a in range(len(g)):
                for j in range(3):
                    _ici_copy(ins[k], j, sems[2 * gi], sems[2 * gi + 1], False).start()
                k += 1
        token[...] = jnp.zeros_like(token)

    res = pl.pallas_call(
        body, name=name, in_specs=[HBM_SPEC] * n,
        out_specs=[SEM_SPEC] * (2 * ng) + [HBM_SPEC] * n + [pl.BlockSpec(memory_space=pltpu.VMEM)],
        out_shape=[DMA_SEM(()) for g in groups for _ in range(2)]
        + [pltpu.HBM(b.shape, b.dtype) for b in flat] + [_sds((8, LANES), F32)],
        input_output_aliases={i: 2 * ng + i for i in range(n)},
        compiler_params=pltpu.CompilerParams(has_side_effects=DATAFLOW),
    )(*[_hbm(b) for b in flat])
    sems = [(res[2 * gi], res[2 * gi + 1]) for gi in range(ng)]
    thru, k = [], 2 * ng
    for g in groups:
        thru.append(list(res[k:k + len(g)]))
        k += len(g)
    return sems, thru, res[-1]


def _gather_wait(name, bufs, sems, after):
    n = len(bufs)

    def body(*refs):
        ins, send, recv = refs[:n], refs[n], refs[n + 1]
        for a in range(n):
            for j in range(3):
                _ici_copy(ins[a], j, send, recv, False).wait_send()
                _ici_copy(ins[a], j, send, recv, True).wait_recv()

    return pl.pallas_call(
        body, name=name, in_specs=[HBM_SPEC] * n + [SEM_SPEC, SEM_SPEC, pl.BlockSpec(memory_space=pl.ANY)],
        out_specs=[HBM_SPEC] * n, out_shape=[pltpu.HBM(b.shape, b.dtype) for b in bufs],
        input_output_aliases={i: i for i in range(n)},
        compiler_params=pltpu.CompilerParams(has_side_effects=DATAFLOW),
    )(*bufs, sems[0], sems[1], after)


def _send_sibling_halves(name, arrs):
    n = len(arrs)

    def body(*refs):
        ins, outs = refs[:n], refs[n:2 * n]
        send, recv = refs[2 * n:]
        x, y, c = _mesh_pos()
        cps = []
        for i in range(n):
            cp = pltpu.make_async_remote_copy(
                src_ref=ins[i].at[:, 1 - c], dst_ref=outs[i],
                send_sem=send.at[i], recv_sem=recv.at[i], device_id=(x, y, 1 - c), device_id_type=MESH_ID)
            cp.start()
            cps.append(cp)
        for cp in cps:
            cp.wait()

    return pl.pallas_call(
        body, name=name, in_specs=[HBM_SPEC] * n, out_specs=[HBM_SPEC] * n,
        out_shape=[_sds((a.shape[0],) + a.shape[2:], a.dtype) for a in arrs],
        scratch_shapes=[DMA_SEM((n,)), DMA_SEM((n,))],
    )(*arrs)


def _chip_copy(p, land, j, send, recv):
    x, y, c = _mesh_pos()
    px, py = _other_chips(x, y)[j]
    return pltpu.make_async_remote_copy(src_ref=p.at[2 * px + py], dst_ref=land.at[j], send_sem=send, recv_sem=recv,
                                        device_id=(px, py, c), device_id_type=MESH_ID)


def _chip_send_start(name, ps):
    n = len(ps)
    lands = [lax.empty((3,) + p.shape[1:], p.dtype) for p in ps]

    def body(*refs):
        ins, lnd, send, recv, token = refs[:n], refs[n:2 * n], refs[2 * n], refs[2 * n + 1], refs[-1]
        for i in range(n):
            for j in range(3):
                _chip_copy(ins[i], lnd[i], j, send, recv).start()
        token[...] = jnp.zeros_like(token)

    res = pl.pallas_call(
        body, name=name, in_specs=[HBM_SPEC] * (2 * n),
        out_specs=[SEM_SPEC, SEM_SPEC] + [HBM_SPEC] * (2 * n) + [pl.BlockSpec(memory_space=pltpu.VMEM)],
        out_shape=[DMA_SEM(()), DMA_SEM(())] + [pltpu.HBM(a.shape, a.dtype) for a in ps + lands]
        + [_sds((8, LANES), F32)],
        input_output_aliases={i: 2 + i for i in range(2 * n)},
        compiler_params=pltpu.CompilerParams(has_side_effects=DATAFLOW),
    )(*[_hbm(a) for a in ps + lands])
    return (res[0], res[1]), list(res[2:2 + n]), list(res[2 + n:2 + 2 * n]), res[-1]


def _chip_send_wait(name, ps, lands, sems, after):
    n = len(ps)

    def body(*refs):
        ins, lnd, send, recv = refs[:n], refs[n:2 * n], refs[2 * n], refs[2 * n + 1]
        for i in range(n):
            for j in range(3):
                cp = _chip_copy(ins[i], lnd[i], j, send, recv)
                cp.wait_send()
                cp.wait_recv()

    res = pl.pallas_call(
        body, name=name, in_specs=[HBM_SPEC] * (2 * n) + [SEM_SPEC, SEM_SPEC, pl.BlockSpec(memory_space=pl.ANY)],
        out_specs=[HBM_SPEC] * (2 * n), out_shape=[pltpu.HBM(a.shape, a.dtype) for a in ps + lands],
        input_output_aliases={i: i for i in range(2 * n)},
        compiler_params=pltpu.CompilerParams(has_side_effects=DATAFLOW),
    )(*ps, *lands, sems[0], sems[1], after)
    return list(res[:n]), list(res[n:])


def _join_halves(arrs):
    n = len(arrs)

    def body(*refs):
        bufs = refs[n:2 * n]
        send, recv = refs[2 * n:]
        x, y, c = _mesh_pos()
        cps = []
        for i in range(n):
            mine = bufs[i].at[:, c]
            cp = pltpu.make_async_remote_copy(
                src_ref=mine, dst_ref=mine, send_sem=send.at[i], recv_sem=recv.at[i],
                device_id=(x, y, 1 - c), device_id_type=MESH_ID)
            cp.start()
            cps.append(cp)
        for i, cp in enumerate(cps):
            theirs = bufs[i].at[:, 1 - c]
            cp.wait_send()
            pltpu.make_async_remote_copy(
                src_ref=theirs, dst_ref=theirs, send_sem=send.at[i], recv_sem=recv.at[i],
                device_id=(x, y, 1 - c), device_id_type=MESH_ID).wait_recv()

    return pl.pallas_call(
        body, name="join_halves", in_specs=[HBM_SPEC] * n, out_specs=[HBM_SPEC] * n,
        out_shape=[_sds(a.shape, a.dtype) for a in arrs],
        scratch_shapes=[DMA_SEM((n,)), DMA_SEM((n,))], input_output_aliases={i: i for i in range(n)},
    )(*arrs)


def _peer_copy(buf, land, k, send, recv, landing):
    x, y, c = _mesh_pos()
    px, py, pc = x ^ ((k >> 2) & 1), y ^ ((k >> 1) & 1), c ^ (k & 1)
    slot = 4 * px + 2 * py + pc if landing else 4 * x + 2 * y + c
    return pltpu.make_async_remote_copy(src_ref=buf, dst_ref=land.at[slot], send_sem=send, recv_sem=recv,
                                        device_id=(px, py, pc), device_id_type=MESH_ID)


def _exchange_all(buf):
    def body(in_ref, out_ref, send, recv):
        cps = [_peer_copy(in_ref, out_ref, k, send.at[k - 1], recv.at[k - 1], False) for k in range(1, 8)]
        for cp in cps:
            cp.start()
        for k in range(1, 8):
            cps[k - 1].wait_send()
            _peer_copy(in_ref, out_ref, k, send.at[k - 1], recv.at[k - 1], True).wait_recv()

    return pl.pallas_call(
        body, name="exchange_all", in_specs=[HBM_SPEC], out_specs=HBM_SPEC,
        out_shape=_sds((8,) + buf.shape, buf.dtype), scratch_shapes=[DMA_SEM((7,)), DMA_SEM((7,))],
    )(buf)


def _exchange_start(name, buf):
    land = lax.empty((8,) + buf.shape, buf.dtype)

    def body(in_ref, land_ref, send, recv, in_thru, land_thru, token):
        for k in range(1, 8):
            _peer_copy(in_ref, land_ref, k, send, recv, False).start()
        token[...] = jnp.zeros_like(token)

    res = pl.pallas_call(
        body, name=name, in_specs=[HBM_SPEC] * 2,
        out_specs=[SEM_SPEC, SEM_SPEC, HBM_SPEC, HBM_SPEC, pl.BlockSpec(memory_space=pltpu.VMEM)],
        out_shape=[DMA_SEM(()), DMA_SEM(()), pltpu.HBM(buf.shape, buf.dtype), pltpu.HBM(land.shape, land.dtype),
                   _sds((8, LANES), F32)],
        input_output_aliases={0: 2, 1: 3}, compiler_params=pltpu.CompilerParams(has_side_effects=DATAFLOW),
    )(_hbm(buf), _hbm(land))
    return (res[0], res[1]), res[2], res[3], res[4]


def _exchange_wait(name, buf, land, sems, after):
    def body(in_ref, land_ref, send, recv, after_ref, in_thru, land_thru):
        for k in range(1, 8):
            _peer_copy(in_ref, land_ref, k, send, recv, False).wait_send()
            _peer_copy(in_ref, land_ref, k, send, recv, True).wait_recv()

    res = pl.pallas_call(
        body, name=name, in_specs=[HBM_SPEC, HBM_SPEC, SEM_SPEC, SEM_SPEC, pl.BlockSpec(memory_space=pl.ANY)],
        out_specs=[HBM_SPEC, HBM_SPEC], out_shape=[pltpu.HBM(buf.shape, buf.dtype), pltpu.HBM(land.shape, land.dtype)],
        input_output_aliases={0: 0, 1: 1}, compiler_params=pltpu.CompilerParams(has_side_effects=DATAFLOW),
    )(buf, land, sems[0], sems[1], after)
    return res[0], res[1]


def _pad_to(a, axis, size):
    pad = [(0, 0)] * a.ndim
    pad[axis] = (0, size - a.shape[axis])
    return jnp.pad(a, pad)


def _strips(w):
    k, d = w.shape
    return _pad_to(w, 0, 32).reshape(32, d // LANES, LANES).transpose(1, 0, 2)


def kernel(x, norm_mix, w_in, gate_bias, conv_w, conv_b, conv_ln_g, conv_ln_b, w_conv_out, sgu_ln_g, sgu_ln_b, w_spatial, b_spatial, w_sgu_out, w_o, norm_ffn, w_ffn_gate, w_ffn_up, w_ffn_down, norm_final, loss_target, m_norm_mix, m_w_in, m_gate_bias, m_conv_w, m_conv_b, m_conv_ln_g, m_conv_ln_b, m_w_conv_out, m_sgu_ln_g, m_sgu_ln_b, m_w_spatial, m_b_spatial, m_w_sgu_out, m_w_o, m_norm_ffn, m_w_ffn_gate, m_w_ffn_up, m_w_ffn_down, m_norm_final, v_norm_mix, v_w_in, v_gate_bias, v_conv_w, v_conv_b, v_conv_ln_g, v_conv_ln_b, v_w_conv_out, v_sgu_ln_g, v_sgu_ln_b, v_w_spatial, v_b_spatial, v_w_sgu_out, v_w_o, v_norm_ffn, v_w_ffn_gate, v_w_ffn_up, v_w_ffn_down, v_norm_final):
    BL, S, D = x.shape
    T = BL * S
    L = w_in.shape[0]
    CS = w_in.shape[2]
    CN = CS // 3
    DQ = D // NSHARD
    FS = w_ffn_gate.shape[2]
    F = NSHARD * FS
    G, CH = w_spatial.shape[1], w_spatial.shape[2]
    KW = conv_w.shape[1]
    CQ = conv_w.shape[3]
    NSTR = D // LANES
    tm = min(512, S // 2)
    tm2 = max(tm // 2, CH)
    rb = min(64, tm)
    mx, my, mc = _mesh_pos()
    pos = jnp.stack([mc, 2 * mx + my]).astype(jnp.int32)

    def placed(name, w, dtype=BF16, dep=None, layer=None, **kw):
        return _place_shard("place_" + name, w, pos, dtype, _row_tile(w.shape[-2], 256), dep, layer, **kw)

    w_in0 = placed("w_in", w_in, layer=0)
    cw_p = placed("conv_w", _pad_to(conv_w.reshape(L, KW, CQ), 1, 32).reshape(L * 32, CQ), F32)
    fsems, fflying, ftoken = _gather_start("gather_start_first", [[w_in0, cw_p]])
    wts = []
    for l in range(L):
        w_sq = None
        for i, w in enumerate([w_conv_out, w_sgu_out, w_o]):
            w_sq = placed("w_sq", w, dep=ftoken, layer=l, into=w_sq, row_off=i * DQ, out_rows=3 * DQ)
        wts.append(dict(w_in=placed("w_in", w_in, dep=ftoken, layer=l) if l else None, w_sq=w_sq,
                        wg=placed("w_gate", w_ffn_gate[l].T, dep=ftoken), wu=placed("w_up", w_ffn_up[l].T, dep=ftoken),
                        wd=placed("w_down", w_ffn_down, dep=ftoken, layer=l)))
    ffn_keys = ["wg", "wu", "wd"]
    order = [[(0, "w_sq")], [(0, k) for k in ffn_keys]]
    order += [[(l, k) for k in ["w_in", "w_sq"] + ffn_keys] for l in range(1, L)]
    gsems, flying, token = _gather_start("gather_start", [[wts[l][k] for l, k in grp] for grp in order])
    first = _gather_wait("gather_wait_first", fflying[0], fsems[0], token)
    wts[0]["w_in"], cw_g = _forward_sibling("gather_first", first, False)
    conv_w_full = cw_g.reshape(NSHARD, L, 32, CQ).transpose(1, 2, 0, 3).reshape(L, 32, D)[:, :KW]

    def land(gi, after):
        bufs = _gather_wait("gather_wait_%d" % gi, flying[gi], gsems[gi], after)
        bufs = _forward_sibling("gather_forward_%d" % gi, bufs, False)
        for (l, k), b in zip(order[gi], bufs):
            wts[l][k] = b

    x2d = x.reshape(T, D)
    tgt = loss_target.reshape(T, D)
    row = lambda a, l: a[l].reshape(1, -1)

    saved = []
    xc = x2d
    for l in range(L):
        ws_b = w_spatial[l].astype(BF16)
        bs_b = jnp.repeat(b_spatial[l].T, D // G, axis=1)
        cw_s = _strips(conv_w_full[l])
        h, proj = _mix_in_fwd(xc, row(norm_mix, l), wts[l]["w_in"], 0, tm, token if l == 0 else None)
        if l == 0:
            land(0, h)
        c1h, rstd_c, c3, ya = _conv_fwd(proj, cw_s, row(conv_b, l), row(conv_ln_g, l), row(conv_ln_b, l),
                                        wts[l]["w_sq"], 0, S, tm, rb)
        if l == 0:
            land(1, ya)
        mixed, gated, yb, merged, x1 = _sgu_merge_fwd(proj, ya, xc, row(sgu_ln_g, l), row(sgu_ln_b, l), ws_b, bs_b,
                                                      row(gate_bias, l), wts[l]["w_sq"], 0, tm2)
        ffn_w = [wts[l][k].reshape(F, D) for k in ffn_keys]
        h2, gt, up, act, x2 = _ffn_fwd(x1, row(norm_ffn, l), *ffn_w, tm2)
        if l + 1 < L:
            land(l + 2, x2)
        saved.append(dict(x=xc, h=h, proj=proj, c1h=c1h, rstd_c=rstd_c, c3=c3, ya=ya, mixed=mixed, gated=gated,
                          yb=yb, merged=merged, x1=x1, h2=h2, gt=gt, up=up, act=act, ws_b=ws_b, cw=conv_w_full[l]))
        xc = x2

    dx, loss_part, d_norm_final = _loss_head(xc, norm_final.reshape(1, D), tgt, tm)
    loss = lax.psum(loss_part[0, 0], ("x", "y", "c"))

    g_acc = {}

    def reduce_start(tag, layer, named):
        arrs = [g.reshape(NSHARD, 2, g.shape[1] // 2, g.shape[2]) for _, g in named]
        from_sib = _send_sibling_halves("sibling_" + tag, arrs)
        ps = [_add_halves("presum_" + nm, a, r, pos, _row_tile(a.shape[2], 256))
              for (nm, _), a, r in zip(named, arrs, from_sib)]
        sems, ps, lands, tok = _chip_send_start("chip_send_start_" + tag, ps)
        return dict(tag=tag, layer=layer, names=[nm for nm, _ in named], ps=ps, lands=lands, sems=sems), tok

    def reduce_finish(pend, after):
        ps, lands = _chip_send_wait("chip_send_wait_" + pend["tag"], pend["ps"], pend["lands"], pend["sems"], after)
        for nm, p, r in zip(pend["names"], ps, lands):
            g_acc[nm] = _add_shards("shardsum_" + nm, p, r, pos, _row_tile(p.shape[1], 256), pend["layer"], L,
                                    g_acc.get(nm))

    me_idx = (4 * mx + 2 * my + mc).astype(jnp.int32).reshape(1)
    exchanges = []

    def pack_rows(pieces):
        packed = jnp.concatenate(pieces, axis=0)
        return _pad_to(packed, 0, -(-packed.shape[0] // 8) * 8)

    def unpack_rows(summed, pieces):
        out, off = [], 0
        for p in pieces:
            out.append(summed[off:off + p.shape[0]])
            off += p.shape[0]
        return out

    def small_start(tag, pieces):
        sems, buf, land, token = _exchange_start("exchange_start_" + tag, pack_rows(pieces))
        return dict(tag=tag, pieces=pieces, buf=buf, land=land, sems=sems, token=token)

    def small_finish(st, after):
        buf, land = _exchange_wait("exchange_wait_" + st["tag"], st["buf"], st["land"], st["sems"], after)
        return unpack_rows(_sum_slots(buf, land, me_idx, _row_tile(buf.shape[0], 256)), st["pieces"])

    small = [None] * L
    tt = min(1024, T // 2)
    nt = T // tt
    pending, tok = None, None
    for l in reversed(range(L)):
        sv, wt = saved[l], wts[l]
        ffn_w = [wt[k].reshape(F, D) for k in ffn_keys]
        dx1, dgt, dup, d_norm_ffn = _ffn_bwd(dx, sv["x1"], sv["gt"], sv["up"], row(norm_ffn, l), *ffn_w, tm2, tok)
        tn_a = ((tt, F // 2), lambda j, t: (t, j))
        tn_b = ((tt, D), lambda j, t: (t, 0))
        tn_o = ((F, D), (F // 2, D), lambda j, t: (j, 0), 2, nt)
        g_g, = _tn_matmul("grad_w_gate", dgt, *tn_a, [sv["h2"]], *tn_b, *tn_o)
        g_u, = _tn_matmul("grad_w_up", dup, *tn_a, [sv["h2"]], *tn_b, *tn_o)
        g_d, = _tn_matmul("grad_w_down", sv["act"], *tn_a, [dx], *tn_b, *tn_o)
        if pending is not None:
            reduce_finish(pending, g_d)
        ffn_pend, tok = reduce_start("ffn%d" % l, l, [
            ("w_ffn_gate", g_g.reshape(NSHARD, FS, D)), ("w_ffn_up", g_u.reshape(NSHARD, FS, D)),
            ("w_ffn_down", g_d.reshape(NSHARD, FS, D))])
        wst_b = jnp.swapaxes(sv["ws_b"], 1, 2)
        dya, dyb, dp3, d_gate_bias, d_sgu_g, d_sgu_b, d_bs, d_ws = _merge_sgu_bwd(
            dx1, sv["proj"], sv["ya"], sv["yb"], sv["mixed"], row(sgu_ln_g, l), row(sgu_ln_b, l), sv["ws_b"], wst_b,
            row(gate_bias, l), wt["w_sq"], 0, tm2, tok)
        sq_args = ((tt, D), lambda j, t: (t, 0))
        sq_out = ((D, D), (D, D), lambda j, t: (0, 0), 1, nt)
        g_o, = _tn_matmul("grad_w_o", sv["merged"], *sq_args, [dx1], *sq_args, *sq_out)
        g_so, = _tn_matmul("grad_w_sgu_out", sv["gated"], *sq_args, [dyb], *sq_args, *sq_out)
        g_co, = _tn_matmul("grad_w_conv_out", sv["c3"], *sq_args, [dya], *sq_args, *sq_out)
        dc1, d_cln_g, d_cln_b, d_conv_b = _conv_ln_bwd(dya, sv["c1h"], sv["rstd_c"], row(conv_ln_g, l),
                                                       row(conv_ln_b, l), wt["w_sq"], 0, tm)
        small[l] = [None, d_gate_bias.reshape(2, D), None, d_conv_b, d_cln_g, d_cln_b, d_sgu_g, d_sgu_b,
                    d_ws.reshape(G * CH * CH // D, D), d_bs.reshape(G * CH // D, D), d_norm_ffn]
        tok_x = None
        if l == 0:
            early = [k for k in range(len(small[0])) if small[0][k] is not None]
            exchanges.append((small_start("early0", [small[0][k] for k in early]), [(0, k) for k in early]))
            tok_x = exchanges[-1][0]["token"]
        dp3, d_cw_s = _conv_bwd(dc1, sv["proj"], dp3, _strips(sv["cw"][::-1]), S, tm, rb, tok_x)
        g_in, = _tn_matmul("grad_w_in", sv["h"], (tt, D), lambda j, t: (t, 0), [dp3], (tt, CS), lambda j, t: (t, j),
                           (NSHARD, D, CS), (None, D, CS), lambda j, t: (j, 0, 0), NSHARD, nt)
        reduce_finish(ffn_pend, g_in)
        pending, tok = reduce_start("mix%d" % l, l, [
            ("w_in", g_in), ("w_conv_out", g_co.reshape(NSHARD, DQ, D)), ("w_sgu_out", g_so.reshape(NSHARD, DQ, D)),
            ("w_o", g_o.reshape(NSHARD, DQ, D))])
        dx, d_norm_mix = _mix_in_bwd(dx1, dp3, sv["x"], row(norm_mix, l), wt["w_in"], 0, tm, tok)
        small[l][0] = d_norm_mix
        small[l][2] = d_cw_s.transpose(1, 0, 2).reshape(32, D)
        if l > 0:
            exchanges.append((small_start("layer%d" % l, small[l]), [(l, k) for k in range(len(small[l]))]))
            tok = [tok, exchanges[-1][0]["token"]]
    reduce_finish(pending, dx)
    grad_x = dx.reshape(BL, S, D)

    names = ["w_in", "w_conv_out", "w_sgu_out", "w_o", "w_ffn_gate", "w_ffn_up", "w_ffn_down"]
    g_full = _join_halves([g_acc[nm] for nm in names])
    g_w_in, g_w_co, g_w_so, g_w_o, g_w_g, g_w_u, g_w_d = [g.reshape(L, 2 * g.shape[2], g.shape[3]) for g in g_full]
    g_w_g = jnp.swapaxes(g_w_g, 1, 2)
    g_w_u = jnp.swapaxes(g_w_u, 1, 2)

    late = [small[0][0], small[0][2], d_norm_final]
    packed = pack_rows(late)
    summed = _sum_slots(packed, _exchange_all(packed), me_idx, _row_tile(packed.shape[0], 256))
    sg = [[None] * len(small[l]) for l in range(L)]
    sg[0][0], sg[0][2], g_norm_final = unpack_rows(summed, late)
    g_norm_final = g_norm_final[0]
    for st, where in exchanges:
        for (l, k), piece in zip(where, small_finish(st, summed)):
            sg[l][k] = piece

    def per_layer(k, shape):
        return jnp.stack([sg[l][k] for l in range(L)]).reshape(shape)

    g_norm_mix = per_layer(0, (L, D))
    g_gate_bias = per_layer(1, (L, 2 * D))
    g_conv_w_full = jnp.stack([sg[l][2][:KW] for l in range(L)])
    g_conv_w = lax.dynamic_slice_in_dim(g_conv_w_full, (2 * mx + my) * CQ, CQ, axis=2).reshape(L, KW, 1, CQ)
    g_conv_b = per_layer(3, (L, D))
    g_conv_ln_g = per_layer(4, (L, D))
    g_conv_ln_b = per_layer(5, (L, D))
    g_sgu_ln_g = per_layer(6, (L, D))
    g_sgu_ln_b = per_layer(7, (L, D))
    g_w_spatial = per_layer(8, (L, G, CH, CH))
    g_b_spatial = per_layer(9, (L, G, CH))
    g_norm_ffn = per_layer(10, (L, D))

    grads = [g_norm_mix, g_w_in, g_gate_bias, g_conv_w, g_conv_b, g_conv_ln_g, g_conv_ln_b, g_w_co, g_sgu_ln_g,
             g_sgu_ln_b, g_w_spatial, g_b_spatial, g_w_so, g_w_o, g_norm_ffn, g_w_g, g_w_u, g_w_d, g_norm_final]
    weights = [norm_mix, w_in, gate_bias, conv_w, conv_b, conv_ln_g, conv_ln_b, w_conv_out, sgu_ln_g, sgu_ln_b,
               w_spatial, b_spatial, w_sgu_out, w_o, norm_ffn, w_ffn_gate, w_ffn_up, w_ffn_down, norm_final]
    ms = [m_norm_mix, m_w_in, m_gate_bias, m_conv_w, m_conv_b, m_conv_ln_g, m_conv_ln_b, m_w_conv_out, m_sgu_ln_g,
          m_sgu_ln_b, m_w_spatial, m_b_spatial, m_w_sgu_out, m_w_o, m_norm_ffn, m_w_ffn_gate, m_w_ffn_up,
          m_w_ffn_down, m_norm_final]
    vs = [v_norm_mix, v_w_in, v_gate_bias, v_conv_w, v_conv_b, v_conv_ln_g, v_conv_ln_b, v_w_conv_out, v_sgu_ln_g,
          v_sgu_ln_b, v_w_spatial, v_b_spatial, v_w_sgu_out, v_w_o, v_norm_ffn, v_w_ffn_gate, v_w_ffn_up,
          v_w_ffn_down, v_norm_final]

    big_idx = [1, 7, 12, 13, 15, 16, 17]
    transposed = [15, 16]
    deltas, new_m, new_v = [None] * 19, [None] * 19, [None] * 19
    for k in big_idx:
        shp = weights[k].shape
        r2 = (shp[0] * shp[1], shp[2])
        res = _adamw("adamw_" + str(k), weights[k].reshape(r2), grads[k].reshape(r2), ms[k].reshape(r2),
                     vs[k].reshape(r2), _row_tile(r2[0], 256), k not in transposed)
        deltas[k], new_m[k], new_v[k] = [a.reshape(shp) for a in res[:3]]
        if k not in transposed:
            grads[k] = res[3].reshape(shp)
    small_idx = [k for k in range(19) if k not in big_idx]
    pick = lambda arrs: [arrs[k].reshape(1, -1) if arrs[k].ndim == 1 else arrs[k] for k in small_idx]
    d_, m_, v_ = _adamw_many(pick(weights), pick(grads), pick(ms), pick(vs))
    for i, k in enumerate(small_idx):
        shp = weights[k].shape
        deltas[k], new_m[k], new_v[k] = d_[i].reshape(shp), m_[i].reshape(shp), v_[i].reshape(shp)

    return (loss, grad_x, *grads, *deltas, *new_m, *new_v)
```

```python
import functools

import jax
import jax.numpy as jnp
from jax import lax
from jax.experimental import pallas as pl
from jax.experimental.pallas import tpu as pltpu

F32 = jnp.float32
BF16 = jnp.bfloat16
EPS = 1e-6
ADAM_LR = 0.001
ADAM_B1 = 0.9
ADAM_B2 = 0.999
ADAM_EPS = 1e-08
ADAM_WD = 0.01
ADAM_STEP = 10

NSHARD = 4
LANES = 128
HALO = 16
VMEM_LIMIT = 60 * 1024 * 1024
MESH_ID = pl.DeviceIdType.MESH


def _dot(a, b):
    return jnp.dot(a, b, preferred_element_type=F32)


def _dot_nt(a, b):
    return lax.dot_general(a, b, (((1,), (1,)), ((), ())), preferred_element_type=F32)


def _dot_tn(a, b):
    return lax.dot_general(a, b, (((0,), (0,)), ((), ())), preferred_element_type=F32)


def _sig(z):
    return 1.0 / (1.0 + jnp.exp(-z))


def _res(shape, imap=None):
    nd = len(shape)
    if imap is None:
        imap = lambda *_: (0,) * nd
    return pl.BlockSpec(shape, imap, pipeline_mode=pl.Buffered(1))


def _cparams(sem):
    return pltpu.CompilerParams(dimension_semantics=sem, vmem_limit_bytes=VMEM_LIMIT)


def _sds(shape, dtype):
    return jax.ShapeDtypeStruct(shape, dtype)


def _after(body, n_in, dep):
    deps = [] if dep is None else [d for d in (dep if isinstance(dep, (list, tuple)) else [dep]) if d is not None]
    if not deps:
        return body, [], []

    def wrapped(*refs):
        return body(*refs[:n_in], *refs[n_in + len(deps):])

    return wrapped, [pl.BlockSpec(memory_space=pl.ANY)] * len(deps), deps


def _mix_in_fwd(x2d, g_mix, w_in_g, layer, tm, dep=None):
    T, D = x2d.shape
    CS = w_in_g.shape[2]
    CN = CS // 3

    def body(x_ref, g_ref, w_ref, h_ref, p_ref):
        x = x_ref[...]
        rstd = lax.rsqrt(jnp.mean(x * x, axis=-1, keepdims=True) + EPS)
        h = (x * rstd * g_ref[...]).astype(BF16)
        h_ref[...] = h
        for s in range(NSHARD):
            for j in range(3):
                c0 = s * CS + j * CN
                p_ref[:, c0:c0 + CN] = _dot(h, w_ref[s, :, j * CN:(j + 1) * CN]).astype(BF16)

    body, dep_spec, dep_arg = _after(body, 3, dep)
    return pl.pallas_call(
        body, name="mix_in_fwd", grid=(T // tm,),
        in_specs=[pl.BlockSpec((tm, D), lambda i: (i, 0)), _res((1, D)),
                  _res((NSHARD, D, CS), lambda i: (0, layer, 0))] + dep_spec,
        out_specs=[pl.BlockSpec((tm, D), lambda i: (i, 0)), pl.BlockSpec((tm, NSHARD * CS), lambda i: (i, 0))],
        out_shape=[_sds((T, D), BF16), _sds((T, NSHARD * CS), BF16)],
        compiler_params=_cparams(("parallel",)),
    )(x2d, g_mix, w_in_g, *dep_arg)


def _halo_maps(tm, n_rows):
    nb = tm // HALO
    last = n_rows // HALO - 1
    prev = lambda i: (jnp.maximum(i * nb - 1, 0), 0)
    nxt = lambda i: (jnp.minimum((i + 1) * nb, last), 0)
    return prev, nxt


def _dwconv(pad_ref, w_ref, out_ref, n_strips, tm, kw, rb):
    off = HALO - (kw - 1) // 2

    def strip(cs, carry):
        for r0 in range(0, tm, rb):
            acc = jnp.zeros((rb, LANES), F32)
            for k in range(kw):
                r = r0 + off + k
                acc = acc + w_ref[cs, k:k + 1, :] * pad_ref[cs, r:r + rb, :]
            out_ref[cs, r0:r0 + rb, :] = acc
        return carry

    lax.fori_loop(0, n_strips, strip, 0)


def _fill_c0_pad(pad_ref, pa_ref, pprev_ref, pnext_ref, D, tm, first, last):
    for cs in range(D // LANES):
        lo, hi = cs * LANES, (cs + 1) * LANES

        def c0_of(ref):
            return ref[:, lo:hi].astype(F32) * _sig(ref[:, D + lo:D + hi].astype(F32))

        pad_ref[cs, HALO:HALO + tm, :] = c0_of(pa_ref)
        pad_ref[cs, 0:HALO, :] = jnp.where(first, 0.0, c0_of(pprev_ref))
        pad_ref[cs, HALO + tm:HALO + tm + HALO, :] = jnp.where(last, 0.0, c0_of(pnext_ref))


def _conv_fwd(proj, conv_w_s, conv_b, ln_g, ln_b, w_sq_g, layer, seq, tm, rb):
    T = proj.shape[0]
    D = conv_b.shape[1]
    DQ = D // NSHARD
    NSTR = D // LANES
    KW = 31
    tps = seq // tm
    prev, nxt = _halo_maps(tm, T)

    def body(pa_ref, pprev_ref, pnext_ref, w_ref, b_ref, g_ref, be_ref, wco_ref,
             c1h_ref, rstd_ref, c3_ref, ya_ref, pad_ref, c1s_ref):
        i = pl.program_id(0)
        first = (i % tps) == 0
        last = (i % tps) == tps - 1
        _fill_c0_pad(pad_ref, pa_ref, pprev_ref, pnext_ref, D, tm, first, last)
        _dwconv(pad_ref, w_ref, c1s_ref, NSTR, tm, KW, rb)
        wco = wco_ref[...].reshape(D, D)
        for r0 in (0, tm // 2):
            rows = slice(r0, r0 + tm // 2)
            c1 = jnp.concatenate([c1s_ref[cs, rows, :] for cs in range(NSTR)], axis=1) + b_ref[...]
            mu = jnp.mean(c1, axis=-1, keepdims=True)
            cc = c1 - mu
            rstd = lax.rsqrt(jnp.mean(cc * cc, axis=-1, keepdims=True) + EPS)
            c1h = cc * rstd
            c1h_ref[rows, :] = c1h.astype(BF16)
            rstd_ref[rows, :] = rstd
            c2 = c1h * g_ref[...] + be_ref[...]
            c3 = (c2 * _sig(c2)).astype(BF16)
            c3_ref[rows, :] = c3
            ya_ref[rows, :] = _dot(c3, wco).astype(BF16)

    row = lambda i: (i, 0)
    return pl.pallas_call(
        body, name="conv_fwd", grid=(T // tm,),
        in_specs=[pl.BlockSpec((tm, 2 * D), row), pl.BlockSpec((HALO, 2 * D), prev), pl.BlockSpec((HALO, 2 * D), nxt),
                  _res((NSTR, 32, LANES)), _res((1, D)), _res((1, D)), _res((1, D)),
                  _res((NSHARD, DQ, D), lambda i: (0, layer * 3 + 0, 0))],
        out_specs=[pl.BlockSpec((tm, D), row), pl.BlockSpec((tm, 1), row), pl.BlockSpec((tm, D), row),
                   pl.BlockSpec((tm, D), row)],
        out_shape=[_sds((T, D), BF16), _sds((T, 1), F32), _sds((T, D), BF16), _sds((T, D), BF16)],
        scratch_shapes=[pltpu.VMEM((NSTR, tm + 2 * HALO, LANES), F32), pltpu.VMEM((NSTR, tm, LANES), F32)],
        compiler_params=_cparams(("parallel",)),
    )(proj, proj, proj, conv_w_s, conv_b, ln_g, ln_b, w_sq_g)


def _sgu_merge_fwd(proj, ya, x2d, ln_g, ln_b, ws_b, bs_b, gate_bias, w_sq_g, layer, tm):
    T, D = x2d.shape
    DQ = D // NSHARD
    G, CH, _ = ws_b.shape
    GD = D // G

    def body(puv_ref, pg_ref, ya_ref, x_ref, g_ref, be_ref, ws_ref, bsb_ref, gb_ref, wso_ref, wo_ref,
             mixed_ref, gated_ref, yb_ref, merged_ref, x1_ref, mix_scr):
        u = puv_ref[:, :D].astype(F32)
        v = puv_ref[:, D:].astype(F32)
        mu = jnp.mean(v, axis=-1, keepdims=True)
        vc = v - mu
        rstd = lax.rsqrt(jnp.mean(vc * vc, axis=-1, keepdims=True) + EPS)
        vn = (vc * rstd * g_ref[...] + be_ref[...]).astype(BF16)
        nch = tm // CH
        for g in range(G):
            cols = slice(g * GD, (g + 1) * GD)
            rhs = jnp.concatenate([vn[ch * CH:(ch + 1) * CH, cols] for ch in range(nch)], axis=1)
            res = _dot(ws_ref[g], rhs)
            for ch in range(nch):
                mix_scr[ch * CH:(ch + 1) * CH, cols] = res[:, ch * GD:(ch + 1) * GD] + bsb_ref[:, cols]
        mixed = mix_scr[...]
        mixed_ref[...] = mixed.astype(BF16)
        gated = (u * mixed).astype(BF16)
        gated_ref[...] = gated
        yb = _dot(gated, wso_ref[...].reshape(D, D))
        yb_ref[...] = yb.astype(BF16)
        sa = _sig(pg_ref[:, :D].astype(F32) + gb_ref[:, :D])
        sb = _sig(pg_ref[:, D:].astype(F32) + gb_ref[:, D:])
        merged = (sa * ya_ref[...].astype(F32) + sb * yb).astype(BF16)
        merged_ref[...] = merged
        x1_ref[...] = x_ref[...] + _dot(merged, wo_ref[...].reshape(D, D))

    row = lambda i: (i, 0)
    return pl.pallas_call(
        body, name="sgu_merge_fwd", grid=(T // tm,),
        in_specs=[pl.BlockSpec((tm, 2 * D), lambda i: (i, 1)), pl.BlockSpec((tm, 2 * D), lambda i: (i, 2)),
                  pl.BlockSpec((tm, D), row), pl.BlockSpec((tm, D), row),
                  _res((1, D)), _res((1, D)), _res((G, CH, CH)), _res((CH, D)), _res((1, 2 * D)),
                  _res((NSHARD, DQ, D), lambda i: (0, layer * 3 + 1, 0)),
                  _res((NSHARD, DQ, D), lambda i: (0, layer * 3 + 2, 0))],
        out_specs=[pl.BlockSpec((tm, D), row)] * 5,
        out_shape=[_sds((T, D), BF16)] * 4 + [_sds((T, D), F32)],
        scratch_shapes=[pltpu.VMEM((tm, D), F32)],
        compiler_params=_cparams(("parallel",)),
    )(proj, proj, ya, x2d, ln_g, ln_b, ws_b, bs_b, gate_bias, w_sq_g, w_sq_g)


def _ffn_chunks(F):
    assert F % 256 == 0, F
    return [(c0, min(512, F - c0)) for c0 in range(0, F, 512)]


def _ffn_fwd(x1, g_ffn, wgt, wut, wd, tm):
    T, D = x1.shape
    F = wd.shape[0]

    def body(x_ref, g_ref, wg_ref, wu_ref, wd_ref, h2_ref, gt_ref, up_ref, act_ref, x2_ref):
        x = x_ref[...]
        rstd = lax.rsqrt(jnp.mean(x * x, axis=-1, keepdims=True) + EPS)
        h2 = (x * rstd * g_ref[...]).astype(BF16)
        h2_ref[...] = h2
        acc = x
        chunks = _ffn_chunks(F)

        def gate_up(c0, cw):
            return _dot_nt(h2, wg_ref[c0:c0 + cw, :]), _dot_nt(h2, wu_ref[c0:c0 + cw, :])

        nxt = gate_up(*chunks[0])
        for ci, (c0, cw) in enumerate(chunks):
            gt, up = nxt
            if ci + 1 < len(chunks):
                nxt = gate_up(*chunks[ci + 1])
            gt_ref[:, c0:c0 + cw] = gt.astype(BF16)
            up_ref[:, c0:c0 + cw] = up.astype(BF16)
            act = (gt * _sig(gt) * up).astype(BF16)
            act_ref[:, c0:c0 + cw] = act
            acc = acc + _dot(act, wd_ref[c0:c0 + cw, :])
        x2_ref[...] = acc

    row = lambda i: (i, 0)
    return pl.pallas_call(
        body, name="ffn_fwd", grid=(T // tm,),
        in_specs=[pl.BlockSpec((tm, D), row), _res((1, D)), _res((F, D)), _res((F, D)), _res((F, D))],
        out_specs=[pl.BlockSpec((tm, D), row), pl.BlockSpec((tm, F), row), pl.BlockSpec((tm, F), row),
                   pl.BlockSpec((tm, F), row), pl.BlockSpec((tm, D), row)],
        out_shape=[_sds((T, D), BF16), _sds((T, F), BF16), _sds((T, F), BF16), _sds((T, F), BF16), _sds((T, D), F32)],
        compiler_params=_cparams(("parallel",)),
    )(x1, g_ffn, wgt, wut, wd)


def _loss_head(xf, g_fin, target, tm):
    T, D = xf.shape
    n = T // tm

    def body(x_ref, g_ref, t_ref, dx_ref, loss_ref, dg_ref, acc_ref):
        i = pl.program_id(0)

        @pl.when(i == 0)
        def _():
            acc_ref[...] = jnp.zeros_like(acc_ref)
            dg_ref[...] = jnp.zeros_like(dg_ref)

        x = x_ref[...]
        g = g_ref[...]
        rstd = lax.rsqrt(jnp.mean(x * x, axis=-1, keepdims=True) + EPS)
        xh = x * rstd
        diff = xh * g - t_ref[...]
        acc_ref[...] += jnp.sum(diff * diff, axis=0, keepdims=True)
        dy = diff * (1.0 / D)
        dg_ref[...] += jnp.sum(dy * xh, axis=0, keepdims=True)
        dxh = dy * g
        dx_ref[...] = rstd * (dxh - xh * jnp.mean(dxh * xh, axis=-1, keepdims=True))

        @pl.when(i == n - 1)
        def _():
            tot = jnp.sum(acc_ref[...], axis=-1, keepdims=True) * (0.5 / D)
            loss_ref[...] = jnp.broadcast_to(tot, loss_ref.shape)

    row = lambda i: (i, 0)
    return pl.pallas_call(
        body, name="loss_head", grid=(n,),
        in_specs=[pl.BlockSpec((tm, D), row), _res((1, D)), pl.BlockSpec((tm, D), row)],
        out_specs=[pl.BlockSpec((tm, D), row), pl.BlockSpec((1, LANES), lambda i: (0, 0)),
                   pl.BlockSpec((1, D), lambda i: (0, 0))],
        out_shape=[_sds((T, D), F32), _sds((1, LANES), F32), _sds((1, D), F32)],
        scratch_shapes=[pltpu.VMEM((1, D), F32)],
        compiler_params=_cparams(("arbitrary",)),
    )(xf, g_fin, target)


def _ffn_bwd(dx2, x1, gt, up, g_ffn, wgt, wut, wd, tm, dep=None):
    T, D = x1.shape
    F = wd.shape[0]

    def body(dx2_ref, x1_ref, gt_ref, up_ref, g_ref, wg_ref, wu_ref, wd_ref, dx1_ref, dgt_ref, dup_ref, dg_ref):
        i = pl.program_id(0)

        @pl.when(i == 0)
        def _():
            dg_ref[...] = jnp.zeros_like(dg_ref)

        dx2 = dx2_ref[...]
        dx2b = dx2.astype(BF16)
        dh2 = jnp.zeros((tm, D), F32)
        chunks = _ffn_chunks(F)
        dact_next = _dot_nt(dx2b, wd_ref[0:chunks[0][1], :])
        for ci, (c0, cw) in enumerate(chunks):
            dact = dact_next
            if ci + 1 < len(chunks):
                n0, nw = chunks[ci + 1]
                dact_next = _dot_nt(dx2b, wd_ref[n0:n0 + nw, :])
            g = gt_ref[:, c0:c0 + cw].astype(F32)
            u = up_ref[:, c0:c0 + cw].astype(F32)
            sg = _sig(g)
            dup = (dact * (g * sg)).astype(BF16)
            dgt = (dact * u * (sg * (1.0 + g * (1.0 - sg)))).astype(BF16)
            dgt_ref[:, c0:c0 + cw] = dgt
            dup_ref[:, c0:c0 + cw] = dup
            dh2 = dh2 + _dot(dgt, wg_ref[c0:c0 + cw, :]) + _dot(dup, wu_ref[c0:c0 + cw, :])
        x = x1_ref[...]
        rstd = lax.rsqrt(jnp.mean(x * x, axis=-1, keepdims=True) + EPS)
        xh = x * rstd
        dg_ref[...] += jnp.sum(dh2 * xh, axis=0, keepdims=True)
        dxh = dh2 * g_ref[...]
        dx1_ref[...] = dx2 + rstd * (dxh - xh * jnp.mean(dxh * xh, axis=-1, keepdims=True))

    row = lambda i: (i, 0)
    body, dep_spec, dep_arg = _after(body, 8, dep)
    return pl.pallas_call(
        body, name="ffn_bwd", grid=(T // tm,),
        in_specs=[pl.BlockSpec((tm, D), row), pl.BlockSpec((tm, D), row), pl.BlockSpec((tm, F), row),
                  pl.BlockSpec((tm, F), row), _res((1, D)), _res((F, D)), _res((F, D)), _res((F, D))] + dep_spec,
        out_specs=[pl.BlockSpec((tm, D), row), pl.BlockSpec((tm, F), row), pl.BlockSpec((tm, F), row),
                   pl.BlockSpec((1, D), lambda i: (0, 0))],
        out_shape=[_sds((T, D), F32), _sds((T, F), BF16), _sds((T, F), BF16), _sds((1, D), F32)],
        compiler_params=_cparams(("arbitrary",)),
    )(dx2, x1, gt, up, g_ffn, wgt, wut, wd, *dep_arg)


def _merge_sgu_bwd(dx1, proj, ya, yb, mixed, ln_g, ln_b, ws_b, wst_b, gate_bias, w_sq_g, layer, tm, dep=None):
    T, D = dx1.shape
    DQ = D // NSHARD
    G, CH, _ = ws_b.shape
    GD = D // G

    def body(dx1_ref, puv_ref, pg_ref, ya_ref, yb_ref, mixed_ref, g_ref, be_ref, ws_ref, wst_ref, gb_ref,
             wso_ref, wo_ref, dya_ref, dyb_ref, dp_ref, dgb_ref, dlg_ref, dlb_ref, dbs_ref, dws_ref,
             dvn_scr, dbs_scr):
        i = pl.program_id(0)

        @pl.when(i == 0)
        def _():
            for r in (dgb_ref, dlg_ref, dlb_ref, dws_ref, dbs_scr):
                r[...] = jnp.zeros_like(r)

        dmerged = _dot_nt(dx1_ref[...].astype(BF16), wo_ref[...].reshape(D, D))
        sa = _sig(pg_ref[:, :D].astype(F32) + gb_ref[:, :D])
        sb = _sig(pg_ref[:, D:].astype(F32) + gb_ref[:, D:])
        dya = (dmerged * sa).astype(BF16)
        dyb = (dmerged * sb).astype(BF16)
        dya_ref[...] = dya
        dyb_ref[...] = dyb
        dga = dmerged * ya_ref[...].astype(F32) * (sa * (1.0 - sa))
        dgb = dmerged * yb_ref[...].astype(F32) * (sb * (1.0 - sb))
        dp_ref[:, 4 * D:5 * D] = dga.astype(BF16)
        dp_ref[:, 5 * D:6 * D] = dgb.astype(BF16)
        dgb_ref[:, :D] += jnp.sum(dga, axis=0, keepdims=True)
        dgb_ref[:, D:] += jnp.sum(dgb, axis=0, keepdims=True)

        dgated = _dot_nt(dyb, wso_ref[...].reshape(D, D))
        u = puv_ref[:, :D].astype(F32)
        v = puv_ref[:, D:].astype(F32)
        dp_ref[:, 2 * D:3 * D] = (dgated * mixed_ref[...].astype(F32)).astype(BF16)
        dmixed = dgated * u
        mu = jnp.mean(v, axis=-1, keepdims=True)
        vc = v - mu
        rstd = lax.rsqrt(jnp.mean(vc * vc, axis=-1, keepdims=True) + EPS)
        vh = vc * rstd
        vn = (vh * g_ref[...] + be_ref[...]).astype(BF16)
        dmb = dmixed.astype(BF16)
        nch = tm // CH
        bs_part = dmixed[0:CH, :]
        for ch in range(1, nch):
            bs_part = bs_part + dmixed[ch * CH:(ch + 1) * CH, :]
        dbs_scr[...] += bs_part
        for g in range(G):
            cols = slice(g * GD, (g + 1) * GD)
            dm_g = jnp.concatenate([dmb[ch * CH:(ch + 1) * CH, cols] for ch in range(nch)], axis=1)
            vn_g = jnp.concatenate([vn[ch * CH:(ch + 1) * CH, cols] for ch in range(nch)], axis=1)
            dws_ref[g] += _dot_nt(dm_g, vn_g)
            dvn_g = _dot(wst_ref[g], dm_g)
            for ch in range(nch):
                dvn_scr[ch * CH:(ch + 1) * CH, cols] = dvn_g[:, ch * GD:(ch + 1) * GD]
        dvn = dvn_scr[...]
        dlg_ref[...] += jnp.sum(dvn * vh, axis=0, keepdims=True)
        dlb_ref[...] += jnp.sum(dvn, axis=0, keepdims=True)
        dxh = dvn * g_ref[...]
        dv = rstd * (dxh - jnp.mean(dxh, axis=-1, keepdims=True) - vh * jnp.mean(dxh * vh, axis=-1, keepdims=True))
        dp_ref[:, 3 * D:4 * D] = dv.astype(BF16)

        @pl.when(i == pl.num_programs(0) - 1)
        def _():
            for g in range(G):
                blk = dbs_scr[:, g * GD:(g + 1) * GD]
                if GD != CH:
                    blk = jnp.concatenate([blk, jnp.zeros((CH, CH - GD), F32)], axis=1)
                dbs_ref[:, g * CH:(g + 1) * CH] = jnp.sum(blk.T, axis=0, keepdims=True)

    row = lambda i: (i, 0)
    fixed2 = lambda i: (0, 0)
    body, dep_spec, dep_arg = _after(body, 13, dep)
    return pl.pallas_call(
        body, name="merge_sgu_bwd", grid=(T // tm,),
        in_specs=[pl.BlockSpec((tm, D), row), pl.BlockSpec((tm, 2 * D), lambda i: (i, 1)),
                  pl.BlockSpec((tm, 2 * D), lambda i: (i, 2)), pl.BlockSpec((tm, D), row), pl.BlockSpec((tm, D), row),
                  pl.BlockSpec((tm, D), row), _res((1, D)), _res((1, D)), _res((G, CH, CH)), _res((G, CH, CH)),
                  _res((1, 2 * D)),
                  _res((NSHARD, DQ, D), lambda i: (0, layer * 3 + 1, 0)),
                  _res((NSHARD, DQ, D), lambda i: (0, layer * 3 + 2, 0))] + dep_spec,
        out_specs=[pl.BlockSpec((tm, D), row), pl.BlockSpec((tm, D), row),
                   pl.BlockSpec((tm, 6 * D), row),
                   pl.BlockSpec((1, 2 * D), fixed2), pl.BlockSpec((1, D), fixed2), pl.BlockSpec((1, D), fixed2),
                   pl.BlockSpec((1, G * CH), fixed2), pl.BlockSpec((G, CH, CH), lambda i: (0, 0, 0))],
        out_shape=[_sds((T, D), BF16), _sds((T, D), BF16), _sds((T, 6 * D), BF16),
                   _sds((1, 2 * D), F32), _sds((1, D), F32), _sds((1, D), F32), _sds((1, G * CH), F32),
                   _sds((G, CH, CH), F32)],
        scratch_shapes=[pltpu.VMEM((tm, D), F32), pltpu.VMEM((CH, D), F32)],
        compiler_params=_cparams(("arbitrary",)),
    )(dx1, proj, proj, ya, yb, mixed, ln_g, ln_b, ws_b, wst_b, gate_bias, w_sq_g, w_sq_g, *dep_arg)


def _conv_ln_bwd(dya, c1h, rstd_c, ln_g, ln_b, w_sq_g, layer, tm):
    T, D = dya.shape
    DQ = D // NSHARD

    def body(dya_ref, c1h_ref, rstd_ref, g_ref, be_ref, wco_ref, dc1_ref, dlg_ref, dlb_ref, dcb_ref):
        i = pl.program_id(0)

        @pl.when(i == 0)
        def _():
            for r in (dlg_ref, dlb_ref, dcb_ref):
                r[...] = jnp.zeros_like(r)

        wco = wco_ref[...].reshape(D, D)
        blocks = [slice(r0, r0 + tm // 2) for r0 in (0, tm // 2)]
        dc3_next = _dot_nt(dya_ref[blocks[0], :], wco)
        for bi, rows in enumerate(blocks):
            dc3 = dc3_next
            if bi == 0:
                dc3_next = _dot_nt(dya_ref[blocks[1], :], wco)
            c1h = c1h_ref[rows, :].astype(F32)
            c2 = c1h * g_ref[...] + be_ref[...]
            sg = _sig(c2)
            dc2 = dc3 * (sg * (1.0 + c2 * (1.0 - sg)))
            dlg_ref[...] += jnp.sum(dc2 * c1h, axis=0, keepdims=True)
            dlb_ref[...] += jnp.sum(dc2, axis=0, keepdims=True)
            dxh = dc2 * g_ref[...]
            dc1 = rstd_ref[rows, :] * (dxh - jnp.mean(dxh, axis=-1, keepdims=True)
                                       - c1h * jnp.mean(dxh * c1h, axis=-1, keepdims=True))
            dc1_ref[rows, :] = dc1
            dcb_ref[...] += jnp.sum(dc1, axis=0, keepdims=True)

    row = lambda i: (i, 0)
    fixed2 = lambda i: (0, 0)
    return pl.pallas_call(
        body, name="conv_ln_bwd", grid=(T // tm,),
        in_specs=[pl.BlockSpec((tm, D), row), pl.BlockSpec((tm, D), row), pl.BlockSpec((tm, 1), row),
                  _res((1, D)), _res((1, D)), _res((NSHARD, DQ, D), lambda i: (0, layer * 3 + 0, 0))],
        out_specs=[pl.BlockSpec((tm, D), row), pl.BlockSpec((1, D), fixed2), pl.BlockSpec((1, D), fixed2),
                   pl.BlockSpec((1, D), fixed2)],
        out_shape=[_sds((T, D), F32), _sds((1, D), F32), _sds((1, D), F32), _sds((1, D), F32)],
        compiler_params=_cparams(("arbitrary",)),
    )(dya, c1h, rstd_c, ln_g, ln_b, w_sq_g)


def _conv_bwd(dc1, proj, dp3, conv_wf_s, seq, tm, rb, dep=None):
    T, D = dc1.shape
    NSTR = D // LANES
    KW = 31
    PADK = (KW - 1) // 2
    tps = seq // tm
    prev, nxt = _halo_maps(tm, T)
    n = T // tm

    def body(dc_ref, dcprev_ref, dcnext_ref, pa_ref, pprev_ref, pnext_ref, wf_ref, dp_in_ref,
             dp_ref, dw_ref, pad_ref, dpad_ref, dc0_ref, dwacc_ref):
        del dp_in_ref
        i = pl.program_id(0)
        first = (i % tps) == 0
        last = (i % tps) == tps - 1

        @pl.when(i == 0)
        def _():
            dwacc_ref[...] = jnp.zeros_like(dwacc_ref)

        _fill_c0_pad(pad_ref, pa_ref, pprev_ref, pnext_ref, D, tm, first, last)
        for cs in range(NSTR):
            lo, hi = cs * LANES, (cs + 1) * LANES
            dpad_ref[cs, HALO:HALO + tm, :] = dc_ref[:, lo:hi]
            dpad_ref[cs, 0:HALO, :] = jnp.where(first, 0.0, dcprev_ref[:, lo:hi])
            dpad_ref[cs, HALO + tm:HALO + tm + HALO, :] = jnp.where(last, 0.0, dcnext_ref[:, lo:hi])
        _dwconv(dpad_ref, wf_ref, dc0_ref, NSTR, tm, KW, rb)

        def strip(cs, carry):
            for r0 in range(0, tm, rb):
                d = dpad_ref[cs, HALO + r0:HALO + r0 + rb, :]
                for k in range(KW):
                    r = r0 + HALO - PADK + k
                    prod = d * pad_ref[cs, r:r + rb, :]
                    dwacc_ref[cs, k * 8:(k + 1) * 8, :] += jnp.sum(prod.reshape(rb // 8, 8, LANES), axis=0)
            return carry

        lax.fori_loop(0, NSTR, strip, 0)

        for cs in range(NSTR):
            lo, hi = cs * LANES, (cs + 1) * LANES
            av = pa_ref[:, lo:hi].astype(F32)
            sg = _sig(pa_ref[:, D + lo:D + hi].astype(F32))
            dc0 = dc0_ref[cs]
            dp_ref[:, lo:hi] = (dc0 * sg).astype(BF16)
            dp_ref[:, D + lo:D + hi] = (dc0 * av * (sg * (1.0 - sg))).astype(BF16)

        @pl.when(i == n - 1)
        def _():
            for cs in range(NSTR):
                dw_ref[cs] = jnp.sum(dwacc_ref[cs].reshape(32, 8, LANES), axis=1)

    row = lambda i: (i, 0)
    body, dep_spec, dep_arg = _after(body, 8, dep)
    return pl.pallas_call(
        body, name="conv_bwd", grid=(n,),
        in_specs=[pl.BlockSpec((tm, D), row), pl.BlockSpec((HALO, D), prev), pl.BlockSpec((HALO, D), nxt),
                  pl.BlockSpec((tm, 2 * D), row), pl.BlockSpec((HALO, 2 * D), prev), pl.BlockSpec((HALO, 2 * D), nxt),
                  _res((NSTR, 32, LANES)), pl.BlockSpec(memory_space=pl.ANY)] + dep_spec,
        out_specs=[pl.BlockSpec((tm, 2 * D), row),
                   pl.BlockSpec((NSTR, 32, LANES), lambda i: (0, 0, 0))],
        out_shape=[_sds(dp3.shape, BF16), _sds((NSTR, 32, LANES), F32)],
        scratch_shapes=[pltpu.VMEM((NSTR, tm + 2 * HALO, LANES), F32), pltpu.VMEM((NSTR, tm + 2 * HALO, LANES), F32),
                        pltpu.VMEM((NSTR, tm, LANES), F32), pltpu.VMEM((NSTR, 32 * 8, LANES), F32)],
        input_output_aliases={7: 0},
        compiler_params=_cparams(("arbitrary",)),
    )(dc1, dc1, dc1, proj, proj, proj, conv_wf_s, dp3, *dep_arg)


def _mix_in_bwd(dx1, dp3, x2d, g_mix, w_in_g, layer, tm, dep=None):
    T, D = x2d.shape
    CS = w_in_g.shape[2]
    CN = CS // 3

    def body(dx1_ref, dp_ref, x_ref, g_ref, w_ref, dx_ref, dg_ref):
        i = pl.program_id(0)

        @pl.when(i == 0)
        def _():
            dg_ref[...] = jnp.zeros_like(dg_ref)

        dh = jnp.zeros((tm, D), F32)
        for j in range(12):
            dh = dh + _dot_nt(dp_ref[:, j * CN:(j + 1) * CN], w_ref[j // 3, :, (j % 3) * CN:(j % 3 + 1) * CN])
        x = x_ref[...]
        rstd = lax.rsqrt(jnp.mean(x * x, axis=-1, keepdims=True) + EPS)
        xh = x * rstd
        dg_ref[...] += jnp.sum(dh * xh, axis=0, keepdims=True)
        dxh = dh * g_ref[...]
        dx_ref[...] = dx1_ref[...] + rstd * (dxh - xh * jnp.mean(dxh * xh, axis=-1, keepdims=True))

    row = lambda i: (i, 0)
    body, dep_spec, dep_arg = _after(body, 5, dep)
    return pl.pallas_call(
        body, name="mix_in_bwd", grid=(T // tm,),
        in_specs=[pl.BlockSpec((tm, D), row), pl.BlockSpec((tm, 6 * D), row),
                  pl.BlockSpec((tm, D), row), _res((1, D)), _res((NSHARD, D, CS), lambda i: (0, layer, 0))] + dep_spec,
        out_specs=[pl.BlockSpec((tm, D), row), pl.BlockSpec((1, D), lambda i: (0, 0))],
        out_shape=[_sds((T, D), F32), _sds((1, D), F32)],
        compiler_params=_cparams(("arbitrary",)),
    )(dx1, dp3, x2d, g_mix, w_in_g, *dep_arg)


def _tn_matmul(name, a, a_block, a_map, bs, b_block, b_map, out_shape, out_block, out_map, nj, nt):
    kk = [d for d in a_block if d is not None][-1]
    nn = [d for d in b_block if d is not None][-1]
    nb = len(bs)
    a_list = a if isinstance(a, (list, tuple)) else [a]
    na = len(a_list)

    def body(*refs):
        a_refs, b_refs = refs[:na], refs[na:na + nb]
        o_refs, acc_refs = refs[-2 * nb:-nb], refs[-nb:]
        t = pl.program_id(1)

        @pl.when(t == 0)
        def _():
            for acc_ref in acc_refs:
                acc_ref[...] = jnp.zeros_like(acc_ref)

        a_ts = [a_ref[...].astype(BF16) for a_ref in a_refs]
        for i, (b_ref, acc_ref) in enumerate(zip(b_refs, acc_refs)):
            acc_ref[...] += _dot_tn(a_ts[i % na], b_ref[...].astype(BF16))

        @pl.when(t == nt - 1)
        def _():
            for o_ref, acc_ref in zip(o_refs, acc_refs):
                o_ref[...] = acc_ref[...].astype(o_ref.dtype)

    return pl.pallas_call(
        body, name=name, grid=(nj, nt),
        in_specs=[pl.BlockSpec(a_block, a_map)] * na + [pl.BlockSpec(b_block, b_map)] * nb,
        out_specs=[pl.BlockSpec(out_block, out_map)] * nb, out_shape=[_sds(out_shape, BF16)] * nb,
        scratch_shapes=[pltpu.VMEM((kk, nn), F32)] * nb,
        compiler_params=_cparams(("parallel", "arbitrary")),
    )(*a_list, *bs)


def _place_shard(name, w, pos, dtype, tr, dep=None, layer=None, into=None, row_off=0, out_rows=None):
    R, C = w.shape[-2:]
    out_rows = out_rows or R

    def body(pos_ref, w_ref, *rest):
        del pos_ref
        rest[-1][...] = w_ref[...].astype(dtype)

    in_spec = (pl.BlockSpec((tr, C), lambda r, pos: (r, 0)) if layer is None else
               pl.BlockSpec((None, tr, C), lambda r, pos: (layer, r, 0)))
    extra, extra_args = ([], []) if into is None else ([pl.BlockSpec(memory_space=pl.ANY)], [into])
    body, dep_spec, dep_arg = _after(body, 2 + len(extra), dep)
    grid_spec = pltpu.PrefetchScalarGridSpec(
        num_scalar_prefetch=1, grid=(R // tr,), in_specs=[in_spec] + extra + dep_spec,
        out_specs=pl.BlockSpec((None, tr, C), lambda r, pos: (pos[1], row_off // tr + r, 0)))
    return pl.pallas_call(body, name=name, grid_spec=grid_spec, out_shape=_sds((NSHARD, out_rows, C), dtype),
                          input_output_aliases={} if into is None else {2: 0},
                          compiler_params=_cparams(("parallel",)))(pos, w, *extra_args, *dep_arg)


def _add_halves(name, g, rbuf, pos, tr):
    NS, _, H, C = g.shape

    def body(pos_ref, g_ref, r_ref, o_ref):
        del pos_ref
        o_ref[...] = (g_ref[...].astype(F32) + r_ref[...].astype(F32)).astype(BF16)

    grid_spec = pltpu.PrefetchScalarGridSpec(
        num_scalar_prefetch=1, grid=(NS, H // tr),
        in_specs=[pl.BlockSpec((None, None, tr, C), lambda s, r, pos: (s, pos[0], r, 0)),
                  pl.BlockSpec((None, tr, C), lambda s, r, pos: (s, r, 0))],
        out_specs=pl.BlockSpec((None, tr, C), lambda s, r, pos: (s, r, 0)))
    return pl.pallas_call(body, name=name, grid_spec=grid_spec, out_shape=_sds((NS, H, C), BF16),
                          compiler_params=_cparams(("parallel", "parallel")))(pos, g, rbuf)


def _add_shards(name, p, rbuf, pos, tr, layer, n_layers, prev):
    _, H, C = p.shape

    def body(pos_ref, p_ref, r_ref, *rest):
        del pos_ref
        o_ref = rest[-1]
        acc = p_ref[...].astype(F32)
        for j in range(3):
            acc = acc + r_ref[j].astype(F32)
        o_ref[...] = acc

    in_specs = [pl.BlockSpec((None, tr, C), lambda r, pos: (pos[1], r, 0)),
                pl.BlockSpec((3, tr, C), lambda r, pos: (0, r, 0))]
    args = [pos, p, rbuf]
    aliases = {}
    if prev is not None:
        in_specs.append(pl.BlockSpec(memory_space=pl.ANY))
        args.append(prev)
        aliases = {3: 0}
    grid_spec = pltpu.PrefetchScalarGridSpec(
        num_scalar_prefetch=1, grid=(H // tr,), in_specs=in_specs,
        out_specs=pl.BlockSpec((None, None, tr, C), lambda r, pos: (layer, pos[0], r, 0)))
    return pl.pallas_call(body, name=name, grid_spec=grid_spec, out_shape=_sds((n_layers, 2, H, C), F32),
                          input_output_aliases=aliases, compiler_params=_cparams(("parallel",)))(*args)


def _sum_slots(own, land, me, tr):
    NS8, R, C = land.shape

    def body(me_ref, own_ref, l_ref, o_ref):
        acc = None
        for j in range(NS8):
            term = jnp.where(me_ref[0] == j, own_ref[...], l_ref[j])
            acc = term if acc is None else acc + term
        o_ref[...] = acc

    grid_spec = pltpu.PrefetchScalarGridSpec(
        num_scalar_prefetch=1, grid=(R // tr,),
        in_specs=[pl.BlockSpec((tr, C), lambda i, me: (i, 0)), pl.BlockSpec((NS8, tr, C), lambda i, me: (0, i, 0))],
        out_specs=pl.BlockSpec((tr, C), lambda i, me: (i, 0)))
    return pl.pallas_call(body, name="sum_slots", grid_spec=grid_spec, out_shape=_sds((R, C), F32),
                          compiler_params=_cparams(("parallel",)))(me, own, land)


def _adamw_update(w_ref, g_ref, m_ref, v_ref, d_ref, mo_ref, vo_ref):
    g_ = g_ref[...]
    m_ = ADAM_B1 * m_ref[...] + (1.0 - ADAM_B1) * g_
    v_ = ADAM_B2 * v_ref[...] + (1.0 - ADAM_B2) * (g_ * g_)
    mo_ref[...] = m_
    vo_ref[...] = v_
    m_hat = m_ / (1.0 - ADAM_B1 ** ADAM_STEP)
    v_hat = v_ / (1.0 - ADAM_B2 ** ADAM_STEP)
    d_ref[...] = -ADAM_LR * (m_hat / (jnp.sqrt(v_hat) + ADAM_EPS) + ADAM_WD * w_ref[...])


def _adamw(name, w, g, m, v, tr, emit_g):
    R, C = w.shape
    n_out = 4 if emit_g else 3

    def body(w_ref, g_ref, m_ref, v_ref, d_ref, mo_ref, vo_ref, *go_ref):
        _adamw_update(w_ref, g_ref, m_ref, v_ref, d_ref, mo_ref, vo_ref)
        if emit_g:
            go_ref[0][...] = g_ref[...]

    spec = pl.BlockSpec((tr, C), lambda i: (i, 0))
    return pl.pallas_call(
        body, name=name, grid=(R // tr,), in_specs=[spec] * 4, out_specs=[spec] * n_out,
        out_shape=[_sds((R, C), F32)] * n_out, compiler_params=_cparams(("parallel",)))(w, g, m, v)


def _adamw_many(ws, gs, ms, vs):
    n = len(ws)

    def body(*refs):
        ins, outs = refs[:4 * n], refs[4 * n:]
        for k in range(n):
            _adamw_update(ins[k], ins[n + k], ins[2 * n + k], ins[3 * n + k], outs[k], outs[n + k], outs[2 * n + k])

    vmem = pl.BlockSpec(memory_space=pltpu.VMEM)
    res = pl.pallas_call(
        body, name="adamw_small", in_specs=[vmem] * (4 * n), out_specs=[vmem] * (3 * n),
        out_shape=[_sds(w.shape, F32) for w in ws] * 3,
        compiler_params=pltpu.CompilerParams(vmem_limit_bytes=VMEM_LIMIT))(*ws, *gs, *ms, *vs)
    return list(res[:n]), list(res[n:2 * n]), list(res[2 * n:])


def _row_tile(rows, cap):
    best = rows
    for t in range(8, min(rows, cap) + 1, 8):
        if rows % t == 0:
            best = t
    return best


HBM_SPEC = pl.BlockSpec(memory_space=pltpu.HBM)
SEM_SPEC = pl.BlockSpec(memory_space=pltpu.SEMAPHORE)
DATAFLOW = pltpu.SideEffectType.DATAFLOW_SIDE_EFFECTING
DMA_SEM = pltpu.SemaphoreType.DMA


def _hbm(a):
    return pltpu.with_memory_space_constraint(a, pltpu.HBM)


def _mesh_pos():
    return lax.axis_index("x"), lax.axis_index("y"), lax.axis_index("c")


def _other_chips(x, y):
    return [(1 - x, y), (x, 1 - y), (1 - x, 1 - y)]


def _half_rows(buf, shard, core):
    h = buf.shape[1] // 2
    return buf.at[shard, pl.ds(core * h, h), :]


def _ici_copy(buf, j, send, recv, landing):
    x, y, c = _mesh_pos()
    px, py = _other_chips(x, y)[j]
    part = _half_rows(buf, 2 * px + py if landing else 2 * x + y, c)
    return pltpu.make_async_remote_copy(src_ref=part, dst_ref=part, send_sem=send, recv_sem=recv,
                                        device_id=(px, py, c), device_id_type=MESH_ID)


def _sibling_copy(buf, j, send, recv, landing):
    x, y, c = _mesh_pos()
    px, py = _other_chips(x, y)[j]
    part = _half_rows(buf, 2 * px + py, 1 - c if landing else c)
    return pltpu.make_async_remote_copy(src_ref=part, dst_ref=part, send_sem=send, recv_sem=recv,
                                        device_id=(x, y, 1 - c), device_id_type=MESH_ID)


def _forward_sibling(name, bufs, with_ici):
    n = len(bufs)

    def body(*refs):
        ins = refs[:n]
        send_ici, recv_ici, send_d2d, recv_d2d = refs[2 * n:]
        sends = []
        if with_ici:
            for i in range(n):
                for j in range(3):
                    cp = _ici_copy(ins[i], j, send_ici.at[i, j], recv_ici.at[i, j], False)
                    cp.start()
                    sends.append(cp)
        for i in range(n):
            for j in range(3):
                if with_ici:
                    _ici_copy(ins[i], j, send_ici.at[i, j], recv_ici.at[i, j], True).wait_recv()
                cp = _sibling_copy(ins[i], j, send_d2d.at[i, j], recv_d2d.at[i, j], False)
                cp.start()
                sends.append(cp)
        for i in range(n):
            for j in range(3):
                _sibling_copy(ins[i], j, send_d2d.at[i, j], recv_d2d.at[i, j], True).wait_recv()
        for cp in sends:
            cp.wait_send()

    return pl.pallas_call(
        body, name=name, in_specs=[HBM_SPEC] * n, out_specs=[HBM_SPEC] * n,
        out_shape=[_sds(b.shape, b.dtype) for b in bufs],
        scratch_shapes=[DMA_SEM((n, 3))] * 4, input_output_aliases={i: i for i in range(n)},
    )(*bufs)


def _gather_start(name, groups):
    flat = [b for g in groups for b in g]
    n, ng = len(flat), len(groups)

    def body(*refs):
        ins, sems, token = refs[:n], refs[n:n + 2 * ng], refs[-1]
        k = 0
        for gi, g in enumerate(groups):
            for a in range(len(g)):
                for j in range(3):
                    _ici_copy(ins[k], j, sems[2 * gi], sems[2 * gi + 1], False).start()
                k += 1
        token[...] = jnp.zeros_like(token)

    res = pl.pallas_call(
        body, name=name, in_specs=[HBM_SPEC] * n,
        out_specs=[SEM_SPEC] * (2 * ng) + [HBM_SPEC] * n + [pl.BlockSpec(memory_space=pltpu.VMEM)],
        out_shape=[DMA_SEM(()) for g in groups for _ in range(2)]
        + [pltpu.HBM(b.shape, b.dtype) for b in flat] + [_sds((8, LANES), F32)],
        input_output_aliases={i: 2 * ng + i for i in range(n)},
        compiler_params=pltpu.CompilerParams(has_side_effects=DATAFLOW),
    )(*[_hbm(b) for b in flat])
    sems = [(res[2 * gi], res[2 * gi + 1]) for gi in range(ng)]
    thru, k = [], 2 * ng
    for g in groups:
        thru.append(list(res[k:k + len(g)]))
        k += len(g)
    return sems, thru, res[-1]


def _gather_wait(name, bufs, sems, after):
    n = len(bufs)

    def body(*refs):
        ins, send, recv = refs[:n], refs[n], refs[n + 1]
        for a in range(n):
            for j in range(3):
                _ici_copy(ins[a], j, send, recv, False).wait_send()
                _ici_copy(ins[a], j, send, recv, True).wait_recv()

    return pl.pallas_call(
        body, name=name, in_specs=[HBM_SPEC] * n + [SEM_SPEC, SEM_SPEC, pl.BlockSpec(memory_space=pl.ANY)],
        out_specs=[HBM_SPEC] * n, out_shape=[pltpu.HBM(b.shape, b.dtype) for b in bufs],
        input_output_aliases={i: i for i in range(n)},
        compiler_params=pltpu.CompilerParams(has_side_effects=DATAFLOW),
    )(*bufs, sems[0], sems[1], after)


def _send_sibling_halves(name, arrs):
    n = len(arrs)

    def body(*refs):
        ins, outs = refs[:n], refs[n:2 * n]
        send, recv = refs[2 * n:]
        x, y, c = _mesh_pos()
        cps = []
        for i in range(n):
            cp = pltpu.make_async_remote_copy(
                src_ref=ins[i].at[:, 1 - c], dst_ref=outs[i],
                send_sem=send.at[i], recv_sem=recv.at[i], device_id=(x, y, 1 - c), device_id_type=MESH_ID)
            cp.start()
            cps.append(cp)
        for cp in cps:
            cp.wait()

    return pl.pallas_call(
        body, name=name, in_specs=[HBM_SPEC] * n, out_specs=[HBM_SPEC] * n,
        out_shape=[_sds((a.shape[0],) + a.shape[2:], a.dtype) for a in arrs],
        scratch_shapes=[DMA_SEM((n,)), DMA_SEM((n,))],
    )(*arrs)


def _chip_copy(p, land, j, send, recv):
    x, y, c = _mesh_pos()
    px, py = _other_chips(x, y)[j]
    return pltpu.make_async_remote_copy(src_ref=p.at[2 * px + py], dst_ref=land.at[j], send_sem=send, recv_sem=recv,
                                        device_id=(px, py, c), device_id_type=MESH_ID)


def _chip_send_start(name, ps):
    n = len(ps)
    lands = [lax.empty((3,) + p.shape[1:], p.dtype) for p in ps]

    def body(*refs):
        ins, lnd, send, recv, token = refs[:n], refs[n:2 * n], refs[2 * n], refs[2 * n + 1], refs[-1]
        for i in range(n):
            for j in range(3):
                _chip_copy(ins[i], lnd[i], j, send, recv).start()
        token[...] = jnp.zeros_like(token)

    res = pl.pallas_call(
        body, name=name, in_specs=[HBM_SPEC] * (2 * n),
        out_specs=[SEM_SPEC, SEM_SPEC] + [HBM_SPEC] * (2 * n) + [pl.BlockSpec(memory_space=pltpu.VMEM)],
        out_shape=[DMA_SEM(()), DMA_SEM(())] + [pltpu.HBM(a.shape, a.dtype) for a in ps + lands]
        + [_sds((8, LANES), F32)],
        input_output_aliases={i: 2 + i for i in range(2 * n)},
        compiler_params=pltpu.CompilerParams(has_side_effects=DATAFLOW),
    )(*[_hbm(a) for a in ps + lands])
    return (res[0], res[1]), list(res[2:2 + n]), list(res[2 + n:2 + 2 * n]), res[-1]


def _chip_send_wait(name, ps, lands, sems, after):
    n = len(ps)

    def body(*refs):
        ins, lnd, send, recv = refs[:n], refs[n:2 * n], refs[2 * n], refs[2 * n + 1]
        for i in range(n):
            for j in range(3):
                cp = _chip_copy(ins[i], lnd[i], j, send, recv)
                cp.wait_send()
                cp.wait_recv()

    res = pl.pallas_call(
        body, name=name, in_specs=[HBM_SPEC] * (2 * n) + [SEM_SPEC, SEM_SPEC, pl.BlockSpec(memory_space=pl.ANY)],
        out_specs=[HBM_SPEC] * (2 * n), out_shape=[pltpu.HBM(a.shape, a.dtype) for a in ps + lands],
        input_output_aliases={i: i for i in range(2 * n)},
        compiler_params=pltpu.CompilerParams(has_side_effects=DATAFLOW),
    )(*ps, *lands, sems[0], sems[1], after)
    return list(res[:n]), list(res[n:])


def _join_halves(arrs):
    n = len(arrs)

    def body(*refs):
        bufs = refs[n:2 * n]
        send, recv = refs[2 * n:]
        x, y, c = _mesh_pos()
        cps = []
        for i in range(n):
            mine = bufs[i].at[:, c]
            cp = pltpu.make_async_remote_copy(
                src_ref=mine, dst_ref=mine, send_sem=send.at[i], recv_sem=recv.at[i],
                device_id=(x, y, 1 - c), device_id_type=MESH_ID)
            cp.start()
            cps.append(cp)
        for i, cp in enumerate(cps):
            theirs = bufs[i].at[:, 1 - c]
            cp.wait_send()
            pltpu.make_async_remote_copy(
                src_ref=theirs, dst_ref=theirs, send_sem=send.at[i], recv_sem=recv.at[i],
                device_id=(x, y, 1 - c), device_id_type=MESH_ID).wait_recv()

    return pl.pallas_call(
        body, name="join_halves", in_specs=[HBM_SPEC] * n, out_specs=[HBM_SPEC] * n,
        out_shape=[_sds(a.shape, a.dtype) for a in arrs],
        scratch_shapes=[DMA_SEM((n,)), DMA_SEM((n,))], input_output_aliases={i: i for i in range(n)},
    )(*arrs)


def _peer_copy(buf, land, k, send, recv, landing):
    x, y, c = _mesh_pos()
    px, py, pc = x ^ ((k >> 2) & 1), y ^ ((k >> 1) & 1), c ^ (k & 1)
    slot = 4 * px + 2 * py + pc if landing else 4 * x + 2 * y + c
    return pltpu.make_async_remote_copy(src_ref=buf, dst_ref=land.at[slot], send_sem=send, recv_sem=recv,
                                        device_id=(px, py, pc), device_id_type=MESH_ID)


def _exchange_all(buf):
    def body(in_ref, out_ref, send, recv):
        cps = [_peer_copy(in_ref, out_ref, k, send.at[k - 1], recv.at[k - 1], False) for k in range(1, 8)]
        for cp in cps:
            cp.start()
        for k in range(1, 8):
            cps[k - 1].wait_send()
            _peer_copy(in_ref, out_ref, k, send.at[k - 1], recv.at[k - 1], True).wait_recv()

    return pl.pallas_call(
        body, name="exchange_all", in_specs=[HBM_SPEC], out_specs=HBM_SPEC,
        out_shape=_sds((8,) + buf.shape, buf.dtype), scratch_shapes=[DMA_SEM((7,)), DMA_SEM((7,))],
    )(buf)


def _exchange_start(name, buf):
    land = lax.empty((8,) + buf.shape, buf.dtype)

    def body(in_ref, land_ref, send, recv, in_thru, land_thru, token):
        for k in range(1, 8):
            _peer_copy(in_ref, land_ref, k, send, recv, False).start()
        token[...] = jnp.zeros_like(token)

    res = pl.pallas_call(
        body, name=name, in_specs=[HBM_SPEC] * 2,
        out_specs=[SEM_SPEC, SEM_SPEC, HBM_SPEC, HBM_SPEC, pl.BlockSpec(memory_space=pltpu.VMEM)],
        out_shape=[DMA_SEM(()), DMA_SEM(()), pltpu.HBM(buf.shape, buf.dtype), pltpu.HBM(land.shape, land.dtype),
                   _sds((8, LANES), F32)],
        input_output_aliases={0: 2, 1: 3}, compiler_params=pltpu.CompilerParams(has_side_effects=DATAFLOW),
    )(_hbm(buf), _hbm(land))
    return (res[0], res[1]), res[2], res[3], res[4]


def _exchange_wait(name, buf, land, sems, after):
    def body(in_ref, land_ref, send, recv, after_ref, in_thru, land_thru):
        for k in range(1, 8):
            _peer_copy(in_ref, land_ref, k, send, recv, False).wait_send()
            _peer_copy(in_ref, land_ref, k, send, recv, True).wait_recv()

    res = pl.pallas_call(
        body, name=name, in_specs=[HBM_SPEC, HBM_SPEC, SEM_SPEC, SEM_SPEC, pl.BlockSpec(memory_space=pl.ANY)],
        out_specs=[HBM_SPEC, HBM_SPEC], out_shape=[pltpu.HBM(buf.shape, buf.dtype), pltpu.HBM(land.shape, land.dtype)],
        input_output_aliases={0: 0, 1: 1}, compiler_params=pltpu.CompilerParams(has_side_effects=DATAFLOW),
    )(buf, land, sems[0], sems[1], after)
    return res[0], res[1]


def _pad_to(a, axis, size):
    pad = [(0, 0)] * a.ndim
    pad[axis] = (0, size - a.shape[axis])
    return jnp.pad(a, pad)


def _strips(w):
    k, d = w.shape
    return _pad_to(w, 0, 32).reshape(32, d // LANES, LANES).transpose(1, 0, 2)


def kernel(x, norm_mix, w_in, gate_bias, conv_w, conv_b, conv_ln_g, conv_ln_b, w_conv_out, sgu_ln_g, sgu_ln_b, w_spatial, b_spatial, w_sgu_out, w_o, norm_ffn, w_ffn_gate, w_ffn_up, w_ffn_down, norm_final, loss_target, m_norm_mix, m_w_in, m_gate_bias, m_conv_w, m_conv_b, m_conv_ln_g, m_conv_ln_b, m_w_conv_out, m_sgu_ln_g, m_sgu_ln_b, m_w_spatial, m_b_spatial, m_w_sgu_out, m_w_o, m_norm_ffn, m_w_ffn_gate, m_w_ffn_up, m_w_ffn_down, m_norm_final, v_norm_mix, v_w_in, v_gate_bias, v_conv_w, v_conv_b, v_conv_ln_g, v_conv_ln_b, v_w_conv_out, v_sgu_ln_g, v_sgu_ln_b, v_w_spatial, v_b_spatial, v_w_sgu_out, v_w_o, v_norm_ffn, v_w_ffn_gate, v_w_ffn_up, v_w_ffn_down, v_norm_final):
    BL, S, D = x.shape
    T = BL * S
    L = w_in.shape[0]
    CS = w_in.shape[2]
    CN = CS // 3
    DQ = D // NSHARD
    FS = w_ffn_gate.shape[2]
    F = NSHARD * FS
    G, CH = w_spatial.shape[1], w_spatial.shape[2]
    KW = conv_w.shape[1]
    CQ = conv_w.shape[3]
    NSTR = D // LANES
    tm = min(512, S // 2)
    tm2 = max(tm // 2, CH)
    rb = min(64, tm)
    mx, my, mc = _mesh_pos()
    pos = jnp.stack([mc, 2 * mx + my]).astype(jnp.int32)

    def placed(name, w, dtype=BF16, dep=None, layer=None, **kw):
        return _place_shard("place_" + name, w, pos, dtype, _row_tile(w.shape[-2], 256), dep, layer, **kw)

    w_in0 = placed("w_in", w_in, layer=0)
    cw_p = placed("conv_w", _pad_to(conv_w.reshape(L, KW, CQ), 1, 32).reshape(L * 32, CQ), F32)
    fsems, fflying, ftoken = _gather_start("gather_start_first", [[w_in0, cw_p]])
    wts = []
    for l in range(L):
        w_sq = None
        for i, w in enumerate([w_conv_out, w_sgu_out, w_o]):
            w_sq = placed("w_sq", w, dep=ftoken, layer=l, into=w_sq, row_off=i * DQ, out_rows=3 * DQ)
        wts.append(dict(w_in=placed("w_in", w_in, dep=ftoken, layer=l) if l else None, w_sq=w_sq,
                        wg=placed("w_gate", w_ffn_gate[l].T, dep=ftoken), wu=placed("w_up", w_ffn_up[l].T, dep=ftoken),
                        wd=placed("w_down", w_ffn_down, dep=ftoken, layer=l)))
    ffn_keys = ["wg", "wu", "wd"]
    order = [[(0, "w_sq")], [(0, k) for k in ffn_keys]]
    order += [[(l, k) for k in ["w_in", "w_sq"] + ffn_keys] for l in range(1, L)]
    gsems, flying, token = _gather_start("gather_start", [[wts[l][k] for l, k in grp] for grp in order])
    first = _gather_wait("gather_wait_first", fflying[0], fsems[0], token)
    wts[0]["w_in"], cw_g = _forward_sibling("gather_first", first, False)
    conv_w_full = cw_g.reshape(NSHARD, L, 32, CQ).transpose(1, 2, 0, 3).reshape(L, 32, D)[:, :KW]

    def land(gi, after):
        bufs = _gather_wait("gather_wait_%d" % gi, flying[gi], gsems[gi], after)
        bufs = _forward_sibling("gather_forward_%d" % gi, bufs, False)
        for (l, k), b in zip(order[gi], bufs):
            wts[l][k] = b

    x2d = x.reshape(T, D)
    tgt = loss_target.reshape(T, D)
    row = lambda a, l: a[l].reshape(1, -1)

    saved = []
    xc = x2d
    for l in range(L):
        ws_b = w_spatial[l].astype(BF16)
        bs_b = jnp.repeat(b_spatial[l].T, D // G, axis=1)
        cw_s = _strips(conv_w_full[l])
        h, proj = _mix_in_fwd(xc, row(norm_mix, l), wts[l]["w_in"], 0, tm, token if l == 0 else None)
        if l == 0:
            land(0, h)
        c1h, rstd_c, c3, ya = _conv_fwd(proj, cw_s, row(conv_b, l), row(conv_ln_g, l), row(conv_ln_b, l),
                                        wts[l]["w_sq"], 0, S, tm, rb)
        if l == 0:
            land(1, ya)
        mixed, gated, yb, merged, x1 = _sgu_merge_fwd(proj, ya, xc, row(sgu_ln_g, l), row(sgu_ln_b, l), ws_b, bs_b,
                                                      row(gate_bias, l), wts[l]["w_sq"], 0, tm2)
        ffn_w = [wts[l][k].reshape(F, D) for k in ffn_keys]
        h2, gt, up, act, x2 = _ffn_fwd(x1, row(norm_ffn, l), *ffn_w, tm2)
        if l + 1 < L:
            land(l + 2, x2)
        saved.append(dict(x=xc, h=h, proj=proj, c1h=c1h, rstd_c=rstd_c, c3=c3, ya=ya, mixed=mixed, gated=gated,
                          yb=yb, merged=merged, x1=x1, h2=h2, gt=gt, up=up, act=act, ws_b=ws_b, cw=conv_w_full[l]))
        xc = x2

    dx, loss_part, d_norm_final = _loss_head(xc, norm_final.reshape(1, D), tgt, tm)
    loss = lax.psum(loss_part[0, 0], ("x", "y", "c"))

    g_acc = {}

    def reduce_start(tag, layer, named):
        arrs = [g.reshape(NSHARD, 2, g.shape[1] // 2, g.shape[2]) for _, g in named]
        from_sib = _send_sibling_halves("sibling_" + tag, arrs)
        ps = [_add_halves("presum_" + nm, a, r, pos, _row_tile(a.shape[2], 256))
              for (nm, _), a, r in zip(named, arrs, from_sib)]
        sems, ps, lands, tok = _chip_send_start("chip_send_start_" + tag, ps)
        return dict(tag=tag, layer=layer, names=[nm for nm, _ in named], ps=ps, lands=lands, sems=sems), tok

    def reduce_finish(pend, after):
        ps, lands = _chip_send_wait("chip_send_wait_" + pend["tag"], pend["ps"], pend["lands"], pend["sems"], after)
        for nm, p, r in zip(pend["names"], ps, lands):
            g_acc[nm] = _add_shards("shardsum_" + nm, p, r, pos, _row_tile(p.shape[1], 256), pend["layer"], L,
                                    g_acc.get(nm))

    me_idx = (4 * mx + 2 * my + mc).astype(jnp.int32).reshape(1)
    exchanges = []

    def pack_rows(pieces):
        packed = jnp.concatenate(pieces, axis=0)
        return _pad_to(packed, 0, -(-packed.shape[0] // 8) * 8)

    def unpack_rows(summed, pieces):
        out, off = [], 0
        for p in pieces:
            out.append(summed[off:off + p.shape[0]])
            off += p.shape[0]
        return out

    def small_start(tag, pieces):
        sems, buf, land, token = _exchange_start("exchange_start_" + tag, pack_rows(pieces))
        return dict(tag=tag, pieces=pieces, buf=buf, land=land, sems=sems, token=token)

    def small_finish(st, after):
        buf, land = _exchange_wait("exchange_wait_" + st["tag"], st["buf"], st["land"], st["sems"], after)
        return unpack_rows(_sum_slots(buf, land, me_idx, _row_tile(buf.shape[0], 256)), st["pieces"])

    small = [None] * L
    tt = min(2048, T // 2)
    nt = T // tt
    pending, tok = None, None
    for l in reversed(range(L)):
        sv, wt = saved[l], wts[l]
        ffn_w = [wt[k].reshape(F, D) for k in ffn_keys]
        dx1, dgt, dup, d_norm_ffn = _ffn_bwd(dx, sv["x1"], sv["gt"], sv["up"], row(norm_ffn, l), *ffn_w, tm2, tok)
        tn_a = ((tt, F // 2), lambda j, t: (t, j))
        tn_b = ((tt, D), lambda j, t: (t, 0))
        tn_o = ((F, D), (F // 2, D), lambda j, t: (j, 0), 2, nt)
        g_g, = _tn_matmul("grad_w_gate", dgt, *tn_a, [sv["h2"]], *tn_b, *tn_o)
        g_u, = _tn_matmul("grad_w_up", dup, *tn_a, [sv["h2"]], *tn_b, *tn_o)
        g_d, = _tn_matmul("grad_w_down", sv["act"], *tn_a, [dx], *tn_b, *tn_o)
        if pending is not None:
            reduce_finish(pending, g_d)
        ffn_pend, tok = reduce_start("ffn%d" % l, l, [
            ("w_ffn_gate", g_g.reshape(NSHARD, FS, D)), ("w_ffn_up", g_u.reshape(NSHARD, FS, D)),
            ("w_ffn_down", g_d.reshape(NSHARD, FS, D))])
        wst_b = jnp.swapaxes(sv["ws_b"], 1, 2)
        dya, dyb, dp3, d_gate_bias, d_sgu_g, d_sgu_b, d_bs, d_ws = _merge_sgu_bwd(
            dx1, sv["proj"], sv["ya"], sv["yb"], sv["mixed"], row(sgu_ln_g, l), row(sgu_ln_b, l), sv["ws_b"], wst_b,
            row(gate_bias, l), wt["w_sq"], 0, tm2, tok)
        sq_args = ((tt, D), lambda j, t: (t, 0))
        sq_out = ((D, D), (D, D), lambda j, t: (0, 0), 1, nt)
        g_o, = _tn_matmul("grad_w_o", sv["merged"], *sq_args, [dx1], *sq_args, *sq_out)
        g_so, = _tn_matmul("grad_w_sgu_out", sv["gated"], *sq_args, [dyb], *sq_args, *sq_out)
        g_co, = _tn_matmul("grad_w_conv_out", sv["c3"], *sq_args, [dya], *sq_args, *sq_out)
        dc1, d_cln_g, d_cln_b, d_conv_b = _conv_ln_bwd(dya, sv["c1h"], sv["rstd_c"], row(conv_ln_g, l),
                                                       row(conv_ln_b, l), wt["w_sq"], 0, tm)
        small[l] = [None, d_gate_bias.reshape(2, D), None, d_conv_b, d_cln_g, d_cln_b, d_sgu_g, d_sgu_b,
                    d_ws.reshape(G * CH * CH // D, D), d_bs.reshape(G * CH // D, D), d_norm_ffn]
        tok_x = None
        if l == 0:
            early = [k for k in range(len(small[0])) if small[0][k] is not None]
            exchanges.append((small_start("early0", [small[0][k] for k in early]), [(0, k) for k in early]))
            tok_x = exchanges[-1][0]["token"]
        dp3, d_cw_s = _conv_bwd(dc1, sv["proj"], dp3, _strips(sv["cw"][::-1]), S, tm, rb, tok_x)
        g_in, = _tn_matmul("grad_w_in", sv["h"], (tt, D), lambda j, t: (t, 0), [dp3], (tt, CS), lambda j, t: (t, j),
                           (NSHARD, D, CS), (None, D, CS), lambda j, t: (j, 0, 0), NSHARD, nt)
        reduce_finish(ffn_pend, g_in)
        pending, tok = reduce_start("mix%d" % l, l, [
            ("w_in", g_in), ("w_conv_out", g_co.reshape(NSHARD, DQ, D)), ("w_sgu_out", g_so.reshape(NSHARD, DQ, D)),
            ("w_o", g_o.reshape(NSHARD, DQ, D))])
        dx, d_norm_mix = _mix_in_bwd(dx1, dp3, sv["x"], row(norm_mix, l), wt["w_in"], 0, tm, tok)
        small[l][0] = d_norm_mix
        small[l][2] = d_cw_s.transpose(1, 0, 2).reshape(32, D)
        if l > 0:
            exchanges.append((small_start("layer%d" % l, small[l]), [(l, k) for k in range(len(small[l]))]))
            tok = [tok, exchanges[-1][0]["token"]]
    reduce_finish(pending, dx)
    grad_x = dx.reshape(BL, S, D)

    names = ["w_in", "w_conv_out", "w_sgu_out", "w_o", "w_ffn_gate", "w_ffn_up", "w_ffn_down"]
    g_full = _join_halves([g_acc[nm] for nm in names])
    g_w_in, g_w_co, g_w_so, g_w_o, g_w_g, g_w_u, g_w_d = [g.reshape(L, 2 * g.shape[2], g.shape[3]) for g in g_full]
    g_w_g = jnp.swapaxes(g_w_g, 1, 2)
    g_w_u = jnp.swapaxes(g_w_u, 1, 2)

    late = [small[0][0], small[0][2], d_norm_final]
    packed = pack_rows(late)
    summed = _sum_slots(packed, _exchange_all(packed), me_idx, _row_tile(packed.shape[0], 256))
    sg = [[None] * len(small[l]) for l in range(L)]
    sg[0][0], sg[0][2], g_norm_final = unpack_rows(summed, late)
    g_norm_final = g_norm_final[0]
    for st, where in exchanges:
        for (l, k), piece in zip(where, small_finish(st, summed)):
            sg[l][k] = piece

    def per_layer(k, shape):
        return jnp.stack([sg[l][k] for l in range(L)]).reshape(shape)

    g_norm_mix = per_layer(0, (L, D))
    g_gate_bias = per_layer(1, (L, 2 * D))
    g_conv_w_full = jnp.stack([sg[l][2][:KW] for l in range(L)])
    g_conv_w = lax.dynamic_slice_in_dim(g_conv_w_full, (2 * mx + my) * CQ, CQ, axis=2).reshape(L, KW, 1, CQ)
    g_conv_b = per_layer(3, (L, D))
    g_conv_ln_g = per_layer(4, (L, D))
    g_conv_ln_b = per_layer(5, (L, D))
    g_sgu_ln_g = per_layer(6, (L, D))
    g_sgu_ln_b = per_layer(7, (L, D))
    g_w_spatial = per_layer(8, (L, G, CH, CH))
    g_b_spatial = per_layer(9, (L, G, CH))
    g_norm_ffn = per_layer(10, (L, D))

    grads = [g_norm_mix, g_w_in, g_gate_bias, g_conv_w, g_conv_b, g_conv_ln_g, g_conv_ln_b, g_w_co, g_sgu_ln_g,
             g_sgu_ln_b, g_w_spatial, g_b_spatial, g_w_so, g_w_o, g_norm_ffn, g_w_g, g_w_u, g_w_d, g_norm_final]
    weights = [norm_mix, w_in, gate_bias, conv_w, conv_b, conv_ln_g, conv_ln_b, w_conv_out, sgu_ln_g, sgu_ln_b,
               w_spatial, b_spatial, w_sgu_out, w_o, norm_ffn, w_ffn_gate, w_ffn_up, w_ffn_down, norm_final]
    ms = [m_norm_mix, m_w_in, m_gate_bias, m_conv_w, m_conv_b, m_conv_ln_g, m_conv_ln_b, m_w_conv_out, m_sgu_ln_g,
          m_sgu_ln_b, m_w_spatial, m_b_spatial, m_w_sgu_out, m_w_o, m_norm_ffn, m_w_ffn_gate, m_w_ffn_up,
          m_w_ffn_down, m_norm_final]
    vs = [v_norm_mix, v_w_in, v_gate_bias, v_conv_w, v_conv_b, v_conv_ln_g, v_conv_ln_b, v_w_conv_out, v_sgu_ln_g,
          v_sgu_ln_b, v_w_spatial, v_b_spatial, v_w_sgu_out, v_w_o, v_norm_ffn, v_w_ffn_gate, v_w_ffn_up,
          v_w_ffn_down, v_norm_final]

    big_idx = [1, 7, 12, 13, 15, 16, 17]
    transposed = [15, 16]
    deltas, new_m, new_v = [None] * 19, [None] * 19, [None] * 19
    for k in big_idx:
        shp = weights[k].shape
        r2 = (shp[0] * shp[1], shp[2])
        res = _adamw("adamw_" + str(k), weights[k].reshape(r2), grads[k].reshape(r2), ms[k].reshape(r2),
                     vs[k].reshape(r2), _row_tile(r2[0], 256), k not in transposed)
        deltas[k], new_m[k], new_v[k] = [a.reshape(shp) for a in res[:3]]
        if k not in transposed:
            grads[k] = res[3].reshape(shp)
    small_idx = [k for k in range(19) if k not in big_idx]
    pick = lambda arrs: [arrs[k].reshape(1, -1) if arrs[k].ndim == 1 else arrs[k] for k in small_idx]
    d_, m_, v_ = _adamw_many(pick(weights), pick(grads), pick(ms), pick(vs))
    for i, k in enumerate(small_idx):
        shp = weights[k].shape
        deltas[k], new_m[k], new_v[k] = d_[i].reshape(shp), m_[i].reshape(shp), v_[i].reshape(shp)

    return (loss, grad_x, *grads, *deltas, *new_m, *new_v)
```

```python
import functools

import jax
import jax.numpy as jnp
from jax import lax
from jax.experimental import pallas as pl
from jax.experimental.pallas import tpu as pltpu

F32 = jnp.float32
BF16 = jnp.bfloat16
EPS = 1e-6
ADAM_LR = 0.001
ADAM_B1 = 0.9
ADAM_B2 = 0.999
ADAM_EPS = 1e-08
ADAM_WD = 0.01
ADAM_STEP = 10

NSHARD = 4
LANES = 128
HALO = 16
VMEM_LIMIT = 60 * 1024 * 1024
MESH_ID = pl.DeviceIdType.MESH


def _dot(a, b):
    return jnp.dot(a, b, preferred_element_type=F32)


def _dot_nt(a, b):
    return lax.dot_general(a, b, (((1,), (1,)), ((), ())), preferred_element_type=F32)


def _dot_tn(a, b):
    return lax.dot_general(a, b, (((0,), (0,)), ((), ())), preferred_element_type=F32)


def _sig(z):
    return 1.0 / (1.0 + jnp.exp(-z))


def _res(shape, imap=None):
    nd = len(shape)
    if imap is None:
        imap = lambda *_: (0,) * nd
    return pl.BlockSpec(shape, imap, pipeline_mode=pl.Buffered(1))


def _cparams(sem):
    return pltpu.CompilerParams(dimension_semantics=sem, vmem_limit_bytes=VMEM_LIMIT)


def _sds(shape, dtype):
    return jax.ShapeDtypeStruct(shape, dtype)


def _after(body, n_in, dep):
    deps = [] if dep is None else [d for d in (dep if isinstance(dep, (list, tuple)) else [dep]) if d is not None]
    if not deps:
        return body, [], []

    def wrapped(*refs):
        return body(*refs[:n_in], *refs[n_in + len(deps):])

    return wrapped, [pl.BlockSpec(memory_space=pl.ANY)] * len(deps), deps


def _mix_in_fwd(x2d, g_mix, w_in_g, layer, tm, dep=None):
    T, D = x2d.shape
    CS = w_in_g.shape[2]
    CN = CS // 3

    def body(x_ref, g_ref, w_ref, h_ref, p_ref):
        x = x_ref[...]
        rstd = lax.rsqrt(jnp.mean(x * x, axis=-1, keepdims=True) + EPS)
        h = (x * rstd * g_ref[...]).astype(BF16)
        h_ref[...] = h
        for s in range(NSHARD):
            for j in range(3):
                c0 = s * CS + j * CN
                p_ref[:, c0:c0 + CN] = _dot(h, w_ref[s, :, j * CN:(j + 1) * CN]).astype(BF16)

    body, dep_spec, dep_arg = _after(body, 3, dep)
    return pl.pallas_call(
        body, name="mix_in_fwd", grid=(T // tm,),
        in_specs=[pl.BlockSpec((tm, D), lambda i: (i, 0)), _res((1, D)),
                  _res((NSHARD, D, CS), lambda i: (0, layer, 0))] + dep_spec,
        out_specs=[pl.BlockSpec((tm, D), lambda i: (i, 0)), pl.BlockSpec((tm, NSHARD * CS), lambda i: (i, 0))],
        out_shape=[_sds((T, D), BF16), _sds((T, NSHARD * CS), BF16)],
        compiler_params=_cparams(("parallel",)),
    )(x2d, g_mix, w_in_g, *dep_arg)


def _halo_maps(tm, n_rows):
    nb = tm // HALO
    last = n_rows // HALO - 1
    prev = lambda i: (jnp.maximum(i * nb - 1, 0), 0)
    nxt = lambda i: (jnp.minimum((i + 1) * nb, last), 0)
    return prev, nxt


def _dwconv(pad_ref, w_ref, out_ref, n_strips, tm, kw, rb):
    off = HALO - (kw - 1) // 2

    def strip(cs, carry):
        for r0 in range(0, tm, rb):
            acc = jnp.zeros((rb, LANES), F32)
            for k in range(kw):
                r = r0 + off + k
                acc = acc + w_ref[cs, k:k + 1, :] * pad_ref[cs, r:r + rb, :]
            out_ref[cs, r0:r0 + rb, :] = acc
        return carry

    lax.fori_loop(0, n_strips, strip, 0)


def _fill_c0_pad(pad_ref, pa_ref, pprev_ref, pnext_ref, D, tm, first, last):
    for cs in range(D // LANES):
        lo, hi = cs * LANES, (cs + 1) * LANES

        def c0_of(ref):
            return ref[:, lo:hi].astype(F32) * _sig(ref[:, D + lo:D + hi].astype(F32))

        pad_ref[cs, HALO:HALO + tm, :] = c0_of(pa_ref)
        pad_ref[cs, 0:HALO, :] = jnp.where(first, 0.0, c0_of(pprev_ref))
        pad_ref[cs, HALO + tm:HALO + tm + HALO, :] = jnp.where(last, 0.0, c0_of(pnext_ref))


def _conv_fwd(proj, conv_w_s, conv_b, ln_g, ln_b, w_sq_g, layer, seq, tm, rb):
    T = proj.shape[0]
    D = conv_b.shape[1]
    DQ = D // NSHARD
    NSTR = D // LANES
    KW = 31
    tps = seq // tm
    prev, nxt = _halo_maps(tm, T)

    def body(pa_ref, pprev_ref, pnext_ref, w_ref, b_ref, g_ref, be_ref, wco_ref,
             c1h_ref, rstd_ref, c3_ref, ya_ref, pad_ref, c1s_ref):
        i = pl.program_id(0)
        first = (i % tps) == 0
        last = (i % tps) == tps - 1
        _fill_c0_pad(pad_ref, pa_ref, pprev_ref, pnext_ref, D, tm, first, last)
        _dwconv(pad_ref, w_ref, c1s_ref, NSTR, tm, KW, rb)
        wco = wco_ref[...].reshape(D, D)
        for r0 in (0, tm // 2):
            rows = slice(r0, r0 + tm // 2)
            c1 = jnp.concatenate([c1s_ref[cs, rows, :] for cs in range(NSTR)], axis=1) + b_ref[...]
            mu = jnp.mean(c1, axis=-1, keepdims=True)
            cc = c1 - mu
            rstd = lax.rsqrt(jnp.mean(cc * cc, axis=-1, keepdims=True) + EPS)
            c1h = cc * rstd
            c1h_ref[rows, :] = c1h.astype(BF16)
            rstd_ref[rows, :] = rstd
            c2 = c1h * g_ref[...] + be_ref[...]
            c3 = (c2 * _sig(c2)).astype(BF16)
            c3_ref[rows, :] = c3
            ya_ref[rows, :] = _dot(c3, wco).astype(BF16)

    row = lambda i: (i, 0)
    return pl.pallas_call(
        body, name="conv_fwd", grid=(T // tm,),
        in_specs=[pl.BlockSpec((tm, 2 * D), row), pl.BlockSpec((HALO, 2 * D), prev), pl.BlockSpec((HALO, 2 * D), nxt),
                  _res((NSTR, 32, LANES)), _res((1, D)), _res((1, D)), _res((1, D)),
                  _res((NSHARD, DQ, D), lambda i: (0, layer * 3 + 0, 0))],
        out_specs=[pl.BlockSpec((tm, D), row), pl.BlockSpec((tm, 1), row), pl.BlockSpec((tm, D), row),
                   pl.BlockSpec((tm, D), row)],
        out_shape=[_sds((T, D), BF16), _sds((T, 1), F32), _sds((T, D), BF16), _sds((T, D), BF16)],
        scratch_shapes=[pltpu.VMEM((NSTR, tm + 2 * HALO, LANES), F32), pltpu.VMEM((NSTR, tm, LANES), F32)],
        compiler_params=_cparams(("parallel",)),
    )(proj, proj, proj, conv_w_s, conv_b, ln_g, ln_b, w_sq_g)


def _ffn_chunks(F):
    assert F % 256 == 0, F
    return [(c0, min(512, F - c0)) for c0 in range(0, F, 512)]


def _sgu_ffn_fwd(proj, ya, x2d, ln_g, ln_b, ws_b, bs_b, gate_bias, w_sq_g, layer, g_ffn, wgt, wut, wd, tm):
    T, D = x2d.shape
    DQ = D // NSHARD
    G, CH, _ = ws_b.shape
    GD = D // G
    F = wd.shape[0]

    def body(puv_ref, pg_ref, ya_ref, x_ref, g_ref, be_ref, ws_ref, bsb_ref, gb_ref, wso_ref, wo_ref,
             gf_ref, wg_ref, wu_ref, wd_ref,
             mixed_ref, gated_ref, yb_ref, merged_ref, x1_ref, h2_ref, gt_ref, up_ref, act_ref, x2_ref, mix_scr):
        u = puv_ref[:, :D].astype(F32)
        v = puv_ref[:, D:].astype(F32)
        mu = jnp.mean(v, axis=-1, keepdims=True)
        vc = v - mu
        rstd = lax.rsqrt(jnp.mean(vc * vc, axis=-1, keepdims=True) + EPS)
        vn = (vc * rstd * g_ref[...] + be_ref[...]).astype(BF16)
        nch = tm // CH
        for g in range(G):
            cols = slice(g * GD, (g + 1) * GD)
            rhs = jnp.concatenate([vn[ch * CH:(ch + 1) * CH, cols] for ch in range(nch)], axis=1)
            res = _dot(ws_ref[g], rhs)
            for ch in range(nch):
                mix_scr[ch * CH:(ch + 1) * CH, cols] = res[:, ch * GD:(ch + 1) * GD] + bsb_ref[:, cols]
        mixed = mix_scr[...]
        mixed_ref[...] = mixed.astype(BF16)
        gated = (u * mixed).astype(BF16)
        gated_ref[...] = gated
        yb = _dot(gated, wso_ref[...].reshape(D, D))
        yb_ref[...] = yb.astype(BF16)
        sa = _sig(pg_ref[:, :D].astype(F32) + gb_ref[:, :D])
        sb = _sig(pg_ref[:, D:].astype(F32) + gb_ref[:, D:])
        merged = (sa * ya_ref[...].astype(F32) + sb * yb).astype(BF16)
        merged_ref[...] = merged
        x1 = x_ref[...] + _dot(merged, wo_ref[...].reshape(D, D))
        x1_ref[...] = x1

        rstd = lax.rsqrt(jnp.mean(x1 * x1, axis=-1, keepdims=True) + EPS)
        h2 = (x1 * rstd * gf_ref[...]).astype(BF16)
        h2_ref[...] = h2
        acc = x1
        chunks = _ffn_chunks(F)

        def gate_up(c0, cw):
            return _dot_nt(h2, wg_ref[c0:c0 + cw, :]), _dot_nt(h2, wu_ref[c0:c0 + cw, :])

        nxt = gate_up(*chunks[0])
        for ci, (c0, cw) in enumerate(chunks):
            gt, up = nxt
            if ci + 1 < len(chunks):
                nxt = gate_up(*chunks[ci + 1])
            gt_ref[:, c0:c0 + cw] = gt.astype(BF16)
            up_ref[:, c0:c0 + cw] = up.astype(BF16)
            act = (gt * _sig(gt) * up).astype(BF16)
            act_ref[:, c0:c0 + cw] = act
            acc = acc + _dot(act, wd_ref[c0:c0 + cw, :])
        x2_ref[...] = acc

    row = lambda i: (i, 0)
    wide = pl.BlockSpec((tm, F), row)
    return pl.pallas_call(
        body, name="sgu_ffn_fwd", grid=(T // tm,),
        in_specs=[pl.BlockSpec((tm, 2 * D), lambda i: (i, 1)), pl.BlockSpec((tm, 2 * D), lambda i: (i, 2)),
                  pl.BlockSpec((tm, D), row), pl.BlockSpec((tm, D), row),
                  _res((1, D)), _res((1, D)), _res((G, CH, CH)), _res((CH, D)), _res((1, 2 * D)),
                  _res((NSHARD, DQ, D), lambda i: (0, layer * 3 + 1, 0)),
                  _res((NSHARD, DQ, D), lambda i: (0, layer * 3 + 2, 0)),
                  _res((1, D)), _res((F, D)), _res((F, D)), _res((F, D))],
        out_specs=[pl.BlockSpec((tm, D), row)] * 6 + [wide, wide, wide, pl.BlockSpec((tm, D), row)],
        out_shape=[_sds((T, D), BF16)] * 4 + [_sds((T, D), F32), _sds((T, D), BF16)] + [_sds((T, F), BF16)] * 3
        + [_sds((T, D), F32)],
        scratch_shapes=[pltpu.VMEM((tm, D), F32)],
        compiler_params=_cparams(("parallel",)),
    )(proj, proj, ya, x2d, ln_g, ln_b, ws_b, bs_b, gate_bias, w_sq_g, w_sq_g, g_ffn, wgt, wut, wd)


def _loss_head(xf, g_fin, target, tm):
    T, D = xf.shape
    n = T // tm

    def body(x_ref, g_ref, t_ref, dx_ref, loss_ref, dg_ref, acc_ref):
        i = pl.program_id(0)

        @pl.when(i == 0)
        def _():
            acc_ref[...] = jnp.zeros_like(acc_ref)
            dg_ref[...] = jnp.zeros_like(dg_ref)

        x = x_ref[...]
        g = g_ref[...]
        rstd = lax.rsqrt(jnp.mean(x * x, axis=-1, keepdims=True) + EPS)
        xh = x * rstd
        diff = xh * g - t_ref[...]
        acc_ref[...] += jnp.sum(diff * diff, axis=0, keepdims=True)
        dy = diff * (1.0 / D)
        dg_ref[...] += jnp.sum(dy * xh, axis=0, keepdims=True)
        dxh = dy * g
        dx_ref[...] = rstd * (dxh - xh * jnp.mean(dxh * xh, axis=-1, keepdims=True))

        @pl.when(i == n - 1)
        def _():
            tot = jnp.sum(acc_ref[...], axis=-1, keepdims=True) * (0.5 / D)
            loss_ref[...] = jnp.broadcast_to(tot, loss_ref.shape)

    row = lambda i: (i, 0)
    return pl.pallas_call(
        body, name="loss_head", grid=(n,),
        in_specs=[pl.BlockSpec((tm, D), row), _res((1, D)), pl.BlockSpec((tm, D), row)],
        out_specs=[pl.BlockSpec((tm, D), row), pl.BlockSpec((1, LANES), lambda i: (0, 0)),
                   pl.BlockSpec((1, D), lambda i: (0, 0))],
        out_shape=[_sds((T, D), F32), _sds((1, LANES), F32), _sds((1, D), F32)],
        scratch_shapes=[pltpu.VMEM((1, D), F32)],
        compiler_params=_cparams(("arbitrary",)),
    )(xf, g_fin, target)


def _ffn_bwd(dx2, x1, gt, up, g_ffn, wgt, wut, wd, tm, dep=None):
    T, D = x1.shape
    F = wd.shape[0]

    def body(dx2_ref, x1_ref, gt_ref, up_ref, g_ref, wg_ref, wu_ref, wd_ref, dx1_ref, dgt_ref, dup_ref, dg_ref):
        i = pl.program_id(0)

        @pl.when(i == 0)
        def _():
            dg_ref[...] = jnp.zeros_like(dg_ref)

        dx2 = dx2_ref[...]
        dx2b = dx2.astype(BF16)
        dh2 = jnp.zeros((tm, D), F32)
        chunks = _ffn_chunks(F)
        dact_next = _dot_nt(dx2b, wd_ref[0:chunks[0][1], :])
        for ci, (c0, cw) in enumerate(chunks):
            dact = dact_next
            if ci + 1 < len(chunks):
                n0, nw = chunks[ci + 1]
                dact_next = _dot_nt(dx2b, wd_ref[n0:n0 + nw, :])
            g = gt_ref[:, c0:c0 + cw].astype(F32)
            u = up_ref[:, c0:c0 + cw].astype(F32)
            sg = _sig(g)
            dup = (dact * (g * sg)).astype(BF16)
            dgt = (dact * u * (sg * (1.0 + g * (1.0 - sg)))).astype(BF16)
            dgt_ref[:, c0:c0 + cw] = dgt
            dup_ref[:, c0:c0 + cw] = dup
            dh2 = dh2 + _dot(dgt, wg_ref[c0:c0 + cw, :]) + _dot(dup, wu_ref[c0:c0 + cw, :])
        x = x1_ref[...]
        rstd = lax.rsqrt(jnp.mean(x * x, axis=-1, keepdims=True) + EPS)
        xh = x * rstd
        dg_ref[...] += jnp.sum(dh2 * xh, axis=0, keepdims=True)
        dxh = dh2 * g_ref[...]
        dx1_ref[...] = dx2 + rstd * (dxh - xh * jnp.mean(dxh * xh, axis=-1, keepdims=True))

    row = lambda i: (i, 0)
    body, dep_spec, dep_arg = _after(body, 8, dep)
    return pl.pallas_call(
        body, name="ffn_bwd", grid=(T // tm,),
        in_specs=[pl.BlockSpec((tm, D), row), pl.BlockSpec((tm, D), row), pl.BlockSpec((tm, F), row),
                  pl.BlockSpec((tm, F), row), _res((1, D)), _res((F, D)), _res((F, D)), _res((F, D))] + dep_spec,
        out_specs=[pl.BlockSpec((tm, D), row), pl.BlockSpec((tm, F), row), pl.BlockSpec((tm, F), row),
                   pl.BlockSpec((1, D), lambda i: (0, 0))],
        out_shape=[_sds((T, D), F32), _sds((T, F), BF16), _sds((T, F), BF16), _sds((1, D), F32)],
        compiler_params=_cparams(("arbitrary",)),
    )(dx2, x1, gt, up, g_ffn, wgt, wut, wd, *dep_arg)


def _merge_sgu_bwd(dx1, proj, ya, yb, mixed, ln_g, ln_b, ws_b, wst_b, gate_bias, w_sq_g, layer, tm, dep=None):
    T, D = dx1.shape
    DQ = D // NSHARD
    G, CH, _ = ws_b.shape
    GD = D // G

    def body(dx1_ref, puv_ref, pg_ref, ya_ref, yb_ref, mixed_ref, g_ref, be_ref, ws_ref, wst_ref, gb_ref,
             wso_ref, wo_ref, dya_ref, dyb_ref, dp_ref, dgb_ref, dlg_ref, dlb_ref, dbs_ref, dws_ref,
             dvn_scr, dbs_scr):
        i = pl.program_id(0)

        @pl.when(i == 0)
        def _():
            for r in (dgb_ref, dlg_ref, dlb_ref, dws_ref, dbs_scr):
                r[...] = jnp.zeros_like(r)

        dmerged = _dot_nt(dx1_ref[...].astype(BF16), wo_ref[...].reshape(D, D))
        sa = _sig(pg_ref[:, :D].astype(F32) + gb_ref[:, :D])
        sb = _sig(pg_ref[:, D:].astype(F32) + gb_ref[:, D:])
        dya = (dmerged * sa).astype(BF16)
        dyb = (dmerged * sb).astype(BF16)
        dya_ref[...] = dya
        dyb_ref[...] = dyb
        dga = dmerged * ya_ref[...].astype(F32) * (sa * (1.0 - sa))
        dgb = dmerged * yb_ref[...].astype(F32) * (sb * (1.0 - sb))
        dp_ref[:, 4 * D:5 * D] = dga.astype(BF16)
        dp_ref[:, 5 * D:6 * D] = dgb.astype(BF16)
        dgb_ref[:, :D] += jnp.sum(dga, axis=0, keepdims=True)
        dgb_ref[:, D:] += jnp.sum(dgb, axis=0, keepdims=True)

        dgated = _dot_nt(dyb, wso_ref[...].reshape(D, D))
        u = puv_ref[:, :D].astype(F32)
        v = puv_ref[:, D:].astype(F32)
        dp_ref[:, 2 * D:3 * D] = (dgated * mixed_ref[...].astype(F32)).astype(BF16)
        dmixed = dgated * u
        mu = jnp.mean(v, axis=-1, keepdims=True)
        vc = v - mu
        rstd = lax.rsqrt(jnp.mean(vc * vc, axis=-1, keepdims=True) + EPS)
        vh = vc * rstd
        vn = (vh * g_ref[...] + be_ref[...]).astype(BF16)
        dmb = dmixed.astype(BF16)
        nch = tm // CH
        bs_part = dmixed[0:CH, :]
        for ch in range(1, nch):
            bs_part = bs_part + dmixed[ch * CH:(ch + 1) * CH, :]
        dbs_scr[...] += bs_part
        for g in range(G):
            cols = slice(g * GD, (g + 1) * GD)
            dm_g = jnp.concatenate([dmb[ch * CH:(ch + 1) * CH, cols] for ch in range(nch)], axis=1)
            vn_g = jnp.concatenate([vn[ch * CH:(ch + 1) * CH, cols] for ch in range(nch)], axis=1)
            dws_ref[g] += _dot_nt(dm_g, vn_g)
            dvn_g = _dot(wst_ref[g], dm_g)
            for ch in range(nch):
                dvn_scr[ch * CH:(ch + 1) * CH, cols] = dvn_g[:, ch * GD:(ch + 1) * GD]
        dvn = dvn_scr[...]
        dlg_ref[...] += jnp.sum(dvn * vh, axis=0, keepdims=True)
        dlb_ref[...] += jnp.sum(dvn, axis=0, keepdims=True)
        dxh = dvn * g_ref[...]
        dv = rstd * (dxh - jnp.mean(dxh, axis=-1, keepdims=True) - vh * jnp.mean(dxh * vh, axis=-1, keepdims=True))
        dp_ref[:, 3 * D:4 * D] = dv.astype(BF16)

        @pl.when(i == pl.num_programs(0) - 1)
        def _():
            for g in range(G):
                blk = dbs_scr[:, g * GD:(g + 1) * GD]
                if GD != CH:
                    blk = jnp.concatenate([blk, jnp.zeros((CH, CH - GD), F32)], axis=1)
                dbs_ref[:, g * CH:(g + 1) * CH] = jnp.sum(blk.T, axis=0, keepdims=True)

    row = lambda i: (i, 0)
    fixed2 = lambda i: (0, 0)
    body, dep_spec, dep_arg = _after(body, 13, dep)
    return pl.pallas_call(
        body, name="merge_sgu_bwd", grid=(T // tm,),
        in_specs=[pl.BlockSpec((tm, D), row), pl.BlockSpec((tm, 2 * D), lambda i: (i, 1)),
                  pl.BlockSpec((tm, 2 * D), lambda i: (i, 2)), pl.BlockSpec((tm, D), row), pl.BlockSpec((tm, D), row),
                  pl.BlockSpec((tm, D), row), _res((1, D)), _res((1, D)), _res((G, CH, CH)), _res((G, CH, CH)),
                  _res((1, 2 * D)),
                  _res((NSHARD, DQ, D), lambda i: (0, layer * 3 + 1, 0)),
                  _res((NSHARD, DQ, D), lambda i: (0, layer * 3 + 2, 0))] + dep_spec,
        out_specs=[pl.BlockSpec((tm, D), row), pl.BlockSpec((tm, D), row),
                   pl.BlockSpec((tm, 6 * D), row),
                   pl.BlockSpec((1, 2 * D), fixed2), pl.BlockSpec((1, D), fixed2), pl.BlockSpec((1, D), fixed2),
                   pl.BlockSpec((1, G * CH), fixed2), pl.BlockSpec((G, CH, CH), lambda i: (0, 0, 0))],
        out_shape=[_sds((T, D), BF16), _sds((T, D), BF16), _sds((T, 6 * D), BF16),
                   _sds((1, 2 * D), F32), _sds((1, D), F32), _sds((1, D), F32), _sds((1, G * CH), F32),
                   _sds((G, CH, CH), F32)],
        scratch_shapes=[pltpu.VMEM((tm, D), F32), pltpu.VMEM((CH, D), F32)],
        compiler_params=_cparams(("arbitrary",)),
    )(dx1, proj, proj, ya, yb, mixed, ln_g, ln_b, ws_b, wst_b, gate_bias, w_sq_g, w_sq_g, *dep_arg)


def _conv_ln_bwd(dya, c1h, rstd_c, ln_g, ln_b, w_sq_g, layer, tm):
    T, D = dya.shape
    DQ = D // NSHARD

    def body(dya_ref, c1h_ref, rstd_ref, g_ref, be_ref, wco_ref, dc1_ref, dlg_ref, dlb_ref, dcb_ref):
        i = pl.program_id(0)

        @pl.when(i == 0)
        def _():
            for r in (dlg_ref, dlb_ref, dcb_ref):
                r[...] = jnp.zeros_like(r)

        wco = wco_ref[...].reshape(D, D)
        blocks = [slice(r0, r0 + tm // 2) for r0 in (0, tm // 2)]
        dc3_next = _dot_nt(dya_ref[blocks[0], :], wco)
        for bi, rows in enumerate(blocks):
            dc3 = dc3_next
            if bi == 0:
                dc3_next = _dot_nt(dya_ref[blocks[1], :], wco)
            c1h = c1h_ref[rows, :].astype(F32)
            c2 = c1h * g_ref[...] + be_ref[...]
            sg = _sig(c2)
            dc2 = dc3 * (sg * (1.0 + c2 * (1.0 - sg)))
            dlg_ref[...] += jnp.sum(dc2 * c1h, axis=0, keepdims=True)
            dlb_ref[...] += jnp.sum(dc2, axis=0, keepdims=True)
            dxh = dc2 * g_ref[...]
            dc1 = rstd_ref[rows, :] * (dxh - jnp.mean(dxh, axis=-1, keepdims=True)
                                       - c1h * jnp.mean(dxh * c1h, axis=-1, keepdims=True))
            dc1_ref[rows, :] = dc1
            dcb_ref[...] += jnp.sum(dc1, axis=0, keepdims=True)

    row = lambda i: (i, 0)
    fixed2 = lambda i: (0, 0)
    return pl.pallas_call(
        body, name="conv_ln_bwd", grid=(T // tm,),
        in_specs=[pl.BlockSpec((tm, D), row), pl.BlockSpec((tm, D), row), pl.BlockSpec((tm, 1), row),
                  _res((1, D)), _res((1, D)), _res((NSHARD, DQ, D), lambda i: (0, layer * 3 + 0, 0))],
        out_specs=[pl.BlockSpec((tm, D), row), pl.BlockSpec((1, D), fixed2), pl.BlockSpec((1, D), fixed2),
                   pl.BlockSpec((1, D), fixed2)],
        out_shape=[_sds((T, D), F32), _sds((1, D), F32), _sds((1, D), F32), _sds((1, D), F32)],
        compiler_params=_cparams(("arbitrary",)),
    )(dya, c1h, rstd_c, ln_g, ln_b, w_sq_g)


def _conv_bwd(dc1, proj, dp3, conv_wf_s, seq, tm, rb, dep=None):
    T, D = dc1.shape
    NSTR = D // LANES
    KW = 31
    PADK = (KW - 1) // 2
    tps = seq // tm
    prev, nxt = _halo_maps(tm, T)
    n = T // tm

    def body(dc_ref, dcprev_ref, dcnext_ref, pa_ref, pprev_ref, pnext_ref, wf_ref, dp_in_ref,
             dp_ref, dw_ref, pad_ref, dpad_ref, dc0_ref, dwacc_ref):
        del dp_in_ref
        i = pl.program_id(0)
        first = (i % tps) == 0
        last = (i % tps) == tps - 1

        @pl.when(i == 0)
        def _():
            dwacc_ref[...] = jnp.zeros_like(dwacc_ref)

        _fill_c0_pad(pad_ref, pa_ref, pprev_ref, pnext_ref, D, tm, first, last)
        for cs in range(NSTR):
            lo, hi = cs * LANES, (cs + 1) * LANES
            dpad_ref[cs, HALO:HALO + tm, :] = dc_ref[:, lo:hi]
            dpad_ref[cs, 0:HALO, :] = jnp.where(first, 0.0, dcprev_ref[:, lo:hi])
            dpad_ref[cs, HALO + tm:HALO + tm + HALO, :] = jnp.where(last, 0.0, dcnext_ref[:, lo:hi])
        _dwconv(dpad_ref, wf_ref, dc0_ref, NSTR, tm, KW, rb)

        def strip(cs, carry):
            for r0 in range(0, tm, rb):
                d = dpad_ref[cs, HALO + r0:HALO + r0 + rb, :]
                for k in range(KW):
                    r = r0 + HALO - PADK + k
                    prod = d * pad_ref[cs, r:r + rb, :]
                    dwacc_ref[cs, k * 8:(k + 1) * 8, :] += jnp.sum(prod.reshape(rb // 8, 8, LANES), axis=0)
            return carry

        lax.fori_loop(0, NSTR, strip, 0)

        for cs in range(NSTR):
            lo, hi = cs * LANES, (cs + 1) * LANES
            av = pa_ref[:, lo:hi].astype(F32)
            sg = _sig(pa_ref[:, D + lo:D + hi].astype(F32))
            dc0 = dc0_ref[cs]
            dp_ref[:, lo:hi] = (dc0 * sg).astype(BF16)
            dp_ref[:, D + lo:D + hi] = (dc0 * av * (sg * (1.0 - sg))).astype(BF16)

        @pl.when(i == n - 1)
        def _():
            for cs in range(NSTR):
                dw_ref[cs] = jnp.sum(dwacc_ref[cs].reshape(32, 8, LANES), axis=1)

    row = lambda i: (i, 0)
    body, dep_spec, dep_arg = _after(body, 8, dep)
    return pl.pallas_call(
        body, name="conv_bwd", grid=(n,),
        in_specs=[pl.BlockSpec((tm, D), row), pl.BlockSpec((HALO, D), prev), pl.BlockSpec((HALO, D), nxt),
                  pl.BlockSpec((tm, 2 * D), row), pl.BlockSpec((HALO, 2 * D), prev), pl.BlockSpec((HALO, 2 * D), nxt),
                  _res((NSTR, 32, LANES)), pl.BlockSpec(memory_space=pl.ANY)] + dep_spec,
        out_specs=[pl.BlockSpec((tm, 2 * D), row),
                   pl.BlockSpec((NSTR, 32, LANES), lambda i: (0, 0, 0))],
        out_shape=[_sds(dp3.shape, BF16), _sds((NSTR, 32, LANES), F32)],
        scratch_shapes=[pltpu.VMEM((NSTR, tm + 2 * HALO, LANES), F32), pltpu.VMEM((NSTR, tm + 2 * HALO, LANES), F32),
                        pltpu.VMEM((NSTR, tm, LANES), F32), pltpu.VMEM((NSTR, 32 * 8, LANES), F32)],
        input_output_aliases={7: 0},
        compiler_params=_cparams(("arbitrary",)),
    )(dc1, dc1, dc1, proj, proj, proj, conv_wf_s, dp3, *dep_arg)


def _mix_in_bwd(dx1, dp3, x2d, g_mix, w_in_g, layer, tm, dep=None):
    T, D = x2d.shape
    CS = w_in_g.shape[2]
    CN = CS // 3

    def body(dx1_ref, dp_ref, x_ref, g_ref, w_ref, dx_ref, dg_ref):
        i = pl.program_id(0)

        @pl.when(i == 0)
        def _():
            dg_ref[...] = jnp.zeros_like(dg_ref)

        dh = jnp.zeros((tm, D), F32)
        for j in range(12):
            dh = dh + _dot_nt(dp_ref[:, j * CN:(j + 1) * CN], w_ref[j // 3, :, (j % 3) * CN:(j % 3 + 1) * CN])
        x = x_ref[...]
        rstd = lax.rsqrt(jnp.mean(x * x, axis=-1, keepdims=True) + EPS)
        xh = x * rstd
        dg_ref[...] += jnp.sum(dh * xh, axis=0, keepdims=True)
        dxh = dh * g_ref[...]
        dx_ref[...] = dx1_ref[...] + rstd * (dxh - xh * jnp.mean(dxh * xh, axis=-1, keepdims=True))

    row = lambda i: (i, 0)
    body, dep_spec, dep_arg = _after(body, 5, dep)
    return pl.pallas_call(
        body, name="mix_in_bwd", grid=(T // tm,),
        in_specs=[pl.BlockSpec((tm, D), row), pl.BlockSpec((tm, 6 * D), row),
                  pl.BlockSpec((tm, D), row), _res((1, D)), _res((NSHARD, D, CS), lambda i: (0, layer, 0))] + dep_spec,
        out_specs=[pl.BlockSpec((tm, D), row), pl.BlockSpec((1, D), lambda i: (0, 0))],
        out_shape=[_sds((T, D), F32), _sds((1, D), F32)],
        compiler_params=_cparams(("arbitrary",)),
    )(dx1, dp3, x2d, g_mix, w_in_g, *dep_arg)


def _tn_matmul(name, a, a_block, a_map, bs, b_block, b_map, out_shape, out_block, out_map, nj, nt):
    kk = [d for d in a_block if d is not None][-1]
    nn = [d for d in b_block if d is not None][-1]
    nb = len(bs)
    a_list = a if isinstance(a, (list, tuple)) else [a]
    na = len(a_list)

    def body(*refs):
        a_refs, b_refs = refs[:na], refs[na:na + nb]
        o_refs, acc_refs = refs[-2 * nb:-nb], refs[-nb:]
        t = pl.program_id(1)

        @pl.when(t == 0)
        def _():
            for acc_ref in acc_refs:
                acc_ref[...] = jnp.zeros_like(acc_ref)

        a_ts = [a_ref[...].astype(BF16) for a_ref in a_refs]
        for i, (b_ref, acc_ref) in enumerate(zip(b_refs, acc_refs)):
            acc_ref[...] += _dot_tn(a_ts[i % na], b_ref[...].astype(BF16))

        @pl.when(t == nt - 1)
        def _():
            for o_ref, acc_ref in zip(o_refs, acc_refs):
                o_ref[...] = acc_ref[...].astype(o_ref.dtype)

    return pl.pallas_call(
        body, name=name, grid=(nj, nt),
        in_specs=[pl.BlockSpec(a_block, a_map)] * na + [pl.BlockSpec(b_block, b_map)] * nb,
        out_specs=[pl.BlockSpec(out_block, out_map)] * nb, out_shape=[_sds(out_shape, BF16)] * nb,
        scratch_shapes=[pltpu.VMEM((kk, nn), F32)] * nb,
        compiler_params=_cparams(("parallel", "arbitrary")),
    )(*a_list, *bs)


def _place_shard(name, w, pos, dtype, tr, dep=None, layer=None, into=None, row_off=0, out_rows=None):
    R, C = w.shape[-2:]
    out_rows = out_rows or R

    def body(pos_ref, w_ref, *rest):
        del pos_ref
        rest[-1][...] = w_ref[...].astype(dtype)

    in_spec = (pl.BlockSpec((tr, C), lambda r, pos: (r, 0)) if layer is None else
               pl.BlockSpec((None, tr, C), lambda r, pos: (layer, r, 0)))
    extra, extra_args = ([], []) if into is None else ([pl.BlockSpec(memory_space=pl.ANY)], [into])
    body, dep_spec, dep_arg = _after(body, 2 + len(extra), dep)
    grid_spec = pltpu.PrefetchScalarGridSpec(
        num_scalar_prefetch=1, grid=(R // tr,), in_specs=[in_spec] + extra + dep_spec,
        out_specs=pl.BlockSpec((None, tr, C), lambda r, pos: (pos[1], row_off // tr + r, 0)))
    return pl.pallas_call(body, name=name, grid_spec=grid_spec, out_shape=_sds((NSHARD, out_rows, C), dtype),
                          input_output_aliases={} if into is None else {2: 0},
                          compiler_params=_cparams(("parallel",)))(pos, w, *extra_args, *dep_arg)


def _add_halves(name, g, rbuf, pos, tr):
    NS, _, H, C = g.shape

    def body(pos_ref, g_ref, r_ref, o_ref):
        del pos_ref
        o_ref[...] = (g_ref[...].astype(F32) + r_ref[...].astype(F32)).astype(BF16)

    grid_spec = pltpu.PrefetchScalarGridSpec(
        num_scalar_prefetch=1, grid=(NS, H // tr),
        in_specs=[pl.BlockSpec((None, None, tr, C), lambda s, r, pos: (s, pos[0], r, 0)),
                  pl.BlockSpec((None, tr, C), lambda s, r, pos: (s, r, 0))],
        out_specs=pl.BlockSpec((None, tr, C), lambda s, r, pos: (s, r, 0)))
    return pl.pallas_call(body, name=name, grid_spec=grid_spec, out_shape=_sds((NS, H, C), BF16),
                          compiler_params=_cparams(("parallel", "parallel")))(pos, g, rbuf)


def _add_shards(name, p, rbuf, pos, tr, layer, n_layers, prev):
    _, H, C = p.shape

    def body(pos_ref, p_ref, r_ref, *rest):
        del pos_ref
        o_ref = rest[-1]
        acc = p_ref[...].astype(F32)
        for j in range(3):
            acc = acc + r_ref[j].astype(F32)
        o_ref[...] = acc

    in_specs = [pl.BlockSpec((None, tr, C), lambda r, pos: (pos[1], r, 0)),
                pl.BlockSpec((3, tr, C), lambda r, pos: (0, r, 0))]
    args = [pos, p, rbuf]
    aliases = {}
    if prev is not None:
        in_specs.append(pl.BlockSpec(memory_space=pl.ANY))
        args.append(prev)
        aliases = {3: 0}
    grid_spec = pltpu.PrefetchScalarGridSpec(
        num_scalar_prefetch=1, grid=(H // tr,), in_specs=in_specs,
        out_specs=pl.BlockSpec((None, None, tr, C), lambda r, pos: (layer, pos[0], r, 0)))
    return pl.pallas_call(body, name=name, grid_spec=grid_spec, out_shape=_sds((n_layers, 2, H, C), F32),
                          input_output_aliases=aliases, compiler_params=_cparams(("parallel",)))(*args)


def _sum_slots(own, land, me, tr):
    NS8, R, C = land.shape

    def body(me_ref, own_ref, l_ref, o_ref):
        acc = None
        for j in range(NS8):
            term = jnp.where(me_ref[0] == j, own_ref[...], l_ref[j])
            acc = term if acc is None else acc + term
        o_ref[...] = acc

    grid_spec = pltpu.PrefetchScalarGridSpec(
        num_scalar_prefetch=1, grid=(R // tr,),
        in_specs=[pl.BlockSpec((tr, C), lambda i, me: (i, 0)), pl.BlockSpec((NS8, tr, C), lambda i, me: (0, i, 0))],
        out_specs=pl.BlockSpec((tr, C), lambda i, me: (i, 0)))
    return pl.pallas_call(body, name="sum_slots", grid_spec=grid_spec, out_shape=_sds((R, C), F32),
                          compiler_params=_cparams(("parallel",)))(me, own, land)


def _adamw_update(w_ref, g_ref, m_ref, v_ref, d_ref, mo_ref, vo_ref):
    g_ = g_ref[...]
    m_ = ADAM_B1 * m_ref[...] + (1.0 - ADAM_B1) * g_
    v_ = ADAM_B2 * v_ref[...] + (1.0 - ADAM_B2) * (g_ * g_)
    mo_ref[...] = m_
    vo_ref[...] = v_
    m_hat = m_ / (1.0 - ADAM_B1 ** ADAM_STEP)
    v_hat = v_ / (1.0 - ADAM_B2 ** ADAM_STEP)
    d_ref[...] = -ADAM_LR * (m_hat / (jnp.sqrt(v_hat) + ADAM_EPS) + ADAM_WD * w_ref[...])


def _adamw(name, w, g, m, v, tr, emit_g):
    R, C = w.shape
    n_out = 4 if emit_g else 3

    def body(w_ref, g_ref, m_ref, v_ref, d_ref, mo_ref, vo_ref, *go_ref):
        _adamw_update(w_ref, g_ref, m_ref, v_ref, d_ref, mo_ref, vo_ref)
        if emit_g:
            go_ref[0][...] = g_ref[...]

    spec = pl.BlockSpec((tr, C), lambda i: (i, 0))
    return pl.pallas_call(
        body, name=name, grid=(R // tr,), in_specs=[spec] * 4, out_specs=[spec] * n_out,
        out_shape=[_sds((R, C), F32)] * n_out, compiler_params=_cparams(("parallel",)))(w, g, m, v)


def _adamw_many(ws, gs, ms, vs):
    n = len(ws)

    def body(*refs):
        ins, outs = refs[:4 * n], refs[4 * n:]
        for k in range(n):
            _adamw_update(ins[k], ins[n + k], ins[2 * n + k], ins[3 * n + k], outs[k], outs[n + k], outs[2 * n + k])

    vmem = pl.BlockSpec(memory_space=pltpu.VMEM)
    res = pl.pallas_call(
        body, name="adamw_small", in_specs=[vmem] * (4 * n), out_specs=[vmem] * (3 * n),
        out_shape=[_sds(w.shape, F32) for w in ws] * 3,
        compiler_params=pltpu.CompilerParams(vmem_limit_bytes=VMEM_LIMIT))(*ws, *gs, *ms, *vs)
    return list(res[:n]), list(res[n:2 * n]), list(res[2 * n:])


def _row_tile(rows, cap):
    best = rows
    for t in range(8, min(rows, cap) + 1, 8):
        if rows % t == 0:
            best = t
    return best


HBM_SPEC = pl.BlockSpec(memory_space=pltpu.HBM)
SEM_SPEC = pl.BlockSpec(memory_space=pltpu.SEMAPHORE)
DATAFLOW = pltpu.SideEffectType.DATAFLOW_SIDE_EFFECTING
DMA_SEM = pltpu.SemaphoreType.DMA


def _hbm(a):
    return pltpu.with_memory_space_constraint(a, pltpu.HBM)


def _mesh_pos():
    return lax.axis_index("x"), lax.axis_index("y"), lax.axis_index("c")


def _other_chips(x, y):
    return [(1 - x, y), (x, 1 - y), (1 - x, 1 - y)]


def _half_rows(buf, shard, core):
    h = buf.shape[1] // 2
    return buf.at[shard, pl.ds(core * h, h), :]


def _ici_copy(buf, j, send, recv, landing):
    x, y, c = _mesh_pos()
    px, py = _other_chips(x, y)[j]
    part = _half_rows(buf, 2 * px + py if landing else 2 * x + y, c)
    return pltpu.make_async_remote_copy(src_ref=part, dst_ref=part, send_sem=send, recv_sem=recv,
                                        device_id=(px, py, c), device_id_type=MESH_ID)


def _sibling_copy(buf, j, send, recv, landing):
    x, y, c = _mesh_pos()
    px, py = _other_chips(x, y)[j]
    part = _half_rows(buf, 2 * px + py, 1 - c if landing else c)
    return pltpu.make_async_remote_copy(src_ref=part, dst_ref=part, send_sem=send, recv_sem=recv,
                                        device_id=(x, y, 1 - c), device_id_type=MESH_ID)


def _forward_sibling(name, bufs, with_ici):
    n = len(bufs)

    def body(*refs):
        ins = refs[:n]
        send_ici, recv_ici, send_d2d, recv_d2d = refs[2 * n:]
        sends = []
        if with_ici:
            for i in range(n):
                for j in range(3):
                    cp = _ici_copy(ins[i], j, send_ici.at[i, j], recv_ici.at[i, j], False)
                    cp.start()
                    sends.append(cp)
        for i in range(n):
            for j in range(3):
                if with_ici:
                    _ici_copy(ins[i], j, send_ici.at[i, j], recv_ici.at[i, j], True).wait_recv()
                cp = _sibling_copy(ins[i], j, send_d2d.at[i, j], recv_d2d.at[i, j], False)
                cp.start()
                sends.append(cp)
        for i in range(n):
            for j in range(3):
                _sibling_copy(ins[i], j, send_d2d.at[i, j], recv_d2d.at[i, j], True).wait_recv()
        for cp in sends:
            cp.wait_send()

    return pl.pallas_call(
        body, name=name, in_specs=[HBM_SPEC] * n, out_specs=[HBM_SPEC] * n,
        out_shape=[_sds(b.shape, b.dtype) for b in bufs],
        scratch_shapes=[DMA_SEM((n, 3))] * 4, input_output_aliases={i: i for i in range(n)},
    )(*bufs)


def _gather_start(name, groups):
    flat = [b for g in groups for b in g]
    n, ng = len(flat), len(groups)

    def body(*refs):
        ins, sems, token = refs[:n], refs[n:n + 2 * ng], refs[-1]
        k = 0
        for gi, g in enumerate(groups):
            for a in range(len(g)):
                for j in range(3):
                    _ici_copy(ins[k], j, sems[2 * gi], sems[2 * gi + 1], False).start()
                k += 1
        token[...] = jnp.zeros_like(token)

    res = pl.pallas_call(
        body, name=name, in_specs=[HBM_SPEC] * n,
        out_specs=[SEM_SPEC] * (2 * ng) + [HBM_SPEC] * n + [pl.BlockSpec(memory_space=pltpu.VMEM)],
        out_shape=[DMA_SEM(()) for g in groups for _ in range(2)]
        + [pltpu.HBM(b.shape, b.dtype) for b in flat] + [_sds((8, LANES), F32)],
        input_output_aliases={i: 2 * ng + i for i in range(n)},
        compiler_params=pltpu.CompilerParams(has_side_effects=DATAFLOW),
    )(*[_hbm(b) for b in flat])
    sems = [(res[2 * gi], res[2 * gi + 1]) for gi in range(ng)]
    thru, k = [], 2 * ng
    for g in groups:
        thru.append(list(res[k:k + len(g)]))
        k += len(g)
    return sems, thru, res[-1]


def _gather_wait(name, bufs, sems, after):
    n = len(bufs)

    def body(*refs):
        ins, send, recv = refs[:n], refs[n], refs[n + 1]
        for a in range(n):
            for j in range(3):
                _ici_copy(ins[a], j, send, recv, False).wait_send()
                _ici_copy(ins[a], j, send, recv, True).wait_recv()

    return pl.pallas_call(
        body, name=name, in_specs=[HBM_SPEC] * n + [SEM_SPEC, SEM_SPEC, pl.BlockSpec(memory_space=pl.ANY)],
        out_specs=[HBM_SPEC] * n, out_shape=[pltpu.HBM(b.shape, b.dtype) for b in bufs],
        input_output_aliases={i: i for i in range(n)},
        compiler_params=pltpu.CompilerParams(has_side_effects=DATAFLOW),
    )(*bufs, sems[0], sems[1], after)


def _send_sibling_halves(name, arrs):
    n = len(arrs)

    def body(*refs):
        ins, outs = refs[:n], refs[n:2 * n]
        send, recv = refs[2 * n:]
        x, y, c = _mesh_pos()
        cps = []
        for i in range(n):
            cp = pltpu.make_async_remote_copy(
                src_ref=ins[i].at[:, 1 - c], dst_ref=outs[i],
                send_sem=send.at[i], recv_sem=recv.at[i], device_id=(x, y, 1 - c), device_id_type=MESH_ID)
            cp.start()
            cps.append(cp)
        for cp in cps:
            cp.wait()

    return pl.pallas_call(
        body, name=name, in_specs=[HBM_SPEC] * n, out_specs=[HBM_SPEC] * n,
        out_shape=[_sds((a.shape[0],) + a.shape[2:], a.dtype) for a in arrs],
        scratch_shapes=[DMA_SEM((n,)), DMA_SEM((n,))],
    )(*arrs)


def _chip_copy(p, land, j, send, recv):
    x, y, c = _mesh_pos()
    px, py = _other_chips(x, y)[j]
    return pltpu.make_async_remote_copy(src_ref=p.at[2 * px + py], dst_ref=land.at[j], send_sem=send, recv_sem=recv,
                                        device_id=(px, py, c), device_id_type=MESH_ID)


def _chip_send_start(name, ps):
    n = len(ps)
    lands = [lax.empty((3,) + p.shape[1:], p.dtype) for p in ps]

    def body(*refs):
        ins, lnd, send, recv, token = refs[:n], refs[n:2 * n], refs[2 * n], refs[2 * n + 1], refs[-1]
        for i in range(n):
            for j in range(3):
                _chip_copy(ins[i], lnd[i], j, send, recv).start()
        token[...] = jnp.zeros_like(token)

    res = pl.pallas_call(
        body, name=name, in_specs=[HBM_SPEC] * (2 * n),
        out_specs=[SEM_SPEC, SEM_SPEC] + [HBM_SPEC] * (2 * n) + [pl.BlockSpec(memory_space=pltpu.VMEM)],
        out_shape=[DMA_SEM(()), DMA_SEM(())] + [pltpu.HBM(a.shape, a.dtype) for a in ps + lands]
        + [_sds((8, LANES), F32)],
        input_output_aliases={i: 2 + i for i in range(2 * n)},
        compiler_params=pltpu.CompilerParams(has_side_effects=DATAFLOW),
    )(*[_hbm(a) for a in ps + lands])
    return (res[0], res[1]), list(res[2:2 + n]), list(res[2 + n:2 + 2 * n]), res[-1]


def _chip_send_wait(name, ps, lands, sems, after):
    n = len(ps)

    def body(*refs):
        ins, lnd, send, recv = refs[:n], refs[n:2 * n], refs[2 * n], refs[2 * n + 1]
        for i in range(n):
            for j in range(3):
                cp = _chip_copy(ins[i], lnd[i], j, send, recv)
                cp.wait_send()
                cp.wait_recv()

    res = pl.pallas_call(
        body, name=name, in_specs=[HBM_SPEC] * (2 * n) + [SEM_SPEC, SEM_SPEC, pl.BlockSpec(memory_space=pl.ANY)],
        out_specs=[HBM_SPEC] * (2 * n), out_shape=[pltpu.HBM(a.shape, a.dtype) for a in ps + lands],
        input_output_aliases={i: i for i in range(2 * n)},
        compiler_params=pltpu.CompilerParams(has_side_effects=DATAFLOW),
    )(*ps, *lands, sems[0], sems[1], after)
    return list(res[:n]), list(res[n:])


def _join_halves(arrs):
    n = len(arrs)

    def body(*refs):
        bufs = refs[n:2 * n]
        send, recv = refs[2 * n:]
        x, y, c = _mesh_pos()
        cps = []
        for i in range(n):
            mine = bufs[i].at[:, c]
            cp = pltpu.make_async_remote_copy(
                src_ref=mine, dst_ref=mine, send_sem=send.at[i], recv_sem=recv.at[i],
                device_id=(x, y, 1 - c), device_id_type=MESH_ID)
            cp.start()
            cps.append(cp)
        for i, cp in enumerate(cps):
            theirs = bufs[i].at[:, 1 - c]
            cp.wait_send()
            pltpu.make_async_remote_copy(
                src_ref=theirs, dst_ref=theirs, send_sem=send.at[i], recv_sem=recv.at[i],
                device_id=(x, y, 1 - c), device_id_type=MESH_ID).wait_recv()

    return pl.pallas_call(
        body, name="join_halves", in_specs=[HBM_SPEC] * n, out_specs=[HBM_SPEC] * n,
        out_shape=[_sds(a.shape, a.dtype) for a in arrs],
        scratch_shapes=[DMA_SEM((n,)), DMA_SEM((n,))], input_output_aliases={i: i for i in range(n)},
    )(*arrs)


def _peer_copy(buf, land, k, send, recv, landing):
    x, y, c = _mesh_pos()
    px, py, pc = x ^ ((k >> 2) & 1), y ^ ((k >> 1) & 1), c ^ (k & 1)
    slot = 4 * px + 2 * py + pc if landing else 4 * x + 2 * y + c
    return pltpu.make_async_remote_copy(src_ref=buf, dst_ref=land.at[slot], send_sem=send, recv_sem=recv,
                                        device_id=(px, py, pc), device_id_type=MESH_ID)


def _exchange_all(buf):
    def body(in_ref, out_ref, send, recv):
        cps = [_peer_copy(in_ref, out_ref, k, send.at[k - 1], recv.at[k - 1], False) for k in range(1, 8)]
        for cp in cps:
            cp.start()
        for k in range(1, 8):
            cps[k - 1].wait_send()
            _peer_copy(in_ref, out_ref, k, send.at[k - 1], recv.at[k - 1], True).wait_recv()

    return pl.pallas_call(
        body, name="exchange_all", in_specs=[HBM_SPEC], out_specs=HBM_SPEC,
        out_shape=_sds((8,) + buf.shape, buf.dtype), scratch_shapes=[DMA_SEM((7,)), DMA_SEM((7,))],
    )(buf)


def _exchange_start(name, buf):
    land = lax.empty((8,) + buf.shape, buf.dtype)

    def body(in_ref, land_ref, send, recv, in_thru, land_thru, token):
        for k in range(1, 8):
            _peer_copy(in_ref, land_ref, k, send, recv, False).start()
        token[...] = jnp.zeros_like(token)

    res = pl.pallas_call(
        body, name=name, in_specs=[HBM_SPEC] * 2,
        out_specs=[SEM_SPEC, SEM_SPEC, HBM_SPEC, HBM_SPEC, pl.BlockSpec(memory_space=pltpu.VMEM)],
        out_shape=[DMA_SEM(()), DMA_SEM(()), pltpu.HBM(buf.shape, buf.dtype), pltpu.HBM(land.shape, land.dtype),
                   _sds((8, LANES), F32)],
        input_output_aliases={0: 2, 1: 3}, compiler_params=pltpu.CompilerParams(has_side_effects=DATAFLOW),
    )(_hbm(buf), _hbm(land))
    return (res[0], res[1]), res[2], res[3], res[4]


def _exchange_wait(name, buf, land, sems, after):
    def body(in_ref, land_ref, send, recv, after_ref, in_thru, land_thru):
        for k in range(1, 8):
            _peer_copy(in_ref, land_ref, k, send, recv, False).wait_send()
            _peer_copy(in_ref, land_ref, k, send, recv, True).wait_recv()

    res = pl.pallas_call(
        body, name=name, in_specs=[HBM_SPEC, HBM_SPEC, SEM_SPEC, SEM_SPEC, pl.BlockSpec(memory_space=pl.ANY)],
        out_specs=[HBM_SPEC, HBM_SPEC], out_shape=[pltpu.HBM(buf.shape, buf.dtype), pltpu.HBM(land.shape, land.dtype)],
        input_output_aliases={0: 0, 1: 1}, compiler_params=pltpu.CompilerParams(has_side_effects=DATAFLOW),
    )(buf, land, sems[0], sems[1], after)
    return res[0], res[1]


def _pad_to(a, axis, size):
    pad = [(0, 0)] * a.ndim
    pad[axis] = (0, size - a.shape[axis])
    return jnp.pad(a, pad)


def _strips(w):
    k, d = w.shape
    return _pad_to(w, 0, 32).reshape(32, d // LANES, LANES).transpose(1, 0, 2)


def kernel(x, norm_mix, w_in, gate_bias, conv_w, conv_b, conv_ln_g, conv_ln_b, w_conv_out, sgu_ln_g, sgu_ln_b, w_spatial, b_spatial, w_sgu_out, w_o, norm_ffn, w_ffn_gate, w_ffn_up, w_ffn_down, norm_final, loss_target, m_norm_mix, m_w_in, m_gate_bias, m_conv_w, m_conv_b, m_conv_ln_g, m_conv_ln_b, m_w_conv_out, m_sgu_ln_g, m_sgu_ln_b, m_w_spatial, m_b_spatial, m_w_sgu_out, m_w_o, m_norm_ffn, m_w_ffn_gate, m_w_ffn_up, m_w_ffn_down, m_norm_final, v_norm_mix, v_w_in, v_gate_bias, v_conv_w, v_conv_b, v_conv_ln_g, v_conv_ln_b, v_w_conv_out, v_sgu_ln_g, v_sgu_ln_b, v_w_spatial, v_b_spatial, v_w_sgu_out, v_w_o, v_norm_ffn, v_w_ffn_gate, v_w_ffn_up, v_w_ffn_down, v_norm_final):
    BL, S, D = x.shape
    T = BL * S
    L = w_in.shape[0]
    CS = w_in.shape[2]
    CN = CS // 3
    DQ = D // NSHARD
    FS = w_ffn_gate.shape[2]
    F = NSHARD * FS
    G, CH = w_spatial.shape[1], w_spatial.shape[2]
    KW = conv_w.shape[1]
    CQ = conv_w.shape[3]
    NSTR = D // LANES
    tm = min(512, S // 2)
    tm2 = max(tm // 2, CH)
    rb = min(64, tm)
    mx, my, mc = _mesh_pos()
    pos = jnp.stack([mc, 2 * mx + my]).astype(jnp.int32)

    def placed(name, w, dtype=BF16, dep=None, layer=None, **kw):
        return _place_shard("place_" + name, w, pos, dtype, _row_tile(w.shape[-2], 256), dep, layer, **kw)

    w_in0 = placed("w_in", w_in, layer=0)
    cw_p = placed("conv_w", _pad_to(conv_w.reshape(L, KW, CQ), 1, 32).reshape(L * 32, CQ), F32)
    fsems, fflying, ftoken = _gather_start("gather_start_first", [[w_in0, cw_p]])
    wts = []
    for l in range(L):
        w_sq = None
        for i, w in enumerate([w_conv_out, w_sgu_out, w_o]):
            w_sq = placed("w_sq", w, dep=ftoken, layer=l, into=w_sq, row_off=i * DQ, out_rows=3 * DQ)
        wts.append(dict(w_in=placed("w_in", w_in, dep=ftoken, layer=l) if l else None, w_sq=w_sq,
                        wg=placed("w_gate", w_ffn_gate[l].T, dep=ftoken), wu=placed("w_up", w_ffn_up[l].T, dep=ftoken),
                        wd=placed("w_down", w_ffn_down, dep=ftoken, layer=l)))
    ffn_keys = ["wg", "wu", "wd"]
    order = [[(0, "w_sq")], [(0, k) for k in ffn_keys]]
    order += [[(l, k) for k in ["w_in", "w_sq"] + ffn_keys] for l in range(1, L)]
    gsems, flying, token = _gather_start("gather_start", [[wts[l][k] for l, k in grp] for grp in order])
    first = _gather_wait("gather_wait_first", fflying[0], fsems[0], token)
    wts[0]["w_in"], cw_g = _forward_sibling("gather_first", first, False)
    conv_w_full = cw_g.reshape(NSHARD, L, 32, CQ).transpose(1, 2, 0, 3).reshape(L, 32, D)[:, :KW]

    def land(gi, after):
        bufs = _gather_wait("gather_wait_%d" % gi, flying[gi], gsems[gi], after)
        bufs = _forward_sibling("gather_forward_%d" % gi, bufs, False)
        for (l, k), b in zip(order[gi], bufs):
            wts[l][k] = b

    x2d = x.reshape(T, D)
    tgt = loss_target.reshape(T, D)
    row = lambda a, l: a[l].reshape(1, -1)

    saved = []
    xc = x2d
    for l in range(L):
        ws_b = w_spatial[l].astype(BF16)
        bs_b = jnp.repeat(b_spatial[l].T, D // G, axis=1)
        cw_s = _strips(conv_w_full[l])
        h, proj = _mix_in_fwd(xc, row(norm_mix, l), wts[l]["w_in"], 0, tm, token if l == 0 else None)
        if l == 0:
            land(0, h)
        c1h, rstd_c, c3, ya = _conv_fwd(proj, cw_s, row(conv_b, l), row(conv_ln_g, l), row(conv_ln_b, l),
                                        wts[l]["w_sq"], 0, S, tm, rb)
        if l == 0:
            land(1, ya)
        ffn_w = [wts[l][k].reshape(F, D) for k in ffn_keys]
        mixed, gated, yb, merged, x1, h2, gt, up, act, x2 = _sgu_ffn_fwd(
            proj, ya, xc, row(sgu_ln_g, l), row(sgu_ln_b, l), ws_b, bs_b, row(gate_bias, l), wts[l]["w_sq"], 0,
            row(norm_ffn, l), *ffn_w, tm2)
        if l + 1 < L:
            land(l + 2, x2)
        saved.append(dict(x=xc, h=h, proj=proj, c1h=c1h, rstd_c=rstd_c, c3=c3, ya=ya, mixed=mixed, gated=gated,
                          yb=yb, merged=merged, x1=x1, h2=h2, gt=gt, up=up, act=act, ws_b=ws_b, cw=conv_w_full[l]))
        xc = x2

    dx, loss_part, d_norm_final = _loss_head(xc, norm_final.reshape(1, D), tgt, tm)
    loss = lax.psum(loss_part[0, 0], ("x", "y", "c"))

    g_acc = {}

    def reduce_start(tag, layer, named):
        arrs = [g.reshape(NSHARD, 2, g.shape[1] // 2, g.shape[2]) for _, g in named]
        from_sib = _send_sibling_halves("sibling_" + tag, arrs)
        ps = [_add_halves("presum_" + nm, a, r, pos, _row_tile(a.shape[2], 256))
              for (nm, _), a, r in zip(named, arrs, from_sib)]
        sems, ps, lands, tok = _chip_send_start("chip_send_start_" + tag, ps)
        return dict(tag=tag, layer=layer, names=[nm for nm, _ in named], ps=ps, lands=lands, sems=sems), tok

    def reduce_finish(pend, after):
        ps, lands = _chip_send_wait("chip_send_wait_" + pend["tag"], pend["ps"], pend["lands"], pend["sems"], after)
        for nm, p, r in zip(pend["names"], ps, lands):
            g_acc[nm] = _add_shards("shardsum_" + nm, p, r, pos, _row_tile(p.shape[1], 256), pend["layer"], L,
                                    g_acc.get(nm))

    me_idx = (4 * mx + 2 * my + mc).astype(jnp.int32).reshape(1)
    exchanges = []

    def pack_rows(pieces):
        packed = jnp.concatenate(pieces, axis=0)
        return _pad_to(packed, 0, -(-packed.shape[0] // 8) * 8)

    def unpack_rows(summed, pieces):
        out, off = [], 0
        for p in pieces:
            out.append(summed[off:off + p.shape[0]])
            off += p.shape[0]
        return out

    def small_start(tag, pieces):
        sems, buf, land, token = _exchange_start("exchange_start_" + tag, pack_rows(pieces))
        return dict(tag=tag, pieces=pieces, buf=buf, land=land, sems=sems, token=token)

    def small_finish(st, after):
        buf, land = _exchange_wait("exchange_wait_" + st["tag"], st["buf"], st["land"], st["sems"], after)
        return unpack_rows(_sum_slots(buf, land, me_idx, _row_tile(buf.shape[0], 256)), st["pieces"])

    small = [None] * L
    tt = min(2048, T // 2)
    nt = T // tt
    pending, tok = None, None
    for l in reversed(range(L)):
        sv, wt = saved[l], wts[l]
        ffn_w = [wt[k].reshape(F, D) for k in ffn_keys]
        dx1, dgt, dup, d_norm_ffn = _ffn_bwd(dx, sv["x1"], sv["gt"], sv["up"], row(norm_ffn, l), *ffn_w, tm2, tok)
        tn_a = ((tt, F // 2), lambda j, t: (t, j))
        tn_b = ((tt, D), lambda j, t: (t, 0))
        tn_o = ((F, D), (F // 2, D), lambda j, t: (j, 0), 2, nt)
        g_g, = _tn_matmul("grad_w_gate", dgt, *tn_a, [sv["h2"]], *tn_b, *tn_o)
        g_u, = _tn_matmul("grad_w_up", dup, *tn_a, [sv["h2"]], *tn_b, *tn_o)
        g_d, = _tn_matmul("grad_w_down", sv["act"], *tn_a, [dx], *tn_b, *tn_o)
        if pending is not None:
            reduce_finish(pending, g_d)
        ffn_pend, tok = reduce_start("ffn%d" % l, l, [
            ("w_ffn_gate", g_g.reshape(NSHARD, FS, D)), ("w_ffn_up", g_u.reshape(NSHARD, FS, D)),
            ("w_ffn_down", g_d.reshape(NSHARD, FS, D))])
        wst_b = jnp.swapaxes(sv["ws_b"], 1, 2)
        dya, dyb, dp3, d_gate_bias, d_sgu_g, d_sgu_b, d_bs, d_ws = _merge_sgu_bwd(
            dx1, sv["proj"], sv["ya"], sv["yb"], sv["mixed"], row(sgu_ln_g, l), row(sgu_ln_b, l), sv["ws_b"], wst_b,
            row(gate_bias, l), wt["w_sq"], 0, tm2, tok)
        sq_args = ((tt, D), lambda j, t: (t, 0))
        sq_out = ((D, D), (D, D), lambda j, t: (0, 0), 1, nt)
        g_o, = _tn_matmul("grad_w_o", sv["merged"], *sq_args, [dx1], *sq_args, *sq_out)
        g_so, = _tn_matmul("grad_w_sgu_out", sv["gated"], *sq_args, [dyb], *sq_args, *sq_out)
        g_co, = _tn_matmul("grad_w_conv_out", sv["c3"], *sq_args, [dya], *sq_args, *sq_out)
        dc1, d_cln_g, d_cln_b, d_conv_b = _conv_ln_bwd(dya, sv["c1h"], sv["rstd_c"], row(conv_ln_g, l),
                                                       row(conv_ln_b, l), wt["w_sq"], 0, tm)
        small[l] = [None, d_gate_bias.reshape(2, D), None, d_conv_b, d_cln_g, d_cln_b, d_sgu_g, d_sgu_b,
                    d_ws.reshape(G * CH * CH // D, D), d_bs.reshape(G * CH // D, D), d_norm_ffn]
        tok_x = None
        if l == 0:
            early = [k for k in range(len(small[0])) if small[0][k] is not None]
            exchanges.append((small_start("early0", [small[0][k] for k in early]), [(0, k) for k in early]))
            tok_x = exchanges[-1][0]["token"]
        dp3, d_cw_s = _conv_bwd(dc1, sv["proj"], dp3, _strips(sv["cw"][::-1]), S, tm, rb, tok_x)
        g_in, = _tn_matmul("grad_w_in", sv["h"], (tt, D), lambda j, t: (t, 0), [dp3], (tt, CS), lambda j, t: (t, j),
                           (NSHARD, D, CS), (None, D, CS), lambda j, t: (j, 0, 0), NSHARD, nt)
        reduce_finish(ffn_pend, g_in)
        pending, tok = reduce_start("mix%d" % l, l, [
            ("w_in", g_in), ("w_conv_out", g_co.reshape(NSHARD, DQ, D)), ("w_sgu_out", g_so.reshape(NSHARD, DQ, D)),
            ("w_o", g_o.reshape(NSHARD, DQ, D))])
        dx, d_norm_mix = _mix_in_bwd(dx1, dp3, sv["x"], row(norm_mix, l), wt["w_in"], 0, tm, tok)
        small[l][0] = d_norm_mix
        small[l][2] = d_cw_s.transpose(1, 0, 2).reshape(32, D)
        if l > 0:
            exchanges.append((small_start("layer%d" % l, small[l]), [(l, k) for k in range(len(small[l]))]))
            tok = [tok, exchanges[-1][0]["token"]]
    reduce_finish(pending, dx)
    grad_x = dx.reshape(BL, S, D)

    names = ["w_in", "w_conv_out", "w_sgu_out", "w_o", "w_ffn_gate", "w_ffn_up", "w_ffn_down"]
    g_full = _join_halves([g_acc[nm] for nm in names])
    g_w_in, g_w_co, g_w_so, g_w_o, g_w_g, g_w_u, g_w_d = [g.reshape(L, 2 * g.shape[2], g.shape[3]) for g in g_full]
    g_w_g = jnp.swapaxes(g_w_g, 1, 2)
    g_w_u = jnp.swapaxes(g_w_u, 1, 2)

    late = [small[0][0], small[0][2], d_norm_final]
    packed = pack_rows(late)
    summed = _sum_slots(packed, _exchange_all(packed), me_idx, _row_tile(packed.shape[0], 256))
    sg = [[None] * len(small[l]) for l in range(L)]
    sg[0][0], sg[0][2], g_norm_final = unpack_rows(summed, late)
    g_norm_final = g_norm_final[0]
    for st, where in exchanges:
        for (l, k), piece in zip(where, small_finish(st, summed)):
            sg[l][k] = piece

    def per_layer(k, shape):
        return jnp.stack([sg[l][k] for l in range(L)]).reshape(shape)

    g_norm_mix = per_layer(0, (L, D))
    g_gate_bias = per_layer(1, (L, 2 * D))
    g_conv_w_full = jnp.stack([sg[l][2][:KW] for l in range(L)])
    g_conv_w = lax.dynamic_slice_in_dim(g_conv_w_full, (2 * mx + my) * CQ, CQ, axis=2).reshape(L, KW, 1, CQ)
    g_conv_b = per_layer(3, (L, D))
    g_conv_ln_g = per_layer(4, (L, D))
    g_conv_ln_b = per_layer(5, (L, D))
    g_sgu_ln_g = per_layer(6, (L, D))
    g_sgu_ln_b = per_layer(7, (L, D))
    g_w_spatial = per_layer(8, (L, G, CH, CH))
    g_b_spatial = per_layer(9, (L, G, CH))
    g_norm_ffn = per_layer(10, (L, D))

    grads = [g_norm_mix, g_w_in, g_gate_bias, g_conv_w, g_conv_b, g_conv_ln_g, g_conv_ln_b, g_w_co, g_sgu_ln_g,
             g_sgu_ln_b, g_w_spatial, g_b_spatial, g_w_so, g_w_o, g_norm_ffn, g_w_g, g_w_u, g_w_d, g_norm_final]
    weights = [norm_mix, w_in, gate_bias, conv_w, conv_b, conv_ln_g, conv_ln_b, w_conv_out, sgu_ln_g, sgu_ln_b,
               w_spatial, b_spatial, w_sgu_out, w_o, norm_ffn, w_ffn_gate, w_ffn_up, w_ffn_down, norm_final]
    ms = [m_norm_mix, m_w_in, m_gate_bias, m_conv_w, m_conv_b, m_conv_ln_g, m_conv_ln_b, m_w_conv_out, m_sgu_ln_g,
          m_sgu_ln_b, m_w_spatial, m_b_spatial, m_w_sgu_out, m_w_o, m_norm_ffn, m_w_ffn_gate, m_w_ffn_up,
          m_w_ffn_down, m_norm_final]
    vs = [v_norm_mix, v_w_in, v_gate_bias, v_conv_w, v_conv_b, v_conv_ln_g, v_conv_ln_b, v_w_conv_out, v_sgu_ln_g,
          v_sgu_ln_b, v_w_spatial, v_b_spatial, v_w_sgu_out, v_w_o, v_norm_ffn, v_w_ffn_gate, v_w_ffn_up,
          v_w_ffn_down, v_norm_final]

    big_idx = [1, 7, 12, 13, 15, 16, 17]
    transposed = [15, 16]
    deltas, new_m, new_v = [None] * 19, [None] * 19, [None] * 19
    for k in big_idx:
        shp = weights[k].shape
        r2 = (shp[0] * shp[1], shp[2])
        res = _adamw("adamw_" + str(k), weights[k].reshape(r2), grads[k].reshape(r2), ms[k].reshape(r2),
                     vs[k].reshape(r2), _row_tile(r2[0], 256), k not in transposed)
        deltas[k], new_m[k], new_v[k] = [a.reshape(shp) for a in res[:3]]
        if k not in transposed:
            grads[k] = res[3].reshape(shp)
    small_idx = [k for k in range(19) if k not in big_idx]
    pick = lambda arrs: [arrs[k].reshape(1, -1) if arrs[k].ndim == 1 else arrs[k] for k in small_idx]
    d_, m_, v_ = _adamw_many(pick(weights), pick(grads), pick(ms), pick(vs))
    for i, k in enumerate(small_idx):
        shp = weights[k].shape
        deltas[k], new_m[k], new_v[k] = d_[i].reshape(shp), m_[i].reshape(shp), v_[i].reshape(shp)

    return (loss, grad_x, *grads, *deltas, *new_m, *new_v)
```

```python
import functools

import jax
import jax.numpy as jnp
from jax import lax
from jax.experimental import pallas as pl
from jax.experimental.pallas import tpu as pltpu

F32 = jnp.float32
BF16 = jnp.bfloat16
EPS = 1e-6
ADAM_LR = 0.001
ADAM_B1 = 0.9
ADAM_B2 = 0.999
ADAM_EPS = 1e-08
ADAM_WD = 0.01
ADAM_STEP = 10

NSHARD = 4
LANES = 128
HALO = 16
VMEM_LIMIT = 60 * 1024 * 1024
MESH_ID = pl.DeviceIdType.MESH


def _dot(a, b):
    return jnp.dot(a, b, preferred_element_type=F32)


def _dot_nt(a, b):
    return lax.dot_general(a, b, (((1,), (1,)), ((), ())), preferred_element_type=F32)


def _dot_tn(a, b):
    return lax.dot_general(a, b, (((0,), (0,)), ((), ())), preferred_element_type=F32)


def _sig(z):
    return 1.0 / (1.0 + jnp.exp(-z))


def _res(shape, imap=None):
    nd = len(shape)
    if imap is None:
        imap = lambda *_: (0,) * nd
    return pl.BlockSpec(shape, imap, pipeline_mode=pl.Buffered(1))


def _cparams(sem):
    return pltpu.CompilerParams(dimension_semantics=sem, vmem_limit_bytes=VMEM_LIMIT)


def _sds(shape, dtype):
    return jax.ShapeDtypeStruct(shape, dtype)


def _after(body, n_in, dep):
    deps = [] if dep is None else [d for d in (dep if isinstance(dep, (list, tuple)) else [dep]) if d is not None]
    if not deps:
        return body, [], []

    def wrapped(*refs):
        return body(*refs[:n_in], *refs[n_in + len(deps):])

    return wrapped, [pl.BlockSpec(memory_space=pl.ANY)] * len(deps), deps


def _mix_in_fwd(x2d, g_mix, w_in_g, layer, tm, dep=None):
    T, D = x2d.shape
    CS = w_in_g.shape[2]
    CN = CS // 3

    def body(x_ref, g_ref, w_ref, h_ref, p_ref):
        x = x_ref[...]
        rstd = lax.rsqrt(jnp.mean(x * x, axis=-1, keepdims=True) + EPS)
        h = (x * rstd * g_ref[...]).astype(BF16)
        h_ref[...] = h
        for s in range(NSHARD):
            for j in range(3):
                c0 = s * CS + j * CN
                p_ref[:, c0:c0 + CN] = _dot(h, w_ref[s, :, j * CN:(j + 1) * CN]).astype(BF16)

    body, dep_spec, dep_arg = _after(body, 3, dep)
    return pl.pallas_call(
        body, name="mix_in_fwd", grid=(T // tm,),
        in_specs=[pl.BlockSpec((tm, D), lambda i: (i, 0)), _res((1, D)),
                  _res((NSHARD, D, CS), lambda i: (0, layer, 0))] + dep_spec,
        out_specs=[pl.BlockSpec((tm, D), lambda i: (i, 0)), pl.BlockSpec((tm, NSHARD * CS), lambda i: (i, 0))],
        out_shape=[_sds((T, D), BF16), _sds((T, NSHARD * CS), BF16)],
        compiler_params=_cparams(("parallel",)),
    )(x2d, g_mix, w_in_g, *dep_arg)


def _halo_maps(tm, n_rows):
    nb = tm // HALO
    last = n_rows // HALO - 1
    prev = lambda i: (jnp.maximum(i * nb - 1, 0), 0)
    nxt = lambda i: (jnp.minimum((i + 1) * nb, last), 0)
    return prev, nxt


def _dwconv(pad_ref, w_ref, out_ref, n_strips, tm, kw, rb):
    off = HALO - (kw - 1) // 2

    def strip(cs, carry):
        for r0 in range(0, tm, rb):
            acc = jnp.zeros((rb, LANES), F32)
            for k in range(kw):
                r = r0 + off + k
                acc = acc + w_ref[cs, k:k + 1, :] * pad_ref[cs, r:r + rb, :]
            out_ref[cs, r0:r0 + rb, :] = acc
        return carry

    lax.fori_loop(0, n_strips, strip, 0)


def _fill_c0_pad(pad_ref, pa_ref, pprev_ref, pnext_ref, D, tm, first, last):
    for cs in range(D // LANES):
        lo, hi = cs * LANES, (cs + 1) * LANES

        def c0_of(ref):
            return ref[:, lo:hi].astype(F32) * _sig(ref[:, D + lo:D + hi].astype(F32))

        pad_ref[cs, HALO:HALO + tm, :] = c0_of(pa_ref)
        pad_ref[cs, 0:HALO, :] = jnp.where(first, 0.0, c0_of(pprev_ref))
        pad_ref[cs, HALO + tm:HALO + tm + HALO, :] = jnp.where(last, 0.0, c0_of(pnext_ref))


def _conv_fwd(proj, conv_w_s, conv_b, ln_g, ln_b, w_sq_g, layer, seq, tm, rb):
    T = proj.shape[0]
    D = conv_b.shape[1]
    DQ = D // NSHARD
    NSTR = D // LANES
    KW = 31
    tps = seq // tm
    prev, nxt = _halo_maps(tm, T)

    def body(pa_ref, pprev_ref, pnext_ref, w_ref, b_ref, g_ref, be_ref, wco_ref,
             c1h_ref, rstd_ref, c3_ref, ya_ref, pad_ref, c1s_ref):
        i = pl.program_id(0)
        first = (i % tps) == 0
        last = (i % tps) == tps - 1
        _fill_c0_pad(pad_ref, pa_ref, pprev_ref, pnext_ref, D, tm, first, last)
        _dwconv(pad_ref, w_ref, c1s_ref, NSTR, tm, KW, rb)
        wco = wco_ref[...].reshape(D, D)
        for r0 in (0, tm // 2):
            rows = slice(r0, r0 + tm // 2)
            c1 = jnp.concatenate([c1s_ref[cs, rows, :] for cs in range(NSTR)], axis=1) + b_ref[...]
            mu = jnp.mean(c1, axis=-1, keepdims=True)
            cc = c1 - mu
            rstd = lax.rsqrt(jnp.mean(cc * cc, axis=-1, keepdims=True) + EPS)
            c1h = cc * rstd
            c1h_ref[rows, :] = c1h.astype(BF16)
            rstd_ref[rows, :] = rstd
            c2 = c1h * g_ref[...] + be_ref[...]
            c3 = (c2 * _sig(c2)).astype(BF16)
            c3_ref[rows, :] = c3
            ya_ref[rows, :] = _dot(c3, wco).astype(BF16)

    row = lambda i: (i, 0)
    return pl.pallas_call(
        body, name="conv_fwd", grid=(T // tm,),
        in_specs=[pl.BlockSpec((tm, 2 * D), row), pl.BlockSpec((HALO, 2 * D), prev), pl.BlockSpec((HALO, 2 * D), nxt),
                  _res((NSTR, 32, LANES)), _res((1, D)), _res((1, D)), _res((1, D)),
                  _res((NSHARD, DQ, D), lambda i: (0, layer * 3 + 0, 0))],
        out_specs=[pl.BlockSpec((tm, D), row), pl.BlockSpec((tm, 1), row), pl.BlockSpec((tm, D), row),
                   pl.BlockSpec((tm, D), row)],
        out_shape=[_sds((T, D), BF16), _sds((T, 1), F32), _sds((T, D), BF16), _sds((T, D), BF16)],
        scratch_shapes=[pltpu.VMEM((NSTR, tm + 2 * HALO, LANES), F32), pltpu.VMEM((NSTR, tm, LANES), F32)],
        compiler_params=_cparams(("parallel",)),
    )(proj, proj, proj, conv_w_s, conv_b, ln_g, ln_b, w_sq_g)


def _ffn_chunks(F):
    assert F % 256 == 0, F
    return [(c0, min(512, F - c0)) for c0 in range(0, F, 512)]


def _sgu_ffn_fwd(proj, ya, x2d, ln_g, ln_b, ws_b, bs_b, gate_bias, w_sq_g, layer, g_ffn, wgt, wut, wd, tm):
    T, D = x2d.shape
    DQ = D // NSHARD
    G, CH, _ = ws_b.shape
    GD = D // G
    F = wd.shape[0]

    def body(puv_ref, pg_ref, ya_ref, x_ref, g_ref, be_ref, ws_ref, bsb_ref, gb_ref, wso_ref, wo_ref,
             gf_ref, wg_ref, wu_ref, wd_ref,
             mixed_ref, gated_ref, yb_ref, merged_ref, x1_ref, h2_ref, gt_ref, up_ref, act_ref, x2_ref, mix_scr):
        u = puv_ref[:, :D].astype(F32)
        v = puv_ref[:, D:].astype(F32)
        mu = jnp.mean(v, axis=-1, keepdims=True)
        vc = v - mu
        rstd = lax.rsqrt(jnp.mean(vc * vc, axis=-1, keepdims=True) + EPS)
        vn = (vc * rstd * g_ref[...] + be_ref[...]).astype(BF16)
        nch = tm // CH
        for g in range(G):
            cols = slice(g * GD, (g + 1) * GD)
            rhs = jnp.concatenate([vn[ch * CH:(ch + 1) * CH, cols] for ch in range(nch)], axis=1)
            res = _dot(ws_ref[g], rhs)
            for ch in range(nch):
                mix_scr[ch * CH:(ch + 1) * CH, cols] = res[:, ch * GD:(ch + 1) * GD] + bsb_ref[:, cols]
        mixed = mix_scr[...]
        mixed_ref[...] = mixed.astype(BF16)
        gated = (u * mixed).astype(BF16)
        gated_ref[...] = gated
        yb = _dot(gated, wso_ref[...].reshape(D, D))
        yb_ref[...] = yb.astype(BF16)
        sa = _sig(pg_ref[:, :D].astype(F32) + gb_ref[:, :D])
        sb = _sig(pg_ref[:, D:].astype(F32) + gb_ref[:, D:])
        merged = (sa * ya_ref[...].astype(F32) + sb * yb).astype(BF16)
        merged_ref[...] = merged
        x1 = x_ref[...] + _dot(merged, wo_ref[...].reshape(D, D))
        x1_ref[...] = x1

        rstd = lax.rsqrt(jnp.mean(x1 * x1, axis=-1, keepdims=True) + EPS)
        h2 = (x1 * rstd * gf_ref[...]).astype(BF16)
        h2_ref[...] = h2
        acc = x1
        chunks = _ffn_chunks(F)

        def gate_up(c0, cw):
            return _dot_nt(h2, wg_ref[c0:c0 + cw, :]), _dot_nt(h2, wu_ref[c0:c0 + cw, :])

        nxt = gate_up(*chunks[0])
        for ci, (c0, cw) in enumerate(chunks):
            gt, up = nxt
            if ci + 1 < len(chunks):
                nxt = gate_up(*chunks[ci + 1])
            gt_ref[:, c0:c0 + cw] = gt.astype(BF16)
            up_ref[:, c0:c0 + cw] = up.astype(BF16)
            act = (gt * _sig(gt) * up).astype(BF16)
            act_ref[:, c0:c0 + cw] = act
            acc = acc + _dot(act, wd_ref[c0:c0 + cw, :])
        x2_ref[...] = acc

    row = lambda i: (i, 0)
    wide = pl.BlockSpec((tm, F), row)
    return pl.pallas_call(
        body, name="sgu_ffn_fwd", grid=(T // tm,),
        in_specs=[pl.BlockSpec((tm, 2 * D), lambda i: (i, 1)), pl.BlockSpec((tm, 2 * D), lambda i: (i, 2)),
                  pl.BlockSpec((tm, D), row), pl.BlockSpec((tm, D), row),
                  _res((1, D)), _res((1, D)), _res((G, CH, CH)), _res((CH, D)), _res((1, 2 * D)),
                  _res((NSHARD, DQ, D), lambda i: (0, layer * 3 + 1, 0)),
                  _res((NSHARD, DQ, D), lambda i: (0, layer * 3 + 2, 0)),
                  _res((1, D)), _res((F, D)), _res((F, D)), _res((F, D))],
        out_specs=[pl.BlockSpec((tm, D), row)] * 6 + [wide, wide, wide, pl.BlockSpec((tm, D), row)],
        out_shape=[_sds((T, D), BF16)] * 4 + [_sds((T, D), F32), _sds((T, D), BF16)] + [_sds((T, F), BF16)] * 3
        + [_sds((T, D), F32)],
        scratch_shapes=[pltpu.VMEM((tm, D), F32)],
        compiler_params=_cparams(("parallel",)),
    )(proj, proj, ya, x2d, ln_g, ln_b, ws_b, bs_b, gate_bias, w_sq_g, w_sq_g, g_ffn, wgt, wut, wd)


def _loss_head(xf, g_fin, target, tm):
    T, D = xf.shape
    n = T // tm

    def body(x_ref, g_ref, t_ref, dx_ref, loss_ref, dg_ref, acc_ref):
        i = pl.program_id(0)

        @pl.when(i == 0)
        def _():
            acc_ref[...] = jnp.zeros_like(acc_ref)
            dg_ref[...] = jnp.zeros_like(dg_ref)

        x = x_ref[...]
        g = g_ref[...]
        rstd = lax.rsqrt(jnp.mean(x * x, axis=-1, keepdims=True) + EPS)
        xh = x * rstd
        diff = xh * g - t_ref[...]
        acc_ref[...] += jnp.sum(diff * diff, axis=0, keepdims=True)
        dy = diff * (1.0 / D)
        dg_ref[...] += jnp.sum(dy * xh, axis=0, keepdims=True)
        dxh = dy * g
        dx_ref[...] = rstd * (dxh - xh * jnp.mean(dxh * xh, axis=-1, keepdims=True))

        @pl.when(i == n - 1)
        def _():
            tot = jnp.sum(acc_ref[...], axis=-1, keepdims=True) * (0.5 / D)
            loss_ref[...] = jnp.broadcast_to(tot, loss_ref.shape)

    row = lambda i: (i, 0)
    return pl.pallas_call(
        body, name="loss_head", grid=(n,),
        in_specs=[pl.BlockSpec((tm, D), row), _res((1, D)), pl.BlockSpec((tm, D), row)],
        out_specs=[pl.BlockSpec((tm, D), row), pl.BlockSpec((1, LANES), lambda i: (0, 0)),
                   pl.BlockSpec((1, D), lambda i: (0, 0))],
        out_shape=[_sds((T, D), F32), _sds((1, LANES), F32), _sds((1, D), F32)],
        scratch_shapes=[pltpu.VMEM((1, D), F32)],
        compiler_params=_cparams(("arbitrary",)),
    )(xf, g_fin, target)


def _ffn_bwd(dx2, x1, gt, up, g_ffn, wgt, wut, wd, tm, dep=None):
    T, D = x1.shape
    F = wd.shape[0]

    def body(dx2_ref, x1_ref, gt_ref, up_ref, g_ref, wg_ref, wu_ref, wd_ref, dx1_ref, dgt_ref, dup_ref, dg_ref):
        i = pl.program_id(0)

        @pl.when(i == 0)
        def _():
            dg_ref[...] = jnp.zeros_like(dg_ref)

        dx2 = dx2_ref[...]
        dx2b = dx2.astype(BF16)
        dh2 = jnp.zeros((tm, D), F32)
        chunks = _ffn_chunks(F)
        dact_next = _dot_nt(dx2b, wd_ref[0:chunks[0][1], :])
        for ci, (c0, cw) in enumerate(chunks):
            dact = dact_next
            if ci + 1 < len(chunks):
                n0, nw = chunks[ci + 1]
                dact_next = _dot_nt(dx2b, wd_ref[n0:n0 + nw, :])
            g = gt_ref[:, c0:c0 + cw].astype(F32)
            u = up_ref[:, c0:c0 + cw].astype(F32)
            sg = _sig(g)
            dup = (dact * (g * sg)).astype(BF16)
            dgt = (dact * u * (sg * (1.0 + g * (1.0 - sg)))).astype(BF16)
            dgt_ref[:, c0:c0 + cw] = dgt
            dup_ref[:, c0:c0 + cw] = dup
            dh2 = dh2 + _dot(dgt, wg_ref[c0:c0 + cw, :]) + _dot(dup, wu_ref[c0:c0 + cw, :])
        x = x1_ref[...]
        rstd = lax.rsqrt(jnp.mean(x * x, axis=-1, keepdims=True) + EPS)
        xh = x * rstd
        dg_ref[...] += jnp.sum(dh2 * xh, axis=0, keepdims=True)
        dxh = dh2 * g_ref[...]
        dx1_ref[...] = dx2 + rstd * (dxh - xh * jnp.mean(dxh * xh, axis=-1, keepdims=True))

    row = lambda i: (i, 0)
    body, dep_spec, dep_arg = _after(body, 8, dep)
    return pl.pallas_call(
        body, name="ffn_bwd", grid=(T // tm,),
        in_specs=[pl.BlockSpec((tm, D), row), pl.BlockSpec((tm, D), row), pl.BlockSpec((tm, F), row),
                  pl.BlockSpec((tm, F), row), _res((1, D)), _res((F, D)), _res((F, D)), _res((F, D))] + dep_spec,
        out_specs=[pl.BlockSpec((tm, D), row), pl.BlockSpec((tm, F), row), pl.BlockSpec((tm, F), row),
                   pl.BlockSpec((1, D), lambda i: (0, 0))],
        out_shape=[_sds((T, D), F32), _sds((T, F), BF16), _sds((T, F), BF16), _sds((1, D), F32)],
        compiler_params=_cparams(("arbitrary",)),
    )(dx2, x1, gt, up, g_ffn, wgt, wut, wd, *dep_arg)


def _merge_sgu_bwd(dx1, proj, ya, yb, mixed, c1h, rstd_c, ln_g, ln_b, ws_b, wst_b, gate_bias, cln_g, cln_b, w_sq_g,
                   layer, tm, dep=None):
    T, D = dx1.shape
    DQ = D // NSHARD
    G, CH, _ = ws_b.shape
    GD = D // G

    def body(dx1_ref, puv_ref, pg_ref, ya_ref, yb_ref, mixed_ref, c1h_ref, rstdc_ref, g_ref, be_ref, ws_ref, wst_ref,
             gb_ref, cg_ref, cbe_ref, wco_ref, wso_ref, wo_ref,
             dya_ref, dyb_ref, dp_ref, dc1_ref, dgb_ref, dlg_ref, dlb_ref, dbs_ref, dws_ref, dcg_ref, dcbe_ref,
             dcb_ref, dvn_scr, dbs_scr):
        i = pl.program_id(0)

        @pl.when(i == 0)
        def _():
            for r in (dgb_ref, dlg_ref, dlb_ref, dws_ref, dbs_scr, dcg_ref, dcbe_ref, dcb_ref):
                r[...] = jnp.zeros_like(r)

        dmerged = _dot_nt(dx1_ref[...].astype(BF16), wo_ref[...].reshape(D, D))
        sa = _sig(pg_ref[:, :D].astype(F32) + gb_ref[:, :D])
        sb = _sig(pg_ref[:, D:].astype(F32) + gb_ref[:, D:])
        dya = (dmerged * sa).astype(BF16)
        dyb = (dmerged * sb).astype(BF16)
        dya_ref[...] = dya
        dyb_ref[...] = dyb
        dc3 = _dot_nt(dya, wco_ref[...].reshape(D, D))
        dgated = _dot_nt(dyb, wso_ref[...].reshape(D, D))

        c1h = c1h_ref[...].astype(F32)
        c2 = c1h * cg_ref[...] + cbe_ref[...]
        sg = _sig(c2)
        dc2 = dc3 * (sg * (1.0 + c2 * (1.0 - sg)))
        dcg_ref[...] += jnp.sum(dc2 * c1h, axis=0, keepdims=True)
        dcbe_ref[...] += jnp.sum(dc2, axis=0, keepdims=True)
        dch = dc2 * cg_ref[...]
        dc1 = rstdc_ref[...] * (dch - jnp.mean(dch, axis=-1, keepdims=True)
                                - c1h * jnp.mean(dch * c1h, axis=-1, keepdims=True))
        dc1_ref[...] = dc1
        dcb_ref[...] += jnp.sum(dc1, axis=0, keepdims=True)

        dga = dmerged * ya_ref[...].astype(F32) * (sa * (1.0 - sa))
        dgb = dmerged * yb_ref[...].astype(F32) * (sb * (1.0 - sb))
        dp_ref[:, 4 * D:5 * D] = dga.astype(BF16)
        dp_ref[:, 5 * D:6 * D] = dgb.astype(BF16)
        dgb_ref[:, :D] += jnp.sum(dga, axis=0, keepdims=True)
        dgb_ref[:, D:] += jnp.sum(dgb, axis=0, keepdims=True)

        u = puv_ref[:, :D].astype(F32)
        v = puv_ref[:, D:].astype(F32)
        dp_ref[:, 2 * D:3 * D] = (dgated * mixed_ref[...].astype(F32)).astype(BF16)
        dmixed = dgated * u
        mu = jnp.mean(v, axis=-1, keepdims=True)
        vc = v - mu
        rstd = lax.rsqrt(jnp.mean(vc * vc, axis=-1, keepdims=True) + EPS)
        vh = vc * rstd
        vn = (vh * g_ref[...] + be_ref[...]).astype(BF16)
        dmb = dmixed.astype(BF16)
        nch = tm // CH
        bs_part = dmixed[0:CH, :]
        for ch in range(1, nch):
            bs_part = bs_part + dmixed[ch * CH:(ch + 1) * CH, :]
        dbs_scr[...] += bs_part
        for g in range(G):
            cols = slice(g * GD, (g + 1) * GD)
            dm_g = jnp.concatenate([dmb[ch * CH:(ch + 1) * CH, cols] for ch in range(nch)], axis=1)
            vn_g = jnp.concatenate([vn[ch * CH:(ch + 1) * CH, cols] for ch in range(nch)], axis=1)
            dws_ref[g] += _dot_nt(dm_g, vn_g)
            dvn_g = _dot(wst_ref[g], dm_g)
            for ch in range(nch):
                dvn_scr[ch * CH:(ch + 1) * CH, cols] = dvn_g[:, ch * GD:(ch + 1) * GD]
        dvn = dvn_scr[...]
        dlg_ref[...] += jnp.sum(dvn * vh, axis=0, keepdims=True)
        dlb_ref[...] += jnp.sum(dvn, axis=0, keepdims=True)
        dxh = dvn * g_ref[...]
        dv = rstd * (dxh - jnp.mean(dxh, axis=-1, keepdims=True) - vh * jnp.mean(dxh * vh, axis=-1, keepdims=True))
        dp_ref[:, 3 * D:4 * D] = dv.astype(BF16)

        @pl.when(i == pl.num_programs(0) - 1)
        def _():
            for g in range(G):
                blk = dbs_scr[:, g * GD:(g + 1) * GD]
                if GD != CH:
                    blk = jnp.concatenate([blk, jnp.zeros((CH, CH - GD), F32)], axis=1)
                dbs_ref[:, g * CH:(g + 1) * CH] = jnp.sum(blk.T, axis=0, keepdims=True)

    row = lambda i: (i, 0)
    fixed2 = lambda i: (0, 0)
    body, dep_spec, dep_arg = _after(body, 18, dep)
    vec = pl.BlockSpec((1, D), fixed2)
    return pl.pallas_call(
        body, name="merge_sgu_bwd", grid=(T // tm,),
        in_specs=[pl.BlockSpec((tm, D), row), pl.BlockSpec((tm, 2 * D), lambda i: (i, 1)),
                  pl.BlockSpec((tm, 2 * D), lambda i: (i, 2)), pl.BlockSpec((tm, D), row), pl.BlockSpec((tm, D), row),
                  pl.BlockSpec((tm, D), row), pl.BlockSpec((tm, D), row), pl.BlockSpec((tm, 1), row),
                  _res((1, D)), _res((1, D)), _res((G, CH, CH)), _res((G, CH, CH)), _res((1, 2 * D)),
                  _res((1, D)), _res((1, D)),
                  _res((NSHARD, DQ, D), lambda i: (0, layer * 3 + 0, 0)),
                  _res((NSHARD, DQ, D), lambda i: (0, layer * 3 + 1, 0)),
                  _res((NSHARD, DQ, D), lambda i: (0, layer * 3 + 2, 0))] + dep_spec,
        out_specs=[pl.BlockSpec((tm, D), row), pl.BlockSpec((tm, D), row), pl.BlockSpec((tm, 6 * D), row),
                   pl.BlockSpec((tm, D), row), pl.BlockSpec((1, 2 * D), fixed2), vec, vec,
                   pl.BlockSpec((1, G * CH), fixed2), pl.BlockSpec((G, CH, CH), lambda i: (0, 0, 0)), vec, vec, vec],
        out_shape=[_sds((T, D), BF16), _sds((T, D), BF16), _sds((T, 6 * D), BF16), _sds((T, D), F32),
                   _sds((1, 2 * D), F32), _sds((1, D), F32), _sds((1, D), F32), _sds((1, G * CH), F32),
                   _sds((G, CH, CH), F32), _sds((1, D), F32), _sds((1, D), F32), _sds((1, D), F32)],
        scratch_shapes=[pltpu.VMEM((tm, D), F32), pltpu.VMEM((CH, D), F32)],
        compiler_params=_cparams(("arbitrary",)),
    )(dx1, proj, proj, ya, yb, mixed, c1h, rstd_c, ln_g, ln_b, ws_b, wst_b, gate_bias, cln_g, cln_b,
      w_sq_g, w_sq_g, w_sq_g, *dep_arg)


def _conv_bwd(dc1, proj, dp3, conv_wf_s, seq, tm, rb, dep=None):
    T, D = dc1.shape
    NSTR = D // LANES
    KW = 31
    PADK = (KW - 1) // 2
    tps = seq // tm
    prev, nxt = _halo_maps(tm, T)
    n = T // tm

    def body(dc_ref, dcprev_ref, dcnext_ref, pa_ref, pprev_ref, pnext_ref, wf_ref, dp_in_ref,
             dp_ref, dw_ref, pad_ref, dpad_ref, dc0_ref, dwacc_ref):
        del dp_in_ref
        i = pl.program_id(0)
        first = (i % tps) == 0
        last = (i % tps) == tps - 1

        @pl.when(i == 0)
        def _():
            dwacc_ref[...] = jnp.zeros_like(dwacc_ref)

        _fill_c0_pad(pad_ref, pa_ref, pprev_ref, pnext_ref, D, tm, first, last)
        for cs in range(NSTR):
            lo, hi = cs * LANES, (cs + 1) * LANES
            dpad_ref[cs, HALO:HALO + tm, :] = dc_ref[:, lo:hi]
            dpad_ref[cs, 0:HALO, :] = jnp.where(first, 0.0, dcprev_ref[:, lo:hi])
            dpad_ref[cs, HALO + tm:HALO + tm + HALO, :] = jnp.where(last, 0.0, dcnext_ref[:, lo:hi])
        _dwconv(dpad_ref, wf_ref, dc0_ref, NSTR, tm, KW, rb)

        def strip(cs, carry):
            for r0 in range(0, tm, rb):
                d = dpad_ref[cs, HALO + r0:HALO + r0 + rb, :]
                for k in range(KW):
                    r = r0 + HALO - PADK + k
                    prod = d * pad_ref[cs, r:r + rb, :]
                    dwacc_ref[cs, k * 8:(k + 1) * 8, :] += jnp.sum(prod.reshape(rb // 8, 8, LANES), axis=0)
            return carry

        lax.fori_loop(0, NSTR, strip, 0)

        for cs in range(NSTR):
            lo, hi = cs * LANES, (cs + 1) * LANES
            av = pa_ref[:, lo:hi].astype(F32)
            sg = _sig(pa_ref[:, D + lo:D + hi].astype(F32))
            dc0 = dc0_ref[cs]
            dp_ref[:, lo:hi] = (dc0 * sg).astype(BF16)
            dp_ref[:, D + lo:D + hi] = (dc0 * av * (sg * (1.0 - sg))).astype(BF16)

        @pl.when(i == n - 1)
        def _():
            for cs in range(NSTR):
                dw_ref[cs] = jnp.sum(dwacc_ref[cs].reshape(32, 8, LANES), axis=1)

    row = lambda i: (i, 0)
    body, dep_spec, dep_arg = _after(body, 8, dep)
    return pl.pallas_call(
        body, name="conv_bwd", grid=(n,),
        in_specs=[pl.BlockSpec((tm, D), row), pl.BlockSpec((HALO, D), prev), pl.BlockSpec((HALO, D), nxt),
                  pl.BlockSpec((tm, 2 * D), row), pl.BlockSpec((HALO, 2 * D), prev), pl.BlockSpec((HALO, 2 * D), nxt),
                  _res((NSTR, 32, LANES)), pl.BlockSpec(memory_space=pl.ANY)] + dep_spec,
        out_specs=[pl.BlockSpec((tm, 2 * D), row),
                   pl.BlockSpec((NSTR, 32, LANES), lambda i: (0, 0, 0))],
        out_shape=[_sds(dp3.shape, BF16), _sds((NSTR, 32, LANES), F32)],
        scratch_shapes=[pltpu.VMEM((NSTR, tm + 2 * HALO, LANES), F32), pltpu.VMEM((NSTR, tm + 2 * HALO, LANES), F32),
                        pltpu.VMEM((NSTR, tm, LANES), F32), pltpu.VMEM((NSTR, 32 * 8, LANES), F32)],
        input_output_aliases={7: 0},
        compiler_params=_cparams(("arbitrary",)),
    )(dc1, dc1, dc1, proj, proj, proj, conv_wf_s, dp3, *dep_arg)


def _mix_in_bwd(dx1, dp3, x2d, g_mix, w_in_g, layer, tm, dep=None):
    T, D = x2d.shape
    CS = w_in_g.shape[2]
    CN = CS // 3

    def body(dx1_ref, dp_ref, x_ref, g_ref, w_ref, dx_ref, dg_ref):
        i = pl.program_id(0)

        @pl.when(i == 0)
        def _():
            dg_ref[...] = jnp.zeros_like(dg_ref)

        dh = jnp.zeros((tm, D), F32)
        for j in range(12):
            dh = dh + _dot_nt(dp_ref[:, j * CN:(j + 1) * CN], w_ref[j // 3, :, (j % 3) * CN:(j % 3 + 1) * CN])
        x = x_ref[...]
        rstd = lax.rsqrt(jnp.mean(x * x, axis=-1, keepdims=True) + EPS)
        xh = x * rstd
        dg_ref[...] += jnp.sum(dh * xh, axis=0, keepdims=True)
        dxh = dh * g_ref[...]
        dx_ref[...] = dx1_ref[...] + rstd * (dxh - xh * jnp.mean(dxh * xh, axis=-1, keepdims=True))

    row = lambda i: (i, 0)
    body, dep_spec, dep_arg = _after(body, 5, dep)
    return pl.pallas_call(
        body, name="mix_in_bwd", grid=(T // tm,),
        in_specs=[pl.BlockSpec((tm, D), row), pl.BlockSpec((tm, 6 * D), row),
                  pl.BlockSpec((tm, D), row), _res((1, D)), _res((NSHARD, D, CS), lambda i: (0, layer, 0))] + dep_spec,
        out_specs=[pl.BlockSpec((tm, D), row), pl.BlockSpec((1, D), lambda i: (0, 0))],
        out_shape=[_sds((T, D), F32), _sds((1, D), F32)],
        compiler_params=_cparams(("arbitrary",)),
    )(dx1, dp3, x2d, g_mix, w_in_g, *dep_arg)


def _tn_matmul(name, a, a_block, a_map, bs, b_block, b_map, out_shape, out_block, out_map, nj, nt):
    kk = [d for d in a_block if d is not None][-1]
    nn = [d for d in b_block if d is not None][-1]
    nb = len(bs)
    a_list = a if isinstance(a, (list, tuple)) else [a]
    na = len(a_list)

    def body(*refs):
        a_refs, b_refs = refs[:na], refs[na:na + nb]
        o_refs, acc_refs = refs[-2 * nb:-nb], refs[-nb:]
        t = pl.program_id(1)

        @pl.when(t == 0)
        def _():
            for acc_ref in acc_refs:
                acc_ref[...] = jnp.zeros_like(acc_ref)

        a_ts = [a_ref[...].astype(BF16) for a_ref in a_refs]
        for i, (b_ref, acc_ref) in enumerate(zip(b_refs, acc_refs)):
            acc_ref[...] += _dot_tn(a_ts[i % na], b_ref[...].astype(BF16))

        @pl.when(t == nt - 1)
        def _():
            for o_ref, acc_ref in zip(o_refs, acc_refs):
                o_ref[...] = acc_ref[...].astype(o_ref.dtype)

    return pl.pallas_call(
        body, name=name, grid=(nj, nt),
        in_specs=[pl.BlockSpec(a_block, a_map)] * na + [pl.BlockSpec(b_block, b_map)] * nb,
        out_specs=[pl.BlockSpec(out_block, out_map)] * nb, out_shape=[_sds(out_shape, BF16)] * nb,
        scratch_shapes=[pltpu.VMEM((kk, nn), F32)] * nb,
        compiler_params=_cparams(("parallel", "arbitrary")),
    )(*a_list, *bs)


def _place_shard(name, w, pos, dtype, tr, dep=None, layer=None, into=None, row_off=0, out_rows=None):
    R, C = w.shape[-2:]
    out_rows = out_rows or R

    def body(pos_ref, w_ref, *rest):
        del pos_ref
        rest[-1][...] = w_ref[...].astype(dtype)

    in_spec = (pl.BlockSpec((tr, C), lambda r, pos: (r, 0)) if layer is None else
               pl.BlockSpec((None, tr, C), lambda r, pos: (layer, r, 0)))
    extra, extra_args = ([], []) if into is None else ([pl.BlockSpec(memory_space=pl.ANY)], [into])
    body, dep_spec, dep_arg = _after(body, 2 + len(extra), dep)
    grid_spec = pltpu.PrefetchScalarGridSpec(
        num_scalar_prefetch=1, grid=(R // tr,), in_specs=[in_spec] + extra + dep_spec,
        out_specs=pl.BlockSpec((None, tr, C), lambda r, pos: (pos[1], row_off // tr + r, 0)))
    return pl.pallas_call(body, name=name, grid_spec=grid_spec, out_shape=_sds((NSHARD, out_rows, C), dtype),
                          input_output_aliases={} if into is None else {2: 0},
                          compiler_params=_cparams(("parallel",)))(pos, w, *extra_args, *dep_arg)


def _add_halves(name, g, rbuf, pos, tr):
    NS, _, H, C = g.shape

    def body(pos_ref, g_ref, r_ref, o_ref):
        del pos_ref
        o_ref[...] = (g_ref[...].astype(F32) + r_ref[...].astype(F32)).astype(BF16)

    grid_spec = pltpu.PrefetchScalarGridSpec(
        num_scalar_prefetch=1, grid=(NS, H // tr),
        in_specs=[pl.BlockSpec((None, None, tr, C), lambda s, r, pos: (s, pos[0], r, 0)),
                  pl.BlockSpec((None, tr, C), lambda s, r, pos: (s, r, 0))],
        out_specs=pl.BlockSpec((None, tr, C), lambda s, r, pos: (s, r, 0)))
    return pl.pallas_call(body, name=name, grid_spec=grid_spec, out_shape=_sds((NS, H, C), BF16),
                          compiler_params=_cparams(("parallel", "parallel")))(pos, g, rbuf)


def _add_shards(name, p, rbuf, pos, tr, layer, n_layers, prev):
    _, H, C = p.shape

    def body(pos_ref, p_ref, r_ref, *rest):
        del pos_ref
        o_ref = rest[-1]
        acc = p_ref[...].astype(F32)
        for j in range(3):
            acc = acc + r_ref[j].astype(F32)
        o_ref[...] = acc

    in_specs = [pl.BlockSpec((None, tr, C), lambda r, pos: (pos[1], r, 0)),
                pl.BlockSpec((3, tr, C), lambda r, pos: (0, r, 0))]
    args = [pos, p, rbuf]
    aliases = {}
    if prev is not None:
        in_specs.append(pl.BlockSpec(memory_space=pl.ANY))
        args.append(prev)
        aliases = {3: 0}
    grid_spec = pltpu.PrefetchScalarGridSpec(
        num_scalar_prefetch=1, grid=(H // tr,), in_specs=in_specs,
        out_specs=pl.BlockSpec((None, None, tr, C), lambda r, pos: (layer, pos[0], r, 0)))
    return pl.pallas_call(body, name=name, grid_spec=grid_spec, out_shape=_sds((n_layers, 2, H, C), F32),
                          input_output_aliases=aliases, compiler_params=_cparams(("parallel",)))(*args)


def _sum_slots(own, land, me, tr):
    NS8, R, C = land.shape

    def body(me_ref, own_ref, l_ref, o_ref):
        acc = None
        for j in range(NS8):
            term = jnp.where(me_ref[0] == j, own_ref[...], l_ref[j])
            acc = term if acc is None else acc + term
        o_ref[...] = acc

    grid_spec = pltpu.PrefetchScalarGridSpec(
        num_scalar_prefetch=1, grid=(R // tr,),
        in_specs=[pl.BlockSpec((tr, C), lambda i, me: (i, 0)), pl.BlockSpec((NS8, tr, C), lambda i, me: (0, i, 0))],
        out_specs=pl.BlockSpec((tr, C), lambda i, me: (i, 0)))
    return pl.pallas_call(body, name="sum_slots", grid_spec=grid_spec, out_shape=_sds((R, C), F32),
                          compiler_params=_cparams(("parallel",)))(me, own, land)


def _adamw_update(w_ref, g_ref, m_ref, v_ref, d_ref, mo_ref, vo_ref):
    g_ = g_ref[...]
    m_ = ADAM_B1 * m_ref[...] + (1.0 - ADAM_B1) * g_
    v_ = ADAM_B2 * v_ref[...] + (1.0 - ADAM_B2) * (g_ * g_)
    mo_ref[...] = m_
    vo_ref[...] = v_
    m_hat = m_ / (1.0 - ADAM_B1 ** ADAM_STEP)
    v_hat = v_ / (1.0 - ADAM_B2 ** ADAM_STEP)
    d_ref[...] = -ADAM_LR * (m_hat / (jnp.sqrt(v_hat) + ADAM_EPS) + ADAM_WD * w_ref[...])


def _adamw(name, w, g, m, v, tr, emit_g):
    R, C = w.shape
    n_out = 4 if emit_g else 3

    def body(w_ref, g_ref, m_ref, v_ref, d_ref, mo_ref, vo_ref, *go_ref):
        _adamw_update(w_ref, g_ref, m_ref, v_ref, d_ref, mo_ref, vo_ref)
        if emit_g:
            go_ref[0][...] = g_ref[...]

    spec = pl.BlockSpec((tr, C), lambda i: (i, 0))
    return pl.pallas_call(
        body, name=name, grid=(R // tr,), in_specs=[spec] * 4, out_specs=[spec] * n_out,
        out_shape=[_sds((R, C), F32)] * n_out, compiler_params=_cparams(("parallel",)))(w, g, m, v)


def _adamw_many(ws, gs, ms, vs):
    n = len(ws)

    def body(*refs):
        ins, outs = refs[:4 * n], refs[4 * n:]
        for k in range(n):
            _adamw_update(ins[k], ins[n + k], ins[2 * n + k], ins[3 * n + k], outs[k], outs[n + k], outs[2 * n + k])

    vmem = pl.BlockSpec(memory_space=pltpu.VMEM)
    res = pl.pallas_call(
        body, name="adamw_small", in_specs=[vmem] * (4 * n), out_specs=[vmem] * (3 * n),
        out_shape=[_sds(w.shape, F32) for w in ws] * 3,
        compiler_params=pltpu.CompilerParams(vmem_limit_bytes=VMEM_LIMIT))(*ws, *gs, *ms, *vs)
    return list(res[:n]), list(res[n:2 * n]), list(res[2 * n:])


def _row_tile(rows, cap):
    best = rows
    for t in range(8, min(rows, cap) + 1, 8):
        if rows % t == 0:
            best = t
    return best


HBM_SPEC = pl.BlockSpec(memory_space=pltpu.HBM)
SEM_SPEC = pl.BlockSpec(memory_space=pltpu.SEMAPHORE)
DATAFLOW = pltpu.SideEffectType.DATAFLOW_SIDE_EFFECTING
DMA_SEM = pltpu.SemaphoreType.DMA


def _hbm(a):
    return pltpu.with_memory_space_constraint(a, pltpu.HBM)


def _mesh_pos():
    return lax.axis_index("x"), lax.axis_index("y"), lax.axis_index("c")


def _other_chips(x, y):
    return [(1 - x, y), (x, 1 - y), (1 - x, 1 - y)]


def _half_rows(buf, shard, core):
    h = buf.shape[1] // 2
    return buf.at[shard, pl.ds(core * h, h), :]


def _ici_copy(buf, j, send, recv, landing):
    x, y, c = _mesh_pos()
    px, py = _other_chips(x, y)[j]
    part = _half_rows(buf, 2 * px + py if landing else 2 * x + y, c)
    return pltpu.make_async_remote_copy(src_ref=part, dst_ref=part, send_sem=send, recv_sem=recv,
                                        device_id=(px, py, c), device_id_type=MESH_ID)


def _sibling_copy(buf, j, send, recv, landing):
    x, y, c = _mesh_pos()
    px, py = _other_chips(x, y)[j]
    part = _half_rows(buf, 2 * px + py, 1 - c if landing else c)
    return pltpu.make_async_remote_copy(src_ref=part, dst_ref=part, send_sem=send, recv_sem=recv,
                                        device_id=(x, y, 1 - c), device_id_type=MESH_ID)


def _forward_sibling(name, bufs, with_ici):
    n = len(bufs)

    def body(*refs):
        ins = refs[:n]
        send_ici, recv_ici, send_d2d, recv_d2d = refs[2 * n:]
        sends = []
        if with_ici:
            for i in range(n):
                for j in range(3):
                    cp = _ici_copy(ins[i], j, send_ici.at[i, j], recv_ici.at[i, j], False)
                    cp.start()
                    sends.append(cp)
        for i in range(n):
            for j in range(3):
                if with_ici:
                    _ici_copy(ins[i], j, send_ici.at[i, j], recv_ici.at[i, j], True).wait_recv()
                cp = _sibling_copy(ins[i], j, send_d2d.at[i, j], recv_d2d.at[i, j], False)
                cp.start()
                sends.append(cp)
        for i in range(n):
            for j in range(3):
                _sibling_copy(ins[i], j, send_d2d.at[i, j], recv_d2d.at[i, j], True).wait_recv()
        for cp in sends:
            cp.wait_send()

    return pl.pallas_call(
        body, name=name, in_specs=[HBM_SPEC] * n, out_specs=[HBM_SPEC] * n,
        out_shape=[_sds(b.shape, b.dtype) for b in bufs],
        scratch_shapes=[DMA_SEM((n, 3))] * 4, input_output_aliases={i: i for i in range(n)},
    )(*bufs)


def _gather_start(name, groups):
    flat = [b for g in groups for b in g]
    n, ng = len(flat), len(groups)

    def body(*refs):
        ins, sems, token = refs[:n], refs[n:n + 2 * ng], refs[-1]
        k = 0
        for gi, g in enumerate(groups):
            for a in range(len(g)):
                for j in range(3):
                    _ici_copy(ins[k], j, sems[2 * gi], sems[2 * gi + 1], False).start()
                k += 1
        token[...] = jnp.zeros_like(token)

    res = pl.pallas_call(
        body, name=name, in_specs=[HBM_SPEC] * n,
        out_specs=[SEM_SPEC] * (2 * ng) + [HBM_SPEC] * n + [pl.BlockSpec(memory_space=pltpu.VMEM)],
        out_shape=[DMA_SEM(()) for g in groups for _ in range(2)]
        + [pltpu.HBM(b.shape, b.dtype) for b in flat] + [_sds((8, LANES), F32)],
        input_output_aliases={i: 2 * ng + i for i in range(n)},
        compiler_params=pltpu.CompilerParams(has_side_effects=DATAFLOW),
    )(*[_hbm(b) for b in flat])
    sems = [(res[2 * gi], res[2 * gi + 1]) for gi in range(ng)]
    thru, k = [], 2 * ng
    for g in groups:
        thru.append(list(res[k:k + len(g)]))
        k += len(g)
    return sems, thru, res[-1]


def _gather_wait(name, bufs, sems, after):
    n = len(bufs)

    def body(*refs):
        ins, send, recv = refs[:n], refs[n], refs[n + 1]
        for a in range(n):
            for j in range(3):
                _ici_copy(ins[a], j, send, recv, False).wait_send()
                _ici_copy(ins[a], j, send, recv, True).wait_recv()

    return pl.pallas_call(
        body, name=name, in_specs=[HBM_SPEC] * n + [SEM_SPEC, SEM_SPEC, pl.BlockSpec(memory_space=pl.ANY)],
        out_specs=[HBM_SPEC] * n, out_shape=[pltpu.HBM(b.shape, b.dtype) for b in bufs],
        input_output_aliases={i: i for i in range(n)},
        compiler_params=pltpu.CompilerParams(has_side_effects=DATAFLOW),
    )(*bufs, sems[0], sems[1], after)


def _send_sibling_halves(name, arrs):
    n = len(arrs)

    def body(*refs):
        ins, outs = refs[:n], refs[n:2 * n]
        send, recv = refs[2 * n:]
        x, y, c = _mesh_pos()
        cps = []
        for i in range(n):
            cp = pltpu.make_async_remote_copy(
                src_ref=ins[i].at[:, 1 - c], dst_ref=outs[i],
                send_sem=send.at[i], recv_sem=recv.at[i], device_id=(x, y, 1 - c), device_id_type=MESH_ID)
            cp.start()
            cps.append(cp)
        for cp in cps:
            cp.wait()

    return pl.pallas_call(
        body, name=name, in_specs=[HBM_SPEC] * n, out_specs=[HBM_SPEC] * n,
        out_shape=[_sds((a.shape[0],) + a.shape[2:], a.dtype) for a in arrs],
        scratch_shapes=[DMA_SEM((n,)), DMA_SEM((n,))],
    )(*arrs)


def _chip_copy(p, land, j, send, recv):
    x, y, c = _mesh_pos()
    px, py = _other_chips(x, y)[j]
    return pltpu.make_async_remote_copy(src_ref=p.at[2 * px + py], dst_ref=land.at[j], send_sem=send, recv_sem=recv,
                                        device_id=(px, py, c), device_id_type=MESH_ID)


def _chip_send_start(name, ps):
    n = len(ps)
    lands = [lax.empty((3,) + p.shape[1:], p.dtype) for p in ps]

    def body(*refs):
        ins, lnd, send, recv, token = refs[:n], refs[n:2 * n], refs[2 * n], refs[2 * n + 1], refs[-1]
        for i in range(n):
            for j in range(3):
                _chip_copy(ins[i], lnd[i], j, send, recv).start()
        token[...] = jnp.zeros_like(token)

    res = pl.pallas_call(
        body, name=name, in_specs=[HBM_SPEC] * (2 * n),
        out_specs=[SEM_SPEC, SEM_SPEC] + [HBM_SPEC] * (2 * n) + [pl.BlockSpec(memory_space=pltpu.VMEM)],
        out_shape=[DMA_SEM(()), DMA_SEM(())] + [pltpu.HBM(a.shape, a.dtype) for a in ps + lands]
        + [_sds((8, LANES), F32)],
        input_output_aliases={i: 2 + i for i in range(2 * n)},
        compiler_params=pltpu.CompilerParams(has_side_effects=DATAFLOW),
    )(*[_hbm(a) for a in ps + lands])
    return (res[0], res[1]), list(res[2:2 + n]), list(res[2 + n:2 + 2 * n]), res[-1]


def _chip_send_wait(name, ps, lands, sems, after):
    n = len(ps)

    def body(*refs):
        ins, lnd, send, recv = refs[:n], refs[n:2 * n], refs[2 * n], refs[2 * n + 1]
        for i in range(n):
            for j in range(3):
                cp = _chip_copy(ins[i], lnd[i], j, send, recv)
                cp.wait_send()
                cp.wait_recv()

    res = pl.pallas_call(
        body, name=name, in_specs=[HBM_SPEC] * (2 * n) + [SEM_SPEC, SEM_SPEC, pl.BlockSpec(memory_space=pl.ANY)],
        out_specs=[HBM_SPEC] * (2 * n), out_shape=[pltpu.HBM(a.shape, a.dtype) for a in ps + lands],
        input_output_aliases={i: i for i in range(2 * n)},
        compiler_params=pltpu.CompilerParams(has_side_effects=DATAFLOW),
    )(*ps, *lands, sems[0], sems[1], after)
    return list(res[:n]), list(res[n:])


def _join_halves(arrs):
    n = len(arrs)

    def body(*refs):
        bufs = refs[n:2 * n]
        send, recv = refs[2 * n:]
        x, y, c = _mesh_pos()
        cps = []
        for i in range(n):
            mine = bufs[i].at[:, c]
            cp = pltpu.make_async_remote_copy(
                src_ref=mine, dst_ref=mine, send_sem=send.at[i], recv_sem=recv.at[i],
                device_id=(x, y, 1 - c), device_id_type=MESH_ID)
            cp.start()
            cps.append(cp)
        for i, cp in enumerate(cps):
            theirs = bufs[i].at[:, 1 - c]
            cp.wait_send()
            pltpu.make_async_remote_copy(
                src_ref=theirs, dst_ref=theirs, send_sem=send.at[i], recv_sem=recv.at[i],
                device_id=(x, y, 1 - c), device_id_type=MESH_ID).wait_recv()

    return pl.pallas_call(
        body, name="join_halves", in_specs=[HBM_SPEC] * n, out_specs=[HBM_SPEC] * n,
        out_shape=[_sds(a.shape, a.dtype) for a in arrs],
        scratch_shapes=[DMA_SEM((n,)), DMA_SEM((n,))], input_output_aliases={i: i for i in range(n)},
    )(*arrs)


def _peer_copy(buf, land, k, send, recv, landing):
    x, y, c = _mesh_pos()
    px, py, pc = x ^ ((k >> 2) & 1), y ^ ((k >> 1) & 1), c ^ (k & 1)
    slot = 4 * px + 2 * py + pc if landing else 4 * x + 2 * y + c
    return pltpu.make_async_remote_copy(src_ref=buf, dst_ref=land.at[slot], send_sem=send, recv_sem=recv,
                                        device_id=(px, py, pc), device_id_type=MESH_ID)


def _exchange_all(buf):
    def body(in_ref, out_ref, send, recv):
        cps = [_peer_copy(in_ref, out_ref, k, send.at[k - 1], recv.at[k - 1], False) for k in range(1, 8)]
        for cp in cps:
            cp.start()
        for k in range(1, 8):
            cps[k - 1].wait_send()
            _peer_copy(in_ref, out_ref, k, send.at[k - 1], recv.at[k - 1], True).wait_recv()

    return pl.pallas_call(
        body, name="exchange_all", in_specs=[HBM_SPEC], out_specs=HBM_SPEC,
        out_shape=_sds((8,) + buf.shape, buf.dtype), scratch_shapes=[DMA_SEM((7,)), DMA_SEM((7,))],
    )(buf)


def _exchange_start(name, buf):
    land = lax.empty((8,) + buf.shape, buf.dtype)

    def body(in_ref, land_ref, send, recv, in_thru, land_thru, token):
        for k in range(1, 8):
            _peer_copy(in_ref, land_ref, k, send, recv, False).start()
        token[...] = jnp.zeros_like(token)

    res = pl.pallas_call(
        body, name=name, in_specs=[HBM_SPEC] * 2,
        out_specs=[SEM_SPEC, SEM_SPEC, HBM_SPEC, HBM_SPEC, pl.BlockSpec(memory_space=pltpu.VMEM)],
        out_shape=[DMA_SEM(()), DMA_SEM(()), pltpu.HBM(buf.shape, buf.dtype), pltpu.HBM(land.shape, land.dtype),
                   _sds((8, LANES), F32)],
        input_output_aliases={0: 2, 1: 3}, compiler_params=pltpu.CompilerParams(has_side_effects=DATAFLOW),
    )(_hbm(buf), _hbm(land))
    return (res[0], res[1]), res[2], res[3], res[4]


def _exchange_wait(name, buf, land, sems, after):
    def body(in_ref, land_ref, send, recv, after_ref, in_thru, land_thru):
        for k in range(1, 8):
            _peer_copy(in_ref, land_ref, k, send, recv, False).wait_send()
            _peer_copy(in_ref, land_ref, k, send, recv, True).wait_recv()

    res = pl.pallas_call(
        body, name=name, in_specs=[HBM_SPEC, HBM_SPEC, SEM_SPEC, SEM_SPEC, pl.BlockSpec(memory_space=pl.ANY)],
        out_specs=[HBM_SPEC, HBM_SPEC], out_shape=[pltpu.HBM(buf.shape, buf.dtype), pltpu.HBM(land.shape, land.dtype)],
        input_output_aliases={0: 0, 1: 1}, compiler_params=pltpu.CompilerParams(has_side_effects=DATAFLOW),
    )(buf, land, sems[0], sems[1], after)
    return res[0], res[1]


def _pad_to(a, axis, size):
    pad = [(0, 0)] * a.ndim
    pad[axis] = (0, size - a.shape[axis])
    return jnp.pad(a, pad)


def _strips(w):
    k, d = w.shape
    return _pad_to(w, 0, 32).reshape(32, d // LANES, LANES).transpose(1, 0, 2)


def kernel(x, norm_mix, w_in, gate_bias, conv_w, conv_b, conv_ln_g, conv_ln_b, w_conv_out, sgu_ln_g, sgu_ln_b, w_spatial, b_spatial, w_sgu_out, w_o, norm_ffn, w_ffn_gate, w_ffn_up, w_ffn_down, norm_final, loss_target, m_norm_mix, m_w_in, m_gate_bias, m_conv_w, m_conv_b, m_conv_ln_g, m_conv_ln_b, m_w_conv_out, m_sgu_ln_g, m_sgu_ln_b, m_w_spatial, m_b_spatial, m_w_sgu_out, m_w_o, m_norm_ffn, m_w_ffn_gate, m_w_ffn_up, m_w_ffn_down, m_norm_final, v_norm_mix, v_w_in, v_gate_bias, v_conv_w, v_conv_b, v_conv_ln_g, v_conv_ln_b, v_w_conv_out, v_sgu_ln_g, v_sgu_ln_b, v_w_spatial, v_b_spatial, v_w_sgu_out, v_w_o, v_norm_ffn, v_w_ffn_gate, v_w_ffn_up, v_w_ffn_down, v_norm_final):
    BL, S, D = x.shape
    T = BL * S
    L = w_in.shape[0]
    CS = w_in.shape[2]
    CN = CS // 3
    DQ = D // NSHARD
    FS = w_ffn_gate.shape[2]
    F = NSHARD * FS
    G, CH = w_spatial.shape[1], w_spatial.shape[2]
    KW = conv_w.shape[1]
    CQ = conv_w.shape[3]
    NSTR = D // LANES
    tm = min(512, S // 2)
    tm2 = max(tm // 2, CH)
    rb = min(64, tm)
    mx, my, mc = _mesh_pos()
    pos = jnp.stack([mc, 2 * mx + my]).astype(jnp.int32)

    def placed(name, w, dtype=BF16, dep=None, layer=None, **kw):
        return _place_shard("place_" + name, w, pos, dtype, _row_tile(w.shape[-2], 256), dep, layer, **kw)

    w_in0 = placed("w_in", w_in, layer=0)
    cw_p = placed("conv_w", _pad_to(conv_w.reshape(L, KW, CQ), 1, 32).reshape(L * 32, CQ), F32)
    fsems, fflying, ftoken = _gather_start("gather_start_first", [[w_in0, cw_p]])
    wts = []
    for l in range(L):
        w_sq = None
        for i, w in enumerate([w_conv_out, w_sgu_out, w_o]):
            w_sq = placed("w_sq", w, dep=ftoken, layer=l, into=w_sq, row_off=i * DQ, out_rows=3 * DQ)
        wts.append(dict(w_in=placed("w_in", w_in, dep=ftoken, layer=l) if l else None, w_sq=w_sq,
                        wg=placed("w_gate", w_ffn_gate[l].T, dep=ftoken), wu=placed("w_up", w_ffn_up[l].T, dep=ftoken),
                        wd=placed("w_down", w_ffn_down, dep=ftoken, layer=l)))
    ffn_keys = ["wg", "wu", "wd"]
    order = [[(0, "w_sq")], [(0, k) for k in ffn_keys]]
    order += [[(l, k) for k in ["w_in", "w_sq"] + ffn_keys] for l in range(1, L)]
    gsems, flying, token = _gather_start("gather_start", [[wts[l][k] for l, k in grp] for grp in order])
    first = _gather_wait("gather_wait_first", fflying[0], fsems[0], token)
    wts[0]["w_in"], cw_g = _forward_sibling("gather_first", first, False)
    conv_w_full = cw_g.reshape(NSHARD, L, 32, CQ).transpose(1, 2, 0, 3).reshape(L, 32, D)[:, :KW]

    def land(gi, after):
        bufs = _gather_wait("gather_wait_%d" % gi, flying[gi], gsems[gi], after)
        bufs = _forward_sibling("gather_forward_%d" % gi, bufs, False)
        for (l, k), b in zip(order[gi], bufs):
            wts[l][k] = b

    x2d = x.reshape(T, D)
    tgt = loss_target.reshape(T, D)
    row = lambda a, l: a[l].reshape(1, -1)

    saved = []
    xc = x2d
    for l in range(L):
        ws_b = w_spatial[l].astype(BF16)
        bs_b = jnp.repeat(b_spatial[l].T, D // G, axis=1)
        cw_s = _strips(conv_w_full[l])
        h, proj = _mix_in_fwd(xc, row(norm_mix, l), wts[l]["w_in"], 0, tm, token if l == 0 else None)
        if l == 0:
            land(0, h)
        c1h, rstd_c, c3, ya = _conv_fwd(proj, cw_s, row(conv_b, l), row(conv_ln_g, l), row(conv_ln_b, l),
                                        wts[l]["w_sq"], 0, S, tm, rb)
        if l == 0:
            land(1, ya)
        ffn_w = [wts[l][k].reshape(F, D) for k in ffn_keys]
        mixed, gated, yb, merged, x1, h2, gt, up, act, x2 = _sgu_ffn_fwd(
            proj, ya, xc, row(sgu_ln_g, l), row(sgu_ln_b, l), ws_b, bs_b, row(gate_bias, l), wts[l]["w_sq"], 0,
            row(norm_ffn, l), *ffn_w, tm2)
        if l + 1 < L:
            land(l + 2, x2)
        saved.append(dict(x=xc, h=h, proj=proj, c1h=c1h, rstd_c=rstd_c, c3=c3, ya=ya, mixed=mixed, gated=gated,
                          yb=yb, merged=merged, x1=x1, h2=h2, gt=gt, up=up, act=act, ws_b=ws_b, cw=conv_w_full[l]))
        xc = x2

    dx, loss_part, d_norm_final = _loss_head(xc, norm_final.reshape(1, D), tgt, tm)
    loss = lax.psum(loss_part[0, 0], ("x", "y", "c"))

    g_acc = {}

    def reduce_start(tag, layer, named):
        arrs = [g.reshape(NSHARD, 2, g.shape[1] // 2, g.shape[2]) for _, g in named]
        from_sib = _send_sibling_halves("sibling_" + tag, arrs)
        ps = [_add_halves("presum_" + nm, a, r, pos, _row_tile(a.shape[2], 256))
              for (nm, _), a, r in zip(named, arrs, from_sib)]
        sems, ps, lands, tok = _chip_send_start("chip_send_start_" + tag, ps)
        return dict(tag=tag, layer=layer, names=[nm for nm, _ in named], ps=ps, lands=lands, sems=sems), tok

    def reduce_finish(pend, after):
        ps, lands = _chip_send_wait("chip_send_wait_" + pend["tag"], pend["ps"], pend["lands"], pend["sems"], after)
        for nm, p, r in zip(pend["names"], ps, lands):
            g_acc[nm] = _add_shards("shardsum_" + nm, p, r, pos, _row_tile(p.shape[1], 256), pend["layer"], L,
                                    g_acc.get(nm))

    me_idx = (4 * mx + 2 * my + mc).astype(jnp.int32).reshape(1)
    exchanges = []

    def pack_rows(pieces):
        packed = jnp.concatenate(pieces, axis=0)
        return _pad_to(packed, 0, -(-packed.shape[0] // 8) * 8)

    def unpack_rows(summed, pieces):
        out, off = [], 0
        for p in pieces:
            out.append(summed[off:off + p.shape[0]])
            off += p.shape[0]
        return out

    def small_start(tag, pieces):
        sems, buf, land, token = _exchange_start("exchange_start_" + tag, pack_rows(pieces))
        return dict(tag=tag, pieces=pieces, buf=buf, land=land, sems=sems, token=token)

    def small_finish(st, after):
        buf, land = _exchange_wait("exchange_wait_" + st["tag"], st["buf"], st["land"], st["sems"], after)
        return unpack_rows(_sum_slots(buf, land, me_idx, _row_tile(buf.shape[0], 256)), st["pieces"])

    small = [None] * L
    tt = min(2048, T // 2)
    nt = T // tt
    pending, tok = None, None
    for l in reversed(range(L)):
        sv, wt = saved[l], wts[l]
        ffn_w = [wt[k].reshape(F, D) for k in ffn_keys]
        dx1, dgt, dup, d_norm_ffn = _ffn_bwd(dx, sv["x1"], sv["gt"], sv["up"], row(norm_ffn, l), *ffn_w, tm2, tok)
        tn_a = ((tt, F // 2), lambda j, t: (t, j))
        tn_b = ((tt, D), lambda j, t: (t, 0))
        tn_o = ((F, D), (F // 2, D), lambda j, t: (j, 0), 2, nt)
        g_g, = _tn_matmul("grad_w_gate", dgt, *tn_a, [sv["h2"]], *tn_b, *tn_o)
        g_u, = _tn_matmul("grad_w_up", dup, *tn_a, [sv["h2"]], *tn_b, *tn_o)
        g_d, = _tn_matmul("grad_w_down", sv["act"], *tn_a, [dx], *tn_b, *tn_o)
        if pending is not None:
            reduce_finish(pending, g_d)
        ffn_pend, tok = reduce_start("ffn%d" % l, l, [
            ("w_ffn_gate", g_g.reshape(NSHARD, FS, D)), ("w_ffn_up", g_u.reshape(NSHARD, FS, D)),
            ("w_ffn_down", g_d.reshape(NSHARD, FS, D))])
        wst_b = jnp.swapaxes(sv["ws_b"], 1, 2)
        dya, dyb, dp3, dc1, d_gate_bias, d_sgu_g, d_sgu_b, d_bs, d_ws, d_cln_g, d_cln_b, d_conv_b = _merge_sgu_bwd(
            dx1, sv["proj"], sv["ya"], sv["yb"], sv["mixed"], sv["c1h"], sv["rstd_c"], row(sgu_ln_g, l),
            row(sgu_ln_b, l), sv["ws_b"], wst_b, row(gate_bias, l), row(conv_ln_g, l), row(conv_ln_b, l), wt["w_sq"],
            0, tm2, tok)
        sq_args = ((tt, D), lambda j, t: (t, 0))
        sq_out = ((D, D), (D, D), lambda j, t: (0, 0), 1, nt)
        g_o, = _tn_matmul("grad_w_o", sv["merged"], *sq_args, [dx1], *sq_args, *sq_out)
        g_so, = _tn_matmul("grad_w_sgu_out", sv["gated"], *sq_args, [dyb], *sq_args, *sq_out)
        g_co, = _tn_matmul("grad_w_conv_out", sv["c3"], *sq_args, [dya], *sq_args, *sq_out)
        small[l] = [None, d_gate_bias.reshape(2, D), None, d_conv_b, d_cln_g, d_cln_b, d_sgu_g, d_sgu_b,
                    d_ws.reshape(G * CH * CH // D, D), d_bs.reshape(G * CH // D, D), d_norm_ffn]
        tok_x = None
        if l == 0:
            early = [k for k in range(len(small[0])) if small[0][k] is not None]
            exchanges.append((small_start("early0", [small[0][k] for k in early]), [(0, k) for k in early]))
            tok_x = exchanges[-1][0]["token"]
        dp3, d_cw_s = _conv_bwd(dc1, sv["proj"], dp3, _strips(sv["cw"][::-1]), S, tm, rb, tok_x)
        g_in, = _tn_matmul("grad_w_in", sv["h"], (tt, D), lambda j, t: (t, 0), [dp3], (tt, CS), lambda j, t: (t, j),
                           (NSHARD, D, CS), (None, D, CS), lambda j, t: (j, 0, 0), NSHARD, nt)
        reduce_finish(ffn_pend, g_in)
        pending, tok = reduce_start("mix%d" % l, l, [
            ("w_in", g_in), ("w_conv_out", g_co.reshape(NSHARD, DQ, D)), ("w_sgu_out", g_so.reshape(NSHARD, DQ, D)),
            ("w_o", g_o.reshape(NSHARD, DQ, D))])
        dx, d_norm_mix = _mix_in_bwd(dx1, dp3, sv["x"], row(norm_mix, l), wt["w_in"], 0, tm, tok)
        small[l][0] = d_norm_mix
        small[l][2] = d_cw_s.transpose(1, 0, 2).reshape(32, D)
        if l > 0:
            exchanges.append((small_start("layer%d" % l, small[l]), [(l, k) for k in range(len(small[l]))]))
            tok = [tok, exchanges[-1][0]["token"]]
    reduce_finish(pending, dx)
    grad_x = dx.reshape(BL, S, D)

    names = ["w_in", "w_conv_out", "w_sgu_out", "w_o", "w_ffn_gate", "w_ffn_up", "w_ffn_down"]
    g_full = _join_halves([g_acc[nm] for nm in names])
    g_w_in, g_w_co, g_w_so, g_w_o, g_w_g, g_w_u, g_w_d = [g.reshape(L, 2 * g.shape[2], g.shape[3]) for g in g_full]
    g_w_g = jnp.swapaxes(g_w_g, 1, 2)
    g_w_u = jnp.swapaxes(g_w_u, 1, 2)

    late = [small[0][0], small[0][2], d_norm_final]
    packed = pack_rows(late)
    summed = _sum_slots(packed, _exchange_all(packed), me_idx, _row_tile(packed.shape[0], 256))
    sg = [[None] * len(small[l]) for l in range(L)]
    sg[0][0], sg[0][2], g_norm_final = unpack_rows(summed, late)
    g_norm_final = g_norm_final[0]
    for st, where in exchanges:
        for (l, k), piece in zip(where, small_finish(st, summed)):
            sg[l][k] = piece

    def per_layer(k, shape):
        return jnp.stack([sg[l][k] for l in range(L)]).reshape(shape)

    g_norm_mix = per_layer(0, (L, D))
    g_gate_bias = per_layer(1, (L, 2 * D))
    g_conv_w_full = jnp.stack([sg[l][2][:KW] for l in range(L)])
    g_conv_w = lax.dynamic_slice_in_dim(g_conv_w_full, (2 * mx + my) * CQ, CQ, axis=2).reshape(L, KW, 1, CQ)
    g_conv_b = per_layer(3, (L, D))
    g_conv_ln_g = per_layer(4, (L, D))
    g_conv_ln_b = per_layer(5, (L, D))
    g_sgu_ln_g = per_layer(6, (L, D))
    g_sgu_ln_b = per_layer(7, (L, D))
    g_w_spatial = per_layer(8, (L, G, CH, CH))
    g_b_spatial = per_layer(9, (L, G, CH))
    g_norm_ffn = per_layer(10, (L, D))

    grads = [g_norm_mix, g_w_in, g_gate_bias, g_conv_w, g_conv_b, g_conv_ln_g, g_conv_ln_b, g_w_co, g_sgu_ln_g,
             g_sgu_ln_b, g_w_spatial, g_b_spatial, g_w_so, g_w_o, g_norm_ffn, g_w_g, g_w_u, g_w_d, g_norm_final]
    weights = [norm_mix, w_in, gate_bias, conv_w, conv_b, conv_ln_g, conv_ln_b, w_conv_out, sgu_ln_g, sgu_ln_b,
               w_spatial, b_spatial, w_sgu_out, w_o, norm_ffn, w_ffn_gate, w_ffn_up, w_ffn_down, norm_final]
    ms = [m_norm_mix, m_w_in, m_gate_bias, m_conv_w, m_conv_b, m_conv_ln_g, m_conv_ln_b, m_w_conv_out, m_sgu_ln_g,
          m_sgu_ln_b, m_w_spatial, m_b_spatial, m_w_sgu_out, m_w_o, m_norm_ffn, m_w_ffn_gate, m_w_ffn_up,
          m_w_ffn_down, m_norm_final]
    vs = [v_norm_mix, v_w_in, v_gate_bias, v_conv_w, v_conv_b, v_conv_ln_g, v_conv_ln_b, v_w_conv_out, v_sgu_ln_g,
          v_sgu_ln_b, v_w_spatial, v_b_spatial, v_w_sgu_out, v_w_o, v_norm_ffn, v_w_ffn_gate, v_w_ffn_up,
          v_w_ffn_down, v_norm_final]

    big_idx = [1, 7, 12, 13, 15, 16, 17]
    transposed = [15, 16]
    deltas, new_m, new_v = [None] * 19, [None] * 19, [None] * 19
    for k in big_idx:
        shp = weights[k].shape
        r2 = (shp[0] * shp[1], shp[2])
        res = _adamw("adamw_" + str(k), weights[k].reshape(r2), grads[k].reshape(r2), ms[k].reshape(r2),
                     vs[k].reshape(r2), _row_tile(r2[0], 256), k not in transposed)
        deltas[k], new_m[k], new_v[k] = [a.reshape(shp) for a in res[:3]]
        if k not in transposed:
            grads[k] = res[3].reshape(shp)
    small_idx = [k for k in range(19) if k not in big_idx]
    pick = lambda arrs: [arrs[k].reshape(1, -1) if arrs[k].ndim == 1 else arrs[k] for k in small_idx]
    d_, m_, v_ = _adamw_many(pick(weights), pick(grads), pick(ms), pick(vs))
    for i, k in enumerate(small_idx):
        shp = weights[k].shape
        deltas[k], new_m[k], new_v[k] = d_[i].reshape(shp), m_[i].reshape(shp), v_[i].reshape(shp)

    return (loss, grad_x, *grads, *deltas, *new_m, *new_v)
```

```python
import jax
import jax.numpy as jnp
from jax import lax
from jax.experimental import pallas as pl
from jax.experimental.pallas import tpu as pltpu

F32 = jnp.float32
BF16 = jnp.bfloat16
EPS = 1e-6
ADAM_LR = 0.001
ADAM_B1 = 0.9
ADAM_B2 = 0.999
ADAM_EPS = 1e-08
ADAM_WD = 0.01
ADAM_STEP = 10

NSHARD = 4
LANES = 128
HALO = 16
VMEM_LIMIT = 60 * 1024 * 1024
MESH_ID = pl.DeviceIdType.MESH


def _dot(a, b):
    return jnp.dot(a, b, preferred_element_type=F32)


def _dot_nt(a, b):
    return lax.dot_general(a, b, (((1,), (1,)), ((), ())), preferred_element_type=F32)


def _dot_tn(a, b):
    return lax.dot_general(a, b, (((0,), (0,)), ((), ())), preferred_element_type=F32)


def _sig(z):
    return 1.0 / (1.0 + jnp.exp(-z))


def _res(shape, imap=None):
    nd = len(shape)
    if imap is None:
        imap = lambda *_: (0,) * nd
    return pl.BlockSpec(shape, imap, pipeline_mode=pl.Buffered(1))


def _cparams(sem):
    return pltpu.CompilerParams(dimension_semantics=sem, vmem_limit_bytes=VMEM_LIMIT)


def _sds(shape, dtype):
    return jax.ShapeDtypeStruct(shape, dtype)


def _after(body, n_in, dep):
    deps = [] if dep is None else [d for d in (dep if isinstance(dep, (list, tuple)) else [dep]) if d is not None]
    if not deps:
        return body, [], []

    def wrapped(*refs):
        return body(*refs[:n_in], *refs[n_in + len(deps):])

    return wrapped, [pl.BlockSpec(memory_space=pl.ANY)] * len(deps), deps


def _mix_in_fwd(x2d, g_mix, w_in_g, layer, tm, dep=None):
    T, D = x2d.shape
    CS = w_in_g.shape[2]
    CN = CS // 3

    def body(x_ref, g_ref, w_ref, h_ref, p_ref):
        x = x_ref[...]
        rstd = lax.rsqrt(jnp.mean(x * x, axis=-1, keepdims=True) + EPS)
        h = (x * rstd * g_ref[...]).astype(BF16)
        h_ref[...] = h
        for s in range(NSHARD):
            for j in range(3):
                c0 = s * CS + j * CN
                p_ref[:, c0:c0 + CN] = _dot(h, w_ref[s, :, j * CN:(j + 1) * CN]).astype(BF16)

    body, dep_spec, dep_arg = _after(body, 3, dep)
    return pl.pallas_call(
        body, name="mix_in_fwd", grid=(T // tm,),
        in_specs=[pl.BlockSpec((tm, D), lambda i: (i, 0)), _res((1, D)),
                  _res((NSHARD, D, CS), lambda i: (0, layer, 0))] + dep_spec,
        out_specs=[pl.BlockSpec((tm, D), lambda i: (i, 0)), pl.BlockSpec((tm, NSHARD * CS), lambda i: (i, 0))],
        out_shape=[_sds((T, D), BF16), _sds((T, NSHARD * CS), BF16)],
        compiler_params=_cparams(("parallel",)),
    )(x2d, g_mix, w_in_g, *dep_arg)


def _halo_maps(tm, n_rows):
    nb = tm // HALO
    last = n_rows // HALO - 1
    prev = lambda i: (jnp.maximum(i * nb - 1, 0), 0)
    nxt = lambda i: (jnp.minimum((i + 1) * nb, last), 0)
    return prev, nxt


def _dwconv(pad_ref, w_ref, out_ref, n_strips, tm, kw, rb):
    off = HALO - (kw - 1) // 2

    def strip(cs, carry):
        for r0 in range(0, tm, rb):
            acc = jnp.zeros((rb, LANES), F32)
            for k in range(kw):
                r = r0 + off + k
                acc = acc + w_ref[cs, k:k + 1, :] * pad_ref[cs, r:r + rb, :]
            out_ref[cs, r0:r0 + rb, :] = acc
        return carry

    lax.fori_loop(0, n_strips, strip, 0)


def _fill_c0_pad(pad_ref, pa_ref, pprev_ref, pnext_ref, D, tm, first, last):
    for cs in range(D // LANES):
        lo, hi = cs * LANES, (cs + 1) * LANES

        def c0_of(ref):
            return ref[:, lo:hi].astype(F32) * _sig(ref[:, D + lo:D + hi].astype(F32))

        pad_ref[cs, HALO:HALO + tm, :] = c0_of(pa_ref)
        pad_ref[cs, 0:HALO, :] = jnp.where(first, 0.0, c0_of(pprev_ref))
        pad_ref[cs, HALO + tm:HALO + tm + HALO, :] = jnp.where(last, 0.0, c0_of(pnext_ref))


def _conv_fwd(proj, conv_w_s, conv_b, ln_g, ln_b, w_sq_g, layer, seq, tm, rb):
    T = proj.shape[0]
    D = conv_b.shape[1]
    DQ = D // NSHARD
    NSTR = D // LANES
    KW = 31
    tps = seq // tm
    prev, nxt = _halo_maps(tm, T)

    def body(pa_ref, pprev_ref, pnext_ref, w_ref, b_ref, g_ref, be_ref, wco_ref,
             c1h_ref, rstd_ref, c3_ref, ya_ref, pad_ref, c1s_ref):
        i = pl.program_id(0)
        first = (i % tps) == 0
        last = (i % tps) == tps - 1
        _fill_c0_pad(pad_ref, pa_ref, pprev_ref, pnext_ref, D, tm, first, last)
        _dwconv(pad_ref, w_ref, c1s_ref, NSTR, tm, KW, rb)
        wco = wco_ref[...].reshape(D, D)
        for r0 in (0, tm // 2):
            rows = slice(r0, r0 + tm // 2)
            c1 = jnp.concatenate([c1s_ref[cs, rows, :] for cs in range(NSTR)], axis=1) + b_ref[...]
            mu = jnp.mean(c1, axis=-1, keepdims=True)
            cc = c1 - mu
            rstd = lax.rsqrt(jnp.mean(cc * cc, axis=-1, keepdims=True) + EPS)
            c1h = cc * rstd
            c1h_ref[rows, :] = c1h.astype(BF16)
            rstd_ref[rows, :] = rstd
            c2 = c1h * g_ref[...] + be_ref[...]
            c3 = (c2 * _sig(c2)).astype(BF16)
            c3_ref[rows, :] = c3
            ya_ref[rows, :] = _dot(c3, wco).astype(BF16)

    row = lambda i: (i, 0)
    return pl.pallas_call(
        body, name="conv_fwd", grid=(T // tm,),
        in_specs=[pl.BlockSpec((tm, 2 * D), row), pl.BlockSpec((HALO, 2 * D), prev), pl.BlockSpec((HALO, 2 * D), nxt),
                  _res((NSTR, 32, LANES)), _res((1, D)), _res((1, D)), _res((1, D)),
                  _res((NSHARD, DQ, D), lambda i: (0, layer * 3 + 0, 0))],
        out_specs=[pl.BlockSpec((tm, D), row), pl.BlockSpec((tm, 1), row), pl.BlockSpec((tm, D), row),
                   pl.BlockSpec((tm, D), row)],
        out_shape=[_sds((T, D), BF16), _sds((T, 1), F32), _sds((T, D), BF16), _sds((T, D), BF16)],
        scratch_shapes=[pltpu.VMEM((NSTR, tm + 2 * HALO, LANES), F32), pltpu.VMEM((NSTR, tm, LANES), F32)],
        compiler_params=_cparams(("parallel",)),
    )(proj, proj, proj, conv_w_s, conv_b, ln_g, ln_b, w_sq_g)


def _ffn_chunks(F):
    assert F % 256 == 0, F
    return [(c0, min(512, F - c0)) for c0 in range(0, F, 512)]


def _sgu_ffn_fwd(proj, ya, x2d, ln_g, ln_b, ws_b, bs_b, gate_bias, w_sq_g, layer, g_ffn, wgt, wut, wd, tm, head=None):
    T, D = x2d.shape
    DQ = D // NSHARD
    G, CH, _ = ws_b.shape
    GD = D // G
    F = wd.shape[0]
    n = T // tm
    n_in = 17 if head else 15

    def body(*refs):
        (puv_ref, pg_ref, ya_ref, x_ref, g_ref, be_ref, ws_ref, bsb_ref, gb_ref, wso_ref, wo_ref,
         gf_ref, wg_ref, wu_ref, wd_ref) = refs[:15]
        mixed_ref, gated_ref, yb_ref, merged_ref, x1_ref, h2_ref, gt_ref, up_ref, act_ref = refs[n_in:n_in + 9]
        mix_scr = refs[-2] if head else refs[-1]
        u = puv_ref[:, :D].astype(F32)
        v = puv_ref[:, D:].astype(F32)
        mu = jnp.mean(v, axis=-1, keepdims=True)
        vc = v - mu
        rstd = lax.rsqrt(jnp.mean(vc * vc, axis=-1, keepdims=True) + EPS)
        vn = (vc * rstd * g_ref[...] + be_ref[...]).astype(BF16)
        nch = tm // CH
        for g in range(G):
            cols = slice(g * GD, (g + 1) * GD)
            rhs = jnp.concatenate([vn[ch * CH:(ch + 1) * CH, cols] for ch in range(nch)], axis=1)
            res = _dot(ws_ref[g], rhs)
            for ch in range(nch):
                mix_scr[ch * CH:(ch + 1) * CH, cols] = res[:, ch * GD:(ch + 1) * GD] + bsb_ref[:, cols]
        mixed = mix_scr[...]
        mixed_ref[...] = mixed.astype(BF16)
        gated = (u * mixed).astype(BF16)
        gated_ref[...] = gated
        yb = _dot(gated, wso_ref[...].reshape(D, D))
        yb_ref[...] = yb.astype(BF16)
        sa = _sig(pg_ref[:, :D].astype(F32) + gb_ref[:, :D])
        sb = _sig(pg_ref[:, D:].astype(F32) + gb_ref[:, D:])
        merged = (sa * ya_ref[...].astype(F32) + sb * yb).astype(BF16)
        merged_ref[...] = merged
        x1 = x_ref[...] + _dot(merged, wo_ref[...].reshape(D, D))
        x1_ref[...] = x1

        rstd = lax.rsqrt(jnp.mean(x1 * x1, axis=-1, keepdims=True) + EPS)
        h2 = (x1 * rstd * gf_ref[...]).astype(BF16)
        h2_ref[...] = h2
        acc = x1
        chunks = _ffn_chunks(F)

        def gate_up(c0, cw):
            return _dot_nt(h2, wg_ref[c0:c0 + cw, :]), _dot_nt(h2, wu_ref[c0:c0 + cw, :])

        nxt = gate_up(*chunks[0])
        for ci, (c0, cw) in enumerate(chunks):
            gt, up = nxt
            if ci + 1 < len(chunks):
                nxt = gate_up(*chunks[ci + 1])
            gt_ref[:, c0:c0 + cw] = gt.astype(BF16)
            up_ref[:, c0:c0 + cw] = up.astype(BF16)
            act = (gt * _sig(gt) * up).astype(BF16)
            act_ref[:, c0:c0 + cw] = act
            acc = acc + _dot(act, wd_ref[c0:c0 + cw, :])
        if not head:
            refs[n_in + 9][...] = acc
            return

        gfin_ref, t_ref = refs[15:17]
        dx_ref, loss_ref, dgf_ref = refs[n_in + 9:n_in + 12]
        sq_ref = refs[-1]
        i = pl.program_id(0)

        @pl.when(i == 0)
        def _():
            sq_ref[...] = jnp.zeros_like(sq_ref)
            dgf_ref[...] = jnp.zeros_like(dgf_ref)

        gfin = gfin_ref[...]
        rstd2 = lax.rsqrt(jnp.mean(acc * acc, axis=-1, keepdims=True) + EPS)
        xh = acc * rstd2
        diff = xh * gfin - t_ref[...]
        sq_ref[...] += jnp.sum(diff * diff, axis=0, keepdims=True)
        dy = diff * (1.0 / D)
        dgf_ref[...] += jnp.sum(dy * xh, axis=0, keepdims=True)
        dxh = dy * gfin
        dx_ref[...] = rstd2 * (dxh - xh * jnp.mean(dxh * xh, axis=-1, keepdims=True))

        @pl.when(i == n - 1)
        def _():
            tot = jnp.sum(sq_ref[...], axis=-1, keepdims=True) * (0.5 / D)
            loss_ref[...] = jnp.broadcast_to(tot, loss_ref.shape)

    row = lambda i: (i, 0)
    fixed2 = lambda i: (0, 0)
    wide = pl.BlockSpec((tm, F), row)
    in_specs = [pl.BlockSpec((tm, 2 * D), lambda i: (i, 1)), pl.BlockSpec((tm, 2 * D), lambda i: (i, 2)),
                pl.BlockSpec((tm, D), row), pl.BlockSpec((tm, D), row),
                _res((1, D)), _res((1, D)), _res((G, CH, CH)), _res((CH, D)), _res((1, 2 * D)),
                _res((NSHARD, DQ, D), lambda i: (0, layer * 3 + 1, 0)),
                _res((NSHARD, DQ, D), lambda i: (0, layer * 3 + 2, 0)),
                _res((1, D)), _res((F, D)), _res((F, D)), _res((F, D))]
    out_specs = [pl.BlockSpec((tm, D), row)] * 6 + [wide, wide, wide, pl.BlockSpec((tm, D), row)]
    out_shape = ([_sds((T, D), BF16)] * 4 + [_sds((T, D), F32), _sds((T, D), BF16)] + [_sds((T, F), BF16)] * 3
                 + [_sds((T, D), F32)])
    scratch = [pltpu.VMEM((tm, D), F32)]
    args = [proj, proj, ya, x2d, ln_g, ln_b, ws_b, bs_b, gate_bias, w_sq_g, w_sq_g, g_ffn, wgt, wut, wd]
    if head:
        in_specs += [_res((1, D)), pl.BlockSpec((tm, D), row)]
        out_specs += [pl.BlockSpec((1, LANES), fixed2), pl.BlockSpec((1, D), fixed2)]
        out_shape += [_sds((1, LANES), F32), _sds((1, D), F32)]
        scratch += [pltpu.VMEM((1, D), F32)]
        args += list(head)
    return pl.pallas_call(
        body, name="sgu_ffn_fwd_head" if head else "sgu_ffn_fwd", grid=(n,),
        in_specs=in_specs, out_specs=out_specs, out_shape=out_shape, scratch_shapes=scratch,
        compiler_params=_cparams(("arbitrary",) if head else ("parallel",)),
    )(*args)


def _ffn_bwd(dx2, x1, gt, up, g_ffn, wgt, wut, wd, tm, dep=None):
    T, D = x1.shape
    F = wd.shape[0]

    def body(dx2_ref, x1_ref, gt_ref, up_ref, g_ref, wg_ref, wu_ref, wd_ref, dx1_ref, dgt_ref, dup_ref, dg_ref):
        i = pl.program_id(0)

        @pl.when(i == 0)
        def _():
            dg_ref[...] = jnp.zeros_like(dg_ref)

        dx2 = dx2_ref[...]
        dx2b = dx2.astype(BF16)
        dh2 = jnp.zeros((tm, D), F32)
        chunks = _ffn_chunks(F)
        dact_next = _dot_nt(dx2b, wd_ref[0:chunks[0][1], :])
        for ci, (c0, cw) in enumerate(chunks):
            dact = dact_next
            if ci + 1 < len(chunks):
                n0, nw = chunks[ci + 1]
                dact_next = _dot_nt(dx2b, wd_ref[n0:n0 + nw, :])
            g = gt_ref[:, c0:c0 + cw].astype(F32)
            u = up_ref[:, c0:c0 + cw].astype(F32)
            sg = _sig(g)
            dup = (dact * (g * sg)).astype(BF16)
            dgt = (dact * u * (sg * (1.0 + g * (1.0 - sg)))).astype(BF16)
            dgt_ref[:, c0:c0 + cw] = dgt
            dup_ref[:, c0:c0 + cw] = dup
            dh2 = dh2 + _dot(dgt, wg_ref[c0:c0 + cw, :]) + _dot(dup, wu_ref[c0:c0 + cw, :])
        x = x1_ref[...]
        rstd = lax.rsqrt(jnp.mean(x * x, axis=-1, keepdims=True) + EPS)
        xh = x * rstd
        dg_ref[...] += jnp.sum(dh2 * xh, axis=0, keepdims=True)
        dxh = dh2 * g_ref[...]
        dx1_ref[...] = dx2 + rstd * (dxh - xh * jnp.mean(dxh * xh, axis=-1, keepdims=True))

    row = lambda i: (i, 0)
    body, dep_spec, dep_arg = _after(body, 8, dep)
    return pl.pallas_call(
        body, name="ffn_bwd", grid=(T // tm,),
        in_specs=[pl.BlockSpec((tm, D), row), pl.BlockSpec((tm, D), row), pl.BlockSpec((tm, F), row),
                  pl.BlockSpec((tm, F), row), _res((1, D)), _res((F, D)), _res((F, D)), _res((F, D))] + dep_spec,
        out_specs=[pl.BlockSpec((tm, D), row), pl.BlockSpec((tm, F), row), pl.BlockSpec((tm, F), row),
                   pl.BlockSpec((1, D), lambda i: (0, 0))],
        out_shape=[_sds((T, D), F32), _sds((T, F), BF16), _sds((T, F), BF16), _sds((1, D), F32)],
        compiler_params=_cparams(("arbitrary",)),
    )(dx2, x1, gt, up, g_ffn, wgt, wut, wd, *dep_arg)


def _merge_sgu_bwd(dx1, proj, ya, yb, mixed, c1h, rstd_c, ln_g, ln_b, ws_b, wst_b, gate_bias, cln_g, cln_b, w_sq_g,
                   layer, tm, dep=None):
    T, D = dx1.shape
    DQ = D // NSHARD
    G, CH, _ = ws_b.shape
    GD = D // G

    def body(dx1_ref, puv_ref, pg_ref, ya_ref, yb_ref, mixed_ref, c1h_ref, rstdc_ref, g_ref, be_ref, ws_ref, wst_ref,
             gb_ref, cg_ref, cbe_ref, wco_ref, wso_ref, wo_ref,
             dya_ref, dyb_ref, dp_ref, dc1_ref, dgb_ref, dlg_ref, dlb_ref, dbs_ref, dws_ref, dcg_ref, dcbe_ref,
             dcb_ref, dvn_scr, dbs_scr):
        i = pl.program_id(0)

        @pl.when(i == 0)
        def _():
            for r in (dgb_ref, dlg_ref, dlb_ref, dws_ref, dbs_scr, dcg_ref, dcbe_ref, dcb_ref):
                r[...] = jnp.zeros_like(r)

        dmerged = _dot_nt(dx1_ref[...].astype(BF16), wo_ref[...].reshape(D, D))
        sa = _sig(pg_ref[:, :D].astype(F32) + gb_ref[:, :D])
        sb = _sig(pg_ref[:, D:].astype(F32) + gb_ref[:, D:])
        dya = (dmerged * sa).astype(BF16)
        dyb = (dmerged * sb).astype(BF16)
        dya_ref[...] = dya
        dyb_ref[...] = dyb
        dc3 = _dot_nt(dya, wco_ref[...].reshape(D, D))
        dgated = _dot_nt(dyb, wso_ref[...].reshape(D, D))

        c1h = c1h_ref[...].astype(F32)
        c2 = c1h * cg_ref[...] + cbe_ref[...]
        sg = _sig(c2)
        dc2 = dc3 * (sg * (1.0 + c2 * (1.0 - sg)))
        dcg_ref[...] += jnp.sum(dc2 * c1h, axis=0, keepdims=True)
        dcbe_ref[...] += jnp.sum(dc2, axis=0, keepdims=True)
        dch = dc2 * cg_ref[...]
        dc1 = rstdc_ref[...] * (dch - jnp.mean(dch, axis=-1, keepdims=True)
                                - c1h * jnp.mean(dch * c1h, axis=-1, keepdims=True))
        dc1_ref[...] = dc1
        dcb_ref[...] += jnp.sum(dc1, axis=0, keepdims=True)

        dga = dmerged * ya_ref[...].astype(F32) * (sa * (1.0 - sa))
        dgb = dmerged * yb_ref[...].astype(F32) * (sb * (1.0 - sb))
        dp_ref[:, 4 * D:5 * D] = dga.astype(BF16)
        dp_ref[:, 5 * D:6 * D] = dgb.astype(BF16)
        dgb_ref[:, :D] += jnp.sum(dga, axis=0, keepdims=True)
        dgb_ref[:, D:] += jnp.sum(dgb, axis=0, keepdims=True)

        u = puv_ref[:, :D].astype(F32)
        v = puv_ref[:, D:].astype(F32)
        dp_ref[:, 2 * D:3 * D] = (dgated * mixed_ref[...].astype(F32)).astype(BF16)
        dmixed = dgated * u
        mu = jnp.mean(v, axis=-1, keepdims=True)
        vc = v - mu
        rstd = lax.rsqrt(jnp.mean(vc * vc, axis=-1, keepdims=True) + EPS)
        vh = vc * rstd
        vn = (vh * g_ref[...] + be_ref[...]).astype(BF16)
        dmb = dmixed.astype(BF16)
        nch = tm // CH
        bs_part = dmixed[0:CH, :]
        for ch in range(1, nch):
            bs_part = bs_part + dmixed[ch * CH:(ch + 1) * CH, :]
        dbs_scr[...] += bs_part
        for g in range(G):
            cols = slice(g * GD, (g + 1) * GD)
            dm_g = jnp.concatenate([dmb[ch * CH:(ch + 1) * CH, cols] for ch in range(nch)], axis=1)
            vn_g = jnp.concatenate([vn[ch * CH:(ch + 1) * CH, cols] for ch in range(nch)], axis=1)
            dws_ref[g] += _dot_nt(dm_g, vn_g)
            dvn_g = _dot(wst_ref[g], dm_g)
            for ch in range(nch):
                dvn_scr[ch * CH:(ch + 1) * CH, cols] = dvn_g[:, ch * GD:(ch + 1) * GD]
        dvn = dvn_scr[...]
        dlg_ref[...] += jnp.sum(dvn * vh, axis=0, keepdims=True)
        dlb_ref[...] += jnp.sum(dvn, axis=0, keepdims=True)
        dxh = dvn * g_ref[...]
        dv = rstd * (dxh - jnp.mean(dxh, axis=-1, keepdims=True) - vh * jnp.mean(dxh * vh, axis=-1, keepdims=True))
        dp_ref[:, 3 * D:4 * D] = dv.astype(BF16)

        @pl.when(i == pl.num_programs(0) - 1)
        def _():
            for g in range(G):
                blk = dbs_scr[:, g * GD:(g + 1) * GD]
                if GD != CH:
                    blk = jnp.concatenate([blk, jnp.zeros((CH, CH - GD), F32)], axis=1)
                dbs_ref[:, g * CH:(g + 1) * CH] = jnp.sum(blk.T, axis=0, keepdims=True)

    row = lambda i: (i, 0)
    fixed2 = lambda i: (0, 0)
    body, dep_spec, dep_arg = _after(body, 18, dep)
    vec = pl.BlockSpec((1, D), fixed2)
    return pl.pallas_call(
        body, name="merge_sgu_bwd", grid=(T // tm,),
        in_specs=[pl.BlockSpec((tm, D), row), pl.BlockSpec((tm, 2 * D), lambda i: (i, 1)),
                  pl.BlockSpec((tm, 2 * D), lambda i: (i, 2)), pl.BlockSpec((tm, D), row), pl.BlockSpec((tm, D), row),
                  pl.BlockSpec((tm, D), row), pl.BlockSpec((tm, D), row), pl.BlockSpec((tm, 1), row),
                  _res((1, D)), _res((1, D)), _res((G, CH, CH)), _res((G, CH, CH)), _res((1, 2 * D)),
                  _res((1, D)), _res((1, D)),
                  _res((NSHARD, DQ, D), lambda i: (0, layer * 3 + 0, 0)),
                  _res((NSHARD, DQ, D), lambda i: (0, layer * 3 + 1, 0)),
                  _res((NSHARD, DQ, D), lambda i: (0, layer * 3 + 2, 0))] + dep_spec,
        out_specs=[pl.BlockSpec((tm, D), row), pl.BlockSpec((tm, D), row), pl.BlockSpec((tm, 6 * D), row),
                   pl.BlockSpec((tm, D), row), pl.BlockSpec((1, 2 * D), fixed2), vec, vec,
                   pl.BlockSpec((1, G * CH), fixed2), pl.BlockSpec((G, CH, CH), lambda i: (0, 0, 0)), vec, vec, vec],
        out_shape=[_sds((T, D), BF16), _sds((T, D), BF16), _sds((T, 6 * D), BF16), _sds((T, D), F32),
                   _sds((1, 2 * D), F32), _sds((1, D), F32), _sds((1, D), F32), _sds((1, G * CH), F32),
                   _sds((G, CH, CH), F32), _sds((1, D), F32), _sds((1, D), F32), _sds((1, D), F32)],
        scratch_shapes=[pltpu.VMEM((tm, D), F32), pltpu.VMEM((CH, D), F32)],
        compiler_params=_cparams(("arbitrary",)),
    )(dx1, proj, proj, ya, yb, mixed, c1h, rstd_c, ln_g, ln_b, ws_b, wst_b, gate_bias, cln_g, cln_b,
      w_sq_g, w_sq_g, w_sq_g, *dep_arg)


def _conv_bwd(dc1, proj, dp3, conv_wf_s, seq, tm, rb, dep=None):
    T, D = dc1.shape
    NSTR = D // LANES
    KW = 31
    PADK = (KW - 1) // 2
    tps = seq // tm
    prev, nxt = _halo_maps(tm, T)
    n = T // tm

    def body(dc_ref, dcprev_ref, dcnext_ref, pa_ref, pprev_ref, pnext_ref, wf_ref, dp_in_ref,
             dp_ref, dw_ref, pad_ref, dpad_ref, dc0_ref, dwacc_ref):
        del dp_in_ref
        i = pl.program_id(0)
        first = (i % tps) == 0
        last = (i % tps) == tps - 1

        @pl.when(i == 0)
        def _():
            dwacc_ref[...] = jnp.zeros_like(dwacc_ref)

        _fill_c0_pad(pad_ref, pa_ref, pprev_ref, pnext_ref, D, tm, first, last)
        for cs in range(NSTR):
            lo, hi = cs * LANES, (cs + 1) * LANES
            dpad_ref[cs, HALO:HALO + tm, :] = dc_ref[:, lo:hi]
            dpad_ref[cs, 0:HALO, :] = jnp.where(first, 0.0, dcprev_ref[:, lo:hi])
            dpad_ref[cs, HALO + tm:HALO + tm + HALO, :] = jnp.where(last, 0.0, dcnext_ref[:, lo:hi])
        _dwconv(dpad_ref, wf_ref, dc0_ref, NSTR, tm, KW, rb)

        def strip(cs, carry):
            for r0 in range(0, tm, rb):
                d = dpad_ref[cs, HALO + r0:HALO + r0 + rb, :]
                for k in range(KW):
                    r = r0 + HALO - PADK + k
                    prod = d * pad_ref[cs, r:r + rb, :]
                    dwacc_ref[cs, k * 8:(k + 1) * 8, :] += jnp.sum(prod.reshape(rb // 8, 8, LANES), axis=0)
            return carry

        lax.fori_loop(0, NSTR, strip, 0)

        for cs in range(NSTR):
            lo, hi = cs * LANES, (cs + 1) * LANES
            av = pa_ref[:, lo:hi].astype(F32)
            sg = _sig(pa_ref[:, D + lo:D + hi].astype(F32))
            dc0 = dc0_ref[cs]
            dp_ref[:, lo:hi] = (dc0 * sg).astype(BF16)
            dp_ref[:, D + lo:D + hi] = (dc0 * av * (sg * (1.0 - sg))).astype(BF16)

        @pl.when(i == n - 1)
        def _():
            for cs in range(NSTR):
                dw_ref[cs] = jnp.sum(dwacc_ref[cs].reshape(32, 8, LANES), axis=1)

    row = lambda i: (i, 0)
    body, dep_spec, dep_arg = _after(body, 8, dep)
    return pl.pallas_call(
        body, name="conv_bwd", grid=(n,),
        in_specs=[pl.BlockSpec((tm, D), row), pl.BlockSpec((HALO, D), prev), pl.BlockSpec((HALO, D), nxt),
                  pl.BlockSpec((tm, 2 * D), row), pl.BlockSpec((HALO, 2 * D), prev), pl.BlockSpec((HALO, 2 * D), nxt),
                  _res((NSTR, 32, LANES)), pl.BlockSpec(memory_space=pl.ANY)] + dep_spec,
        out_specs=[pl.BlockSpec((tm, 2 * D), row),
                   pl.BlockSpec((NSTR, 32, LANES), lambda i: (0, 0, 0))],
        out_shape=[_sds(dp3.shape, BF16), _sds((NSTR, 32, LANES), F32)],
        scratch_shapes=[pltpu.VMEM((NSTR, tm + 2 * HALO, LANES), F32), pltpu.VMEM((NSTR, tm + 2 * HALO, LANES), F32),
                        pltpu.VMEM((NSTR, tm, LANES), F32), pltpu.VMEM((NSTR, 32 * 8, LANES), F32)],
        input_output_aliases={7: 0},
        compiler_params=_cparams(("arbitrary",)),
    )(dc1, dc1, dc1, proj, proj, proj, conv_wf_s, dp3, *dep_arg)


def _mix_in_bwd(dx1, dp3, x2d, g_mix, w_in_g, layer, tm, dep=None):
    T, D = x2d.shape
    CS = w_in_g.shape[2]
    CN = CS // 3

    def body(dx1_ref, dp_ref, x_ref, g_ref, w_ref, dx_ref, dg_ref):
        i = pl.program_id(0)

        @pl.when(i == 0)
        def _():
            dg_ref[...] = jnp.zeros_like(dg_ref)

        dh = jnp.zeros((tm, D), F32)
        for j in range(12):
            dh = dh + _dot_nt(dp_ref[:, j * CN:(j + 1) * CN], w_ref[j // 3, :, (j % 3) * CN:(j % 3 + 1) * CN])
        x = x_ref[...]
        rstd = lax.rsqrt(jnp.mean(x * x, axis=-1, keepdims=True) + EPS)
        xh = x * rstd
        dg_ref[...] += jnp.sum(dh * xh, axis=0, keepdims=True)
        dxh = dh * g_ref[...]
        dx_ref[...] = dx1_ref[...] + rstd * (dxh - xh * jnp.mean(dxh * xh, axis=-1, keepdims=True))

    row = lambda i: (i, 0)
    body, dep_spec, dep_arg = _after(body, 5, dep)
    return pl.pallas_call(
        body, name="mix_in_bwd", grid=(T // tm,),
        in_specs=[pl.BlockSpec((tm, D), row), pl.BlockSpec((tm, 6 * D), row),
                  pl.BlockSpec((tm, D), row), _res((1, D)), _res((NSHARD, D, CS), lambda i: (0, layer, 0))] + dep_spec,
        out_specs=[pl.BlockSpec((tm, D), row), pl.BlockSpec((1, D), lambda i: (0, 0))],
        out_shape=[_sds((T, D), F32), _sds((1, D), F32)],
        compiler_params=_cparams(("arbitrary",)),
    )(dx1, dp3, x2d, g_mix, w_in_g, *dep_arg)


def _tn_matmul(name, a, a_block, a_map, bs, b_block, b_map, out_shape, out_block, out_map, nj, nt):
    kk = [d for d in a_block if d is not None][-1]
    nn = [d for d in b_block if d is not None][-1]
    nb = len(bs)
    a_list = a if isinstance(a, (list, tuple)) else [a]
    na = len(a_list)

    def body(*refs):
        a_refs, b_refs = refs[:na], refs[na:na + nb]
        o_refs, acc_refs = refs[-2 * nb:-nb], refs[-nb:]
        t = pl.program_id(1)

        @pl.when(t == 0)
        def _():
            for acc_ref in acc_refs:
                acc_ref[...] = jnp.zeros_like(acc_ref)

        a_ts = [a_ref[...].astype(BF16) for a_ref in a_refs]
        for i, (b_ref, acc_ref) in enumerate(zip(b_refs, acc_refs)):
            acc_ref[...] += _dot_tn(a_ts[i % na], b_ref[...].astype(BF16))

        @pl.when(t == nt - 1)
        def _():
            for o_ref, acc_ref in zip(o_refs, acc_refs):
                o_ref[...] = acc_ref[...].astype(o_ref.dtype)

    return pl.pallas_call(
        body, name=name, grid=(nj, nt),
        in_specs=[pl.BlockSpec(a_block, a_map)] * na + [pl.BlockSpec(b_block, b_map)] * nb,
        out_specs=[pl.BlockSpec(out_block, out_map)] * nb, out_shape=[_sds(out_shape, BF16)] * nb,
        scratch_shapes=[pltpu.VMEM((kk, nn), F32)] * nb,
        compiler_params=_cparams(("parallel", "arbitrary")),
    )(*a_list, *bs)


def _place_shard(name, w, pos, dtype, tr, dep=None, layer=None, into=None, row_off=0, out_rows=None):
    R, C = w.shape[-2:]
    out_rows = out_rows or R

    def body(pos_ref, w_ref, *rest):
        del pos_ref
        rest[-1][...] = w_ref[...].astype(dtype)

    in_spec = (pl.BlockSpec((tr, C), lambda r, pos: (r, 0)) if layer is None else
               pl.BlockSpec((None, tr, C), lambda r, pos: (layer, r, 0)))
    extra, extra_args = ([], []) if into is None else ([pl.BlockSpec(memory_space=pl.ANY)], [into])
    body, dep_spec, dep_arg = _after(body, 2 + len(extra), dep)
    grid_spec = pltpu.PrefetchScalarGridSpec(
        num_scalar_prefetch=1, grid=(R // tr,), in_specs=[in_spec] + extra + dep_spec,
        out_specs=pl.BlockSpec((None, tr, C), lambda r, pos: (pos[1], row_off // tr + r, 0)))
    return pl.pallas_call(body, name=name, grid_spec=grid_spec, out_shape=_sds((NSHARD, out_rows, C), dtype),
                          input_output_aliases={} if into is None else {2: 0},
                          compiler_params=_cparams(("parallel",)))(pos, w, *extra_args, *dep_arg)


def _add_halves(name, g, rbuf, pos, tr):
    NS, _, H, C = g.shape

    def body(pos_ref, g_ref, r_ref, o_ref):
        del pos_ref
        o_ref[...] = (g_ref[...].astype(F32) + r_ref[...].astype(F32)).astype(BF16)

    grid_spec = pltpu.PrefetchScalarGridSpec(
        num_scalar_prefetch=1, grid=(NS, H // tr),
        in_specs=[pl.BlockSpec((None, None, tr, C), lambda s, r, pos: (s, pos[0], r, 0)),
                  pl.BlockSpec((None, tr, C), lambda s, r, pos: (s, r, 0))],
        out_specs=pl.BlockSpec((None, tr, C), lambda s, r, pos: (s, r, 0)))
    return pl.pallas_call(body, name=name, grid_spec=grid_spec, out_shape=_sds((NS, H, C), BF16),
                          compiler_params=_cparams(("parallel", "parallel")))(pos, g, rbuf)


def _add_shards(name, p, rbuf, pos, tr, layer, n_layers, prev):
    _, H, C = p.shape

    def body(pos_ref, p_ref, r_ref, *rest):
        del pos_ref
        o_ref = rest[-1]
        acc = p_ref[...].astype(F32)
        for j in range(3):
            acc = acc + r_ref[j].astype(F32)
        o_ref[...] = acc

    in_specs = [pl.BlockSpec((None, tr, C), lambda r, pos: (pos[1], r, 0)),
                pl.BlockSpec((3, tr, C), lambda r, pos: (0, r, 0))]
    args = [pos, p, rbuf]
    aliases = {}
    if prev is not None:
        in_specs.append(pl.BlockSpec(memory_space=pl.ANY))
        args.append(prev)
        aliases = {3: 0}
    grid_spec = pltpu.PrefetchScalarGridSpec(
        num_scalar_prefetch=1, grid=(H // tr,), in_specs=in_specs,
        out_specs=pl.BlockSpec((None, None, tr, C), lambda r, pos: (layer, pos[0], r, 0)))
    return pl.pallas_call(body, name=name, grid_spec=grid_spec, out_shape=_sds((n_layers, 2, H, C), F32),
                          input_output_aliases=aliases, compiler_params=_cparams(("parallel",)))(*args)


def _sum_slots(own, land, me, tr):
    NS8, R, C = land.shape

    def body(me_ref, own_ref, l_ref, o_ref):
        acc = None
        for j in range(NS8):
            term = jnp.where(me_ref[0] == j, own_ref[...], l_ref[j])
            acc = term if acc is None else acc + term
        o_ref[...] = acc

    grid_spec = pltpu.PrefetchScalarGridSpec(
        num_scalar_prefetch=1, grid=(R // tr,),
        in_specs=[pl.BlockSpec((tr, C), lambda i, me: (i, 0)), pl.BlockSpec((NS8, tr, C), lambda i, me: (0, i, 0))],
        out_specs=pl.BlockSpec((tr, C), lambda i, me: (i, 0)))
    return pl.pallas_call(body, name="sum_slots", grid_spec=grid_spec, out_shape=_sds((R, C), F32),
                          compiler_params=_cparams(("parallel",)))(me, own, land)


def _adamw_update(w_ref, g_ref, m_ref, v_ref, d_ref, mo_ref, vo_ref):
    g_ = g_ref[...]
    m_ = ADAM_B1 * m_ref[...] + (1.0 - ADAM_B1) * g_
    v_ = ADAM_B2 * v_ref[...] + (1.0 - ADAM_B2) * (g_ * g_)
    mo_ref[...] = m_
    vo_ref[...] = v_
    m_hat = m_ / (1.0 - ADAM_B1 ** ADAM_STEP)
    v_hat = v_ / (1.0 - ADAM_B2 ** ADAM_STEP)
    d_ref[...] = -ADAM_LR * (m_hat / (jnp.sqrt(v_hat) + ADAM_EPS) + ADAM_WD * w_ref[...])


def _adamw(name, w, g, m, v, tr, emit_g):
    R, C = w.shape
    n_out = 4 if emit_g else 3

    def body(w_ref, g_ref, m_ref, v_ref, d_ref, mo_ref, vo_ref, *go_ref):
        _adamw_update(w_ref, g_ref, m_ref, v_ref, d_ref, mo_ref, vo_ref)
        if emit_g:
            go_ref[0][...] = g_ref[...]

    spec = pl.BlockSpec((tr, C), lambda i: (i, 0))
    return pl.pallas_call(
        body, name=name, grid=(R // tr,), in_specs=[spec] * 4, out_specs=[spec] * n_out,
        out_shape=[_sds((R, C), F32)] * n_out, compiler_params=_cparams(("parallel",)))(w, g, m, v)


def _adamw_many(ws, gs, ms, vs):
    n = len(ws)

    def body(*refs):
        ins, outs = refs[:4 * n], refs[4 * n:]
        for k in range(n):
            _adamw_update(ins[k], ins[n + k], ins[2 * n + k], ins[3 * n + k], outs[k], outs[n + k], outs[2 * n + k])

    vmem = pl.BlockSpec(memory_space=pltpu.VMEM)
    res = pl.pallas_call(
        body, name="adamw_small", in_specs=[vmem] * (4 * n), out_specs=[vmem] * (3 * n),
        out_shape=[_sds(w.shape, F32) for w in ws] * 3,
        compiler_params=pltpu.CompilerParams(vmem_limit_bytes=VMEM_LIMIT))(*ws, *gs, *ms, *vs)
    return list(res[:n]), list(res[n:2 * n]), list(res[2 * n:])


def _row_tile(rows, cap):
    best = rows
    for t in range(8, min(rows, cap) + 1, 8):
        if rows % t == 0:
            best = t
    return best


HBM_SPEC = pl.BlockSpec(memory_space=pltpu.HBM)
SEM_SPEC = pl.BlockSpec(memory_space=pltpu.SEMAPHORE)
DATAFLOW = pltpu.SideEffectType.DATAFLOW_SIDE_EFFECTING
DMA_SEM = pltpu.SemaphoreType.DMA


def _hbm(a):
    return pltpu.with_memory_space_constraint(a, pltpu.HBM)


def _mesh_pos():
    return lax.axis_index("x"), lax.axis_index("y"), lax.axis_index("c")


def _other_chips(x, y):
    return [(1 - x, y), (x, 1 - y), (1 - x, 1 - y)]


def _half_rows(buf, shard, core):
    h = buf.shape[1] // 2
    return buf.at[shard, pl.ds(core * h, h), :]


def _ici_copy(buf, j, send, recv, landing):
    x, y, c = _mesh_pos()
    px, py = _other_chips(x, y)[j]
    part = _half_rows(buf, 2 * px + py if landing else 2 * x + y, c)
    return pltpu.make_async_remote_copy(src_ref=part, dst_ref=part, send_sem=send, recv_sem=recv,
                                        device_id=(px, py, c), device_id_type=MESH_ID)


def _sibling_copy(buf, j, send, recv, landing):
    x, y, c = _mesh_pos()
    px, py = _other_chips(x, y)[j]
    part = _half_rows(buf, 2 * px + py, 1 - c if landing else c)
    return pltpu.make_async_remote_copy(src_ref=part, dst_ref=part, send_sem=send, recv_sem=recv,
                                        device_id=(x, y, 1 - c), device_id_type=MESH_ID)


def _forward_sibling(name, bufs, with_ici):
    n = len(bufs)

    def body(*refs):
        ins = refs[:n]
        send_ici, recv_ici, send_d2d, recv_d2d = refs[2 * n:]
        sends = []
        if with_ici:
            for i in range(n):
                for j in range(3):
                    cp = _ici_copy(ins[i], j, send_ici.at[i, j], recv_ici.at[i, j], False)
                    cp.start()
                    sends.append(cp)
        for i in range(n):
            for j in range(3):
                if with_ici:
                    _ici_copy(ins[i], j, send_ici.at[i, j], recv_ici.at[i, j], True).wait_recv()
                cp = _sibling_copy(ins[i], j, send_d2d.at[i, j], recv_d2d.at[i, j], False)
                cp.start()
                sends.append(cp)
        for i in range(n):
            for j in range(3):
                _sibling_copy(ins[i], j, send_d2d.at[i, j], recv_d2d.at[i, j], True).wait_recv()
        for cp in sends:
            cp.wait_send()

    return pl.pallas_call(
        body, name=name, in_specs=[HBM_SPEC] * n, out_specs=[HBM_SPEC] * n,
        out_shape=[_sds(b.shape, b.dtype) for b in bufs],
        scratch_shapes=[DMA_SEM((n, 3))] * 4, input_output_aliases={i: i for i in range(n)},
    )(*bufs)


def _gather_start(name, groups):
    flat = [b for g in groups for b in g]
    n, ng = len(flat), len(groups)

    def body(*refs):
        ins, sems, token = refs[:n], refs[n:n + 2 * ng], refs[-1]
        k = 0
        for gi, g in enumerate(groups):
            for a in range(len(g)):
                for j in range(3):
                    _ici_copy(ins[k], j, sems[2 * gi], sems[2 * gi + 1], False).start()
                k += 1
        token[...] = jnp.zeros_like(token)

    res = pl.pallas_call(
        body, name=name, in_specs=[HBM_SPEC] * n,
        out_specs=[SEM_SPEC] * (2 * ng) + [HBM_SPEC] * n + [pl.BlockSpec(memory_space=pltpu.VMEM)],
        out_shape=[DMA_SEM(()) for g in groups for _ in range(2)]
        + [pltpu.HBM(b.shape, b.dtype) for b in flat] + [_sds((8, LANES), F32)],
        input_output_aliases={i: 2 * ng + i for i in range(n)},
        compiler_params=pltpu.CompilerParams(has_side_effects=DATAFLOW),
    )(*[_hbm(b) for b in flat])
    sems = [(res[2 * gi], res[2 * gi + 1]) for gi in range(ng)]
    thru, k = [], 2 * ng
    for g in groups:
        thru.append(list(res[k:k + len(g)]))
        k += len(g)
    return sems, thru, res[-1]


def _gather_wait(name, bufs, sems, after):
    n = len(bufs)

    def body(*refs):
        ins, send, recv = refs[:n], refs[n], refs[n + 1]
        for a in range(n):
            for j in range(3):
                _ici_copy(ins[a], j, send, recv, False).wait_send()
                _ici_copy(ins[a], j, send, recv, True).wait_recv()

    return pl.pallas_call(
        body, name=name, in_specs=[HBM_SPEC] * n + [SEM_SPEC, SEM_SPEC, pl.BlockSpec(memory_space=pl.ANY)],
        out_specs=[HBM_SPEC] * n, out_shape=[pltpu.HBM(b.shape, b.dtype) for b in bufs],
        input_output_aliases={i: i for i in range(n)},
        compiler_params=pltpu.CompilerParams(has_side_effects=DATAFLOW),
    )(*bufs, sems[0], sems[1], after)


def _send_sibling_halves(name, arrs):
    n = len(arrs)

    def body(*refs):
        ins, outs = refs[:n], refs[n:2 * n]
        send, recv = refs[2 * n:]
        x, y, c = _mesh_pos()
        cps = []
        for i in range(n):
            cp = pltpu.make_async_remote_copy(
                src_ref=ins[i].at[:, 1 - c], dst_ref=outs[i],
                send_sem=send.at[i], recv_sem=recv.at[i], device_id=(x, y, 1 - c), device_id_type=MESH_ID)
            cp.start()
            cps.append(cp)
        for cp in cps:
            cp.wait()

    return pl.pallas_call(
        body, name=name, in_specs=[HBM_SPEC] * n, out_specs=[HBM_SPEC] * n,
        out_shape=[_sds((a.shape[0],) + a.shape[2:], a.dtype) for a in arrs],
        scratch_shapes=[DMA_SEM((n,)), DMA_SEM((n,))],
    )(*arrs)


def _chip_copy(p, land, j, send, recv):
    x, y, c = _mesh_pos()
    px, py = _other_chips(x, y)[j]
    return pltpu.make_async_remote_copy(src_ref=p.at[2 * px + py], dst_ref=land.at[j], send_sem=send, recv_sem=recv,
                                        device_id=(px, py, c), device_id_type=MESH_ID)


def _chip_send_start(name, ps):
    n = len(ps)
    lands = [lax.empty((3,) + p.shape[1:], p.dtype) for p in ps]

    def body(*refs):
        ins, lnd, send, recv, token = refs[:n], refs[n:2 * n], refs[2 * n], refs[2 * n + 1], refs[-1]
        for i in range(n):
            for j in range(3):
                _chip_copy(ins[i], lnd[i], j, send, recv).start()
        token[...] = jnp.zeros_like(token)

    res = pl.pallas_call(
        body, name=name, in_specs=[HBM_SPEC] * (2 * n),
        out_specs=[SEM_SPEC, SEM_SPEC] + [HBM_SPEC] * (2 * n) + [pl.BlockSpec(memory_space=pltpu.VMEM)],
        out_shape=[DMA_SEM(()), DMA_SEM(())] + [pltpu.HBM(a.shape, a.dtype) for a in ps + lands]
        + [_sds((8, LANES), F32)],
        input_output_aliases={i: 2 + i for i in range(2 * n)},
        compiler_params=pltpu.CompilerParams(has_side_effects=DATAFLOW),
    )(*[_hbm(a) for a in ps + lands])
    return (res[0], res[1]), list(res[2:2 + n]), list(res[2 + n:2 + 2 * n]), res[-1]


def _chip_send_wait(name, ps, lands, sems, after):
    n = len(ps)

    def body(*refs):
        ins, lnd, send, recv = refs[:n], refs[n:2 * n], refs[2 * n], refs[2 * n + 1]
        for i in range(n):
            for j in range(3):
                cp = _chip_copy(ins[i], lnd[i], j, send, recv)
                cp.wait_send()
                cp.wait_recv()

    res = pl.pallas_call(
        body, name=name, in_specs=[HBM_SPEC] * (2 * n) + [SEM_SPEC, SEM_SPEC, pl.BlockSpec(memory_space=pl.ANY)],
        out_specs=[HBM_SPEC] * (2 * n), out_shape=[pltpu.HBM(a.shape, a.dtype) for a in ps + lands],
        input_output_aliases={i: i for i in range(2 * n)},
        compiler_params=pltpu.CompilerParams(has_side_effects=DATAFLOW),
    )(*ps, *lands, sems[0], sems[1], after)
    return list(res[:n]), list(res[n:])


def _join_halves(arrs):
    n = len(arrs)

    def body(*refs):
        bufs = refs[n:2 * n]
        send, recv = refs[2 * n:]
        x, y, c = _mesh_pos()
        cps = []
        for i in range(n):
            mine = bufs[i].at[:, c]
            cp = pltpu.make_async_remote_copy(
                src_ref=mine, dst_ref=mine, send_sem=send.at[i], recv_sem=recv.at[i],
                device_id=(x, y, 1 - c), device_id_type=MESH_ID)
            cp.start()
            cps.append(cp)
        for i, cp in enumerate(cps):
            theirs = bufs[i].at[:, 1 - c]
            cp.wait_send()
            pltpu.make_async_remote_copy(
                src_ref=theirs, dst_ref=theirs, send_sem=send.at[i], recv_sem=recv.at[i],
                device_id=(x, y, 1 - c), device_id_type=MESH_ID).wait_recv()

    return pl.pallas_call(
        body, name="join_halves", in_specs=[HBM_SPEC] * n, out_specs=[HBM_SPEC] * n,
        out_shape=[_sds(a.shape, a.dtype) for a in arrs],
        scratch_shapes=[DMA_SEM((n,)), DMA_SEM((n,))], input_output_aliases={i: i for i in range(n)},
    )(*arrs)


def _peer_copy(buf, land, k, send, recv, landing):
    x, y, c = _mesh_pos()
    px, py, pc = x ^ ((k >> 2) & 1), y ^ ((k >> 1) & 1), c ^ (k & 1)
    slot = 4 * px + 2 * py + pc if landing else 4 * x + 2 * y + c
    return pltpu.make_async_remote_copy(src_ref=buf, dst_ref=land.at[slot], send_sem=send, recv_sem=recv,
                                        device_id=(px, py, pc), device_id_type=MESH_ID)


def _exchange_all(buf):
    def body(in_ref, out_ref, send, recv):
        cps = [_peer_copy(in_ref, out_ref, k, send.at[k - 1], recv.at[k - 1], False) for k in range(1, 8)]
        for cp in cps:
            cp.start()
        for k in range(1, 8):
            cps[k - 1].wait_send()
            _peer_copy(in_ref, out_ref, k, send.at[k - 1], recv.at[k - 1], True).wait_recv()

    return pl.pallas_call(
        body, name="exchange_all", in_specs=[HBM_SPEC], out_specs=HBM_SPEC,
        out_shape=_sds((8,) + buf.shape, buf.dtype), scratch_shapes=[DMA_SEM((7,)), DMA_SEM((7,))],
    )(buf)


def _exchange_start(name, buf):
    land = lax.empty((8,) + buf.shape, buf.dtype)

    def body(in_ref, land_ref, send, recv, in_thru, land_thru, token):
        for k in range(1, 8):
            _peer_copy(in_ref, land_ref, k, send, recv, False).start()
        token[...] = jnp.zeros_like(token)

    res = pl.pallas_call(
        body, name=name, in_specs=[HBM_SPEC] * 2,
        out_specs=[SEM_SPEC, SEM_SPEC, HBM_SPEC, HBM_SPEC, pl.BlockSpec(memory_space=pltpu.VMEM)],
        out_shape=[DMA_SEM(()), DMA_SEM(()), pltpu.HBM(buf.shape, buf.dtype), pltpu.HBM(land.shape, land.dtype),
                   _sds((8, LANES), F32)],
        input_output_aliases={0: 2, 1: 3}, compiler_params=pltpu.CompilerParams(has_side_effects=DATAFLOW),
    )(_hbm(buf), _hbm(land))
    return (res[0], res[1]), res[2], res[3], res[4]


def _exchange_wait(name, buf, land, sems, after):
    def body(in_ref, land_ref, send, recv, after_ref, in_thru, land_thru):
        for k in range(1, 8):
            _peer_copy(in_ref, land_ref, k, send, recv, False).wait_send()
            _peer_copy(in_ref, land_ref, k, send, recv, True).wait_recv()

    res = pl.pallas_call(
        body, name=name, in_specs=[HBM_SPEC, HBM_SPEC, SEM_SPEC, SEM_SPEC, pl.BlockSpec(memory_space=pl.ANY)],
        out_specs=[HBM_SPEC, HBM_SPEC], out_shape=[pltpu.HBM(buf.shape, buf.dtype), pltpu.HBM(land.shape, land.dtype)],
        input_output_aliases={0: 0, 1: 1}, compiler_params=pltpu.CompilerParams(has_side_effects=DATAFLOW),
    )(buf, land, sems[0], sems[1], after)
    return res[0], res[1]


def _pad_to(a, axis, size):
    pad = [(0, 0)] * a.ndim
    pad[axis] = (0, size - a.shape[axis])
    return jnp.pad(a, pad)


def _strips(w):
    k, d = w.shape
    return _pad_to(w, 0, 32).reshape(32, d // LANES, LANES).transpose(1, 0, 2)


def kernel(x, norm_mix, w_in, gate_bias, conv_w, conv_b, conv_ln_g, conv_ln_b, w_conv_out, sgu_ln_g, sgu_ln_b, w_spatial, b_spatial, w_sgu_out, w_o, norm_ffn, w_ffn_gate, w_ffn_up, w_ffn_down, norm_final, loss_target, m_norm_mix, m_w_in, m_gate_bias, m_conv_w, m_conv_b, m_conv_ln_g, m_conv_ln_b, m_w_conv_out, m_sgu_ln_g, m_sgu_ln_b, m_w_spatial, m_b_spatial, m_w_sgu_out, m_w_o, m_norm_ffn, m_w_ffn_gate, m_w_ffn_up, m_w_ffn_down, m_norm_final, v_norm_mix, v_w_in, v_gate_bias, v_conv_w, v_conv_b, v_conv_ln_g, v_conv_ln_b, v_w_conv_out, v_sgu_ln_g, v_sgu_ln_b, v_w_spatial, v_b_spatial, v_w_sgu_out, v_w_o, v_norm_ffn, v_w_ffn_gate, v_w_ffn_up, v_w_ffn_down, v_norm_final):
    BL, S, D = x.shape
    T = BL * S
    L = w_in.shape[0]
    CS = w_in.shape[2]
    CN = CS // 3
    DQ = D // NSHARD
    FS = w_ffn_gate.shape[2]
    F = NSHARD * FS
    G, CH = w_spatial.shape[1], w_spatial.shape[2]
    KW = conv_w.shape[1]
    CQ = conv_w.shape[3]
    NSTR = D // LANES
    tm = min(512, S // 2)
    tm2 = max(tm // 2, CH)
    rb = min(64, tm)
    mx, my, mc = _mesh_pos()
    pos = jnp.stack([mc, 2 * mx + my]).astype(jnp.int32)

    def placed(name, w, dtype=BF16, dep=None, layer=None, **kw):
        return _place_shard("place_" + name, w, pos, dtype, _row_tile(w.shape[-2], 256), dep, layer, **kw)

    w_in0 = placed("w_in", w_in, layer=0)
    cw_p = placed("conv_w", _pad_to(conv_w.reshape(L, KW, CQ), 1, 32).reshape(L * 32, CQ), F32)
    fsems, fflying, ftoken = _gather_start("gather_start_first", [[w_in0, cw_p]])
    wts = []
    for l in range(L):
        w_sq = None
        for i, w in enumerate([w_conv_out, w_sgu_out, w_o]):
            w_sq = placed("w_sq", w, dep=ftoken, layer=l, into=w_sq, row_off=i * DQ, out_rows=3 * DQ)
        wts.append(dict(w_in=placed("w_in", w_in, dep=ftoken, layer=l) if l else None, w_sq=w_sq,
                        wg=placed("w_gate", w_ffn_gate[l].T, dep=ftoken), wu=placed("w_up", w_ffn_up[l].T, dep=ftoken),
                        wd=placed("w_down", w_ffn_down, dep=ftoken, layer=l)))
    ffn_keys = ["wg", "wu", "wd"]
    order = [[(0, "w_sq")], [(0, k) for k in ffn_keys]]
    order += [[(l, k) for k in ["w_in", "w_sq"] + ffn_keys] for l in range(1, L)]
    gsems, flying, token = _gather_start("gather_start", [[wts[l][k] for l, k in grp] for grp in order])
    first = _gather_wait("gather_wait_first", fflying[0], fsems[0], token)
    wts[0]["w_in"], cw_g = _forward_sibling("gather_first", first, False)
    conv_w_full = cw_g.reshape(NSHARD, L, 32, CQ).transpose(1, 2, 0, 3).reshape(L, 32, D)[:, :KW]

    def land(gi, after):
        bufs = _gather_wait("gather_wait_%d" % gi, flying[gi], gsems[gi], after)
        bufs = _forward_sibling("gather_forward_%d" % gi, bufs, False)
        for (l, k), b in zip(order[gi], bufs):
            wts[l][k] = b

    x2d = x.reshape(T, D)
    tgt = loss_target.reshape(T, D)
    row = lambda a, l: a[l].reshape(1, -1)

    saved = []
    xc = x2d
    for l in range(L):
        ws_b = w_spatial[l].astype(BF16)
        bs_b = jnp.repeat(b_spatial[l].T, D // G, axis=1)
        cw_s = _strips(conv_w_full[l])
        h, proj = _mix_in_fwd(xc, row(norm_mix, l), wts[l]["w_in"], 0, tm, token if l == 0 else None)
        if l == 0:
            land(0, h)
        c1h, rstd_c, c3, ya = _conv_fwd(proj, cw_s, row(conv_b, l), row(conv_ln_g, l), row(conv_ln_b, l),
                                        wts[l]["w_sq"], 0, S, tm, rb)
        if l == 0:
            land(1, ya)
        ffn_w = [wts[l][k].reshape(F, D) for k in ffn_keys]
        head = (norm_final.reshape(1, D), tgt) if l + 1 == L else None
        mixed, gated, yb, merged, x1, h2, gt, up, act, x2, *tail = _sgu_ffn_fwd(
            proj, ya, xc, row(sgu_ln_g, l), row(sgu_ln_b, l), ws_b, bs_b, row(gate_bias, l), wts[l]["w_sq"], 0,
            row(norm_ffn, l), *ffn_w, tm2, head)
        if l + 1 < L:
            land(l + 2, x2)
        saved.append(dict(x=xc, h=h, proj=proj, c1h=c1h, rstd_c=rstd_c, c3=c3, ya=ya, mixed=mixed, gated=gated,
                          yb=yb, merged=merged, x1=x1, h2=h2, gt=gt, up=up, act=act, ws_b=ws_b, cw=conv_w_full[l]))
        xc = x2

    dx, (loss_part, d_norm_final) = xc, tail
    loss = lax.psum(loss_part[0, 0], ("x", "y", "c"))

    g_acc = {}

    def reduce_start(tag, layer, named):
        arrs = [g.reshape(NSHARD, 2, g.shape[1] // 2, g.shape[2]) for _, g in named]
        from_sib = _send_sibling_halves("sibling_" + tag, arrs)
        ps = [_add_halves("presum_" + nm, a, r, pos, _row_tile(a.shape[2], 256))
              for (nm, _), a, r in zip(named, arrs, from_sib)]
        sems, ps, lands, tok = _chip_send_start("chip_send_start_" + tag, ps)
        return dict(tag=tag, layer=layer, names=[nm for nm, _ in named], ps=ps, lands=lands, sems=sems), tok

    def reduce_finish(pend, after):
        ps, lands = _chip_send_wait("chip_send_wait_" + pend["tag"], pend["ps"], pend["lands"], pend["sems"], after)
        for nm, p, r in zip(pend["names"], ps, lands):
            g_acc[nm] = _add_shards("shardsum_" + nm, p, r, pos, _row_tile(p.shape[1], 256), pend["layer"], L,
                                    g_acc.get(nm))

    me_idx = (4 * mx + 2 * my + mc).astype(jnp.int32).reshape(1)
    exchanges = []

    def pack_rows(pieces):
        packed = jnp.concatenate(pieces, axis=0)
        return _pad_to(packed, 0, -(-packed.shape[0] // 8) * 8)

    def unpack_rows(summed, pieces):
        out, off = [], 0
        for p in pieces:
            out.append(summed[off:off + p.shape[0]])
            off += p.shape[0]
        return out

    def small_start(tag, pieces):
        sems, buf, land, token = _exchange_start("exchange_start_" + tag, pack_rows(pieces))
        return dict(tag=tag, pieces=pieces, buf=buf, land=land, sems=sems, token=token)

    def small_finish(st, after):
        buf, land = _exchange_wait("exchange_wait_" + st["tag"], st["buf"], st["land"], st["sems"], after)
        return unpack_rows(_sum_slots(buf, land, me_idx, _row_tile(buf.shape[0], 256)), st["pieces"])

    small = [None] * L
    tt = min(2048, T // 2)
    nt = T // tt
    pending, tok = None, None
    for l in reversed(range(L)):
        sv, wt = saved[l], wts[l]
        ffn_w = [wt[k].reshape(F, D) for k in ffn_keys]
        dx1, dgt, dup, d_norm_ffn = _ffn_bwd(dx, sv["x1"], sv["gt"], sv["up"], row(norm_ffn, l), *ffn_w, tm2, tok)
        tn_a = ((tt, F // 2), lambda j, t: (t, j))
        tn_b = ((tt, D), lambda j, t: (t, 0))
        tn_o = ((F, D), (F // 2, D), lambda j, t: (j, 0), 2, nt)
        g_g, = _tn_matmul("grad_w_gate", dgt, *tn_a, [sv["h2"]], *tn_b, *tn_o)
        g_u, = _tn_matmul("grad_w_up", dup, *tn_a, [sv["h2"]], *tn_b, *tn_o)
        g_d, = _tn_matmul("grad_w_down", sv["act"], *tn_a, [dx], *tn_b, *tn_o)
        if pending is not None:
            reduce_finish(pending, g_d)
        ffn_pend, tok = reduce_start("ffn%d" % l, l, [
            ("w_ffn_gate", g_g.reshape(NSHARD, FS, D)), ("w_ffn_up", g_u.reshape(NSHARD, FS, D)),
            ("w_ffn_down", g_d.reshape(NSHARD, FS, D))])
        wst_b = jnp.swapaxes(sv["ws_b"], 1, 2)
        dya, dyb, dp3, dc1, d_gate_bias, d_sgu_g, d_sgu_b, d_bs, d_ws, d_cln_g, d_cln_b, d_conv_b = _merge_sgu_bwd(
            dx1, sv["proj"], sv["ya"], sv["yb"], sv["mixed"], sv["c1h"], sv["rstd_c"], row(sgu_ln_g, l),
            row(sgu_ln_b, l), sv["ws_b"], wst_b, row(gate_bias, l), row(conv_ln_g, l), row(conv_ln_b, l), wt["w_sq"],
            0, tm2, tok)
        sq_args = ((tt, D), lambda j, t: (t, 0))
        sq_out = ((D, D), (D, D), lambda j, t: (0, 0), 1, nt)
        g_o, = _tn_matmul("grad_w_o", sv["merged"], *sq_args, [dx1], *sq_args, *sq_out)
        g_so, = _tn_matmul("grad_w_sgu_out", sv["gated"], *sq_args, [dyb], *sq_args, *sq_out)
        g_co, = _tn_matmul("grad_w_conv_out", sv["c3"], *sq_args, [dya], *sq_args, *sq_out)
        small[l] = [None, d_gate_bias.reshape(2, D), None, d_conv_b, d_cln_g, d_cln_b, d_sgu_g, d_sgu_b,
                    d_ws.reshape(G * CH * CH // D, D), d_bs.reshape(G * CH // D, D), d_norm_ffn]
        tok_x = None
        if l == 0:
            early = [k for k in range(len(small[0])) if small[0][k] is not None]
            exchanges.append((small_start("early0", [small[0][k] for k in early]), [(0, k) for k in early]))
            tok_x = exchanges[-1][0]["token"]
        dp3, d_cw_s = _conv_bwd(dc1, sv["proj"], dp3, _strips(sv["cw"][::-1]), S, tm, rb, tok_x)
        g_in, = _tn_matmul("grad_w_in", sv["h"], (tt, D), lambda j, t: (t, 0), [dp3], (tt, CS), lambda j, t: (t, j),
                           (NSHARD, D, CS), (None, D, CS), lambda j, t: (j, 0, 0), NSHARD, nt)
        reduce_finish(ffn_pend, g_in)
        pending, tok = reduce_start("mix%d" % l, l, [
            ("w_in", g_in), ("w_conv_out", g_co.reshape(NSHARD, DQ, D)), ("w_sgu_out", g_so.reshape(NSHARD, DQ, D)),
            ("w_o", g_o.reshape(NSHARD, DQ, D))])
        dx, d_norm_mix = _mix_in_bwd(dx1, dp3, sv["x"], row(norm_mix, l), wt["w_in"], 0, tm, tok)
        small[l][0] = d_norm_mix
        small[l][2] = d_cw_s.transpose(1, 0, 2).reshape(32, D)
        if l > 0:
            exchanges.append((small_start("layer%d" % l, small[l]), [(l, k) for k in range(len(small[l]))]))
            tok = [tok, exchanges[-1][0]["token"]]
    reduce_finish(pending, dx)
    grad_x = dx.reshape(BL, S, D)

    names = ["w_in", "w_conv_out", "w_sgu_out", "w_o", "w_ffn_gate", "w_ffn_up", "w_ffn_down"]
    g_full = _join_halves([g_acc[nm] for nm in names])
    g_w_in, g_w_co, g_w_so, g_w_o, g_w_g, g_w_u, g_w_d = [g.reshape(L, 2 * g.shape[2], g.shape[3]) for g in g_full]
    g_w_g = jnp.swapaxes(g_w_g, 1, 2)
    g_w_u = jnp.swapaxes(g_w_u, 1, 2)

    late = [small[0][0], small[0][2], d_norm_final]
    packed = pack_rows(late)
    summed = _sum_slots(packed, _exchange_all(packed), me_idx, _row_tile(packed.shape[0], 256))
    sg = [[None] * len(small[l]) for l in range(L)]
    sg[0][0], sg[0][2], g_norm_final = unpack_rows(summed, late)
    g_norm_final = g_norm_final[0]
    for st, where in exchanges:
        for (l, k), piece in zip(where, small_finish(st, summed)):
            sg[l][k] = piece

    def per_layer(k, shape):
        return jnp.stack([sg[l][k] for l in range(L)]).reshape(shape)

    g_norm_mix = per_layer(0, (L, D))
    g_gate_bias = per_layer(1, (L, 2 * D))
    g_conv_w_full = jnp.stack([sg[l][2][:KW] for l in range(L)])
    g_conv_w = lax.dynamic_slice_in_dim(g_conv_w_full, (2 * mx + my) * CQ, CQ, axis=2).reshape(L, KW, 1, CQ)
    g_conv_b = per_layer(3, (L, D))
    g_conv_ln_g = per_layer(4, (L, D))
    g_conv_ln_b = per_layer(5, (L, D))
    g_sgu_ln_g = per_layer(6, (L, D))
    g_sgu_ln_b = per_layer(7, (L, D))
    g_w_spatial = per_layer(8, (L, G, CH, CH))
    g_b_spatial = per_layer(9, (L, G, CH))
    g_norm_ffn = per_layer(10, (L, D))

    grads = [g_norm_mix, g_w_in, g_gate_bias, g_conv_w, g_conv_b, g_conv_ln_g, g_conv_ln_b, g_w_co, g_sgu_ln_g,
             g_sgu_ln_b, g_w_spatial, g_b_spatial, g_w_so, g_w_o, g_norm_ffn, g_w_g, g_w_u, g_w_d, g_norm_final]
    weights = [norm_mix, w_in, gate_bias, conv_w, conv_b, conv_ln_g, conv_ln_b, w_conv_out, sgu_ln_g, sgu_ln_b,
               w_spatial, b_spatial, w_sgu_out, w_o, norm_ffn, w_ffn_gate, w_ffn_up, w_ffn_down, norm_final]
    ms = [m_norm_mix, m_w_in, m_gate_bias, m_conv_w, m_conv_b, m_conv_ln_g, m_conv_ln_b, m_w_conv_out, m_sgu_ln_g,
          m_sgu_ln_b, m_w_spatial, m_b_spatial, m_w_sgu_out, m_w_o, m_norm_ffn, m_w_ffn_gate, m_w_ffn_up,
          m_w_ffn_down, m_norm_final]
    vs = [v_norm_mix, v_w_in, v_gate_bias, v_conv_w, v_conv_b, v_conv_ln_g, v_conv_ln_b, v_w_conv_out, v_sgu_ln_g,
          v_sgu_ln_b, v_w_spatial, v_b_spatial, v_w_sgu_out, v_w_o, v_norm_ffn, v_w_ffn_gate, v_w_ffn_up,
          v_w_ffn_down, v_norm_final]

    big_idx = [1, 7, 12, 13, 15, 16, 17]
    transposed = [15, 16]
    deltas, new_m, new_v = [None] * 19, [None] * 19, [None] * 19
    for k in big_idx:
        shp = weights[k].shape
        r2 = (shp[0] * shp[1], shp[2])
        res = _adamw("adamw_" + str(k), weights[k].reshape(r2), grads[k].reshape(r2), ms[k].reshape(r2),
                     vs[k].reshape(r2), _row_tile(r2[0], 256), k not in transposed)
        deltas[k], new_m[k], new_v[k] = [a.reshape(shp) for a in res[:3]]
        if k not in transposed:
            grads[k] = res[3].reshape(shp)
    small_idx = [k for k in range(19) if k not in big_idx]
    pick = lambda arrs: [arrs[k].reshape(1, -1) if arrs[k].ndim == 1 else arrs[k] for k in small_idx]
    d_, m_, v_ = _adamw_many(pick(weights), pick(grads), pick(ms), pick(vs))
    for i, k in enumerate(small_idx):
        shp = weights[k].shape
        deltas[k], new_m[k], new_v[k] = d_[i].reshape(shp), m_[i].reshape(shp), v_[i].reshape(shp)

    return (loss, grad_x, *grads, *deltas, *new_m, *new_v)
```

```python
import jax
import jax.numpy as jnp
from jax import lax
from jax.experimental import pallas as pl
from jax.experimental.pallas import tpu as pltpu

F32 = jnp.float32
BF16 = jnp.bfloat16
EPS = 1e-6
ADAM_LR = 0.001
ADAM_B1 = 0.9
ADAM_B2 = 0.999
ADAM_EPS = 1e-08
ADAM_WD = 0.01
ADAM_STEP = 10

NSHARD = 4
LANES = 128
HALO = 16
VMEM_LIMIT = 60 * 1024 * 1024
MESH_ID = pl.DeviceIdType.MESH


def _dot(a, b):
    return jnp.dot(a, b, preferred_element_type=F32)


def _dot_nt(a, b):
    return lax.dot_general(a, b, (((1,), (1,)), ((), ())), preferred_element_type=F32)


def _dot_tn(a, b):
    return lax.dot_general(a, b, (((0,), (0,)), ((), ())), preferred_element_type=F32)


def _sig(z):
    return 1.0 / (1.0 + jnp.exp(-z))


def _res(shape, imap=None):
    nd = len(shape)
    if imap is None:
        imap = lambda *_: (0,) * nd
    return pl.BlockSpec(shape, imap, pipeline_mode=pl.Buffered(1))


def _cparams(sem):
    return pltpu.CompilerParams(dimension_semantics=sem, vmem_limit_bytes=VMEM_LIMIT)


def _sds(shape, dtype):
    return jax.ShapeDtypeStruct(shape, dtype)


def _after(body, n_in, dep):
    deps = [] if dep is None else [d for d in (dep if isinstance(dep, (list, tuple)) else [dep]) if d is not None]
    if not deps:
        return body, [], []

    def wrapped(*refs):
        return body(*refs[:n_in], *refs[n_in + len(deps):])

    return wrapped, [pl.BlockSpec(memory_space=pl.ANY)] * len(deps), deps


def _mix_in_fwd(x2d, g_mix, w_in_g, layer, tm, dep=None):
    T, D = x2d.shape
    CS = w_in_g.shape[2]
    CN = CS // 3

    def body(x_ref, g_ref, w_ref, h_ref, p_ref):
        x = x_ref[...]
        rstd = lax.rsqrt(jnp.mean(x * x, axis=-1, keepdims=True) + EPS)
        h = (x * rstd * g_ref[...]).astype(BF16)
        h_ref[...] = h
        for s in range(NSHARD):
            for j in range(3):
                c0 = s * CS + j * CN
                p_ref[:, c0:c0 + CN] = _dot(h, w_ref[s, :, j * CN:(j + 1) * CN]).astype(BF16)

    body, dep_spec, dep_arg = _after(body, 3, dep)
    return pl.pallas_call(
        body, name="mix_in_fwd", grid=(T // tm,),
        in_specs=[pl.BlockSpec((tm, D), lambda i: (i, 0)), _res((1, D)),
                  _res((NSHARD, D, CS), lambda i: (0, layer, 0))] + dep_spec,
        out_specs=[pl.BlockSpec((tm, D), lambda i: (i, 0)), pl.BlockSpec((tm, NSHARD * CS), lambda i: (i, 0))],
        out_shape=[_sds((T, D), BF16), _sds((T, NSHARD * CS), BF16)],
        compiler_params=_cparams(("parallel",)),
    )(x2d, g_mix, w_in_g, *dep_arg)


def _halo_maps(tm, n_rows):
    nb = tm // HALO
    last = n_rows // HALO - 1
    prev = lambda i: (jnp.maximum(i * nb - 1, 0), 0)
    nxt = lambda i: (jnp.minimum((i + 1) * nb, last), 0)
    return prev, nxt


def _dwconv(pad_ref, w_ref, out_ref, n_strips, tm, kw, rb):
    off = HALO - (kw - 1) // 2

    def strip(cs, carry):
        for r0 in range(0, tm, rb):
            acc = jnp.zeros((rb, LANES), F32)
            for k in range(kw):
                r = r0 + off + k
                acc = acc + w_ref[cs, k:k + 1, :] * pad_ref[cs, r:r + rb, :]
            out_ref[cs, r0:r0 + rb, :] = acc
        return carry

    lax.fori_loop(0, n_strips, strip, 0)


def _fill_c0_pad(pad_ref, pa_ref, pprev_ref, pnext_ref, D, tm, first, last):
    for cs in range(D // LANES):
        lo, hi = cs * LANES, (cs + 1) * LANES

        def c0_of(ref):
            return ref[:, lo:hi].astype(F32) * _sig(ref[:, D + lo:D + hi].astype(F32))

        pad_ref[cs, HALO:HALO + tm, :] = c0_of(pa_ref)
        pad_ref[cs, 0:HALO, :] = jnp.where(first, 0.0, c0_of(pprev_ref))
        pad_ref[cs, HALO + tm:HALO + tm + HALO, :] = jnp.where(last, 0.0, c0_of(pnext_ref))


def _conv_fwd(proj, conv_w_s, conv_b, ln_g, ln_b, w_sq_g, layer, seq, tm, rb):
    T = proj.shape[0]
    D = conv_b.shape[1]
    DQ = D // NSHARD
    NSTR = D // LANES
    KW = 31
    tps = seq // tm
    prev, nxt = _halo_maps(tm, T)

    def body(pa_ref, pprev_ref, pnext_ref, w_ref, b_ref, g_ref, be_ref, wco_ref,
             c1h_ref, rstd_ref, c3_ref, ya_ref, pad_ref, c1s_ref):
        i = pl.program_id(0)
        first = (i % tps) == 0
        last = (i % tps) == tps - 1
        _fill_c0_pad(pad_ref, pa_ref, pprev_ref, pnext_ref, D, tm, first, last)
        _dwconv(pad_ref, w_ref, c1s_ref, NSTR, tm, KW, rb)
        wco = wco_ref[...].reshape(D, D)
        for r0 in (0, tm // 2):
            rows = slice(r0, r0 + tm // 2)
            c1 = jnp.concatenate([c1s_ref[cs, rows, :] for cs in range(NSTR)], axis=1) + b_ref[...]
            mu = jnp.mean(c1, axis=-1, keepdims=True)
            cc = c1 - mu
            rstd = lax.rsqrt(jnp.mean(cc * cc, axis=-1, keepdims=True) + EPS)
            c1h = cc * rstd
            c1h_ref[rows, :] = c1h.astype(BF16)
            rstd_ref[rows, :] = rstd
            c2 = c1h * g_ref[...] + be_ref[...]
            c3 = (c2 * _sig(c2)).astype(BF16)
            c3_ref[rows, :] = c3
            ya_ref[rows, :] = _dot(c3, wco).astype(BF16)

    row = lambda i: (i, 0)
    return pl.pallas_call(
        body, name="conv_fwd", grid=(T // tm,),
        in_specs=[pl.BlockSpec((tm, 2 * D), row), pl.BlockSpec((HALO, 2 * D), prev), pl.BlockSpec((HALO, 2 * D), nxt),
                  _res((NSTR, 32, LANES)), _res((1, D)), _res((1, D)), _res((1, D)),
                  _res((NSHARD, DQ, D), lambda i: (0, layer * 3 + 0, 0))],
        out_specs=[pl.BlockSpec((tm, D), row), pl.BlockSpec((tm, 1), row), pl.BlockSpec((tm, D), row),
                   pl.BlockSpec((tm, D), row)],
        out_shape=[_sds((T, D), BF16), _sds((T, 1), F32), _sds((T, D), BF16), _sds((T, D), BF16)],
        scratch_shapes=[pltpu.VMEM((NSTR, tm + 2 * HALO, LANES), F32), pltpu.VMEM((NSTR, tm, LANES), F32)],
        compiler_params=_cparams(("parallel",)),
    )(proj, proj, proj, conv_w_s, conv_b, ln_g, ln_b, w_sq_g)


def _ffn_chunks(F):
    assert F % 256 == 0, F
    return [(c0, min(512, F - c0)) for c0 in range(0, F, 512)]


def _sgu_ffn_fwd(proj, ya, x2d, ln_g, ln_b, ws_b, bs_b, gate_bias, w_sq_g, layer, g_ffn, wgt, wut, wd, tm, head=None):
    T, D = x2d.shape
    DQ = D // NSHARD
    G, CH, _ = ws_b.shape
    GD = D // G
    F = wd.shape[0]
    n = T // tm
    n_in = 17 if head else 15

    def body(*refs):
        (puv_ref, pg_ref, ya_ref, x_ref, g_ref, be_ref, ws_ref, bsb_ref, gb_ref, wso_ref, wo_ref,
         gf_ref, wg_ref, wu_ref, wd_ref) = refs[:15]
        mixed_ref, gated_ref, yb_ref, merged_ref, x1_ref, h2_ref, gt_ref, up_ref, act_ref = refs[n_in:n_in + 9]
        mix_scr = refs[-2] if head else refs[-1]
        u = puv_ref[:, :D].astype(F32)
        v = puv_ref[:, D:].astype(F32)
        mu = jnp.mean(v, axis=-1, keepdims=True)
        vc = v - mu
        rstd = lax.rsqrt(jnp.mean(vc * vc, axis=-1, keepdims=True) + EPS)
        vn = (vc * rstd * g_ref[...] + be_ref[...]).astype(BF16)
        nch = tm // CH
        for g in range(G):
            cols = slice(g * GD, (g + 1) * GD)
            rhs = jnp.concatenate([vn[ch * CH:(ch + 1) * CH, cols] for ch in range(nch)], axis=1)
            res = _dot(ws_ref[g], rhs)
            for ch in range(nch):
                mix_scr[ch * CH:(ch + 1) * CH, cols] = res[:, ch * GD:(ch + 1) * GD] + bsb_ref[:, cols]
        mixed = mix_scr[...]
        mixed_ref[...] = mixed.astype(BF16)
        gated = (u * mixed).astype(BF16)
        gated_ref[...] = gated
        yb = _dot(gated, wso_ref[...].reshape(D, D))
        yb_ref[...] = yb.astype(BF16)
        sa = _sig(pg_ref[:, :D].astype(F32) + gb_ref[:, :D])
        sb = _sig(pg_ref[:, D:].astype(F32) + gb_ref[:, D:])
        merged = (sa * ya_ref[...].astype(F32) + sb * yb).astype(BF16)
        merged_ref[...] = merged
        x1 = x_ref[...] + _dot(merged, wo_ref[...].reshape(D, D))
        x1_ref[...] = x1

        rstd = lax.rsqrt(jnp.mean(x1 * x1, axis=-1, keepdims=True) + EPS)
        h2 = (x1 * rstd * gf_ref[...]).astype(BF16)
        h2_ref[...] = h2
        acc = x1
        chunks = _ffn_chunks(F)

        def gate_up(c0, cw):
            return _dot_nt(h2, wg_ref[c0:c0 + cw, :]), _dot_nt(h2, wu_ref[c0:c0 + cw, :])

        nxt = gate_up(*chunks[0])
        for ci, (c0, cw) in enumerate(chunks):
            gt, up = nxt
            if ci + 1 < len(chunks):
                nxt = gate_up(*chunks[ci + 1])
            gt_ref[:, c0:c0 + cw] = gt.astype(BF16)
            up_ref[:, c0:c0 + cw] = up.astype(BF16)
            act = (gt * _sig(gt) * up).astype(BF16)
            act_ref[:, c0:c0 + cw] = act
            acc = acc + _dot(act, wd_ref[c0:c0 + cw, :])
        if not head:
            refs[n_in + 9][...] = acc
            return

        gfin_ref, t_ref = refs[15:17]
        dx_ref, loss_ref, dgf_ref = refs[n_in + 9:n_in + 12]
        sq_ref = refs[-1]
        i = pl.program_id(0)

        @pl.when(i == 0)
        def _():
            sq_ref[...] = jnp.zeros_like(sq_ref)
            dgf_ref[...] = jnp.zeros_like(dgf_ref)

        gfin = gfin_ref[...]
        rstd2 = lax.rsqrt(jnp.mean(acc * acc, axis=-1, keepdims=True) + EPS)
        xh = acc * rstd2
        diff = xh * gfin - t_ref[...]
        sq_ref[...] += jnp.sum(diff * diff, axis=0, keepdims=True)
        dy = diff * (1.0 / D)
        dgf_ref[...] += jnp.sum(dy * xh, axis=0, keepdims=True)
        dxh = dy * gfin
        dx_ref[...] = rstd2 * (dxh - xh * jnp.mean(dxh * xh, axis=-1, keepdims=True))

        @pl.when(i == n - 1)
        def _():
            tot = jnp.sum(sq_ref[...], axis=-1, keepdims=True) * (0.5 / D)
            loss_ref[...] = jnp.broadcast_to(tot, loss_ref.shape)

    row = lambda i: (i, 0)
    fixed2 = lambda i: (0, 0)
    wide = pl.BlockSpec((tm, F), row)
    in_specs = [pl.BlockSpec((tm, 2 * D), lambda i: (i, 1)), pl.BlockSpec((tm, 2 * D), lambda i: (i, 2)),
                pl.BlockSpec((tm, D), row), pl.BlockSpec((tm, D), row),
                _res((1, D)), _res((1, D)), _res((G, CH, CH)), _res((CH, D)), _res((1, 2 * D)),
                _res((NSHARD, DQ, D), lambda i: (0, layer * 3 + 1, 0)),
                _res((NSHARD, DQ, D), lambda i: (0, layer * 3 + 2, 0)),
                _res((1, D)), _res((F, D)), _res((F, D)), _res((F, D))]
    out_specs = [pl.BlockSpec((tm, D), row)] * 6 + [wide, wide, wide, pl.BlockSpec((tm, D), row)]
    out_shape = ([_sds((T, D), BF16)] * 4 + [_sds((T, D), F32), _sds((T, D), BF16)] + [_sds((T, F), BF16)] * 3
                 + [_sds((T, D), F32)])
    scratch = [pltpu.VMEM((tm, D), F32)]
    args = [proj, proj, ya, x2d, ln_g, ln_b, ws_b, bs_b, gate_bias, w_sq_g, w_sq_g, g_ffn, wgt, wut, wd]
    if head:
        in_specs += [_res((1, D)), pl.BlockSpec((tm, D), row)]
        out_specs += [pl.BlockSpec((1, LANES), fixed2), pl.BlockSpec((1, D), fixed2)]
        out_shape += [_sds((1, LANES), F32), _sds((1, D), F32)]
        scratch += [pltpu.VMEM((1, D), F32)]
        args += list(head)
    return pl.pallas_call(
        body, name="sgu_ffn_fwd_head" if head else "sgu_ffn_fwd", grid=(n,),
        in_specs=in_specs, out_specs=out_specs, out_shape=out_shape, scratch_shapes=scratch,
        compiler_params=_cparams(("arbitrary",) if head else ("parallel",)),
    )(*args)


def _ffn_bwd(dx2, x1, gt, up, g_ffn, wgt, wut, wd, tm, dep=None):
    T, D = x1.shape
    F = wd.shape[0]

    def body(dx2_ref, x1_ref, gt_ref, up_ref, g_ref, wg_ref, wu_ref, wd_ref, dx1_ref, dgt_ref, dup_ref, dg_ref):
        i = pl.program_id(0)

        @pl.when(i == 0)
        def _():
            dg_ref[...] = jnp.zeros_like(dg_ref)

        dx2 = dx2_ref[...]
        dx2b = dx2.astype(BF16)
        dh2 = jnp.zeros((tm, D), F32)
        chunks = _ffn_chunks(F)
        dact_next = _dot_nt(dx2b, wd_ref[0:chunks[0][1], :])
        for ci, (c0, cw) in enumerate(chunks):
            dact = dact_next
            if ci + 1 < len(chunks):
                n0, nw = chunks[ci + 1]
                dact_next = _dot_nt(dx2b, wd_ref[n0:n0 + nw, :])
            g = gt_ref[:, c0:c0 + cw].astype(F32)
            u = up_ref[:, c0:c0 + cw].astype(F32)
            sg = _sig(g)
            dup = (dact * (g * sg)).astype(BF16)
            dgt = (dact * u * (sg * (1.0 + g * (1.0 - sg)))).astype(BF16)
            dgt_ref[:, c0:c0 + cw] = dgt
            dup_ref[:, c0:c0 + cw] = dup
            dh2 = dh2 + _dot(dgt, wg_ref[c0:c0 + cw, :]) + _dot(dup, wu_ref[c0:c0 + cw, :])
        x = x1_ref[...]
        rstd = lax.rsqrt(jnp.mean(x * x, axis=-1, keepdims=True) + EPS)
        xh = x * rstd
        dg_ref[...] += jnp.sum(dh2 * xh, axis=0, keepdims=True)
        dxh = dh2 * g_ref[...]
        dx1_ref[...] = dx2 + rstd * (dxh - xh * jnp.mean(dxh * xh, axis=-1, keepdims=True))

    row = lambda i: (i, 0)
    body, dep_spec, dep_arg = _after(body, 8, dep)
    return pl.pallas_call(
        body, name="ffn_bwd", grid=(T // tm,),
        in_specs=[pl.BlockSpec((tm, D), row), pl.BlockSpec((tm, D), row), pl.BlockSpec((tm, F), row),
                  pl.BlockSpec((tm, F), row), _res((1, D)), _res((F, D)), _res((F, D)), _res((F, D))] + dep_spec,
        out_specs=[pl.BlockSpec((tm, D), row), pl.BlockSpec((tm, F), row), pl.BlockSpec((tm, F), row),
                   pl.BlockSpec((1, D), lambda i: (0, 0))],
        out_shape=[_sds((T, D), F32), _sds((T, F), BF16), _sds((T, F), BF16), _sds((1, D), F32)],
        compiler_params=_cparams(("arbitrary",)),
    )(dx2, x1, gt, up, g_ffn, wgt, wut, wd, *dep_arg)


def _merge_sgu_bwd(dx1, proj, ya, yb, mixed, c1h, rstd_c, ln_g, ln_b, ws_b, wst_b, gate_bias, cln_g, cln_b, w_sq_g,
                   layer, tm, dep=None):
    T, D = dx1.shape
    DQ = D // NSHARD
    G, CH, _ = ws_b.shape
    GD = D // G

    def body(dx1_ref, puv_ref, pg_ref, ya_ref, yb_ref, mixed_ref, c1h_ref, rstdc_ref, g_ref, be_ref, ws_ref, wst_ref,
             gb_ref, cg_ref, cbe_ref, wco_ref, wso_ref, wo_ref,
             dya_ref, dyb_ref, dp_ref, dc1_ref, dgb_ref, dlg_ref, dlb_ref, dbs_ref, dws_ref, dcg_ref, dcbe_ref,
             dcb_ref, dvn_scr, dbs_scr):
        i = pl.program_id(0)

        @pl.when(i == 0)
        def _():
            for r in (dgb_ref, dlg_ref, dlb_ref, dws_ref, dbs_scr, dcg_ref, dcbe_ref, dcb_ref):
                r[...] = jnp.zeros_like(r)

        dmerged = _dot_nt(dx1_ref[...].astype(BF16), wo_ref[...].reshape(D, D))
        sa = _sig(pg_ref[:, :D].astype(F32) + gb_ref[:, :D])
        sb = _sig(pg_ref[:, D:].astype(F32) + gb_ref[:, D:])
        dya = (dmerged * sa).astype(BF16)
        dyb = (dmerged * sb).astype(BF16)
        dya_ref[...] = dya
        dyb_ref[...] = dyb
        dc3 = _dot_nt(dya, wco_ref[...].reshape(D, D))
        dgated = _dot_nt(dyb, wso_ref[...].reshape(D, D))

        c1h = c1h_ref[...].astype(F32)
        c2 = c1h * cg_ref[...] + cbe_ref[...]
        sg = _sig(c2)
        dc2 = dc3 * (sg * (1.0 + c2 * (1.0 - sg)))
        dcg_ref[...] += jnp.sum(dc2 * c1h, axis=0, keepdims=True)
        dcbe_ref[...] += jnp.sum(dc2, axis=0, keepdims=True)
        dch = dc2 * cg_ref[...]
        dc1 = rstdc_ref[...] * (dch - jnp.mean(dch, axis=-1, keepdims=True)
                                - c1h * jnp.mean(dch * c1h, axis=-1, keepdims=True))
        dc1_ref[...] = dc1
        dcb_ref[...] += jnp.sum(dc1, axis=0, keepdims=True)

        dga = dmerged * ya_ref[...].astype(F32) * (sa * (1.0 - sa))
        dgb = dmerged * yb_ref[...].astype(F32) * (sb * (1.0 - sb))
        dp_ref[:, 4 * D:5 * D] = dga.astype(BF16)
        dp_ref[:, 5 * D:6 * D] = dgb.astype(BF16)
        dgb_ref[:, :D] += jnp.sum(dga, axis=0, keepdims=True)
        dgb_ref[:, D:] += jnp.sum(dgb, axis=0, keepdims=True)

        u = puv_ref[:, :D].astype(F32)
        v = puv_ref[:, D:].astype(F32)
        dp_ref[:, 2 * D:3 * D] = (dgated * mixed_ref[...].astype(F32)).astype(BF16)
        dmixed = dgated * u
        mu = jnp.mean(v, axis=-1, keepdims=True)
        vc = v - mu
        rstd = lax.rsqrt(jnp.mean(vc * vc, axis=-1, keepdims=True) + EPS)
        vh = vc * rstd
        vn = (vh * g_ref[...] + be_ref[...]).astype(BF16)
        dmb = dmixed.astype(BF16)
        nch = tm // CH
        bs_part = dmixed[0:CH, :]
        for ch in range(1, nch):
            bs_part = bs_part + dmixed[ch * CH:(ch + 1) * CH, :]
        dbs_scr[...] += bs_part
        for g in range(G):
            cols = slice(g * GD, (g + 1) * GD)
            dm_g = jnp.concatenate([dmb[ch * CH:(ch + 1) * CH, cols] for ch in range(nch)], axis=1)
            vn_g = jnp.concatenate([vn[ch * CH:(ch + 1) * CH, cols] for ch in range(nch)], axis=1)
            dws_ref[g] += _dot_nt(dm_g, vn_g)
            dvn_g = _dot(wst_ref[g], dm_g)
            for ch in range(nch):
                dvn_scr[ch * CH:(ch + 1) * CH, cols] = dvn_g[:, ch * GD:(ch + 1) * GD]
        dvn = dvn_scr[...]
        dlg_ref[...] += jnp.sum(dvn * vh, axis=0, keepdims=True)
        dlb_ref[...] += jnp.sum(dvn, axis=0, keepdims=True)
        dxh = dvn * g_ref[...]
        dv = rstd * (dxh - jnp.mean(dxh, axis=-1, keepdims=True) - vh * jnp.mean(dxh * vh, axis=-1, keepdims=True))
        dp_ref[:, 3 * D:4 * D] = dv.astype(BF16)

        @pl.when(i == pl.num_programs(0) - 1)
        def _():
            for g in range(G):
                blk = dbs_scr[:, g * GD:(g + 1) * GD]
                if GD != CH:
                    blk = jnp.concatenate([blk, jnp.zeros((CH, CH - GD), F32)], axis=1)
                dbs_ref[:, g * CH:(g + 1) * CH] = jnp.sum(blk.T, axis=0, keepdims=True)

    row = lambda i: (i, 0)
    fixed2 = lambda i: (0, 0)
    body, dep_spec, dep_arg = _after(body, 18, dep)
    vec = pl.BlockSpec((1, D), fixed2)
    return pl.pallas_call(
        body, name="merge_sgu_bwd", grid=(T // tm,),
        in_specs=[pl.BlockSpec((tm, D), row), pl.BlockSpec((tm, 2 * D), lambda i: (i, 1)),
                  pl.BlockSpec((tm, 2 * D), lambda i: (i, 2)), pl.BlockSpec((tm, D), row), pl.BlockSpec((tm, D), row),
                  pl.BlockSpec((tm, D), row), pl.BlockSpec((tm, D), row), pl.BlockSpec((tm, 1), row),
                  _res((1, D)), _res((1, D)), _res((G, CH, CH)), _res((G, CH, CH)), _res((1, 2 * D)),
                  _res((1, D)), _res((1, D)),
                  _res((NSHARD, DQ, D), lambda i: (0, layer * 3 + 0, 0)),
                  _res((NSHARD, DQ, D), lambda i: (0, layer * 3 + 1, 0)),
                  _res((NSHARD, DQ, D), lambda i: (0, layer * 3 + 2, 0))] + dep_spec,
        out_specs=[pl.BlockSpec((tm, D), row), pl.BlockSpec((tm, D), row), pl.BlockSpec((tm, 6 * D), row),
                   pl.BlockSpec((tm, D), row), pl.BlockSpec((1, 2 * D), fixed2), vec, vec,
                   pl.BlockSpec((1, G * CH), fixed2), pl.BlockSpec((G, CH, CH), lambda i: (0, 0, 0)), vec, vec, vec],
        out_shape=[_sds((T, D), BF16), _sds((T, D), BF16), _sds((T, 6 * D), BF16), _sds((T, D), F32),
                   _sds((1, 2 * D), F32), _sds((1, D), F32), _sds((1, D), F32), _sds((1, G * CH), F32),
                   _sds((G, CH, CH), F32), _sds((1, D), F32), _sds((1, D), F32), _sds((1, D), F32)],
        scratch_shapes=[pltpu.VMEM((tm, D), F32), pltpu.VMEM((CH, D), F32)],
        compiler_params=_cparams(("arbitrary",)),
    )(dx1, proj, proj, ya, yb, mixed, c1h, rstd_c, ln_g, ln_b, ws_b, wst_b, gate_bias, cln_g, cln_b,
      w_sq_g, w_sq_g, w_sq_g, *dep_arg)


def _conv_bwd(dc1, proj, dp3, conv_wf_s, seq, tm, rb, dep=None):
    T, D = dc1.shape
    NSTR = D // LANES
    KW = 31
    PADK = (KW - 1) // 2
    tps = seq // tm
    prev, nxt = _halo_maps(tm, T)
    n = T // tm

    def body(dc_ref, dcprev_ref, dcnext_ref, pa_ref, pprev_ref, pnext_ref, wf_ref, dp_in_ref,
             dp_ref, dw_ref, pad_ref, dpad_ref, dc0_ref, dwacc_ref):
        del dp_in_ref
        i = pl.program_id(0)
        first = (i % tps) == 0
        last = (i % tps) == tps - 1

        @pl.when(i == 0)
        def _():
            dwacc_ref[...] = jnp.zeros_like(dwacc_ref)

        _fill_c0_pad(pad_ref, pa_ref, pprev_ref, pnext_ref, D, tm, first, last)
        for cs in range(NSTR):
            lo, hi = cs * LANES, (cs + 1) * LANES
            dpad_ref[cs, HALO:HALO + tm, :] = dc_ref[:, lo:hi]
            dpad_ref[cs, 0:HALO, :] = jnp.where(first, 0.0, dcprev_ref[:, lo:hi])
            dpad_ref[cs, HALO + tm:HALO + tm + HALO, :] = jnp.where(last, 0.0, dcnext_ref[:, lo:hi])
        _dwconv(dpad_ref, wf_ref, dc0_ref, NSTR, tm, KW, rb)

        def strip(cs, carry):
            for r0 in range(0, tm, rb):
                d = dpad_ref[cs, HALO + r0:HALO + r0 + rb, :]
                for k in range(KW):
                    r = r0 + HALO - PADK + k
                    prod = d * pad_ref[cs, r:r + rb, :]
                    dwacc_ref[cs, k * 8:(k + 1) * 8, :] += jnp.sum(prod.reshape(rb // 8, 8, LANES), axis=0)
            return carry

        lax.fori_loop(0, NSTR, strip, 0)

        for cs in range(NSTR):
            lo, hi = cs * LANES, (cs + 1) * LANES
            av = pa_ref[:, lo:hi].astype(F32)
            sg = _sig(pa_ref[:, D + lo:D + hi].astype(F32))
            dc0 = dc0_ref[cs]
            dp_ref[:, lo:hi] = (dc0 * sg).astype(BF16)
            dp_ref[:, D + lo:D + hi] = (dc0 * av * (sg * (1.0 - sg))).astype(BF16)

        @pl.when(i == n - 1)
        def _():
            for cs in range(NSTR):
                dw_ref[cs] = jnp.sum(dwacc_ref[cs].reshape(32, 8, LANES), axis=1)

    row = lambda i: (i, 0)
    body, dep_spec, dep_arg = _after(body, 8, dep)
    return pl.pallas_call(
        body, name="conv_bwd", grid=(n,),
        in_specs=[pl.BlockSpec((tm, D), row), pl.BlockSpec((HALO, D), prev), pl.BlockSpec((HALO, D), nxt),
                  pl.BlockSpec((tm, 2 * D), row), pl.BlockSpec((HALO, 2 * D), prev), pl.BlockSpec((HALO, 2 * D), nxt),
                  _res((NSTR, 32, LANES)), pl.BlockSpec(memory_space=pl.ANY)] + dep_spec,
        out_specs=[pl.BlockSpec((tm, 2 * D), row),
                   pl.BlockSpec((NSTR, 32, LANES), lambda i: (0, 0, 0))],
        out_shape=[_sds(dp3.shape, BF16), _sds((NSTR, 32, LANES), F32)],
        scratch_shapes=[pltpu.VMEM((NSTR, tm + 2 * HALO, LANES), F32), pltpu.VMEM((NSTR, tm + 2 * HALO, LANES), F32),
                        pltpu.VMEM((NSTR, tm, LANES), F32), pltpu.VMEM((NSTR, 32 * 8, LANES), F32)],
        input_output_aliases={7: 0},
        compiler_params=_cparams(("arbitrary",)),
    )(dc1, dc1, dc1, proj, proj, proj, conv_wf_s, dp3, *dep_arg)


def _mix_in_bwd(dx1, dp3, x2d, g_mix, w_in_g, layer, tm, dep=None):
    T, D = x2d.shape
    CS = w_in_g.shape[2]
    CN = CS // 3

    def body(dx1_ref, dp_ref, x_ref, g_ref, w_ref, dx_ref, dg_ref):
        i = pl.program_id(0)

        @pl.when(i == 0)
        def _():
            dg_ref[...] = jnp.zeros_like(dg_ref)

        dh = jnp.zeros((tm, D), F32)
        for j in range(12):
            dh = dh + _dot_nt(dp_ref[:, j * CN:(j + 1) * CN], w_ref[j // 3, :, (j % 3) * CN:(j % 3 + 1) * CN])
        x = x_ref[...]
        rstd = lax.rsqrt(jnp.mean(x * x, axis=-1, keepdims=True) + EPS)
        xh = x * rstd
        dg_ref[...] += jnp.sum(dh * xh, axis=0, keepdims=True)
        dxh = dh * g_ref[...]
        dx_ref[...] = dx1_ref[...] + rstd * (dxh - xh * jnp.mean(dxh * xh, axis=-1, keepdims=True))

    row = lambda i: (i, 0)
    body, dep_spec, dep_arg = _after(body, 5, dep)
    return pl.pallas_call(
        body, name="mix_in_bwd", grid=(T // tm,),
        in_specs=[pl.BlockSpec((tm, D), row), pl.BlockSpec((tm, 6 * D), row),
                  pl.BlockSpec((tm, D), row), _res((1, D)), _res((NSHARD, D, CS), lambda i: (0, layer, 0))] + dep_spec,
        out_specs=[pl.BlockSpec((tm, D), row), pl.BlockSpec((1, D), lambda i: (0, 0))],
        out_shape=[_sds((T, D), F32), _sds((1, D), F32)],
        compiler_params=_cparams(("arbitrary",)),
    )(dx1, dp3, x2d, g_mix, w_in_g, *dep_arg)


def _tn_matmul(name, a, a_block, a_map, bs, b_block, b_map, out_shape, out_block, out_map, nj, nt):
    kk = [d for d in a_block if d is not None][-1]
    nn = [d for d in b_block if d is not None][-1]
    nb = len(bs)
    a_list = a if isinstance(a, (list, tuple)) else [a]
    na = len(a_list)

    def body(*refs):
        a_refs, b_refs = refs[:na], refs[na:na + nb]
        o_refs, acc_refs = refs[-2 * nb:-nb], refs[-nb:]
        t = pl.program_id(1)

        @pl.when(t == 0)
        def _():
            for acc_ref in acc_refs:
                acc_ref[...] = jnp.zeros_like(acc_ref)

        a_ts = [a_ref[...].astype(BF16) for a_ref in a_refs]
        for i, (b_ref, acc_ref) in enumerate(zip(b_refs, acc_refs)):
            acc_ref[...] += _dot_tn(a_ts[i % na], b_ref[...].astype(BF16))

        @pl.when(t == nt - 1)
        def _():
            for o_ref, acc_ref in zip(o_refs, acc_refs):
                o_ref[...] = acc_ref[...].astype(o_ref.dtype)

    return pl.pallas_call(
        body, name=name, grid=(nj, nt),
        in_specs=[pl.BlockSpec(a_block, a_map)] * na + [pl.BlockSpec(b_block, b_map)] * nb,
        out_specs=[pl.BlockSpec(out_block, out_map)] * nb, out_shape=[_sds(out_shape, BF16)] * nb,
        scratch_shapes=[pltpu.VMEM((kk, nn), F32)] * nb,
        compiler_params=_cparams(("parallel", "arbitrary")),
    )(*a_list, *bs)


def _place_shard(name, w, pos, dtype, tr, dep=None, layer=None, into=None, row_off=0, out_rows=None):
    R, C = w.shape[-2:]
    out_rows = out_rows or R

    def body(pos_ref, w_ref, *rest):
        del pos_ref
        rest[-1][...] = w_ref[...].astype(dtype)

    in_spec = (pl.BlockSpec((tr, C), lambda r, pos: (r, 0)) if layer is None else
               pl.BlockSpec((None, tr, C), lambda r, pos: (layer, r, 0)))
    extra, extra_args = ([], []) if into is None else ([pl.BlockSpec(memory_space=pl.ANY)], [into])
    body, dep_spec, dep_arg = _after(body, 2 + len(extra), dep)
    grid_spec = pltpu.PrefetchScalarGridSpec(
        num_scalar_prefetch=1, grid=(R // tr,), in_specs=[in_spec] + extra + dep_spec,
        out_specs=pl.BlockSpec((None, tr, C), lambda r, pos: (pos[1], row_off // tr + r, 0)))
    return pl.pallas_call(body, name=name, grid_spec=grid_spec, out_shape=_sds((NSHARD, out_rows, C), dtype),
                          input_output_aliases={} if into is None else {2: 0},
                          compiler_params=_cparams(("parallel",)))(pos, w, *extra_args, *dep_arg)


def _add_halves(name, g, rbuf, pos, tr):
    NS, _, H, C = g.shape

    def body(pos_ref, g_ref, r_ref, o_ref):
        del pos_ref
        o_ref[...] = (g_ref[...].astype(F32) + r_ref[...].astype(F32)).astype(BF16)

    grid_spec = pltpu.PrefetchScalarGridSpec(
        num_scalar_prefetch=1, grid=(NS, H // tr),
        in_specs=[pl.BlockSpec((None, None, tr, C), lambda s, r, pos: (s, pos[0], r, 0)),
                  pl.BlockSpec((None, tr, C), lambda s, r, pos: (s, r, 0))],
        out_specs=pl.BlockSpec((None, tr, C), lambda s, r, pos: (s, r, 0)))
    return pl.pallas_call(body, name=name, grid_spec=grid_spec, out_shape=_sds((NS, H, C), BF16),
                          compiler_params=_cparams(("parallel", "parallel")))(pos, g, rbuf)


def _add_shards(name, p, rbuf, pos, tr, layer, n_layers, prev):
    _, H, C = p.shape

    def body(pos_ref, p_ref, r_ref, *rest):
        del pos_ref
        o_ref = rest[-1]
        acc = p_ref[...].astype(F32)
        for j in range(3):
            acc = acc + r_ref[j].astype(F32)
        o_ref[...] = acc

    in_specs = [pl.BlockSpec((None, tr, C), lambda r, pos: (pos[1], r, 0)),
                pl.BlockSpec((3, tr, C), lambda r, pos: (0, r, 0))]
    args = [pos, p, rbuf]
    aliases = {}
    if prev is not None:
        in_specs.append(pl.BlockSpec(memory_space=pl.ANY))
        args.append(prev)
        aliases = {3: 0}
    grid_spec = pltpu.PrefetchScalarGridSpec(
        num_scalar_prefetch=1, grid=(H // tr,), in_specs=in_specs,
        out_specs=pl.BlockSpec((None, None, tr, C), lambda r, pos: (layer, pos[0], r, 0)))
    return pl.pallas_call(body, name=name, grid_spec=grid_spec, out_shape=_sds((n_layers, 2, H, C), F32),
                          input_output_aliases=aliases, compiler_params=_cparams(("parallel",)))(*args)


def _sum_slots(own, land, me, tr):
    NS8, R, C = land.shape

    def body(me_ref, own_ref, l_ref, o_ref):
        acc = None
        for j in range(NS8):
            term = jnp.where(me_ref[0] == j, own_ref[...], l_ref[j])
            acc = term if acc is None else acc + term
        o_ref[...] = acc

    grid_spec = pltpu.PrefetchScalarGridSpec(
        num_scalar_prefetch=1, grid=(R // tr,),
        in_specs=[pl.BlockSpec((tr, C), lambda i, me: (i, 0)), pl.BlockSpec((NS8, tr, C), lambda i, me: (0, i, 0))],
        out_specs=pl.BlockSpec((tr, C), lambda i, me: (i, 0)))
    return pl.pallas_call(body, name="sum_slots", grid_spec=grid_spec, out_shape=_sds((R, C), F32),
                          compiler_params=_cparams(("parallel",)))(me, own, land)


def _adamw_update(w_ref, g_ref, m_ref, v_ref, d_ref, mo_ref, vo_ref):
    g_ = g_ref[...]
    m_ = ADAM_B1 * m_ref[...] + (1.0 - ADAM_B1) * g_
    v_ = ADAM_B2 * v_ref[...] + (1.0 - ADAM_B2) * (g_ * g_)
    mo_ref[...] = m_
    vo_ref[...] = v_
    m_hat = m_ / (1.0 - ADAM_B1 ** ADAM_STEP)
    v_hat = v_ / (1.0 - ADAM_B2 ** ADAM_STEP)
    d_ref[...] = -ADAM_LR * (m_hat / (jnp.sqrt(v_hat) + ADAM_EPS) + ADAM_WD * w_ref[...])


def _adamw(name, w, g, m, v, tr, emit_g):
    R, C = w.shape
    n_out = 4 if emit_g else 3

    def body(w_ref, g_ref, m_ref, v_ref, d_ref, mo_ref, vo_ref, *go_ref):
        _adamw_update(w_ref, g_ref, m_ref, v_ref, d_ref, mo_ref, vo_ref)
        if emit_g:
            go_ref[0][...] = g_ref[...]

    spec = pl.BlockSpec((tr, C), lambda i: (i, 0))
    return pl.pallas_call(
        body, name=name, grid=(R // tr,), in_specs=[spec] * 4, out_specs=[spec] * n_out,
        out_shape=[_sds((R, C), F32)] * n_out, compiler_params=_cparams(("parallel",)))(w, g, m, v)


def _adamw_many(ws, gs, ms, vs):
    n = len(ws)

    def body(*refs):
        ins, outs = refs[:4 * n], refs[4 * n:]
        for k in range(n):
            _adamw_update(ins[k], ins[n + k], ins[2 * n + k], ins[3 * n + k], outs[k], outs[n + k], outs[2 * n + k])

    vmem = pl.BlockSpec(memory_space=pltpu.VMEM)
    res = pl.pallas_call(
        body, name="adamw_small", in_specs=[vmem] * (4 * n), out_specs=[vmem] * (3 * n),
        out_shape=[_sds(w.shape, F32) for w in ws] * 3,
        compiler_params=pltpu.CompilerParams(vmem_limit_bytes=VMEM_LIMIT))(*ws, *gs, *ms, *vs)
    return list(res[:n]), list(res[n:2 * n]), list(res[2 * n:])


def _row_tile(rows, cap):
    best = rows
    for t in range(8, min(rows, cap) + 1, 8):
        if rows % t == 0:
            best = t
    return best


HBM_SPEC = pl.BlockSpec(memory_space=pltpu.HBM)
SEM_SPEC = pl.BlockSpec(memory_space=pltpu.SEMAPHORE)
DATAFLOW = pltpu.SideEffectType.DATAFLOW_SIDE_EFFECTING
DMA_SEM = pltpu.SemaphoreType.DMA


def _hbm(a):
    return pltpu.with_memory_space_constraint(a, pltpu.HBM)


def _mesh_pos():
    return lax.axis_index("x"), lax.axis_index("y"), lax.axis_index("c")


def _other_chips(x, y):
    return [(1 - x, y), (x, 1 - y), (1 - x, 1 - y)]


def _half_rows(buf, shard, core):
    h = buf.shape[1] // 2
    return buf.at[shard, pl.ds(core * h, h), :]


def _ici_copy(buf, j, send, recv, landing):
    x, y, c = _mesh_pos()
    px, py = _other_chips(x, y)[j]
    part = _half_rows(buf, 2 * px + py if landing else 2 * x + y, c)
    return pltpu.make_async_remote_copy(src_ref=part, dst_ref=part, send_sem=send, recv_sem=recv,
                                        device_id=(px, py, c), device_id_type=MESH_ID)


def _sibling_copy(buf, j, send, recv, landing):
    x, y, c = _mesh_pos()
    px, py = _other_chips(x, y)[j]
    part = _half_rows(buf, 2 * px + py, 1 - c if landing else c)
    return pltpu.make_async_remote_copy(src_ref=part, dst_ref=part, send_sem=send, recv_sem=recv,
                                        device_id=(x, y, 1 - c), device_id_type=MESH_ID)


def _forward_sibling(name, bufs, with_ici):
    n = len(bufs)

    def body(*refs):
        ins = refs[:n]
        send_ici, recv_ici, send_d2d, recv_d2d = refs[2 * n:]
        sends = []
        if with_ici:
            for i in range(n):
                for j in range(3):
                    cp = _ici_copy(ins[i], j, send_ici.at[i, j], recv_ici.at[i, j], False)
                    cp.start()
                    sends.append(cp)
        for i in range(n):
            for j in range(3):
                if with_ici:
                    _ici_copy(ins[i], j, send_ici.at[i, j], recv_ici.at[i, j], True).wait_recv()
                cp = _sibling_copy(ins[i], j, send_d2d.at[i, j], recv_d2d.at[i, j], False)
                cp.start()
                sends.append(cp)
        for i in range(n):
            for j in range(3):
                _sibling_copy(ins[i], j, send_d2d.at[i, j], recv_d2d.at[i, j], True).wait_recv()
        for cp in sends:
            cp.wait_send()

    return pl.pallas_call(
        body, name=name, in_specs=[HBM_SPEC] * n, out_specs=[HBM_SPEC] * n,
        out_shape=[_sds(b.shape, b.dtype) for b in bufs],
        scratch_shapes=[DMA_SEM((n, 3))] * 4, input_output_aliases={i: i for i in range(n)},
    )(*bufs)


def _gather_start(name, groups):
    flat = [b for g in groups for b in g]
    n, ng = len(flat), len(groups)

    def body(*refs):
        ins, sems, token = refs[:n], refs[n:n + 2 * ng], refs[-1]
        k = 0
        for gi, g in enumerate(groups):
            for a in range(len(g)):
                for j in range(3):
                    _ici_copy(ins[k], j, sems[2 * gi], sems[2 * gi + 1], False).start()
                k += 1
        token[...] = jnp.zeros_like(token)

    res = pl.pallas_call(
        body, name=name, in_specs=[HBM_SPEC] * n,
        out_specs=[SEM_SPEC] * (2 * ng) + [HBM_SPEC] * n + [pl.BlockSpec(memory_space=pltpu.VMEM)],
        out_shape=[DMA_SEM(()) for g in groups for _ in range(2)]
        + [pltpu.HBM(b.shape, b.dtype) for b in flat] + [_sds((8, LANES), F32)],
        input_output_aliases={i: 2 * ng + i for i in range(n)},
        compiler_params=pltpu.CompilerParams(has_side_effects=DATAFLOW),
    )(*[_hbm(b) for b in flat])
    sems = [(res[2 * gi], res[2 * gi + 1]) for gi in range(ng)]
    thru, k = [], 2 * ng
    for g in groups:
        thru.append(list(res[k:k + len(g)]))
        k += len(g)
    return sems, thru, res[-1]


def _gather_wait(name, bufs, sems, after):
    n = len(bufs)

    def body(*refs):
        ins, send, recv = refs[:n], refs[n], refs[n + 1]
        for a in range(n):
            for j in range(3):
                _ici_copy(ins[a], j, send, recv, False).wait_send()
                _ici_copy(ins[a], j, send, recv, True).wait_recv()

    return pl.pallas_call(
        body, name=name, in_specs=[HBM_SPEC] * n + [SEM_SPEC, SEM_SPEC, pl.BlockSpec(memory_space=pl.ANY)],
        out_specs=[HBM_SPEC] * n, out_shape=[pltpu.HBM(b.shape, b.dtype) for b in bufs],
        input_output_aliases={i: i for i in range(n)},
        compiler_params=pltpu.CompilerParams(has_side_effects=DATAFLOW),
    )(*bufs, sems[0], sems[1], after)


def _send_sibling_halves(name, arrs):
    n = len(arrs)

    def body(*refs):
        ins, outs = refs[:n], refs[n:2 * n]
        send, recv = refs[2 * n:]
        x, y, c = _mesh_pos()
        cps = []
        for i in range(n):
            cp = pltpu.make_async_remote_copy(
                src_ref=ins[i].at[:, 1 - c], dst_ref=outs[i],
                send_sem=send.at[i], recv_sem=recv.at[i], device_id=(x, y, 1 - c), device_id_type=MESH_ID)
            cp.start()
            cps.append(cp)
        for cp in cps:
            cp.wait()

    return pl.pallas_call(
        body, name=name, in_specs=[HBM_SPEC] * n, out_specs=[HBM_SPEC] * n,
        out_shape=[_sds((a.shape[0],) + a.shape[2:], a.dtype) for a in arrs],
        scratch_shapes=[DMA_SEM((n,)), DMA_SEM((n,))],
    )(*arrs)


def _chip_copy(p, land, j, send, recv):
    x, y, c = _mesh_pos()
    px, py = _other_chips(x, y)[j]
    return pltpu.make_async_remote_copy(src_ref=p.at[2 * px + py], dst_ref=land.at[j], send_sem=send, recv_sem=recv,
                                        device_id=(px, py, c), device_id_type=MESH_ID)


def _chip_send_start(name, ps):
    n = len(ps)
    lands = [lax.empty((3,) + p.shape[1:], p.dtype) for p in ps]

    def body(*refs):
        ins, lnd, send, recv, token = refs[:n], refs[n:2 * n], refs[2 * n], refs[2 * n + 1], refs[-1]
        for i in range(n):
            for j in range(3):
                _chip_copy(ins[i], lnd[i], j, send, recv).start()
        token[...] = jnp.zeros_like(token)

    res = pl.pallas_call(
        body, name=name, in_specs=[HBM_SPEC] * (2 * n),
        out_specs=[SEM_SPEC, SEM_SPEC] + [HBM_SPEC] * (2 * n) + [pl.BlockSpec(memory_space=pltpu.VMEM)],
        out_shape=[DMA_SEM(()), DMA_SEM(())] + [pltpu.HBM(a.shape, a.dtype) for a in ps + lands]
        + [_sds((8, LANES), F32)],
        input_output_aliases={i: 2 + i for i in range(2 * n)},
        compiler_params=pltpu.CompilerParams(has_side_effects=DATAFLOW),
    )(*[_hbm(a) for a in ps + lands])
    return (res[0], res[1]), list(res[2:2 + n]), list(res[2 + n:2 + 2 * n]), res[-1]


def _chip_send_wait(name, ps, lands, sems, after):
    n = len(ps)

    def body(*refs):
        ins, lnd, send, recv = refs[:n], refs[n:2 * n], refs[2 * n], refs[2 * n + 1]
        for i in range(n):
            for j in range(3):
                cp = _chip_copy(ins[i], lnd[i], j, send, recv)
                cp.wait_send()
                cp.wait_recv()

    res = pl.pallas_call(
        body, name=name, in_specs=[HBM_SPEC] * (2 * n) + [SEM_SPEC, SEM_SPEC, pl.BlockSpec(memory_space=pl.ANY)],
        out_specs=[HBM_SPEC] * (2 * n), out_shape=[pltpu.HBM(a.shape, a.dtype) for a in ps + lands],
        input_output_aliases={i: i for i in range(2 * n)},
        compiler_params=pltpu.CompilerParams(has_side_effects=DATAFLOW),
    )(*ps, *lands, sems[0], sems[1], after)
    return list(res[:n]), list(res[n:])


def _join_halves(arrs):
    n = len(arrs)

    def body(*refs):
        bufs = refs[n:2 * n]
        send, recv = refs[2 * n:]
        x, y, c = _mesh_pos()
        cps = []
        for i in range(n):
            mine = bufs[i].at[:, c]
            cp = pltpu.make_async_remote_copy(
                src_ref=mine, dst_ref=mine, send_sem=send.at[i], recv_sem=recv.at[i],
                device_id=(x, y, 1 - c), device_id_type=MESH_ID)
            cp.start()
            cps.append(cp)
        for i, cp in enumerate(cps):
            theirs = bufs[i].at[:, 1 - c]
            cp.wait_send()
            pltpu.make_async_remote_copy(
                src_ref=theirs, dst_ref=theirs, send_sem=send.at[i], recv_sem=recv.at[i],
                device_id=(x, y, 1 - c), device_id_type=MESH_ID).wait_recv()

    return pl.pallas_call(
        body, name="join_halves", in_specs=[HBM_SPEC] * n, out_specs=[HBM_SPEC] * n,
        out_shape=[_sds(a.shape, a.dtype) for a in arrs],
        scratch_shapes=[DMA_SEM((n,)), DMA_SEM((n,))], input_output_aliases={i: i for i in range(n)},
    )(*arrs)


def _peer_copy(buf, land, k, send, recv, landing):
    x, y, c = _mesh_pos()
    px, py, pc = x ^ ((k >> 2) & 1), y ^ ((k >> 1) & 1), c ^ (k & 1)
    slot = 4 * px + 2 * py + pc if landing else 4 * x + 2 * y + c
    return pltpu.make_async_remote_copy(src_ref=buf, dst_ref=land.at[slot], send_sem=send, recv_sem=recv,
                                        device_id=(px, py, pc), device_id_type=MESH_ID)


def _exchange_all(buf):
    def body(in_ref, out_ref, send, recv):
        cps = [_peer_copy(in_ref, out_ref, k, send.at[k - 1], recv.at[k - 1], False) for k in range(1, 8)]
        for cp in cps:
            cp.start()
        for k in range(1, 8):
            cps[k - 1].wait_send()
            _peer_copy(in_ref, out_ref, k, send.at[k - 1], recv.at[k - 1], True).wait_recv()

    return pl.pallas_call(
        body, name="exchange_all", in_specs=[HBM_SPEC], out_specs=HBM_SPEC,
        out_shape=_sds((8,) + buf.shape, buf.dtype), scratch_shapes=[DMA_SEM((7,)), DMA_SEM((7,))],
    )(buf)


def _exchange_start(name, buf):
    land = lax.empty((8,) + buf.shape, buf.dtype)

    def body(in_ref, land_ref, send, recv, in_thru, land_thru, token):
        for k in range(1, 8):
            _peer_copy(in_ref, land_ref, k, send, recv, False).start()
        token[...] = jnp.zeros_like(token)

    res = pl.pallas_call(
        body, name=name, in_specs=[HBM_SPEC] * 2,
        out_specs=[SEM_SPEC, SEM_SPEC, HBM_SPEC, HBM_SPEC, pl.BlockSpec(memory_space=pltpu.VMEM)],
        out_shape=[DMA_SEM(()), DMA_SEM(()), pltpu.HBM(buf.shape, buf.dtype), pltpu.HBM(land.shape, land.dtype),
                   _sds((8, LANES), F32)],
        input_output_aliases={0: 2, 1: 3}, compiler_params=pltpu.CompilerParams(has_side_effects=DATAFLOW),
    )(_hbm(buf), _hbm(land))
    return (res[0], res[1]), res[2], res[3], res[4]


def _exchange_wait(name, buf, land, sems, after):
    def body(in_ref, land_ref, send, recv, after_ref, in_thru, land_thru):
        for k in range(1, 8):
            _peer_copy(in_ref, land_ref, k, send, recv, False).wait_send()
            _peer_copy(in_ref, land_ref, k, send, recv, True).wait_recv()

    res = pl.pallas_call(
        body, name=name, in_specs=[HBM_SPEC, HBM_SPEC, SEM_SPEC, SEM_SPEC, pl.BlockSpec(memory_space=pl.ANY)],
        out_specs=[HBM_SPEC, HBM_SPEC], out_shape=[pltpu.HBM(buf.shape, buf.dtype), pltpu.HBM(land.shape, land.dtype)],
        input_output_aliases={0: 0, 1: 1}, compiler_params=pltpu.CompilerParams(has_side_effects=DATAFLOW),
    )(buf, land, sems[0], sems[1], after)
    return res[0], res[1]


def _pad_to(a, axis, size):
    pad = [(0, 0)] * a.ndim
    pad[axis] = (0, size - a.shape[axis])
    return jnp.pad(a, pad)


def _strips(w):
    k, d = w.shape
    return _pad_to(w, 0, 32).reshape(32, d // LANES, LANES).transpose(1, 0, 2)


def kernel(x, norm_mix, w_in, gate_bias, conv_w, conv_b, conv_ln_g, conv_ln_b, w_conv_out, sgu_ln_g, sgu_ln_b, w_spatial, b_spatial, w_sgu_out, w_o, norm_ffn, w_ffn_gate, w_ffn_up, w_ffn_down, norm_final, loss_target, m_norm_mix, m_w_in, m_gate_bias, m_conv_w, m_conv_b, m_conv_ln_g, m_conv_ln_b, m_w_conv_out, m_sgu_ln_g, m_sgu_ln_b, m_w_spatial, m_b_spatial, m_w_sgu_out, m_w_o, m_norm_ffn, m_w_ffn_gate, m_w_ffn_up, m_w_ffn_down, m_norm_final, v_norm_mix, v_w_in, v_gate_bias, v_conv_w, v_conv_b, v_conv_ln_g, v_conv_ln_b, v_w_conv_out, v_sgu_ln_g, v_sgu_ln_b, v_w_spatial, v_b_spatial, v_w_sgu_out, v_w_o, v_norm_ffn, v_w_ffn_gate, v_w_ffn_up, v_w_ffn_down, v_norm_final):
    BL, S, D = x.shape
    T = BL * S
    L = w_in.shape[0]
    CS = w_in.shape[2]
    CN = CS // 3
    DQ = D // NSHARD
    FS = w_ffn_gate.shape[2]
    F = NSHARD * FS
    G, CH = w_spatial.shape[1], w_spatial.shape[2]
    KW = conv_w.shape[1]
    CQ = conv_w.shape[3]
    NSTR = D // LANES
    tm = min(512, S // 2)
    tm2 = max(tm // 2, CH)
    rb = min(64, tm)
    mx, my, mc = _mesh_pos()
    pos = jnp.stack([mc, 2 * mx + my]).astype(jnp.int32)

    def placed(name, w, dtype=BF16, dep=None, layer=None, **kw):
        return _place_shard("place_" + name, w, pos, dtype, _row_tile(w.shape[-2], 256), dep, layer, **kw)

    w_in0 = placed("w_in", w_in, layer=0)
    cw_p = placed("conv_w", _pad_to(conv_w.reshape(L, KW, CQ), 1, 32).reshape(L * 32, CQ), F32)
    fsems, fflying, ftoken = _gather_start("gather_start_first", [[w_in0, cw_p]])
    wts = []
    for l in range(L):
        w_sq = None
        for i, w in enumerate([w_conv_out, w_sgu_out, w_o]):
            w_sq = placed("w_sq", w, dep=ftoken, layer=l, into=w_sq, row_off=i * DQ, out_rows=3 * DQ)
        wts.append(dict(w_in=placed("w_in", w_in, dep=ftoken, layer=l) if l else None, w_sq=w_sq,
                        wg=placed("w_gate", w_ffn_gate[l].T, dep=ftoken), wu=placed("w_up", w_ffn_up[l].T, dep=ftoken),
                        wd=placed("w_down", w_ffn_down, dep=ftoken, layer=l)))
    ffn_keys = ["wg", "wu", "wd"]
    order = [[(0, "w_sq")], [(0, k) for k in ffn_keys]]
    order += [[(l, k) for k in ["w_in", "w_sq"] + ffn_keys] for l in range(1, L)]
    gsems, flying, token = _gather_start("gather_start", [[wts[l][k] for l, k in grp] for grp in order])
    first = _gather_wait("gather_wait_first", fflying[0], fsems[0], token)
    wts[0]["w_in"], cw_g = _forward_sibling("gather_first", first, False)
    conv_w_full = cw_g.reshape(NSHARD, L, 32, CQ).transpose(1, 2, 0, 3).reshape(L, 32, D)[:, :KW]

    def land(gi, after):
        bufs = _gather_wait("gather_wait_%d" % gi, flying[gi], gsems[gi], after)
        bufs = _forward_sibling("gather_forward_%d" % gi, bufs, False)
        for (l, k), b in zip(order[gi], bufs):
            wts[l][k] = b

    x2d = x.reshape(T, D)
    tgt = loss_target.reshape(T, D)
    row = lambda a, l: a[l].reshape(1, -1)

    saved = []
    xc = x2d
    for l in range(L):
        ws_b = w_spatial[l].astype(BF16)
        bs_b = jnp.repeat(b_spatial[l].T, D // G, axis=1)
        cw_s = _strips(conv_w_full[l])
        h, proj = _mix_in_fwd(xc, row(norm_mix, l), wts[l]["w_in"], 0, tm, token if l == 0 else None)
        if l == 0:
            land(0, h)
        c1h, rstd_c, c3, ya = _conv_fwd(proj, cw_s, row(conv_b, l), row(conv_ln_g, l), row(conv_ln_b, l),
                                        wts[l]["w_sq"], 0, S, tm, rb)
        if l == 0:
            land(1, ya)
        ffn_w = [wts[l][k].reshape(F, D) for k in ffn_keys]
        head = (norm_final.reshape(1, D), tgt) if l + 1 == L else None
        mixed, gated, yb, merged, x1, h2, gt, up, act, x2, *tail = _sgu_ffn_fwd(
            proj, ya, xc, row(sgu_ln_g, l), row(sgu_ln_b, l), ws_b, bs_b, row(gate_bias, l), wts[l]["w_sq"], 0,
            row(norm_ffn, l), *ffn_w, tm2, head)
        if l + 1 < L:
            land(l + 2, x2)
        saved.append(dict(x=xc, h=h, proj=proj, c1h=c1h, rstd_c=rstd_c, c3=c3, ya=ya, mixed=mixed, gated=gated,
                          yb=yb, merged=merged, x1=x1, h2=h2, gt=gt, up=up, act=act, ws_b=ws_b, cw=conv_w_full[l]))
        xc = x2

    dx, (loss_part, d_norm_final) = xc, tail
    loss = lax.psum(loss_part[0, 0], ("x", "y", "c"))

    g_acc = {}

    def reduce_start(tag, layer, named):
        arrs = [g.reshape(NSHARD, 2, g.shape[1] // 2, g.shape[2]) for _, g in named]
        from_sib = _send_sibling_halves("sibling_" + tag, arrs)
        ps = [_add_halves("presum_" + nm, a, r, pos, _row_tile(a.shape[2], 256))
              for (nm, _), a, r in zip(named, arrs, from_sib)]
        sems, ps, lands, tok = _chip_send_start("chip_send_start_" + tag, ps)
        return dict(tag=tag, layer=layer, names=[nm for nm, _ in named], ps=ps, lands=lands, sems=sems), tok

    def reduce_finish(pend, after):
        ps, lands = _chip_send_wait("chip_send_wait_" + pend["tag"], pend["ps"], pend["lands"], pend["sems"], after)
        for nm, p, r in zip(pend["names"], ps, lands):
            g_acc[nm] = _add_shards("shardsum_" + nm, p, r, pos, _row_tile(p.shape[1], 256), pend["layer"], L,
                                    g_acc.get(nm))

    me_idx = (4 * mx + 2 * my + mc).astype(jnp.int32).reshape(1)
    exchanges = []

    def pack_rows(pieces):
        packed = jnp.concatenate(pieces, axis=0)
        return _pad_to(packed, 0, -(-packed.shape[0] // 8) * 8)

    def unpack_rows(summed, pieces):
        out, off = [], 0
        for p in pieces:
            out.append(summed[off:off + p.shape[0]])
            off += p.shape[0]
        return out

    def small_start(tag, pieces):
        sems, buf, land, token = _exchange_start("exchange_start_" + tag, pack_rows(pieces))
        return dict(tag=tag, pieces=pieces, buf=buf, land=land, sems=sems, token=token)

    def small_finish(st, after):
        buf, land = _exchange_wait("exchange_wait_" + st["tag"], st["buf"], st["land"], st["sems"], after)
        return unpack_rows(_sum_slots(buf, land, me_idx, _row_tile(buf.shape[0], 256)), st["pieces"])

    small = [None] * L
    tt = min(2048, T // 2)
    nt = T // tt
    pending, tok = None, None
    for l in reversed(range(L)):
        sv, wt = saved[l], wts[l]
        ffn_w = [wt[k].reshape(F, D) for k in ffn_keys]
        dx1, dgt, dup, d_norm_ffn = _ffn_bwd(dx, sv["x1"], sv["gt"], sv["up"], row(norm_ffn, l), *ffn_w, tm, tok)
        tn_a = ((tt, F // 2), lambda j, t: (t, j))
        tn_b = ((tt, D), lambda j, t: (t, 0))
        tn_o = ((F, D), (F // 2, D), lambda j, t: (j, 0), 2, nt)
        g_g, = _tn_matmul("grad_w_gate", dgt, *tn_a, [sv["h2"]], *tn_b, *tn_o)
        g_u, = _tn_matmul("grad_w_up", dup, *tn_a, [sv["h2"]], *tn_b, *tn_o)
        g_d, = _tn_matmul("grad_w_down", sv["act"], *tn_a, [dx], *tn_b, *tn_o)
        if pending is not None:
            reduce_finish(pending, g_d)
        ffn_pend, tok = reduce_start("ffn%d" % l, l, [
            ("w_ffn_gate", g_g.reshape(NSHARD, FS, D)), ("w_ffn_up", g_u.reshape(NSHARD, FS, D)),
            ("w_ffn_down", g_d.reshape(NSHARD, FS, D))])
        wst_b = jnp.swapaxes(sv["ws_b"], 1, 2)
        dya, dyb, dp3, dc1, d_gate_bias, d_sgu_g, d_sgu_b, d_bs, d_ws, d_cln_g, d_cln_b, d_conv_b = _merge_sgu_bwd(
            dx1, sv["proj"], sv["ya"], sv["yb"], sv["mixed"], sv["c1h"], sv["rstd_c"], row(sgu_ln_g, l),
            row(sgu_ln_b, l), sv["ws_b"], wst_b, row(gate_bias, l), row(conv_ln_g, l), row(conv_ln_b, l), wt["w_sq"],
            0, tm2, tok)
        sq_args = ((tt, D), lambda j, t: (t, 0))
        sq_out = ((D, D), (D, D), lambda j, t: (0, 0), 1, nt)
        g_o, = _tn_matmul("grad_w_o", sv["merged"], *sq_args, [dx1], *sq_args, *sq_out)
        g_so, = _tn_matmul("grad_w_sgu_out", sv["gated"], *sq_args, [dyb], *sq_args, *sq_out)
        g_co, = _tn_matmul("grad_w_conv_out", sv["c3"], *sq_args, [dya], *sq_args, *sq_out)
        small[l] = [None, d_gate_bias.reshape(2, D), None, d_conv_b, d_cln_g, d_cln_b, d_sgu_g, d_sgu_b,
                    d_ws.reshape(G * CH * CH // D, D), d_bs.reshape(G * CH // D, D), d_norm_ffn]
        tok_x = None
        if l == 0:
            early = [k for k in range(len(small[0])) if small[0][k] is not None]
            exchanges.append((small_start("early0", [small[0][k] for k in early]), [(0, k) for k in early]))
            tok_x = exchanges[-1][0]["token"]
        dp3, d_cw_s = _conv_bwd(dc1, sv["proj"], dp3, _strips(sv["cw"][::-1]), S, tm, rb, tok_x)
        g_in, = _tn_matmul("grad_w_in", sv["h"], (tt, D), lambda j, t: (t, 0), [dp3], (tt, CS), lambda j, t: (t, j),
                           (NSHARD, D, CS), (None, D, CS), lambda j, t: (j, 0, 0), NSHARD, nt)
        reduce_finish(ffn_pend, g_in)
        pending, tok = reduce_start("mix%d" % l, l, [
            ("w_in", g_in), ("w_conv_out", g_co.reshape(NSHARD, DQ, D)), ("w_sgu_out", g_so.reshape(NSHARD, DQ, D)),
            ("w_o", g_o.reshape(NSHARD, DQ, D))])
        dx, d_norm_mix = _mix_in_bwd(dx1, dp3, sv["x"], row(norm_mix, l), wt["w_in"], 0, tm, tok)
        small[l][0] = d_norm_mix
        small[l][2] = d_cw_s.transpose(1, 0, 2).reshape(32, D)
        if l > 0:
            exchanges.append((small_start("layer%d" % l, small[l]), [(l, k) for k in range(len(small[l]))]))
            tok = [tok, exchanges[-1][0]["token"]]
    reduce_finish(pending, dx)
    grad_x = dx.reshape(BL, S, D)

    names = ["w_in", "w_conv_out", "w_sgu_out", "w_o", "w_ffn_gate", "w_ffn_up", "w_ffn_down"]
    g_full = _join_halves([g_acc[nm] for nm in names])
    g_w_in, g_w_co, g_w_so, g_w_o, g_w_g, g_w_u, g_w_d = [g.reshape(L, 2 * g.shape[2], g.shape[3]) for g in g_full]
    g_w_g = jnp.swapaxes(g_w_g, 1, 2)
    g_w_u = jnp.swapaxes(g_w_u, 1, 2)

    late = [small[0][0], small[0][2], d_norm_final]
    packed = pack_rows(late)
    summed = _sum_slots(packed, _exchange_all(packed), me_idx, _row_tile(packed.shape[0], 256))
    sg = [[None] * len(small[l]) for l in range(L)]
    sg[0][0], sg[0][2], g_norm_final = unpack_rows(summed, late)
    g_norm_final = g_norm_final[0]
    for st, where in exchanges:
        for (l, k), piece in zip(where, small_finish(st, summed)):
            sg[l][k] = piece

    def per_layer(k, shape):
        return jnp.stack([sg[l][k] for l in range(L)]).reshape(shape)

    g_norm_mix = per_layer(0, (L, D))
    g_gate_bias = per_layer(1, (L, 2 * D))
    g_conv_w_full = jnp.stack([sg[l][2][:KW] for l in range(L)])
    g_conv_w = lax.dynamic_slice_in_dim(g_conv_w_full, (2 * mx + my) * CQ, CQ, axis=2).reshape(L, KW, 1, CQ)
    g_conv_b = per_layer(3, (L, D))
    g_conv_ln_g = per_layer(4, (L, D))
    g_conv_ln_b = per_layer(5, (L, D))
    g_sgu_ln_g = per_layer(6, (L, D))
    g_sgu_ln_b = per_layer(7, (L, D))
    g_w_spatial = per_layer(8, (L, G, CH, CH))
    g_b_spatial = per_layer(9, (L, G, CH))
    g_norm_ffn = per_layer(10, (L, D))

    grads = [g_norm_mix, g_w_in, g_gate_bias, g_conv_w, g_conv_b, g_conv_ln_g, g_conv_ln_b, g_w_co, g_sgu_ln_g,
             g_sgu_ln_b, g_w_spatial, g_b_spatial, g_w_so, g_w_o, g_norm_ffn, g_w_g, g_w_u, g_w_d, g_norm_final]
    weights = [norm_mix, w_in, gate_bias, conv_w, conv_b, conv_ln_g, conv_ln_b, w_conv_out, sgu_ln_g, sgu_ln_b,
               w_spatial, b_spatial, w_sgu_out, w_o, norm_ffn, w_ffn_gate, w_ffn_up, w_ffn_down, norm_final]
    ms = [m_norm_mix, m_w_in, m_gate_bias, m_conv_w, m_conv_b, m_conv_ln_g, m_conv_ln_b, m_w_conv_out, m_sgu_ln_g,
          m_sgu_ln_b, m_w_spatial, m_b_spatial, m_w_sgu_out, m_w_o, m_norm_ffn, m_w_ffn_gate, m_w_ffn_up,
          m_w_ffn_down, m_norm_final]
    vs = [v_norm_mix, v_w_in, v_gate_bias, v_conv_w, v_conv_b, v_conv_ln_g, v_conv_ln_b, v_w_conv_out, v_sgu_ln_g,
          v_sgu_ln_b, v_w_spatial, v_b_spatial, v_w_sgu_out, v_w_o, v_norm_ffn, v_w_ffn_gate, v_w_ffn_up,
          v_w_ffn_down, v_norm_final]

    big_idx = [1, 7, 12, 13, 15, 16, 17]
    transposed = [15, 16]
    deltas, new_m, new_v = [None] * 19, [None] * 19, [None] * 19
    for k in big_idx:
        shp = weights[k].shape
        r2 = (shp[0] * shp[1], shp[2])
        res = _adamw("adamw_" + str(k), weights[k].reshape(r2), grads[k].reshape(r2), ms[k].reshape(r2),
                     vs[k].reshape(r2), _row_tile(r2[0], 256), k not in transposed)
        deltas[k], new_m[k], new_v[k] = [a.reshape(shp) for a in res[:3]]
        if k not in transposed:
            grads[k] = res[3].reshape(shp)
    small_idx = [k for k in range(19) if k not in big_idx]
    pick = lambda arrs: [arrs[k].reshape(1, -1) if arrs[k].ndim == 1 else arrs[k] for k in small_idx]
    d_, m_, v_ = _adamw_many(pick(weights), pick(grads), pick(ms), pick(vs))
    for i, k in enumerate(small_idx):
        shp = weights[k].shape
        deltas[k], new_m[k], new_v[k] = d_[i].reshape(shp), m_[i].reshape(shp), v_[i].reshape(shp)

    return (loss, grad_x, *grads, *deltas, *new_m, *new_v)
```

```python
import jax
import jax.numpy as jnp
from jax import lax
from jax.experimental import pallas as pl
from jax.experimental.pallas import tpu as pltpu

F32 = jnp.float32
BF16 = jnp.bfloat16
EPS = 1e-6
ADAM_LR = 0.001
ADAM_B1 = 0.9
ADAM_B2 = 0.999
ADAM_EPS = 1e-08
ADAM_WD = 0.01
ADAM_STEP = 10

NSHARD = 4
LANES = 128
HALO = 16
VMEM_LIMIT = 60 * 1024 * 1024
MESH_ID = pl.DeviceIdType.MESH


def _dot(a, b):
    return jnp.dot(a, b, preferred_element_type=F32)


def _dot_nt(a, b):
    return lax.dot_general(a, b, (((1,), (1,)), ((), ())), preferred_element_type=F32)


def _dot_tn(a, b):
    return lax.dot_general(a, b, (((0,), (0,)), ((), ())), preferred_element_type=F32)


def _sig(z):
    return 1.0 / (1.0 + jnp.exp(-z))


def _res(shape, imap=None):
    nd = len(shape)
    if imap is None:
        imap = lambda *_: (0,) * nd
    return pl.BlockSpec(shape, imap, pipeline_mode=pl.Buffered(1))


def _cparams(sem):
    return pltpu.CompilerParams(dimension_semantics=sem, vmem_limit_bytes=VMEM_LIMIT)


def _sds(shape, dtype):
    return jax.ShapeDtypeStruct(shape, dtype)


def _after(body, n_in, dep):
    deps = [] if dep is None else [d for d in (dep if isinstance(dep, (list, tuple)) else [dep]) if d is not None]
    if not deps:
        return body, [], []

    def wrapped(*refs):
        return body(*refs[:n_in], *refs[n_in + len(deps):])

    return wrapped, [pl.BlockSpec(memory_space=pl.ANY)] * len(deps), deps


def _mix_in_fwd(x2d, g_mix, w_in_g, layer, tm, dep=None):
    T, D = x2d.shape
    CS = w_in_g.shape[2]
    CN = CS // 3

    def body(x_ref, g_ref, w_ref, h_ref, p_ref):
        x = x_ref[...]
        rstd = lax.rsqrt(jnp.mean(x * x, axis=-1, keepdims=True) + EPS)
        h = (x * rstd * g_ref[...]).astype(BF16)
        h_ref[...] = h
        for s in range(NSHARD):
            for j in range(3):
                c0 = s * CS + j * CN
                p_ref[:, c0:c0 + CN] = _dot(h, w_ref[s, :, j * CN:(j + 1) * CN]).astype(BF16)

    body, dep_spec, dep_arg = _after(body, 3, dep)
    return pl.pallas_call(
        body, name="mix_in_fwd", grid=(T // tm,),
        in_specs=[pl.BlockSpec((tm, D), lambda i: (i, 0)), _res((1, D)),
                  _res((NSHARD, D, CS), lambda i: (0, layer, 0))] + dep_spec,
        out_specs=[pl.BlockSpec((tm, D), lambda i: (i, 0)), pl.BlockSpec((tm, NSHARD * CS), lambda i: (i, 0))],
        out_shape=[_sds((T, D), BF16), _sds((T, NSHARD * CS), BF16)],
        compiler_params=_cparams(("parallel",)),
    )(x2d, g_mix, w_in_g, *dep_arg)


def _halo_maps(tm, n_rows):
    nb = tm // HALO
    last = n_rows // HALO - 1
    prev = lambda i: (jnp.maximum(i * nb - 1, 0), 0)
    nxt = lambda i: (jnp.minimum((i + 1) * nb, last), 0)
    return prev, nxt


def _dwconv(pad_ref, w_ref, out_ref, n_strips, tm, kw, rb):
    off = HALO - (kw - 1) // 2

    def strip(cs, carry):
        for r0 in range(0, tm, rb):
            acc = jnp.zeros((rb, LANES), F32)
            for k in range(kw):
                r = r0 + off + k
                acc = acc + w_ref[cs, k:k + 1, :] * pad_ref[cs, r:r + rb, :]
            out_ref[cs, r0:r0 + rb, :] = acc
        return carry

    lax.fori_loop(0, n_strips, strip, 0)


def _fill_c0_pad(pad_ref, pa_ref, pprev_ref, pnext_ref, D, tm, first, last):
    for cs in range(D // LANES):
        lo, hi = cs * LANES, (cs + 1) * LANES

        def c0_of(ref):
            return ref[:, lo:hi].astype(F32) * _sig(ref[:, D + lo:D + hi].astype(F32))

        pad_ref[cs, HALO:HALO + tm, :] = c0_of(pa_ref)
        pad_ref[cs, 0:HALO, :] = jnp.where(first, 0.0, c0_of(pprev_ref))
        pad_ref[cs, HALO + tm:HALO + tm + HALO, :] = jnp.where(last, 0.0, c0_of(pnext_ref))


def _conv_fwd(proj, conv_w_s, conv_b, ln_g, ln_b, w_sq_g, layer, seq, tm, rb):
    T = proj.shape[0]
    D = conv_b.shape[1]
    DQ = D // NSHARD
    NSTR = D // LANES
    KW = 31
    tps = seq // tm
    prev, nxt = _halo_maps(tm, T)

    def body(pa_ref, pprev_ref, pnext_ref, w_ref, b_ref, g_ref, be_ref, wco_ref,
             c1h_ref, rstd_ref, c3_ref, ya_ref, pad_ref, c1s_ref):
        i = pl.program_id(0)
        first = (i % tps) == 0
        last = (i % tps) == tps - 1
        _fill_c0_pad(pad_ref, pa_ref, pprev_ref, pnext_ref, D, tm, first, last)
        _dwconv(pad_ref, w_ref, c1s_ref, NSTR, tm, KW, rb)
        wco = wco_ref[...].reshape(D, D)
        for r0 in (0, tm // 2):
            rows = slice(r0, r0 + tm // 2)
            c1 = jnp.concatenate([c1s_ref[cs, rows, :] for cs in range(NSTR)], axis=1) + b_ref[...]
            mu = jnp.mean(c1, axis=-1, keepdims=True)
            cc = c1 - mu
            rstd = lax.rsqrt(jnp.mean(cc * cc, axis=-1, keepdims=True) + EPS)
            c1h = cc * rstd
            c1h_ref[rows, :] = c1h.astype(BF16)
            rstd_ref[rows, :] = rstd
            c2 = c1h * g_ref[...] + be_ref[...]
            c3 = (c2 * _sig(c2)).astype(BF16)
            c3_ref[rows, :] = c3
            ya_ref[rows, :] = _dot(c3, wco).astype(BF16)

    row = lambda i: (i, 0)
    return pl.pallas_call(
        body, name="conv_fwd", grid=(T // tm,),
        in_specs=[pl.BlockSpec((tm, 2 * D), row), pl.BlockSpec((HALO, 2 * D), prev), pl.BlockSpec((HALO, 2 * D), nxt),
                  _res((NSTR, 32, LANES)), _res((1, D)), _res((1, D)), _res((1, D)),
                  _res((NSHARD, DQ, D), lambda i: (0, layer * 3 + 0, 0))],
        out_specs=[pl.BlockSpec((tm, D), row), pl.BlockSpec((tm, 1), row), pl.BlockSpec((tm, D), row),
                   pl.BlockSpec((tm, D), row)],
        out_shape=[_sds((T, D), BF16), _sds((T, 1), F32), _sds((T, D), BF16), _sds((T, D), BF16)],
        scratch_shapes=[pltpu.VMEM((NSTR, tm + 2 * HALO, LANES), F32), pltpu.VMEM((NSTR, tm, LANES), F32)],
        compiler_params=_cparams(("parallel",)),
    )(proj, proj, proj, conv_w_s, conv_b, ln_g, ln_b, w_sq_g)


def _ffn_chunks(F):
    assert F % 256 == 0, F
    return [(c0, min(512, F - c0)) for c0 in range(0, F, 512)]


def _sgu_ffn_fwd(proj, ya, x2d, ln_g, ln_b, ws_b, bs_b, gate_bias, w_sq_g, layer, g_ffn, wgt, wut, wd, tm, head=None):
    T, D = x2d.shape
    DQ = D // NSHARD
    G, CH, _ = ws_b.shape
    GD = D // G
    F = wd.shape[0]
    n = T // tm
    n_in = 17 if head else 15

    def body(*refs):
        (puv_ref, pg_ref, ya_ref, x_ref, g_ref, be_ref, ws_ref, bsb_ref, gb_ref, wso_ref, wo_ref,
         gf_ref, wg_ref, wu_ref, wd_ref) = refs[:15]
        mixed_ref, gated_ref, yb_ref, merged_ref, x1_ref, h2_ref, gt_ref, up_ref, act_ref = refs[n_in:n_in + 9]
        mix_scr = refs[-2] if head else refs[-1]
        u = puv_ref[:, :D].astype(F32)
        v = puv_ref[:, D:].astype(F32)
        mu = jnp.mean(v, axis=-1, keepdims=True)
        vc = v - mu
        rstd = lax.rsqrt(jnp.mean(vc * vc, axis=-1, keepdims=True) + EPS)
        vn = (vc * rstd * g_ref[...] + be_ref[...]).astype(BF16)
        nch = tm // CH
        for g in range(G):
            cols = slice(g * GD, (g + 1) * GD)
            rhs = jnp.concatenate([vn[ch * CH:(ch + 1) * CH, cols] for ch in range(nch)], axis=1)
            res = _dot(ws_ref[g], rhs)
            for ch in range(nch):
                mix_scr[ch * CH:(ch + 1) * CH, cols] = res[:, ch * GD:(ch + 1) * GD] + bsb_ref[:, cols]
        mixed = mix_scr[...]
        mixed_ref[...] = mixed.astype(BF16)
        gated = (u * mixed).astype(BF16)
        gated_ref[...] = gated
        yb = _dot(gated, wso_ref[...].reshape(D, D))
        yb_ref[...] = yb.astype(BF16)
        sa = _sig(pg_ref[:, :D].astype(F32) + gb_ref[:, :D])
        sb = _sig(pg_ref[:, D:].astype(F32) + gb_ref[:, D:])
        merged = (sa * ya_ref[...].astype(F32) + sb * yb).astype(BF16)
        merged_ref[...] = merged
        x1 = x_ref[...] + _dot(merged, wo_ref[...].reshape(D, D))
        x1_ref[...] = x1

        rstd = lax.rsqrt(jnp.mean(x1 * x1, axis=-1, keepdims=True) + EPS)
        h2 = (x1 * rstd * gf_ref[...]).astype(BF16)
        h2_ref[...] = h2
        acc = x1
        chunks = _ffn_chunks(F)

        def gate_up(c0, cw):
            return _dot_nt(h2, wg_ref[c0:c0 + cw, :]), _dot_nt(h2, wu_ref[c0:c0 + cw, :])

        nxt = gate_up(*chunks[0])
        for ci, (c0, cw) in enumerate(chunks):
            gt, up = nxt
            if ci + 1 < len(chunks):
                nxt = gate_up(*chunks[ci + 1])
            gt_ref[:, c0:c0 + cw] = gt.astype(BF16)
            up_ref[:, c0:c0 + cw] = up.astype(BF16)
            act = (gt * _sig(gt) * up).astype(BF16)
            act_ref[:, c0:c0 + cw] = act
            acc = acc + _dot(act, wd_ref[c0:c0 + cw, :])
        if not head:
            refs[n_in + 9][...] = acc
            return

        gfin_ref, t_ref = refs[15:17]
        dx_ref, loss_ref, dgf_ref = refs[n_in + 9:n_in + 12]
        sq_ref = refs[-1]
        i = pl.program_id(0)

        @pl.when(i == 0)
        def _():
            sq_ref[...] = jnp.zeros_like(sq_ref)
            dgf_ref[...] = jnp.zeros_like(dgf_ref)

        gfin = gfin_ref[...]
        rstd2 = lax.rsqrt(jnp.mean(acc * acc, axis=-1, keepdims=True) + EPS)
        xh = acc * rstd2
        diff = xh * gfin - t_ref[...]
        sq_ref[...] += jnp.sum(diff * diff, axis=0, keepdims=True)
        dy = diff * (1.0 / D)
        dgf_ref[...] += jnp.sum(dy * xh, axis=0, keepdims=True)
        dxh = dy * gfin
        dx_ref[...] = rstd2 * (dxh - xh * jnp.mean(dxh * xh, axis=-1, keepdims=True))

        @pl.when(i == n - 1)
        def _():
            tot = jnp.sum(sq_ref[...], axis=-1, keepdims=True) * (0.5 / D)
            loss_ref[...] = jnp.broadcast_to(tot, loss_ref.shape)

    row = lambda i: (i, 0)
    fixed2 = lambda i: (0, 0)
    wide = pl.BlockSpec((tm, F), row)
    in_specs = [pl.BlockSpec((tm, 2 * D), lambda i: (i, 1)), pl.BlockSpec((tm, 2 * D), lambda i: (i, 2)),
                pl.BlockSpec((tm, D), row), pl.BlockSpec((tm, D), row),
                _res((1, D)), _res((1, D)), _res((G, CH, CH)), _res((CH, D)), _res((1, 2 * D)),
                _res((NSHARD, DQ, D), lambda i: (0, layer * 3 + 1, 0)),
                _res((NSHARD, DQ, D), lambda i: (0, layer * 3 + 2, 0)),
                _res((1, D)), _res((F, D)), _res((F, D)), _res((F, D))]
    out_specs = [pl.BlockSpec((tm, D), row)] * 6 + [wide, wide, wide, pl.BlockSpec((tm, D), row)]
    out_shape = ([_sds((T, D), BF16)] * 4 + [_sds((T, D), F32), _sds((T, D), BF16)] + [_sds((T, F), BF16)] * 3
                 + [_sds((T, D), F32)])
    scratch = [pltpu.VMEM((tm, D), F32)]
    args = [proj, proj, ya, x2d, ln_g, ln_b, ws_b, bs_b, gate_bias, w_sq_g, w_sq_g, g_ffn, wgt, wut, wd]
    if head:
        in_specs += [_res((1, D)), pl.BlockSpec((tm, D), row)]
        out_specs += [pl.BlockSpec((1, LANES), fixed2), pl.BlockSpec((1, D), fixed2)]
        out_shape += [_sds((1, LANES), F32), _sds((1, D), F32)]
        scratch += [pltpu.VMEM((1, D), F32)]
        args += list(head)
    return pl.pallas_call(
        body, name="sgu_ffn_fwd_head" if head else "sgu_ffn_fwd", grid=(n,),
        in_specs=in_specs, out_specs=out_specs, out_shape=out_shape, scratch_shapes=scratch,
        compiler_params=_cparams(("arbitrary",) if head else ("parallel",)),
    )(*args)


def _ffn_bwd(dx2, x1, gt, up, g_ffn, wgt, wut, wd, tm, dep=None):
    T, D = x1.shape
    F = wd.shape[0]

    def body(dx2_ref, x1_ref, gt_ref, up_ref, g_ref, wg_ref, wu_ref, wd_ref, dx1_ref, dgt_ref, dup_ref, dg_ref):
        i = pl.program_id(0)

        @pl.when(i == 0)
        def _():
            dg_ref[...] = jnp.zeros_like(dg_ref)

        dx2 = dx2_ref[...]
        dx2b = dx2.astype(BF16)
        dh2 = jnp.zeros((tm, D), F32)
        chunks = _ffn_chunks(F)
        dact_next = _dot_nt(dx2b, wd_ref[0:chunks[0][1], :])
        for ci, (c0, cw) in enumerate(chunks):
            dact = dact_next
            if ci + 1 < len(chunks):
                n0, nw = chunks[ci + 1]
                dact_next = _dot_nt(dx2b, wd_ref[n0:n0 + nw, :])
            g = gt_ref[:, c0:c0 + cw].astype(F32)
            u = up_ref[:, c0:c0 + cw].astype(F32)
            sg = _sig(g)
            dup = (dact * (g * sg)).astype(BF16)
            dgt = (dact * u * (sg * (1.0 + g * (1.0 - sg)))).astype(BF16)
            dgt_ref[:, c0:c0 + cw] = dgt
            dup_ref[:, c0:c0 + cw] = dup
            dh2 = dh2 + _dot(dgt, wg_ref[c0:c0 + cw, :]) + _dot(dup, wu_ref[c0:c0 + cw, :])
        x = x1_ref[...]
        rstd = lax.rsqrt(jnp.mean(x * x, axis=-1, keepdims=True) + EPS)
        xh = x * rstd
        dg_ref[...] += jnp.sum(dh2 * xh, axis=0, keepdims=True)
        dxh = dh2 * g_ref[...]
        dx1_ref[...] = dx2 + rstd * (dxh - xh * jnp.mean(dxh * xh, axis=-1, keepdims=True))

    row = lambda i: (i, 0)
    body, dep_spec, dep_arg = _after(body, 8, dep)
    return pl.pallas_call(
        body, name="ffn_bwd", grid=(T // tm,),
        in_specs=[pl.BlockSpec((tm, D), row), pl.BlockSpec((tm, D), row), pl.BlockSpec((tm, F), row),
                  pl.BlockSpec((tm, F), row), _res((1, D)), _res((F, D)), _res((F, D)), _res((F, D))] + dep_spec,
        out_specs=[pl.BlockSpec((tm, D), row), pl.BlockSpec((tm, F), row), pl.BlockSpec((tm, F), row),
                   pl.BlockSpec((1, D), lambda i: (0, 0))],
        out_shape=[_sds((T, D), F32), _sds((T, F), BF16), _sds((T, F), BF16), _sds((1, D), F32)],
        compiler_params=_cparams(("arbitrary",)),
    )(dx2, x1, gt, up, g_ffn, wgt, wut, wd, *dep_arg)


def _merge_sgu_bwd(dx1, proj, ya, yb, mixed, c1h, rstd_c, ln_g, ln_b, ws_b, wst_b, gate_bias, cln_g, cln_b, w_sq_g,
                   layer, tm, dep=None):
    T, D = dx1.shape
    DQ = D // NSHARD
    G, CH, _ = ws_b.shape
    GD = D // G

    def body(dx1_ref, puv_ref, pg_ref, ya_ref, yb_ref, mixed_ref, c1h_ref, rstdc_ref, g_ref, be_ref, ws_ref, wst_ref,
             gb_ref, cg_ref, cbe_ref, wco_ref, wso_ref, wo_ref,
             dya_ref, dyb_ref, dp_ref, dc1_ref, dgb_ref, dlg_ref, dlb_ref, dbs_ref, dws_ref, dcg_ref, dcbe_ref,
             dcb_ref, dvn_scr, dbs_scr):
        i = pl.program_id(0)

        @pl.when(i == 0)
        def _():
            for r in (dgb_ref, dlg_ref, dlb_ref, dws_ref, dbs_scr, dcg_ref, dcbe_ref, dcb_ref):
                r[...] = jnp.zeros_like(r)

        dmerged = _dot_nt(dx1_ref[...].astype(BF16), wo_ref[...].reshape(D, D))
        sa = _sig(pg_ref[:, :D].astype(F32) + gb_ref[:, :D])
        sb = _sig(pg_ref[:, D:].astype(F32) + gb_ref[:, D:])
        dya = (dmerged * sa).astype(BF16)
        dyb = (dmerged * sb).astype(BF16)
        dya_ref[...] = dya
        dyb_ref[...] = dyb
        dc3 = _dot_nt(dya, wco_ref[...].reshape(D, D))
        dgated = _dot_nt(dyb, wso_ref[...].reshape(D, D))

        c1h = c1h_ref[...].astype(F32)
        c2 = c1h * cg_ref[...] + cbe_ref[...]
        sg = _sig(c2)
        dc2 = dc3 * (sg * (1.0 + c2 * (1.0 - sg)))
        dcg_ref[...] += jnp.sum(dc2 * c1h, axis=0, keepdims=True)
        dcbe_ref[...] += jnp.sum(dc2, axis=0, keepdims=True)
        dch = dc2 * cg_ref[...]
        dc1 = rstdc_ref[...] * (dch - jnp.mean(dch, axis=-1, keepdims=True)
                                - c1h * jnp.mean(dch * c1h, axis=-1, keepdims=True))
        dc1_ref[...] = dc1
        dcb_ref[...] += jnp.sum(dc1, axis=0, keepdims=True)

        dga = dmerged * ya_ref[...].astype(F32) * (sa * (1.0 - sa))
        dgb = dmerged * yb_ref[...].astype(F32) * (sb * (1.0 - sb))
        dp_ref[:, 4 * D:5 * D] = dga.astype(BF16)
        dp_ref[:, 5 * D:6 * D] = dgb.astype(BF16)
        dgb_ref[:, :D] += jnp.sum(dga, axis=0, keepdims=True)
        dgb_ref[:, D:] += jnp.sum(dgb, axis=0, keepdims=True)

        u = puv_ref[:, :D].astype(F32)
        v = puv_ref[:, D:].astype(F32)
        dp_ref[:, 2 * D:3 * D] = (dgated * mixed_ref[...].astype(F32)).astype(BF16)
        dmixed = dgated * u
        mu = jnp.mean(v, axis=-1, keepdims=True)
        vc = v - mu
        rstd = lax.rsqrt(jnp.mean(vc * vc, axis=-1, keepdims=True) + EPS)
        vh = vc * rstd
        vn = (vh * g_ref[...] + be_ref[...]).astype(BF16)
        dmb = dmixed.astype(BF16)
        nch = tm // CH
        bs_part = dmixed[0:CH, :]
        for ch in range(1, nch):
            bs_part = bs_part + dmixed[ch * CH:(ch + 1) * CH, :]
        dbs_scr[...] += bs_part
        for g in range(G):
            cols = slice(g * GD, (g + 1) * GD)
            dm_g = jnp.concatenate([dmb[ch * CH:(ch + 1) * CH, cols] for ch in range(nch)], axis=1)
            vn_g = jnp.concatenate([vn[ch * CH:(ch + 1) * CH, cols] for ch in range(nch)], axis=1)
            dws_ref[g] += _dot_nt(dm_g, vn_g)
            dvn_g = _dot(wst_ref[g], dm_g)
            for ch in range(nch):
                dvn_scr[ch * CH:(ch + 1) * CH, cols] = dvn_g[:, ch * GD:(ch + 1) * GD]
        dvn = dvn_scr[...]
        dlg_ref[...] += jnp.sum(dvn * vh, axis=0, keepdims=True)
        dlb_ref[...] += jnp.sum(dvn, axis=0, keepdims=True)
        dxh = dvn * g_ref[...]
        dv = rstd * (dxh - jnp.mean(dxh, axis=-1, keepdims=True) - vh * jnp.mean(dxh * vh, axis=-1, keepdims=True))
        dp_ref[:, 3 * D:4 * D] = dv.astype(BF16)

        @pl.when(i == pl.num_programs(0) - 1)
        def _():
            for g in range(G):
                blk = dbs_scr[:, g * GD:(g + 1) * GD]
                if GD != CH:
                    blk = jnp.concatenate([blk, jnp.zeros((CH, CH - GD), F32)], axis=1)
                dbs_ref[:, g * CH:(g + 1) * CH] = jnp.sum(blk.T, axis=0, keepdims=True)

    row = lambda i: (i, 0)
    fixed2 = lambda i: (0, 0)
    body, dep_spec, dep_arg = _after(body, 18, dep)
    vec = pl.BlockSpec((1, D), fixed2)
    return pl.pallas_call(
        body, name="merge_sgu_bwd", grid=(T // tm,),
        in_specs=[pl.BlockSpec((tm, D), row), pl.BlockSpec((tm, 2 * D), lambda i: (i, 1)),
                  pl.BlockSpec((tm, 2 * D), lambda i: (i, 2)), pl.BlockSpec((tm, D), row), pl.BlockSpec((tm, D), row),
                  pl.BlockSpec((tm, D), row), pl.BlockSpec((tm, D), row), pl.BlockSpec((tm, 1), row),
                  _res((1, D)), _res((1, D)), _res((G, CH, CH)), _res((G, CH, CH)), _res((1, 2 * D)),
                  _res((1, D)), _res((1, D)),
                  _res((NSHARD, DQ, D), lambda i: (0, layer * 3 + 0, 0)),
                  _res((NSHARD, DQ, D), lambda i: (0, layer * 3 + 1, 0)),
                  _res((NSHARD, DQ, D), lambda i: (0, layer * 3 + 2, 0))] + dep_spec,
        out_specs=[pl.BlockSpec((tm, D), row), pl.BlockSpec((tm, D), row), pl.BlockSpec((tm, 6 * D), row),
                   pl.BlockSpec((tm, D), row), pl.BlockSpec((1, 2 * D), fixed2), vec, vec,
                   pl.BlockSpec((1, G * CH), fixed2), pl.BlockSpec((G, CH, CH), lambda i: (0, 0, 0)), vec, vec, vec],
        out_shape=[_sds((T, D), BF16), _sds((T, D), BF16), _sds((T, 6 * D), BF16), _sds((T, D), F32),
                   _sds((1, 2 * D), F32), _sds((1, D), F32), _sds((1, D), F32), _sds((1, G * CH), F32),
                   _sds((G, CH, CH), F32), _sds((1, D), F32), _sds((1, D), F32), _sds((1, D), F32)],
        scratch_shapes=[pltpu.VMEM((tm, D), F32), pltpu.VMEM((CH, D), F32)],
        compiler_params=_cparams(("arbitrary",)),
    )(dx1, proj, proj, ya, yb, mixed, c1h, rstd_c, ln_g, ln_b, ws_b, wst_b, gate_bias, cln_g, cln_b,
      w_sq_g, w_sq_g, w_sq_g, *dep_arg)


def _conv_bwd(dc1, proj, dp3, conv_wf_s, seq, tm, rb, dep=None):
    T, D = dc1.shape
    NSTR = D // LANES
    KW = 31
    PADK = (KW - 1) // 2
    tps = seq // tm
    prev, nxt = _halo_maps(tm, T)
    n = T // tm

    def body(dc_ref, dcprev_ref, dcnext_ref, pa_ref, pprev_ref, pnext_ref, wf_ref, dp_in_ref,
             dp_ref, dw_ref, pad_ref, dpad_ref, dc0_ref, dwacc_ref):
        del dp_in_ref
        i = pl.program_id(0)
        first = (i % tps) == 0
        last = (i % tps) == tps - 1

        @pl.when(i == 0)
        def _():
            dwacc_ref[...] = jnp.zeros_like(dwacc_ref)

        _fill_c0_pad(pad_ref, pa_ref, pprev_ref, pnext_ref, D, tm, first, last)
        for cs in range(NSTR):
            lo, hi = cs * LANES, (cs + 1) * LANES
            dpad_ref[cs, HALO:HALO + tm, :] = dc_ref[:, lo:hi]
            dpad_ref[cs, 0:HALO, :] = jnp.where(first, 0.0, dcprev_ref[:, lo:hi])
            dpad_ref[cs, HALO + tm:HALO + tm + HALO, :] = jnp.where(last, 0.0, dcnext_ref[:, lo:hi])
        _dwconv(dpad_ref, wf_ref, dc0_ref, NSTR, tm, KW, rb)

        def strip(cs, carry):
            for r0 in range(0, tm, rb):
                d = dpad_ref[cs, HALO + r0:HALO + r0 + rb, :]
                for k in range(KW):
                    r = r0 + HALO - PADK + k
                    prod = d * pad_ref[cs, r:r + rb, :]
                    dwacc_ref[cs, k * 8:(k + 1) * 8, :] += jnp.sum(prod.reshape(rb // 8, 8, LANES), axis=0)
            return carry

        lax.fori_loop(0, NSTR, strip, 0)

        for cs in range(NSTR):
            lo, hi = cs * LANES, (cs + 1) * LANES
            av = pa_ref[:, lo:hi].astype(F32)
            sg = _sig(pa_ref[:, D + lo:D + hi].astype(F32))
            dc0 = dc0_ref[cs]
            dp_ref[:, lo:hi] = (dc0 * sg).astype(BF16)
            dp_ref[:, D + lo:D + hi] = (dc0 * av * (sg * (1.0 - sg))).astype(BF16)

        @pl.when(i == n - 1)
        def _():
            for cs in range(NSTR):
                dw_ref[cs] = jnp.sum(dwacc_ref[cs].reshape(32, 8, LANES), axis=1)

    row = lambda i: (i, 0)
    body, dep_spec, dep_arg = _after(body, 8, dep)
    return pl.pallas_call(
        body, name="conv_bwd", grid=(n,),
        in_specs=[pl.BlockSpec((tm, D), row), pl.BlockSpec((HALO, D), prev), pl.BlockSpec((HALO, D), nxt),
                  pl.BlockSpec((tm, 2 * D), row), pl.BlockSpec((HALO, 2 * D), prev), pl.BlockSpec((HALO, 2 * D), nxt),
                  _res((NSTR, 32, LANES)), pl.BlockSpec(memory_space=pl.ANY)] + dep_spec,
        out_specs=[pl.BlockSpec((tm, 2 * D), row),
                   pl.BlockSpec((NSTR, 32, LANES), lambda i: (0, 0, 0))],
        out_shape=[_sds(dp3.shape, BF16), _sds((NSTR, 32, LANES), F32)],
        scratch_shapes=[pltpu.VMEM((NSTR, tm + 2 * HALO, LANES), F32), pltpu.VMEM((NSTR, tm + 2 * HALO, LANES), F32),
                        pltpu.VMEM((NSTR, tm, LANES), F32), pltpu.VMEM((NSTR, 32 * 8, LANES), F32)],
        input_output_aliases={7: 0},
        compiler_params=_cparams(("arbitrary",)),
    )(dc1, dc1, dc1, proj, proj, proj, conv_wf_s, dp3, *dep_arg)


def _mix_in_bwd(dx1, dp3, x2d, g_mix, w_in_g, layer, tm, dep=None):
    T, D = x2d.shape
    CS = w_in_g.shape[2]
    CN = CS // 3

    def body(dx1_ref, dp_ref, x_ref, g_ref, w_ref, dx_ref, dg_ref):
        i = pl.program_id(0)

        @pl.when(i == 0)
        def _():
            dg_ref[...] = jnp.zeros_like(dg_ref)

        dh = jnp.zeros((tm, D), F32)
        for j in range(12):
            dh = dh + _dot_nt(dp_ref[:, j * CN:(j + 1) * CN], w_ref[j // 3, :, (j % 3) * CN:(j % 3 + 1) * CN])
        x = x_ref[...]
        rstd = lax.rsqrt(jnp.mean(x * x, axis=-1, keepdims=True) + EPS)
        xh = x * rstd
        dg_ref[...] += jnp.sum(dh * xh, axis=0, keepdims=True)
        dxh = dh * g_ref[...]
        dx_ref[...] = dx1_ref[...] + rstd * (dxh - xh * jnp.mean(dxh * xh, axis=-1, keepdims=True))

    row = lambda i: (i, 0)
    body, dep_spec, dep_arg = _after(body, 5, dep)
    return pl.pallas_call(
        body, name="mix_in_bwd", grid=(T // tm,),
        in_specs=[pl.BlockSpec((tm, D), row), pl.BlockSpec((tm, 6 * D), row),
                  pl.BlockSpec((tm, D), row), _res((1, D)), _res((NSHARD, D, CS), lambda i: (0, layer, 0))] + dep_spec,
        out_specs=[pl.BlockSpec((tm, D), row), pl.BlockSpec((1, D), lambda i: (0, 0))],
        out_shape=[_sds((T, D), F32), _sds((1, D), F32)],
        compiler_params=_cparams(("arbitrary",)),
    )(dx1, dp3, x2d, g_mix, w_in_g, *dep_arg)


def _tn_matmul(name, a, a_block, a_map, bs, b_block, b_map, out_shape, out_block, out_map, nj, nt):
    kk = [d for d in a_block if d is not None][-1]
    nn = [d for d in b_block if d is not None][-1]
    nb = len(bs)
    a_list = a if isinstance(a, (list, tuple)) else [a]
    na = len(a_list)

    def body(*refs):
        a_refs, b_refs = refs[:na], refs[na:na + nb]
        o_refs, acc_refs = refs[-2 * nb:-nb], refs[-nb:]
        t = pl.program_id(1)

        @pl.when(t == 0)
        def _():
            for acc_ref in acc_refs:
                acc_ref[...] = jnp.zeros_like(acc_ref)

        a_ts = [a_ref[...].astype(BF16) for a_ref in a_refs]
        for i, (b_ref, acc_ref) in enumerate(zip(b_refs, acc_refs)):
            acc_ref[...] += _dot_tn(a_ts[i % na], b_ref[...].astype(BF16))

        @pl.when(t == nt - 1)
        def _():
            for o_ref, acc_ref in zip(o_refs, acc_refs):
                o_ref[...] = acc_ref[...].astype(o_ref.dtype)

    return pl.pallas_call(
        body, name=name, grid=(nj, nt),
        in_specs=[pl.BlockSpec(a_block, a_map)] * na + [pl.BlockSpec(b_block, b_map)] * nb,
        out_specs=[pl.BlockSpec(out_block, out_map)] * nb, out_shape=[_sds(out_shape, BF16)] * nb,
        scratch_shapes=[pltpu.VMEM((kk, nn), F32)] * nb,
        compiler_params=_cparams(("parallel", "arbitrary")),
    )(*a_list, *bs)


def _place_shard(name, w, pos, dtype, tr, dep=None, layer=None, into=None, row_off=0, out_rows=None):
    R, C = w.shape[-2:]
    out_rows = out_rows or R

    def body(pos_ref, w_ref, *rest):
        del pos_ref
        rest[-1][...] = w_ref[...].astype(dtype)

    in_spec = (pl.BlockSpec((tr, C), lambda r, pos: (r, 0)) if layer is None else
               pl.BlockSpec((None, tr, C), lambda r, pos: (layer, r, 0)))
    extra, extra_args = ([], []) if into is None else ([pl.BlockSpec(memory_space=pl.ANY)], [into])
    body, dep_spec, dep_arg = _after(body, 2 + len(extra), dep)
    grid_spec = pltpu.PrefetchScalarGridSpec(
        num_scalar_prefetch=1, grid=(R // tr,), in_specs=[in_spec] + extra + dep_spec,
        out_specs=pl.BlockSpec((None, tr, C), lambda r, pos: (pos[1], row_off // tr + r, 0)))
    return pl.pallas_call(body, name=name, grid_spec=grid_spec, out_shape=_sds((NSHARD, out_rows, C), dtype),
                          input_output_aliases={} if into is None else {2: 0},
                          compiler_params=_cparams(("parallel",)))(pos, w, *extra_args, *dep_arg)


def _add_halves(name, g, rbuf, pos, tr):
    NS, _, H, C = g.shape

    def body(pos_ref, g_ref, r_ref, o_ref):
        del pos_ref
        o_ref[...] = (g_ref[...].astype(F32) + r_ref[...].astype(F32)).astype(BF16)

    grid_spec = pltpu.PrefetchScalarGridSpec(
        num_scalar_prefetch=1, grid=(NS, H // tr),
        in_specs=[pl.BlockSpec((None, None, tr, C), lambda s, r, pos: (s, pos[0], r, 0)),
                  pl.BlockSpec((None, tr, C), lambda s, r, pos: (s, r, 0))],
        out_specs=pl.BlockSpec((None, tr, C), lambda s, r, pos: (s, r, 0)))
    return pl.pallas_call(body, name=name, grid_spec=grid_spec, out_shape=_sds((NS, H, C), BF16),
                          compiler_params=_cparams(("parallel", "parallel")))(pos, g, rbuf)


def _add_shards(name, p, rbuf, pos, tr, layer, n_layers, prev):
    _, H, C = p.shape

    def body(pos_ref, p_ref, r_ref, *rest):
        del pos_ref
        o_ref = rest[-1]
        acc = p_ref[...].astype(F32)
        for j in range(3):
            acc = acc + r_ref[j].astype(F32)
        o_ref[...] = acc

    in_specs = [pl.BlockSpec((None, tr, C), lambda r, pos: (pos[1], r, 0)),
                pl.BlockSpec((3, tr, C), lambda r, pos: (0, r, 0))]
    args = [pos, p, rbuf]
    aliases = {}
    if prev is not None:
        in_specs.append(pl.BlockSpec(memory_space=pl.ANY))
        args.append(prev)
        aliases = {3: 0}
    grid_spec = pltpu.PrefetchScalarGridSpec(
        num_scalar_prefetch=1, grid=(H // tr,), in_specs=in_specs,
        out_specs=pl.BlockSpec((None, None, tr, C), lambda r, pos: (layer, pos[0], r, 0)))
    return pl.pallas_call(body, name=name, grid_spec=grid_spec, out_shape=_sds((n_layers, 2, H, C), F32),
                          input_output_aliases=aliases, compiler_params=_cparams(("parallel",)))(*args)


def _sum_slots(own, land, me, tr):
    NS8, R, C = land.shape

    def body(me_ref, own_ref, l_ref, o_ref):
        acc = None
        for j in range(NS8):
            term = jnp.where(me_ref[0] == j, own_ref[...], l_ref[j])
            acc = term if acc is None else acc + term
        o_ref[...] = acc

    grid_spec = pltpu.PrefetchScalarGridSpec(
        num_scalar_prefetch=1, grid=(R // tr,),
        in_specs=[pl.BlockSpec((tr, C), lambda i, me: (i, 0)), pl.BlockSpec((NS8, tr, C), lambda i, me: (0, i, 0))],
        out_specs=pl.BlockSpec((tr, C), lambda i, me: (i, 0)))
    return pl.pallas_call(body, name="sum_slots", grid_spec=grid_spec, out_shape=_sds((R, C), F32),
                          compiler_params=_cparams(("parallel",)))(me, own, land)


def _adamw_update(w_ref, g_ref, m_ref, v_ref, d_ref, mo_ref, vo_ref):
    g_ = g_ref[...]
    m_ = ADAM_B1 * m_ref[...] + (1.0 - ADAM_B1) * g_
    v_ = ADAM_B2 * v_ref[...] + (1.0 - ADAM_B2) * (g_ * g_)
    mo_ref[...] = m_
    vo_ref[...] = v_
    m_hat = m_ / (1.0 - ADAM_B1 ** ADAM_STEP)
    v_hat = v_ / (1.0 - ADAM_B2 ** ADAM_STEP)
    d_ref[...] = -ADAM_LR * (m_hat / (jnp.sqrt(v_hat) + ADAM_EPS) + ADAM_WD * w_ref[...])


def _adamw(name, w, g, m, v, tr, emit_g):
    R, C = w.shape
    n_out = 4 if emit_g else 3

    def body(w_ref, g_ref, m_ref, v_ref, d_ref, mo_ref, vo_ref, *go_ref):
        _adamw_update(w_ref, g_ref, m_ref, v_ref, d_ref, mo_ref, vo_ref)
        if emit_g:
            go_ref[0][...] = g_ref[...]

    spec = pl.BlockSpec((tr, C), lambda i: (i, 0))
    return pl.pallas_call(
        body, name=name, grid=(R // tr,), in_specs=[spec] * 4, out_specs=[spec] * n_out,
        out_shape=[_sds((R, C), F32)] * n_out, compiler_params=_cparams(("parallel",)))(w, g, m, v)


def _adamw_many(ws, gs, ms, vs):
    n = len(ws)

    def body(*refs):
        ins, outs = refs[:4 * n], refs[4 * n:]
        for k in range(n):
            _adamw_update(ins[k], ins[n + k], ins[2 * n + k], ins[3 * n + k], outs[k], outs[n + k], outs[2 * n + k])

    vmem = pl.BlockSpec(memory_space=pltpu.VMEM)
    res = pl.pallas_call(
        body, name="adamw_small", in_specs=[vmem] * (4 * n), out_specs=[vmem] * (3 * n),
        out_shape=[_sds(w.shape, F32) for w in ws] * 3,
        compiler_params=pltpu.CompilerParams(vmem_limit_bytes=VMEM_LIMIT))(*ws, *gs, *ms, *vs)
    return list(res[:n]), list(res[n:2 * n]), list(res[2 * n:])


def _row_tile(rows, cap):
    best = rows
    for t in range(8, min(rows, cap) + 1, 8):
        if rows % t == 0:
            best = t
    return best


HBM_SPEC = pl.BlockSpec(memory_space=pltpu.HBM)
SEM_SPEC = pl.BlockSpec(memory_space=pltpu.SEMAPHORE)
DATAFLOW = pltpu.SideEffectType.DATAFLOW_SIDE_EFFECTING
DMA_SEM = pltpu.SemaphoreType.DMA


def _hbm(a):
    return pltpu.with_memory_space_constraint(a, pltpu.HBM)


def _mesh_pos():
    return lax.axis_index("x"), lax.axis_index("y"), lax.axis_index("c")


def _other_chips(x, y):
    return [(1 - x, y), (x, 1 - y), (1 - x, 1 - y)]


def _half_rows(buf, shard, core):
    h = buf.shape[1] // 2
    return buf.at[shard, pl.ds(core * h, h), :]


def _ici_copy(buf, j, send, recv, landing):
    x, y, c = _mesh_pos()
    px, py = _other_chips(x, y)[j]
    part = _half_rows(buf, 2 * px + py if landing else 2 * x + y, c)
    return pltpu.make_async_remote_copy(src_ref=part, dst_ref=part, send_sem=send, recv_sem=recv,
                                        device_id=(px, py, c), device_id_type=MESH_ID)


def _sibling_copy(buf, j, send, recv, landing):
    x, y, c = _mesh_pos()
    px, py = _other_chips(x, y)[j]
    part = _half_rows(buf, 2 * px + py, 1 - c if landing else c)
    return pltpu.make_async_remote_copy(src_ref=part, dst_ref=part, send_sem=send, recv_sem=recv,
                                        device_id=(x, y, 1 - c), device_id_type=MESH_ID)


def _forward_sibling(name, bufs, with_ici):
    n = len(bufs)

    def body(*refs):
        ins = refs[:n]
        send_ici, recv_ici, send_d2d, recv_d2d = refs[2 * n:]
        sends = []
        if with_ici:
            for i in range(n):
                for j in range(3):
                    cp = _ici_copy(ins[i], j, send_ici.at[i, j], recv_ici.at[i, j], False)
                    cp.start()
                    sends.append(cp)
        for i in range(n):
            for j in range(3):
                if with_ici:
                    _ici_copy(ins[i], j, send_ici.at[i, j], recv_ici.at[i, j], True).wait_recv()
                cp = _sibling_copy(ins[i], j, send_d2d.at[i, j], recv_d2d.at[i, j], False)
                cp.start()
                sends.append(cp)
        for i in range(n):
            for j in range(3):
                _sibling_copy(ins[i], j, send_d2d.at[i, j], recv_d2d.at[i, j], True).wait_recv()
        for cp in sends:
            cp.wait_send()

    return pl.pallas_call(
        body, name=name, in_specs=[HBM_SPEC] * n, out_specs=[HBM_SPEC] * n,
        out_shape=[_sds(b.shape, b.dtype) for b in bufs],
        scratch_shapes=[DMA_SEM((n, 3))] * 4, input_output_aliases={i: i for i in range(n)},
    )(*bufs)


def _gather_start(name, groups):
    flat = [b for g in groups for b in g]
    n, ng = len(flat), len(groups)

    def body(*refs):
        ins, sems, token = refs[:n], refs[n:n + 2 * ng], refs[-1]
        k = 0
        for gi, g in enumerate(groups):
            for a in range(len(g)):
                for j in range(3):
                    _ici_copy(ins[k], j, sems[2 * gi], sems[2 * gi + 1], False).start()
                k += 1
        token[...] = jnp.zeros_like(token)

    res = pl.pallas_call(
        body, name=name, in_specs=[HBM_SPEC] * n,
        out_specs=[SEM_SPEC] * (2 * ng) + [HBM_SPEC] * n + [pl.BlockSpec(memory_space=pltpu.VMEM)],
        out_shape=[DMA_SEM(()) for g in groups for _ in range(2)]
        + [pltpu.HBM(b.shape, b.dtype) for b in flat] + [_sds((8, LANES), F32)],
        input_output_aliases={i: 2 * ng + i for i in range(n)},
        compiler_params=pltpu.CompilerParams(has_side_effects=DATAFLOW),
    )(*[_hbm(b) for b in flat])
    sems = [(res[2 * gi], res[2 * gi + 1]) for gi in range(ng)]
    thru, k = [], 2 * ng
    for g in groups:
        thru.append(list(res[k:k + len(g)]))
        k += len(g)
    return sems, thru, res[-1]


def _gather_wait(name, bufs, sems, after):
    n = len(bufs)

    def body(*refs):
        ins, send, recv = refs[:n], refs[n], refs[n + 1]
        for a in range(n):
            for j in range(3):
                _ici_copy(ins[a], j, send, recv, False).wait_send()
                _ici_copy(ins[a], j, send, recv, True).wait_recv()

    return pl.pallas_call(
        body, name=name, in_specs=[HBM_SPEC] * n + [SEM_SPEC, SEM_SPEC, pl.BlockSpec(memory_space=pl.ANY)],
        out_specs=[HBM_SPEC] * n, out_shape=[pltpu.HBM(b.shape, b.dtype) for b in bufs],
        input_output_aliases={i: i for i in range(n)},
        compiler_params=pltpu.CompilerParams(has_side_effects=DATAFLOW),
    )(*bufs, sems[0], sems[1], after)


def _send_sibling_halves(name, arrs):
    n = len(arrs)

    def body(*refs):
        ins, outs = refs[:n], refs[n:2 * n]
        send, recv = refs[2 * n:]
        x, y, c = _mesh_pos()
        cps = []
        for i in range(n):
            cp = pltpu.make_async_remote_copy(
                src_ref=ins[i].at[:, 1 - c], dst_ref=outs[i],
                send_sem=send.at[i], recv_sem=recv.at[i], device_id=(x, y, 1 - c), device_id_type=MESH_ID)
            cp.start()
            cps.append(cp)
        for cp in cps:
            cp.wait()

    return pl.pallas_call(
        body, name=name, in_specs=[HBM_SPEC] * n, out_specs=[HBM_SPEC] * n,
        out_shape=[_sds((a.shape[0],) + a.shape[2:], a.dtype) for a in arrs],
        scratch_shapes=[DMA_SEM((n,)), DMA_SEM((n,))],
    )(*arrs)


def _sibling_copy_half(arr, land, send, recv):
    x, y, c = _mesh_pos()
    return pltpu.make_async_remote_copy(src_ref=arr.at[:, 1 - c], dst_ref=land, send_sem=send, recv_sem=recv,
                                        device_id=(x, y, 1 - c), device_id_type=MESH_ID)


def _sibling_start(name, arrs):
    n = len(arrs)
    lands = [lax.empty((a.shape[0],) + a.shape[2:], a.dtype) for a in arrs]

    def body(*refs):
        ins, lnd, send, recv, token = refs[:n], refs[n:2 * n], refs[2 * n], refs[2 * n + 1], refs[-1]
        for i in range(n):
            _sibling_copy_half(ins[i], lnd[i], send, recv).start()
        token[...] = jnp.zeros_like(token)

    res = pl.pallas_call(
        body, name=name, in_specs=[HBM_SPEC] * (2 * n),
        out_specs=[SEM_SPEC, SEM_SPEC] + [HBM_SPEC] * (2 * n) + [pl.BlockSpec(memory_space=pltpu.VMEM)],
        out_shape=[DMA_SEM(()), DMA_SEM(())] + [pltpu.HBM(a.shape, a.dtype) for a in arrs + lands]
        + [_sds((8, LANES), F32)],
        input_output_aliases={i: 2 + i for i in range(2 * n)},
        compiler_params=pltpu.CompilerParams(has_side_effects=DATAFLOW),
    )(*[_hbm(a) for a in arrs + lands])
    return (res[0], res[1]), list(res[2:2 + n]), list(res[2 + n:2 + 2 * n]), res[-1]


def _sibling_wait(name, arrs, lands, sems, after):
    n = len(arrs)

    def body(*refs):
        ins, lnd, send, recv = refs[:n], refs[n:2 * n], refs[2 * n], refs[2 * n + 1]
        for i in range(n):
            cp = _sibling_copy_half(ins[i], lnd[i], send, recv)
            cp.wait_send()
            cp.wait_recv()

    res = pl.pallas_call(
        body, name=name, in_specs=[HBM_SPEC] * (2 * n) + [SEM_SPEC, SEM_SPEC, pl.BlockSpec(memory_space=pl.ANY)],
        out_specs=[HBM_SPEC] * (2 * n), out_shape=[pltpu.HBM(a.shape, a.dtype) for a in arrs + lands],
        input_output_aliases={i: i for i in range(2 * n)},
        compiler_params=pltpu.CompilerParams(has_side_effects=DATAFLOW),
    )(*arrs, *lands, sems[0], sems[1], after)
    return list(res[:n]), list(res[n:])


def _chip_copy(p, land, j, send, recv):
    x, y, c = _mesh_pos()
    px, py = _other_chips(x, y)[j]
    return pltpu.make_async_remote_copy(src_ref=p.at[2 * px + py], dst_ref=land.at[j], send_sem=send, recv_sem=recv,
                                        device_id=(px, py, c), device_id_type=MESH_ID)


def _chip_send_start(name, ps):
    n = len(ps)
    lands = [lax.empty((3,) + p.shape[1:], p.dtype) for p in ps]

    def body(*refs):
        ins, lnd, send, recv, token = refs[:n], refs[n:2 * n], refs[2 * n], refs[2 * n + 1], refs[-1]
        for i in range(n):
            for j in range(3):
                _chip_copy(ins[i], lnd[i], j, send, recv).start()
        token[...] = jnp.zeros_like(token)

    res = pl.pallas_call(
        body, name=name, in_specs=[HBM_SPEC] * (2 * n),
        out_specs=[SEM_SPEC, SEM_SPEC] + [HBM_SPEC] * (2 * n) + [pl.BlockSpec(memory_space=pltpu.VMEM)],
        out_shape=[DMA_SEM(()), DMA_SEM(())] + [pltpu.HBM(a.shape, a.dtype) for a in ps + lands]
        + [_sds((8, LANES), F32)],
        input_output_aliases={i: 2 + i for i in range(2 * n)},
        compiler_params=pltpu.CompilerParams(has_side_effects=DATAFLOW),
    )(*[_hbm(a) for a in ps + lands])
    return (res[0], res[1]), list(res[2:2 + n]), list(res[2 + n:2 + 2 * n]), res[-1]


def _chip_send_wait(name, ps, lands, sems, after):
    n = len(ps)

    def body(*refs):
        ins, lnd, send, recv = refs[:n], refs[n:2 * n], refs[2 * n], refs[2 * n + 1]
        for i in range(n):
            for j in range(3):
                cp = _chip_copy(ins[i], lnd[i], j, send, recv)
                cp.wait_send()
                cp.wait_recv()

    res = pl.pallas_call(
        body, name=name, in_specs=[HBM_SPEC] * (2 * n) + [SEM_SPEC, SEM_SPEC, pl.BlockSpec(memory_space=pl.ANY)],
        out_specs=[HBM_SPEC] * (2 * n), out_shape=[pltpu.HBM(a.shape, a.dtype) for a in ps + lands],
        input_output_aliases={i: i for i in range(2 * n)},
        compiler_params=pltpu.CompilerParams(has_side_effects=DATAFLOW),
    )(*ps, *lands, sems[0], sems[1], after)
    return list(res[:n]), list(res[n:])


def _join_halves(arrs):
    n = len(arrs)

    def body(*refs):
        bufs = refs[n:2 * n]
        send, recv = refs[2 * n:]
        x, y, c = _mesh_pos()
        cps = []
        for i in range(n):
            mine = bufs[i].at[:, c]
            cp = pltpu.make_async_remote_copy(
                src_ref=mine, dst_ref=mine, send_sem=send.at[i], recv_sem=recv.at[i],
                device_id=(x, y, 1 - c), device_id_type=MESH_ID)
            cp.start()
            cps.append(cp)
        for i, cp in enumerate(cps):
            theirs = bufs[i].at[:, 1 - c]
            cp.wait_send()
            pltpu.make_async_remote_copy(
                src_ref=theirs, dst_ref=theirs, send_sem=send.at[i], recv_sem=recv.at[i],
                device_id=(x, y, 1 - c), device_id_type=MESH_ID).wait_recv()

    return pl.pallas_call(
        body, name="join_halves", in_specs=[HBM_SPEC] * n, out_specs=[HBM_SPEC] * n,
        out_shape=[_sds(a.shape, a.dtype) for a in arrs],
        scratch_shapes=[DMA_SEM((n,)), DMA_SEM((n,))], input_output_aliases={i: i for i in range(n)},
    )(*arrs)


def _peer_copy(buf, land, k, send, recv, landing):
    x, y, c = _mesh_pos()
    px, py, pc = x ^ ((k >> 2) & 1), y ^ ((k >> 1) & 1), c ^ (k & 1)
    slot = 4 * px + 2 * py + pc if landing else 4 * x + 2 * y + c
    return pltpu.make_async_remote_copy(src_ref=buf, dst_ref=land.at[slot], send_sem=send, recv_sem=recv,
                                        device_id=(px, py, pc), device_id_type=MESH_ID)


def _exchange_all(buf):
    def body(in_ref, out_ref, send, recv):
        cps = [_peer_copy(in_ref, out_ref, k, send.at[k - 1], recv.at[k - 1], False) for k in range(1, 8)]
        for cp in cps:
            cp.start()
        for k in range(1, 8):
            cps[k - 1].wait_send()
            _peer_copy(in_ref, out_ref, k, send.at[k - 1], recv.at[k - 1], True).wait_recv()

    return pl.pallas_call(
        body, name="exchange_all", in_specs=[HBM_SPEC], out_specs=HBM_SPEC,
        out_shape=_sds((8,) + buf.shape, buf.dtype), scratch_shapes=[DMA_SEM((7,)), DMA_SEM((7,))],
    )(buf)


def _exchange_start(name, buf):
    land = lax.empty((8,) + buf.shape, buf.dtype)

    def body(in_ref, land_ref, send, recv, in_thru, land_thru, token):
        for k in range(1, 8):
            _peer_copy(in_ref, land_ref, k, send, recv, False).start()
        token[...] = jnp.zeros_like(token)

    res = pl.pallas_call(
        body, name=name, in_specs=[HBM_SPEC] * 2,
        out_specs=[SEM_SPEC, SEM_SPEC, HBM_SPEC, HBM_SPEC, pl.BlockSpec(memory_space=pltpu.VMEM)],
        out_shape=[DMA_SEM(()), DMA_SEM(()), pltpu.HBM(buf.shape, buf.dtype), pltpu.HBM(land.shape, land.dtype),
                   _sds((8, LANES), F32)],
        input_output_aliases={0: 2, 1: 3}, compiler_params=pltpu.CompilerParams(has_side_effects=DATAFLOW),
    )(_hbm(buf), _hbm(land))
    return (res[0], res[1]), res[2], res[3], res[4]


def _exchange_wait(name, buf, land, sems, after):
    def body(in_ref, land_ref, send, recv, after_ref, in_thru, land_thru):
        for k in range(1, 8):
            _peer_copy(in_ref, land_ref, k, send, recv, False).wait_send()
            _peer_copy(in_ref, land_ref, k, send, recv, True).wait_recv()

    res = pl.pallas_call(
        body, name=name, in_specs=[HBM_SPEC, HBM_SPEC, SEM_SPEC, SEM_SPEC, pl.BlockSpec(memory_space=pl.ANY)],
        out_specs=[HBM_SPEC, HBM_SPEC], out_shape=[pltpu.HBM(buf.shape, buf.dtype), pltpu.HBM(land.shape, land.dtype)],
        input_output_aliases={0: 0, 1: 1}, compiler_params=pltpu.CompilerParams(has_side_effects=DATAFLOW),
    )(buf, land, sems[0], sems[1], after)
    return res[0], res[1]


def _pad_to(a, axis, size):
    pad = [(0, 0)] * a.ndim
    pad[axis] = (0, size - a.shape[axis])
    return jnp.pad(a, pad)


def _strips(w):
    k, d = w.shape
    return _pad_to(w, 0, 32).reshape(32, d // LANES, LANES).transpose(1, 0, 2)


def kernel(x, norm_mix, w_in, gate_bias, conv_w, conv_b, conv_ln_g, conv_ln_b, w_conv_out, sgu_ln_g, sgu_ln_b, w_spatial, b_spatial, w_sgu_out, w_o, norm_ffn, w_ffn_gate, w_ffn_up, w_ffn_down, norm_final, loss_target, m_norm_mix, m_w_in, m_gate_bias, m_conv_w, m_conv_b, m_conv_ln_g, m_conv_ln_b, m_w_conv_out, m_sgu_ln_g, m_sgu_ln_b, m_w_spatial, m_b_spatial, m_w_sgu_out, m_w_o, m_norm_ffn, m_w_ffn_gate, m_w_ffn_up, m_w_ffn_down, m_norm_final, v_norm_mix, v_w_in, v_gate_bias, v_conv_w, v_conv_b, v_conv_ln_g, v_conv_ln_b, v_w_conv_out, v_sgu_ln_g, v_sgu_ln_b, v_w_spatial, v_b_spatial, v_w_sgu_out, v_w_o, v_norm_ffn, v_w_ffn_gate, v_w_ffn_up, v_w_ffn_down, v_norm_final):
    BL, S, D = x.shape
    T = BL * S
    L = w_in.shape[0]
    CS = w_in.shape[2]
    CN = CS // 3
    DQ = D // NSHARD
    FS = w_ffn_gate.shape[2]
    F = NSHARD * FS
    G, CH = w_spatial.shape[1], w_spatial.shape[2]
    KW = conv_w.shape[1]
    CQ = conv_w.shape[3]
    NSTR = D // LANES
    tm = min(512, S // 2)
    tm2 = max(tm // 2, CH)
    rb = min(64, tm)
    mx, my, mc = _mesh_pos()
    pos = jnp.stack([mc, 2 * mx + my]).astype(jnp.int32)

    def placed(name, w, dtype=BF16, dep=None, layer=None, **kw):
        return _place_shard("place_" + name, w, pos, dtype, _row_tile(w.shape[-2], 256), dep, layer, **kw)

    w_in0 = placed("w_in", w_in, layer=0)
    cw_p = placed("conv_w", _pad_to(conv_w.reshape(L, KW, CQ), 1, 32).reshape(L * 32, CQ), F32)
    fsems, fflying, ftoken = _gather_start("gather_start_first", [[w_in0, cw_p]])
    wts = []
    for l in range(L):
        w_sq = None
        for i, w in enumerate([w_conv_out, w_sgu_out, w_o]):
            w_sq = placed("w_sq", w, dep=ftoken, layer=l, into=w_sq, row_off=i * DQ, out_rows=3 * DQ)
        wts.append(dict(w_in=placed("w_in", w_in, dep=ftoken, layer=l) if l else None, w_sq=w_sq,
                        wg=placed("w_gate", w_ffn_gate[l].T, dep=ftoken), wu=placed("w_up", w_ffn_up[l].T, dep=ftoken),
                        wd=placed("w_down", w_ffn_down, dep=ftoken, layer=l)))
    ffn_keys = ["wg", "wu", "wd"]
    order = [[(0, "w_sq")], [(0, k) for k in ffn_keys]]
    order += [[(l, k) for k in ["w_in", "w_sq"] + ffn_keys] for l in range(1, L)]
    gsems, flying, token = _gather_start("gather_start", [[wts[l][k] for l, k in grp] for grp in order])
    first = _gather_wait("gather_wait_first", fflying[0], fsems[0], token)
    wts[0]["w_in"], cw_g = _forward_sibling("gather_first", first, False)
    conv_w_full = cw_g.reshape(NSHARD, L, 32, CQ).transpose(1, 2, 0, 3).reshape(L, 32, D)[:, :KW]

    def land(gi, after):
        bufs = _gather_wait("gather_wait_%d" % gi, flying[gi], gsems[gi], after)
        bufs = _forward_sibling("gather_forward_%d" % gi, bufs, False)
        for (l, k), b in zip(order[gi], bufs):
            wts[l][k] = b

    x2d = x.reshape(T, D)
    tgt = loss_target.reshape(T, D)
    row = lambda a, l: a[l].reshape(1, -1)

    saved = []
    xc = x2d
    for l in range(L):
        ws_b = w_spatial[l].astype(BF16)
        bs_b = jnp.repeat(b_spatial[l].T, D // G, axis=1)
        cw_s = _strips(conv_w_full[l])
        h, proj = _mix_in_fwd(xc, row(norm_mix, l), wts[l]["w_in"], 0, tm, token if l == 0 else None)
        if l == 0:
            land(0, h)
        c1h, rstd_c, c3, ya = _conv_fwd(proj, cw_s, row(conv_b, l), row(conv_ln_g, l), row(conv_ln_b, l),
                                        wts[l]["w_sq"], 0, S, tm, rb)
        if l == 0:
            land(1, ya)
        ffn_w = [wts[l][k].reshape(F, D) for k in ffn_keys]
        head = (norm_final.reshape(1, D), tgt) if l + 1 == L else None
        mixed, gated, yb, merged, x1, h2, gt, up, act, x2, *tail = _sgu_ffn_fwd(
            proj, ya, xc, row(sgu_ln_g, l), row(sgu_ln_b, l), ws_b, bs_b, row(gate_bias, l), wts[l]["w_sq"], 0,
            row(norm_ffn, l), *ffn_w, tm2, head)
        if l + 1 < L:
            land(l + 2, x2)
        saved.append(dict(x=xc, h=h, proj=proj, c1h=c1h, rstd_c=rstd_c, c3=c3, ya=ya, mixed=mixed, gated=gated,
                          yb=yb, merged=merged, x1=x1, h2=h2, gt=gt, up=up, act=act, ws_b=ws_b, cw=conv_w_full[l]))
        xc = x2

    dx, (loss_part, d_norm_final) = xc, tail
    loss = lax.psum(loss_part[0, 0], ("x", "y", "c"))

    g_acc = {}

    def reduce_start(tag, layer, named):
        arrs = [g.reshape(NSHARD, 2, g.shape[1] // 2, g.shape[2]) for _, g in named]
        from_sib = _send_sibling_halves("sibling_" + tag, arrs)
        ps = [_add_halves("presum_" + nm, a, r, pos, _row_tile(a.shape[2], 256))
              for (nm, _), a, r in zip(named, arrs, from_sib)]
        sems, ps, lands, tok = _chip_send_start("chip_send_start_" + tag, ps)
        return dict(tag=tag, layer=layer, names=[nm for nm, _ in named], ps=ps, lands=lands, sems=sems), tok

    def reduce_begin(tag, layer, named):
        arrs = [g.reshape(NSHARD, 2, g.shape[1] // 2, g.shape[2]) for _, g in named]
        sems, arrs, lands, tok = _sibling_start("sibling_start_" + tag, arrs)
        return dict(tag=tag, layer=layer, names=[nm for nm, _ in named], arrs=arrs, lands=lands, sems=sems), tok

    def reduce_mid(st, after):
        arrs, from_sib = _sibling_wait("sibling_wait_" + st["tag"], st["arrs"], st["lands"], st["sems"], after)
        ps = [_add_halves("presum_" + nm, a, r, pos, _row_tile(a.shape[2], 256))
              for nm, a, r in zip(st["names"], arrs, from_sib)]
        sems, ps, lands, tok = _chip_send_start("chip_send_start_" + st["tag"], ps)
        return dict(tag=st["tag"], layer=st["layer"], names=st["names"], ps=ps, lands=lands, sems=sems), tok

    def reduce_finish(pend, after):
        ps, lands = _chip_send_wait("chip_send_wait_" + pend["tag"], pend["ps"], pend["lands"], pend["sems"], after)
        for nm, p, r in zip(pend["names"], ps, lands):
            g_acc[nm] = _add_shards("shardsum_" + nm, p, r, pos, _row_tile(p.shape[1], 256), pend["layer"], L,
                                    g_acc.get(nm))

    me_idx = (4 * mx + 2 * my + mc).astype(jnp.int32).reshape(1)
    exchanges = []

    def pack_rows(pieces):
        packed = jnp.concatenate(pieces, axis=0)
        return _pad_to(packed, 0, -(-packed.shape[0] // 8) * 8)

    def unpack_rows(summed, pieces):
        out, off = [], 0
        for p in pieces:
            out.append(summed[off:off + p.shape[0]])
            off += p.shape[0]
        return out

    def small_start(tag, pieces):
        sems, buf, land, token = _exchange_start("exchange_start_" + tag, pack_rows(pieces))
        return dict(tag=tag, pieces=pieces, buf=buf, land=land, sems=sems, token=token)

    def small_finish(st, after):
        buf, land = _exchange_wait("exchange_wait_" + st["tag"], st["buf"], st["land"], st["sems"], after)
        return unpack_rows(_sum_slots(buf, land, me_idx, _row_tile(buf.shape[0], 256)), st["pieces"])

    small = [None] * L
    tt = min(2048, T // 2)
    nt = T // tt
    pending, tok = None, None
    for l in reversed(range(L)):
        sv, wt = saved[l], wts[l]
        ffn_w = [wt[k].reshape(F, D) for k in ffn_keys]
        dx1, dgt, dup, d_norm_ffn = _ffn_bwd(dx, sv["x1"], sv["gt"], sv["up"], row(norm_ffn, l), *ffn_w, tm2, tok)
        tn_a = ((tt, F // 2), lambda j, t: (t, j))
        tn_b = ((tt, D), lambda j, t: (t, 0))
        tn_o = ((F, D), (F // 2, D), lambda j, t: (j, 0), 2, nt)
        g_g, = _tn_matmul("grad_w_gate", dgt, *tn_a, [sv["h2"]], *tn_b, *tn_o)
        g_u, = _tn_matmul("grad_w_up", dup, *tn_a, [sv["h2"]], *tn_b, *tn_o)
        g_d, = _tn_matmul("grad_w_down", sv["act"], *tn_a, [dx], *tn_b, *tn_o)
        if pending is not None:
            reduce_finish(pending, g_d)
        ffn_st, tok = reduce_begin("ffn%d" % l, l, [
            ("w_ffn_gate", g_g.reshape(NSHARD, FS, D)), ("w_ffn_up", g_u.reshape(NSHARD, FS, D)),
            ("w_ffn_down", g_d.reshape(NSHARD, FS, D))])
        wst_b = jnp.swapaxes(sv["ws_b"], 1, 2)
        dya, dyb, dp3, dc1, d_gate_bias, d_sgu_g, d_sgu_b, d_bs, d_ws, d_cln_g, d_cln_b, d_conv_b = _merge_sgu_bwd(
            dx1, sv["proj"], sv["ya"], sv["yb"], sv["mixed"], sv["c1h"], sv["rstd_c"], row(sgu_ln_g, l),
            row(sgu_ln_b, l), sv["ws_b"], wst_b, row(gate_bias, l), row(conv_ln_g, l), row(conv_ln_b, l), wt["w_sq"],
            0, tm2, tok)
        ffn_pend, tok_c = reduce_mid(ffn_st, dya)
        sq_args = ((tt, D), lambda j, t: (t, 0))
        sq_out = ((D, D), (D, D), lambda j, t: (0, 0), 1, nt)
        g_o, = _tn_matmul("grad_w_o", sv["merged"], *sq_args, [dx1], *sq_args, *sq_out)
        g_so, = _tn_matmul("grad_w_sgu_out", sv["gated"], *sq_args, [dyb], *sq_args, *sq_out)
        g_co, = _tn_matmul("grad_w_conv_out", sv["c3"], *sq_args, [dya], *sq_args, *sq_out)
        small[l] = [None, d_gate_bias.reshape(2, D), None, d_conv_b, d_cln_g, d_cln_b, d_sgu_g, d_sgu_b,
                    d_ws.reshape(G * CH * CH // D, D), d_bs.reshape(G * CH // D, D), d_norm_ffn]
        tok_x = None
        if l == 0:
            early = [k for k in range(len(small[0])) if small[0][k] is not None]
            exchanges.append((small_start("early0", [small[0][k] for k in early]), [(0, k) for k in early]))
            tok_x = exchanges[-1][0]["token"]
        dp3, d_cw_s = _conv_bwd(dc1, sv["proj"], dp3, _strips(sv["cw"][::-1]), S, tm, rb, [tok_x, tok_c])
        g_in, = _tn_matmul("grad_w_in", sv["h"], (tt, D), lambda j, t: (t, 0), [dp3], (tt, CS), lambda j, t: (t, j),
                           (NSHARD, D, CS), (None, D, CS), lambda j, t: (j, 0, 0), NSHARD, nt)
        reduce_finish(ffn_pend, g_in)
        mix_named = [("w_in", g_in), ("w_conv_out", g_co.reshape(NSHARD, DQ, D)),
                     ("w_sgu_out", g_so.reshape(NSHARD, DQ, D)), ("w_o", g_o.reshape(NSHARD, DQ, D))]
        if l > 0:
            mix_st, tok = reduce_begin("mix%d" % l, l, mix_named)
            dx, d_norm_mix = _mix_in_bwd(dx1, dp3, sv["x"], row(norm_mix, l), wt["w_in"], 0, tm, tok)
            pending, tok = reduce_mid(mix_st, dx)
        else:
            pending, tok = reduce_start("mix%d" % l, l, mix_named)
            dx, d_norm_mix = _mix_in_bwd(dx1, dp3, sv["x"], row(norm_mix, l), wt["w_in"], 0, tm, tok)
        small[l][0] = d_norm_mix
        small[l][2] = d_cw_s.transpose(1, 0, 2).reshape(32, D)
        if l > 0:
            exchanges.append((small_start("layer%d" % l, small[l]), [(l, k) for k in range(len(small[l]))]))
            tok = [tok, exchanges[-1][0]["token"]]
    reduce_finish(pending, dx)
    grad_x = dx.reshape(BL, S, D)

    names = ["w_in", "w_conv_out", "w_sgu_out", "w_o", "w_ffn_gate", "w_ffn_up", "w_ffn_down"]
    g_full = _join_halves([g_acc[nm] for nm in names])
    g_w_in, g_w_co, g_w_so, g_w_o, g_w_g, g_w_u, g_w_d = [g.reshape(L, 2 * g.shape[2], g.shape[3]) for g in g_full]
    g_w_g = jnp.swapaxes(g_w_g, 1, 2)
    g_w_u = jnp.swapaxes(g_w_u, 1, 2)

    late = [small[0][0], small[0][2], d_norm_final]
    packed = pack_rows(late)
    summed = _sum_slots(packed, _exchange_all(packed), me_idx, _row_tile(packed.shape[0], 256))
    sg = [[None] * len(small[l]) for l in range(L)]
    sg[0][0], sg[0][2], g_norm_final = unpack_rows(summed, late)
    g_norm_final = g_norm_final[0]
    for st, where in exchanges:
        for (l, k), piece in zip(where, small_finish(st, summed)):
            sg[l][k] = piece

    def per_layer(k, shape):
        return jnp.stack([sg[l][k] for l in range(L)]).reshape(shape)

    g_norm_mix = per_layer(0, (L, D))
    g_gate_bias = per_layer(1, (L, 2 * D))
    g_conv_w_full = jnp.stack([sg[l][2][:KW] for l in range(L)])
    g_conv_w = lax.dynamic_slice_in_dim(g_conv_w_full, (2 * mx + my) * CQ, CQ, axis=2).reshape(L, KW, 1, CQ)
    g_conv_b = per_layer(3, (L, D))
    g_conv_ln_g = per_layer(4, (L, D))
    g_conv_ln_b = per_layer(5, (L, D))
    g_sgu_ln_g = per_layer(6, (L, D))
    g_sgu_ln_b = per_layer(7, (L, D))
    g_w_spatial = per_layer(8, (L, G, CH, CH))
    g_b_spatial = per_layer(9, (L, G, CH))
    g_norm_ffn = per_layer(10, (L, D))

    grads = [g_norm_mix, g_w_in, g_gate_bias, g_conv_w, g_conv_b, g_conv_ln_g, g_conv_ln_b, g_w_co, g_sgu_ln_g,
             g_sgu_ln_b, g_w_spatial, g_b_spatial, g_w_so, g_w_o, g_norm_ffn, g_w_g, g_w_u, g_w_d, g_norm_final]
    weights = [norm_mix, w_in, gate_bias, conv_w, conv_b, conv_ln_g, conv_ln_b, w_conv_out, sgu_ln_g, sgu_ln_b,
               w_spatial, b_spatial, w_sgu_out, w_o, norm_ffn, w_ffn_gate, w_ffn_up, w_ffn_down, norm_final]
    ms = [m_norm_mix, m_w_in, m_gate_bias, m_conv_w, m_conv_b, m_conv_ln_g, m_conv_ln_b, m_w_conv_out, m_sgu_ln_g,
          m_sgu_ln_b, m_w_spatial, m_b_spatial, m_w_sgu_out, m_w_o, m_norm_ffn, m_w_ffn_gate, m_w_ffn_up,
          m_w_ffn_down, m_norm_final]
    vs = [v_norm_mix, v_w_in, v_gate_bias, v_conv_w, v_conv_b, v_conv_ln_g, v_conv_ln_b, v_w_conv_out, v_sgu_ln_g,
          v_sgu_ln_b, v_w_spatial, v_b_spatial, v_w_sgu_out, v_w_o, v_norm_ffn, v_w_ffn_gate, v_w_ffn_up,
          v_w_ffn_down, v_norm_final]

    big_idx = [1, 7, 12, 13, 15, 16, 17]
    transposed = [15, 16]
    deltas, new_m, new_v = [None] * 19, [None] * 19, [None] * 19
    for k in big_idx:
        shp = weights[k].shape
        r2 = (shp[0] * shp[1], shp[2])
        res = _adamw("adamw_" + str(k), weights[k].reshape(r2), grads[k].reshape(r2), ms[k].reshape(r2),
                     vs[k].reshape(r2), _row_tile(r2[0], 256), k not in transposed)
        deltas[k], new_m[k], new_v[k] = [a.reshape(shp) for a in res[:3]]
        if k not in transposed:
            grads[k] = res[3].reshape(shp)
    small_idx = [k for k in range(19) if k not in big_idx]
    pick = lambda arrs: [arrs[k].reshape(1, -1) if arrs[k].ndim == 1 else arrs[k] for k in small_idx]
    d_, m_, v_ = _adamw_many(pick(weights), pick(grads), pick(ms), pick(vs))
    for i, k in enumerate(small_idx):
        shp = weights[k].shape
        deltas[k], new_m[k], new_v[k] = d_[i].reshape(shp), m_[i].reshape(shp), v_[i].reshape(shp)

    return (loss, grad_x, *grads, *deltas, *new_m, *new_v)
```

```python
import jax
import jax.numpy as jnp
from jax import lax
from jax.experimental import pallas as pl
from jax.experimental.pallas import tpu as pltpu

F32 = jnp.float32
BF16 = jnp.bfloat16
EPS = 1e-6
ADAM_LR = 0.001
ADAM_B1 = 0.9
ADAM_B2 = 0.999
ADAM_EPS = 1e-08
ADAM_WD = 0.01
ADAM_STEP = 10

NSHARD = 4
LANES = 128
HALO = 16
VMEM_LIMIT = 60 * 1024 * 1024
MESH_ID = pl.DeviceIdType.MESH


def _dot(a, b):
    return jnp.dot(a, b, preferred_element_type=F32)


def _dot_nt(a, b):
    return lax.dot_general(a, b, (((1,), (1,)), ((), ())), preferred_element_type=F32)


def _dot_tn(a, b):
    return lax.dot_general(a, b, (((0,), (0,)), ((), ())), preferred_element_type=F32)


def _sig(z):
    return 1.0 / (1.0 + jnp.exp(-z))


def _res(shape, imap=None):
    nd = len(shape)
    if imap is None:
        imap = lambda *_: (0,) * nd
    return pl.BlockSpec(shape, imap, pipeline_mode=pl.Buffered(1))


def _cparams(sem):
    return pltpu.CompilerParams(dimension_semantics=sem, vmem_limit_bytes=VMEM_LIMIT)


def _sds(shape, dtype):
    return jax.ShapeDtypeStruct(shape, dtype)


def _after(body, n_in, dep):
    deps = [] if dep is None else [d for d in (dep if isinstance(dep, (list, tuple)) else [dep]) if d is not None]
    if not deps:
        return body, [], []

    def wrapped(*refs):
        return body(*refs[:n_in], *refs[n_in + len(deps):])

    return wrapped, [pl.BlockSpec(memory_space=pl.ANY)] * len(deps), deps


def _mix_in_fwd(x2d, g_mix, w_in_g, layer, tm, dep=None):
    T, D = x2d.shape
    CS = w_in_g.shape[2]
    CN = CS // 3

    def body(x_ref, g_ref, w_ref, h_ref, p_ref):
        x = x_ref[...]
        rstd = lax.rsqrt(jnp.mean(x * x, axis=-1, keepdims=True) + EPS)
        h = (x * rstd * g_ref[...]).astype(BF16)
        h_ref[...] = h
        for s in range(NSHARD):
            for j in range(3):
                c0 = s * CS + j * CN
                p_ref[:, c0:c0 + CN] = _dot(h, w_ref[s, :, j * CN:(j + 1) * CN]).astype(BF16)

    body, dep_spec, dep_arg = _after(body, 3, dep)
    return pl.pallas_call(
        body, name="mix_in_fwd", grid=(T // tm,),
        in_specs=[pl.BlockSpec((tm, D), lambda i: (i, 0)), _res((1, D)),
                  _res((NSHARD, D, CS), lambda i: (0, layer, 0))] + dep_spec,
        out_specs=[pl.BlockSpec((tm, D), lambda i: (i, 0)), pl.BlockSpec((tm, NSHARD * CS), lambda i: (i, 0))],
        out_shape=[_sds((T, D), BF16), _sds((T, NSHARD * CS), BF16)],
        compiler_params=_cparams(("parallel",)),
    )(x2d, g_mix, w_in_g, *dep_arg)


def _halo_maps(tm, n_rows):
    nb = tm // HALO
    last = n_rows // HALO - 1
    prev = lambda i: (jnp.maximum(i * nb - 1, 0), 0)
    nxt = lambda i: (jnp.minimum((i + 1) * nb, last), 0)
    return prev, nxt


def _dwconv(pad_ref, w_ref, out_ref, n_strips, tm, kw, rb):
    off = HALO - (kw - 1) // 2

    def strip(cs, carry):
        for r0 in range(0, tm, rb):
            acc = jnp.zeros((rb, LANES), F32)
            for k in range(kw):
                r = r0 + off + k
                acc = acc + w_ref[cs, k:k + 1, :] * pad_ref[cs, r:r + rb, :]
            out_ref[cs, r0:r0 + rb, :] = acc
        return carry

    lax.fori_loop(0, n_strips, strip, 0)


def _fill_c0_pad(pad_ref, pa_ref, pprev_ref, pnext_ref, D, tm, first, last):
    for cs in range(D // LANES):
        lo, hi = cs * LANES, (cs + 1) * LANES

        def c0_of(ref):
            return ref[:, lo:hi].astype(F32) * _sig(ref[:, D + lo:D + hi].astype(F32))

        pad_ref[cs, HALO:HALO + tm, :] = c0_of(pa_ref)
        pad_ref[cs, 0:HALO, :] = jnp.where(first, 0.0, c0_of(pprev_ref))
        pad_ref[cs, HALO + tm:HALO + tm + HALO, :] = jnp.where(last, 0.0, c0_of(pnext_ref))


def _conv_fwd(proj, conv_w_s, conv_b, ln_g, ln_b, w_sq_g, layer, seq, tm, rb):
    T = proj.shape[0]
    D = conv_b.shape[1]
    DQ = D // NSHARD
    NSTR = D // LANES
    KW = 31
    tps = seq // tm
    prev, nxt = _halo_maps(tm, T)

    def body(pa_ref, pprev_ref, pnext_ref, w_ref, b_ref, g_ref, be_ref, wco_ref,
             c1h_ref, rstd_ref, c3_ref, ya_ref, pad_ref, c1s_ref):
        i = pl.program_id(0)
        first = (i % tps) == 0
        last = (i % tps) == tps - 1
        _fill_c0_pad(pad_ref, pa_ref, pprev_ref, pnext_ref, D, tm, first, last)
        _dwconv(pad_ref, w_ref, c1s_ref, NSTR, tm, KW, rb)
        wco = wco_ref[...].reshape(D, D)
        for r0 in (0, tm // 2):
            rows = slice(r0, r0 + tm // 2)
            c1 = jnp.concatenate([c1s_ref[cs, rows, :] for cs in range(NSTR)], axis=1) + b_ref[...]
            mu = jnp.mean(c1, axis=-1, keepdims=True)
            cc = c1 - mu
            rstd = lax.rsqrt(jnp.mean(cc * cc, axis=-1, keepdims=True) + EPS)
            c1h = cc * rstd
            c1h_ref[rows, :] = c1h.astype(BF16)
            rstd_ref[rows, :] = rstd
            c2 = c1h * g_ref[...] + be_ref[...]
            c3 = (c2 * _sig(c2)).astype(BF16)
            c3_ref[rows, :] = c3
            ya_ref[rows, :] = _dot(c3, wco).astype(BF16)

    row = lambda i: (i, 0)
    return pl.pallas_call(
        body, name="conv_fwd", grid=(T // tm,),
        in_specs=[pl.BlockSpec((tm, 2 * D), row), pl.BlockSpec((HALO, 2 * D), prev), pl.BlockSpec((HALO, 2 * D), nxt),
                  _res((NSTR, 32, LANES)), _res((1, D)), _res((1, D)), _res((1, D)),
                  _res((NSHARD, DQ, D), lambda i: (0, layer * 3 + 0, 0))],
        out_specs=[pl.BlockSpec((tm, D), row), pl.BlockSpec((tm, 1), row), pl.BlockSpec((tm, D), row),
                   pl.BlockSpec((tm, D), row)],
        out_shape=[_sds((T, D), BF16), _sds((T, 1), F32), _sds((T, D), BF16), _sds((T, D), BF16)],
        scratch_shapes=[pltpu.VMEM((NSTR, tm + 2 * HALO, LANES), F32), pltpu.VMEM((NSTR, tm, LANES), F32)],
        compiler_params=_cparams(("parallel",)),
    )(proj, proj, proj, conv_w_s, conv_b, ln_g, ln_b, w_sq_g)


def _ffn_chunks(F):
    assert F % 256 == 0, F
    return [(c0, min(512, F - c0)) for c0 in range(0, F, 512)]


def _sgu_ffn_fwd(proj, ya, x2d, ln_g, ln_b, ws_b, bs_b, gate_bias, w_sq_g, layer, g_ffn, wgt, wut, wd, tm, head=None):
    T, D = x2d.shape
    DQ = D // NSHARD
    G, CH, _ = ws_b.shape
    GD = D // G
    F = wd.shape[0]
    n = T // tm
    n_in = 17 if head else 15

    def body(*refs):
        (puv_ref, pg_ref, ya_ref, x_ref, g_ref, be_ref, ws_ref, bsb_ref, gb_ref, wso_ref, wo_ref,
         gf_ref, wg_ref, wu_ref, wd_ref) = refs[:15]
        mixed_ref, gated_ref, yb_ref, merged_ref, x1_ref, h2_ref, gt_ref, up_ref, act_ref = refs[n_in:n_in + 9]
        mix_scr = refs[-2] if head else refs[-1]
        u = puv_ref[:, :D].astype(F32)
        v = puv_ref[:, D:].astype(F32)
        mu = jnp.mean(v, axis=-1, keepdims=True)
        vc = v - mu
        rstd = lax.rsqrt(jnp.mean(vc * vc, axis=-1, keepdims=True) + EPS)
        vn = (vc * rstd * g_ref[...] + be_ref[...]).astype(BF16)
        nch = tm // CH
        for g in range(G):
            cols = slice(g * GD, (g + 1) * GD)
            rhs = jnp.concatenate([vn[ch * CH:(ch + 1) * CH, cols] for ch in range(nch)], axis=1)
            res = _dot(ws_ref[g], rhs)
            for ch in range(nch):
                mix_scr[ch * CH:(ch + 1) * CH, cols] = res[:, ch * GD:(ch + 1) * GD] + bsb_ref[:, cols]
        mixed = mix_scr[...]
        mixed_ref[...] = mixed.astype(BF16)
        gated = (u * mixed).astype(BF16)
        gated_ref[...] = gated
        yb = _dot(gated, wso_ref[...].reshape(D, D))
        yb_ref[...] = yb.astype(BF16)
        sa = _sig(pg_ref[:, :D].astype(F32) + gb_ref[:, :D])
        sb = _sig(pg_ref[:, D:].astype(F32) + gb_ref[:, D:])
        merged = (sa * ya_ref[...].astype(F32) + sb * yb).astype(BF16)
        merged_ref[...] = merged
        x1 = x_ref[...] + _dot(merged, wo_ref[...].reshape(D, D))
        x1_ref[...] = x1

        rstd = lax.rsqrt(jnp.mean(x1 * x1, axis=-1, keepdims=True) + EPS)
        h2 = (x1 * rstd * gf_ref[...]).astype(BF16)
        h2_ref[...] = h2
        acc = x1
        chunks = _ffn_chunks(F)

        def gate_up(c0, cw):
            return _dot_nt(h2, wg_ref[c0:c0 + cw, :]), _dot_nt(h2, wu_ref[c0:c0 + cw, :])

        nxt = gate_up(*chunks[0])
        for ci, (c0, cw) in enumerate(chunks):
            gt, up = nxt
            if ci + 1 < len(chunks):
                nxt = gate_up(*chunks[ci + 1])
            gt_ref[:, c0:c0 + cw] = gt.astype(BF16)
            up_ref[:, c0:c0 + cw] = up.astype(BF16)
            act = (gt * _sig(gt) * up).astype(BF16)
            act_ref[:, c0:c0 + cw] = act
            acc = acc + _dot(act, wd_ref[c0:c0 + cw, :])
        if not head:
            refs[n_in + 9][...] = acc
            return

        gfin_ref, t_ref = refs[15:17]
        dx_ref, loss_ref, dgf_ref = refs[n_in + 9:n_in + 12]
        sq_ref = refs[-1]
        i = pl.program_id(0)

        @pl.when(i == 0)
        def _():
            sq_ref[...] = jnp.zeros_like(sq_ref)
            dgf_ref[...] = jnp.zeros_like(dgf_ref)

        gfin = gfin_ref[...]
        rstd2 = lax.rsqrt(jnp.mean(acc * acc, axis=-1, keepdims=True) + EPS)
        xh = acc * rstd2
        diff = xh * gfin - t_ref[...]
        sq_ref[...] += jnp.sum(diff * diff, axis=0, keepdims=True)
        dy = diff * (1.0 / D)
        dgf_ref[...] += jnp.sum(dy * xh, axis=0, keepdims=True)
        dxh = dy * gfin
        dx_ref[...] = rstd2 * (dxh - xh * jnp.mean(dxh * xh, axis=-1, keepdims=True))

        @pl.when(i == n - 1)
        def _():
            tot = jnp.sum(sq_ref[...], axis=-1, keepdims=True) * (0.5 / D)
            loss_ref[...] = jnp.broadcast_to(tot, loss_ref.shape)

    row = lambda i: (i, 0)
    fixed2 = lambda i: (0, 0)
    wide = pl.BlockSpec((tm, F), row)
    in_specs = [pl.BlockSpec((tm, 2 * D), lambda i: (i, 1)), pl.BlockSpec((tm, 2 * D), lambda i: (i, 2)),
                pl.BlockSpec((tm, D), row), pl.BlockSpec((tm, D), row),
                _res((1, D)), _res((1, D)), _res((G, CH, CH)), _res((CH, D)), _res((1, 2 * D)),
                _res((NSHARD, DQ, D), lambda i: (0, layer * 3 + 1, 0)),
                _res((NSHARD, DQ, D), lambda i: (0, layer * 3 + 2, 0)),
                _res((1, D)), _res((F, D)), _res((F, D)), _res((F, D))]
    out_specs = [pl.BlockSpec((tm, D), row)] * 6 + [wide, wide, wide, pl.BlockSpec((tm, D), row)]
    out_shape = ([_sds((T, D), BF16)] * 4 + [_sds((T, D), F32), _sds((T, D), BF16)] + [_sds((T, F), BF16)] * 3
                 + [_sds((T, D), F32)])
    scratch = [pltpu.VMEM((tm, D), F32)]
    args = [proj, proj, ya, x2d, ln_g, ln_b, ws_b, bs_b, gate_bias, w_sq_g, w_sq_g, g_ffn, wgt, wut, wd]
    if head:
        in_specs += [_res((1, D)), pl.BlockSpec((tm, D), row)]
        out_specs += [pl.BlockSpec((1, LANES), fixed2), pl.BlockSpec((1, D), fixed2)]
        out_shape += [_sds((1, LANES), F32), _sds((1, D), F32)]
        scratch += [pltpu.VMEM((1, D), F32)]
        args += list(head)
    return pl.pallas_call(
        body, name="sgu_ffn_fwd_head" if head else "sgu_ffn_fwd", grid=(n,),
        in_specs=in_specs, out_specs=out_specs, out_shape=out_shape, scratch_shapes=scratch,
        compiler_params=_cparams(("arbitrary",) if head else ("parallel",)),
    )(*args)


def _ffn_bwd(dx2, x1, gt, up, g_ffn, wgt, wut, wd, tm, dep=None):
    T, D = x1.shape
    F = wd.shape[0]

    def body(dx2_ref, x1_ref, gt_ref, up_ref, g_ref, wg_ref, wu_ref, wd_ref, dx1_ref, dgt_ref, dup_ref, dg_ref):
        i = pl.program_id(0)

        @pl.when(i == 0)
        def _():
            dg_ref[...] = jnp.zeros_like(dg_ref)

        dx2 = dx2_ref[...]
        dx2b = dx2.astype(BF16)
        dh2 = jnp.zeros((tm, D), F32)
        chunks = _ffn_chunks(F)
        dact_next = _dot_nt(dx2b, wd_ref[0:chunks[0][1], :])
        for ci, (c0, cw) in enumerate(chunks):
            dact = dact_next
            if ci + 1 < len(chunks):
                n0, nw = chunks[ci + 1]
                dact_next = _dot_nt(dx2b, wd_ref[n0:n0 + nw, :])
            g = gt_ref[:, c0:c0 + cw].astype(F32)
            u = up_ref[:, c0:c0 + cw].astype(F32)
            sg = _sig(g)
            dup = (dact * (g * sg)).astype(BF16)
            dgt = (dact * u * (sg * (1.0 + g * (1.0 - sg)))).astype(BF16)
            dgt_ref[:, c0:c0 + cw] = dgt
            dup_ref[:, c0:c0 + cw] = dup
            dh2 = dh2 + _dot(dgt, wg_ref[c0:c0 + cw, :]) + _dot(dup, wu_ref[c0:c0 + cw, :])
        x = x1_ref[...]
        rstd = lax.rsqrt(jnp.mean(x * x, axis=-1, keepdims=True) + EPS)
        xh = x * rstd
        dg_ref[...] += jnp.sum(dh2 * xh, axis=0, keepdims=True)
        dxh = dh2 * g_ref[...]
        dx1_ref[...] = dx2 + rstd * (dxh - xh * jnp.mean(dxh * xh, axis=-1, keepdims=True))

    row = lambda i: (i, 0)
    body, dep_spec, dep_arg = _after(body, 8, dep)
    return pl.pallas_call(
        body, name="ffn_bwd", grid=(T // tm,),
        in_specs=[pl.BlockSpec((tm, D), row), pl.BlockSpec((tm, D), row), pl.BlockSpec((tm, F), row),
                  pl.BlockSpec((tm, F), row), _res((1, D)), _res((F, D)), _res((F, D)), _res((F, D))] + dep_spec,
        out_specs=[pl.BlockSpec((tm, D), row), pl.BlockSpec((tm, F), row), pl.BlockSpec((tm, F), row),
                   pl.BlockSpec((1, D), lambda i: (0, 0))],
        out_shape=[_sds((T, D), F32), _sds((T, F), BF16), _sds((T, F), BF16), _sds((1, D), F32)],
        compiler_params=_cparams(("arbitrary",)),
    )(dx2, x1, gt, up, g_ffn, wgt, wut, wd, *dep_arg)


def _merge_sgu_bwd(dx1, proj, ya, yb, mixed, c1h, rstd_c, ln_g, ln_b, ws_b, wst_b, gate_bias, cln_g, cln_b, w_sq_g,
                   layer, tm, dep=None):
    T, D = dx1.shape
    DQ = D // NSHARD
    G, CH, _ = ws_b.shape
    GD = D // G

    def body(dx1_ref, puv_ref, pg_ref, ya_ref, yb_ref, mixed_ref, c1h_ref, rstdc_ref, g_ref, be_ref, ws_ref, wst_ref,
             gb_ref, cg_ref, cbe_ref, wco_ref, wso_ref, wo_ref,
             dya_ref, dyb_ref, dp_ref, dc1_ref, dgb_ref, dlg_ref, dlb_ref, dbs_ref, dws_ref, dcg_ref, dcbe_ref,
             dcb_ref, dvn_scr, dbs_scr):
        i = pl.program_id(0)

        @pl.when(i == 0)
        def _():
            for r in (dgb_ref, dlg_ref, dlb_ref, dws_ref, dbs_scr, dcg_ref, dcbe_ref, dcb_ref):
                r[...] = jnp.zeros_like(r)

        dmerged = _dot_nt(dx1_ref[...].astype(BF16), wo_ref[...].reshape(D, D))
        sa = _sig(pg_ref[:, :D].astype(F32) + gb_ref[:, :D])
        sb = _sig(pg_ref[:, D:].astype(F32) + gb_ref[:, D:])
        dya = (dmerged * sa).astype(BF16)
        dyb = (dmerged * sb).astype(BF16)
        dya_ref[...] = dya
        dyb_ref[...] = dyb
        dc3 = _dot_nt(dya, wco_ref[...].reshape(D, D))
        dgated = _dot_nt(dyb, wso_ref[...].reshape(D, D))

        c1h = c1h_ref[...].astype(F32)
        c2 = c1h * cg_ref[...] + cbe_ref[...]
        sg = _sig(c2)
        dc2 = dc3 * (sg * (1.0 + c2 * (1.0 - sg)))
        dcg_ref[...] += jnp.sum(dc2 * c1h, axis=0, keepdims=True)
        dcbe_ref[...] += jnp.sum(dc2, axis=0, keepdims=True)
        dch = dc2 * cg_ref[...]
        dc1 = rstdc_ref[...] * (dch - jnp.mean(dch, axis=-1, keepdims=True)
                                - c1h * jnp.mean(dch * c1h, axis=-1, keepdims=True))
        dc1_ref[...] = dc1
        dcb_ref[...] += jnp.sum(dc1, axis=0, keepdims=True)

        dga = dmerged * ya_ref[...].astype(F32) * (sa * (1.0 - sa))
        dgb = dmerged * yb_ref[...].astype(F32) * (sb * (1.0 - sb))
        dp_ref[:, 4 * D:5 * D] = dga.astype(BF16)
        dp_ref[:, 5 * D:6 * D] = dgb.astype(BF16)
        dgb_ref[:, :D] += jnp.sum(dga, axis=0, keepdims=True)
        dgb_ref[:, D:] += jnp.sum(dgb, axis=0, keepdims=True)

        u = puv_ref[:, :D].astype(F32)
        v = puv_ref[:, D:].astype(F32)
        dp_ref[:, 2 * D:3 * D] = (dgated * mixed_ref[...].astype(F32)).astype(BF16)
        dmixed = dgated * u
        mu = jnp.mean(v, axis=-1, keepdims=True)
        vc = v - mu
        rstd = lax.rsqrt(jnp.mean(vc * vc, axis=-1, keepdims=True) + EPS)
        vh = vc * rstd
        vn = (vh * g_ref[...] + be_ref[...]).astype(BF16)
        dmb = dmixed.astype(BF16)
        nch = tm // CH
        bs_part = dmixed[0:CH, :]
        for ch in range(1, nch):
            bs_part = bs_part + dmixed[ch * CH:(ch + 1) * CH, :]
        dbs_scr[...] += bs_part
        for g in range(G):
            cols = slice(g * GD, (g + 1) * GD)
            dm_g = jnp.concatenate([dmb[ch * CH:(ch + 1) * CH, cols] for ch in range(nch)], axis=1)
            vn_g = jnp.concatenate([vn[ch * CH:(ch + 1) * CH, cols] for ch in range(nch)], axis=1)
            dws_ref[g] += _dot_nt(dm_g, vn_g)
            dvn_g = _dot(wst_ref[g], dm_g)
            for ch in range(nch):
                dvn_scr[ch * CH:(ch + 1) * CH, cols] = dvn_g[:, ch * GD:(ch + 1) * GD]
        dvn = dvn_scr[...]
        dlg_ref[...] += jnp.sum(dvn * vh, axis=0, keepdims=True)
        dlb_ref[...] += jnp.sum(dvn, axis=0, keepdims=True)
        dxh = dvn * g_ref[...]
        dv = rstd * (dxh - jnp.mean(dxh, axis=-1, keepdims=True) - vh * jnp.mean(dxh * vh, axis=-1, keepdims=True))
        dp_ref[:, 3 * D:4 * D] = dv.astype(BF16)

        @pl.when(i == pl.num_programs(0) - 1)
        def _():
            for g in range(G):
                blk = dbs_scr[:, g * GD:(g + 1) * GD]
                if GD != CH:
                    blk = jnp.concatenate([blk, jnp.zeros((CH, CH - GD), F32)], axis=1)
                dbs_ref[:, g * CH:(g + 1) * CH] = jnp.sum(blk.T, axis=0, keepdims=True)

    row = lambda i: (i, 0)
    fixed2 = lambda i: (0, 0)
    body, dep_spec, dep_arg = _after(body, 18, dep)
    vec = pl.BlockSpec((1, D), fixed2)
    return pl.pallas_call(
        body, name="merge_sgu_bwd", grid=(T // tm,),
        in_specs=[pl.BlockSpec((tm, D), row), pl.BlockSpec((tm, 2 * D), lambda i: (i, 1)),
                  pl.BlockSpec((tm, 2 * D), lambda i: (i, 2)), pl.BlockSpec((tm, D), row), pl.BlockSpec((tm, D), row),
                  pl.BlockSpec((tm, D), row), pl.BlockSpec((tm, D), row), pl.BlockSpec((tm, 1), row),
                  _res((1, D)), _res((1, D)), _res((G, CH, CH)), _res((G, CH, CH)), _res((1, 2 * D)),
                  _res((1, D)), _res((1, D)),
                  _res((NSHARD, DQ, D), lambda i: (0, layer * 3 + 0, 0)),
                  _res((NSHARD, DQ, D), lambda i: (0, layer * 3 + 1, 0)),
                  _res((NSHARD, DQ, D), lambda i: (0, layer * 3 + 2, 0))] + dep_spec,
        out_specs=[pl.BlockSpec((tm, D), row), pl.BlockSpec((tm, D), row), pl.BlockSpec((tm, 6 * D), row),
                   pl.BlockSpec((tm, D), row), pl.BlockSpec((1, 2 * D), fixed2), vec, vec,
                   pl.BlockSpec((1, G * CH), fixed2), pl.BlockSpec((G, CH, CH), lambda i: (0, 0, 0)), vec, vec, vec],
        out_shape=[_sds((T, D), BF16), _sds((T, D), BF16), _sds((T, 6 * D), BF16), _sds((T, D), F32),
                   _sds((1, 2 * D), F32), _sds((1, D), F32), _sds((1, D), F32), _sds((1, G * CH), F32),
                   _sds((G, CH, CH), F32), _sds((1, D), F32), _sds((1, D), F32), _sds((1, D), F32)],
        scratch_shapes=[pltpu.VMEM((tm, D), F32), pltpu.VMEM((CH, D), F32)],
        compiler_params=_cparams(("arbitrary",)),
    )(dx1, proj, proj, ya, yb, mixed, c1h, rstd_c, ln_g, ln_b, ws_b, wst_b, gate_bias, cln_g, cln_b,
      w_sq_g, w_sq_g, w_sq_g, *dep_arg)


def _conv_bwd(dc1, proj, dp3, conv_wf_s, seq, tm, rb, dep=None):
    T, D = dc1.shape
    NSTR = D // LANES
    KW = 31
    PADK = (KW - 1) // 2
    tps = seq // tm
    prev, nxt = _halo_maps(tm, T)
    n = T // tm

    def body(dc_ref, dcprev_ref, dcnext_ref, pa_ref, pprev_ref, pnext_ref, wf_ref, dp_in_ref,
             dp_ref, dw_ref, pad_ref, dpad_ref, dc0_ref, dwacc_ref):
        del dp_in_ref
        i = pl.program_id(0)
        first = (i % tps) == 0
        last = (i % tps) == tps - 1

        @pl.when(i == 0)
        def _():
            dwacc_ref[...] = jnp.zeros_like(dwacc_ref)

        _fill_c0_pad(pad_ref, pa_ref, pprev_ref, pnext_ref, D, tm, first, last)
        for cs in range(NSTR):
            lo, hi = cs * LANES, (cs + 1) * LANES
            dpad_ref[cs, HALO:HALO + tm, :] = dc_ref[:, lo:hi]
            dpad_ref[cs, 0:HALO, :] = jnp.where(first, 0.0, dcprev_ref[:, lo:hi])
            dpad_ref[cs, HALO + tm:HALO + tm + HALO, :] = jnp.where(last, 0.0, dcnext_ref[:, lo:hi])
        _dwconv(dpad_ref, wf_ref, dc0_ref, NSTR, tm, KW, rb)

        def strip(cs, carry):
            for r0 in range(0, tm, rb):
                d = dpad_ref[cs, HALO + r0:HALO + r0 + rb, :]
                for k in range(KW):
                    r = r0 + HALO - PADK + k
                    prod = d * pad_ref[cs, r:r + rb, :]
                    dwacc_ref[cs, k * 8:(k + 1) * 8, :] += jnp.sum(prod.reshape(rb // 8, 8, LANES), axis=0)
            return carry

        lax.fori_loop(0, NSTR, strip, 0)

        for cs in range(NSTR):
            lo, hi = cs * LANES, (cs + 1) * LANES
            av = pa_ref[:, lo:hi].astype(F32)
            sg = _sig(pa_ref[:, D + lo:D + hi].astype(F32))
            dc0 = dc0_ref[cs]
            dp_ref[:, lo:hi] = (dc0 * sg).astype(BF16)
            dp_ref[:, D + lo:D + hi] = (dc0 * av * (sg * (1.0 - sg))).astype(BF16)

        @pl.when(i == n - 1)
        def _():
            for cs in range(NSTR):
                dw_ref[cs] = jnp.sum(dwacc_ref[cs].reshape(32, 8, LANES), axis=1)

    row = lambda i: (i, 0)
    body, dep_spec, dep_arg = _after(body, 8, dep)
    return pl.pallas_call(
        body, name="conv_bwd", grid=(n,),
        in_specs=[pl.BlockSpec((tm, D), row), pl.BlockSpec((HALO, D), prev), pl.BlockSpec((HALO, D), nxt),
                  pl.BlockSpec((tm, 2 * D), row), pl.BlockSpec((HALO, 2 * D), prev), pl.BlockSpec((HALO, 2 * D), nxt),
                  _res((NSTR, 32, LANES)), pl.BlockSpec(memory_space=pl.ANY)] + dep_spec,
        out_specs=[pl.BlockSpec((tm, 2 * D), row),
                   pl.BlockSpec((NSTR, 32, LANES), lambda i: (0, 0, 0))],
        out_shape=[_sds(dp3.shape, BF16), _sds((NSTR, 32, LANES), F32)],
        scratch_shapes=[pltpu.VMEM((NSTR, tm + 2 * HALO, LANES), F32), pltpu.VMEM((NSTR, tm + 2 * HALO, LANES), F32),
                        pltpu.VMEM((NSTR, tm, LANES), F32), pltpu.VMEM((NSTR, 32 * 8, LANES), F32)],
        input_output_aliases={7: 0},
        compiler_params=_cparams(("arbitrary",)),
    )(dc1, dc1, dc1, proj, proj, proj, conv_wf_s, dp3, *dep_arg)


def _mix_in_bwd(dx1, dp3, x2d, g_mix, w_in_g, layer, tm, dep=None):
    T, D = x2d.shape
    CS = w_in_g.shape[2]
    CN = CS // 3

    def body(dx1_ref, dp_ref, x_ref, g_ref, w_ref, dx_ref, dg_ref):
        i = pl.program_id(0)

        @pl.when(i == 0)
        def _():
            dg_ref[...] = jnp.zeros_like(dg_ref)

        dh = jnp.zeros((tm, D), F32)
        for j in range(12):
            dh = dh + _dot_nt(dp_ref[:, j * CN:(j + 1) * CN], w_ref[j // 3, :, (j % 3) * CN:(j % 3 + 1) * CN])
        x = x_ref[...]
        rstd = lax.rsqrt(jnp.mean(x * x, axis=-1, keepdims=True) + EPS)
        xh = x * rstd
        dg_ref[...] += jnp.sum(dh * xh, axis=0, keepdims=True)
        dxh = dh * g_ref[...]
        dx_ref[...] = dx1_ref[...] + rstd * (dxh - xh * jnp.mean(dxh * xh, axis=-1, keepdims=True))

    row = lambda i: (i, 0)
    body, dep_spec, dep_arg = _after(body, 5, dep)
    return pl.pallas_call(
        body, name="mix_in_bwd", grid=(T // tm,),
        in_specs=[pl.BlockSpec((tm, D), row), pl.BlockSpec((tm, 6 * D), row),
                  pl.BlockSpec((tm, D), row), _res((1, D)), _res((NSHARD, D, CS), lambda i: (0, layer, 0))] + dep_spec,
        out_specs=[pl.BlockSpec((tm, D), row), pl.BlockSpec((1, D), lambda i: (0, 0))],
        out_shape=[_sds((T, D), F32), _sds((1, D), F32)],
        compiler_params=_cparams(("arbitrary",)),
    )(dx1, dp3, x2d, g_mix, w_in_g, *dep_arg)


def _tn_matmul(name, a, a_block, a_map, bs, b_block, b_map, out_shape, out_block, out_map, nj, nt):
    kk = [d for d in a_block if d is not None][-1]
    nn = [d for d in b_block if d is not None][-1]
    nb = len(bs)
    a_list = a if isinstance(a, (list, tuple)) else [a]
    na = len(a_list)

    def body(*refs):
        a_refs, b_refs = refs[:na], refs[na:na + nb]
        o_refs, acc_refs = refs[-2 * nb:-nb], refs[-nb:]
        t = pl.program_id(1)

        @pl.when(t == 0)
        def _():
            for acc_ref in acc_refs:
                acc_ref[...] = jnp.zeros_like(acc_ref)

        a_ts = [a_ref[...].astype(BF16) for a_ref in a_refs]
        for i, (b_ref, acc_ref) in enumerate(zip(b_refs, acc_refs)):
            acc_ref[...] += _dot_tn(a_ts[i % na], b_ref[...].astype(BF16))

        @pl.when(t == nt - 1)
        def _():
            for o_ref, acc_ref in zip(o_refs, acc_refs):
                o_ref[...] = acc_ref[...].astype(o_ref.dtype)

    return pl.pallas_call(
        body, name=name, grid=(nj, nt),
        in_specs=[pl.BlockSpec(a_block, a_map)] * na + [pl.BlockSpec(b_block, b_map)] * nb,
        out_specs=[pl.BlockSpec(out_block, out_map)] * nb, out_shape=[_sds(out_shape, BF16)] * nb,
        scratch_shapes=[pltpu.VMEM((kk, nn), F32)] * nb,
        compiler_params=_cparams(("parallel", "arbitrary")),
    )(*a_list, *bs)


def _place_shard(name, w, pos, dtype, tr, dep=None, layer=None, into=None, row_off=0, out_rows=None):
    R, C = w.shape[-2:]
    out_rows = out_rows or R

    def body(pos_ref, w_ref, *rest):
        del pos_ref
        rest[-1][...] = w_ref[...].astype(dtype)

    in_spec = (pl.BlockSpec((tr, C), lambda r, pos: (r, 0)) if layer is None else
               pl.BlockSpec((None, tr, C), lambda r, pos: (layer, r, 0)))
    extra, extra_args = ([], []) if into is None else ([pl.BlockSpec(memory_space=pl.ANY)], [into])
    body, dep_spec, dep_arg = _after(body, 2 + len(extra), dep)
    grid_spec = pltpu.PrefetchScalarGridSpec(
        num_scalar_prefetch=1, grid=(R // tr,), in_specs=[in_spec] + extra + dep_spec,
        out_specs=pl.BlockSpec((None, tr, C), lambda r, pos: (pos[1], row_off // tr + r, 0)))
    return pl.pallas_call(body, name=name, grid_spec=grid_spec, out_shape=_sds((NSHARD, out_rows, C), dtype),
                          input_output_aliases={} if into is None else {2: 0},
                          compiler_params=_cparams(("parallel",)))(pos, w, *extra_args, *dep_arg)


def _add_halves(name, g, rbuf, pos, tr):
    NS, _, H, C = g.shape

    def body(pos_ref, g_ref, r_ref, o_ref):
        del pos_ref
        o_ref[...] = (g_ref[...].astype(F32) + r_ref[...].astype(F32)).astype(BF16)

    grid_spec = pltpu.PrefetchScalarGridSpec(
        num_scalar_prefetch=1, grid=(NS, H // tr),
        in_specs=[pl.BlockSpec((None, None, tr, C), lambda s, r, pos: (s, pos[0], r, 0)),
                  pl.BlockSpec((None, tr, C), lambda s, r, pos: (s, r, 0))],
        out_specs=pl.BlockSpec((None, tr, C), lambda s, r, pos: (s, r, 0)))
    return pl.pallas_call(body, name=name, grid_spec=grid_spec, out_shape=_sds((NS, H, C), BF16),
                          compiler_params=_cparams(("parallel", "parallel")))(pos, g, rbuf)


def _add_shards(name, p, rbuf, pos, tr, layer, n_layers, prev):
    _, H, C = p.shape

    def body(pos_ref, p_ref, r_ref, *rest):
        del pos_ref
        o_ref = rest[-1]
        acc = p_ref[...].astype(F32)
        for j in range(3):
            acc = acc + r_ref[j].astype(F32)
        o_ref[...] = acc

    in_specs = [pl.BlockSpec((None, tr, C), lambda r, pos: (pos[1], r, 0)),
                pl.BlockSpec((3, tr, C), lambda r, pos: (0, r, 0))]
    args = [pos, p, rbuf]
    aliases = {}
    if prev is not None:
        in_specs.append(pl.BlockSpec(memory_space=pl.ANY))
        args.append(prev)
        aliases = {3: 0}
    grid_spec = pltpu.PrefetchScalarGridSpec(
        num_scalar_prefetch=1, grid=(H // tr,), in_specs=in_specs,
        out_specs=pl.BlockSpec((None, None, tr, C), lambda r, pos: (layer, pos[0], r, 0)))
    return pl.pallas_call(body, name=name, grid_spec=grid_spec, out_shape=_sds((n_layers, 2, H, C), F32),
                          input_output_aliases=aliases, compiler_params=_cparams(("parallel",)))(*args)


def _sum_slots(own, land, me, tr):
    NS8, R, C = land.shape

    def body(me_ref, own_ref, l_ref, o_ref):
        acc = None
        for j in range(NS8):
            term = jnp.where(me_ref[0] == j, own_ref[...], l_ref[j])
            acc = term if acc is None else acc + term
        o_ref[...] = acc

    grid_spec = pltpu.PrefetchScalarGridSpec(
        num_scalar_prefetch=1, grid=(R // tr,),
        in_specs=[pl.BlockSpec((tr, C), lambda i, me: (i, 0)), pl.BlockSpec((NS8, tr, C), lambda i, me: (0, i, 0))],
        out_specs=pl.BlockSpec((tr, C), lambda i, me: (i, 0)))
    return pl.pallas_call(body, name="sum_slots", grid_spec=grid_spec, out_shape=_sds((R, C), F32),
                          compiler_params=_cparams(("parallel",)))(me, own, land)


def _adamw_update(w_ref, g_ref, m_ref, v_ref, d_ref, mo_ref, vo_ref):
    g_ = g_ref[...]
    m_ = ADAM_B1 * m_ref[...] + (1.0 - ADAM_B1) * g_
    v_ = ADAM_B2 * v_ref[...] + (1.0 - ADAM_B2) * (g_ * g_)
    mo_ref[...] = m_
    vo_ref[...] = v_
    m_hat = m_ / (1.0 - ADAM_B1 ** ADAM_STEP)
    v_hat = v_ / (1.0 - ADAM_B2 ** ADAM_STEP)
    d_ref[...] = -ADAM_LR * (m_hat / (jnp.sqrt(v_hat) + ADAM_EPS) + ADAM_WD * w_ref[...])


def _adamw(name, w, g, m, v, tr, emit_g):
    R, C = w.shape
    n_out = 4 if emit_g else 3

    def body(w_ref, g_ref, m_ref, v_ref, d_ref, mo_ref, vo_ref, *go_ref):
        _adamw_update(w_ref, g_ref, m_ref, v_ref, d_ref, mo_ref, vo_ref)
        if emit_g:
            go_ref[0][...] = g_ref[...]

    spec = pl.BlockSpec((tr, C), lambda i: (i, 0))
    return pl.pallas_call(
        body, name=name, grid=(R // tr,), in_specs=[spec] * 4, out_specs=[spec] * n_out,
        out_shape=[_sds((R, C), F32)] * n_out, compiler_params=_cparams(("parallel",)))(w, g, m, v)


def _adamw_many(ws, gs, ms, vs):
    n = len(ws)

    def body(*refs):
        ins, outs = refs[:4 * n], refs[4 * n:]
        for k in range(n):
            _adamw_update(ins[k], ins[n + k], ins[2 * n + k], ins[3 * n + k], outs[k], outs[n + k], outs[2 * n + k])

    vmem = pl.BlockSpec(memory_space=pltpu.VMEM)
    res = pl.pallas_call(
        body, name="adamw_small", in_specs=[vmem] * (4 * n), out_specs=[vmem] * (3 * n),
        out_shape=[_sds(w.shape, F32) for w in ws] * 3,
        compiler_params=pltpu.CompilerParams(vmem_limit_bytes=VMEM_LIMIT))(*ws, *gs, *ms, *vs)
    return list(res[:n]), list(res[n:2 * n]), list(res[2 * n:])


def _row_tile(rows, cap):
    best = rows
    for t in range(8, min(rows, cap) + 1, 8):
        if rows % t == 0:
            best = t
    return best


HBM_SPEC = pl.BlockSpec(memory_space=pltpu.HBM)
SEM_SPEC = pl.BlockSpec(memory_space=pltpu.SEMAPHORE)
DATAFLOW = pltpu.SideEffectType.DATAFLOW_SIDE_EFFECTING
DMA_SEM = pltpu.SemaphoreType.DMA


def _hbm(a):
    return pltpu.with_memory_space_constraint(a, pltpu.HBM)


def _mesh_pos():
    return lax.axis_index("x"), lax.axis_index("y"), lax.axis_index("c")


def _other_chips(x, y):
    return [(1 - x, y), (x, 1 - y), (1 - x, 1 - y)]


def _half_rows(buf, shard, core):
    h = buf.shape[1] // 2
    return buf.at[shard, pl.ds(core * h, h), :]


def _ici_copy(buf, j, send, recv, landing):
    x, y, c = _mesh_pos()
    px, py = _other_chips(x, y)[j]
    part = _half_rows(buf, 2 * px + py if landing else 2 * x + y, c)
    return pltpu.make_async_remote_copy(src_ref=part, dst_ref=part, send_sem=send, recv_sem=recv,
                                        device_id=(px, py, c), device_id_type=MESH_ID)


def _sibling_copy(buf, j, send, recv, landing):
    x, y, c = _mesh_pos()
    px, py = _other_chips(x, y)[j]
    part = _half_rows(buf, 2 * px + py, 1 - c if landing else c)
    return pltpu.make_async_remote_copy(src_ref=part, dst_ref=part, send_sem=send, recv_sem=recv,
                                        device_id=(x, y, 1 - c), device_id_type=MESH_ID)


def _forward_sibling(name, bufs, with_ici):
    n = len(bufs)

    def body(*refs):
        ins = refs[:n]
        send_ici, recv_ici, send_d2d, recv_d2d = refs[2 * n:]
        sends = []
        if with_ici:
            for i in range(n):
                for j in range(3):
                    cp = _ici_copy(ins[i], j, send_ici.at[i, j], recv_ici.at[i, j], False)
                    cp.start()
                    sends.append(cp)
        for i in range(n):
            for j in range(3):
                if with_ici:
                    _ici_copy(ins[i], j, send_ici.at[i, j], recv_ici.at[i, j], True).wait_recv()
                cp = _sibling_copy(ins[i], j, send_d2d.at[i, j], recv_d2d.at[i, j], False)
                cp.start()
                sends.append(cp)
        for i in range(n):
            for j in range(3):
                _sibling_copy(ins[i], j, send_d2d.at[i, j], recv_d2d.at[i, j], True).wait_recv()
        for cp in sends:
            cp.wait_send()

    return pl.pallas_call(
        body, name=name, in_specs=[HBM_SPEC] * n, out_specs=[HBM_SPEC] * n,
        out_shape=[_sds(b.shape, b.dtype) for b in bufs],
        scratch_shapes=[DMA_SEM((n, 3))] * 4, input_output_aliases={i: i for i in range(n)},
    )(*bufs)


def _gather_start(name, groups):
    flat = [b for g in groups for b in g]
    n, ng = len(flat), len(groups)

    def body(*refs):
        ins, sems, token = refs[:n], refs[n:n + 2 * ng], refs[-1]
        k = 0
        for gi, g in enumerate(groups):
            for a in range(len(g)):
                for j in range(3):
                    _ici_copy(ins[k], j, sems[2 * gi], sems[2 * gi + 1], False).start()
                k += 1
        token[...] = jnp.zeros_like(token)

    res = pl.pallas_call(
        body, name=name, in_specs=[HBM_SPEC] * n,
        out_specs=[SEM_SPEC] * (2 * ng) + [HBM_SPEC] * n + [pl.BlockSpec(memory_space=pltpu.VMEM)],
        out_shape=[DMA_SEM(()) for g in groups for _ in range(2)]
        + [pltpu.HBM(b.shape, b.dtype) for b in flat] + [_sds((8, LANES), F32)],
        input_output_aliases={i: 2 * ng + i for i in range(n)},
        compiler_params=pltpu.CompilerParams(has_side_effects=DATAFLOW),
    )(*[_hbm(b) for b in flat])
    sems = [(res[2 * gi], res[2 * gi + 1]) for gi in range(ng)]
    thru, k = [], 2 * ng
    for g in groups:
        thru.append(list(res[k:k + len(g)]))
        k += len(g)
    return sems, thru, res[-1]


def _gather_wait(name, bufs, sems, after):
    n = len(bufs)

    def body(*refs):
        ins, send, recv = refs[:n], refs[n], refs[n + 1]
        for a in range(n):
            for j in range(3):
                _ici_copy(ins[a], j, send, recv, False).wait_send()
                _ici_copy(ins[a], j, send, recv, True).wait_recv()

    return pl.pallas_call(
        body, name=name, in_specs=[HBM_SPEC] * n + [SEM_SPEC, SEM_SPEC, pl.BlockSpec(memory_space=pl.ANY)],
        out_specs=[HBM_SPEC] * n, out_shape=[pltpu.HBM(b.shape, b.dtype) for b in bufs],
        input_output_aliases={i: i for i in range(n)},
        compiler_params=pltpu.CompilerParams(has_side_effects=DATAFLOW),
    )(*bufs, sems[0], sems[1], after)


def _send_sibling_halves(name, arrs):
    n = len(arrs)

    def body(*refs):
        ins, outs = refs[:n], refs[n:2 * n]
        send, recv = refs[2 * n:]
        x, y, c = _mesh_pos()
        cps = []
        for i in range(n):
            cp = pltpu.make_async_remote_copy(
                src_ref=ins[i].at[:, 1 - c], dst_ref=outs[i],
                send_sem=send.at[i], recv_sem=recv.at[i], device_id=(x, y, 1 - c), device_id_type=MESH_ID)
            cp.start()
            cps.append(cp)
        for cp in cps:
            cp.wait()

    return pl.pallas_call(
        body, name=name, in_specs=[HBM_SPEC] * n, out_specs=[HBM_SPEC] * n,
        out_shape=[_sds((a.shape[0],) + a.shape[2:], a.dtype) for a in arrs],
        scratch_shapes=[DMA_SEM((n,)), DMA_SEM((n,))],
    )(*arrs)


def _sibling_copy_half(arr, land, send, recv):
    x, y, c = _mesh_pos()
    return pltpu.make_async_remote_copy(src_ref=arr.at[:, 1 - c], dst_ref=land, send_sem=send, recv_sem=recv,
                                        device_id=(x, y, 1 - c), device_id_type=MESH_ID)


def _sibling_start(name, arrs):
    n = len(arrs)
    lands = [lax.empty((a.shape[0],) + a.shape[2:], a.dtype) for a in arrs]

    def body(*refs):
        ins, lnd, send, recv, token = refs[:n], refs[n:2 * n], refs[2 * n], refs[2 * n + 1], refs[-1]
        for i in range(n):
            _sibling_copy_half(ins[i], lnd[i], send, recv).start()
        token[...] = jnp.zeros_like(token)

    res = pl.pallas_call(
        body, name=name, in_specs=[HBM_SPEC] * (2 * n),
        out_specs=[SEM_SPEC, SEM_SPEC] + [HBM_SPEC] * (2 * n) + [pl.BlockSpec(memory_space=pltpu.VMEM)],
        out_shape=[DMA_SEM(()), DMA_SEM(())] + [pltpu.HBM(a.shape, a.dtype) for a in arrs + lands]
        + [_sds((8, LANES), F32)],
        input_output_aliases={i: 2 + i for i in range(2 * n)},
        compiler_params=pltpu.CompilerParams(has_side_effects=DATAFLOW),
    )(*[_hbm(a) for a in arrs + lands])
    return (res[0], res[1]), list(res[2:2 + n]), list(res[2 + n:2 + 2 * n]), res[-1]


def _sibling_wait(name, arrs, lands, sems, after):
    n = len(arrs)

    def body(*refs):
        ins, lnd, send, recv = refs[:n], refs[n:2 * n], refs[2 * n], refs[2 * n + 1]
        for i in range(n):
            cp = _sibling_copy_half(ins[i], lnd[i], send, recv)
            cp.wait_send()
            cp.wait_recv()

    res = pl.pallas_call(
        body, name=name, in_specs=[HBM_SPEC] * (2 * n) + [SEM_SPEC, SEM_SPEC, pl.BlockSpec(memory_space=pl.ANY)],
        out_specs=[HBM_SPEC] * (2 * n), out_shape=[pltpu.HBM(a.shape, a.dtype) for a in arrs + lands],
        input_output_aliases={i: i for i in range(2 * n)},
        compiler_params=pltpu.CompilerParams(has_side_effects=DATAFLOW),
    )(*arrs, *lands, sems[0], sems[1], after)
    return list(res[:n]), list(res[n:])


def _chip_copy(p, land, j, send, recv):
    x, y, c = _mesh_pos()
    px, py = _other_chips(x, y)[j]
    return pltpu.make_async_remote_copy(src_ref=p.at[2 * px + py], dst_ref=land.at[j], send_sem=send, recv_sem=recv,
                                        device_id=(px, py, c), device_id_type=MESH_ID)


def _chip_send_start(name, ps):
    n = len(ps)
    lands = [lax.empty((3,) + p.shape[1:], p.dtype) for p in ps]

    def body(*refs):
        ins, lnd, send, recv, token = refs[:n], refs[n:2 * n], refs[2 * n], refs[2 * n + 1], refs[-1]
        for i in range(n):
            for j in range(3):
                _chip_copy(ins[i], lnd[i], j, send, recv).start()
        token[...] = jnp.zeros_like(token)

    res = pl.pallas_call(
        body, name=name, in_specs=[HBM_SPEC] * (2 * n),
        out_specs=[SEM_SPEC, SEM_SPEC] + [HBM_SPEC] * (2 * n) + [pl.BlockSpec(memory_space=pltpu.VMEM)],
        out_shape=[DMA_SEM(()), DMA_SEM(())] + [pltpu.HBM(a.shape, a.dtype) for a in ps + lands]
        + [_sds((8, LANES), F32)],
        input_output_aliases={i: 2 + i for i in range(2 * n)},
        compiler_params=pltpu.CompilerParams(has_side_effects=DATAFLOW),
    )(*[_hbm(a) for a in ps + lands])
    return (res[0], res[1]), list(res[2:2 + n]), list(res[2 + n:2 + 2 * n]), res[-1]


def _chip_send_wait(name, ps, lands, sems, after):
    n = len(ps)

    def body(*refs):
        ins, lnd, send, recv = refs[:n], refs[n:2 * n], refs[2 * n], refs[2 * n + 1]
        for i in range(n):
            for j in range(3):
                cp = _chip_copy(ins[i], lnd[i], j, send, recv)
                cp.wait_send()
                cp.wait_recv()

    res = pl.pallas_call(
        body, name=name, in_specs=[HBM_SPEC] * (2 * n) + [SEM_SPEC, SEM_SPEC, pl.BlockSpec(memory_space=pl.ANY)],
        out_specs=[HBM_SPEC] * (2 * n), out_shape=[pltpu.HBM(a.shape, a.dtype) for a in ps + lands],
        input_output_aliases={i: i for i in range(2 * n)},
        compiler_params=pltpu.CompilerParams(has_side_effects=DATAFLOW),
    )(*ps, *lands, sems[0], sems[1], after)
    return list(res[:n]), list(res[n:])


def _join_halves(arrs):
    n = len(arrs)

    def body(*refs):
        bufs = refs[n:2 * n]
        send, recv = refs[2 * n:]
        x, y, c = _mesh_pos()
        cps = []
        for i in range(n):
            mine = bufs[i].at[:, c]
            cp = pltpu.make_async_remote_copy(
                src_ref=mine, dst_ref=mine, send_sem=send.at[i], recv_sem=recv.at[i],
                device_id=(x, y, 1 - c), device_id_type=MESH_ID)
            cp.start()
            cps.append(cp)
        for i, cp in enumerate(cps):
            theirs = bufs[i].at[:, 1 - c]
            cp.wait_send()
            pltpu.make_async_remote_copy(
                src_ref=theirs, dst_ref=theirs, send_sem=send.at[i], recv_sem=recv.at[i],
                device_id=(x, y, 1 - c), device_id_type=MESH_ID).wait_recv()

    return pl.pallas_call(
        body, name="join_halves", in_specs=[HBM_SPEC] * n, out_specs=[HBM_SPEC] * n,
        out_shape=[_sds(a.shape, a.dtype) for a in arrs],
        scratch_shapes=[DMA_SEM((n,)), DMA_SEM((n,))], input_output_aliases={i: i for i in range(n)},
    )(*arrs)


def _peer_copy(buf, land, k, send, recv, landing):
    x, y, c = _mesh_pos()
    px, py, pc = x ^ ((k >> 2) & 1), y ^ ((k >> 1) & 1), c ^ (k & 1)
    slot = 4 * px + 2 * py + pc if landing else 4 * x + 2 * y + c
    return pltpu.make_async_remote_copy(src_ref=buf, dst_ref=land.at[slot], send_sem=send, recv_sem=recv,
                                        device_id=(px, py, pc), device_id_type=MESH_ID)


def _exchange_all(buf, dep=None):
    def body(in_ref, out_ref, send, recv):
        cps = [_peer_copy(in_ref, out_ref, k, send.at[k - 1], recv.at[k - 1], False) for k in range(1, 8)]
        for cp in cps:
            cp.start()
        for k in range(1, 8):
            cps[k - 1].wait_send()
            _peer_copy(in_ref, out_ref, k, send.at[k - 1], recv.at[k - 1], True).wait_recv()

    body, dep_spec, dep_arg = _after(body, 1, dep)
    return pl.pallas_call(
        body, name="exchange_all", in_specs=[HBM_SPEC] + dep_spec, out_specs=HBM_SPEC,
        out_shape=_sds((8,) + buf.shape, buf.dtype), scratch_shapes=[DMA_SEM((7,)), DMA_SEM((7,))],
    )(buf, *dep_arg)


def _join_copy(buf, landing):
    x, y, c = _mesh_pos()
    return buf.at[:, 1 - c if landing else c], (x, y, 1 - c)


def _join_start(bufs):
    n = len(bufs)

    def body(*refs):
        ins, send, recv, token = refs[:n], refs[n], refs[n + 1], refs[-1]
        for i in range(n):
            part, sib = _join_copy(ins[i], False)
            pltpu.make_async_remote_copy(src_ref=part, dst_ref=part, send_sem=send, recv_sem=recv, device_id=sib,
                                         device_id_type=MESH_ID).start()
        token[...] = jnp.zeros_like(token)

    res = pl.pallas_call(
        body, name="join_start", in_specs=[HBM_SPEC] * n,
        out_specs=[SEM_SPEC, SEM_SPEC] + [HBM_SPEC] * n + [pl.BlockSpec(memory_space=pltpu.VMEM)],
        out_shape=[DMA_SEM(()), DMA_SEM(())] + [pltpu.HBM(b.shape, b.dtype) for b in bufs] + [_sds((8, LANES), F32)],
        input_output_aliases={i: 2 + i for i in range(n)},
        compiler_params=pltpu.CompilerParams(has_side_effects=DATAFLOW),
    )(*[_hbm(b) for b in bufs])
    return (res[0], res[1]), list(res[2:2 + n]), res[-1]


def _join_wait(bufs, sems, after):
    n = len(bufs)

    def body(*refs):
        ins, send, recv = refs[:n], refs[n], refs[n + 1]
        for i in range(n):
            part, sib = _join_copy(ins[i], False)
            pltpu.make_async_remote_copy(src_ref=part, dst_ref=part, send_sem=send, recv_sem=recv, device_id=sib,
                                         device_id_type=MESH_ID).wait_send()
            part, sib = _join_copy(ins[i], True)
            pltpu.make_async_remote_copy(src_ref=part, dst_ref=part, send_sem=send, recv_sem=recv, device_id=sib,
                                         device_id_type=MESH_ID).wait_recv()

    return list(pl.pallas_call(
        body, name="join_wait", in_specs=[HBM_SPEC] * n + [SEM_SPEC, SEM_SPEC, pl.BlockSpec(memory_space=pl.ANY)],
        out_specs=[HBM_SPEC] * n, out_shape=[pltpu.HBM(b.shape, b.dtype) for b in bufs],
        input_output_aliases={i: i for i in range(n)},
        compiler_params=pltpu.CompilerParams(has_side_effects=DATAFLOW),
    )(*bufs, sems[0], sems[1], after))


def _exchange_start(name, buf):
    land = lax.empty((8,) + buf.shape, buf.dtype)

    def body(in_ref, land_ref, send, recv, in_thru, land_thru, token):
        for k in range(1, 8):
            _peer_copy(in_ref, land_ref, k, send, recv, False).start()
        token[...] = jnp.zeros_like(token)

    res = pl.pallas_call(
        body, name=name, in_specs=[HBM_SPEC] * 2,
        out_specs=[SEM_SPEC, SEM_SPEC, HBM_SPEC, HBM_SPEC, pl.BlockSpec(memory_space=pltpu.VMEM)],
        out_shape=[DMA_SEM(()), DMA_SEM(()), pltpu.HBM(buf.shape, buf.dtype), pltpu.HBM(land.shape, land.dtype),
                   _sds((8, LANES), F32)],
        input_output_aliases={0: 2, 1: 3}, compiler_params=pltpu.CompilerParams(has_side_effects=DATAFLOW),
    )(_hbm(buf), _hbm(land))
    return (res[0], res[1]), res[2], res[3], res[4]


def _exchange_wait(name, buf, land, sems, after):
    def body(in_ref, land_ref, send, recv, after_ref, in_thru, land_thru):
        for k in range(1, 8):
            _peer_copy(in_ref, land_ref, k, send, recv, False).wait_send()
            _peer_copy(in_ref, land_ref, k, send, recv, True).wait_recv()

    res = pl.pallas_call(
        body, name=name, in_specs=[HBM_SPEC, HBM_SPEC, SEM_SPEC, SEM_SPEC, pl.BlockSpec(memory_space=pl.ANY)],
        out_specs=[HBM_SPEC, HBM_SPEC], out_shape=[pltpu.HBM(buf.shape, buf.dtype), pltpu.HBM(land.shape, land.dtype)],
        input_output_aliases={0: 0, 1: 1}, compiler_params=pltpu.CompilerParams(has_side_effects=DATAFLOW),
    )(buf, land, sems[0], sems[1], after)
    return res[0], res[1]


def _pad_to(a, axis, size):
    pad = [(0, 0)] * a.ndim
    pad[axis] = (0, size - a.shape[axis])
    return jnp.pad(a, pad)


def _strips(w):
    k, d = w.shape
    return _pad_to(w, 0, 32).reshape(32, d // LANES, LANES).transpose(1, 0, 2)


def kernel(x, norm_mix, w_in, gate_bias, conv_w, conv_b, conv_ln_g, conv_ln_b, w_conv_out, sgu_ln_g, sgu_ln_b, w_spatial, b_spatial, w_sgu_out, w_o, norm_ffn, w_ffn_gate, w_ffn_up, w_ffn_down, norm_final, loss_target, m_norm_mix, m_w_in, m_gate_bias, m_conv_w, m_conv_b, m_conv_ln_g, m_conv_ln_b, m_w_conv_out, m_sgu_ln_g, m_sgu_ln_b, m_w_spatial, m_b_spatial, m_w_sgu_out, m_w_o, m_norm_ffn, m_w_ffn_gate, m_w_ffn_up, m_w_ffn_down, m_norm_final, v_norm_mix, v_w_in, v_gate_bias, v_conv_w, v_conv_b, v_conv_ln_g, v_conv_ln_b, v_w_conv_out, v_sgu_ln_g, v_sgu_ln_b, v_w_spatial, v_b_spatial, v_w_sgu_out, v_w_o, v_norm_ffn, v_w_ffn_gate, v_w_ffn_up, v_w_ffn_down, v_norm_final):
    BL, S, D = x.shape
    T = BL * S
    L = w_in.shape[0]
    CS = w_in.shape[2]
    CN = CS // 3
    DQ = D // NSHARD
    FS = w_ffn_gate.shape[2]
    F = NSHARD * FS
    G, CH = w_spatial.shape[1], w_spatial.shape[2]
    KW = conv_w.shape[1]
    CQ = conv_w.shape[3]
    NSTR = D // LANES
    tm = min(512, S // 2)
    tm2 = max(tm // 2, CH)
    rb = min(64, tm)
    mx, my, mc = _mesh_pos()
    pos = jnp.stack([mc, 2 * mx + my]).astype(jnp.int32)

    def placed(name, w, dtype=BF16, dep=None, layer=None, **kw):
        return _place_shard("place_" + name, w, pos, dtype, _row_tile(w.shape[-2], 256), dep, layer, **kw)

    w_in0 = placed("w_in", w_in, layer=0)
    cw_p = placed("conv_w", _pad_to(conv_w.reshape(L, KW, CQ), 1, 32).reshape(L * 32, CQ), F32)
    fsems, fflying, ftoken = _gather_start("gather_start_first", [[w_in0, cw_p]])
    wts = []
    for l in range(L):
        w_sq = None
        for i, w in enumerate([w_conv_out, w_sgu_out, w_o]):
            w_sq = placed("w_sq", w, dep=ftoken, layer=l, into=w_sq, row_off=i * DQ, out_rows=3 * DQ)
        wts.append(dict(w_in=placed("w_in", w_in, dep=ftoken, layer=l) if l else None, w_sq=w_sq,
                        wg=placed("w_gate", w_ffn_gate[l].T, dep=ftoken), wu=placed("w_up", w_ffn_up[l].T, dep=ftoken),
                        wd=placed("w_down", w_ffn_down, dep=ftoken, layer=l)))
    ffn_keys = ["wg", "wu", "wd"]
    order = [[(0, "w_sq")], [(0, k) for k in ffn_keys]]
    order += [[(l, k) for k in ["w_in", "w_sq"] + ffn_keys] for l in range(1, L)]
    gsems, flying, token = _gather_start("gather_start", [[wts[l][k] for l, k in grp] for grp in order])
    first = _gather_wait("gather_wait_first", fflying[0], fsems[0], token)
    wts[0]["w_in"], cw_g = _forward_sibling("gather_first", first, False)
    conv_w_full = cw_g.reshape(NSHARD, L, 32, CQ).transpose(1, 2, 0, 3).reshape(L, 32, D)[:, :KW]

    def land(gi, after):
        bufs = _gather_wait("gather_wait_%d" % gi, flying[gi], gsems[gi], after)
        bufs = _forward_sibling("gather_forward_%d" % gi, bufs, False)
        for (l, k), b in zip(order[gi], bufs):
            wts[l][k] = b

    x2d = x.reshape(T, D)
    tgt = loss_target.reshape(T, D)
    row = lambda a, l: a[l].reshape(1, -1)

    saved = []
    xc = x2d
    for l in range(L):
        ws_b = w_spatial[l].astype(BF16)
        bs_b = jnp.repeat(b_spatial[l].T, D // G, axis=1)
        cw_s = _strips(conv_w_full[l])
        h, proj = _mix_in_fwd(xc, row(norm_mix, l), wts[l]["w_in"], 0, tm, token if l == 0 else None)
        if l == 0:
            land(0, h)
        c1h, rstd_c, c3, ya = _conv_fwd(proj, cw_s, row(conv_b, l), row(conv_ln_g, l), row(conv_ln_b, l),
                                        wts[l]["w_sq"], 0, S, tm, rb)
        if l == 0:
            land(1, ya)
        ffn_w = [wts[l][k].reshape(F, D) for k in ffn_keys]
        head = (norm_final.reshape(1, D), tgt) if l + 1 == L else None
        mixed, gated, yb, merged, x1, h2, gt, up, act, x2, *tail = _sgu_ffn_fwd(
            proj, ya, xc, row(sgu_ln_g, l), row(sgu_ln_b, l), ws_b, bs_b, row(gate_bias, l), wts[l]["w_sq"], 0,
            row(norm_ffn, l), *ffn_w, tm2, head)
        if l + 1 < L:
            land(l + 2, x2)
        saved.append(dict(x=xc, h=h, proj=proj, c1h=c1h, rstd_c=rstd_c, c3=c3, ya=ya, mixed=mixed, gated=gated,
                          yb=yb, merged=merged, x1=x1, h2=h2, gt=gt, up=up, act=act, ws_b=ws_b, cw=conv_w_full[l]))
        xc = x2

    dx, (loss_part, d_norm_final) = xc, tail
    loss = lax.psum(loss_part[0, 0], ("x", "y", "c"))

    g_acc = {}

    def reduce_start(tag, layer, named):
        arrs = [g.reshape(NSHARD, 2, g.shape[1] // 2, g.shape[2]) for _, g in named]
        from_sib = _send_sibling_halves("sibling_" + tag, arrs)
        ps = [_add_halves("presum_" + nm, a, r, pos, _row_tile(a.shape[2], 256))
              for (nm, _), a, r in zip(named, arrs, from_sib)]
        sems, ps, lands, tok = _chip_send_start("chip_send_start_" + tag, ps)
        return dict(tag=tag, layer=layer, names=[nm for nm, _ in named], ps=ps, lands=lands, sems=sems), tok

    def reduce_begin(tag, layer, named):
        arrs = [g.reshape(NSHARD, 2, g.shape[1] // 2, g.shape[2]) for _, g in named]
        sems, arrs, lands, tok = _sibling_start("sibling_start_" + tag, arrs)
        return dict(tag=tag, layer=layer, names=[nm for nm, _ in named], arrs=arrs, lands=lands, sems=sems), tok

    def reduce_mid(st, after):
        arrs, from_sib = _sibling_wait("sibling_wait_" + st["tag"], st["arrs"], st["lands"], st["sems"], after)
        ps = [_add_halves("presum_" + nm, a, r, pos, _row_tile(a.shape[2], 256))
              for nm, a, r in zip(st["names"], arrs, from_sib)]
        sems, ps, lands, tok = _chip_send_start("chip_send_start_" + st["tag"], ps)
        return dict(tag=st["tag"], layer=st["layer"], names=st["names"], ps=ps, lands=lands, sems=sems), tok

    def reduce_finish(pend, after):
        ps, lands = _chip_send_wait("chip_send_wait_" + pend["tag"], pend["ps"], pend["lands"], pend["sems"], after)
        for nm, p, r in zip(pend["names"], ps, lands):
            g_acc[nm] = _add_shards("shardsum_" + nm, p, r, pos, _row_tile(p.shape[1], 256), pend["layer"], L,
                                    g_acc.get(nm))

    me_idx = (4 * mx + 2 * my + mc).astype(jnp.int32).reshape(1)
    exchanges = []

    def pack_rows(pieces):
        packed = jnp.concatenate(pieces, axis=0)
        return _pad_to(packed, 0, -(-packed.shape[0] // 8) * 8)

    def unpack_rows(summed, pieces):
        out, off = [], 0
        for p in pieces:
            out.append(summed[off:off + p.shape[0]])
            off += p.shape[0]
        return out

    def small_start(tag, pieces):
        sems, buf, land, token = _exchange_start("exchange_start_" + tag, pack_rows(pieces))
        return dict(tag=tag, pieces=pieces, buf=buf, land=land, sems=sems, token=token)

    def small_finish(st, after):
        buf, land = _exchange_wait("exchange_wait_" + st["tag"], st["buf"], st["land"], st["sems"], after)
        return unpack_rows(_sum_slots(buf, land, me_idx, _row_tile(buf.shape[0], 256)), st["pieces"])

    small = [None] * L
    tt = min(2048, T // 2)
    nt = T // tt
    pending, tok = None, None
    for l in reversed(range(L)):
        sv, wt = saved[l], wts[l]
        ffn_w = [wt[k].reshape(F, D) for k in ffn_keys]
        dx1, dgt, dup, d_norm_ffn = _ffn_bwd(dx, sv["x1"], sv["gt"], sv["up"], row(norm_ffn, l), *ffn_w, tm2, tok)
        tn_a = ((tt, F // 2), lambda j, t: (t, j))
        tn_b = ((tt, D), lambda j, t: (t, 0))
        tn_o = ((F, D), (F // 2, D), lambda j, t: (j, 0), 2, nt)
        g_g, = _tn_matmul("grad_w_gate", dgt, *tn_a, [sv["h2"]], *tn_b, *tn_o)
        g_u, = _tn_matmul("grad_w_up", dup, *tn_a, [sv["h2"]], *tn_b, *tn_o)
        g_d, = _tn_matmul("grad_w_down", sv["act"], *tn_a, [dx], *tn_b, *tn_o)
        if pending is not None:
            reduce_finish(pending, g_d)
        ffn_st, tok = reduce_begin("ffn%d" % l, l, [
            ("w_ffn_gate", g_g.reshape(NSHARD, FS, D)), ("w_ffn_up", g_u.reshape(NSHARD, FS, D)),
            ("w_ffn_down", g_d.reshape(NSHARD, FS, D))])
        wst_b = jnp.swapaxes(sv["ws_b"], 1, 2)
        dya, dyb, dp3, dc1, d_gate_bias, d_sgu_g, d_sgu_b, d_bs, d_ws, d_cln_g, d_cln_b, d_conv_b = _merge_sgu_bwd(
            dx1, sv["proj"], sv["ya"], sv["yb"], sv["mixed"], sv["c1h"], sv["rstd_c"], row(sgu_ln_g, l),
            row(sgu_ln_b, l), sv["ws_b"], wst_b, row(gate_bias, l), row(conv_ln_g, l), row(conv_ln_b, l), wt["w_sq"],
            0, tm2, tok)
        ffn_pend, tok_c = reduce_mid(ffn_st, dya)
        sq_args = ((tt, D), lambda j, t: (t, 0))
        sq_out = ((D, D), (D, D), lambda j, t: (0, 0), 1, nt)
        g_o, = _tn_matmul("grad_w_o", sv["merged"], *sq_args, [dx1], *sq_args, *sq_out)
        g_so, = _tn_matmul("grad_w_sgu_out", sv["gated"], *sq_args, [dyb], *sq_args, *sq_out)
        g_co, = _tn_matmul("grad_w_conv_out", sv["c3"], *sq_args, [dya], *sq_args, *sq_out)
        small[l] = [None, d_gate_bias.reshape(2, D), None, d_conv_b, d_cln_g, d_cln_b, d_sgu_g, d_sgu_b,
                    d_ws.reshape(G * CH * CH // D, D), d_bs.reshape(G * CH // D, D), d_norm_ffn]
        tok_x = None
        if l == 0:
            early = [k for k in range(len(small[0])) if small[0][k] is not None]
            exchanges.append((small_start("early0", [small[0][k] for k in early]), [(0, k) for k in early]))
            tok_x = exchanges[-1][0]["token"]
        dp3, d_cw_s = _conv_bwd(dc1, sv["proj"], dp3, _strips(sv["cw"][::-1]), S, tm, rb, [tok_x, tok_c])
        g_in, = _tn_matmul("grad_w_in", sv["h"], (tt, D), lambda j, t: (t, 0), [dp3], (tt, CS), lambda j, t: (t, j),
                           (NSHARD, D, CS), (None, D, CS), lambda j, t: (j, 0, 0), NSHARD, nt)
        reduce_finish(ffn_pend, g_in)
        mix_named = [("w_in", g_in), ("w_conv_out", g_co.reshape(NSHARD, DQ, D)),
                     ("w_sgu_out", g_so.reshape(NSHARD, DQ, D)), ("w_o", g_o.reshape(NSHARD, DQ, D))]
        if l > 0:
            mix_st, tok = reduce_begin("mix%d" % l, l, mix_named)
            dx, d_norm_mix = _mix_in_bwd(dx1, dp3, sv["x"], row(norm_mix, l), wt["w_in"], 0, tm, tok)
            pending, tok = reduce_mid(mix_st, dx)
        else:
            pending, tok = reduce_start("mix%d" % l, l, mix_named)
            dx, d_norm_mix = _mix_in_bwd(dx1, dp3, sv["x"], row(norm_mix, l), wt["w_in"], 0, tm, tok)
        small[l][0] = d_norm_mix
        small[l][2] = d_cw_s.transpose(1, 0, 2).reshape(32, D)
        if l > 0:
            exchanges.append((small_start("layer%d" % l, small[l]), [(l, k) for k in range(len(small[l]))]))
            tok = [tok, exchanges[-1][0]["token"]]
    reduce_finish(pending, dx)
    grad_x = dx.reshape(BL, S, D)

    names = ["w_in", "w_conv_out", "w_sgu_out", "w_o", "w_ffn_gate", "w_ffn_up", "w_ffn_down"]
    jsems, jbufs, jtok = _join_start([g_acc[nm] for nm in names])

    late = [small[0][0], small[0][2], d_norm_final]
    packed = pack_rows(late)
    summed = _sum_slots(packed, _exchange_all(packed, jtok), me_idx, _row_tile(packed.shape[0], 256))
    sg = [[None] * len(small[l]) for l in range(L)]
    sg[0][0], sg[0][2], g_norm_final = unpack_rows(summed, late)
    g_norm_final = g_norm_final[0]
    last = summed
    for st, where in exchanges:
        for (l, k), piece in zip(where, small_finish(st, summed)):
            sg[l][k] = last = piece
    g_full = _join_wait(jbufs, jsems, last)
    g_w_in, g_w_co, g_w_so, g_w_o, g_w_g, g_w_u, g_w_d = [g.reshape(L, 2 * g.shape[2], g.shape[3]) for g in g_full]
    g_w_g = jnp.swapaxes(g_w_g, 1, 2)
    g_w_u = jnp.swapaxes(g_w_u, 1, 2)

    def per_layer(k, shape):
        return jnp.stack([sg[l][k] for l in range(L)]).reshape(shape)

    g_norm_mix = per_layer(0, (L, D))
    g_gate_bias = per_layer(1, (L, 2 * D))
    g_conv_w_full = jnp.stack([sg[l][2][:KW] for l in range(L)])
    g_conv_w = lax.dynamic_slice_in_dim(g_conv_w_full, (2 * mx + my) * CQ, CQ, axis=2).reshape(L, KW, 1, CQ)
    g_conv_b = per_layer(3, (L, D))
    g_conv_ln_g = per_layer(4, (L, D))
    g_conv_ln_b = per_layer(5, (L, D))
    g_sgu_ln_g = per_layer(6, (L, D))
    g_sgu_ln_b = per_layer(7, (L, D))
    g_w_spatial = per_layer(8, (L, G, CH, CH))
    g_b_spatial = per_layer(9, (L, G, CH))
    g_norm_ffn = per_layer(10, (L, D))

    grads = [g_norm_mix, g_w_in, g_gate_bias, g_conv_w, g_conv_b, g_conv_ln_g, g_conv_ln_b, g_w_co, g_sgu_ln_g,
             g_sgu_ln_b, g_w_spatial, g_b_spatial, g_w_so, g_w_o, g_norm_ffn, g_w_g, g_w_u, g_w_d, g_norm_final]
    weights = [norm_mix, w_in, gate_bias, conv_w, conv_b, conv_ln_g, conv_ln_b, w_conv_out, sgu_ln_g, sgu_ln_b,
               w_spatial, b_spatial, w_sgu_out, w_o, norm_ffn, w_ffn_gate, w_ffn_up, w_ffn_down, norm_final]
    ms = [m_norm_mix, m_w_in, m_gate_bias, m_conv_w, m_conv_b, m_conv_ln_g, m_conv_ln_b, m_w_conv_out, m_sgu_ln_g,
          m_sgu_ln_b, m_w_spatial, m_b_spatial, m_w_sgu_out, m_w_o, m_norm_ffn, m_w_ffn_gate, m_w_ffn_up,
          m_w_ffn_down, m_norm_final]
    vs = [v_norm_mix, v_w_in, v_gate_bias, v_conv_w, v_conv_b, v_conv_ln_g, v_conv_ln_b, v_w_conv_out, v_sgu_ln_g,
          v_sgu_ln_b, v_w_spatial, v_b_spatial, v_w_sgu_out, v_w_o, v_norm_ffn, v_w_ffn_gate, v_w_ffn_up,
          v_w_ffn_down, v_norm_final]

    big_idx = [1, 7, 12, 13, 15, 16, 17]
    transposed = [15, 16]
    deltas, new_m, new_v = [None] * 19, [None] * 19, [None] * 19
    for k in big_idx:
        shp = weights[k].shape
        r2 = (shp[0] * shp[1], shp[2])
        res = _adamw("adamw_" + str(k), weights[k].reshape(r2), grads[k].reshape(r2), ms[k].reshape(r2),
                     vs[k].reshape(r2), _row_tile(r2[0], 256), k not in transposed)
        deltas[k], new_m[k], new_v[k] = [a.reshape(shp) for a in res[:3]]
        if k not in transposed:
            grads[k] = res[3].reshape(shp)
    small_idx = [k for k in range(19) if k not in big_idx]
    pick = lambda arrs: [arrs[k].reshape(1, -1) if arrs[k].ndim == 1 else arrs[k] for k in small_idx]
    d_, m_, v_ = _adamw_many(pick(weights), pick(grads), pick(ms), pick(vs))
    for i, k in enumerate(small_idx):
        shp = weights[k].shape
        deltas[k], new_m[k], new_v[k] = d_[i].reshape(shp), m_[i].reshape(shp), v_[i].reshape(shp)

    return (loss, grad_x, *grads, *deltas, *new_m, *new_v)
```
